```python
import math
import jax, jax.numpy as jnp
from jax import lax
import numpy as np


D_MODEL = 1024
BATCH = 8
SEQ = 4096
DEPTH = 4

N_A = DEPTH // 2
N_B = DEPTH - N_A
N_HEADS = 16
HEAD_DIM = D_MODEL // N_HEADS
CONV_W = 31
FFN_DIM = 2816
FFN_CONV_W = 3
PLE_DIM = 256
Q_BLOCK = 128
LN_EPS = 1e-5
DN_ALPHA = (2.0 * DEPTH) ** 0.25
DN_BETA = (8.0 * DEPTH) ** -0.25

kernel_name = "yoco_conformer_stickbreaking_hybrid"


def layer_norm(x, g, b):
    xf = x.astype(jnp.float32)
    mu = jnp.mean(xf, axis=-1, keepdims=True)
    var = jnp.mean(jnp.square(xf - mu), axis=-1, keepdims=True)
    y = (xf - mu) * lax.rsqrt(var + LN_EPS)
    return (y * g.astype(jnp.float32) + b.astype(jnp.float32)).astype(x.dtype)


def causal_dwconv(x, w, b):
    k = w.shape[0]
    y = lax.conv_general_dilated(
        x, w[:, None, :].astype(x.dtype), window_strides=(1,), padding=[(k - 1, 0)],
        dimension_numbers=("NWC", "WIO", "NWC"), feature_group_count=x.shape[-1])
    return y + b


def conformer_conv(x, pw1_w, pw1_b, dw_w, dw_b, ln_g, ln_b, pw2_w, pw2_b):
    h = x @ pw1_w + pw1_b
    a, g = jnp.split(h, 2, axis=-1)
    h = a * jax.nn.sigmoid(g)
    h = causal_dwconv(h, dw_w, dw_b)
    h = layer_norm(h, ln_g, ln_b)
    h = jax.nn.silu(h)
    return h @ pw2_w + pw2_b


def stick_breaking_attention(q, k, v):
    b, s, h, dh = q.shape
    nb = s // Q_BLOCK
    scale = 1.0 / math.sqrt(dh)
    kh = jnp.transpose(k, (0, 2, 1, 3))
    vh = jnp.transpose(v, (0, 2, 1, 3))
    qb = jnp.transpose(q.reshape(b, nb, Q_BLOCK, h, dh), (1, 0, 3, 2, 4))
    t0s = jnp.arange(nb, dtype=jnp.int32) * Q_BLOCK
    ts = jnp.arange(s, dtype=jnp.int32)

    def block(args):
        qblk, t0 = args
        z = jnp.einsum('bhqd,bhkd->bhqk', qblk, kh).astype(jnp.float32) * scale
        tq = t0 + jnp.arange(Q_BLOCK, dtype=jnp.int32)
        mask = ts[None, :] < tq[:, None]
        log_1m = jnp.where(mask, jax.nn.log_sigmoid(-z), 0.0)
        rev = lax.cumsum(log_1m, axis=3, reverse=True)
        log_a = jax.nn.log_sigmoid(z) + (rev - log_1m)
        a = jnp.where(mask, jnp.exp(log_a), 0.0)
        return jnp.einsum('bhqk,bhkd->bhqd', a.astype(vh.dtype), vh)

    out = lax.map(block, (qb, t0s))
    return jnp.transpose(out, (1, 0, 3, 2, 4)).reshape(b, s, h * dh)


def conv_gated_ffn(x, w_up, w_gate, conv_w, conv_b, w_down):
    u = x @ w_up
    g = causal_dwconv(x @ w_gate, conv_w, conv_b)
    return (jax.nn.silu(g) * u) @ w_down


def _fwd_setup_inputs(seed: int = 0) -> dict:
    key = jax.random.key(seed)
    ks = jax.random.split(key, 32)
    D, F = D_MODEL, FFN_DIM

    def nrm(k, shape, scale):
        return jax.random.normal(k, shape, jnp.float32) * scale

    return {
        "x": nrm(ks[0], (BATCH, SEQ, D), 1.0),
        "p": nrm(ks[1], (DEPTH, BATCH, SEQ, PLE_DIM), 1.0),
        "a_pw1_w": nrm(ks[2], (N_A, D, 2 * D), D ** -0.5),
        "a_pw1_b": nrm(ks[3], (N_A, 2 * D), 0.02),
        "a_dw_w": nrm(ks[4], (N_A, CONV_W, D), CONV_W ** -0.5),
        "a_dw_b": nrm(ks[5], (N_A, D), 0.02),
        "a_ln_g": 1.0 + nrm(ks[6], (N_A, D), 0.02),
        "a_ln_b": nrm(ks[7], (N_A, D), 0.02),
        "a_pw2_w": nrm(ks[8], (N_A, D, D), D ** -0.5 * DN_BETA),
        "a_pw2_b": nrm(ks[9], (N_A, D), 0.02),
        "b_wq": nrm(ks[10], (N_B, D, D), D ** -0.5),
        "kv_wk": nrm(ks[11], (D, D), D ** -0.5),
        "kv_wv": nrm(ks[12], (D, D), D ** -0.5 * DN_BETA),
        "b_wo": nrm(ks[13], (N_B, D, D), D ** -0.5 * DN_BETA),
        "ln_mix_g": 1.0 + nrm(ks[14], (DEPTH, D), 0.02),
        "ln_mix_b": nrm(ks[15], (DEPTH, D), 0.02),
        "ffn_w_up": nrm(ks[16], (DEPTH, D, F), D ** -0.5),
        "ffn_w_gate": nrm(ks[17], (DEPTH, D, F), D ** -0.5),
        "ffn_conv_w": nrm(ks[18], (DEPTH, FFN_CONV_W, F), FFN_CONV_W ** -0.5),
        "ffn_conv_b": nrm(ks[19], (DEPTH, F), 0.02),
        "ffn_w_down": nrm(ks[20], (DEPTH, F, D), F ** -0.5 * DN_BETA),
        "ple_w_gate": nrm(ks[21], (DEPTH, D, D), D ** -0.5),
        "ple_w_proj": nrm(ks[22], (DEPTH, PLE_DIM, D), PLE_DIM ** -0.5 * DN_BETA),
        "ln_ffn_g": 1.0 + nrm(ks[23], (DEPTH, D), 0.02),
        "ln_ffn_b": nrm(ks[24], (DEPTH, D), 0.02),
    }


def _fwd_reference(x, p, a_pw1_w, a_pw1_b, a_dw_w, a_dw_b, a_ln_g, a_ln_b, a_pw2_w, a_pw2_b,
              b_wq, kv_wk, kv_wv, b_wo, ln_mix_g, ln_mix_b,
              ffn_w_up, ffn_w_gate, ffn_conv_w, ffn_conv_b, ffn_w_down,
              ple_w_gate, ple_w_proj, ln_ffn_g, ln_ffn_b):
    b, s, d = x.shape
    k_shared = None
    v_shared = None
    for i in range(DEPTH):
        if i < N_A:
            mix = conformer_conv(x, a_pw1_w[i], a_pw1_b[i], a_dw_w[i], a_dw_b[i],
                                 a_ln_g[i], a_ln_b[i], a_pw2_w[i], a_pw2_b[i])
        else:
            j = i - N_A
            if k_shared is None:
                k_shared = (x @ kv_wk).reshape(b, s, N_HEADS, HEAD_DIM)
                v_shared = (x @ kv_wv).reshape(b, s, N_HEADS, HEAD_DIM)
            q = (x @ b_wq[j]).reshape(b, s, N_HEADS, HEAD_DIM)
            mix = stick_breaking_attention(q, k_shared, v_shared) @ b_wo[j]
        x = layer_norm(DN_ALPHA * x + mix, ln_mix_g[i], ln_mix_b[i])
        ffn = conv_gated_ffn(x, ffn_w_up[i], ffn_w_gate[i], ffn_conv_w[i], ffn_conv_b[i], ffn_w_down[i])
        ple = jax.nn.sigmoid(x @ ple_w_gate[i]) * (p[i] @ ple_w_proj[i])
        x = layer_norm(DN_ALPHA * x + ffn + ple, ln_ffn_g[i], ln_ffn_b[i])
    return x


import jax as _jax
import jax.numpy as _jnp

TWIN_FORMAT = 'train_step'
FWD_PARAMS = ['x', 'p', 'a_pw1_w', 'a_pw1_b', 'a_dw_w', 'a_dw_b', 'a_ln_g', 'a_ln_b', 'a_pw2_w', 'a_pw2_b', 'b_wq', 'kv_wk', 'kv_wv', 'b_wo', 'ln_mix_g', 'ln_mix_b', 'ffn_w_up', 'ffn_w_gate', 'ffn_conv_w', 'ffn_conv_b', 'ffn_w_down', 'ple_w_gate', 'ple_w_proj', 'ln_ffn_g', 'ln_ffn_b']
TWIN_WEIGHTS = ['a_pw1_w', 'a_pw1_b', 'a_dw_w', 'a_dw_b', 'a_ln_g', 'a_ln_b', 'a_pw2_w', 'a_pw2_b', 'b_wq', 'kv_wk', 'kv_wv', 'b_wo', 'ln_mix_g', 'ln_mix_b', 'ffn_w_up', 'ffn_w_gate', 'ffn_conv_w', 'ffn_conv_b', 'ffn_w_down', 'ple_w_gate', 'ple_w_proj', 'ln_ffn_g', 'ln_ffn_b']
TWIN_DIFF_INPUT = 'x'
TWIN_INPUTS = ['x', 'p', 'a_pw1_w', 'a_pw1_b', 'a_dw_w', 'a_dw_b', 'a_ln_g', 'a_ln_b', 'a_pw2_w', 'a_pw2_b', 'b_wq', 'kv_wk', 'kv_wv', 'b_wo', 'ln_mix_g', 'ln_mix_b', 'ffn_w_up', 'ffn_w_gate', 'ffn_conv_w', 'ffn_conv_b', 'ffn_w_down', 'ple_w_gate', 'ple_w_proj', 'ln_ffn_g', 'ln_ffn_b', 'loss_target', 'm_a_pw1_w', 'm_a_pw1_b', 'm_a_dw_w', 'm_a_dw_b', 'm_a_ln_g', 'm_a_ln_b', 'm_a_pw2_w', 'm_a_pw2_b', 'm_b_wq', 'm_kv_wk', 'm_kv_wv', 'm_b_wo', 'm_ln_mix_g', 'm_ln_mix_b', 'm_ffn_w_up', 'm_ffn_w_gate', 'm_ffn_conv_w', 'm_ffn_conv_b', 'm_ffn_w_down', 'm_ple_w_gate', 'm_ple_w_proj', 'm_ln_ffn_g', 'm_ln_ffn_b', 'v_a_pw1_w', 'v_a_pw1_b', 'v_a_dw_w', 'v_a_dw_b', 'v_a_ln_g', 'v_a_ln_b', 'v_a_pw2_w', 'v_a_pw2_b', 'v_b_wq', 'v_kv_wk', 'v_kv_wv', 'v_b_wo', 'v_ln_mix_g', 'v_ln_mix_b', 'v_ffn_w_up', 'v_ffn_w_gate', 'v_ffn_conv_w', 'v_ffn_conv_b', 'v_ffn_w_down', 'v_ple_w_gate', 'v_ple_w_proj', 'v_ln_ffn_g', 'v_ln_ffn_b']
TWIN_OUTPUTS = ['loss', 'grad_x', 'grad_a_pw1_w', 'grad_a_pw1_b', 'grad_a_dw_w', 'grad_a_dw_b', 'grad_a_ln_g', 'grad_a_ln_b', 'grad_a_pw2_w', 'grad_a_pw2_b', 'grad_b_wq', 'grad_kv_wk', 'grad_kv_wv', 'grad_b_wo', 'grad_ln_mix_g', 'grad_ln_mix_b', 'grad_ffn_w_up', 'grad_ffn_w_gate', 'grad_ffn_conv_w', 'grad_ffn_conv_b', 'grad_ffn_w_down', 'grad_ple_w_gate', 'grad_ple_w_proj', 'grad_ln_ffn_g', 'grad_ln_ffn_b', 'delta_a_pw1_w', 'delta_a_pw1_b', 'delta_a_dw_w', 'delta_a_dw_b', 'delta_a_ln_g', 'delta_a_ln_b', 'delta_a_pw2_w', 'delta_a_pw2_b', 'delta_b_wq', 'delta_kv_wk', 'delta_kv_wv', 'delta_b_wo', 'delta_ln_mix_g', 'delta_ln_mix_b', 'delta_ffn_w_up', 'delta_ffn_w_gate', 'delta_ffn_conv_w', 'delta_ffn_conv_b', 'delta_ffn_w_down', 'delta_ple_w_gate', 'delta_ple_w_proj', 'delta_ln_ffn_g', 'delta_ln_ffn_b', 'new_m_a_pw1_w', 'new_m_a_pw1_b', 'new_m_a_dw_w', 'new_m_a_dw_b', 'new_m_a_ln_g', 'new_m_a_ln_b', 'new_m_a_pw2_w', 'new_m_a_pw2_b', 'new_m_b_wq', 'new_m_kv_wk', 'new_m_kv_wv', 'new_m_b_wo', 'new_m_ln_mix_g', 'new_m_ln_mix_b', 'new_m_ffn_w_up', 'new_m_ffn_w_gate', 'new_m_ffn_conv_w', 'new_m_ffn_conv_b', 'new_m_ffn_w_down', 'new_m_ple_w_gate', 'new_m_ple_w_proj', 'new_m_ln_ffn_g', 'new_m_ln_ffn_b', 'new_v_a_pw1_w', 'new_v_a_pw1_b', 'new_v_a_dw_w', 'new_v_a_dw_b', 'new_v_a_ln_g', 'new_v_a_ln_b', 'new_v_a_pw2_w', 'new_v_a_pw2_b', 'new_v_b_wq', 'new_v_kv_wk', 'new_v_kv_wv', 'new_v_b_wo', 'new_v_ln_mix_g', 'new_v_ln_mix_b', 'new_v_ffn_w_up', 'new_v_ffn_w_gate', 'new_v_ffn_conv_w', 'new_v_ffn_conv_b', 'new_v_ffn_w_down', 'new_v_ple_w_gate', 'new_v_ple_w_proj', 'new_v_ln_ffn_g', 'new_v_ln_ffn_b']
TWIN_LEAF_KINDS = {'loss': 'loss', 'grad_x': 'grad_x', 'grad_a_pw1_w': 'grad_w', 'grad_a_pw1_b': 'grad_w', 'grad_a_dw_w': 'grad_w', 'grad_a_dw_b': 'grad_w', 'grad_a_ln_g': 'grad_w', 'grad_a_ln_b': 'grad_w', 'grad_a_pw2_w': 'grad_w', 'grad_a_pw2_b': 'grad_w', 'grad_b_wq': 'grad_w', 'grad_kv_wk': 'grad_w', 'grad_kv_wv': 'grad_w', 'grad_b_wo': 'grad_w', 'grad_ln_mix_g': 'grad_w', 'grad_ln_mix_b': 'grad_w', 'grad_ffn_w_up': 'grad_w', 'grad_ffn_w_gate': 'grad_w', 'grad_ffn_conv_w': 'grad_w', 'grad_ffn_conv_b': 'grad_w', 'grad_ffn_w_down': 'grad_w', 'grad_ple_w_gate': 'grad_w', 'grad_ple_w_proj': 'grad_w', 'grad_ln_ffn_g': 'grad_w', 'grad_ln_ffn_b': 'grad_w', 'delta_a_pw1_w': 'delta_w', 'delta_a_pw1_b': 'delta_w', 'delta_a_dw_w': 'delta_w', 'delta_a_dw_b': 'delta_w', 'delta_a_ln_g': 'delta_w', 'delta_a_ln_b': 'delta_w', 'delta_a_pw2_w': 'delta_w', 'delta_a_pw2_b': 'delta_w', 'delta_b_wq': 'delta_w', 'delta_kv_wk': 'delta_w', 'delta_kv_wv': 'delta_w', 'delta_b_wo': 'delta_w', 'delta_ln_mix_g': 'delta_w', 'delta_ln_mix_b': 'delta_w', 'delta_ffn_w_up': 'delta_w', 'delta_ffn_w_gate': 'delta_w', 'delta_ffn_conv_w': 'delta_w', 'delta_ffn_conv_b': 'delta_w', 'delta_ffn_w_down': 'delta_w', 'delta_ple_w_gate': 'delta_w', 'delta_ple_w_proj': 'delta_w', 'delta_ln_ffn_g': 'delta_w', 'delta_ln_ffn_b': 'delta_w', 'new_m_a_pw1_w': 'new_m', 'new_m_a_pw1_b': 'new_m', 'new_m_a_dw_w': 'new_m', 'new_m_a_dw_b': 'new_m', 'new_m_a_ln_g': 'new_m', 'new_m_a_ln_b': 'new_m', 'new_m_a_pw2_w': 'new_m', 'new_m_a_pw2_b': 'new_m', 'new_m_b_wq': 'new_m', 'new_m_kv_wk': 'new_m', 'new_m_kv_wv': 'new_m', 'new_m_b_wo': 'new_m', 'new_m_ln_mix_g': 'new_m', 'new_m_ln_mix_b': 'new_m', 'new_m_ffn_w_up': 'new_m', 'new_m_ffn_w_gate': 'new_m', 'new_m_ffn_conv_w': 'new_m', 'new_m_ffn_conv_b': 'new_m', 'new_m_ffn_w_down': 'new_m', 'new_m_ple_w_gate': 'new_m', 'new_m_ple_w_proj': 'new_m', 'new_m_ln_ffn_g': 'new_m', 'new_m_ln_ffn_b': 'new_m', 'new_v_a_pw1_w': 'new_v', 'new_v_a_pw1_b': 'new_v', 'new_v_a_dw_w': 'new_v', 'new_v_a_dw_b': 'new_v', 'new_v_a_ln_g': 'new_v', 'new_v_a_ln_b': 'new_v', 'new_v_a_pw2_w': 'new_v', 'new_v_a_pw2_b': 'new_v', 'new_v_b_wq': 'new_v', 'new_v_kv_wk': 'new_v', 'new_v_kv_wv': 'new_v', 'new_v_b_wo': 'new_v', 'new_v_ln_mix_g': 'new_v', 'new_v_ln_mix_b': 'new_v', 'new_v_ffn_w_up': 'new_v', 'new_v_ffn_w_gate': 'new_v', 'new_v_ffn_conv_w': 'new_v', 'new_v_ffn_conv_b': 'new_v', 'new_v_ffn_w_down': 'new_v', 'new_v_ple_w_gate': 'new_v', 'new_v_ple_w_proj': 'new_v', 'new_v_ln_ffn_g': 'new_v', 'new_v_ln_ffn_b': 'new_v'}


def _forward(args):
    return _fwd_reference(*[args[k] for k in FWD_PARAMS])


def _output_shape():
    out = _jax.eval_shape(lambda: _forward(_fwd_setup_inputs(0)))
    return out.shape, out.dtype

N_MICROBATCH = 1
ADAM_LR = 0.001
ADAM_B1 = 0.9
ADAM_B2 = 0.999
ADAM_EPS = 1e-08
ADAM_WD = 0.01
ADAM_STEP = 10
PER_EXAMPLE_BATCH_AXIS = {'x': 0, 'p': 1, 'loss_target': 0}
SHARED_INPUTS = []
_WEIGHT_DTYPES = {'a_pw1_w': _jnp.float32, 'a_pw1_b': _jnp.float32, 'a_dw_w': _jnp.float32, 'a_dw_b': _jnp.float32, 'a_ln_g': _jnp.float32, 'a_ln_b': _jnp.float32, 'a_pw2_w': _jnp.float32, 'a_pw2_b': _jnp.float32, 'b_wq': _jnp.float32, 'kv_wk': _jnp.float32, 'kv_wv': _jnp.float32, 'b_wo': _jnp.float32, 'ln_mix_g': _jnp.float32, 'ln_mix_b': _jnp.float32, 'ffn_w_up': _jnp.float32, 'ffn_w_gate': _jnp.float32, 'ffn_conv_w': _jnp.float32, 'ffn_conv_b': _jnp.float32, 'ffn_w_down': _jnp.float32, 'ple_w_gate': _jnp.float32, 'ple_w_proj': _jnp.float32, 'ln_ffn_g': _jnp.float32, 'ln_ffn_b': _jnp.float32}
MOMENT_SCALE = {'a_pw1_w': 2.096429e-02, 'a_pw1_b': 3.768230e-02, 'a_dw_w': 2.783651e-02, 'a_dw_b': 8.753425e-02, 'a_ln_g': 4.287896e-02, 'a_ln_b': 5.490774e-02, 'a_pw2_w': 7.535216e-02, 'a_pw2_b': 2.500214e-01, 'b_wq': 5.609297e-03, 'kv_wk': 7.915095e-03, 'kv_wv': 4.174037e-02, 'b_wo': 2.963409e-02, 'ln_mix_g': 9.688972e-01, 'ln_mix_b': 4.374415e-01, 'ffn_w_up': 1.611854e-02, 'ffn_w_gate': 1.653811e-02, 'ffn_conv_w': 1.664067e-02, 'ffn_conv_b': 1.600491e-02, 'ffn_w_down': 6.347770e-02, 'ple_w_gate': 9.253733e-03, 'ple_w_proj': 5.636077e-02, 'ln_ffn_g': 1.606768e+01, 'ln_ffn_b': 1.287323e+00}


def _to_microbatches(a, axis):
    t = _jnp.moveaxis(a, axis, 0)
    t = t.reshape((N_MICROBATCH, t.shape[0] // N_MICROBATCH) + t.shape[1:])
    return _jnp.moveaxis(t, 1, axis + 1)


def setup_inputs(seed: int = 0) -> dict:
    inp = _fwd_setup_inputs(seed)
    key = _jax.random.fold_in(_jax.random.key(seed), 7919)
    shape, _ = _output_shape()
    out = dict(inp)
    out["loss_target"] = _jax.random.normal(_jax.random.fold_in(key, 0), shape, _jnp.float32)
    for i, name in enumerate(TWIN_WEIGHTS):
        w = inp[name].astype(_jnp.float32)
        if MOMENT_SCALE is None:
            s = _jnp.sqrt(_jnp.mean(_jnp.square(w)) + 1e-30)
        else:
            s = MOMENT_SCALE[name]
        km, kv = _jax.random.split(_jax.random.fold_in(key, i + 1))
        out[name] = w
        out["m_" + name] = s * _jax.random.normal(km, w.shape, _jnp.float32)
        out["v_" + name] = (s * s) * _jax.random.uniform(kv, w.shape, _jnp.float32, 0.5, 1.5)
    if N_MICROBATCH > 1:
        for name, axis in PER_EXAMPLE_BATCH_AXIS.items():
            out[name] = _to_microbatches(out[name], axis)
    return {'x': out['x'], 'p': out['p'], 'a_pw1_w': out['a_pw1_w'], 'a_pw1_b': out['a_pw1_b'], 'a_dw_w': out['a_dw_w'], 'a_dw_b': out['a_dw_b'], 'a_ln_g': out['a_ln_g'], 'a_ln_b': out['a_ln_b'], 'a_pw2_w': out['a_pw2_w'], 'a_pw2_b': out['a_pw2_b'], 'b_wq': out['b_wq'], 'kv_wk': out['kv_wk'], 'kv_wv': out['kv_wv'], 'b_wo': out['b_wo'], 'ln_mix_g': out['ln_mix_g'], 'ln_mix_b': out['ln_mix_b'], 'ffn_w_up': out['ffn_w_up'], 'ffn_w_gate': out['ffn_w_gate'], 'ffn_conv_w': out['ffn_conv_w'], 'ffn_conv_b': out['ffn_conv_b'], 'ffn_w_down': out['ffn_w_down'], 'ple_w_gate': out['ple_w_gate'], 'ple_w_proj': out['ple_w_proj'], 'ln_ffn_g': out['ln_ffn_g'], 'ln_ffn_b': out['ln_ffn_b'], 'loss_target': out['loss_target'], 'm_a_pw1_w': out['m_a_pw1_w'], 'm_a_pw1_b': out['m_a_pw1_b'], 'm_a_dw_w': out['m_a_dw_w'], 'm_a_dw_b': out['m_a_dw_b'], 'm_a_ln_g': out['m_a_ln_g'], 'm_a_ln_b': out['m_a_ln_b'], 'm_a_pw2_w': out['m_a_pw2_w'], 'm_a_pw2_b': out['m_a_pw2_b'], 'm_b_wq': out['m_b_wq'], 'm_kv_wk': out['m_kv_wk'], 'm_kv_wv': out['m_kv_wv'], 'm_b_wo': out['m_b_wo'], 'm_ln_mix_g': out['m_ln_mix_g'], 'm_ln_mix_b': out['m_ln_mix_b'], 'm_ffn_w_up': out['m_ffn_w_up'], 'm_ffn_w_gate': out['m_ffn_w_gate'], 'm_ffn_conv_w': out['m_ffn_conv_w'], 'm_ffn_conv_b': out['m_ffn_conv_b'], 'm_ffn_w_down': out['m_ffn_w_down'], 'm_ple_w_gate': out['m_ple_w_gate'], 'm_ple_w_proj': out['m_ple_w_proj'], 'm_ln_ffn_g': out['m_ln_ffn_g'], 'm_ln_ffn_b': out['m_ln_ffn_b'], 'v_a_pw1_w': out['v_a_pw1_w'], 'v_a_pw1_b': out['v_a_pw1_b'], 'v_a_dw_w': out['v_a_dw_w'], 'v_a_dw_b': out['v_a_dw_b'], 'v_a_ln_g': out['v_a_ln_g'], 'v_a_ln_b': out['v_a_ln_b'], 'v_a_pw2_w': out['v_a_pw2_w'], 'v_a_pw2_b': out['v_a_pw2_b'], 'v_b_wq': out['v_b_wq'], 'v_kv_wk': out['v_kv_wk'], 'v_kv_wv': out['v_kv_wv'], 'v_b_wo': out['v_b_wo'], 'v_ln_mix_g': out['v_ln_mix_g'], 'v_ln_mix_b': out['v_ln_mix_b'], 'v_ffn_w_up': out['v_ffn_w_up'], 'v_ffn_w_gate': out['v_ffn_w_gate'], 'v_ffn_conv_w': out['v_ffn_conv_w'], 'v_ffn_conv_b': out['v_ffn_conv_b'], 'v_ffn_w_down': out['v_ffn_w_down'], 'v_ple_w_gate': out['v_ple_w_gate'], 'v_ple_w_proj': out['v_ple_w_proj'], 'v_ln_ffn_g': out['v_ln_ffn_g'], 'v_ln_ffn_b': out['v_ln_ffn_b']}


def _loss(weights, diff, rest, loss_target):
    with _jax.named_scope("forward"):
        args = {**rest, TWIN_DIFF_INPUT: diff, **{k: w.astype(_WEIGHT_DTYPES[k]) for k, w in weights.items()}}
        y = _forward(args)
    with _jax.named_scope("loss_head"):
        err = _jnp.square(y.astype(_jnp.float32) - loss_target)
        return 0.5 * _jnp.sum(_jnp.mean(err, axis=-1)) if err.ndim else 0.5 * err


def _adamw(w, g, m, v):
    m = ADAM_B1 * m + (1.0 - ADAM_B1) * g
    v = ADAM_B2 * v + (1.0 - ADAM_B2) * _jnp.square(g)
    m_hat = m / (1.0 - ADAM_B1 ** ADAM_STEP)
    v_hat = v / (1.0 - ADAM_B2 ** ADAM_STEP)
    delta = -ADAM_LR * (m_hat / (_jnp.sqrt(v_hat) + ADAM_EPS) + ADAM_WD * w)
    return delta, m, v


def reference(x, p, a_pw1_w, a_pw1_b, a_dw_w, a_dw_b, a_ln_g, a_ln_b, a_pw2_w, a_pw2_b, b_wq, kv_wk, kv_wv, b_wo, ln_mix_g, ln_mix_b, ffn_w_up, ffn_w_gate, ffn_conv_w, ffn_conv_b, ffn_w_down, ple_w_gate, ple_w_proj, ln_ffn_g, ln_ffn_b, loss_target, m_a_pw1_w, m_a_pw1_b, m_a_dw_w, m_a_dw_b, m_a_ln_g, m_a_ln_b, m_a_pw2_w, m_a_pw2_b, m_b_wq, m_kv_wk, m_kv_wv, m_b_wo, m_ln_mix_g, m_ln_mix_b, m_ffn_w_up, m_ffn_w_gate, m_ffn_conv_w, m_ffn_conv_b, m_ffn_w_down, m_ple_w_gate, m_ple_w_proj, m_ln_ffn_g, m_ln_ffn_b, v_a_pw1_w, v_a_pw1_b, v_a_dw_w, v_a_dw_b, v_a_ln_g, v_a_ln_b, v_a_pw2_w, v_a_pw2_b, v_b_wq, v_kv_wk, v_kv_wv, v_b_wo, v_ln_mix_g, v_ln_mix_b, v_ffn_w_up, v_ffn_w_gate, v_ffn_conv_w, v_ffn_conv_b, v_ffn_w_down, v_ple_w_gate, v_ple_w_proj, v_ln_ffn_g, v_ln_ffn_b):
    given = dict(x=x, p=p, a_pw1_w=a_pw1_w, a_pw1_b=a_pw1_b, a_dw_w=a_dw_w, a_dw_b=a_dw_b, a_ln_g=a_ln_g, a_ln_b=a_ln_b, a_pw2_w=a_pw2_w, a_pw2_b=a_pw2_b, b_wq=b_wq, kv_wk=kv_wk, kv_wv=kv_wv, b_wo=b_wo, ln_mix_g=ln_mix_g, ln_mix_b=ln_mix_b, ffn_w_up=ffn_w_up, ffn_w_gate=ffn_w_gate, ffn_conv_w=ffn_conv_w, ffn_conv_b=ffn_conv_b, ffn_w_down=ffn_w_down, ple_w_gate=ple_w_gate, ple_w_proj=ple_w_proj, ln_ffn_g=ln_ffn_g, ln_ffn_b=ln_ffn_b, loss_target=loss_target, m_a_pw1_w=m_a_pw1_w, m_a_pw1_b=m_a_pw1_b, m_a_dw_w=m_a_dw_w, m_a_dw_b=m_a_dw_b, m_a_ln_g=m_a_ln_g, m_a_ln_b=m_a_ln_b, m_a_pw2_w=m_a_pw2_w, m_a_pw2_b=m_a_pw2_b, m_b_wq=m_b_wq, m_kv_wk=m_kv_wk, m_kv_wv=m_kv_wv, m_b_wo=m_b_wo, m_ln_mix_g=m_ln_mix_g, m_ln_mix_b=m_ln_mix_b, m_ffn_w_up=m_ffn_w_up, m_ffn_w_gate=m_ffn_w_gate, m_ffn_conv_w=m_ffn_conv_w, m_ffn_conv_b=m_ffn_conv_b, m_ffn_w_down=m_ffn_w_down, m_ple_w_gate=m_ple_w_gate, m_ple_w_proj=m_ple_w_proj, m_ln_ffn_g=m_ln_ffn_g, m_ln_ffn_b=m_ln_ffn_b, v_a_pw1_w=v_a_pw1_w, v_a_pw1_b=v_a_pw1_b, v_a_dw_w=v_a_dw_w, v_a_dw_b=v_a_dw_b, v_a_ln_g=v_a_ln_g, v_a_ln_b=v_a_ln_b, v_a_pw2_w=v_a_pw2_w, v_a_pw2_b=v_a_pw2_b, v_b_wq=v_b_wq, v_kv_wk=v_kv_wk, v_kv_wv=v_kv_wv, v_b_wo=v_b_wo, v_ln_mix_g=v_ln_mix_g, v_ln_mix_b=v_ln_mix_b, v_ffn_w_up=v_ffn_w_up, v_ffn_w_gate=v_ffn_w_gate, v_ffn_conv_w=v_ffn_conv_w, v_ffn_conv_b=v_ffn_conv_b, v_ffn_w_down=v_ffn_w_down, v_ple_w_gate=v_ple_w_gate, v_ple_w_proj=v_ple_w_proj, v_ln_ffn_g=v_ln_ffn_g, v_ln_ffn_b=v_ln_ffn_b)
    weights = {n: given[n] for n in TWIN_WEIGHTS}
    shared = {n: given[n] for n in SHARED_INPUTS}
    per_example = {n: given[n] for n in ['x', 'p']}
    grad_fn = _jax.value_and_grad(_loss, argnums=(0, 1))

    def one_microbatch(ex, loss_target):
        ex = dict(ex)
        diff = ex.pop(TWIN_DIFF_INPUT)
        return grad_fn(weights, diff, {**shared, **ex}, loss_target)

    if N_MICROBATCH == 1:
        loss, (grad_w, grad_x) = one_microbatch(per_example, given["loss_target"])
    else:
        def body(carry, xs):
            loss_sum, grad_sum = carry
            l_k, (gw_k, gx_k) = one_microbatch(xs[0], xs[1])
            with _jax.named_scope("update"):
                return (loss_sum + l_k, _jax.tree.map(_jnp.add, grad_sum, gw_k)), gx_k

        init = (_jnp.zeros((), _jnp.float32), _jax.tree.map(_jnp.zeros_like, weights))
        (loss, grad_w), grad_x = _jax.lax.scan(body, init, (per_example, given["loss_target"]))
    with _jax.named_scope("update"):
        delta_w, new_m, new_v = {}, {}, {}
        for n in TWIN_WEIGHTS:
            delta_w[n], new_m[n], new_v[n] = _adamw(weights[n], grad_w[n], given["m_" + n], given["v_" + n])
    return (loss, grad_x, *[grad_w[n] for n in TWIN_WEIGHTS], *[delta_w[n] for n in TWIN_WEIGHTS],
            *[new_m[n] for n in TWIN_WEIGHTS], *[new_v[n] for n in TWIN_WEIGHTS])
```

```python
import functools
import math

import jax
import jax.numpy as jnp
import numpy as np
from jax import lax
from jax.experimental import pallas as pl
from jax.experimental.pallas import tpu as pltpu

F32, BF16 = jnp.float32, jnp.bfloat16

HEAD_DIM = 64
LN_EPS = 1e-5
DEPTH = 4
N_A = DEPTH // 2
DN_ALPHA = (2.0 * DEPTH) ** 0.25
N_CHIPS = 4

ADAM_LR, ADAM_B1, ADAM_B2, ADAM_EPS, ADAM_WD, ADAM_STEP = 0.001, 0.9, 0.999, 1e-08, 0.01, 10

VMEM_LIMIT_BYTES = 56 * 2**20
LANES = 128
CONV_HALO = 32
FFN_HALO = 8

NN = (((1,), (0,)), ((), ()))
NT = (((1,), (1,)), ((), ()))
TN = (((0,), (0,)), ((), ()))

WEIGHTS = ['a_pw1_w', 'a_pw1_b', 'a_dw_w', 'a_dw_b', 'a_ln_g', 'a_ln_b', 'a_pw2_w', 'a_pw2_b', 'b_wq', 'kv_wk', 'kv_wv',
           'b_wo', 'ln_mix_g', 'ln_mix_b', 'ffn_w_up', 'ffn_w_gate', 'ffn_conv_w', 'ffn_conv_b', 'ffn_w_down', 'ple_w_gate',
           'ple_w_proj', 'ln_ffn_g', 'ln_ffn_b']
REPLICATED = ('ln_mix_g', 'ln_mix_b', 'ffn_conv_b', 'ln_ffn_g', 'ln_ffn_b')
BIG = ('a_pw1_w', 'a_pw2_w', 'b_wq', 'kv_wk', 'kv_wv', 'b_wo', 'ffn_w_up', 'ffn_w_gate', 'ffn_w_down', 'ple_w_gate',
       'ple_w_proj')


def _tile(n, pref, mult=8):
    t = min(n, pref)
    while t > 0:
        if n % t == 0 and t % mult == 0:
            return t
        t -= 1
    return n


def _params(sem):
    return pltpu.CompilerParams(dimension_semantics=sem, vmem_limit_bytes=VMEM_LIMIT_BYTES)


def _sigmoid(x):
    return 1.0 / (1.0 + jnp.exp(-x))


def _mm_call(name, grid, terms, out_spec, out_sds, dims, bias=None, res=None, res_alpha=1.0, out_scale=None):
    n_terms = len(terms)

    def body(*refs):
        o_ref = refs[-1]
        acc = None
        for t in range(n_terms):
            a = refs[2 * t][...].astype(BF16)
            b = refs[2 * t + 1][...].astype(BF16)
            d = lax.dot_general(a, b, dims, preferred_element_type=F32)
            acc = d if acc is None else acc + d
        k = 2 * n_terms
        if bias is not None:
            acc = acc + refs[k][...]
            k += 1
        if res is not None:
            acc = acc + res_alpha * refs[k][...]
        if out_scale is not None:
            acc = acc * out_scale
        o_ref[...] = acc.astype(o_ref.dtype)

    operands, specs = [], []
    for a, a_spec, b, b_spec in terms:
        operands += [a, b]
        specs += [a_spec, b_spec]
    for extra in (bias, res):
        if extra is not None:
            operands.append(extra[0])
            specs.append(extra[1])
    return pl.pallas_call(body, out_shape=out_sds, grid=grid, in_specs=specs, out_specs=out_spec,
                          compiler_params=_params(("parallel",) * len(grid)), name=name)(*operands)


def mm_colsm(name, x, w4, layer, bias4=None, tm=512):
    M, K = x.shape
    n = w4.shape[-1]
    tm = _tile(M, tm)
    terms = [(x, pl.BlockSpec((tm, K), lambda j, i: (i, 0)), w4, pl.BlockSpec((None, None, K, n), lambda j, i: (j, layer, 0, 0)))]
    bias = None if bias4 is None else (bias4, pl.BlockSpec((None, None, 1, n), lambda j, i: (j, layer, 0, 0)))
    return _mm_call(name, (N_CHIPS, M // tm), terms, pl.BlockSpec((None, tm, n), lambda j, i: (j, i, 0)),
                    jax.ShapeDtypeStruct((N_CHIPS, M, n), F32), NN, bias=bias)


def mm_rowsm(name, a, w4, layer, a_sm=False, bias=None, out_dtype=F32, out_scale=None, tm=512):
    kc, N = w4.shape[-2:]
    M = a.shape[-2]
    tm = _tile(M, tm)
    terms = []
    for j in range(N_CHIPS):
        if a_sm:
            a_spec = pl.BlockSpec((None, tm, kc), lambda i, j=j: (j, i, 0))
        else:
            a_spec = pl.BlockSpec((tm, kc), lambda i, j=j: (i, j))
        terms.append((a, a_spec, w4, pl.BlockSpec((None, None, kc, N), lambda i, j=j: (j, layer, 0, 0))))
    b = None if bias is None else (bias, pl.BlockSpec((None, 1, N), lambda i: (layer, 0, 0)))
    return _mm_call(name, (M // tm,), terms, pl.BlockSpec((tm, N), lambda i: (i, 0)), jax.ShapeDtypeStruct((M, N), out_dtype),
                    NN, bias=b, out_scale=out_scale)


def mm_nt_rowsm(name, dy, w4, layer, out_sm=False, res=None, res_alpha=1.0, tm=512):
    kc, N = w4.shape[-2:]
    M = dy.shape[0]
    tm = _tile(M, tm)
    terms = [(dy, pl.BlockSpec((tm, N), lambda j, i: (i, 0)), w4, pl.BlockSpec((None, None, kc, N), lambda j, i: (j, layer, 0, 0)))]
    if out_sm:
        out_spec, sds = pl.BlockSpec((None, tm, kc), lambda j, i: (j, i, 0)), jax.ShapeDtypeStruct((N_CHIPS, M, kc), F32)
    else:
        out_spec, sds = pl.BlockSpec((tm, kc), lambda j, i: (i, j)), jax.ShapeDtypeStruct((M, N_CHIPS * kc), F32)
    r = None if res is None else (res, pl.BlockSpec((tm, kc), lambda j, i: (i, j)))
    return _mm_call(name, (N_CHIPS, M // tm), terms, out_spec, sds, NT, res=r, res_alpha=res_alpha)


def mm_nt_colsm(name, dy_sm, w4, layer, res=None, res_alpha=1.0, tm=512):
    K, n = w4.shape[-2:]
    M = dy_sm.shape[1]
    tm = _tile(M, tm)
    terms = [(dy_sm, pl.BlockSpec((None, tm, n), lambda i, j=j: (j, i, 0)), w4,
              pl.BlockSpec((None, None, K, n), lambda i, j=j: (j, layer, 0, 0))) for j in range(N_CHIPS)]
    r = None if res is None else (res, pl.BlockSpec((tm, K), lambda i: (i, 0)))
    return _mm_call(name, (M // tm,), terms, pl.BlockSpec((tm, K), lambda i: (i, 0)), jax.ShapeDtypeStruct((M, K), F32), NT,
                    res=r, res_alpha=res_alpha)


def mm_tn_colsm(name, x, dy_sm, tk=512):
    M, K = x.shape
    n = dy_sm.shape[-1]
    tk = _tile(K, tk, LANES)
    terms = [(x, pl.BlockSpec((M, tk), lambda j, k: (0, k)), dy_sm, pl.BlockSpec((None, M, n), lambda j, k: (j, 0, 0)))]
    return _mm_call(name, (N_CHIPS, K // tk), terms, pl.BlockSpec((None, tk, n), lambda j, k: (j, k, 0)),
                    jax.ShapeDtypeStruct((N_CHIPS, K, n), F32), TN)


def mm_tn_rowsm(name, a, dy, a_sm=False, tn=512):
    M, N = dy.shape
    kc = a.shape[-1] if a_sm else a.shape[-1] // N_CHIPS
    tn = _tile(N, tn, LANES)
    a_spec = pl.BlockSpec((None, M, kc), lambda j, n: (j, 0, 0)) if a_sm else pl.BlockSpec((M, kc), lambda j, n: (0, j))
    terms = [(a, a_spec, dy, pl.BlockSpec((M, tn), lambda j, n: (0, n)))]
    return _mm_call(name, (N_CHIPS, N // tn), terms, pl.BlockSpec((None, kc, tn), lambda j, n: (j, 0, n)),
                    jax.ShapeDtypeStruct((N_CHIPS, kc, N), F32), TN)


def mm_proj(name, p4, layer, w4, tm=512):
    S, P = p4.shape[-2:]
    n = w4.shape[-1]
    tm = _tile(S, tm)
    terms = [(p4, pl.BlockSpec((None, None, tm, P), lambda j, i: (layer, 0, i, 0)), w4,
              pl.BlockSpec((None, None, P, n), lambda j, i: (j, layer, 0, 0)))]
    return _mm_call(name, (N_CHIPS, S // tm), terms, pl.BlockSpec((tm, n), lambda j, i: (i, j)),
                    jax.ShapeDtypeStruct((S, N_CHIPS * n), F32), NN)


def mm_tn_proj(name, p4, layer, dpp):
    S, P = p4.shape[-2:]
    n = dpp.shape[-1] // N_CHIPS
    terms = [(p4, pl.BlockSpec((None, None, S, P), lambda j: (layer, 0, 0, 0)), dpp, pl.BlockSpec((S, n), lambda j: (0, j)))]
    return _mm_call(name, (N_CHIPS,), terms, pl.BlockSpec((None, P, n), lambda j: (j, 0, 0)),
                    jax.ShapeDtypeStruct((N_CHIPS, P, n), F32), TN)


def ln_fwd(name, x, mix, g, b, layer, tg=None, pp=None, tm=256):
    S, D = x.shape
    tm = _tile(S, tm)
    ple = tg is not None

    def body(*refs):
        if ple:
            x_ref, m_ref, tg_ref, pp_ref, g_ref, b_ref, y_ref, xh_ref, rs_ref = refs
        else:
            x_ref, m_ref, g_ref, b_ref, y_ref, xh_ref, rs_ref = refs
        r = DN_ALPHA * x_ref[...] + m_ref[...]
        if ple:
            r = r + _sigmoid(tg_ref[...]) * pp_ref[...]
        mu = jnp.mean(r, axis=-1, keepdims=True)
        d = r - mu
        var = jnp.mean(d * d, axis=-1, keepdims=True)
        rstd = lax.rsqrt(var + LN_EPS)
        xh = d * rstd
        y_ref[...] = xh * g_ref[...] + b_ref[...]
        xh_ref[...] = xh
        rs_ref[...] = rstd

    row = pl.BlockSpec((tm, D), lambda i: (i, 0))
    vec = pl.BlockSpec((None, 1, D), lambda i: (layer, 0, 0))
    ins = [x, mix] + ([tg, pp] if ple else []) + [g, b]
    specs = [row, row] + ([row, row] if ple else []) + [vec, vec]
    return pl.pallas_call(body, grid=(S // tm,), in_specs=specs,
                          out_specs=(row, row, pl.BlockSpec((tm, 1), lambda i: (i, 0))),
                          out_shape=(jax.ShapeDtypeStruct((S, D), F32), jax.ShapeDtypeStruct((S, D), F32),
                                     jax.ShapeDtypeStruct((S, 1), F32)),
                          compiler_params=_params(("parallel",)), name=name)(*ins)


def ln_bwd(name, dy, xh, rstd, g, layer, tg=None, pp=None, tm=256):
    S, D = dy.shape
    tm = _tile(S, tm)
    ple = tg is not None

    def body(*refs):
        if ple:
            dy_ref, xh_ref, rs_ref, g_ref, tg_ref, pp_ref, dr_ref, acc_ref, dtg_ref, dpp_ref = refs
        else:
            dy_ref, xh_ref, rs_ref, g_ref, dr_ref, acc_ref = refs
        dy_, xh_ = dy_ref[...], xh_ref[...]
        dxh = dy_ * g_ref[...]
        m1 = jnp.mean(dxh, axis=-1, keepdims=True)
        m2 = jnp.mean(dxh * xh_, axis=-1, keepdims=True)
        dr = rs_ref[...] * (dxh - m1 - xh_ * m2)
        dr_ref[...] = dr

        @pl.when(pl.program_id(0) == 0)
        def _():
            acc_ref[...] = jnp.zeros_like(acc_ref)
        acc_ref[0:1, :] += jnp.sum(dy_ * xh_, axis=0, keepdims=True)
        acc_ref[1:2, :] += jnp.sum(dy_, axis=0, keepdims=True)
        acc_ref[2:3, :] += jnp.sum(dr, axis=0, keepdims=True)
        if ple:
            pg = _sigmoid(tg_ref[...])
            dtg_ref[...] = (dr * pp_ref[...] * pg * (1.0 - pg)).astype(BF16)
            dpp_ref[...] = (dr * pg).astype(BF16)

    row = pl.BlockSpec((tm, D), lambda i: (i, 0))
    ins = [dy, xh, rstd, g] + ([tg, pp] if ple else [])
    specs = [row, row, pl.BlockSpec((tm, 1), lambda i: (i, 0)), pl.BlockSpec((None, 1, D), lambda i: (layer, 0, 0))] + ([row, row] if ple else [])
    outs = [jax.ShapeDtypeStruct((S, D), F32), jax.ShapeDtypeStruct((8, D), F32)]
    out_specs = [row, pl.BlockSpec((8, D), lambda i: (0, 0))]
    if ple:
        outs += [jax.ShapeDtypeStruct((S, D), BF16)] * 2
        out_specs += [row, row]
    return pl.pallas_call(body, grid=(S // tm,), in_specs=specs, out_specs=tuple(out_specs), out_shape=tuple(outs),
                          compiler_params=_params(("arbitrary",)), name=name)(*ins)


def glu_fwd(name, h_sm, tm=512):
    _, S, n = h_sm.shape
    tm = _tile(S, tm)
    half = N_CHIPS // 2

    def body(a_ref, g_ref, u_ref):
        u_ref[...] = a_ref[...] * _sigmoid(g_ref[...])

    return pl.pallas_call(body, grid=(half, S // tm),
                          in_specs=[pl.BlockSpec((None, tm, n), lambda j, i: (j, i, 0)),
                                    pl.BlockSpec((None, tm, n), lambda j, i: (j + half, i, 0))],
                          out_specs=pl.BlockSpec((tm, n), lambda j, i: (i, j)),
                          out_shape=jax.ShapeDtypeStruct((S, half * n), F32),
                          compiler_params=_params(("parallel", "parallel")), name=name)(h_sm, h_sm)


def conv_ln_silu_fwd(name, u, w, b, g, beta, layer, ts=128):
    S, D = u.shape
    kw = w.shape[1]
    ts = _tile(S, ts, CONV_HALO)
    lc = LANES if D % LANES == 0 else D

    def body(h_ref, u_ref, w_ref, b_ref, g_ref, be_ref, c_ref, s_ref, win_ref):
        i = pl.program_id(0)
        win_ref[0:CONV_HALO, :] = jnp.where(i == 0, 0.0, h_ref[...])
        win_ref[CONV_HALO:, :] = u_ref[...]
        for cc in range(D // lc):
            cs = slice(cc * lc, (cc + 1) * lc)
            acc = jnp.zeros((ts, lc), F32) + b_ref[:, cs]
            for k in range(kw):
                off = CONV_HALO - (kw - 1) + k
                acc = acc + w_ref[k:k + 1, cs] * win_ref[off:off + ts, cs]
            c_ref[:, cs] = acc
        c = c_ref[...]
        mu = jnp.mean(c, axis=-1, keepdims=True)
        d = c - mu
        var = jnp.mean(d * d, axis=-1, keepdims=True)
        nrm = d * lax.rsqrt(var + LN_EPS) * g_ref[...] + be_ref[...]
        s_ref[...] = (nrm * _sigmoid(nrm)).astype(BF16)

    row = pl.BlockSpec((ts, D), lambda i: (i, 0))
    vec = pl.BlockSpec((None, 1, D), lambda i: (layer, 0, 0))
    halo = pl.BlockSpec((CONV_HALO, D), lambda i: (jnp.maximum(i * (ts // CONV_HALO) - 1, 0), 0))
    return pl.pallas_call(body, grid=(S // ts,),
                          in_specs=[halo, row, pl.BlockSpec((None, kw, D), lambda i: (layer, 0, 0)), vec, vec, vec],
                          out_specs=(row, row),
                          out_shape=(jax.ShapeDtypeStruct((S, D), F32), jax.ShapeDtypeStruct((S, D), BF16)),
                          scratch_shapes=[pltpu.VMEM((ts + CONV_HALO, D), F32)],
                          compiler_params=_params(("parallel",)), name=name)(u, u, w, b, g, beta)


def ln_silu_bwd(name, ds, c, g, beta, layer, tm=256):
    S, D = c.shape
    tm = _tile(S, tm)

    def body(ds_ref, c_ref, g_ref, be_ref, dc_ref, acc_ref):
        c_ = c_ref[...]
        mu = jnp.mean(c_, axis=-1, keepdims=True)
        d = c_ - mu
        var = jnp.mean(d * d, axis=-1, keepdims=True)
        rstd = lax.rsqrt(var + LN_EPS)
        xh = d * rstd
        nrm = xh * g_ref[...] + be_ref[...]
        sg = _sigmoid(nrm)
        dn = ds_ref[...] * (sg * (1.0 + nrm * (1.0 - sg)))
        dxh = dn * g_ref[...]
        m1 = jnp.mean(dxh, axis=-1, keepdims=True)
        m2 = jnp.mean(dxh * xh, axis=-1, keepdims=True)
        dc = rstd * (dxh - m1 - xh * m2)
        dc_ref[...] = dc

        @pl.when(pl.program_id(0) == 0)
        def _():
            acc_ref[...] = jnp.zeros_like(acc_ref)
        acc_ref[0:1, :] += jnp.sum(dn * xh, axis=0, keepdims=True)
        acc_ref[1:2, :] += jnp.sum(dn, axis=0, keepdims=True)
        acc_ref[2:3, :] += jnp.sum(dc, axis=0, keepdims=True)

    row = pl.BlockSpec((tm, D), lambda i: (i, 0))
    vec = pl.BlockSpec((None, 1, D), lambda i: (layer, 0, 0))
    return pl.pallas_call(body, grid=(S // tm,), in_specs=[row, row, vec, vec],
                          out_specs=(row, pl.BlockSpec((8, D), lambda i: (0, 0))),
                          out_shape=(jax.ShapeDtypeStruct((S, D), F32), jax.ShapeDtypeStruct((8, D), F32)),
                          compiler_params=_params(("arbitrary",)), name=name)(ds, c, g, beta)


def conv_glu_bwd(name, dc, u, h_sm, w, layer, ts=128):
    S, D = dc.shape
    kw = w.shape[1]
    half = N_CHIPS // 2
    n = D // half
    ts = _tile(S, ts, CONV_HALO)
    nblk = S // ts
    lc = LANES if n % LANES == 0 else n

    def body(dc_ref, dcn_ref, u_ref, up_ref, a_ref, g_ref, w_ref, da_ref, dg_ref, dw_ref, dba_ref, dbg_ref, dwin_ref, uwin_ref):
        i = pl.program_id(1)
        dwin_ref[0:ts, :] = dc_ref[...]
        dwin_ref[ts:, :] = jnp.where(i == nblk - 1, 0.0, dcn_ref[...])
        uwin_ref[0:CONV_HALO, :] = jnp.where(i == 0, 0.0, up_ref[...])
        uwin_ref[CONV_HALO:, :] = u_ref[...]

        @pl.when(i == 0)
        def _():
            dw_ref[...] = jnp.zeros_like(dw_ref)
            dba_ref[...] = jnp.zeros_like(dba_ref)
            dbg_ref[...] = jnp.zeros_like(dbg_ref)

        for cc in range(n // lc):
            cs = slice(cc * lc, (cc + 1) * lc)
            dcb = dwin_ref[0:ts, cs]
            du = jnp.zeros((ts, lc), F32)
            for k in range(kw):
                du = du + w_ref[k:k + 1, cs] * dwin_ref[kw - 1 - k:kw - 1 - k + ts, cs]
                off = CONV_HALO - (kw - 1) + k
                dw_ref[k:k + 1, cs] += jnp.sum(dcb * uwin_ref[off:off + ts, cs], axis=0, keepdims=True)
            a = a_ref[:, cs]
            sg = _sigmoid(g_ref[:, cs])
            da = du * sg
            dg = du * a * sg * (1.0 - sg)
            da_ref[:, cs] = da.astype(BF16)
            dg_ref[:, cs] = dg.astype(BF16)
            dba_ref[0:1, cs] += jnp.sum(da, axis=0, keepdims=True)
            dbg_ref[0:1, cs] += jnp.sum(dg, axis=0, keepdims=True)

    r = ts // CONV_HALO
    main = pl.BlockSpec((ts, n), lambda j, i: (i, j))
    nxt = pl.BlockSpec((CONV_HALO, n), lambda j, i: (jnp.minimum((i + 1) * r, S // CONV_HALO - 1), j))
    prv = pl.BlockSpec((CONV_HALO, n), lambda j, i: (jnp.maximum(i * r - 1, 0), j))
    sm_a = pl.BlockSpec((None, ts, n), lambda j, i: (j, i, 0))
    sm_g = pl.BlockSpec((None, ts, n), lambda j, i: (j + half, i, 0))
    acc8a = pl.BlockSpec((None, 8, n), lambda j, i: (j, 0, 0))
    acc8g = pl.BlockSpec((None, 8, n), lambda j, i: (j + half, 0, 0))
    da, dg, dw, dba, dbg = pl.pallas_call(
        body, grid=(half, nblk),
        in_specs=[main, nxt, main, prv, sm_a, sm_g, pl.BlockSpec((None, kw, n), lambda j, i: (layer, 0, j))],
        out_specs=(pl.BlockSpec((None, ts, n), lambda j, i: (j, i, 0)), pl.BlockSpec((None, ts, n), lambda j, i: (j, i, 0)),
                   pl.BlockSpec((None, 32, n), lambda j, i: (j, 0, 0)),
                   pl.BlockSpec((None, 8, n), lambda j, i: (j, 0, 0)), pl.BlockSpec((None, 8, n), lambda j, i: (j, 0, 0))),
        out_shape=(jax.ShapeDtypeStruct((half, S, n), BF16), jax.ShapeDtypeStruct((half, S, n), BF16),
                   jax.ShapeDtypeStruct((half, 32, n), F32),
                   jax.ShapeDtypeStruct((half, 8, n), F32), jax.ShapeDtypeStruct((half, 8, n), F32)),
        scratch_shapes=[pltpu.VMEM((ts + CONV_HALO, n), F32), pltpu.VMEM((ts + CONV_HALO, n), F32)],
        compiler_params=_params(("parallel", "arbitrary")), name=name)(dc, dc, u, u, h_sm, h_sm, w)
    del acc8a, acc8g
    return da, dg, dw, dba, dbg


def _ffn_gc(win_ref, w_ref, b_ref, ts, kw, base):
    gc = b_ref[...] + jnp.zeros((ts, win_ref.shape[1]), F32)
    for k in range(kw):
        off = base - (kw - 1) + k
        gc = gc + w_ref[k:k + 1, :] * win_ref[off:off + ts, :]
    return gc


def ffn_gate_fwd(name, up_sm, gp_sm, w4, b4, layer, ts=128):
    _, S, n = up_sm.shape
    kw = w4.shape[2]
    ts = _tile(S, ts, FFN_HALO)

    def body(up_ref, gp_ref, gph_ref, w_ref, b_ref, hf_ref, win_ref):
        i = pl.program_id(1)
        win_ref[0:FFN_HALO, :] = jnp.where(i == 0, 0.0, gph_ref[...])
        win_ref[FFN_HALO:, :] = gp_ref[...]
        gc = _ffn_gc(win_ref, w_ref, b_ref, ts, kw, FFN_HALO)
        hf_ref[...] = (gc * _sigmoid(gc) * up_ref[...]).astype(BF16)

    main = pl.BlockSpec((None, ts, n), lambda j, i: (j, i, 0))
    prv = pl.BlockSpec((None, FFN_HALO, n), lambda j, i: (j, jnp.maximum(i * (ts // FFN_HALO) - 1, 0), 0))
    return pl.pallas_call(body, grid=(N_CHIPS, S // ts),
                          in_specs=[main, main, prv, pl.BlockSpec((None, None, kw, n), lambda j, i: (j, layer, 0, 0)),
                                    pl.BlockSpec((None, None, 1, n), lambda j, i: (layer, j, 0, 0))],
                          out_specs=main, out_shape=jax.ShapeDtypeStruct((N_CHIPS, S, n), BF16),
                          scratch_shapes=[pltpu.VMEM((ts + FFN_HALO, n), F32)],
                          compiler_params=_params(("parallel", "parallel")), name=name)(up_sm, gp_sm, gp_sm, w4, b4)


def ffn_gate_bwd_a(name, dhf_sm, up_sm, gp_sm, w4, b4, layer, ts=128):
    _, S, n = up_sm.shape
    kw = w4.shape[2]
    ts = _tile(S, ts, FFN_HALO)

    def body(dhf_ref, up_ref, gp_ref, gph_ref, w_ref, b_ref, dup_ref, dgc_ref, win_ref):
        i = pl.program_id(1)
        win_ref[0:FFN_HALO, :] = jnp.where(i == 0, 0.0, gph_ref[...])
        win_ref[FFN_HALO:, :] = gp_ref[...]
        gc = _ffn_gc(win_ref, w_ref, b_ref, ts, kw, FFN_HALO)
        sg = _sigmoid(gc)
        dhf = dhf_ref[...]
        dup_ref[...] = (dhf * gc * sg).astype(BF16)
        dgc_ref[...] = dhf * up_ref[...] * (sg * (1.0 + gc * (1.0 - sg)))

    main = pl.BlockSpec((None, ts, n), lambda j, i: (j, i, 0))
    prv = pl.BlockSpec((None, FFN_HALO, n), lambda j, i: (j, jnp.maximum(i * (ts // FFN_HALO) - 1, 0), 0))
    return pl.pallas_call(body, grid=(N_CHIPS, S // ts),
                          in_specs=[main, main, main, prv, pl.BlockSpec((None, None, kw, n), lambda j, i: (j, layer, 0, 0)),
                                    pl.BlockSpec((None, None, 1, n), lambda j, i: (layer, j, 0, 0))],
                          out_specs=(main, main),
                          out_shape=(jax.ShapeDtypeStruct((N_CHIPS, S, n), BF16), jax.ShapeDtypeStruct((N_CHIPS, S, n), F32)),
                          scratch_shapes=[pltpu.VMEM((ts + FFN_HALO, n), F32)],
                          compiler_params=_params(("parallel", "parallel")), name=name)(dhf_sm, up_sm, gp_sm, gp_sm, w4, b4)


def ffn_gate_bwd_b(name, dgc_sm, gp_sm, w4, layer, ts=128):
    _, S, n = gp_sm.shape
    kw = w4.shape[2]
    ts = _tile(S, ts, FFN_HALO)
    nblk = S // ts

    def body(dgc_ref, dgn_ref, gp_ref, gph_ref, w_ref, dgp_ref, acc_ref, dwin_ref, gwin_ref):
        i = pl.program_id(1)
        dgc = dgc_ref[...]
        dwin_ref[0:ts, :] = dgc
        dwin_ref[ts:, :] = jnp.where(i == nblk - 1, 0.0, dgn_ref[...])
        gwin_ref[0:FFN_HALO, :] = jnp.where(i == 0, 0.0, gph_ref[...])
        gwin_ref[FFN_HALO:, :] = gp_ref[...]

        @pl.when(i == 0)
        def _():
            acc_ref[...] = jnp.zeros_like(acc_ref)
        dgp = jnp.zeros((ts, n), F32)
        for k in range(kw):
            dgp = dgp + w_ref[k:k + 1, :] * dwin_ref[kw - 1 - k:kw - 1 - k + ts, :]
            off = FFN_HALO - (kw - 1) + k
            acc_ref[k:k + 1, :] += jnp.sum(dgc * gwin_ref[off:off + ts, :], axis=0, keepdims=True)
        acc_ref[7:8, :] += jnp.sum(dgc, axis=0, keepdims=True)
        dgp_ref[...] = dgp.astype(BF16)

    r = ts // FFN_HALO
    main = pl.BlockSpec((None, ts, n), lambda j, i: (j, i, 0))
    nxt = pl.BlockSpec((None, FFN_HALO, n), lambda j, i: (j, jnp.minimum((i + 1) * r, S // FFN_HALO - 1), 0))
    prv = pl.BlockSpec((None, FFN_HALO, n), lambda j, i: (j, jnp.maximum(i * r - 1, 0), 0))
    return pl.pallas_call(body, grid=(N_CHIPS, nblk),
                          in_specs=[main, nxt, main, prv, pl.BlockSpec((None, None, kw, n), lambda j, i: (j, layer, 0, 0))],
                          out_specs=(main, pl.BlockSpec((None, 8, n), lambda j, i: (j, 0, 0))),
                          out_shape=(jax.ShapeDtypeStruct((N_CHIPS, S, n), BF16), jax.ShapeDtypeStruct((N_CHIPS, 8, n), F32)),
                          scratch_shapes=[pltpu.VMEM((ts + FFN_HALO, n), F32), pltpu.VMEM((ts + FFN_HALO, n), F32)],
                          compiler_params=_params(("parallel", "arbitrary")), name=name)(dgc_sm, dgc_sm, gp_sm, gp_sm, w4)


def _neg_softplus(z):
    e = jnp.exp(-jnp.abs(z))
    return -(jnp.maximum(z, 0.0) + jnp.log(1.0 + e)), e


def _split_dot(x, t):
    hi = x.astype(BF16)
    lo = (x - hi.astype(F32)).astype(BF16)
    return jnp.dot(hi, t, preferred_element_type=F32) + jnp.dot(lo, t, preferred_element_type=F32)


def attn_fwd(name, q, k, v, bq=512, w=256):
    H, S, dh = q.shape
    bq = _tile(S, bq)
    w = _tile(bq, w)
    nsub = bq // w

    def body(q_ref, k_ref, v_ref, o_ref, tot_ref):
        qi = pl.program_id(1)
        qb = q_ref[...]
        rr = lax.broadcasted_iota(jnp.int32, (w, w), 0)
        cc = lax.broadcasted_iota(jnp.int32, (w, w), 1)
        t_suf = (rr >= cc).astype(BF16)
        tq = qi * bq + lax.broadcasted_iota(jnp.int32, (bq, w), 0)
        tk = lax.broadcasted_iota(jnp.int32, (bq, w), 1)

        def block(kstart, run, acc, masked):
            kb = k_ref[pl.ds(kstart, w), :]
            vb = v_ref[pl.ds(kstart, w), :]
            z = lax.dot_general(qb, kb, NT, preferred_element_type=F32)
            lg, _ = _neg_softplus(z)
            if masked:
                m = (tk + kstart) < tq
                lg = jnp.where(m, lg, 0.0)
            cum = _split_dot(lg, t_suf) + run
            a = jnp.exp(z + cum)
            if masked:
                a = jnp.where(m, a, 0.0)
            acc = acc + jnp.dot(a.astype(BF16), vb, preferred_element_type=F32)
            return cum[:, 0:1], acc

        run = jnp.zeros((bq, 1), F32)
        acc = jnp.zeros((bq, dh), F32)
        for sb in reversed(range(nsub)):
            run, acc = block(pl.multiple_of(qi * bq + sb * w, w), run, acc, True)
        nblk = qi * nsub

        def loop(it, carry):
            return block(pl.multiple_of((nblk - 1 - it) * w, w), carry[0], carry[1], False)
        run, acc = lax.fori_loop(0, nblk, loop, (run, acc))
        o_ref[...] = acc
        tot_ref[...] = run

    qs = pl.BlockSpec((None, bq, dh), lambda h, i: (h, i, 0))
    kv = pl.BlockSpec((None, S, dh), lambda h, i: (h, 0, 0))
    return pl.pallas_call(body, grid=(H, S // bq), in_specs=[qs, kv, kv],
                          out_specs=(qs, pl.BlockSpec((None, bq, 1), lambda h, i: (h, i, 0))),
                          out_shape=(jax.ShapeDtypeStruct((H, S, dh), F32), jax.ShapeDtypeStruct((H, S, 1), F32)),
                          compiler_params=_params(("parallel", "parallel")), name=name)(q, k, v)


def attn_bwd(name, q, k, v, do, tot, dk0=None, dv0=None, bq=512, w=256):
    H, S, dh = q.shape
    bq = _tile(S, bq)
    w = _tile(bq, w)
    nsub = bq // w
    scale = 1.0 / math.sqrt(dh)
    init = dk0 is not None

    def body(*refs):
        if init:
            q_ref, k_ref, v_ref, do_ref, tot_ref, dk0_ref, dv0_ref, dq_ref, dk_ref, dv_ref = refs
        else:
            q_ref, k_ref, v_ref, do_ref, tot_ref, dq_ref, dk_ref, dv_ref = refs
        qi = pl.program_id(1)

        @pl.when(qi == 0)
        def _():
            dk_ref[...] = dk0_ref[...] if init else jnp.zeros_like(dk_ref)
            dv_ref[...] = dv0_ref[...] if init else jnp.zeros_like(dv_ref)

        qb = q_ref[...]
        dob = do_ref[...]
        tot_ = tot_ref[...]
        rr = lax.broadcasted_iota(jnp.int32, (w, w), 0)
        cc = lax.broadcasted_iota(jnp.int32, (w, w), 1)
        t_pre_ex = (rr < cc).astype(BF16)
        t_pre_in = (rr <= cc).astype(BF16)
        tq = qi * bq + lax.broadcasted_iota(jnp.int32, (bq, w), 0)
        tk = lax.broadcasted_iota(jnp.int32, (bq, w), 1)

        def block(kstart, pl_run, pg_run, dq, masked):
            kb = k_ref[pl.ds(kstart, w), :]
            vb = v_ref[pl.ds(kstart, w), :]
            z = lax.dot_general(qb, kb, NT, preferred_element_type=F32)
            lg, e = _neg_softplus(z)
            if masked:
                m = (tk + kstart) < tq
                lg = jnp.where(m, lg, 0.0)
            pex = _split_dot(lg, t_pre_ex)
            a = jnp.exp(z + (tot_ - pl_run - pex))
            if masked:
                a = jnp.where(m, a, 0.0)
            da = lax.dot_general(dob, vb, NT, preferred_element_type=F32)
            g = da * a
            pin = _split_dot(g, t_pre_in) + pg_run
            sig = jnp.where(z >= 0.0, 1.0, e) / (1.0 + e)
            dz = g - sig * pin
            if masked:
                dz = jnp.where(m, dz, 0.0)
            dzb = dz.astype(BF16)
            dq = dq + jnp.dot(dzb, kb, preferred_element_type=F32)
            dk_ref[pl.ds(kstart, w), :] += lax.dot_general(dzb, qb, TN, preferred_element_type=F32)
            dv_ref[pl.ds(kstart, w), :] += lax.dot_general(a.astype(BF16), dob, TN, preferred_element_type=F32)
            pl_run = pl_run + pex[:, w - 1:w] + lg[:, w - 1:w]
            return pl_run, pin[:, w - 1:w], dq

        zero = jnp.zeros((bq, 1), F32)

        def loop(it, carry):
            return block(pl.multiple_of(it * w, w), carry[0], carry[1], carry[2], False)
        pl_run, pg_run, dq = lax.fori_loop(0, qi * nsub, loop, (zero, zero, jnp.zeros((bq, dh), F32)))
        for sb in range(nsub):
            pl_run, pg_run, dq = block(pl.multiple_of(qi * bq + sb * w, w), pl_run, pg_run, dq, True)
        dq_ref[...] = dq * scale

    qs = pl.BlockSpec((None, bq, dh), lambda h, i: (h, i, 0))
    kv = pl.BlockSpec((None, S, dh), lambda h, i: (h, 0, 0))
    ins = [q, k, v, do, tot] + ([dk0, dv0] if init else [])
    specs = [qs, kv, kv, qs, pl.BlockSpec((None, bq, 1), lambda h, i: (h, i, 0))] + ([kv, kv] if init else [])
    sds = jax.ShapeDtypeStruct((H, S, dh), F32)
    return pl.pallas_call(body, grid=(H, S // bq), in_specs=specs, out_specs=(qs, kv, kv), out_shape=(sds, sds, sds),
                          compiler_params=_params(("parallel", "arbitrary")), name=name)(*ins)


def loss_head(name, y, tgt, tm=512):
    S, D = y.shape
    tm = _tile(S, tm)

    def body(y_ref, t_ref, dy_ref, acc_ref):
        @pl.when(pl.program_id(0) == 0)
        def _():
            acc_ref[...] = jnp.zeros_like(acc_ref)
        e = y_ref[...] - t_ref[...]
        dy_ref[...] = e * (1.0 / D)
        acc_ref[...] += jnp.sum(e * e)

    row = pl.BlockSpec((tm, D), lambda i: (i, 0))
    return pl.pallas_call(body, grid=(S // tm,), in_specs=[row, row],
                          out_specs=(row, pl.BlockSpec((8, LANES), lambda i: (0, 0))),
                          out_shape=(jax.ShapeDtypeStruct((S, D), F32), jax.ShapeDtypeStruct((8, LANES), F32)),
                          compiler_params=_params(("arbitrary",)), name=name)(y, tgt)


def _heads(a):
    S, D = a.shape
    return jnp.transpose(a.reshape(S, D // HEAD_DIM, HEAD_DIM), (1, 0, 2))


def _unheads(a):
    H, S, dh = a.shape
    return jnp.transpose(a, (1, 0, 2)).reshape(S, H * dh)


def _to_sm(a, axis=-1):
    axis = axis % a.ndim
    shp = a.shape[:axis] + (N_CHIPS, a.shape[axis] // N_CHIPS) + a.shape[axis + 1:]
    return jnp.moveaxis(a.reshape(shp), axis, 0)


def _from_sm(a, axis=-1):
    nd = a.ndim - 1
    axis = axis % nd
    b = jnp.moveaxis(a, 0, axis)
    return b.reshape(b.shape[:axis] + (b.shape[axis] * b.shape[axis + 1],) + b.shape[axis + 2:])


def forward_backward(x, p4, tgt, W):
    S, D = x.shape
    scale = 1.0 / math.sqrt(HEAD_DIM)
    saved = []
    kh = vh = x_kv = None
    for i in range(DEPTH):
        sv = {'x': x}
        if i < N_A:
            h_sm = mm_colsm(f"pw1_{i}", x, W['a_pw1_w'], i, W['a_pw1_b'])
            u = glu_fwd(f"glu_{i}", h_sm)
            c, s = conv_ln_silu_fwd(f"convln_{i}", u, W['a_dw_w'], W['a_dw_b'], W['a_ln_g'], W['a_ln_b'], i)
            mix = mm_rowsm(f"pw2_{i}", s, W['a_pw2_w'], i, bias=W['a_pw2_b'])
            sv.update(h_sm=h_sm, u=u, c=c, s=s)
        else:
            j = i - N_A
            if kh is None:
                x_kv = x
                kh = _heads(mm_rowsm("wk", x, W['kv_wk'], 0, out_dtype=BF16))
                vh = _heads(mm_rowsm("wv", x, W['kv_wv'], 0, out_dtype=BF16))
            qh = _heads(mm_rowsm(f"wq_{j}", x, W['b_wq'], j, out_dtype=BF16, out_scale=scale))
            oh, tot = attn_fwd(f"attn_{j}", qh, kh, vh)
            o = _unheads(oh)
            mix = mm_rowsm(f"wo_{j}", o, W['b_wo'], j)
            sv.update(qh=qh, o=o, tot=tot)
        x1, xh1, rs1 = ln_fwd(f"lnmix_{i}", x, mix, W['ln_mix_g'], W['ln_mix_b'], i)
        up_sm = mm_colsm(f"up_{i}", x1, W['ffn_w_up'], i)
        gp_sm = mm_colsm(f"gate_{i}", x1, W['ffn_w_gate'], i)
        hf_sm = ffn_gate_fwd(f"ffngate_{i}", up_sm, gp_sm, W['ffn_conv_w'], W['ffn_conv_b'], i)
        ffn = mm_rowsm(f"down_{i}", hf_sm, W['ffn_w_down'], i, a_sm=True)
        tg = mm_rowsm(f"plegate_{i}", x1, W['ple_w_gate'], i)
        pp = mm_proj(f"pleproj_{i}", p4, i, W['ple_w_proj'])
        x2, xh2, rs2 = ln_fwd(f"lnffn_{i}", x1, ffn, W['ln_ffn_g'], W['ln_ffn_b'], i, tg=tg, pp=pp)
        sv.update(x1=x1, xh1=xh1, rs1=rs1, up_sm=up_sm, gp_sm=gp_sm, hf_sm=hf_sm, tg=tg, pp=pp, xh2=xh2, rs2=rs2)
        saved.append(sv)
        x = x2

    dx, lacc = loss_head("loss", x, tgt)
    loss_sum = lacc[0, 0]

    G = {n: [None] * DEPTH for n in WEIGHTS}
    dkh = dvh = None
    for i in reversed(range(DEPTH)):
        sv = saved[i]
        dr, acc, dtg, dpp = ln_bwd(f"lnffn_b_{i}", dx, sv['xh2'], sv['rs2'], W['ln_ffn_g'], i, tg=sv['tg'], pp=sv['pp'])
        G['ln_ffn_g'][i], G['ln_ffn_b'][i] = acc[0], acc[1]
        G['ple_w_proj'][i] = mm_tn_proj(f"dproj_{i}", p4, i, dpp)
        G['ple_w_gate'][i] = mm_tn_rowsm(f"dplegate_{i}", sv['x1'], dtg)
        G['ffn_w_down'][i] = mm_tn_rowsm(f"ddown_{i}", sv['hf_sm'], dr, a_sm=True)
        dhf_sm = mm_nt_rowsm(f"dhf_{i}", dr, W['ffn_w_down'], i, out_sm=True)
        dup_sm, dgc_sm = ffn_gate_bwd_a(f"ffngate_ba_{i}", dhf_sm, sv['up_sm'], sv['gp_sm'], W['ffn_conv_w'], W['ffn_conv_b'], i)
        dgp_sm, cacc = ffn_gate_bwd_b(f"ffngate_bb_{i}", dgc_sm, sv['gp_sm'], W['ffn_conv_w'], i)
        kw = W['ffn_conv_w'].shape[2]
        G['ffn_conv_w'][i] = cacc[:, 0:kw, :]
        G['ffn_conv_b'][i] = cacc[:, 7, :].reshape(-1)
        G['ffn_w_up'][i] = mm_tn_colsm(f"dup_{i}", sv['x1'], dup_sm)
        G['ffn_w_gate'][i] = mm_tn_colsm(f"dgate_{i}", sv['x1'], dgp_sm)
        dx1 = mm_nt_rowsm(f"dx1a_{i}", dtg, W['ple_w_gate'], i, res=dr, res_alpha=DN_ALPHA)
        dx1 = mm_nt_colsm(f"dx1b_{i}", dup_sm, W['ffn_w_up'], i, res=dx1)
        dx1 = mm_nt_colsm(f"dx1c_{i}", dgp_sm, W['ffn_w_gate'], i, res=dx1)

        dr1, acc1 = ln_bwd(f"lnmix_b_{i}", dx1, sv['xh1'], sv['rs1'], W['ln_mix_g'], i)
        G['ln_mix_g'][i], G['ln_mix_b'][i] = acc1[0], acc1[1]
        xin = sv['x']
        if i < N_A:
            G['a_pw2_b'][i] = acc1[2]
            G['a_pw2_w'][i] = mm_tn_rowsm(f"dpw2_{i}", sv['s'], dr1)
            ds = mm_nt_rowsm(f"ds_{i}", dr1, W['a_pw2_w'], i)
            dc, cacc = ln_silu_bwd(f"lnsilu_b_{i}", ds, sv['c'], W['a_ln_g'], W['a_ln_b'], i)
            G['a_ln_g'][i], G['a_ln_b'][i], G['a_dw_b'][i] = cacc[0], cacc[1], cacc[2]
            da, dg, dw, dba, dbg = conv_glu_bwd(f"convglu_b_{i}", dc, sv['u'], sv['h_sm'], W['a_dw_w'], i)
            kw = W['a_dw_w'].shape[1]
            G['a_dw_w'][i] = _from_sm(dw[:, 0:kw, :], axis=-1)
            dh_sm = jnp.concatenate([da, dg], axis=0)
            G['a_pw1_b'][i] = jnp.concatenate([dba[:, 0, :], dbg[:, 0, :]], axis=0)
            G['a_pw1_w'][i] = mm_tn_colsm(f"dpw1_{i}", xin, dh_sm)
            dx = mm_nt_colsm(f"dxa_{i}", dh_sm, W['a_pw1_w'], i, res=dr1, res_alpha=DN_ALPHA)
        else:
            j = i - N_A
            G['b_wo'][j] = mm_tn_rowsm(f"dwo_{j}", sv['o'], dr1)
            do = mm_nt_rowsm(f"do_{j}", dr1, W['b_wo'], j)
            dqh, dkh, dvh = attn_bwd(f"attn_b_{j}", sv['qh'], kh, vh, _heads(do).astype(BF16), sv['tot'], dkh, dvh)
            dq = _unheads(dqh)
            G['b_wq'][j] = mm_tn_rowsm(f"dwq_{j}", xin, dq)
            dx = mm_nt_rowsm(f"dxq_{j}", dq, W['b_wq'], j, res=dr1, res_alpha=DN_ALPHA)
            if j == 0:
                dk, dv = _unheads(dkh), _unheads(dvh)
                G['kv_wk'][0] = mm_tn_rowsm("dwk", x_kv, dk)
                G['kv_wv'][0] = mm_tn_rowsm("dwv", x_kv, dv)
                dx = mm_nt_rowsm("dxk", dk, W['kv_wk'], 0, res=dx)
                dx = mm_nt_rowsm("dxv", dv, W['kv_wv'], 0, res=dx)
    return loss_sum, dx, G


MESH = pl.DeviceIdType.MESH
HBM = pl.BlockSpec(memory_space=pltpu.HBM)


def _place():
    x, y, c = lax.axis_index("x"), lax.axis_index("y"), lax.axis_index("c")
    others = [(1 - x, y), (x, 1 - y), (1 - x, 1 - y)]
    return x, y, c, others


def allgather_chips(name, arrs):
    n = len(arrs)

    def body(*refs):
        ins, outs = refs[:n], refs[n:2 * n]
        send_sems, recv_sems, local_sems = refs[2 * n:]
        x, y, c, others = _place()
        me = 2 * x + y
        sibling = (x, y, 1 - c)

        def remote(a, k, src, chip_id, half, to):
            return pltpu.make_async_remote_copy(src_ref=src, dst_ref=outs[a].at[chip_id, half], send_sem=send_sems.at[a, k],
                                                recv_sem=recv_sems.at[a, k], device_id=to, device_id_type=MESH)

        own = [pltpu.make_async_copy(ins[a], outs[a].at[me], local_sems.at[a]) for a in range(n)]
        for cp in own:
            cp.start()
        first = [remote(a, k, ins[a].at[c], me, c, (ch[0], ch[1], c)) for a in range(n) for k, ch in enumerate(others)]
        for cp in first:
            cp.start()
        passed = []
        for a in range(n):
            for k, ch in enumerate(others):
                cid = 2 * ch[0] + ch[1]
                remote(a, k, ins[a].at[c], cid, c, sibling).wait_recv()
                fwd = remote(a, 3 + k, outs[a].at[cid, c], cid, c, sibling)
                fwd.start()
                passed.append(fwd)
        for a in range(n):
            for k, ch in enumerate(others):
                cid = 2 * ch[0] + ch[1]
                remote(a, 3 + k, ins[a].at[c], cid, 1 - c, sibling).wait_recv()
        for cp in first + passed:
            cp.wait_send()
        for cp in own:
            cp.wait()

    return pl.pallas_call(body, out_shape=tuple(jax.ShapeDtypeStruct((N_CHIPS,) + a.shape, a.dtype) for a in arrs),
                          in_specs=[HBM] * n, out_specs=tuple([HBM] * n),
                          scratch_shapes=[pltpu.SemaphoreType.DMA((n, 6)), pltpu.SemaphoreType.DMA((n, 6)),
                                          pltpu.SemaphoreType.DMA((n,))],
                          name=name)(*arrs)


def exchange_sibling(name, g):
    _, _, R, L = g.shape

    def body(g_ref, o_ref, send_sems, recv_sems):
        x, y, c, _ = _place()
        cps = [pltpu.make_async_remote_copy(src_ref=g_ref.at[j, 1 - c], dst_ref=o_ref.at[j], send_sem=send_sems.at[j],
                                            recv_sem=recv_sems.at[j], device_id=(x, y, 1 - c), device_id_type=MESH)
               for j in range(N_CHIPS)]
        for cp in cps:
            cp.start()
        for cp in cps:
            cp.wait()

    return pl.pallas_call(body, out_shape=jax.ShapeDtypeStruct((N_CHIPS, R, L), g.dtype), in_specs=[HBM], out_specs=HBM,
                          scratch_shapes=[pltpu.SemaphoreType.DMA((N_CHIPS,)), pltpu.SemaphoreType.DMA((N_CHIPS,))],
                          name=name)(g)


def scatter_chips(name, s):
    _, R, L = s.shape

    def body(s_ref, o_ref, send_sems, recv_sems):
        x, y, c, others = _place()
        cps = [pltpu.make_async_remote_copy(src_ref=s_ref.at[2 * ch[0] + ch[1]], dst_ref=o_ref.at[k], send_sem=send_sems.at[k],
                                            recv_sem=recv_sems.at[k], device_id=(ch[0], ch[1], c), device_id_type=MESH)
               for k, ch in enumerate(others)]
        for cp in cps:
            cp.start()
        for cp in cps:
            cp.wait()

    return pl.pallas_call(body, out_shape=jax.ShapeDtypeStruct((3, R, L), s.dtype), in_specs=[HBM], out_specs=HBM,
                          scratch_shapes=[pltpu.SemaphoreType.DMA((3,)), pltpu.SemaphoreType.DMA((3,))], name=name)(s)


def share_sibling(name, t):
    R, L = t.shape

    def body(t_ref, o_ref, send_sem, recv_sem, local_sem):
        x, y, c, _ = _place()
        own = pltpu.make_async_copy(t_ref, o_ref.at[c], local_sem)
        own.start()
        cp = pltpu.make_async_remote_copy(src_ref=t_ref, dst_ref=o_ref.at[c], send_sem=send_sem, recv_sem=recv_sem,
                                          device_id=(x, y, 1 - c), device_id_type=MESH)
        cp.start()
        pltpu.make_async_remote_copy(src_ref=t_ref, dst_ref=o_ref.at[1 - c], send_sem=send_sem, recv_sem=recv_sem,
                                     device_id=(x, y, 1 - c), device_id_type=MESH).wait_recv()
        cp.wait_send()
        own.wait()

    return pl.pallas_call(body, out_shape=jax.ShapeDtypeStruct((2, R, L), t.dtype), in_specs=[HBM], out_specs=HBM,
                          scratch_shapes=[pltpu.SemaphoreType.DMA, pltpu.SemaphoreType.DMA, pltpu.SemaphoreType.DMA],
                          name=name)(t)


def add_halves(name, g, recv, c_arr, tr=512):
    _, _, R, L = g.shape
    tr = _tile(R, tr)

    def body(c_ref, a_ref, b_ref, o_ref):
        o_ref[...] = a_ref[...] + b_ref[...]

    blk = pl.BlockSpec((None, tr, L), lambda j, i, c: (j, i, 0))
    gs = pltpu.PrefetchScalarGridSpec(num_scalar_prefetch=1, grid=(N_CHIPS, R // tr),
                                      in_specs=[pl.BlockSpec((None, None, tr, L), lambda j, i, c: (j, c[0], i, 0)), blk],
                                      out_specs=blk)
    return pl.pallas_call(body, grid_spec=gs, out_shape=jax.ShapeDtypeStruct((N_CHIPS, R, L), F32),
                          compiler_params=_params(("parallel", "parallel")), name=name)(c_arr, g, recv)


def add_chips(name, s, recv, me_arr, tr=512):
    _, R, L = s.shape
    tr = _tile(R, tr)

    def body(me_ref, a_ref, b_ref, o_ref):
        o_ref[...] = ((a_ref[...] + b_ref[0]) + b_ref[1]) + b_ref[2]

    gs = pltpu.PrefetchScalarGridSpec(num_scalar_prefetch=1, grid=(R // tr,),
                                      in_specs=[pl.BlockSpec((None, tr, L), lambda i, me: (me[0], i, 0)),
                                                pl.BlockSpec((3, tr, L), lambda i, me: (0, i, 0))],
                                      out_specs=pl.BlockSpec((tr, L), lambda i, me: (i, 0)))
    return pl.pallas_call(body, grid_spec=gs, out_shape=jax.ShapeDtypeStruct((R, L), F32),
                          compiler_params=_params(("parallel",)), name=name)(me_arr, s, recv)


def reduce_scatter(g, c_arr, me_arr):
    r1 = exchange_sibling("rs_sibling", g)
    s1 = add_halves("rs_add_cores", g, r1, c_arr)
    r2 = scatter_chips("rs_chips", s1)
    tot = add_chips("rs_add_chips", s1, r2, me_arr)
    return share_sibling("rs_share", tot)


def adamw(name, w, g, m, v, tr=512):
    shp = w.shape
    cols = shp[-1]
    w2, g2, m2, v2 = (a.reshape(-1, cols) for a in (w, g, m, v))
    rows = w2.shape[0]
    tr = _tile(rows, tr)

    def body(w_ref, g_ref, m_ref, v_ref, d_ref, mo_ref, vo_ref):
        g_ = g_ref[...]
        m_ = ADAM_B1 * m_ref[...] + (1.0 - ADAM_B1) * g_
        v_ = ADAM_B2 * v_ref[...] + (1.0 - ADAM_B2) * (g_ * g_)
        m_hat = m_ / (1.0 - ADAM_B1 ** ADAM_STEP)
        v_hat = v_ / (1.0 - ADAM_B2 ** ADAM_STEP)
        d_ref[...] = -ADAM_LR * (m_hat / (jnp.sqrt(v_hat) + ADAM_EPS) + ADAM_WD * w_ref[...])
        mo_ref[...] = m_
        vo_ref[...] = v_

    blk = pl.BlockSpec((tr, cols), lambda i: (i, 0))
    sds = jax.ShapeDtypeStruct((rows, cols), F32)
    d, mo, vo = pl.pallas_call(body, grid=(rows // tr,), in_specs=[blk] * 4, out_specs=(blk, blk, blk), out_shape=(sds, sds, sds),
                               compiler_params=_params(("parallel",)), name=name)(w2, g2, m2, v2)
    return d.reshape(shp), mo.reshape(shp), vo.reshape(shp)


PACK_ALIGN = 1024


def _pad_to(a, mult, axis=-1):
    axis = axis % a.ndim
    extra = (-a.shape[axis]) % mult
    if extra == 0:
        return a
    pads = [(0, 0)] * a.ndim
    pads[axis] = (0, extra)
    return jnp.pad(a, pads)


def _pack(pieces, lead, row_mult):
    nl = len(lead)
    flat, offs, sizes, off = [], [], [], 0
    for a in pieces:
        f = a.reshape(lead + (-1,))
        sizes.append(f.shape[-1])
        f = _pad_to(f, PACK_ALIGN)
        offs.append(off)
        off += f.shape[-1]
        flat.append(f)
    cat = _pad_to(jnp.concatenate(flat, axis=nl), 2 * row_mult * LANES)
    return cat.reshape(lead + (2, -1, LANES)), offs, sizes


def _unpack(packed, lead, offs, sizes, shapes):
    flat = packed.reshape(lead + (-1,))
    return [lax.slice_in_dim(flat, o, o + s, axis=len(lead)).reshape(lead + tuple(shp)) for o, s, shp in zip(offs, sizes, shapes)]


def _stack_grads(G, D):
    out = {}
    for n in WEIGHTS:
        parts = [g for g in G[n] if g is not None]
        if n in ('kv_wk', 'kv_wv'):
            out[n] = parts[0]
        elif n in REPLICATED:
            out[n] = jnp.stack(parts, axis=0).reshape(N_CHIPS, -1)
        elif n in ('a_dw_w', 'a_dw_b', 'a_ln_g', 'a_ln_b', 'a_pw2_b'):
            out[n] = _to_sm(jnp.stack(parts, axis=0), axis=-1)
        else:
            out[n] = jnp.stack(parts, axis=1)
    return out


def _whole_weights(big, small, rep, D):
    W = {}
    for n in BIG:
        a = big[n]
        W[n] = a[:, None] if n in ('kv_wk', 'kv_wv') else a
    W['a_pw1_b'] = small['a_pw1_b'][:, :, None, :]
    W['a_dw_w'] = _from_sm(small['a_dw_w'], axis=-1)
    for n in ('a_dw_b', 'a_ln_g', 'a_ln_b', 'a_pw2_b'):
        W[n] = _from_sm(small[n], axis=-1)[:, None, :]
    W['ffn_conv_w'] = small['ffn_conv_w']
    L, F = rep['ffn_conv_b'].shape
    W['ffn_conv_b'] = rep['ffn_conv_b'].reshape(L, N_CHIPS, 1, F // N_CHIPS)
    for n in ('ln_mix_g', 'ln_mix_b', 'ln_ffn_g', 'ln_ffn_b'):
        W[n] = rep[n][:, None, :]
    return W


SMALL = ('a_pw1_b', 'a_dw_w', 'a_dw_b', 'a_ln_g', 'a_ln_b', 'a_pw2_b', 'ffn_conv_w')


def _step(x, p, loss_target, w, m, v):
    S, D = x.shape[-2:]
    x2, tgt = x.reshape(S, D), loss_target.reshape(S, D)
    c_arr = lax.axis_index("c").astype(jnp.int32).reshape(1)
    me_arr = (2 * lax.axis_index("x") + lax.axis_index("y")).astype(jnp.int32).reshape(1)

    big_in = [w[n].astype(BF16).reshape((2, -1) + w[n].shape[1:] if w[n].ndim == 3 else (2, -1, w[n].shape[-1])) for n in BIG]
    small_in, s_offs, s_sizes = _pack([w[n] for n in SMALL], (), 8)
    gathered = allgather_chips("gather_weights", big_in + [small_in])
    big = {n: g.reshape((N_CHIPS,) + w[n].shape) for n, g in zip(BIG, gathered[:-1])}
    small = dict(zip(SMALL, _unpack(gathered[-1], (N_CHIPS,), s_offs, s_sizes, [w[n].shape for n in SMALL])))
    W = _whole_weights(big, small, {n: w[n] for n in REPLICATED}, D)

    loss_sum, dx, G = forward_backward(x2, p, tgt, W)
    loss = lax.psum(0.5 * loss_sum / D, ("x", "y", "c"))

    g_sm = _stack_grads(G, D)
    packed, offs, sizes = _pack([g_sm[n] for n in WEIGHTS], (N_CHIPS,), 512)
    reduced = reduce_scatter(packed, c_arr, me_arr)
    shapes = [w[n].shape if n not in REPLICATED else (w[n].size // N_CHIPS,) for n in WEIGHTS]
    g_mine = dict(zip(WEIGHTS, _unpack(reduced, (), offs, sizes, shapes)))
    rep_in, r_offs, r_sizes = _pack([g_mine[n] for n in REPLICATED], (), 8)
    rep_all = allgather_chips("gather_replicated_grads", [rep_in])[0]
    for n, g in zip(REPLICATED, _unpack(rep_all, (N_CHIPS,), r_offs, r_sizes, [(w[n].size // N_CHIPS,) for n in REPLICATED])):
        g_mine[n] = g.reshape(w[n].shape)

    grads, deltas, new_m, new_v = [], [], [], []
    for n in WEIGHTS:
        d, mo, vo = adamw(f"adamw_{n}", w[n], g_mine[n], m[n], v[n])
        grads.append(g_mine[n])
        deltas.append(d)
        new_m.append(mo)
        new_v.append(vo)
    return (loss, dx.reshape(x.shape), *grads, *deltas, *new_m, *new_v)


def kernel(x, p, a_pw1_w, a_pw1_b, a_dw_w, a_dw_b, a_ln_g, a_ln_b, a_pw2_w, a_pw2_b, b_wq, kv_wk, kv_wv, b_wo, ln_mix_g, ln_mix_b, ffn_w_up, ffn_w_gate, ffn_conv_w, ffn_conv_b, ffn_w_down, ple_w_gate, ple_w_proj, ln_ffn_g, ln_ffn_b, loss_target, m_a_pw1_w, m_a_pw1_b, m_a_dw_w, m_a_dw_b, m_a_ln_g, m_a_ln_b, m_a_pw2_w, m_a_pw2_b, m_b_wq, m_kv_wk, m_kv_wv, m_b_wo, m_ln_mix_g, m_ln_mix_b, m_ffn_w_up, m_ffn_w_gate, m_ffn_conv_w, m_ffn_conv_b, m_ffn_w_down, m_ple_w_gate, m_ple_w_proj, m_ln_ffn_g, m_ln_ffn_b, v_a_pw1_w, v_a_pw1_b, v_a_dw_w, v_a_dw_b, v_a_ln_g, v_a_ln_b, v_a_pw2_w, v_a_pw2_b, v_b_wq, v_kv_wk, v_kv_wv, v_b_wo, v_ln_mix_g, v_ln_mix_b, v_ffn_w_up, v_ffn_w_gate, v_ffn_conv_w, v_ffn_conv_b, v_ffn_w_down, v_ple_w_gate, v_ple_w_proj, v_ln_ffn_g, v_ln_ffn_b):
    vals = dict(locals())
    w = {n: vals[n] for n in WEIGHTS}
    m = {n: vals["m_" + n] for n in WEIGHTS}
    v = {n: vals["v_" + n] for n in WEIGHTS}
    return _step(x, p, loss_target, w, m, v)
```

```python
import functools
import math

import jax
import jax.numpy as jnp
import numpy as np
from jax import lax
from jax.experimental import pallas as pl
from jax.experimental.pallas import tpu as pltpu

F32, BF16 = jnp.float32, jnp.bfloat16

HEAD_DIM = 64
LN_EPS = 1e-5
DEPTH = 4
N_A = DEPTH // 2
DN_ALPHA = (2.0 * DEPTH) ** 0.25
N_CHIPS = 4

ADAM_LR, ADAM_B1, ADAM_B2, ADAM_EPS, ADAM_WD, ADAM_STEP = 0.001, 0.9, 0.999, 1e-08, 0.01, 10

VMEM_LIMIT_BYTES = 56 * 2**20
LANES = 128
CONV_HALO = 32
FFN_HALO = 8

NN = (((1,), (0,)), ((), ()))
NT = (((1,), (1,)), ((), ()))
TN = (((0,), (0,)), ((), ()))

WEIGHTS = ['a_pw1_w', 'a_pw1_b', 'a_dw_w', 'a_dw_b', 'a_ln_g', 'a_ln_b', 'a_pw2_w', 'a_pw2_b', 'b_wq', 'kv_wk', 'kv_wv',
           'b_wo', 'ln_mix_g', 'ln_mix_b', 'ffn_w_up', 'ffn_w_gate', 'ffn_conv_w', 'ffn_conv_b', 'ffn_w_down', 'ple_w_gate',
           'ple_w_proj', 'ln_ffn_g', 'ln_ffn_b']
REPLICATED = ('ln_mix_g', 'ln_mix_b', 'ffn_conv_b', 'ln_ffn_g', 'ln_ffn_b')
BIG = ('a_pw1_w', 'a_pw2_w', 'b_wq', 'kv_wk', 'kv_wv', 'b_wo', 'ffn_w_up', 'ffn_w_gate', 'ffn_w_down', 'ple_w_gate',
       'ple_w_proj')


def _tile(n, pref, mult=8):
    t = min(n, pref)
    while t > 0:
        if n % t == 0 and t % mult == 0:
            return t
        t -= 1
    return n


def _params(sem):
    return pltpu.CompilerParams(dimension_semantics=sem, vmem_limit_bytes=VMEM_LIMIT_BYTES)


def _sigmoid(x):
    return 1.0 / (1.0 + jnp.exp(-x))


def _mm_call(name, grid, terms, out_spec, out_sds, dims, bias=None, res=None, res_alpha=1.0, out_scale=None):
    n_terms = len(terms)

    def body(*refs):
        o_ref = refs[-1]
        acc = None
        for t in range(n_terms):
            a = refs[2 * t][...].astype(BF16)
            b = refs[2 * t + 1][...].astype(BF16)
            d = lax.dot_general(a, b, dims, preferred_element_type=F32)
            acc = d if acc is None else acc + d
        k = 2 * n_terms
        if bias is not None:
            acc = acc + refs[k][...]
            k += 1
        if res is not None:
            acc = acc + res_alpha * refs[k][...]
        if out_scale is not None:
            acc = acc * out_scale
        o_ref[...] = acc.astype(o_ref.dtype)

    operands, specs = [], []
    for a, a_spec, b, b_spec in terms:
        operands += [a, b]
        specs += [a_spec, b_spec]
    for extra in (bias, res):
        if extra is not None:
            operands.append(extra[0])
            specs.append(extra[1])
    return pl.pallas_call(body, out_shape=out_sds, grid=grid, in_specs=specs, out_specs=out_spec,
                          compiler_params=_params(("parallel",) * len(grid)), name=name)(*operands)


def mm_colsm(name, x, w4, layer, bias4=None, tm=512):
    M, K = x.shape
    n = w4.shape[-1]
    tm = _tile(M, tm)
    terms = [(x, pl.BlockSpec((tm, K), lambda j, i: (i, 0)), w4, pl.BlockSpec((None, None, K, n), lambda j, i: (j, layer, 0, 0)))]
    bias = None if bias4 is None else (bias4, pl.BlockSpec((None, None, 1, n), lambda j, i: (j, layer, 0, 0)))
    return _mm_call(name, (N_CHIPS, M // tm), terms, pl.BlockSpec((None, tm, n), lambda j, i: (j, i, 0)),
                    jax.ShapeDtypeStruct((N_CHIPS, M, n), F32), NN, bias=bias)


def mm_rowsm(name, a, w4, layer, a_sm=False, bias=None, out_dtype=F32, out_scale=None, tm=512):
    kc, N = w4.shape[-2:]
    M = a.shape[-2]
    tm = _tile(M, tm)
    terms = []
    for j in range(N_CHIPS):
        if a_sm:
            a_spec = pl.BlockSpec((None, tm, kc), lambda i, j=j: (j, i, 0))
        else:
            a_spec = pl.BlockSpec((tm, kc), lambda i, j=j: (i, j))
        terms.append((a, a_spec, w4, pl.BlockSpec((None, None, kc, N), lambda i, j=j: (j, layer, 0, 0))))
    b = None if bias is None else (bias, pl.BlockSpec((None, 1, N), lambda i: (layer, 0, 0)))
    return _mm_call(name, (M // tm,), terms, pl.BlockSpec((tm, N), lambda i: (i, 0)), jax.ShapeDtypeStruct((M, N), out_dtype),
                    NN, bias=b, out_scale=out_scale)


def mm_nt_rowsm(name, dy, w4, layer, out_sm=False, res=None, res_alpha=1.0, tm=512):
    kc, N = w4.shape[-2:]
    M = dy.shape[0]
    tm = _tile(M, tm)
    terms = [(dy, pl.BlockSpec((tm, N), lambda j, i: (i, 0)), w4, pl.BlockSpec((None, None, kc, N), lambda j, i: (j, layer, 0, 0)))]
    if out_sm:
        out_spec, sds = pl.BlockSpec((None, tm, kc), lambda j, i: (j, i, 0)), jax.ShapeDtypeStruct((N_CHIPS, M, kc), F32)
    else:
        out_spec, sds = pl.BlockSpec((tm, kc), lambda j, i: (i, j)), jax.ShapeDtypeStruct((M, N_CHIPS * kc), F32)
    r = None if res is None else (res, pl.BlockSpec((tm, kc), lambda j, i: (i, j)))
    return _mm_call(name, (N_CHIPS, M // tm), terms, out_spec, sds, NT, res=r, res_alpha=res_alpha)


def mm_nt_colsm(name, dy_sm, w4, layer, res=None, res_alpha=1.0, tm=512):
    K, n = w4.shape[-2:]
    M = dy_sm.shape[1]
    tm = _tile(M, tm)
    terms = [(dy_sm, pl.BlockSpec((None, tm, n), lambda i, j=j: (j, i, 0)), w4,
              pl.BlockSpec((None, None, K, n), lambda i, j=j: (j, layer, 0, 0))) for j in range(N_CHIPS)]
    r = None if res is None else (res, pl.BlockSpec((tm, K), lambda i: (i, 0)))
    return _mm_call(name, (M // tm,), terms, pl.BlockSpec((tm, K), lambda i: (i, 0)), jax.ShapeDtypeStruct((M, K), F32), NT,
                    res=r, res_alpha=res_alpha)


def mm_tn_colsm(name, x, dy_sm, tk=512):
    M, K = x.shape
    n = dy_sm.shape[-1]
    tk = _tile(K, tk, LANES)
    terms = [(x, pl.BlockSpec((M, tk), lambda j, k: (0, k)), dy_sm, pl.BlockSpec((None, M, n), lambda j, k: (j, 0, 0)))]
    return _mm_call(name, (N_CHIPS, K // tk), terms, pl.BlockSpec((None, tk, n), lambda j, k: (j, k, 0)),
                    jax.ShapeDtypeStruct((N_CHIPS, K, n), F32), TN)


def mm_tn_rowsm(name, a, dy, a_sm=False, tn=512):
    M, N = dy.shape
    kc = a.shape[-1] if a_sm else a.shape[-1] // N_CHIPS
    tn = _tile(N, tn, LANES)
    a_spec = pl.BlockSpec((None, M, kc), lambda j, n: (j, 0, 0)) if a_sm else pl.BlockSpec((M, kc), lambda j, n: (0, j))
    terms = [(a, a_spec, dy, pl.BlockSpec((M, tn), lambda j, n: (0, n)))]
    return _mm_call(name, (N_CHIPS, N // tn), terms, pl.BlockSpec((None, kc, tn), lambda j, n: (j, 0, n)),
                    jax.ShapeDtypeStruct((N_CHIPS, kc, N), F32), TN)


def mm_proj(name, p4, layer, w4, tm=512):
    S, P = p4.shape[-2:]
    n = w4.shape[-1]
    tm = _tile(S, tm)
    terms = [(p4, pl.BlockSpec((None, None, tm, P), lambda j, i: (layer, 0, i, 0)), w4,
              pl.BlockSpec((None, None, P, n), lambda j, i: (j, layer, 0, 0)))]
    return _mm_call(name, (N_CHIPS, S // tm), terms, pl.BlockSpec((tm, n), lambda j, i: (i, j)),
                    jax.ShapeDtypeStruct((S, N_CHIPS * n), F32), NN)


def mm_tn_proj(name, p4, layer, dpp):
    S, P = p4.shape[-2:]
    n = dpp.shape[-1] // N_CHIPS
    terms = [(p4, pl.BlockSpec((None, None, S, P), lambda j: (layer, 0, 0, 0)), dpp, pl.BlockSpec((S, n), lambda j: (0, j)))]
    return _mm_call(name, (N_CHIPS,), terms, pl.BlockSpec((None, P, n), lambda j: (j, 0, 0)),
                    jax.ShapeDtypeStruct((N_CHIPS, P, n), F32), TN)


def ln_fwd(name, x, mix, g, b, layer, tg=None, pp=None, tm=256):
    S, D = x.shape
    tm = _tile(S, tm)
    ple = tg is not None

    def body(*refs):
        if ple:
            x_ref, m_ref, tg_ref, pp_ref, g_ref, b_ref, y_ref, xh_ref, rs_ref = refs
        else:
            x_ref, m_ref, g_ref, b_ref, y_ref, xh_ref, rs_ref = refs
        r = DN_ALPHA * x_ref[...] + m_ref[...]
        if ple:
            r = r + _sigmoid(tg_ref[...]) * pp_ref[...]
        mu = jnp.mean(r, axis=-1, keepdims=True)
        d = r - mu
        var = jnp.mean(d * d, axis=-1, keepdims=True)
        rstd = lax.rsqrt(var + LN_EPS)
        xh = d * rstd
        y_ref[...] = xh * g_ref[...] + b_ref[...]
        xh_ref[...] = xh
        rs_ref[...] = rstd

    row = pl.BlockSpec((tm, D), lambda i: (i, 0))
    vec = pl.BlockSpec((None, 1, D), lambda i: (layer, 0, 0))
    ins = [x, mix] + ([tg, pp] if ple else []) + [g, b]
    specs = [row, row] + ([row, row] if ple else []) + [vec, vec]
    return pl.pallas_call(body, grid=(S // tm,), in_specs=specs,
                          out_specs=(row, row, pl.BlockSpec((tm, 1), lambda i: (i, 0))),
                          out_shape=(jax.ShapeDtypeStruct((S, D), F32), jax.ShapeDtypeStruct((S, D), F32),
                                     jax.ShapeDtypeStruct((S, 1), F32)),
                          compiler_params=_params(("parallel",)), name=name)(*ins)


def ln_bwd(name, dy, xh, rstd, g, layer, tg=None, pp=None, tm=256):
    S, D = dy.shape
    tm = _tile(S, tm)
    ple = tg is not None

    def body(*refs):
        if ple:
            dy_ref, xh_ref, rs_ref, g_ref, tg_ref, pp_ref, dr_ref, acc_ref, dtg_ref, dpp_ref = refs
        else:
            dy_ref, xh_ref, rs_ref, g_ref, dr_ref, acc_ref = refs
        dy_, xh_ = dy_ref[...], xh_ref[...]
        dxh = dy_ * g_ref[...]
        m1 = jnp.mean(dxh, axis=-1, keepdims=True)
        m2 = jnp.mean(dxh * xh_, axis=-1, keepdims=True)
        dr = rs_ref[...] * (dxh - m1 - xh_ * m2)
        dr_ref[...] = dr

        @pl.when(pl.program_id(0) == 0)
        def _():
            acc_ref[...] = jnp.zeros_like(acc_ref)
        acc_ref[0:1, :] += jnp.sum(dy_ * xh_, axis=0, keepdims=True)
        acc_ref[1:2, :] += jnp.sum(dy_, axis=0, keepdims=True)
        acc_ref[2:3, :] += jnp.sum(dr, axis=0, keepdims=True)
        if ple:
            pg = _sigmoid(tg_ref[...])
            dtg_ref[...] = (dr * pp_ref[...] * pg * (1.0 - pg)).astype(BF16)
            dpp_ref[...] = (dr * pg).astype(BF16)

    row = pl.BlockSpec((tm, D), lambda i: (i, 0))
    ins = [dy, xh, rstd, g] + ([tg, pp] if ple else [])
    specs = [row, row, pl.BlockSpec((tm, 1), lambda i: (i, 0)), pl.BlockSpec((None, 1, D), lambda i: (layer, 0, 0))] + ([row, row] if ple else [])
    outs = [jax.ShapeDtypeStruct((S, D), F32), jax.ShapeDtypeStruct((8, D), F32)]
    out_specs = [row, pl.BlockSpec((8, D), lambda i: (0, 0))]
    if ple:
        outs += [jax.ShapeDtypeStruct((S, D), BF16)] * 2
        out_specs += [row, row]
    return pl.pallas_call(body, grid=(S // tm,), in_specs=specs, out_specs=tuple(out_specs), out_shape=tuple(outs),
                          compiler_params=_params(("arbitrary",)), name=name)(*ins)


def glu_fwd(name, h_sm, tm=512):
    _, S, n = h_sm.shape
    tm = _tile(S, tm)
    half = N_CHIPS // 2

    def body(a_ref, g_ref, u_ref):
        u_ref[...] = a_ref[...] * _sigmoid(g_ref[...])

    return pl.pallas_call(body, grid=(half, S // tm),
                          in_specs=[pl.BlockSpec((None, tm, n), lambda j, i: (j, i, 0)),
                                    pl.BlockSpec((None, tm, n), lambda j, i: (j + half, i, 0))],
                          out_specs=pl.BlockSpec((tm, n), lambda j, i: (i, j)),
                          out_shape=jax.ShapeDtypeStruct((S, half * n), F32),
                          compiler_params=_params(("parallel", "parallel")), name=name)(h_sm, h_sm)


def conv_ln_silu_fwd(name, u, w, b, g, beta, layer, ts=128):
    S, D = u.shape
    kw = w.shape[1]
    ts = _tile(S, ts, CONV_HALO)
    lc = LANES if D % LANES == 0 else D

    def body(h_ref, u_ref, w_ref, b_ref, g_ref, be_ref, c_ref, s_ref, win_ref):
        i = pl.program_id(0)
        win_ref[0:CONV_HALO, :] = jnp.where(i == 0, 0.0, h_ref[...])
        win_ref[CONV_HALO:, :] = u_ref[...]
        for cc in range(D // lc):
            cs = slice(cc * lc, (cc + 1) * lc)
            acc = jnp.zeros((ts, lc), F32) + b_ref[:, cs]
            for k in range(kw):
                off = CONV_HALO - (kw - 1) + k
                acc = acc + w_ref[k:k + 1, cs] * win_ref[off:off + ts, cs]
            c_ref[:, cs] = acc
        c = c_ref[...]
        mu = jnp.mean(c, axis=-1, keepdims=True)
        d = c - mu
        var = jnp.mean(d * d, axis=-1, keepdims=True)
        nrm = d * lax.rsqrt(var + LN_EPS) * g_ref[...] + be_ref[...]
        s_ref[...] = (nrm * _sigmoid(nrm)).astype(BF16)

    row = pl.BlockSpec((ts, D), lambda i: (i, 0))
    vec = pl.BlockSpec((None, 1, D), lambda i: (layer, 0, 0))
    halo = pl.BlockSpec((CONV_HALO, D), lambda i: (jnp.maximum(i * (ts // CONV_HALO) - 1, 0), 0))
    return pl.pallas_call(body, grid=(S // ts,),
                          in_specs=[halo, row, pl.BlockSpec((None, kw, D), lambda i: (layer, 0, 0)), vec, vec, vec],
                          out_specs=(row, row),
                          out_shape=(jax.ShapeDtypeStruct((S, D), F32), jax.ShapeDtypeStruct((S, D), BF16)),
                          scratch_shapes=[pltpu.VMEM((ts + CONV_HALO, D), F32)],
                          compiler_params=_params(("parallel",)), name=name)(u, u, w, b, g, beta)


def ln_silu_bwd(name, ds, c, g, beta, layer, tm=256):
    S, D = c.shape
    tm = _tile(S, tm)

    def body(ds_ref, c_ref, g_ref, be_ref, dc_ref, acc_ref):
        c_ = c_ref[...]
        mu = jnp.mean(c_, axis=-1, keepdims=True)
        d = c_ - mu
        var = jnp.mean(d * d, axis=-1, keepdims=True)
        rstd = lax.rsqrt(var + LN_EPS)
        xh = d * rstd
        nrm = xh * g_ref[...] + be_ref[...]
        sg = _sigmoid(nrm)
        dn = ds_ref[...] * (sg * (1.0 + nrm * (1.0 - sg)))
        dxh = dn * g_ref[...]
        m1 = jnp.mean(dxh, axis=-1, keepdims=True)
        m2 = jnp.mean(dxh * xh, axis=-1, keepdims=True)
        dc = rstd * (dxh - m1 - xh * m2)
        dc_ref[...] = dc

        @pl.when(pl.program_id(0) == 0)
        def _():
            acc_ref[...] = jnp.zeros_like(acc_ref)
        acc_ref[0:1, :] += jnp.sum(dn * xh, axis=0, keepdims=True)
        acc_ref[1:2, :] += jnp.sum(dn, axis=0, keepdims=True)
        acc_ref[2:3, :] += jnp.sum(dc, axis=0, keepdims=True)

    row = pl.BlockSpec((tm, D), lambda i: (i, 0))
    vec = pl.BlockSpec((None, 1, D), lambda i: (layer, 0, 0))
    return pl.pallas_call(body, grid=(S // tm,), in_specs=[row, row, vec, vec],
                          out_specs=(row, pl.BlockSpec((8, D), lambda i: (0, 0))),
                          out_shape=(jax.ShapeDtypeStruct((S, D), F32), jax.ShapeDtypeStruct((8, D), F32)),
                          compiler_params=_params(("arbitrary",)), name=name)(ds, c, g, beta)


def conv_glu_bwd(name, dc, u, h_sm, w, layer, ts=128):
    S, D = dc.shape
    kw = w.shape[1]
    half = N_CHIPS // 2
    n = D // half
    ts = _tile(S, ts, CONV_HALO)
    nblk = S // ts
    lc = LANES if n % LANES == 0 else n

    def body(dc_ref, dcn_ref, u_ref, up_ref, a_ref, g_ref, w_ref, da_ref, dg_ref, dw_ref, dba_ref, dbg_ref, dwin_ref, uwin_ref):
        i = pl.program_id(1)
        dwin_ref[0:ts, :] = dc_ref[...]
        dwin_ref[ts:, :] = jnp.where(i == nblk - 1, 0.0, dcn_ref[...])
        uwin_ref[0:CONV_HALO, :] = jnp.where(i == 0, 0.0, up_ref[...])
        uwin_ref[CONV_HALO:, :] = u_ref[...]

        @pl.when(i == 0)
        def _():
            dw_ref[...] = jnp.zeros_like(dw_ref)
            dba_ref[...] = jnp.zeros_like(dba_ref)
            dbg_ref[...] = jnp.zeros_like(dbg_ref)

        for cc in range(n // lc):
            cs = slice(cc * lc, (cc + 1) * lc)
            dcb = dwin_ref[0:ts, cs]
            du = jnp.zeros((ts, lc), F32)
            for k in range(kw):
                du = du + w_ref[k:k + 1, cs] * dwin_ref[kw - 1 - k:kw - 1 - k + ts, cs]
                off = CONV_HALO - (kw - 1) + k
                dw_ref[k:k + 1, cs] += jnp.sum(dcb * uwin_ref[off:off + ts, cs], axis=0, keepdims=True)
            a = a_ref[:, cs]
            sg = _sigmoid(g_ref[:, cs])
            da = du * sg
            dg = du * a * sg * (1.0 - sg)
            da_ref[:, cs] = da.astype(BF16)
            dg_ref[:, cs] = dg.astype(BF16)
            dba_ref[0:1, cs] += jnp.sum(da, axis=0, keepdims=True)
            dbg_ref[0:1, cs] += jnp.sum(dg, axis=0, keepdims=True)

    r = ts // CONV_HALO
    main = pl.BlockSpec((ts, n), lambda j, i: (i, j))
    nxt = pl.BlockSpec((CONV_HALO, n), lambda j, i: (jnp.minimum((i + 1) * r, S // CONV_HALO - 1), j))
    prv = pl.BlockSpec((CONV_HALO, n), lambda j, i: (jnp.maximum(i * r - 1, 0), j))
    sm_a = pl.BlockSpec((None, ts, n), lambda j, i: (j, i, 0))
    sm_g = pl.BlockSpec((None, ts, n), lambda j, i: (j + half, i, 0))
    acc8a = pl.BlockSpec((None, 8, n), lambda j, i: (j, 0, 0))
    acc8g = pl.BlockSpec((None, 8, n), lambda j, i: (j + half, 0, 0))
    da, dg, dw, dba, dbg = pl.pallas_call(
        body, grid=(half, nblk),
        in_specs=[main, nxt, main, prv, sm_a, sm_g, pl.BlockSpec((None, kw, n), lambda j, i: (layer, 0, j))],
        out_specs=(pl.BlockSpec((None, ts, n), lambda j, i: (j, i, 0)), pl.BlockSpec((None, ts, n), lambda j, i: (j, i, 0)),
                   pl.BlockSpec((None, 32, n), lambda j, i: (j, 0, 0)),
                   pl.BlockSpec((None, 8, n), lambda j, i: (j, 0, 0)), pl.BlockSpec((None, 8, n), lambda j, i: (j, 0, 0))),
        out_shape=(jax.ShapeDtypeStruct((half, S, n), BF16), jax.ShapeDtypeStruct((half, S, n), BF16),
                   jax.ShapeDtypeStruct((half, 32, n), F32),
                   jax.ShapeDtypeStruct((half, 8, n), F32), jax.ShapeDtypeStruct((half, 8, n), F32)),
        scratch_shapes=[pltpu.VMEM((ts + CONV_HALO, n), F32), pltpu.VMEM((ts + CONV_HALO, n), F32)],
        compiler_params=_params(("parallel", "arbitrary")), name=name)(dc, dc, u, u, h_sm, h_sm, w)
    del acc8a, acc8g
    return da, dg, dw, dba, dbg


def _ffn_gc(win_ref, w_ref, b_ref, ts, kw, base):
    gc = b_ref[...] + jnp.zeros((ts, win_ref.shape[1]), F32)
    for k in range(kw):
        off = base - (kw - 1) + k
        gc = gc + w_ref[k:k + 1, :] * win_ref[off:off + ts, :]
    return gc


def ffn_gate_fwd(name, up_sm, gp_sm, w4, b4, layer, ts=256):
    _, S, n = up_sm.shape
    kw = w4.shape[2]
    ts = _tile(S, ts, FFN_HALO)

    def body(up_ref, gp_ref, gph_ref, w_ref, b_ref, hf_ref, win_ref):
        i = pl.program_id(1)
        win_ref[0:FFN_HALO, :] = jnp.where(i == 0, 0.0, gph_ref[...])
        win_ref[FFN_HALO:, :] = gp_ref[...]
        gc = _ffn_gc(win_ref, w_ref, b_ref, ts, kw, FFN_HALO)
        hf_ref[...] = (gc * _sigmoid(gc) * up_ref[...]).astype(BF16)

    main = pl.BlockSpec((None, ts, n), lambda j, i: (j, i, 0))
    prv = pl.BlockSpec((None, FFN_HALO, n), lambda j, i: (j, jnp.maximum(i * (ts // FFN_HALO) - 1, 0), 0))
    return pl.pallas_call(body, grid=(N_CHIPS, S // ts),
                          in_specs=[main, main, prv, pl.BlockSpec((None, None, kw, n), lambda j, i: (j, layer, 0, 0)),
                                    pl.BlockSpec((None, None, 1, n), lambda j, i: (layer, j, 0, 0))],
                          out_specs=main, out_shape=jax.ShapeDtypeStruct((N_CHIPS, S, n), BF16),
                          scratch_shapes=[pltpu.VMEM((ts + FFN_HALO, n), F32)],
                          compiler_params=_params(("parallel", "parallel")), name=name)(up_sm, gp_sm, gp_sm, w4, b4)


def ffn_gate_bwd_a(name, dhf_sm, up_sm, gp_sm, w4, b4, layer, ts=256):
    _, S, n = up_sm.shape
    kw = w4.shape[2]
    ts = _tile(S, ts, FFN_HALO)

    def body(dhf_ref, up_ref, gp_ref, gph_ref, w_ref, b_ref, dup_ref, dgc_ref, win_ref):
        i = pl.program_id(1)
        win_ref[0:FFN_HALO, :] = jnp.where(i == 0, 0.0, gph_ref[...])
        win_ref[FFN_HALO:, :] = gp_ref[...]
        gc = _ffn_gc(win_ref, w_ref, b_ref, ts, kw, FFN_HALO)
        sg = _sigmoid(gc)
        dhf = dhf_ref[...]
        dup_ref[...] = (dhf * gc * sg).astype(BF16)
        dgc_ref[...] = dhf * up_ref[...] * (sg * (1.0 + gc * (1.0 - sg)))

    main = pl.BlockSpec((None, ts, n), lambda j, i: (j, i, 0))
    prv = pl.BlockSpec((None, FFN_HALO, n), lambda j, i: (j, jnp.maximum(i * (ts // FFN_HALO) - 1, 0), 0))
    return pl.pallas_call(body, grid=(N_CHIPS, S // ts),
                          in_specs=[main, main, main, prv, pl.BlockSpec((None, None, kw, n), lambda j, i: (j, layer, 0, 0)),
                                    pl.BlockSpec((None, None, 1, n), lambda j, i: (layer, j, 0, 0))],
                          out_specs=(main, main),
                          out_shape=(jax.ShapeDtypeStruct((N_CHIPS, S, n), BF16), jax.ShapeDtypeStruct((N_CHIPS, S, n), F32)),
                          scratch_shapes=[pltpu.VMEM((ts + FFN_HALO, n), F32)],
                          compiler_params=_params(("parallel", "parallel")), name=name)(dhf_sm, up_sm, gp_sm, gp_sm, w4, b4)


def ffn_gate_bwd_b(name, dgc_sm, gp_sm, w4, layer, ts=256):
    _, S, n = gp_sm.shape
    kw = w4.shape[2]
    ts = _tile(S, ts, FFN_HALO)
    nblk = S // ts

    def body(dgc_ref, dgn_ref, gp_ref, gph_ref, w_ref, dgp_ref, acc_ref, dwin_ref, gwin_ref):
        i = pl.program_id(1)
        dgc = dgc_ref[...]
        dwin_ref[0:ts, :] = dgc
        dwin_ref[ts:, :] = jnp.where(i == nblk - 1, 0.0, dgn_ref[...])
        gwin_ref[0:FFN_HALO, :] = jnp.where(i == 0, 0.0, gph_ref[...])
        gwin_ref[FFN_HALO:, :] = gp_ref[...]

        @pl.when(i == 0)
        def _():
            acc_ref[...] = jnp.zeros_like(acc_ref)
        dgp = jnp.zeros((ts, n), F32)
        for k in range(kw):
            dgp = dgp + w_ref[k:k + 1, :] * dwin_ref[kw - 1 - k:kw - 1 - k + ts, :]
            off = FFN_HALO - (kw - 1) + k
            acc_ref[k:k + 1, :] += jnp.sum(dgc * gwin_ref[off:off + ts, :], axis=0, keepdims=True)
        acc_ref[7:8, :] += jnp.sum(dgc, axis=0, keepdims=True)
        dgp_ref[...] = dgp.astype(BF16)

    r = ts // FFN_HALO
    main = pl.BlockSpec((None, ts, n), lambda j, i: (j, i, 0))
    nxt = pl.BlockSpec((None, FFN_HALO, n), lambda j, i: (j, jnp.minimum((i + 1) * r, S // FFN_HALO - 1), 0))
    prv = pl.BlockSpec((None, FFN_HALO, n), lambda j, i: (j, jnp.maximum(i * r - 1, 0), 0))
    return pl.pallas_call(body, grid=(N_CHIPS, nblk),
                          in_specs=[main, nxt, main, prv, pl.BlockSpec((None, None, kw, n), lambda j, i: (j, layer, 0, 0))],
                          out_specs=(main, pl.BlockSpec((None, 8, n), lambda j, i: (j, 0, 0))),
                          out_shape=(jax.ShapeDtypeStruct((N_CHIPS, S, n), BF16), jax.ShapeDtypeStruct((N_CHIPS, 8, n), F32)),
                          scratch_shapes=[pltpu.VMEM((ts + FFN_HALO, n), F32), pltpu.VMEM((ts + FFN_HALO, n), F32)],
                          compiler_params=_params(("parallel", "arbitrary")), name=name)(dgc_sm, dgc_sm, gp_sm, gp_sm, w4)


def _neg_softplus(z):
    e = jnp.exp(-jnp.abs(z))
    return -(jnp.maximum(z, 0.0) + jnp.log(1.0 + e)), e


def _split_dot(x, t):
    hi = x.astype(BF16)
    lo = (x - hi.astype(F32)).astype(BF16)
    return jnp.dot(hi, t, preferred_element_type=F32) + jnp.dot(lo, t, preferred_element_type=F32)


STICK_GONE = -100.0
NOT_SWEPT = -1e30


def attn_fwd(name, q, k, v, bq=512, w=256):
    S, D = q.shape
    dh = HEAD_DIM
    hpb = LANES // dh
    bq = _tile(S, bq)
    w = _tile(bq, w)
    nsub = bq // w
    nkb = S // w

    def body(q_ref, k_ref, v_ref, o_ref, runs_ref, rs_ref):
        qi = pl.program_id(1)
        rr = lax.broadcasted_iota(jnp.int32, (w, w), 0)
        cc = lax.broadcasted_iota(jnp.int32, (w, w), 1)
        t_suf = (rr >= cc).astype(BF16)
        tq = qi * bq + lax.broadcasted_iota(jnp.int32, (bq, w), 0)
        tk = lax.broadcasted_iota(jnp.int32, (bq, w), 1)
        lane = lax.broadcasted_iota(jnp.int32, (bq, LANES), 1)
        ntot = (qi + 1) * nsub
        for hh in range(hpb):
            hs = slice(hh * dh, (hh + 1) * dh)
            qb = q_ref[:, hs]
            rs_ref[...] = jnp.where(lane < ntot, NOT_SWEPT, 0.0)

            def block(kb, run, acc, masked):
                kstart = pl.multiple_of(kb * w, w)
                kblk = k_ref[pl.ds(kstart, w), hs]
                vblk = v_ref[pl.ds(kstart, w), hs]
                z = lax.dot_general(qb, kblk, NT, preferred_element_type=F32)
                lg, _ = _neg_softplus(z)
                if masked:
                    m = (tk + kstart) < tq
                    lg = jnp.where(m, lg, 0.0)
                cum = _split_dot(lg, t_suf) + run
                a = jnp.exp(z + cum)
                if masked:
                    a = jnp.where(m, a, 0.0)
                acc = acc + jnp.dot(a.astype(BF16), vblk, preferred_element_type=F32)
                run = cum[:, 0:1]
                rs_ref[...] = jnp.where(lane == kb, run, rs_ref[...])
                return run, acc

            run = jnp.zeros((bq, 1), F32)
            acc = jnp.zeros((bq, dh), F32)
            for sb in reversed(range(nsub)):
                run, acc = block(qi * nsub + sb, run, acc, True)

            def cond(c):
                return jnp.logical_and(c[0] >= 0, jnp.max(c[1]) > STICK_GONE)

            def step(c):
                run, acc = block(c[0], c[1], c[2], False)
                return c[0] - 1, run, acc
            _, run, acc = lax.while_loop(cond, step, (qi * nsub - 1, run, acc))
            o_ref[:, hs] = acc
            runs_ref[hh] = rs_ref[:, 0:nkb]

    qs = pl.BlockSpec((bq, LANES), lambda h, i: (i, h))
    kv = pl.BlockSpec((S, LANES), lambda h, i: (0, h))
    return pl.pallas_call(body, grid=(D // LANES, S // bq), in_specs=[qs, kv, kv],
                          out_specs=(qs, pl.BlockSpec((hpb, bq, nkb), lambda h, i: (h, i, 0))),
                          out_shape=(jax.ShapeDtypeStruct((S, D), F32), jax.ShapeDtypeStruct((D // dh, S, nkb), F32)),
                          scratch_shapes=[pltpu.VMEM((bq, LANES), F32)],
                          compiler_params=_params(("parallel", "parallel")), name=name)(q, k, v)


def attn_bwd(name, q, k, v, do, runs, dk0=None, dv0=None, bq=512, w=256):
    S, D = q.shape
    dh = HEAD_DIM
    hpb = LANES // dh
    bq = _tile(S, bq)
    w = _tile(bq, w)
    nsub = bq // w
    nkb = S // w
    scale = 1.0 / math.sqrt(dh)
    init = dk0 is not None

    def body(*refs):
        if init:
            q_ref, k_ref, v_ref, do_ref, runs_ref, dk0_ref, dv0_ref, dq_ref, dk_ref, dv_ref, rs_ref = refs
        else:
            q_ref, k_ref, v_ref, do_ref, runs_ref, dq_ref, dk_ref, dv_ref, rs_ref = refs
        qi = pl.program_id(1)

        @pl.when(qi == 0)
        def _():
            dk_ref[...] = dk0_ref[...] if init else jnp.zeros_like(dk_ref)
            dv_ref[...] = dv0_ref[...] if init else jnp.zeros_like(dv_ref)

        rr = lax.broadcasted_iota(jnp.int32, (w, w), 0)
        cc = lax.broadcasted_iota(jnp.int32, (w, w), 1)
        t_suf = (rr >= cc).astype(BF16)
        t_pre = (rr <= cc).astype(BF16)
        tq = qi * bq + lax.broadcasted_iota(jnp.int32, (bq, w), 0)
        tk = lax.broadcasted_iota(jnp.int32, (bq, w), 1)
        lane = lax.broadcasted_iota(jnp.int32, (bq, LANES), 1)
        lane1 = lax.broadcasted_iota(jnp.int32, (1, LANES), 1)
        ntot = (qi + 1) * nsub
        for hh in range(hpb):
            hs = slice(hh * dh, (hh + 1) * dh)
            qb = q_ref[:, hs]
            dob = do_ref[:, hs].astype(BF16)
            rs_ref[...] = jnp.zeros_like(rs_ref)
            rs_ref[:, 0:nkb] = runs_ref[hh]
            colmax = jnp.max(rs_ref[...], axis=0, keepdims=True)
            dead = jnp.logical_and(jnp.logical_and(lane1 >= 1, lane1 <= ntot), colmax <= STICK_GONE)
            kb0 = jnp.minimum(jnp.sum(dead.astype(jnp.int32)), ntot - nsub)

            def block(kb, pg_run, dq, masked):
                kstart = pl.multiple_of(kb * w, w)
                kblk = k_ref[pl.ds(kstart, w), hs]
                vblk = v_ref[pl.ds(kstart, w), hs]
                right = jnp.sum(jnp.where(lane == kb + 1, rs_ref[...], 0.0), axis=1, keepdims=True)
                z = lax.dot_general(qb, kblk, NT, preferred_element_type=F32)
                lg, e = _neg_softplus(z)
                if masked:
                    m = (tk + kstart) < tq
                    lg = jnp.where(m, lg, 0.0)
                a = jnp.exp(z + _split_dot(lg, t_suf) + right)
                if masked:
                    a = jnp.where(m, a, 0.0)
                da = lax.dot_general(dob, vblk, NT, preferred_element_type=F32)
                g = da * a
                pin = _split_dot(g, t_pre) + pg_run
                sig = jnp.where(z >= 0.0, 1.0, e) / (1.0 + e)
                dz = g - sig * pin
                if masked:
                    dz = jnp.where(m, dz, 0.0)
                dzb = dz.astype(BF16)
                dq = dq + jnp.dot(dzb, kblk, preferred_element_type=F32)
                dk_ref[pl.ds(kstart, w), hs] += lax.dot_general(dzb, qb, TN, preferred_element_type=F32)
                dv_ref[pl.ds(kstart, w), hs] += lax.dot_general(a.astype(BF16), dob, TN, preferred_element_type=F32)
                return pin[:, w - 1:w], dq

            def loop(kb, carry):
                return block(kb, carry[0], carry[1], False)
            pg_run, dq = lax.fori_loop(kb0, qi * nsub, loop, (jnp.zeros((bq, 1), F32), jnp.zeros((bq, dh), F32)))
            for sb in range(nsub):
                pg_run, dq = block(qi * nsub + sb, pg_run, dq, True)
            dq_ref[:, hs] = dq * scale

    qs = pl.BlockSpec((bq, LANES), lambda h, i: (i, h))
    kv = pl.BlockSpec((S, LANES), lambda h, i: (0, h))
    ins = [q, k, v, do, runs] + ([dk0, dv0] if init else [])
    specs = [qs, kv, kv, qs, pl.BlockSpec((hpb, bq, nkb), lambda h, i: (h, i, 0))] + ([kv, kv] if init else [])
    sds = jax.ShapeDtypeStruct((S, D), F32)
    return pl.pallas_call(body, grid=(D // LANES, S // bq), in_specs=specs, out_specs=(qs, kv, kv), out_shape=(sds, sds, sds),
                          scratch_shapes=[pltpu.VMEM((bq, LANES), F32)],
                          compiler_params=_params(("parallel", "arbitrary")), name=name)(*ins)


def loss_head(name, y, tgt, tm=512):
    S, D = y.shape
    tm = _tile(S, tm)

    def body(y_ref, t_ref, dy_ref, acc_ref):
        @pl.when(pl.program_id(0) == 0)
        def _():
            acc_ref[...] = jnp.zeros_like(acc_ref)
        e = y_ref[...] - t_ref[...]
        dy_ref[...] = e * (1.0 / D)
        acc_ref[...] += jnp.sum(e * e)

    row = pl.BlockSpec((tm, D), lambda i: (i, 0))
    return pl.pallas_call(body, grid=(S // tm,), in_specs=[row, row],
                          out_specs=(row, pl.BlockSpec((8, LANES), lambda i: (0, 0))),
                          out_shape=(jax.ShapeDtypeStruct((S, D), F32), jax.ShapeDtypeStruct((8, LANES), F32)),
                          compiler_params=_params(("arbitrary",)), name=name)(y, tgt)


def _to_sm(a, axis=-1):
    axis = axis % a.ndim
    shp = a.shape[:axis] + (N_CHIPS, a.shape[axis] // N_CHIPS) + a.shape[axis + 1:]
    return jnp.moveaxis(a.reshape(shp), axis, 0)


def _from_sm(a, axis=-1):
    nd = a.ndim - 1
    axis = axis % nd
    b = jnp.moveaxis(a, 0, axis)
    return b.reshape(b.shape[:axis] + (b.shape[axis] * b.shape[axis + 1],) + b.shape[axis + 2:])


def forward_backward(x, p4, tgt, W):
    S, D = x.shape
    scale = 1.0 / math.sqrt(HEAD_DIM)
    saved = []
    kh = vh = x_kv = None
    for i in range(DEPTH):
        sv = {'x': x}
        if i < N_A:
            h_sm = mm_colsm(f"pw1_{i}", x, W['a_pw1_w'], i, W['a_pw1_b'])
            u = glu_fwd(f"glu_{i}", h_sm)
            c, s = conv_ln_silu_fwd(f"convln_{i}", u, W['a_dw_w'], W['a_dw_b'], W['a_ln_g'], W['a_ln_b'], i)
            mix = mm_rowsm(f"pw2_{i}", s, W['a_pw2_w'], i, bias=W['a_pw2_b'])
            sv.update(h_sm=h_sm, u=u, c=c, s=s)
        else:
            j = i - N_A
            if kh is None:
                x_kv = x
                kh = mm_rowsm("wk", x, W['kv_wk'], 0, out_dtype=BF16)
                vh = mm_rowsm("wv", x, W['kv_wv'], 0, out_dtype=BF16)
            qh = mm_rowsm(f"wq_{j}", x, W['b_wq'], j, out_dtype=BF16, out_scale=scale)
            o, runs = attn_fwd(f"attn_{j}", qh, kh, vh)
            mix = mm_rowsm(f"wo_{j}", o, W['b_wo'], j)
            sv.update(qh=qh, o=o, runs=runs)
        x1, xh1, rs1 = ln_fwd(f"lnmix_{i}", x, mix, W['ln_mix_g'], W['ln_mix_b'], i)
        up_sm = mm_colsm(f"up_{i}", x1, W['ffn_w_up'], i)
        gp_sm = mm_colsm(f"gate_{i}", x1, W['ffn_w_gate'], i)
        hf_sm = ffn_gate_fwd(f"ffngate_{i}", up_sm, gp_sm, W['ffn_conv_w'], W['ffn_conv_b'], i)
        ffn = mm_rowsm(f"down_{i}", hf_sm, W['ffn_w_down'], i, a_sm=True)
        tg = mm_rowsm(f"plegate_{i}", x1, W['ple_w_gate'], i)
        pp = mm_proj(f"pleproj_{i}", p4, i, W['ple_w_proj'])
        x2, xh2, rs2 = ln_fwd(f"lnffn_{i}", x1, ffn, W['ln_ffn_g'], W['ln_ffn_b'], i, tg=tg, pp=pp)
        sv.update(x1=x1, xh1=xh1, rs1=rs1, up_sm=up_sm, gp_sm=gp_sm, hf_sm=hf_sm, tg=tg, pp=pp, xh2=xh2, rs2=rs2)
        saved.append(sv)
        x = x2

    dx, lacc = loss_head("loss", x, tgt)
    loss_sum = lacc[0, 0]

    G = {n: [None] * DEPTH for n in WEIGHTS}
    dk = dv = None
    for i in reversed(range(DEPTH)):
        sv = saved[i]
        dr, acc, dtg, dpp = ln_bwd(f"lnffn_b_{i}", dx, sv['xh2'], sv['rs2'], W['ln_ffn_g'], i, tg=sv['tg'], pp=sv['pp'])
        G['ln_ffn_g'][i], G['ln_ffn_b'][i] = acc[0], acc[1]
        G['ple_w_proj'][i] = mm_tn_proj(f"dproj_{i}", p4, i, dpp)
        G['ple_w_gate'][i] = mm_tn_rowsm(f"dplegate_{i}", sv['x1'], dtg)
        G['ffn_w_down'][i] = mm_tn_rowsm(f"ddown_{i}", sv['hf_sm'], dr, a_sm=True)
        dhf_sm = mm_nt_rowsm(f"dhf_{i}", dr, W['ffn_w_down'], i, out_sm=True)
        dup_sm, dgc_sm = ffn_gate_bwd_a(f"ffngate_ba_{i}", dhf_sm, sv['up_sm'], sv['gp_sm'], W['ffn_conv_w'], W['ffn_conv_b'], i)
        dgp_sm, cacc = ffn_gate_bwd_b(f"ffngate_bb_{i}", dgc_sm, sv['gp_sm'], W['ffn_conv_w'], i)
        kw = W['ffn_conv_w'].shape[2]
        G['ffn_conv_w'][i] = cacc[:, 0:kw, :]
        G['ffn_conv_b'][i] = cacc[:, 7, :].reshape(-1)
        G['ffn_w_up'][i] = mm_tn_colsm(f"dup_{i}", sv['x1'], dup_sm)
        G['ffn_w_gate'][i] = mm_tn_colsm(f"dgate_{i}", sv['x1'], dgp_sm)
        dx1 = mm_nt_rowsm(f"dx1a_{i}", dtg, W['ple_w_gate'], i, res=dr, res_alpha=DN_ALPHA)
        dx1 = mm_nt_colsm(f"dx1b_{i}", dup_sm, W['ffn_w_up'], i, res=dx1)
        dx1 = mm_nt_colsm(f"dx1c_{i}", dgp_sm, W['ffn_w_gate'], i, res=dx1)

        dr1, acc1 = ln_bwd(f"lnmix_b_{i}", dx1, sv['xh1'], sv['rs1'], W['ln_mix_g'], i)
        G['ln_mix_g'][i], G['ln_mix_b'][i] = acc1[0], acc1[1]
        xin = sv['x']
        if i < N_A:
            G['a_pw2_b'][i] = acc1[2]
            G['a_pw2_w'][i] = mm_tn_rowsm(f"dpw2_{i}", sv['s'], dr1)
            ds = mm_nt_rowsm(f"ds_{i}", dr1, W['a_pw2_w'], i)
            dc, cacc = ln_silu_bwd(f"lnsilu_b_{i}", ds, sv['c'], W['a_ln_g'], W['a_ln_b'], i)
            G['a_ln_g'][i], G['a_ln_b'][i], G['a_dw_b'][i] = cacc[0], cacc[1], cacc[2]
            da, dg, dw, dba, dbg = conv_glu_bwd(f"convglu_b_{i}", dc, sv['u'], sv['h_sm'], W['a_dw_w'], i)
            kw = W['a_dw_w'].shape[1]
            G['a_dw_w'][i] = _from_sm(dw[:, 0:kw, :], axis=-1)
            dh_sm = jnp.concatenate([da, dg], axis=0)
            G['a_pw1_b'][i] = jnp.concatenate([dba[:, 0, :], dbg[:, 0, :]], axis=0)
            G['a_pw1_w'][i] = mm_tn_colsm(f"dpw1_{i}", xin, dh_sm)
            dx = mm_nt_colsm(f"dxa_{i}", dh_sm, W['a_pw1_w'], i, res=dr1, res_alpha=DN_ALPHA)
        else:
            j = i - N_A
            G['b_wo'][j] = mm_tn_rowsm(f"dwo_{j}", sv['o'], dr1)
            do = mm_nt_rowsm(f"do_{j}", dr1, W['b_wo'], j)
            dq, dk, dv = attn_bwd(f"attn_b_{j}", sv['qh'], kh, vh, do, sv['runs'], dk, dv)
            G['b_wq'][j] = mm_tn_rowsm(f"dwq_{j}", xin, dq)
            dx = mm_nt_rowsm(f"dxq_{j}", dq, W['b_wq'], j, res=dr1, res_alpha=DN_ALPHA)
            if j == 0:
                G['kv_wk'][0] = mm_tn_rowsm("dwk", x_kv, dk)
                G['kv_wv'][0] = mm_tn_rowsm("dwv", x_kv, dv)
                dx = mm_nt_rowsm("dxk", dk, W['kv_wk'], 0, res=dx)
                dx = mm_nt_rowsm("dxv", dv, W['kv_wv'], 0, res=dx)
    return loss_sum, dx, G


MESH = pl.DeviceIdType.MESH
HBM = pl.BlockSpec(memory_space=pltpu.HBM)


def _place():
    x, y, c = lax.axis_index("x"), lax.axis_index("y"), lax.axis_index("c")
    others = [(1 - x, y), (x, 1 - y), (1 - x, 1 - y)]
    return x, y, c, others


def allgather_chips(name, arrs):
    n = len(arrs)

    def body(*refs):
        ins, outs = refs[:n], refs[n:2 * n]
        send_sems, recv_sems, local_sems = refs[2 * n:]
        x, y, c, others = _place()
        me = 2 * x + y
        sibling = (x, y, 1 - c)

        def remote(a, k, src, chip_id, half, to):
            return pltpu.make_async_remote_copy(src_ref=src, dst_ref=outs[a].at[chip_id, half], send_sem=send_sems.at[a, k],
                                                recv_sem=recv_sems.at[a, k], device_id=to, device_id_type=MESH)

        own = [pltpu.make_async_copy(ins[a], outs[a].at[me], local_sems.at[a]) for a in range(n)]
        for cp in own:
            cp.start()
        first = [remote(a, k, ins[a].at[c], me, c, (ch[0], ch[1], c)) for a in range(n) for k, ch in enumerate(others)]
        for cp in first:
            cp.start()
        passed = []
        for a in range(n):
            for k, ch in enumerate(others):
                cid = 2 * ch[0] + ch[1]
                remote(a, k, ins[a].at[c], cid, c, sibling).wait_recv()
                fwd = remote(a, 3 + k, outs[a].at[cid, c], cid, c, sibling)
                fwd.start()
                passed.append(fwd)
        for a in range(n):
            for k, ch in enumerate(others):
                cid = 2 * ch[0] + ch[1]
                remote(a, 3 + k, ins[a].at[c], cid, 1 - c, sibling).wait_recv()
        for cp in first + passed:
            cp.wait_send()
        for cp in own:
            cp.wait()

    return pl.pallas_call(body, out_shape=tuple(jax.ShapeDtypeStruct((N_CHIPS,) + a.shape, a.dtype) for a in arrs),
                          in_specs=[HBM] * n, out_specs=tuple([HBM] * n),
                          scratch_shapes=[pltpu.SemaphoreType.DMA((n, 6)), pltpu.SemaphoreType.DMA((n, 6)),
                                          pltpu.SemaphoreType.DMA((n,))],
                          name=name)(*arrs)


def exchange_sibling(name, gs):
    n = len(gs)

    def body(*refs):
        g_refs, o_refs = refs[:n], refs[n:2 * n]
        send_sems, recv_sems = refs[2 * n:]
        x, y, c, _ = _place()
        cps = [pltpu.make_async_remote_copy(src_ref=g_refs[a].at[j, 1 - c], dst_ref=o_refs[a].at[j], send_sem=send_sems.at[a, j],
                                            recv_sem=recv_sems.at[a, j], device_id=(x, y, 1 - c), device_id_type=MESH)
               for a in range(n) for j in range(N_CHIPS)]
        for cp in cps:
            cp.start()
        for cp in cps:
            cp.wait()

    return pl.pallas_call(body, out_shape=tuple(jax.ShapeDtypeStruct((N_CHIPS,) + g.shape[2:], g.dtype) for g in gs),
                          in_specs=[HBM] * n, out_specs=tuple([HBM] * n),
                          scratch_shapes=[pltpu.SemaphoreType.DMA((n, N_CHIPS)), pltpu.SemaphoreType.DMA((n, N_CHIPS))],
                          name=name)(*gs)


def scatter_chips(name, ss):
    n = len(ss)

    def body(*refs):
        s_refs, o_refs = refs[:n], refs[n:2 * n]
        send_sems, recv_sems = refs[2 * n:]
        x, y, c, others = _place()
        cps = [pltpu.make_async_remote_copy(src_ref=s_refs[a].at[2 * ch[0] + ch[1]], dst_ref=o_refs[a].at[k],
                                            send_sem=send_sems.at[a, k], recv_sem=recv_sems.at[a, k],
                                            device_id=(ch[0], ch[1], c), device_id_type=MESH)
               for a in range(n) for k, ch in enumerate(others)]
        for cp in cps:
            cp.start()
        for cp in cps:
            cp.wait()

    return pl.pallas_call(body, out_shape=tuple(jax.ShapeDtypeStruct((3,) + s.shape[1:], s.dtype) for s in ss),
                          in_specs=[HBM] * n, out_specs=tuple([HBM] * n),
                          scratch_shapes=[pltpu.SemaphoreType.DMA((n, 3)), pltpu.SemaphoreType.DMA((n, 3))], name=name)(*ss)


def share_sibling(name, ts):
    n = len(ts)

    def body(*refs):
        t_refs, o_refs = refs[:n], refs[n:2 * n]
        send_sems, recv_sems, local_sems = refs[2 * n:]
        x, y, c, _ = _place()
        own = [pltpu.make_async_copy(t_refs[a], o_refs[a].at[c], local_sems.at[a]) for a in range(n)]
        cps = [pltpu.make_async_remote_copy(src_ref=t_refs[a], dst_ref=o_refs[a].at[c], send_sem=send_sems.at[a],
                                            recv_sem=recv_sems.at[a], device_id=(x, y, 1 - c), device_id_type=MESH)
               for a in range(n)]
        for cp in own + cps:
            cp.start()
        for a in range(n):
            pltpu.make_async_remote_copy(src_ref=t_refs[a], dst_ref=o_refs[a].at[1 - c], send_sem=send_sems.at[a],
                                         recv_sem=recv_sems.at[a], device_id=(x, y, 1 - c), device_id_type=MESH).wait_recv()
        for cp in cps:
            cp.wait_send()
        for cp in own:
            cp.wait()

    return pl.pallas_call(body, out_shape=tuple(jax.ShapeDtypeStruct((2,) + t.shape, t.dtype) for t in ts),
                          in_specs=[HBM] * n, out_specs=tuple([HBM] * n),
                          scratch_shapes=[pltpu.SemaphoreType.DMA((n,)), pltpu.SemaphoreType.DMA((n,)), pltpu.SemaphoreType.DMA((n,))],
                          name=name)(*ts)


def add_halves(name, g, recv, place, out_dtype, tr=512):
    _, _, R, C = g.shape
    tr = _tile(R, tr, 16)

    def body(p_ref, a_ref, b_ref, o_ref):
        o_ref[...] = (a_ref[...] + b_ref[...]).astype(o_ref.dtype)

    blk = pl.BlockSpec((None, tr, C), lambda j, i, p: (j, i, 0))
    gs = pltpu.PrefetchScalarGridSpec(num_scalar_prefetch=1, grid=(N_CHIPS, R // tr),
                                      in_specs=[pl.BlockSpec((None, None, tr, C), lambda j, i, p: (j, p[0], i, 0)), blk],
                                      out_specs=blk)
    return pl.pallas_call(body, grid_spec=gs, out_shape=jax.ShapeDtypeStruct((N_CHIPS, R, C), out_dtype),
                          compiler_params=_params(("parallel", "parallel")), name=name)(place, g, recv)


def add_chips(name, g, r1, r2, place, tr=512):
    _, _, R, C = g.shape
    tr = _tile(R, tr, 16)

    def body(p_ref, a_ref, b_ref, c_ref, o_ref):
        s = a_ref[...] + b_ref[...]
        o_ref[...] = ((s + c_ref[0].astype(F32)) + c_ref[1].astype(F32)) + c_ref[2].astype(F32)

    gs = pltpu.PrefetchScalarGridSpec(num_scalar_prefetch=1, grid=(R // tr,),
                                      in_specs=[pl.BlockSpec((None, None, tr, C), lambda i, p: (p[1], p[0], i, 0)),
                                                pl.BlockSpec((None, tr, C), lambda i, p: (p[1], i, 0)),
                                                pl.BlockSpec((3, tr, C), lambda i, p: (0, i, 0))],
                                      out_specs=pl.BlockSpec((tr, C), lambda i, p: (i, 0)))
    return pl.pallas_call(body, grid_spec=gs, out_shape=jax.ShapeDtypeStruct((R, C), F32),
                          compiler_params=_params(("parallel",)), name=name)(place, g, r1, r2)


def reduce_scatter(gs, wire_dtypes, place):
    r1 = exchange_sibling("rs_sibling", gs)
    s1 = [add_halves(f"rs_add_cores_{a}", g, r, place, dt) for a, (g, r, dt) in enumerate(zip(gs, r1, wire_dtypes))]
    r2 = scatter_chips("rs_chips", s1)
    tot = [add_chips(f"rs_add_chips_{a}", g, r, rr, place) for a, (g, r, rr) in enumerate(zip(gs, r1, r2))]
    return share_sibling("rs_share", tot)


def adamw(name, w, g, m, v, tr=512):
    shp = w.shape
    cols = shp[-1]
    w2, g2, m2, v2 = (a.reshape(-1, cols) for a in (w, g, m, v))
    rows = w2.shape[0]
    tr = _tile(rows, tr)

    def body(w_ref, g_ref, m_ref, v_ref, d_ref, mo_ref, vo_ref):
        g_ = g_ref[...]
        m_ = ADAM_B1 * m_ref[...] + (1.0 - ADAM_B1) * g_
        v_ = ADAM_B2 * v_ref[...] + (1.0 - ADAM_B2) * (g_ * g_)
        m_hat = m_ / (1.0 - ADAM_B1 ** ADAM_STEP)
        v_hat = v_ / (1.0 - ADAM_B2 ** ADAM_STEP)
        d_ref[...] = -ADAM_LR * (m_hat / (jnp.sqrt(v_hat) + ADAM_EPS) + ADAM_WD * w_ref[...])
        mo_ref[...] = m_
        vo_ref[...] = v_

    blk = pl.BlockSpec((tr, cols), lambda i: (i, 0))
    sds = jax.ShapeDtypeStruct((rows, cols), F32)
    d, mo, vo = pl.pallas_call(body, grid=(rows // tr,), in_specs=[blk] * 4, out_specs=(blk, blk, blk), out_shape=(sds, sds, sds),
                               compiler_params=_params(("parallel",)), name=name)(w2, g2, m2, v2)
    return d.reshape(shp), mo.reshape(shp), vo.reshape(shp)


PACK_ALIGN = 1024


def _pad_to(a, mult, axis=-1):
    axis = axis % a.ndim
    extra = (-a.shape[axis]) % mult
    if extra == 0:
        return a
    pads = [(0, 0)] * a.ndim
    pads[axis] = (0, extra)
    return jnp.pad(a, pads)


def _pack(pieces, lead, row_mult):
    nl = len(lead)
    flat, offs, sizes, off = [], [], [], 0
    for a in pieces:
        f = a.reshape(lead + (-1,))
        sizes.append(f.shape[-1])
        f = _pad_to(f, PACK_ALIGN)
        offs.append(off)
        off += f.shape[-1]
        flat.append(f)
    cat = _pad_to(jnp.concatenate(flat, axis=nl), 2 * row_mult * LANES)
    return cat.reshape(lead + (2, -1, LANES)), offs, sizes


def _unpack(packed, lead, offs, sizes, shapes):
    flat = packed.reshape(lead + (-1,))
    return [lax.slice_in_dim(flat, o, o + s, axis=len(lead)).reshape(lead + tuple(shp)) for o, s, shp in zip(offs, sizes, shapes)]


def _stack_grads(G, names):
    out = {}
    for n in names:
        parts = [g for g in G[n] if g is not None]
        if n in ('kv_wk', 'kv_wv'):
            out[n] = parts[0]
        elif n in REPLICATED:
            out[n] = jnp.stack(parts, axis=0).reshape(N_CHIPS, -1)
        elif n in ('a_dw_w', 'a_dw_b', 'a_ln_g', 'a_ln_b', 'a_pw2_b'):
            out[n] = _to_sm(jnp.stack(parts, axis=0), axis=-1)
        else:
            out[n] = jnp.stack(parts, axis=1)
    return out


def _group_rows(G, names):
    groups = {}
    for n in names:
        parts = [g for g in G[n] if g is not None]
        groups.setdefault(parts[0].shape[-1], []).append((n, parts))
    arrays, members = [], []
    for cols, items in groups.items():
        pieces, where, off = [], [], 0
        for n, parts in items:
            rows = sum(q.shape[1] for q in parts)
            where.append((n, off, rows))
            off += rows
            pieces += parts
        assert off % 32 == 0, (cols, off)
        arrays.append(jnp.concatenate(pieces, axis=1).reshape(N_CHIPS, 2, off // 2, cols))
        members.append(where)
    return arrays, members


def _whole_weights(big, small, rep, D):
    W = {}
    for n in BIG:
        a = big[n]
        W[n] = a[:, None] if n in ('kv_wk', 'kv_wv') else a
    W['a_pw1_b'] = small['a_pw1_b'][:, :, None, :]
    W['a_dw_w'] = _from_sm(small['a_dw_w'], axis=-1)
    for n in ('a_dw_b', 'a_ln_g', 'a_ln_b', 'a_pw2_b'):
        W[n] = _from_sm(small[n], axis=-1)[:, None, :]
    W['ffn_conv_w'] = small['ffn_conv_w']
    L, F = rep['ffn_conv_b'].shape
    W['ffn_conv_b'] = rep['ffn_conv_b'].reshape(L, N_CHIPS, 1, F // N_CHIPS)
    for n in ('ln_mix_g', 'ln_mix_b', 'ln_ffn_g', 'ln_ffn_b'):
        W[n] = rep[n][:, None, :]
    return W


SMALL = ('a_pw1_b', 'a_dw_w', 'a_dw_b', 'a_ln_g', 'a_ln_b', 'a_pw2_b', 'ffn_conv_w')


def _step(x, p, loss_target, w, m, v):
    S, D = x.shape[-2:]
    x2, tgt = x.reshape(S, D), loss_target.reshape(S, D)
    place = jnp.stack([lax.axis_index("c"), 2 * lax.axis_index("x") + lax.axis_index("y")]).astype(jnp.int32)

    big_in = [w[n].astype(BF16).reshape((2, -1) + w[n].shape[1:] if w[n].ndim == 3 else (2, -1, w[n].shape[-1])) for n in BIG]
    small_in, s_offs, s_sizes = _pack([w[n] for n in SMALL], (), 8)
    gathered = allgather_chips("gather_weights", big_in + [small_in])
    big = {n: g.reshape((N_CHIPS,) + w[n].shape) for n, g in zip(BIG, gathered[:-1])}
    small = dict(zip(SMALL, _unpack(gathered[-1], (N_CHIPS,), s_offs, s_sizes, [w[n].shape for n in SMALL])))
    W = _whole_weights(big, small, {n: w[n] for n in REPLICATED}, D)

    loss_sum, dx, G = forward_backward(x2, p, tgt, W)
    loss = lax.psum(0.5 * loss_sum / D, ("x", "y", "c"))

    vectors = [n for n in WEIGHTS if n not in BIG]
    mats, members = _group_rows(G, BIG)
    g_sm = _stack_grads(G, vectors)
    packed, offs, sizes = _pack([g_sm[n] for n in vectors], (N_CHIPS,), 512)
    reduced = reduce_scatter(mats + [packed], [BF16] * len(mats) + [F32], place)
    shapes = [w[n].shape if n not in REPLICATED else (w[n].size // N_CHIPS,) for n in vectors]
    g_mine = dict(zip(vectors, _unpack(reduced[-1], (), offs, sizes, shapes)))
    for red, where in zip(reduced[:-1], members):
        rows = red.reshape(-1, red.shape[-1])
        for n, off, cnt in where:
            g_mine[n] = lax.slice_in_dim(rows, off, off + cnt, axis=0).reshape(w[n].shape)
    rep_in, r_offs, r_sizes = _pack([g_mine[n] for n in REPLICATED], (), 8)
    rep_all = allgather_chips("gather_replicated_grads", [rep_in])[0]
    for n, g in zip(REPLICATED, _unpack(rep_all, (N_CHIPS,), r_offs, r_sizes, [(w[n].size // N_CHIPS,) for n in REPLICATED])):
        g_mine[n] = g.reshape(w[n].shape)

    grads, deltas, new_m, new_v = [], [], [], []
    for n in WEIGHTS:
        d, mo, vo = adamw(f"adamw_{n}", w[n], g_mine[n], m[n], v[n])
        grads.append(g_mine[n])
        deltas.append(d)
        new_m.append(mo)
        new_v.append(vo)
    return (loss, dx.reshape(x.shape), *grads, *deltas, *new_m, *new_v)


def kernel(x, p, a_pw1_w, a_pw1_b, a_dw_w, a_dw_b, a_ln_g, a_ln_b, a_pw2_w, a_pw2_b, b_wq, kv_wk, kv_wv, b_wo, ln_mix_g, ln_mix_b, ffn_w_up, ffn_w_gate, ffn_conv_w, ffn_conv_b, ffn_w_down, ple_w_gate, ple_w_proj, ln_ffn_g, ln_ffn_b, loss_target, m_a_pw1_w, m_a_pw1_b, m_a_dw_w, m_a_dw_b, m_a_ln_g, m_a_ln_b, m_a_pw2_w, m_a_pw2_b, m_b_wq, m_kv_wk, m_kv_wv, m_b_wo, m_ln_mix_g, m_ln_mix_b, m_ffn_w_up, m_ffn_w_gate, m_ffn_conv_w, m_ffn_conv_b, m_ffn_w_down, m_ple_w_gate, m_ple_w_proj, m_ln_ffn_g, m_ln_ffn_b, v_a_pw1_w, v_a_pw1_b, v_a_dw_w, v_a_dw_b, v_a_ln_g, v_a_ln_b, v_a_pw2_w, v_a_pw2_b, v_b_wq, v_kv_wk, v_kv_wv, v_b_wo, v_ln_mix_g, v_ln_mix_b, v_ffn_w_up, v_ffn_w_gate, v_ffn_conv_w, v_ffn_conv_b, v_ffn_w_down, v_ple_w_gate, v_ple_w_proj, v_ln_ffn_g, v_ln_ffn_b):
    vals = dict(locals())
    w = {n: vals[n] for n in WEIGHTS}
    m = {n: vals["m_" + n] for n in WEIGHTS}
    v = {n: vals["v_" + n] for n in WEIGHTS}
    return _step(x, p, loss_target, w, m, v)
```

```python
import functools
import math

import jax
import jax.numpy as jnp
import numpy as np
from jax import lax
from jax.experimental import pallas as pl
from jax.experimental.pallas import tpu as pltpu

F32, BF16 = jnp.float32, jnp.bfloat16

HEAD_DIM = 64
LN_EPS = 1e-5
DEPTH = 4
N_A = DEPTH // 2
DN_ALPHA = (2.0 * DEPTH) ** 0.25
N_CHIPS = 4

ADAM_LR, ADAM_B1, ADAM_B2, ADAM_EPS, ADAM_WD, ADAM_STEP = 0.001, 0.9, 0.999, 1e-08, 0.01, 10

VMEM_LIMIT_BYTES = 56 * 2**20
LANES = 128
CONV_HALO = 32
FFN_HALO = 8

NN = (((1,), (0,)), ((), ()))
NT = (((1,), (1,)), ((), ()))
TN = (((0,), (0,)), ((), ()))

WEIGHTS = ['a_pw1_w', 'a_pw1_b', 'a_dw_w', 'a_dw_b', 'a_ln_g', 'a_ln_b', 'a_pw2_w', 'a_pw2_b', 'b_wq', 'kv_wk', 'kv_wv',
           'b_wo', 'ln_mix_g', 'ln_mix_b', 'ffn_w_up', 'ffn_w_gate', 'ffn_conv_w', 'ffn_conv_b', 'ffn_w_down', 'ple_w_gate',
           'ple_w_proj', 'ln_ffn_g', 'ln_ffn_b']
REPLICATED = ('ln_mix_g', 'ln_mix_b', 'ffn_conv_b', 'ln_ffn_g', 'ln_ffn_b')
BIG = ('a_pw1_w', 'a_pw2_w', 'b_wq', 'kv_wk', 'kv_wv', 'b_wo', 'ffn_w_up', 'ffn_w_gate', 'ffn_w_down', 'ple_w_gate',
       'ple_w_proj')


def _tile(n, pref, mult=8):
    t = min(n, pref)
    while t > 0:
        if n % t == 0 and t % mult == 0:
            return t
        t -= 1
    return n


def _params(sem):
    return pltpu.CompilerParams(dimension_semantics=sem, vmem_limit_bytes=VMEM_LIMIT_BYTES)


def _sigmoid(x):
    return 1.0 / (1.0 + jnp.exp(-x))


def _mm_call(name, grid, terms, out_spec, out_sds, dims, bias=None, res=None, res_alpha=1.0, out_scale=None):
    n_terms = len(terms)

    def body(*refs):
        o_ref = refs[-1]
        acc = None
        for t in range(n_terms):
            a = refs[2 * t][...].astype(BF16)
            b = refs[2 * t + 1][...].astype(BF16)
            d = lax.dot_general(a, b, dims, preferred_element_type=F32)
            acc = d if acc is None else acc + d
        k = 2 * n_terms
        if bias is not None:
            acc = acc + refs[k][...]
            k += 1
        if res is not None:
            acc = acc + res_alpha * refs[k][...]
        if out_scale is not None:
            acc = acc * out_scale
        o_ref[...] = acc.astype(o_ref.dtype)

    operands, specs = [], []
    for a, a_spec, b, b_spec in terms:
        operands += [a, b]
        specs += [a_spec, b_spec]
    for extra in (bias, res):
        if extra is not None:
            operands.append(extra[0])
            specs.append(extra[1])
    return pl.pallas_call(body, out_shape=out_sds, grid=grid, in_specs=specs, out_specs=out_spec,
                          compiler_params=_params(("parallel",) * len(grid)), name=name)(*operands)


def mm_colsm(name, x, w4, layer, bias4=None, tm=512):
    M, K = x.shape
    n = w4.shape[-1]
    tm = _tile(M, tm)
    terms = [(x, pl.BlockSpec((tm, K), lambda j, i: (i, 0)), w4, pl.BlockSpec((None, None, K, n), lambda j, i: (j, layer, 0, 0)))]
    bias = None if bias4 is None else (bias4, pl.BlockSpec((None, None, 1, n), lambda j, i: (j, layer, 0, 0)))
    return _mm_call(name, (N_CHIPS, M // tm), terms, pl.BlockSpec((None, tm, n), lambda j, i: (j, i, 0)),
                    jax.ShapeDtypeStruct((N_CHIPS, M, n), F32), NN, bias=bias)


def mm_rowsm(name, a, w4, layer, a_sm=False, bias=None, out_dtype=F32, out_scale=None, tm=512):
    kc, N = w4.shape[-2:]
    M = a.shape[-2]
    tm = _tile(M, tm)
    terms = []
    for j in range(N_CHIPS):
        if a_sm:
            a_spec = pl.BlockSpec((None, tm, kc), lambda i, j=j: (j, i, 0))
        else:
            a_spec = pl.BlockSpec((tm, kc), lambda i, j=j: (i, j))
        terms.append((a, a_spec, w4, pl.BlockSpec((None, None, kc, N), lambda i, j=j: (j, layer, 0, 0))))
    b = None if bias is None else (bias, pl.BlockSpec((None, 1, N), lambda i: (layer, 0, 0)))
    return _mm_call(name, (M // tm,), terms, pl.BlockSpec((tm, N), lambda i: (i, 0)), jax.ShapeDtypeStruct((M, N), out_dtype),
                    NN, bias=b, out_scale=out_scale)


def mm_nt_rowsm(name, dy, w4, layer, out_sm=False, res=None, res_alpha=1.0, tm=512):
    kc, N = w4.shape[-2:]
    M = dy.shape[0]
    tm = _tile(M, tm)
    terms = [(dy, pl.BlockSpec((tm, N), lambda j, i: (i, 0)), w4, pl.BlockSpec((None, None, kc, N), lambda j, i: (j, layer, 0, 0)))]
    if out_sm:
        out_spec, sds = pl.BlockSpec((None, tm, kc), lambda j, i: (j, i, 0)), jax.ShapeDtypeStruct((N_CHIPS, M, kc), F32)
    else:
        out_spec, sds = pl.BlockSpec((tm, kc), lambda j, i: (i, j)), jax.ShapeDtypeStruct((M, N_CHIPS * kc), F32)
    r = None if res is None else (res, pl.BlockSpec((tm, kc), lambda j, i: (i, j)))
    return _mm_call(name, (N_CHIPS, M // tm), terms, out_spec, sds, NT, res=r, res_alpha=res_alpha)


def mm_nt_colsm(name, dy_sm, w4, layer, res=None, res_alpha=1.0, tm=512):
    K, n = w4.shape[-2:]
    M = dy_sm.shape[1]
    tm = _tile(M, tm)
    terms = [(dy_sm, pl.BlockSpec((None, tm, n), lambda i, j=j: (j, i, 0)), w4,
              pl.BlockSpec((None, None, K, n), lambda i, j=j: (j, layer, 0, 0))) for j in range(N_CHIPS)]
    r = None if res is None else (res, pl.BlockSpec((tm, K), lambda i: (i, 0)))
    return _mm_call(name, (M // tm,), terms, pl.BlockSpec((tm, K), lambda i: (i, 0)), jax.ShapeDtypeStruct((M, K), F32), NT,
                    res=r, res_alpha=res_alpha)


def mm_tn_colsm(name, x, dy_sm, tk=512):
    M, K = x.shape
    n = dy_sm.shape[-1]
    tk = _tile(K, tk, LANES)
    terms = [(x, pl.BlockSpec((M, tk), lambda j, k: (0, k)), dy_sm, pl.BlockSpec((None, M, n), lambda j, k: (j, 0, 0)))]
    return _mm_call(name, (N_CHIPS, K // tk), terms, pl.BlockSpec((None, tk, n), lambda j, k: (j, k, 0)),
                    jax.ShapeDtypeStruct((N_CHIPS, K, n), F32), TN)


def mm_tn_rowsm(name, a, dy, a_sm=False, tn=512):
    M, N = dy.shape
    kc = a.shape[-1] if a_sm else a.shape[-1] // N_CHIPS
    tn = _tile(N, tn, LANES)
    a_spec = pl.BlockSpec((None, M, kc), lambda j, n: (j, 0, 0)) if a_sm else pl.BlockSpec((M, kc), lambda j, n: (0, j))
    terms = [(a, a_spec, dy, pl.BlockSpec((M, tn), lambda j, n: (0, n)))]
    return _mm_call(name, (N_CHIPS, N // tn), terms, pl.BlockSpec((None, kc, tn), lambda j, n: (j, 0, n)),
                    jax.ShapeDtypeStruct((N_CHIPS, kc, N), F32), TN)


def mm_proj(name, p4, layer, w4, tm=512):
    S, P = p4.shape[-2:]
    n = w4.shape[-1]
    tm = _tile(S, tm)
    terms = [(p4, pl.BlockSpec((None, None, tm, P), lambda j, i: (layer, 0, i, 0)), w4,
              pl.BlockSpec((None, None, P, n), lambda j, i: (j, layer, 0, 0)))]
    return _mm_call(name, (N_CHIPS, S // tm), terms, pl.BlockSpec((tm, n), lambda j, i: (i, j)),
                    jax.ShapeDtypeStruct((S, N_CHIPS * n), F32), NN)


def mm_tn_proj(name, p4, layer, dpp):
    S, P = p4.shape[-2:]
    n = dpp.shape[-1] // N_CHIPS
    terms = [(p4, pl.BlockSpec((None, None, S, P), lambda j: (layer, 0, 0, 0)), dpp, pl.BlockSpec((S, n), lambda j: (0, j)))]
    return _mm_call(name, (N_CHIPS,), terms, pl.BlockSpec((None, P, n), lambda j: (j, 0, 0)),
                    jax.ShapeDtypeStruct((N_CHIPS, P, n), F32), TN)


def ln_fwd(name, x, mix, g, b, layer, tg=None, pp=None, tm=256):
    S, D = x.shape
    tm = _tile(S, tm)
    ple = tg is not None

    def body(*refs):
        if ple:
            x_ref, m_ref, tg_ref, pp_ref, g_ref, b_ref, y_ref, xh_ref, rs_ref = refs
        else:
            x_ref, m_ref, g_ref, b_ref, y_ref, xh_ref, rs_ref = refs
        r = DN_ALPHA * x_ref[...] + m_ref[...]
        if ple:
            r = r + _sigmoid(tg_ref[...]) * pp_ref[...]
        mu = jnp.mean(r, axis=-1, keepdims=True)
        d = r - mu
        var = jnp.mean(d * d, axis=-1, keepdims=True)
        rstd = lax.rsqrt(var + LN_EPS)
        xh = d * rstd
        y_ref[...] = xh * g_ref[...] + b_ref[...]
        xh_ref[...] = xh
        rs_ref[...] = rstd

    row = pl.BlockSpec((tm, D), lambda i: (i, 0))
    vec = pl.BlockSpec((None, 1, D), lambda i: (layer, 0, 0))
    ins = [x, mix] + ([tg, pp] if ple else []) + [g, b]
    specs = [row, row] + ([row, row] if ple else []) + [vec, vec]
    return pl.pallas_call(body, grid=(S // tm,), in_specs=specs,
                          out_specs=(row, row, pl.BlockSpec((tm, 1), lambda i: (i, 0))),
                          out_shape=(jax.ShapeDtypeStruct((S, D), F32), jax.ShapeDtypeStruct((S, D), F32),
                                     jax.ShapeDtypeStruct((S, 1), F32)),
                          compiler_params=_params(("parallel",)), name=name)(*ins)


def ln_bwd(name, dy, xh, rstd, g, layer, tg=None, pp=None, tm=256):
    S, D = dy.shape
    tm = _tile(S, tm)
    ple = tg is not None

    def body(*refs):
        if ple:
            dy_ref, xh_ref, rs_ref, g_ref, tg_ref, pp_ref, dr_ref, acc_ref, dtg_ref, dpp_ref = refs
        else:
            dy_ref, xh_ref, rs_ref, g_ref, dr_ref, acc_ref = refs
        dy_, xh_ = dy_ref[...], xh_ref[...]
        dxh = dy_ * g_ref[...]
        m1 = jnp.mean(dxh, axis=-1, keepdims=True)
        m2 = jnp.mean(dxh * xh_, axis=-1, keepdims=True)
        dr = rs_ref[...] * (dxh - m1 - xh_ * m2)
        dr_ref[...] = dr

        @pl.when(pl.program_id(0) == 0)
        def _():
            acc_ref[...] = jnp.zeros_like(acc_ref)
        acc_ref[0:1, :] += jnp.sum(dy_ * xh_, axis=0, keepdims=True)
        acc_ref[1:2, :] += jnp.sum(dy_, axis=0, keepdims=True)
        acc_ref[2:3, :] += jnp.sum(dr, axis=0, keepdims=True)
        if ple:
            pg = _sigmoid(tg_ref[...])
            dtg_ref[...] = (dr * pp_ref[...] * pg * (1.0 - pg)).astype(BF16)
            dpp_ref[...] = (dr * pg).astype(BF16)

    row = pl.BlockSpec((tm, D), lambda i: (i, 0))
    ins = [dy, xh, rstd, g] + ([tg, pp] if ple else [])
    specs = [row, row, pl.BlockSpec((tm, 1), lambda i: (i, 0)), pl.BlockSpec((None, 1, D), lambda i: (layer, 0, 0))] + ([row, row] if ple else [])
    outs = [jax.ShapeDtypeStruct((S, D), F32), jax.ShapeDtypeStruct((8, D), F32)]
    out_specs = [row, pl.BlockSpec((8, D), lambda i: (0, 0))]
    if ple:
        outs += [jax.ShapeDtypeStruct((S, D), BF16)] * 2
        out_specs += [row, row]
    return pl.pallas_call(body, grid=(S // tm,), in_specs=specs, out_specs=tuple(out_specs), out_shape=tuple(outs),
                          compiler_params=_params(("arbitrary",)), name=name)(*ins)


def glu_fwd(name, h_sm, tm=512):
    _, S, n = h_sm.shape
    tm = _tile(S, tm)
    half = N_CHIPS // 2

    def body(a_ref, g_ref, u_ref):
        u_ref[...] = a_ref[...] * _sigmoid(g_ref[...])

    return pl.pallas_call(body, grid=(half, S // tm),
                          in_specs=[pl.BlockSpec((None, tm, n), lambda j, i: (j, i, 0)),
                                    pl.BlockSpec((None, tm, n), lambda j, i: (j + half, i, 0))],
                          out_specs=pl.BlockSpec((tm, n), lambda j, i: (i, j)),
                          out_shape=jax.ShapeDtypeStruct((S, half * n), F32),
                          compiler_params=_params(("parallel", "parallel")), name=name)(h_sm, h_sm)


def conv_ln_silu_fwd(name, u, w, b, g, beta, layer, ts=128):
    S, D = u.shape
    kw = w.shape[1]
    ts = _tile(S, ts, CONV_HALO)
    lc = LANES if D % LANES == 0 else D

    def body(h_ref, u_ref, w_ref, b_ref, g_ref, be_ref, c_ref, s_ref, win_ref):
        i = pl.program_id(0)
        win_ref[0:CONV_HALO, :] = jnp.where(i == 0, 0.0, h_ref[...])
        win_ref[CONV_HALO:, :] = u_ref[...]
        for cc in range(D // lc):
            cs = slice(cc * lc, (cc + 1) * lc)
            acc = jnp.zeros((ts, lc), F32) + b_ref[:, cs]
            for k in range(kw):
                off = CONV_HALO - (kw - 1) + k
                acc = acc + w_ref[k:k + 1, cs] * win_ref[off:off + ts, cs]
            c_ref[:, cs] = acc
        c = c_ref[...]
        mu = jnp.mean(c, axis=-1, keepdims=True)
        d = c - mu
        var = jnp.mean(d * d, axis=-1, keepdims=True)
        nrm = d * lax.rsqrt(var + LN_EPS) * g_ref[...] + be_ref[...]
        s_ref[...] = (nrm * _sigmoid(nrm)).astype(BF16)

    row = pl.BlockSpec((ts, D), lambda i: (i, 0))
    vec = pl.BlockSpec((None, 1, D), lambda i: (layer, 0, 0))
    halo = pl.BlockSpec((CONV_HALO, D), lambda i: (jnp.maximum(i * (ts // CONV_HALO) - 1, 0), 0))
    return pl.pallas_call(body, grid=(S // ts,),
                          in_specs=[halo, row, pl.BlockSpec((None, kw, D), lambda i: (layer, 0, 0)), vec, vec, vec],
                          out_specs=(row, row),
                          out_shape=(jax.ShapeDtypeStruct((S, D), F32), jax.ShapeDtypeStruct((S, D), BF16)),
                          scratch_shapes=[pltpu.VMEM((ts + CONV_HALO, D), F32)],
                          compiler_params=_params(("parallel",)), name=name)(u, u, w, b, g, beta)


def ln_silu_bwd(name, ds, c, g, beta, layer, tm=256):
    S, D = c.shape
    tm = _tile(S, tm)

    def body(ds_ref, c_ref, g_ref, be_ref, dc_ref, acc_ref):
        c_ = c_ref[...]
        mu = jnp.mean(c_, axis=-1, keepdims=True)
        d = c_ - mu
        var = jnp.mean(d * d, axis=-1, keepdims=True)
        rstd = lax.rsqrt(var + LN_EPS)
        xh = d * rstd
        nrm = xh * g_ref[...] + be_ref[...]
        sg = _sigmoid(nrm)
        dn = ds_ref[...] * (sg * (1.0 + nrm * (1.0 - sg)))
        dxh = dn * g_ref[...]
        m1 = jnp.mean(dxh, axis=-1, keepdims=True)
        m2 = jnp.mean(dxh * xh, axis=-1, keepdims=True)
        dc = rstd * (dxh - m1 - xh * m2)
        dc_ref[...] = dc

        @pl.when(pl.program_id(0) == 0)
        def _():
            acc_ref[...] = jnp.zeros_like(acc_ref)
        acc_ref[0:1, :] += jnp.sum(dn * xh, axis=0, keepdims=True)
        acc_ref[1:2, :] += jnp.sum(dn, axis=0, keepdims=True)
        acc_ref[2:3, :] += jnp.sum(dc, axis=0, keepdims=True)

    row = pl.BlockSpec((tm, D), lambda i: (i, 0))
    vec = pl.BlockSpec((None, 1, D), lambda i: (layer, 0, 0))
    return pl.pallas_call(body, grid=(S // tm,), in_specs=[row, row, vec, vec],
                          out_specs=(row, pl.BlockSpec((8, D), lambda i: (0, 0))),
                          out_shape=(jax.ShapeDtypeStruct((S, D), F32), jax.ShapeDtypeStruct((8, D), F32)),
                          compiler_params=_params(("arbitrary",)), name=name)(ds, c, g, beta)


def conv_glu_bwd(name, dc, u, h_sm, w, layer, ts=128):
    S, D = dc.shape
    kw = w.shape[1]
    half = N_CHIPS // 2
    n = D // half
    ts = _tile(S, ts, CONV_HALO)
    nblk = S // ts
    lc = LANES if n % LANES == 0 else n

    def body(dc_ref, dcn_ref, u_ref, up_ref, a_ref, g_ref, w_ref, da_ref, dg_ref, dw_ref, dba_ref, dbg_ref, dwin_ref, uwin_ref):
        i = pl.program_id(1)
        dwin_ref[0:ts, :] = dc_ref[...]
        dwin_ref[ts:, :] = jnp.where(i == nblk - 1, 0.0, dcn_ref[...])
        uwin_ref[0:CONV_HALO, :] = jnp.where(i == 0, 0.0, up_ref[...])
        uwin_ref[CONV_HALO:, :] = u_ref[...]

        @pl.when(i == 0)
        def _():
            dw_ref[...] = jnp.zeros_like(dw_ref)
            dba_ref[...] = jnp.zeros_like(dba_ref)
            dbg_ref[...] = jnp.zeros_like(dbg_ref)

        for cc in range(n // lc):
            cs = slice(cc * lc, (cc + 1) * lc)
            dcb = dwin_ref[0:ts, cs]
            du = jnp.zeros((ts, lc), F32)
            for k in range(kw):
                du = du + w_ref[k:k + 1, cs] * dwin_ref[kw - 1 - k:kw - 1 - k + ts, cs]
                off = CONV_HALO - (kw - 1) + k
                dw_ref[k:k + 1, cs] += jnp.sum(dcb * uwin_ref[off:off + ts, cs], axis=0, keepdims=True)
            a = a_ref[:, cs]
            sg = _sigmoid(g_ref[:, cs])
            da = du * sg
            dg = du * a * sg * (1.0 - sg)
            da_ref[:, cs] = da.astype(BF16)
            dg_ref[:, cs] = dg.astype(BF16)
            dba_ref[0:1, cs] += jnp.sum(da, axis=0, keepdims=True)
            dbg_ref[0:1, cs] += jnp.sum(dg, axis=0, keepdims=True)

    r = ts // CONV_HALO
    main = pl.BlockSpec((ts, n), lambda j, i: (i, j))
    nxt = pl.BlockSpec((CONV_HALO, n), lambda j, i: (jnp.minimum((i + 1) * r, S // CONV_HALO - 1), j))
    prv = pl.BlockSpec((CONV_HALO, n), lambda j, i: (jnp.maximum(i * r - 1, 0), j))
    sm_a = pl.BlockSpec((None, ts, n), lambda j, i: (j, i, 0))
    sm_g = pl.BlockSpec((None, ts, n), lambda j, i: (j + half, i, 0))
    acc8a = pl.BlockSpec((None, 8, n), lambda j, i: (j, 0, 0))
    acc8g = pl.BlockSpec((None, 8, n), lambda j, i: (j + half, 0, 0))
    da, dg, dw, dba, dbg = pl.pallas_call(
        body, grid=(half, nblk),
        in_specs=[main, nxt, main, prv, sm_a, sm_g, pl.BlockSpec((None, kw, n), lambda j, i: (layer, 0, j))],
        out_specs=(pl.BlockSpec((None, ts, n), lambda j, i: (j, i, 0)), pl.BlockSpec((None, ts, n), lambda j, i: (j, i, 0)),
                   pl.BlockSpec((None, 32, n), lambda j, i: (j, 0, 0)),
                   pl.BlockSpec((None, 8, n), lambda j, i: (j, 0, 0)), pl.BlockSpec((None, 8, n), lambda j, i: (j, 0, 0))),
        out_shape=(jax.ShapeDtypeStruct((half, S, n), BF16), jax.ShapeDtypeStruct((half, S, n), BF16),
                   jax.ShapeDtypeStruct((half, 32, n), F32),
                   jax.ShapeDtypeStruct((half, 8, n), F32), jax.ShapeDtypeStruct((half, 8, n), F32)),
        scratch_shapes=[pltpu.VMEM((ts + CONV_HALO, n), F32), pltpu.VMEM((ts + CONV_HALO, n), F32)],
        compiler_params=_params(("parallel", "arbitrary")), name=name)(dc, dc, u, u, h_sm, h_sm, w)
    del acc8a, acc8g
    return da, dg, dw, dba, dbg


ROW_CHUNK = 16


def _ffn_gc(win_ref, w_ref, b_ref, r0, rows, kw, base):
    gc = b_ref[...] + jnp.zeros((rows, win_ref.shape[1]), F32)
    for k in range(kw):
        off = r0 + base - (kw - 1) + k
        gc = gc + w_ref[k:k + 1, :] * win_ref[off:off + rows, :]
    return gc


def ffn_gate_fwd(name, up_sm, gp_sm, w4, b4, layer, ts=256):
    _, S, n = up_sm.shape
    kw = w4.shape[2]
    ts = _tile(S, ts, ROW_CHUNK)
    rc = ROW_CHUNK

    def body(up_ref, gp_ref, gph_ref, w_ref, b_ref, hf_ref, win_ref):
        i = pl.program_id(1)
        win_ref[0:FFN_HALO, :] = jnp.where(i == 0, 0.0, gph_ref[...])
        win_ref[FFN_HALO:, :] = gp_ref[...]
        for r0 in range(0, ts, rc):
            gc = _ffn_gc(win_ref, w_ref, b_ref, r0, rc, kw, FFN_HALO)
            hf_ref[r0:r0 + rc, :] = (gc * _sigmoid(gc) * up_ref[r0:r0 + rc, :]).astype(BF16)

    main = pl.BlockSpec((None, ts, n), lambda j, i: (j, i, 0))
    prv = pl.BlockSpec((None, FFN_HALO, n), lambda j, i: (j, jnp.maximum(i * (ts // FFN_HALO) - 1, 0), 0))
    return pl.pallas_call(body, grid=(N_CHIPS, S // ts),
                          in_specs=[main, main, prv, pl.BlockSpec((None, None, kw, n), lambda j, i: (j, layer, 0, 0)),
                                    pl.BlockSpec((None, None, 1, n), lambda j, i: (layer, j, 0, 0))],
                          out_specs=main, out_shape=jax.ShapeDtypeStruct((N_CHIPS, S, n), BF16),
                          scratch_shapes=[pltpu.VMEM((ts + FFN_HALO, n), F32)],
                          compiler_params=_params(("parallel", "parallel")), name=name)(up_sm, gp_sm, gp_sm, w4, b4)


def ffn_gate_bwd_a(name, dhf_sm, up_sm, gp_sm, w4, b4, layer, ts=256):
    _, S, n = up_sm.shape
    kw = w4.shape[2]
    ts = _tile(S, ts, ROW_CHUNK)
    rc = ROW_CHUNK

    def body(dhf_ref, up_ref, gp_ref, gph_ref, w_ref, b_ref, dup_ref, dgc_ref, win_ref):
        i = pl.program_id(1)
        win_ref[0:FFN_HALO, :] = jnp.where(i == 0, 0.0, gph_ref[...])
        win_ref[FFN_HALO:, :] = gp_ref[...]
        for r0 in range(0, ts, rc):
            rows = slice(r0, r0 + rc)
            gc = _ffn_gc(win_ref, w_ref, b_ref, r0, rc, kw, FFN_HALO)
            sg = _sigmoid(gc)
            dhf = dhf_ref[rows, :]
            dup_ref[rows, :] = (dhf * gc * sg).astype(BF16)
            dgc_ref[rows, :] = dhf * up_ref[rows, :] * (sg * (1.0 + gc * (1.0 - sg)))

    main = pl.BlockSpec((None, ts, n), lambda j, i: (j, i, 0))
    prv = pl.BlockSpec((None, FFN_HALO, n), lambda j, i: (j, jnp.maximum(i * (ts // FFN_HALO) - 1, 0), 0))
    return pl.pallas_call(body, grid=(N_CHIPS, S // ts),
                          in_specs=[main, main, main, prv, pl.BlockSpec((None, None, kw, n), lambda j, i: (j, layer, 0, 0)),
                                    pl.BlockSpec((None, None, 1, n), lambda j, i: (layer, j, 0, 0))],
                          out_specs=(main, main),
                          out_shape=(jax.ShapeDtypeStruct((N_CHIPS, S, n), BF16), jax.ShapeDtypeStruct((N_CHIPS, S, n), F32)),
                          scratch_shapes=[pltpu.VMEM((ts + FFN_HALO, n), F32)],
                          compiler_params=_params(("parallel", "parallel")), name=name)(dhf_sm, up_sm, gp_sm, gp_sm, w4, b4)


def ffn_gate_bwd_b(name, dgc_sm, gp_sm, w4, layer, ts=256):
    _, S, n = gp_sm.shape
    kw = w4.shape[2]
    ts = _tile(S, ts, ROW_CHUNK)
    rc = ROW_CHUNK
    nblk = S // ts

    def body(dgc_ref, dgn_ref, gp_ref, gph_ref, w_ref, dgp_ref, acc_ref, dwin_ref, gwin_ref):
        i = pl.program_id(1)
        dwin_ref[0:ts, :] = dgc_ref[...]
        dwin_ref[ts:, :] = jnp.where(i == nblk - 1, 0.0, dgn_ref[...])
        gwin_ref[0:FFN_HALO, :] = jnp.where(i == 0, 0.0, gph_ref[...])
        gwin_ref[FFN_HALO:, :] = gp_ref[...]

        @pl.when(i == 0)
        def _():
            acc_ref[...] = jnp.zeros_like(acc_ref)
        sums = [jnp.zeros((8, n), F32) for _ in range(kw + 1)]
        for r0 in range(0, ts, rc):
            dgc = dwin_ref[r0:r0 + rc, :]
            dgp = jnp.zeros((rc, n), F32)
            for k in range(kw):
                dgp = dgp + w_ref[k:k + 1, :] * dwin_ref[r0 + kw - 1 - k:r0 + kw - 1 - k + rc, :]
                off = r0 + FFN_HALO - (kw - 1) + k
                prod = dgc * gwin_ref[off:off + rc, :]
                sums[k] = sums[k] + prod[0:8, :] + prod[8:16, :]
            sums[kw] = sums[kw] + dgc[0:8, :] + dgc[8:16, :]
            dgp_ref[r0:r0 + rc, :] = dgp.astype(BF16)
        for k in range(kw):
            acc_ref[k:k + 1, :] += jnp.sum(sums[k], axis=0, keepdims=True)
        acc_ref[7:8, :] += jnp.sum(sums[kw], axis=0, keepdims=True)

    r = ts // FFN_HALO
    main = pl.BlockSpec((None, ts, n), lambda j, i: (j, i, 0))
    nxt = pl.BlockSpec((None, FFN_HALO, n), lambda j, i: (j, jnp.minimum((i + 1) * r, S // FFN_HALO - 1), 0))
    prv = pl.BlockSpec((None, FFN_HALO, n), lambda j, i: (j, jnp.maximum(i * r - 1, 0), 0))
    return pl.pallas_call(body, grid=(N_CHIPS, nblk),
                          in_specs=[main, nxt, main, prv, pl.BlockSpec((None, None, kw, n), lambda j, i: (j, layer, 0, 0))],
                          out_specs=(main, pl.BlockSpec((None, 8, n), lambda j, i: (j, 0, 0))),
                          out_shape=(jax.ShapeDtypeStruct((N_CHIPS, S, n), BF16), jax.ShapeDtypeStruct((N_CHIPS, 8, n), F32)),
                          scratch_shapes=[pltpu.VMEM((ts + FFN_HALO, n), F32), pltpu.VMEM((ts + FFN_HALO, n), F32)],
                          compiler_params=_params(("parallel", "arbitrary")), name=name)(dgc_sm, dgc_sm, gp_sm, gp_sm, w4)


def _neg_softplus(z):
    e = jnp.exp(-jnp.abs(z))
    return -(jnp.maximum(z, 0.0) + jnp.log(1.0 + e)), e


def _split_dot(x, t):
    hi = x.astype(BF16)
    lo = (x - hi.astype(F32)).astype(BF16)
    return jnp.dot(hi, t, preferred_element_type=F32) + jnp.dot(lo, t, preferred_element_type=F32)


STICK_GONE = -100.0
NOT_SWEPT = -1e30


def attn_fwd(name, q, k, v, bq=256, w=256):
    S, D = q.shape
    dh = HEAD_DIM
    hpb = LANES // dh
    bq = _tile(S, bq)
    w = _tile(bq, w)
    nsub = bq // w
    nkb = S // w

    def body(q_ref, k_ref, v_ref, o_ref, runs_ref, rs_ref):
        qi = pl.program_id(1)
        rr = lax.broadcasted_iota(jnp.int32, (w, w), 0)
        cc = lax.broadcasted_iota(jnp.int32, (w, w), 1)
        t_suf = (rr >= cc).astype(BF16)
        tq = qi * bq + lax.broadcasted_iota(jnp.int32, (bq, w), 0)
        tk = lax.broadcasted_iota(jnp.int32, (bq, w), 1)
        lane = lax.broadcasted_iota(jnp.int32, (bq, LANES), 1)
        ntot = (qi + 1) * nsub
        for hh in range(hpb):
            hs = slice(hh * dh, (hh + 1) * dh)
            qb = q_ref[:, hs]
            rs_ref[...] = jnp.where(lane < ntot, NOT_SWEPT, 0.0)

            def block(kb, run, acc, masked):
                kstart = pl.multiple_of(kb * w, w)
                kblk = k_ref[pl.ds(kstart, w), hs]
                vblk = v_ref[pl.ds(kstart, w), hs]
                z = lax.dot_general(qb, kblk, NT, preferred_element_type=F32)
                lg, _ = _neg_softplus(z)
                if masked:
                    m = (tk + kstart) < tq
                    lg = jnp.where(m, lg, 0.0)
                cum = _split_dot(lg, t_suf) + run
                a = jnp.exp(z + cum)
                if masked:
                    a = jnp.where(m, a, 0.0)
                acc = acc + jnp.dot(a.astype(BF16), vblk, preferred_element_type=F32)
                run = cum[:, 0:1]
                rs_ref[...] = jnp.where(lane == kb, run, rs_ref[...])
                return run, acc

            run = jnp.zeros((bq, 1), F32)
            acc = jnp.zeros((bq, dh), F32)
            for sb in reversed(range(nsub)):
                run, acc = block(qi * nsub + sb, run, acc, True)

            def cond(c):
                return jnp.logical_and(c[0] >= 0, jnp.max(c[1]) > STICK_GONE)

            def step(c):
                run, acc = block(c[0], c[1], c[2], False)
                return c[0] - 1, run, acc
            _, run, acc = lax.while_loop(cond, step, (qi * nsub - 1, run, acc))
            o_ref[:, hs] = acc
            runs_ref[hh] = rs_ref[:, 0:nkb]

    qs = pl.BlockSpec((bq, LANES), lambda h, i: (i, h))
    kv = pl.BlockSpec((S, LANES), lambda h, i: (0, h))
    return pl.pallas_call(body, grid=(D // LANES, S // bq), in_specs=[qs, kv, kv],
                          out_specs=(qs, pl.BlockSpec((hpb, bq, nkb), lambda h, i: (h, i, 0))),
                          out_shape=(jax.ShapeDtypeStruct((S, D), F32), jax.ShapeDtypeStruct((D // dh, S, nkb), F32)),
                          scratch_shapes=[pltpu.VMEM((bq, LANES), F32)],
                          compiler_params=_params(("parallel", "parallel")), name=name)(q, k, v)


def attn_bwd(name, q, k, v, do, runs, dk0=None, dv0=None, bq=256, w=256):
    S, D = q.shape
    dh = HEAD_DIM
    hpb = LANES // dh
    bq = _tile(S, bq)
    w = _tile(bq, w)
    nsub = bq // w
    nkb = S // w
    scale = 1.0 / math.sqrt(dh)
    init = dk0 is not None

    def body(*refs):
        if init:
            q_ref, k_ref, v_ref, do_ref, runs_ref, dk0_ref, dv0_ref, dq_ref, dk_ref, dv_ref, rs_ref = refs
        else:
            q_ref, k_ref, v_ref, do_ref, runs_ref, dq_ref, dk_ref, dv_ref, rs_ref = refs
        qi = pl.program_id(1)

        @pl.when(qi == 0)
        def _():
            dk_ref[...] = dk0_ref[...] if init else jnp.zeros_like(dk_ref)
            dv_ref[...] = dv0_ref[...] if init else jnp.zeros_like(dv_ref)

        rr = lax.broadcasted_iota(jnp.int32, (w, w), 0)
        cc = lax.broadcasted_iota(jnp.int32, (w, w), 1)
        t_suf = (rr >= cc).astype(BF16)
        t_pre = (rr <= cc).astype(BF16)
        tq = qi * bq + lax.broadcasted_iota(jnp.int32, (bq, w), 0)
        tk = lax.broadcasted_iota(jnp.int32, (bq, w), 1)
        lane = lax.broadcasted_iota(jnp.int32, (bq, LANES), 1)
        lane1 = lax.broadcasted_iota(jnp.int32, (1, LANES), 1)
        ntot = (qi + 1) * nsub
        for hh in range(hpb):
            hs = slice(hh * dh, (hh + 1) * dh)
            qb = q_ref[:, hs]
            dob = do_ref[:, hs].astype(BF16)
            rs_ref[...] = jnp.zeros_like(rs_ref)
            rs_ref[:, 0:nkb] = runs_ref[hh]
            colmax = jnp.max(rs_ref[...], axis=0, keepdims=True)
            dead = jnp.logical_and(jnp.logical_and(lane1 >= 1, lane1 <= ntot), colmax <= STICK_GONE)
            kb0 = jnp.minimum(jnp.sum(dead.astype(jnp.int32)), ntot - nsub)

            def block(kb, pg_run, dq, masked):
                kstart = pl.multiple_of(kb * w, w)
                kblk = k_ref[pl.ds(kstart, w), hs]
                vblk = v_ref[pl.ds(kstart, w), hs]
                right = jnp.sum(jnp.where(lane == kb + 1, rs_ref[...], 0.0), axis=1, keepdims=True)
                z = lax.dot_general(qb, kblk, NT, preferred_element_type=F32)
                lg, e = _neg_softplus(z)
                if masked:
                    m = (tk + kstart) < tq
                    lg = jnp.where(m, lg, 0.0)
                a = jnp.exp(z + _split_dot(lg, t_suf) + right)
                if masked:
                    a = jnp.where(m, a, 0.0)
                da = lax.dot_general(dob, vblk, NT, preferred_element_type=F32)
                g = da * a
                pin = _split_dot(g, t_pre) + pg_run
                sig = jnp.where(z >= 0.0, 1.0, e) / (1.0 + e)
                dz = g - sig * pin
                if masked:
                    dz = jnp.where(m, dz, 0.0)
                dzb = dz.astype(BF16)
                dq = dq + jnp.dot(dzb, kblk, preferred_element_type=F32)
                dk_ref[pl.ds(kstart, w), hs] += lax.dot_general(dzb, qb, TN, preferred_element_type=F32)
                dv_ref[pl.ds(kstart, w), hs] += lax.dot_general(a.astype(BF16), dob, TN, preferred_element_type=F32)
                return pin[:, w - 1:w], dq

            def loop(kb, carry):
                return block(kb, carry[0], carry[1], False)
            pg_run, dq = lax.fori_loop(kb0, qi * nsub, loop, (jnp.zeros((bq, 1), F32), jnp.zeros((bq, dh), F32)))
            for sb in range(nsub):
                pg_run, dq = block(qi * nsub + sb, pg_run, dq, True)
            dq_ref[:, hs] = dq * scale

    qs = pl.BlockSpec((bq, LANES), lambda h, i: (i, h))
    kv = pl.BlockSpec((S, LANES), lambda h, i: (0, h))
    ins = [q, k, v, do, runs] + ([dk0, dv0] if init else [])
    specs = [qs, kv, kv, qs, pl.BlockSpec((hpb, bq, nkb), lambda h, i: (h, i, 0))] + ([kv, kv] if init else [])
    sds = jax.ShapeDtypeStruct((S, D), F32)
    return pl.pallas_call(body, grid=(D // LANES, S // bq), in_specs=specs, out_specs=(qs, kv, kv), out_shape=(sds, sds, sds),
                          scratch_shapes=[pltpu.VMEM((bq, LANES), F32)],
                          compiler_params=_params(("parallel", "arbitrary")), name=name)(*ins)


def loss_head(name, y, tgt, tm=512):
    S, D = y.shape
    tm = _tile(S, tm)

    def body(y_ref, t_ref, dy_ref, acc_ref):
        @pl.when(pl.program_id(0) == 0)
        def _():
            acc_ref[...] = jnp.zeros_like(acc_ref)
        e = y_ref[...] - t_ref[...]
        dy_ref[...] = e * (1.0 / D)
        acc_ref[...] += jnp.sum(e * e)

    row = pl.BlockSpec((tm, D), lambda i: (i, 0))
    return pl.pallas_call(body, grid=(S // tm,), in_specs=[row, row],
                          out_specs=(row, pl.BlockSpec((8, LANES), lambda i: (0, 0))),
                          out_shape=(jax.ShapeDtypeStruct((S, D), F32), jax.ShapeDtypeStruct((8, LANES), F32)),
                          compiler_params=_params(("arbitrary",)), name=name)(y, tgt)


def _to_sm(a, axis=-1):
    axis = axis % a.ndim
    shp = a.shape[:axis] + (N_CHIPS, a.shape[axis] // N_CHIPS) + a.shape[axis + 1:]
    return jnp.moveaxis(a.reshape(shp), axis, 0)


def _from_sm(a, axis=-1):
    nd = a.ndim - 1
    axis = axis % nd
    b = jnp.moveaxis(a, 0, axis)
    return b.reshape(b.shape[:axis] + (b.shape[axis] * b.shape[axis + 1],) + b.shape[axis + 2:])


def forward_backward(x, p4, tgt, W):
    S, D = x.shape
    scale = 1.0 / math.sqrt(HEAD_DIM)
    saved = []
    kh = vh = x_kv = None
    for i in range(DEPTH):
        sv = {'x': x}
        if i < N_A:
            h_sm = mm_colsm(f"pw1_{i}", x, W['a_pw1_w'], i, W['a_pw1_b'])
            u = glu_fwd(f"glu_{i}", h_sm)
            c, s = conv_ln_silu_fwd(f"convln_{i}", u, W['a_dw_w'], W['a_dw_b'], W['a_ln_g'], W['a_ln_b'], i)
            mix = mm_rowsm(f"pw2_{i}", s, W['a_pw2_w'], i, bias=W['a_pw2_b'])
            sv.update(h_sm=h_sm, u=u, c=c, s=s)
        else:
            j = i - N_A
            if kh is None:
                x_kv = x
                kh = mm_rowsm("wk", x, W['kv_wk'], 0, out_dtype=BF16)
                vh = mm_rowsm("wv", x, W['kv_wv'], 0, out_dtype=BF16)
            qh = mm_rowsm(f"wq_{j}", x, W['b_wq'], j, out_dtype=BF16, out_scale=scale)
            o, runs = attn_fwd(f"attn_{j}", qh, kh, vh)
            mix = mm_rowsm(f"wo_{j}", o, W['b_wo'], j)
            sv.update(qh=qh, o=o, runs=runs)
        x1, xh1, rs1 = ln_fwd(f"lnmix_{i}", x, mix, W['ln_mix_g'], W['ln_mix_b'], i)
        up_sm = mm_colsm(f"up_{i}", x1, W['ffn_w_up'], i)
        gp_sm = mm_colsm(f"gate_{i}", x1, W['ffn_w_gate'], i)
        hf_sm = ffn_gate_fwd(f"ffngate_{i}", up_sm, gp_sm, W['ffn_conv_w'], W['ffn_conv_b'], i)
        ffn = mm_rowsm(f"down_{i}", hf_sm, W['ffn_w_down'], i, a_sm=True)
        tg = mm_rowsm(f"plegate_{i}", x1, W['ple_w_gate'], i)
        pp = mm_proj(f"pleproj_{i}", p4, i, W['ple_w_proj'])
        x2, xh2, rs2 = ln_fwd(f"lnffn_{i}", x1, ffn, W['ln_ffn_g'], W['ln_ffn_b'], i, tg=tg, pp=pp)
        sv.update(x1=x1, xh1=xh1, rs1=rs1, up_sm=up_sm, gp_sm=gp_sm, hf_sm=hf_sm, tg=tg, pp=pp, xh2=xh2, rs2=rs2)
        saved.append(sv)
        x = x2

    dx, lacc = loss_head("loss", x, tgt)
    loss_sum = lacc[0, 0]

    G = {n: [None] * DEPTH for n in WEIGHTS}
    dk = dv = None
    for i in reversed(range(DEPTH)):
        sv = saved[i]
        dr, acc, dtg, dpp = ln_bwd(f"lnffn_b_{i}", dx, sv['xh2'], sv['rs2'], W['ln_ffn_g'], i, tg=sv['tg'], pp=sv['pp'])
        G['ln_ffn_g'][i], G['ln_ffn_b'][i] = acc[0], acc[1]
        G['ple_w_proj'][i] = mm_tn_proj(f"dproj_{i}", p4, i, dpp)
        G['ple_w_gate'][i] = mm_tn_rowsm(f"dplegate_{i}", sv['x1'], dtg)
        G['ffn_w_down'][i] = mm_tn_rowsm(f"ddown_{i}", sv['hf_sm'], dr, a_sm=True)
        dhf_sm = mm_nt_rowsm(f"dhf_{i}", dr, W['ffn_w_down'], i, out_sm=True)
        dup_sm, dgc_sm = ffn_gate_bwd_a(f"ffngate_ba_{i}", dhf_sm, sv['up_sm'], sv['gp_sm'], W['ffn_conv_w'], W['ffn_conv_b'], i)
        dgp_sm, cacc = ffn_gate_bwd_b(f"ffngate_bb_{i}", dgc_sm, sv['gp_sm'], W['ffn_conv_w'], i)
        kw = W['ffn_conv_w'].shape[2]
        G['ffn_conv_w'][i] = cacc[:, 0:kw, :]
        G['ffn_conv_b'][i] = cacc[:, 7, :].reshape(-1)
        G['ffn_w_up'][i] = mm_tn_colsm(f"dup_{i}", sv['x1'], dup_sm)
        G['ffn_w_gate'][i] = mm_tn_colsm(f"dgate_{i}", sv['x1'], dgp_sm)
        dx1 = mm_nt_rowsm(f"dx1a_{i}", dtg, W['ple_w_gate'], i, res=dr, res_alpha=DN_ALPHA)
        dx1 = mm_nt_colsm(f"dx1b_{i}", dup_sm, W['ffn_w_up'], i, res=dx1)
        dx1 = mm_nt_colsm(f"dx1c_{i}", dgp_sm, W['ffn_w_gate'], i, res=dx1)

        dr1, acc1 = ln_bwd(f"lnmix_b_{i}", dx1, sv['xh1'], sv['rs1'], W['ln_mix_g'], i)
        G['ln_mix_g'][i], G['ln_mix_b'][i] = acc1[0], acc1[1]
        xin = sv['x']
        if i < N_A:
            G['a_pw2_b'][i] = acc1[2]
            G['a_pw2_w'][i] = mm_tn_rowsm(f"dpw2_{i}", sv['s'], dr1)
            ds = mm_nt_rowsm(f"ds_{i}", dr1, W['a_pw2_w'], i)
            dc, cacc = ln_silu_bwd(f"lnsilu_b_{i}", ds, sv['c'], W['a_ln_g'], W['a_ln_b'], i)
            G['a_ln_g'][i], G['a_ln_b'][i], G['a_dw_b'][i] = cacc[0], cacc[1], cacc[2]
            da, dg, dw, dba, dbg = conv_glu_bwd(f"convglu_b_{i}", dc, sv['u'], sv['h_sm'], W['a_dw_w'], i)
            kw = W['a_dw_w'].shape[1]
            G['a_dw_w'][i] = _from_sm(dw[:, 0:kw, :], axis=-1)
            dh_sm = jnp.concatenate([da, dg], axis=0)
            G['a_pw1_b'][i] = jnp.concatenate([dba[:, 0, :], dbg[:, 0, :]], axis=0)
            G['a_pw1_w'][i] = mm_tn_colsm(f"dpw1_{i}", xin, dh_sm)
            dx = mm_nt_colsm(f"dxa_{i}", dh_sm, W['a_pw1_w'], i, res=dr1, res_alpha=DN_ALPHA)
        else:
            j = i - N_A
            G['b_wo'][j] = mm_tn_rowsm(f"dwo_{j}", sv['o'], dr1)
            do = mm_nt_rowsm(f"do_{j}", dr1, W['b_wo'], j)
            dq, dk, dv = attn_bwd(f"attn_b_{j}", sv['qh'], kh, vh, do, sv['runs'], dk, dv)
            G['b_wq'][j] = mm_tn_rowsm(f"dwq_{j}", xin, dq)
            dx = mm_nt_rowsm(f"dxq_{j}", dq, W['b_wq'], j, res=dr1, res_alpha=DN_ALPHA)
            if j == 0:
                G['kv_wk'][0] = mm_tn_rowsm("dwk", x_kv, dk)
                G['kv_wv'][0] = mm_tn_rowsm("dwv", x_kv, dv)
                dx = mm_nt_rowsm("dxk", dk, W['kv_wk'], 0, res=dx)
                dx = mm_nt_rowsm("dxv", dv, W['kv_wv'], 0, res=dx)
    return loss_sum, dx, G


MESH = pl.DeviceIdType.MESH
HBM = pl.BlockSpec(memory_space=pltpu.HBM)


def _place():
    x, y, c = lax.axis_index("x"), lax.axis_index("y"), lax.axis_index("c")
    others = [(1 - x, y), (x, 1 - y), (1 - x, 1 - y)]
    return x, y, c, others


def allgather_chips(name, arrs):
    n = len(arrs)

    def body(*refs):
        ins, outs = refs[:n], refs[n:2 * n]
        send_sems, recv_sems = refs[2 * n:]
        x, y, c, others = _place()
        me = 2 * x + y
        sibling = (x, y, 1 - c)

        def remote(a, k, src, chip_id, half, to):
            return pltpu.make_async_remote_copy(src_ref=src, dst_ref=outs[a].at[chip_id, half], send_sem=send_sems.at[a, k],
                                                recv_sem=recv_sems.at[a, k], device_id=to, device_id_type=MESH)

        first = [remote(a, k, ins[a].at[c], me, c, (ch[0], ch[1], c)) for a in range(n) for k, ch in enumerate(others)]
        for cp in first:
            cp.start()
        passed = []
        for a in range(n):
            for k, ch in enumerate(others):
                cid = 2 * ch[0] + ch[1]
                remote(a, k, ins[a].at[c], cid, c, sibling).wait_recv()
                fwd = remote(a, 3 + k, outs[a].at[cid, c], cid, c, sibling)
                fwd.start()
                passed.append(fwd)
        for a in range(n):
            for k, ch in enumerate(others):
                cid = 2 * ch[0] + ch[1]
                remote(a, 3 + k, ins[a].at[c], cid, 1 - c, sibling).wait_recv()
        for cp in first + passed:
            cp.wait_send()

    outs = pl.pallas_call(body, out_shape=tuple(jax.ShapeDtypeStruct((N_CHIPS,) + a.shape, a.dtype) for a in arrs),
                          in_specs=[HBM] * n, out_specs=tuple([HBM] * n),
                          scratch_shapes=[pltpu.SemaphoreType.DMA((n, 6)), pltpu.SemaphoreType.DMA((n, 6))],
                          name=name)(*arrs)
    me = 2 * lax.axis_index("x") + lax.axis_index("y")
    return [lax.dynamic_update_index_in_dim(o, a, me, 0) for o, a in zip(outs, arrs)]


def exchange_sibling(name, gs):
    n = len(gs)

    def body(*refs):
        g_refs, o_refs = refs[:n], refs[n:2 * n]
        send_sems, recv_sems = refs[2 * n:]
        x, y, c, _ = _place()
        cps = [pltpu.make_async_remote_copy(src_ref=g_refs[a].at[j, 1 - c], dst_ref=o_refs[a].at[j], send_sem=send_sems.at[a, j],
                                            recv_sem=recv_sems.at[a, j], device_id=(x, y, 1 - c), device_id_type=MESH)
               for a in range(n) for j in range(N_CHIPS)]
        for cp in cps:
            cp.start()
        for cp in cps:
            cp.wait()

    return pl.pallas_call(body, out_shape=tuple(jax.ShapeDtypeStruct((N_CHIPS,) + g.shape[2:], g.dtype) for g in gs),
                          in_specs=[HBM] * n, out_specs=tuple([HBM] * n),
                          scratch_shapes=[pltpu.SemaphoreType.DMA((n, N_CHIPS)), pltpu.SemaphoreType.DMA((n, N_CHIPS))],
                          name=name)(*gs)


def scatter_chips(name, ss):
    n = len(ss)

    def body(*refs):
        s_refs, o_refs = refs[:n], refs[n:2 * n]
        send_sems, recv_sems = refs[2 * n:]
        x, y, c, others = _place()
        cps = [pltpu.make_async_remote_copy(src_ref=s_refs[a].at[2 * ch[0] + ch[1]], dst_ref=o_refs[a].at[k],
                                            send_sem=send_sems.at[a, k], recv_sem=recv_sems.at[a, k],
                                            device_id=(ch[0], ch[1], c), device_id_type=MESH)
               for a in range(n) for k, ch in enumerate(others)]
        for cp in cps:
            cp.start()
        for cp in cps:
            cp.wait()

    return pl.pallas_call(body, out_shape=tuple(jax.ShapeDtypeStruct((3,) + s.shape[1:], s.dtype) for s in ss),
                          in_specs=[HBM] * n, out_specs=tuple([HBM] * n),
                          scratch_shapes=[pltpu.SemaphoreType.DMA((n, 3)), pltpu.SemaphoreType.DMA((n, 3))], name=name)(*ss)


def share_sibling(name, ts):
    n = len(ts)

    def body(*refs):
        o_refs = refs[n:2 * n]
        send_sems, recv_sems = refs[2 * n:]
        x, y, c, _ = _place()
        cps = [pltpu.make_async_remote_copy(src_ref=o_refs[a].at[c], dst_ref=o_refs[a].at[c], send_sem=send_sems.at[a],
                                            recv_sem=recv_sems.at[a], device_id=(x, y, 1 - c), device_id_type=MESH)
               for a in range(n)]
        for cp in cps:
            cp.start()
        for a in range(n):
            pltpu.make_async_remote_copy(src_ref=o_refs[a].at[c], dst_ref=o_refs[a].at[1 - c], send_sem=send_sems.at[a],
                                         recv_sem=recv_sems.at[a], device_id=(x, y, 1 - c), device_id_type=MESH).wait_recv()
        for cp in cps:
            cp.wait_send()

    return pl.pallas_call(body, out_shape=tuple(jax.ShapeDtypeStruct(t.shape, t.dtype) for t in ts),
                          in_specs=[HBM] * n, out_specs=tuple([HBM] * n), input_output_aliases={a: a for a in range(n)},
                          scratch_shapes=[pltpu.SemaphoreType.DMA((n,)), pltpu.SemaphoreType.DMA((n,))],
                          name=name)(*ts)


def add_halves(name, g, recv, place, out_dtype, tr=512):
    _, _, R, C = g.shape
    tr = _tile(R, tr, 16)

    def body(p_ref, a_ref, b_ref, o_ref):
        o_ref[...] = (a_ref[...] + b_ref[...]).astype(o_ref.dtype)

    blk = pl.BlockSpec((None, tr, C), lambda j, i, p: (j, i, 0))
    gs = pltpu.PrefetchScalarGridSpec(num_scalar_prefetch=1, grid=(N_CHIPS, R // tr),
                                      in_specs=[pl.BlockSpec((None, None, tr, C), lambda j, i, p: (j, p[0], i, 0)), blk],
                                      out_specs=blk)
    return pl.pallas_call(body, grid_spec=gs, out_shape=jax.ShapeDtypeStruct((N_CHIPS, R, C), out_dtype),
                          compiler_params=_params(("parallel", "parallel")), name=name)(place, g, recv)


def add_chips(name, g, r1, r2, place, tr=512):
    _, _, R, C = g.shape
    tr = _tile(R, tr, 16)

    def body(p_ref, a_ref, b_ref, c_ref, o_ref):
        s = a_ref[...] + b_ref[...]
        o_ref[...] = ((s + c_ref[0].astype(F32)) + c_ref[1].astype(F32)) + c_ref[2].astype(F32)

    gs = pltpu.PrefetchScalarGridSpec(num_scalar_prefetch=1, grid=(R // tr,),
                                      in_specs=[pl.BlockSpec((None, None, tr, C), lambda i, p: (p[1], p[0], i, 0)),
                                                pl.BlockSpec((None, tr, C), lambda i, p: (p[1], i, 0)),
                                                pl.BlockSpec((3, tr, C), lambda i, p: (0, i, 0))],
                                      out_specs=pl.BlockSpec((None, tr, C), lambda i, p: (p[0], i, 0)))
    return pl.pallas_call(body, grid_spec=gs, out_shape=jax.ShapeDtypeStruct((2, R, C), F32),
                          compiler_params=_params(("parallel",)), name=name)(place, g, r1, r2)


def reduce_scatter(gs, wire_dtypes, place):
    r1 = exchange_sibling("rs_sibling", gs)
    s1 = [add_halves(f"rs_add_cores_{a}", g, r, place, dt) for a, (g, r, dt) in enumerate(zip(gs, r1, wire_dtypes))]
    r2 = scatter_chips("rs_chips", s1)
    tot = [add_chips(f"rs_add_chips_{a}", g, r, rr, place) for a, (g, r, rr) in enumerate(zip(gs, r1, r2))]
    return share_sibling("rs_share", tot)


def adamw(name, w, g, m, v, tr=512):
    shp = w.shape
    cols = shp[-1]
    w2, g2, m2, v2 = (a.reshape(-1, cols) for a in (w, g, m, v))
    rows = w2.shape[0]
    tr = _tile(rows, tr)

    def body(w_ref, g_ref, m_ref, v_ref, d_ref, mo_ref, vo_ref):
        g_ = g_ref[...]
        m_ = ADAM_B1 * m_ref[...] + (1.0 - ADAM_B1) * g_
        v_ = ADAM_B2 * v_ref[...] + (1.0 - ADAM_B2) * (g_ * g_)
        m_hat = m_ / (1.0 - ADAM_B1 ** ADAM_STEP)
        v_hat = v_ / (1.0 - ADAM_B2 ** ADAM_STEP)
        d_ref[...] = -ADAM_LR * (m_hat / (jnp.sqrt(v_hat) + ADAM_EPS) + ADAM_WD * w_ref[...])
        mo_ref[...] = m_
        vo_ref[...] = v_

    blk = pl.BlockSpec((tr, cols), lambda i: (i, 0))
    sds = jax.ShapeDtypeStruct((rows, cols), F32)
    d, mo, vo = pl.pallas_call(body, grid=(rows // tr,), in_specs=[blk] * 4, out_specs=(blk, blk, blk), out_shape=(sds, sds, sds),
                               compiler_params=_params(("parallel",)), name=name)(w2, g2, m2, v2)
    return d.reshape(shp), mo.reshape(shp), vo.reshape(shp)


PACK_ALIGN = 1024


def _pad_to(a, mult, axis=-1):
    axis = axis % a.ndim
    extra = (-a.shape[axis]) % mult
    if extra == 0:
        return a
    pads = [(0, 0)] * a.ndim
    pads[axis] = (0, extra)
    return jnp.pad(a, pads)


def _pack(pieces, lead, row_mult):
    nl = len(lead)
    flat, offs, sizes, off = [], [], [], 0
    for a in pieces:
        f = a.reshape(lead + (-1,))
        sizes.append(f.shape[-1])
        f = _pad_to(f, PACK_ALIGN)
        offs.append(off)
        off += f.shape[-1]
        flat.append(f)
    cat = _pad_to(jnp.concatenate(flat, axis=nl), 2 * row_mult * LANES)
    return cat.reshape(lead + (2, -1, LANES)), offs, sizes


def _unpack(packed, lead, offs, sizes, shapes):
    flat = packed.reshape(lead + (-1,))
    return [lax.slice_in_dim(flat, o, o + s, axis=len(lead)).reshape(lead + tuple(shp)) for o, s, shp in zip(offs, sizes, shapes)]


def _stack_grads(G, names):
    out = {}
    for n in names:
        parts = [g for g in G[n] if g is not None]
        if n in ('kv_wk', 'kv_wv'):
            out[n] = parts[0]
        elif n in REPLICATED:
            out[n] = jnp.stack(parts, axis=0).reshape(N_CHIPS, -1)
        elif n in ('a_dw_w', 'a_dw_b', 'a_ln_g', 'a_ln_b', 'a_pw2_b'):
            out[n] = _to_sm(jnp.stack(parts, axis=0), axis=-1)
        else:
            out[n] = jnp.stack(parts, axis=1)
    return out


def _group_rows(G, names):
    groups = {}
    for n in names:
        parts = [g for g in G[n] if g is not None]
        groups.setdefault(parts[0].shape[-1], []).append((n, parts))
    arrays, members = [], []
    for cols, items in groups.items():
        pieces, where, off = [], [], 0
        for n, parts in items:
            rows = sum(q.shape[1] for q in parts)
            where.append((n, off, rows))
            off += rows
            pieces += parts
        assert off % 32 == 0, (cols, off)
        arrays.append(jnp.concatenate(pieces, axis=1).reshape(N_CHIPS, 2, off // 2, cols))
        members.append(where)
    return arrays, members


def _whole_weights(big, small, rep, D):
    W = {}
    for n in BIG:
        a = big[n]
        W[n] = a[:, None] if n in ('kv_wk', 'kv_wv') else a
    W['a_pw1_b'] = small['a_pw1_b'][:, :, None, :]
    W['a_dw_w'] = _from_sm(small['a_dw_w'], axis=-1)
    for n in ('a_dw_b', 'a_ln_g', 'a_ln_b', 'a_pw2_b'):
        W[n] = _from_sm(small[n], axis=-1)[:, None, :]
    W['ffn_conv_w'] = small['ffn_conv_w']
    L, F = rep['ffn_conv_b'].shape
    W['ffn_conv_b'] = rep['ffn_conv_b'].reshape(L, N_CHIPS, 1, F // N_CHIPS)
    for n in ('ln_mix_g', 'ln_mix_b', 'ln_ffn_g', 'ln_ffn_b'):
        W[n] = rep[n][:, None, :]
    return W


SMALL = ('a_pw1_b', 'a_dw_w', 'a_dw_b', 'a_ln_g', 'a_ln_b', 'a_pw2_b', 'ffn_conv_w')


def _step(x, p, loss_target, w, m, v):
    S, D = x.shape[-2:]
    x2, tgt = x.reshape(S, D), loss_target.reshape(S, D)
    place = jnp.stack([lax.axis_index("c"), 2 * lax.axis_index("x") + lax.axis_index("y")]).astype(jnp.int32)

    big_in = [w[n].astype(BF16).reshape((2, -1) + w[n].shape[1:] if w[n].ndim == 3 else (2, -1, w[n].shape[-1])) for n in BIG]
    small_in, s_offs, s_sizes = _pack([w[n] for n in SMALL], (), 8)
    gathered = allgather_chips("gather_weights", big_in + [small_in])
    big = {n: g.reshape((N_CHIPS,) + w[n].shape) for n, g in zip(BIG, gathered[:-1])}
    small = dict(zip(SMALL, _unpack(gathered[-1], (N_CHIPS,), s_offs, s_sizes, [w[n].shape for n in SMALL])))
    W = _whole_weights(big, small, {n: w[n] for n in REPLICATED}, D)

    loss_sum, dx, G = forward_backward(x2, p, tgt, W)
    loss = lax.psum(0.5 * loss_sum / D, ("x", "y", "c"))

    vectors = [n for n in WEIGHTS if n not in BIG]
    mats, members = _group_rows(G, BIG)
    g_sm = _stack_grads(G, vectors)
    packed, offs, sizes = _pack([g_sm[n] for n in vectors], (N_CHIPS,), 512)
    reduced = reduce_scatter(mats + [packed], [BF16] * len(mats) + [F32], place)
    shapes = [w[n].shape if n not in REPLICATED else (w[n].size // N_CHIPS,) for n in vectors]
    g_mine = dict(zip(vectors, _unpack(reduced[-1], (), offs, sizes, shapes)))
    for red, where in zip(reduced[:-1], members):
        rows = red.reshape(-1, red.shape[-1])
        for n, off, cnt in where:
            g_mine[n] = lax.slice_in_dim(rows, off, off + cnt, axis=0).reshape(w[n].shape)
    rep_in, r_offs, r_sizes = _pack([g_mine[n] for n in REPLICATED], (), 8)
    rep_all = allgather_chips("gather_replicated_grads", [rep_in])[0]
    for n, g in zip(REPLICATED, _unpack(rep_all, (N_CHIPS,), r_offs, r_sizes, [(w[n].size // N_CHIPS,) for n in REPLICATED])):
        g_mine[n] = g.reshape(w[n].shape)

    grads, deltas, new_m, new_v = [], [], [], []
    for n in WEIGHTS:
        d, mo, vo = adamw(f"adamw_{n}", w[n], g_mine[n], m[n], v[n])
        grads.append(g_mine[n])
        deltas.append(d)
        new_m.append(mo)
        new_v.append(vo)
    return (loss, dx.reshape(x.shape), *grads, *deltas, *new_m, *new_v)


def kernel(x, p, a_pw1_w, a_pw1_b, a_dw_w, a_dw_b, a_ln_g, a_ln_b, a_pw2_w, a_pw2_b, b_wq, kv_wk, kv_wv, b_wo, ln_mix_g, ln_mix_b, ffn_w_up, ffn_w_gate, ffn_conv_w, ffn_conv_b, ffn_w_down, ple_w_gate, ple_w_proj, ln_ffn_g, ln_ffn_b, loss_target, m_a_pw1_w, m_a_pw1_b, m_a_dw_w, m_a_dw_b, m_a_ln_g, m_a_ln_b, m_a_pw2_w, m_a_pw2_b, m_b_wq, m_kv_wk, m_kv_wv, m_b_wo, m_ln_mix_g, m_ln_mix_b, m_ffn_w_up, m_ffn_w_gate, m_ffn_conv_w, m_ffn_conv_b, m_ffn_w_down, m_ple_w_gate, m_ple_w_proj, m_ln_ffn_g, m_ln_ffn_b, v_a_pw1_w, v_a_pw1_b, v_a_dw_w, v_a_dw_b, v_a_ln_g, v_a_ln_b, v_a_pw2_w, v_a_pw2_b, v_b_wq, v_kv_wk, v_kv_wv, v_b_wo, v_ln_mix_g, v_ln_mix_b, v_ffn_w_up, v_ffn_w_gate, v_ffn_conv_w, v_ffn_conv_b, v_ffn_w_down, v_ple_w_gate, v_ple_w_proj, v_ln_ffn_g, v_ln_ffn_b):
    vals = dict(locals())
    w = {n: vals[n] for n in WEIGHTS}
    m = {n: vals["m_" + n] for n in WEIGHTS}
    v = {n: vals["v_" + n] for n in WEIGHTS}
    return _step(x, p, loss_target, w, m, v)
```

```python
import functools
import math

import jax
import jax.numpy as jnp
import numpy as np
from jax import lax
from jax.experimental import pallas as pl
from jax.experimental.pallas import tpu as pltpu

F32, BF16 = jnp.float32, jnp.bfloat16

HEAD_DIM = 64
LN_EPS = 1e-5
DEPTH = 4
N_A = DEPTH // 2
DN_ALPHA = (2.0 * DEPTH) ** 0.25
N_CHIPS = 4

ADAM_LR, ADAM_B1, ADAM_B2, ADAM_EPS, ADAM_WD, ADAM_STEP = 0.001, 0.9, 0.999, 1e-08, 0.01, 10

VMEM_LIMIT_BYTES = 56 * 2**20
LANES = 128
CONV_HALO = 32
FFN_HALO = 8

NN = (((1,), (0,)), ((), ()))
NT = (((1,), (1,)), ((), ()))
TN = (((0,), (0,)), ((), ()))

WEIGHTS = ['a_pw1_w', 'a_pw1_b', 'a_dw_w', 'a_dw_b', 'a_ln_g', 'a_ln_b', 'a_pw2_w', 'a_pw2_b', 'b_wq', 'kv_wk', 'kv_wv',
           'b_wo', 'ln_mix_g', 'ln_mix_b', 'ffn_w_up', 'ffn_w_gate', 'ffn_conv_w', 'ffn_conv_b', 'ffn_w_down', 'ple_w_gate',
           'ple_w_proj', 'ln_ffn_g', 'ln_ffn_b']
REPLICATED = ('ln_mix_g', 'ln_mix_b', 'ffn_conv_b', 'ln_ffn_g', 'ln_ffn_b')
BIG = ('a_pw1_w', 'a_pw2_w', 'b_wq', 'kv_wk', 'kv_wv', 'b_wo', 'ffn_w_up', 'ffn_w_gate', 'ffn_w_down', 'ple_w_gate',
       'ple_w_proj')


def _tile(n, pref, mult=8):
    t = min(n, pref)
    while t > 0:
        if n % t == 0 and t % mult == 0:
            return t
        t -= 1
    return n


def _params(sem):
    return pltpu.CompilerParams(dimension_semantics=sem, vmem_limit_bytes=VMEM_LIMIT_BYTES)


def _sigmoid(x):
    return 1.0 / (1.0 + jnp.exp(-x))


def _mm_call(name, grid, terms, out_spec, out_sds, dims, bias=None, res=None, res_alpha=1.0, out_scale=None, into=None):
    n_terms = len(terms)

    def body(*refs):
        o_ref = refs[-1]
        acc = None
        for t in range(n_terms):
            a = refs[2 * t][...].astype(BF16)
            b = refs[2 * t + 1][...].astype(BF16)
            d = lax.dot_general(a, b, dims, preferred_element_type=F32)
            acc = d if acc is None else acc + d
        k = 2 * n_terms
        if bias is not None:
            acc = acc + refs[k][...]
            k += 1
        if res is not None:
            acc = acc + res_alpha * refs[k][...]
        if out_scale is not None:
            acc = acc * out_scale
        o_ref[...] = acc.astype(o_ref.dtype)

    operands, specs = [], []
    for a, a_spec, b, b_spec in terms:
        operands += [a, b]
        specs += [a_spec, b_spec]
    for extra in (bias, res):
        if extra is not None:
            operands.append(extra[0])
            specs.append(extra[1])
    aliases = {}
    if into is not None:
        aliases = {len(operands): 0}
        operands.append(into)
        specs.append(pl.BlockSpec(memory_space=pl.ANY))
        out_sds = jax.ShapeDtypeStruct(into.shape, into.dtype)
    return pl.pallas_call(body, out_shape=out_sds, grid=grid, in_specs=specs, out_specs=out_spec, input_output_aliases=aliases,
                          compiler_params=_params(("parallel",) * len(grid)), name=name)(*operands)


def mm_colsm(name, x, w4, layer, bias4=None, tm=512):
    M, K = x.shape
    n = w4.shape[-1]
    tm = _tile(M, tm)
    terms = [(x, pl.BlockSpec((tm, K), lambda j, i: (i, 0)), w4, pl.BlockSpec((None, None, K, n), lambda j, i: (j, layer, 0, 0)))]
    bias = None if bias4 is None else (bias4, pl.BlockSpec((None, None, 1, n), lambda j, i: (j, layer, 0, 0)))
    return _mm_call(name, (N_CHIPS, M // tm), terms, pl.BlockSpec((None, tm, n), lambda j, i: (j, i, 0)),
                    jax.ShapeDtypeStruct((N_CHIPS, M, n), F32), NN, bias=bias)


def mm_rowsm(name, a, w4, layer, a_sm=False, bias=None, out_dtype=F32, out_scale=None, tm=512):
    kc, N = w4.shape[-2:]
    M = a.shape[-2]
    tm = _tile(M, tm)
    terms = []
    for j in range(N_CHIPS):
        if a_sm:
            a_spec = pl.BlockSpec((None, tm, kc), lambda i, j=j: (j, i, 0))
        else:
            a_spec = pl.BlockSpec((tm, kc), lambda i, j=j: (i, j))
        terms.append((a, a_spec, w4, pl.BlockSpec((None, None, kc, N), lambda i, j=j: (j, layer, 0, 0))))
    b = None if bias is None else (bias, pl.BlockSpec((None, 1, N), lambda i: (layer, 0, 0)))
    return _mm_call(name, (M // tm,), terms, pl.BlockSpec((tm, N), lambda i: (i, 0)), jax.ShapeDtypeStruct((M, N), out_dtype),
                    NN, bias=b, out_scale=out_scale)


def mm_nt_rowsm(name, dy, w4, layer, out_sm=False, res=None, res_alpha=1.0, tm=512):
    kc, N = w4.shape[-2:]
    M = dy.shape[0]
    tm = _tile(M, tm)
    terms = [(dy, pl.BlockSpec((tm, N), lambda j, i: (i, 0)), w4, pl.BlockSpec((None, None, kc, N), lambda j, i: (j, layer, 0, 0)))]
    if out_sm:
        out_spec, sds = pl.BlockSpec((None, tm, kc), lambda j, i: (j, i, 0)), jax.ShapeDtypeStruct((N_CHIPS, M, kc), F32)
    else:
        out_spec, sds = pl.BlockSpec((tm, kc), lambda j, i: (i, j)), jax.ShapeDtypeStruct((M, N_CHIPS * kc), F32)
    r = None if res is None else (res, pl.BlockSpec((tm, kc), lambda j, i: (i, j)))
    return _mm_call(name, (N_CHIPS, M // tm), terms, out_spec, sds, NT, res=r, res_alpha=res_alpha)


def mm_nt_colsm(name, dy_parts, w4, layer, res=None, res_alpha=1.0, tm=512):
    K, n = w4.shape[-2:]
    M = dy_parts[0][0].shape[1]
    tm = _tile(M, tm)
    terms = [(arr, pl.BlockSpec((None, tm, n), lambda i, idx=idx: (idx, i, 0)), w4,
              pl.BlockSpec((None, None, K, n), lambda i, j=j: (j, layer, 0, 0))) for j, (arr, idx) in enumerate(dy_parts)]
    r = None if res is None else (res, pl.BlockSpec((tm, K), lambda i: (i, 0)))
    return _mm_call(name, (M // tm,), terms, pl.BlockSpec((tm, K), lambda i: (i, 0)), jax.ShapeDtypeStruct((M, K), F32), NT,
                    res=r, res_alpha=res_alpha)


def _sm_parts(a):
    return [(a, j) for j in range(a.shape[0])]


def mm_tn_colsm(name, x, dy, into, off, j0=0, tk=512):
    M, K = x.shape
    nj, _, n = dy.shape
    tk = _tile(K, tk, LANES)
    assert off % tk == 0
    terms = [(x, pl.BlockSpec((M, tk), lambda j, k: (0, k)), dy, pl.BlockSpec((None, M, n), lambda j, k: (j, 0, 0)))]
    return _mm_call(name, (nj, K // tk), terms, pl.BlockSpec((None, tk, n), lambda j, k: (j + j0, off // tk + k, 0)),
                    None, TN, into=into)


def mm_tn_rowsm(name, a, dy, into, off, a_sm=False, tn=512):
    M, N = dy.shape
    kc = a.shape[-1] if a_sm else a.shape[-1] // N_CHIPS
    tn = _tile(N, tn, LANES)
    assert off % kc == 0
    a_spec = pl.BlockSpec((None, M, kc), lambda j, n: (j, 0, 0)) if a_sm else pl.BlockSpec((M, kc), lambda j, n: (0, j))
    terms = [(a, a_spec, dy, pl.BlockSpec((M, tn), lambda j, n: (0, n)))]
    return _mm_call(name, (N_CHIPS, N // tn), terms, pl.BlockSpec((None, kc, tn), lambda j, n: (j, off // kc, n)),
                    None, TN, into=into)


def mm_proj(name, p4, layer, w4, tm=512):
    S, P = p4.shape[-2:]
    n = w4.shape[-1]
    tm = _tile(S, tm)
    terms = [(p4, pl.BlockSpec((None, None, tm, P), lambda j, i: (layer, 0, i, 0)), w4,
              pl.BlockSpec((None, None, P, n), lambda j, i: (j, layer, 0, 0)))]
    return _mm_call(name, (N_CHIPS, S // tm), terms, pl.BlockSpec((tm, n), lambda j, i: (i, j)),
                    jax.ShapeDtypeStruct((S, N_CHIPS * n), F32), NN)


def mm_tn_proj(name, p4, layer, dpp, into, off):
    S, P = p4.shape[-2:]
    n = dpp.shape[-1] // N_CHIPS
    assert off % P == 0
    terms = [(p4, pl.BlockSpec((None, None, S, P), lambda j: (layer, 0, 0, 0)), dpp, pl.BlockSpec((S, n), lambda j: (0, j)))]
    return _mm_call(name, (N_CHIPS,), terms, pl.BlockSpec((None, P, n), lambda j: (j, off // P, 0)), None, TN, into=into)


def ln_fwd(name, x, mix, g, b, layer, tg=None, pp=None, tm=256):
    S, D = x.shape
    tm = _tile(S, tm, 16)
    ple = tg is not None

    def body(*refs):
        if ple:
            x_ref, m_ref, tg_ref, pp_ref, g_ref, b_ref, y_ref, yb_ref, xh_ref, rs_ref = refs
        else:
            x_ref, m_ref, g_ref, b_ref, y_ref, yb_ref, xh_ref, rs_ref = refs
        r = DN_ALPHA * x_ref[...] + m_ref[...]
        if ple:
            r = r + _sigmoid(tg_ref[...]) * pp_ref[...]
        mu = jnp.mean(r, axis=-1, keepdims=True)
        d = r - mu
        var = jnp.mean(d * d, axis=-1, keepdims=True)
        rstd = lax.rsqrt(var + LN_EPS)
        xh = d * rstd
        y = xh * g_ref[...] + b_ref[...]
        y_ref[...] = y
        yb_ref[...] = y.astype(BF16)
        xh_ref[...] = xh
        rs_ref[...] = rstd

    row = pl.BlockSpec((tm, D), lambda i: (i, 0))
    vec = pl.BlockSpec((None, 1, D), lambda i: (layer, 0, 0))
    ins = [x, mix] + ([tg, pp] if ple else []) + [g, b]
    specs = [row, row] + ([row, row] if ple else []) + [vec, vec]
    return pl.pallas_call(body, grid=(S // tm,), in_specs=specs,
                          out_specs=(row, row, row, pl.BlockSpec((tm, 1), lambda i: (i, 0))),
                          out_shape=(jax.ShapeDtypeStruct((S, D), F32), jax.ShapeDtypeStruct((S, D), BF16),
                                     jax.ShapeDtypeStruct((S, D), F32), jax.ShapeDtypeStruct((S, 1), F32)),
                          compiler_params=_params(("parallel",)), name=name)(*ins)


def ln_bwd(name, dy, xh, rstd, g, layer, tg=None, pp=None, tm=256):
    S, D = dy.shape
    tm = _tile(S, tm)
    ple = tg is not None

    def body(*refs):
        if ple:
            dy_ref, xh_ref, rs_ref, g_ref, tg_ref, pp_ref, dr_ref, acc_ref, dtg_ref, dpp_ref = refs
        else:
            dy_ref, xh_ref, rs_ref, g_ref, dr_ref, acc_ref = refs
        dy_, xh_ = dy_ref[...], xh_ref[...]
        dxh = dy_ * g_ref[...]
        m1 = jnp.mean(dxh, axis=-1, keepdims=True)
        m2 = jnp.mean(dxh * xh_, axis=-1, keepdims=True)
        dr = rs_ref[...] * (dxh - m1 - xh_ * m2)
        dr_ref[...] = dr

        @pl.when(pl.program_id(0) == 0)
        def _():
            acc_ref[...] = jnp.zeros_like(acc_ref)
        acc_ref[0:1, :] += jnp.sum(dy_ * xh_, axis=0, keepdims=True)
        acc_ref[1:2, :] += jnp.sum(dy_, axis=0, keepdims=True)
        acc_ref[2:3, :] += jnp.sum(dr, axis=0, keepdims=True)
        if ple:
            pg = _sigmoid(tg_ref[...])
            dtg_ref[...] = (dr * pp_ref[...] * pg * (1.0 - pg)).astype(BF16)
            dpp_ref[...] = (dr * pg).astype(BF16)

    row = pl.BlockSpec((tm, D), lambda i: (i, 0))
    ins = [dy, xh, rstd, g] + ([tg, pp] if ple else [])
    specs = [row, row, pl.BlockSpec((tm, 1), lambda i: (i, 0)), pl.BlockSpec((None, 1, D), lambda i: (layer, 0, 0))] + ([row, row] if ple else [])
    outs = [jax.ShapeDtypeStruct((S, D), F32), jax.ShapeDtypeStruct((8, D), F32)]
    out_specs = [row, pl.BlockSpec((8, D), lambda i: (0, 0))]
    if ple:
        outs += [jax.ShapeDtypeStruct((S, D), BF16)] * 2
        out_specs += [row, row]
    return pl.pallas_call(body, grid=(S // tm,), in_specs=specs, out_specs=tuple(out_specs), out_shape=tuple(outs),
                          compiler_params=_params(("arbitrary",)), name=name)(*ins)


def glu_fwd(name, h_sm, tm=512):
    _, S, n = h_sm.shape
    tm = _tile(S, tm)
    half = N_CHIPS // 2

    def body(a_ref, g_ref, u_ref):
        u_ref[...] = a_ref[...] * _sigmoid(g_ref[...])

    return pl.pallas_call(body, grid=(half, S // tm),
                          in_specs=[pl.BlockSpec((None, tm, n), lambda j, i: (j, i, 0)),
                                    pl.BlockSpec((None, tm, n), lambda j, i: (j + half, i, 0))],
                          out_specs=pl.BlockSpec((tm, n), lambda j, i: (i, j)),
                          out_shape=jax.ShapeDtypeStruct((S, half * n), F32),
                          compiler_params=_params(("parallel", "parallel")), name=name)(h_sm, h_sm)


def conv_ln_silu_fwd(name, u, w, b, g, beta, layer, ts=128):
    S, D = u.shape
    kw = w.shape[1]
    ts = _tile(S, ts, CONV_HALO)
    lc = LANES if D % LANES == 0 else D

    def body(h_ref, u_ref, w_ref, b_ref, g_ref, be_ref, c_ref, s_ref, win_ref):
        i = pl.program_id(0)
        win_ref[0:CONV_HALO, :] = jnp.where(i == 0, 0.0, h_ref[...])
        win_ref[CONV_HALO:, :] = u_ref[...]
        for cc in range(D // lc):
            cs = slice(cc * lc, (cc + 1) * lc)
            acc = jnp.zeros((ts, lc), F32) + b_ref[:, cs]
            for k in range(kw):
                off = CONV_HALO - (kw - 1) + k
                acc = acc + w_ref[k:k + 1, cs] * win_ref[off:off + ts, cs]
            c_ref[:, cs] = acc
        c = c_ref[...]
        mu = jnp.mean(c, axis=-1, keepdims=True)
        d = c - mu
        var = jnp.mean(d * d, axis=-1, keepdims=True)
        nrm = d * lax.rsqrt(var + LN_EPS) * g_ref[...] + be_ref[...]
        s_ref[...] = (nrm * _sigmoid(nrm)).astype(BF16)

    row = pl.BlockSpec((ts, D), lambda i: (i, 0))
    vec = pl.BlockSpec((None, 1, D), lambda i: (layer, 0, 0))
    halo = pl.BlockSpec((CONV_HALO, D), lambda i: (jnp.maximum(i * (ts // CONV_HALO) - 1, 0), 0))
    return pl.pallas_call(body, grid=(S // ts,),
                          in_specs=[halo, row, pl.BlockSpec((None, kw, D), lambda i: (layer, 0, 0)), vec, vec, vec],
                          out_specs=(row, row),
                          out_shape=(jax.ShapeDtypeStruct((S, D), F32), jax.ShapeDtypeStruct((S, D), BF16)),
                          scratch_shapes=[pltpu.VMEM((ts + CONV_HALO, D), F32)],
                          compiler_params=_params(("parallel",)), name=name)(u, u, w, b, g, beta)


def ln_silu_bwd(name, ds, c, g, beta, layer, tm=256):
    S, D = c.shape
    tm = _tile(S, tm)

    def body(ds_ref, c_ref, g_ref, be_ref, dc_ref, acc_ref):
        c_ = c_ref[...]
        mu = jnp.mean(c_, axis=-1, keepdims=True)
        d = c_ - mu
        var = jnp.mean(d * d, axis=-1, keepdims=True)
        rstd = lax.rsqrt(var + LN_EPS)
        xh = d * rstd
        nrm = xh * g_ref[...] + be_ref[...]
        sg = _sigmoid(nrm)
        dn = ds_ref[...] * (sg * (1.0 + nrm * (1.0 - sg)))
        dxh = dn * g_ref[...]
        m1 = jnp.mean(dxh, axis=-1, keepdims=True)
        m2 = jnp.mean(dxh * xh, axis=-1, keepdims=True)
        dc = rstd * (dxh - m1 - xh * m2)
        dc_ref[...] = dc

        @pl.when(pl.program_id(0) == 0)
        def _():
            acc_ref[...] = jnp.zeros_like(acc_ref)
        acc_ref[0:1, :] += jnp.sum(dn * xh, axis=0, keepdims=True)
        acc_ref[1:2, :] += jnp.sum(dn, axis=0, keepdims=True)
        acc_ref[2:3, :] += jnp.sum(dc, axis=0, keepdims=True)

    row = pl.BlockSpec((tm, D), lambda i: (i, 0))
    vec = pl.BlockSpec((None, 1, D), lambda i: (layer, 0, 0))
    return pl.pallas_call(body, grid=(S // tm,), in_specs=[row, row, vec, vec],
                          out_specs=(row, pl.BlockSpec((8, D), lambda i: (0, 0))),
                          out_shape=(jax.ShapeDtypeStruct((S, D), F32), jax.ShapeDtypeStruct((8, D), F32)),
                          compiler_params=_params(("arbitrary",)), name=name)(ds, c, g, beta)


def conv_glu_bwd(name, dc, u, h_sm, w, layer, ts=128):
    S, D = dc.shape
    kw = w.shape[1]
    half = N_CHIPS // 2
    n = D // half
    ts = _tile(S, ts, CONV_HALO)
    nblk = S // ts
    lc = LANES if n % LANES == 0 else n

    def body(dc_ref, dcn_ref, u_ref, up_ref, a_ref, g_ref, w_ref, da_ref, dg_ref, dw_ref, dba_ref, dbg_ref, dwin_ref, uwin_ref):
        i = pl.program_id(1)
        dwin_ref[0:ts, :] = dc_ref[...]
        dwin_ref[ts:, :] = jnp.where(i == nblk - 1, 0.0, dcn_ref[...])
        uwin_ref[0:CONV_HALO, :] = jnp.where(i == 0, 0.0, up_ref[...])
        uwin_ref[CONV_HALO:, :] = u_ref[...]

        @pl.when(i == 0)
        def _():
            dw_ref[...] = jnp.zeros_like(dw_ref)
            dba_ref[...] = jnp.zeros_like(dba_ref)
            dbg_ref[...] = jnp.zeros_like(dbg_ref)

        for cc in range(n // lc):
            cs = slice(cc * lc, (cc + 1) * lc)
            dcb = dwin_ref[0:ts, cs]
            du = jnp.zeros((ts, lc), F32)
            for k in range(kw):
                du = du + w_ref[k:k + 1, cs] * dwin_ref[kw - 1 - k:kw - 1 - k + ts, cs]
                off = CONV_HALO - (kw - 1) + k
                dw_ref[k:k + 1, cs] += jnp.sum(dcb * uwin_ref[off:off + ts, cs], axis=0, keepdims=True)
            a = a_ref[:, cs]
            sg = _sigmoid(g_ref[:, cs])
            da = du * sg
            dg = du * a * sg * (1.0 - sg)
            da_ref[:, cs] = da.astype(BF16)
            dg_ref[:, cs] = dg.astype(BF16)
            dba_ref[0:1, cs] += jnp.sum(da, axis=0, keepdims=True)
            dbg_ref[0:1, cs] += jnp.sum(dg, axis=0, keepdims=True)

    r = ts // CONV_HALO
    main = pl.BlockSpec((ts, n), lambda j, i: (i, j))
    nxt = pl.BlockSpec((CONV_HALO, n), lambda j, i: (jnp.minimum((i + 1) * r, S // CONV_HALO - 1), j))
    prv = pl.BlockSpec((CONV_HALO, n), lambda j, i: (jnp.maximum(i * r - 1, 0), j))
    sm_a = pl.BlockSpec((None, ts, n), lambda j, i: (j, i, 0))
    sm_g = pl.BlockSpec((None, ts, n), lambda j, i: (j + half, i, 0))
    acc8a = pl.BlockSpec((None, 8, n), lambda j, i: (j, 0, 0))
    acc8g = pl.BlockSpec((None, 8, n), lambda j, i: (j + half, 0, 0))
    da, dg, dw, dba, dbg = pl.pallas_call(
        body, grid=(half, nblk),
        in_specs=[main, nxt, main, prv, sm_a, sm_g, pl.BlockSpec((None, kw, n), lambda j, i: (layer, 0, j))],
        out_specs=(pl.BlockSpec((None, ts, n), lambda j, i: (j, i, 0)), pl.BlockSpec((None, ts, n), lambda j, i: (j, i, 0)),
                   pl.BlockSpec((None, 32, n), lambda j, i: (j, 0, 0)),
                   pl.BlockSpec((None, 8, n), lambda j, i: (j, 0, 0)), pl.BlockSpec((None, 8, n), lambda j, i: (j, 0, 0))),
        out_shape=(jax.ShapeDtypeStruct((half, S, n), BF16), jax.ShapeDtypeStruct((half, S, n), BF16),
                   jax.ShapeDtypeStruct((half, 32, n), F32),
                   jax.ShapeDtypeStruct((half, 8, n), F32), jax.ShapeDtypeStruct((half, 8, n), F32)),
        scratch_shapes=[pltpu.VMEM((ts + CONV_HALO, n), F32), pltpu.VMEM((ts + CONV_HALO, n), F32)],
        compiler_params=_params(("parallel", "arbitrary")), name=name)(dc, dc, u, u, h_sm, h_sm, w)
    del acc8a, acc8g
    return da, dg, dw, dba, dbg


ROW_CHUNK = 16


def _ffn_gc(win_ref, w_ref, b_ref, r0, rows, kw, base):
    gc = b_ref[...] + jnp.zeros((rows, win_ref.shape[1]), F32)
    for k in range(kw):
        off = r0 + base - (kw - 1) + k
        gc = gc + w_ref[k:k + 1, :] * win_ref[off:off + rows, :]
    return gc


def ffn_gate_fwd(name, up_sm, gp_sm, w4, b4, layer, ts=256):
    _, S, n = up_sm.shape
    kw = w4.shape[2]
    ts = _tile(S, ts, ROW_CHUNK)
    rc = ROW_CHUNK

    def body(up_ref, gp_ref, gph_ref, w_ref, b_ref, hf_ref, win_ref):
        i = pl.program_id(1)
        win_ref[0:FFN_HALO, :] = jnp.where(i == 0, 0.0, gph_ref[...])
        win_ref[FFN_HALO:, :] = gp_ref[...]
        for r0 in range(0, ts, rc):
            gc = _ffn_gc(win_ref, w_ref, b_ref, r0, rc, kw, FFN_HALO)
            hf_ref[r0:r0 + rc, :] = (gc * _sigmoid(gc) * up_ref[r0:r0 + rc, :]).astype(BF16)

    main = pl.BlockSpec((None, ts, n), lambda j, i: (j, i, 0))
    prv = pl.BlockSpec((None, FFN_HALO, n), lambda j, i: (j, jnp.maximum(i * (ts // FFN_HALO) - 1, 0), 0))
    return pl.pallas_call(body, grid=(N_CHIPS, S // ts),
                          in_specs=[main, main, prv, pl.BlockSpec((None, None, kw, n), lambda j, i: (j, layer, 0, 0)),
                                    pl.BlockSpec((None, None, 1, n), lambda j, i: (layer, j, 0, 0))],
                          out_specs=main, out_shape=jax.ShapeDtypeStruct((N_CHIPS, S, n), BF16),
                          scratch_shapes=[pltpu.VMEM((ts + FFN_HALO, n), F32)],
                          compiler_params=_params(("parallel", "parallel")), name=name)(up_sm, gp_sm, gp_sm, w4, b4)


def ffn_gate_bwd_a(name, dhf_sm, up_sm, gp_sm, w4, b4, layer, ts=256):
    _, S, n = up_sm.shape
    kw = w4.shape[2]
    ts = _tile(S, ts, ROW_CHUNK)
    rc = ROW_CHUNK

    def body(dhf_ref, up_ref, gp_ref, gph_ref, w_ref, b_ref, dup_ref, dgc_ref, win_ref):
        i = pl.program_id(1)
        win_ref[0:FFN_HALO, :] = jnp.where(i == 0, 0.0, gph_ref[...])
        win_ref[FFN_HALO:, :] = gp_ref[...]
        for r0 in range(0, ts, rc):
            rows = slice(r0, r0 + rc)
            gc = _ffn_gc(win_ref, w_ref, b_ref, r0, rc, kw, FFN_HALO)
            sg = _sigmoid(gc)
            dhf = dhf_ref[rows, :]
            dup_ref[rows, :] = (dhf * gc * sg).astype(BF16)
            dgc_ref[rows, :] = dhf * up_ref[rows, :] * (sg * (1.0 + gc * (1.0 - sg)))

    main = pl.BlockSpec((None, ts, n), lambda j, i: (j, i, 0))
    prv = pl.BlockSpec((None, FFN_HALO, n), lambda j, i: (j, jnp.maximum(i * (ts // FFN_HALO) - 1, 0), 0))
    return pl.pallas_call(body, grid=(N_CHIPS, S // ts),
                          in_specs=[main, main, main, prv, pl.BlockSpec((None, None, kw, n), lambda j, i: (j, layer, 0, 0)),
                                    pl.BlockSpec((None, None, 1, n), lambda j, i: (layer, j, 0, 0))],
                          out_specs=(main, main),
                          out_shape=(jax.ShapeDtypeStruct((N_CHIPS, S, n), BF16), jax.ShapeDtypeStruct((N_CHIPS, S, n), F32)),
                          scratch_shapes=[pltpu.VMEM((ts + FFN_HALO, n), F32)],
                          compiler_params=_params(("parallel", "parallel")), name=name)(dhf_sm, up_sm, gp_sm, gp_sm, w4, b4)


def ffn_gate_bwd_b(name, dgc_sm, gp_sm, w4, layer, ts=256):
    _, S, n = gp_sm.shape
    kw = w4.shape[2]
    ts = _tile(S, ts, ROW_CHUNK)
    rc = ROW_CHUNK
    nblk = S // ts

    def body(dgc_ref, dgn_ref, gp_ref, gph_ref, w_ref, dgp_ref, acc_ref, dwin_ref, gwin_ref):
        i = pl.program_id(1)
        dwin_ref[0:ts, :] = dgc_ref[...]
        dwin_ref[ts:, :] = jnp.where(i == nblk - 1, 0.0, dgn_ref[...])
        gwin_ref[0:FFN_HALO, :] = jnp.where(i == 0, 0.0, gph_ref[...])
        gwin_ref[FFN_HALO:, :] = gp_ref[...]

        @pl.when(i == 0)
        def _():
            acc_ref[...] = jnp.zeros_like(acc_ref)
        sums = [jnp.zeros((8, n), F32) for _ in range(kw + 1)]
        for r0 in range(0, ts, rc):
            dgc = dwin_ref[r0:r0 + rc, :]
            dgp = jnp.zeros((rc, n), F32)
            for k in range(kw):
                dgp = dgp + w_ref[k:k + 1, :] * dwin_ref[r0 + kw - 1 - k:r0 + kw - 1 - k + rc, :]
                off = r0 + FFN_HALO - (kw - 1) + k
                prod = dgc * gwin_ref[off:off + rc, :]
                sums[k] = sums[k] + prod[0:8, :] + prod[8:16, :]
            sums[kw] = sums[kw] + dgc[0:8, :] + dgc[8:16, :]
            dgp_ref[r0:r0 + rc, :] = dgp.astype(BF16)
        for k in range(kw):
            acc_ref[k:k + 1, :] += jnp.sum(sums[k], axis=0, keepdims=True)
        acc_ref[7:8, :] += jnp.sum(sums[kw], axis=0, keepdims=True)

    r = ts // FFN_HALO
    main = pl.BlockSpec((None, ts, n), lambda j, i: (j, i, 0))
    nxt = pl.BlockSpec((None, FFN_HALO, n), lambda j, i: (j, jnp.minimum((i + 1) * r, S // FFN_HALO - 1), 0))
    prv = pl.BlockSpec((None, FFN_HALO, n), lambda j, i: (j, jnp.maximum(i * r - 1, 0), 0))
    return pl.pallas_call(body, grid=(N_CHIPS, nblk),
                          in_specs=[main, nxt, main, prv, pl.BlockSpec((None, None, kw, n), lambda j, i: (j, layer, 0, 0))],
                          out_specs=(main, pl.BlockSpec((None, 8, n), lambda j, i: (j, 0, 0))),
                          out_shape=(jax.ShapeDtypeStruct((N_CHIPS, S, n), BF16), jax.ShapeDtypeStruct((N_CHIPS, 8, n), F32)),
                          scratch_shapes=[pltpu.VMEM((ts + FFN_HALO, n), F32), pltpu.VMEM((ts + FFN_HALO, n), F32)],
                          compiler_params=_params(("parallel", "arbitrary")), name=name)(dgc_sm, dgc_sm, gp_sm, gp_sm, w4)


def _neg_softplus(z):
    e = jnp.exp(-jnp.abs(z))
    return -(jnp.maximum(z, 0.0) + jnp.log(1.0 + e)), e


def _split_dot(x, t):
    hi = x.astype(BF16)
    lo = (x - hi.astype(F32)).astype(BF16)
    return jnp.dot(hi, t, preferred_element_type=F32) + jnp.dot(lo, t, preferred_element_type=F32)


STICK_GONE = -100.0
NOT_SWEPT = -1e30


def attn_fwd(name, q, k, v, bq=512, w=256):
    S, D = q.shape
    dh = HEAD_DIM
    hpb = LANES // dh
    bq = _tile(S, bq)
    w = _tile(bq, w)
    nsub = bq // w
    nkb = S // w

    def body(q_ref, k_ref, v_ref, o_ref, runs_ref, rs_ref):
        qi = pl.program_id(1)
        rr = lax.broadcasted_iota(jnp.int32, (w, w), 0)
        cc = lax.broadcasted_iota(jnp.int32, (w, w), 1)
        t_suf = (rr >= cc).astype(BF16)
        tq = qi * bq + lax.broadcasted_iota(jnp.int32, (bq, w), 0)
        tk = lax.broadcasted_iota(jnp.int32, (bq, w), 1)
        lane = lax.broadcasted_iota(jnp.int32, (bq, LANES), 1)
        ntot = (qi + 1) * nsub
        heads = [slice(hh * dh, (hh + 1) * dh) for hh in range(hpb)]
        qbs = [q_ref[:, hs] for hs in heads]
        for hh in range(hpb):
            rs_ref[hh] = jnp.where(lane < ntot, NOT_SWEPT, 0.0)

        def block(kb, carry, masked):
            kstart = pl.multiple_of(kb * w, w)
            if masked:
                m = (tk + kstart) < tq
            out = []
            for hh, hs in enumerate(heads):
                run, acc = carry[2 * hh], carry[2 * hh + 1]
                kblk = k_ref[pl.ds(kstart, w), hs]
                vblk = v_ref[pl.ds(kstart, w), hs]
                z = lax.dot_general(qbs[hh], kblk, NT, preferred_element_type=F32)
                lg, _ = _neg_softplus(z)
                if masked:
                    lg = jnp.where(m, lg, 0.0)
                cum = _split_dot(lg, t_suf) + run
                a = jnp.exp(z + cum)
                if masked:
                    a = jnp.where(m, a, 0.0)
                acc = acc + jnp.dot(a.astype(BF16), vblk, preferred_element_type=F32)
                run = cum[:, 0:1]
                rs_ref[hh] = jnp.where(lane == kb, run, rs_ref[hh])
                out += [run, acc]
            return tuple(out)

        carry = (jnp.zeros((bq, 1), F32), jnp.zeros((bq, dh), F32)) * hpb
        for sb in reversed(range(nsub)):
            carry = block(qi * nsub + sb, carry, True)

        def cond(c):
            alive = functools.reduce(jnp.maximum, [jnp.max(c[1 + 2 * hh]) for hh in range(hpb)])
            return jnp.logical_and(c[0] >= 0, alive > STICK_GONE)

        def step(c):
            return (c[0] - 1,) + block(c[0], c[1:], False)
        carry = lax.while_loop(cond, step, (qi * nsub - 1,) + carry)[1:]
        for hh, hs in enumerate(heads):
            o_ref[:, hs] = carry[2 * hh + 1]
            runs_ref[hh] = rs_ref[hh, :, 0:nkb]

    qs = pl.BlockSpec((bq, LANES), lambda h, i: (i, h))
    kv = pl.BlockSpec((S, LANES), lambda h, i: (0, h))
    return pl.pallas_call(body, grid=(D // LANES, S // bq), in_specs=[qs, kv, kv],
                          out_specs=(qs, pl.BlockSpec((hpb, bq, nkb), lambda h, i: (h, i, 0))),
                          out_shape=(jax.ShapeDtypeStruct((S, D), F32), jax.ShapeDtypeStruct((D // dh, S, nkb), F32)),
                          scratch_shapes=[pltpu.VMEM((hpb, bq, LANES), F32)],
                          compiler_params=_params(("parallel", "parallel")), name=name)(q, k, v)


def attn_bwd(name, q, k, v, do, runs, dk0=None, dv0=None, bq=512, w=256):
    S, D = q.shape
    dh = HEAD_DIM
    hpb = LANES // dh
    bq = _tile(S, bq)
    w = _tile(bq, w)
    nsub = bq // w
    nkb = S // w
    scale = 1.0 / math.sqrt(dh)
    init = dk0 is not None

    def body(*refs):
        if init:
            q_ref, k_ref, v_ref, do_ref, runs_ref, dk0_ref, dv0_ref, dq_ref, dk_ref, dv_ref, rs_ref = refs
        else:
            q_ref, k_ref, v_ref, do_ref, runs_ref, dq_ref, dk_ref, dv_ref, rs_ref = refs
        qi = pl.program_id(1)

        @pl.when(qi == 0)
        def _():
            dk_ref[...] = dk0_ref[...] if init else jnp.zeros_like(dk_ref)
            dv_ref[...] = dv0_ref[...] if init else jnp.zeros_like(dv_ref)

        rr = lax.broadcasted_iota(jnp.int32, (w, w), 0)
        cc = lax.broadcasted_iota(jnp.int32, (w, w), 1)
        t_suf = (rr >= cc).astype(BF16)
        t_pre = (rr <= cc).astype(BF16)
        tq = qi * bq + lax.broadcasted_iota(jnp.int32, (bq, w), 0)
        tk = lax.broadcasted_iota(jnp.int32, (bq, w), 1)
        lane = lax.broadcasted_iota(jnp.int32, (bq, LANES), 1)
        lane1 = lax.broadcasted_iota(jnp.int32, (1, LANES), 1)
        ntot = (qi + 1) * nsub
        heads = [slice(hh * dh, (hh + 1) * dh) for hh in range(hpb)]
        qbs = [q_ref[:, hs] for hs in heads]
        dobs = [do_ref[:, hs].astype(BF16) for hs in heads]
        kb0 = ntot - nsub
        for hh in range(hpb):
            rs_ref[hh] = jnp.zeros((bq, LANES), F32)
            rs_ref[hh, :, 0:nkb] = runs_ref[hh]
            colmax = jnp.max(rs_ref[hh], axis=0, keepdims=True)
            dead = jnp.logical_and(jnp.logical_and(lane1 >= 1, lane1 <= ntot), colmax <= STICK_GONE)
            kb0 = jnp.minimum(kb0, jnp.sum(dead.astype(jnp.int32)))

        def block(kb, carry, masked):
            kstart = pl.multiple_of(kb * w, w)
            if masked:
                m = (tk + kstart) < tq
            out = []
            for hh, hs in enumerate(heads):
                pg_run, dq = carry[2 * hh], carry[2 * hh + 1]
                qb, dob = qbs[hh], dobs[hh]
                kblk = k_ref[pl.ds(kstart, w), hs]
                vblk = v_ref[pl.ds(kstart, w), hs]
                right = jnp.sum(jnp.where(lane == kb + 1, rs_ref[hh], 0.0), axis=1, keepdims=True)
                z = lax.dot_general(qb, kblk, NT, preferred_element_type=F32)
                lg, e = _neg_softplus(z)
                if masked:
                    lg = jnp.where(m, lg, 0.0)
                a = jnp.exp(z + _split_dot(lg, t_suf) + right)
                if masked:
                    a = jnp.where(m, a, 0.0)
                da = lax.dot_general(dob, vblk, NT, preferred_element_type=F32)
                g = da * a
                pin = _split_dot(g, t_pre) + pg_run
                sig = jnp.where(z >= 0.0, 1.0, e) / (1.0 + e)
                dz = g - sig * pin
                if masked:
                    dz = jnp.where(m, dz, 0.0)
                dzb = dz.astype(BF16)
                dq = dq + jnp.dot(dzb, kblk, preferred_element_type=F32)
                dk_ref[pl.ds(kstart, w), hs] += lax.dot_general(dzb, qb, TN, preferred_element_type=F32)
                dv_ref[pl.ds(kstart, w), hs] += lax.dot_general(a.astype(BF16), dob, TN, preferred_element_type=F32)
                out += [pin[:, w - 1:w], dq]
            return tuple(out)

        carry = (jnp.zeros((bq, 1), F32), jnp.zeros((bq, dh), F32)) * hpb
        carry = lax.fori_loop(kb0, qi * nsub, lambda kb, c: block(kb, c, False), carry)
        for sb in range(nsub):
            carry = block(qi * nsub + sb, carry, True)
        for hh, hs in enumerate(heads):
            dq_ref[:, hs] = carry[2 * hh + 1] * scale

    qs = pl.BlockSpec((bq, LANES), lambda h, i: (i, h))
    kv = pl.BlockSpec((S, LANES), lambda h, i: (0, h))
    ins = [q, k, v, do, runs] + ([dk0, dv0] if init else [])
    specs = [qs, kv, kv, qs, pl.BlockSpec((hpb, bq, nkb), lambda h, i: (h, i, 0))] + ([kv, kv] if init else [])
    sds = jax.ShapeDtypeStruct((S, D), F32)
    return pl.pallas_call(body, grid=(D // LANES, S // bq), in_specs=specs, out_specs=(qs, kv, kv), out_shape=(sds, sds, sds),
                          scratch_shapes=[pltpu.VMEM((hpb, bq, LANES), F32)],
                          compiler_params=_params(("parallel", "arbitrary")), name=name)(*ins)


def loss_head(name, y, tgt, tm=512):
    S, D = y.shape
    tm = _tile(S, tm)

    def body(y_ref, t_ref, dy_ref, acc_ref):
        @pl.when(pl.program_id(0) == 0)
        def _():
            acc_ref[...] = jnp.zeros_like(acc_ref)
        e = y_ref[...] - t_ref[...]
        dy_ref[...] = e * (1.0 / D)
        acc_ref[...] += jnp.sum(e * e)

    row = pl.BlockSpec((tm, D), lambda i: (i, 0))
    return pl.pallas_call(body, grid=(S // tm,), in_specs=[row, row],
                          out_specs=(row, pl.BlockSpec((8, LANES), lambda i: (0, 0))),
                          out_shape=(jax.ShapeDtypeStruct((S, D), F32), jax.ShapeDtypeStruct((8, LANES), F32)),
                          compiler_params=_params(("arbitrary",)), name=name)(y, tgt)


def _to_sm(a, axis=-1):
    axis = axis % a.ndim
    shp = a.shape[:axis] + (N_CHIPS, a.shape[axis] // N_CHIPS) + a.shape[axis + 1:]
    return jnp.moveaxis(a.reshape(shp), axis, 0)


def _from_sm(a, axis=-1):
    nd = a.ndim - 1
    axis = axis % nd
    b = jnp.moveaxis(a, 0, axis)
    return b.reshape(b.shape[:axis] + (b.shape[axis] * b.shape[axis + 1],) + b.shape[axis + 2:])


class GradBuffers:
    def __init__(self, W):
        groups = {}
        for n in BIG:
            _, layers, rows, cols = W[n].shape
            groups.setdefault(cols, []).append((rows, n, layers))
        self.where, self.cols_of, self.buf, self.members = {}, {}, {}, {}
        for cols, items in groups.items():
            off, members = 0, []
            for rows, n, layers in sorted(items, key=lambda t: -t[0]):
                assert off % rows == 0
                self.where[n], self.cols_of[n] = (off, rows), cols
                members.append((n, off, rows * layers))
                off += rows * layers
            assert off % 32 == 0
            self.buf[cols] = lax.empty((N_CHIPS, off, cols), F32)
            self.members[cols] = members

    def put(self, n, layer, fn, **kw):
        cols = self.cols_of[n]
        off, rows = self.where[n]
        self.buf[cols] = fn(into=self.buf[cols], off=off + layer * rows, **kw)


def forward_backward(x, p4, tgt, W):
    S, D = x.shape
    scale = 1.0 / math.sqrt(HEAD_DIM)
    saved = []
    kh = vh = xb_kv = None
    xb = x.astype(BF16)
    for i in range(DEPTH):
        sv = {'xb': xb}
        if i < N_A:
            h_sm = mm_colsm(f"pw1_{i}", xb, W['a_pw1_w'], i, W['a_pw1_b'])
            u = glu_fwd(f"glu_{i}", h_sm)
            c, s = conv_ln_silu_fwd(f"convln_{i}", u, W['a_dw_w'], W['a_dw_b'], W['a_ln_g'], W['a_ln_b'], i)
            mix = mm_rowsm(f"pw2_{i}", s, W['a_pw2_w'], i, bias=W['a_pw2_b'])
            sv.update(h_sm=h_sm, u=u, c=c, s=s)
        else:
            j = i - N_A
            if kh is None:
                xb_kv = xb
                kh = mm_rowsm("wk", xb, W['kv_wk'], 0, out_dtype=BF16)
                vh = mm_rowsm("wv", xb, W['kv_wv'], 0, out_dtype=BF16)
            qh = mm_rowsm(f"wq_{j}", xb, W['b_wq'], j, out_dtype=BF16, out_scale=scale)
            o, runs = attn_fwd(f"attn_{j}", qh, kh, vh)
            mix = mm_rowsm(f"wo_{j}", o, W['b_wo'], j)
            sv.update(qh=qh, o=o, runs=runs)
        x1, x1b, xh1, rs1 = ln_fwd(f"lnmix_{i}", x, mix, W['ln_mix_g'], W['ln_mix_b'], i)
        up_sm = mm_colsm(f"up_{i}", x1b, W['ffn_w_up'], i)
        gp_sm = mm_colsm(f"gate_{i}", x1b, W['ffn_w_gate'], i)
        hf_sm = ffn_gate_fwd(f"ffngate_{i}", up_sm, gp_sm, W['ffn_conv_w'], W['ffn_conv_b'], i)
        ffn = mm_rowsm(f"down_{i}", hf_sm, W['ffn_w_down'], i, a_sm=True)
        tg = mm_rowsm(f"plegate_{i}", x1b, W['ple_w_gate'], i)
        pp = mm_proj(f"pleproj_{i}", p4, i, W['ple_w_proj'])
        x2, x2b, xh2, rs2 = ln_fwd(f"lnffn_{i}", x1, ffn, W['ln_ffn_g'], W['ln_ffn_b'], i, tg=tg, pp=pp)
        sv.update(x1b=x1b, xh1=xh1, rs1=rs1, up_sm=up_sm, gp_sm=gp_sm, hf_sm=hf_sm, tg=tg, pp=pp, xh2=xh2, rs2=rs2)
        saved.append(sv)
        x, xb = x2, x2b

    dx, lacc = loss_head("loss", x, tgt)
    loss_sum = lacc[0, 0]

    G = {n: [None] * DEPTH for n in WEIGHTS if n not in BIG}
    gb = GradBuffers(W)
    dk = dv = None
    for i in reversed(range(DEPTH)):
        sv = saved[i]
        dr, acc, dtg, dpp = ln_bwd(f"lnffn_b_{i}", dx, sv['xh2'], sv['rs2'], W['ln_ffn_g'], i, tg=sv['tg'], pp=sv['pp'])
        G['ln_ffn_g'][i], G['ln_ffn_b'][i] = acc[0], acc[1]
        gb.put('ple_w_proj', i, functools.partial(mm_tn_proj, f"dproj_{i}", p4, i, dpp))
        gb.put('ple_w_gate', i, functools.partial(mm_tn_rowsm, f"dplegate_{i}", sv['x1b'], dtg))
        gb.put('ffn_w_down', i, functools.partial(mm_tn_rowsm, f"ddown_{i}", sv['hf_sm'], dr, a_sm=True))
        dhf_sm = mm_nt_rowsm(f"dhf_{i}", dr, W['ffn_w_down'], i, out_sm=True)
        dup_sm, dgc_sm = ffn_gate_bwd_a(f"ffngate_ba_{i}", dhf_sm, sv['up_sm'], sv['gp_sm'], W['ffn_conv_w'], W['ffn_conv_b'], i)
        dgp_sm, cacc = ffn_gate_bwd_b(f"ffngate_bb_{i}", dgc_sm, sv['gp_sm'], W['ffn_conv_w'], i)
        kw = W['ffn_conv_w'].shape[2]
        G['ffn_conv_w'][i] = cacc[:, 0:kw, :]
        G['ffn_conv_b'][i] = cacc[:, 7, :].reshape(-1)
        gb.put('ffn_w_up', i, functools.partial(mm_tn_colsm, f"dup_{i}", sv['x1b'], dup_sm))
        gb.put('ffn_w_gate', i, functools.partial(mm_tn_colsm, f"dgate_{i}", sv['x1b'], dgp_sm))
        dx1 = mm_nt_rowsm(f"dx1a_{i}", dtg, W['ple_w_gate'], i, res=dr, res_alpha=DN_ALPHA)
        dx1 = mm_nt_colsm(f"dx1b_{i}", _sm_parts(dup_sm), W['ffn_w_up'], i, res=dx1)
        dx1 = mm_nt_colsm(f"dx1c_{i}", _sm_parts(dgp_sm), W['ffn_w_gate'], i, res=dx1)

        dr1, acc1 = ln_bwd(f"lnmix_b_{i}", dx1, sv['xh1'], sv['rs1'], W['ln_mix_g'], i)
        G['ln_mix_g'][i], G['ln_mix_b'][i] = acc1[0], acc1[1]
        xin = sv['xb']
        if i < N_A:
            G['a_pw2_b'][i] = acc1[2]
            gb.put('a_pw2_w', i, functools.partial(mm_tn_rowsm, f"dpw2_{i}", sv['s'], dr1))
            ds = mm_nt_rowsm(f"ds_{i}", dr1, W['a_pw2_w'], i)
            dc, cacc = ln_silu_bwd(f"lnsilu_b_{i}", ds, sv['c'], W['a_ln_g'], W['a_ln_b'], i)
            G['a_ln_g'][i], G['a_ln_b'][i], G['a_dw_b'][i] = cacc[0], cacc[1], cacc[2]
            da, dg, dw, dba, dbg = conv_glu_bwd(f"convglu_b_{i}", dc, sv['u'], sv['h_sm'], W['a_dw_w'], i)
            kw = W['a_dw_w'].shape[1]
            G['a_dw_w'][i] = _from_sm(dw[:, 0:kw, :], axis=-1)
            G['a_pw1_b'][i] = jnp.concatenate([dba[:, 0, :], dbg[:, 0, :]], axis=0)
            half = da.shape[0]
            gb.put('a_pw1_w', i, functools.partial(mm_tn_colsm, f"dpw1a_{i}", xin, da))
            gb.put('a_pw1_w', i, functools.partial(mm_tn_colsm, f"dpw1g_{i}", xin, dg), j0=half)
            dx = mm_nt_colsm(f"dxa_{i}", _sm_parts(da) + _sm_parts(dg), W['a_pw1_w'], i, res=dr1, res_alpha=DN_ALPHA)
        else:
            j = i - N_A
            gb.put('b_wo', j, functools.partial(mm_tn_rowsm, f"dwo_{j}", sv['o'], dr1))
            do = mm_nt_rowsm(f"do_{j}", dr1, W['b_wo'], j)
            dq, dk, dv = attn_bwd(f"attn_b_{j}", sv['qh'], kh, vh, do, sv['runs'], dk, dv)
            gb.put('b_wq', j, functools.partial(mm_tn_rowsm, f"dwq_{j}", xin, dq))
            dx = mm_nt_rowsm(f"dxq_{j}", dq, W['b_wq'], j, res=dr1, res_alpha=DN_ALPHA)
            if j == 0:
                gb.put('kv_wk', 0, functools.partial(mm_tn_rowsm, "dwk", xb_kv, dk))
                gb.put('kv_wv', 0, functools.partial(mm_tn_rowsm, "dwv", xb_kv, dv))
                dx = mm_nt_rowsm("dxk", dk, W['kv_wk'], 0, res=dx)
                dx = mm_nt_rowsm("dxv", dv, W['kv_wv'], 0, res=dx)
    return loss_sum, dx, G, gb


MESH = pl.DeviceIdType.MESH
HBM = pl.BlockSpec(memory_space=pltpu.HBM)


def _place():
    x, y, c = lax.axis_index("x"), lax.axis_index("y"), lax.axis_index("c")
    others = [(1 - x, y), (x, 1 - y), (1 - x, 1 - y)]
    return x, y, c, others


def allgather_chips(name, arrs):
    n = len(arrs)

    def body(*refs):
        ins, outs = refs[:n], refs[n:2 * n]
        send_sems, recv_sems = refs[2 * n:]
        x, y, c, others = _place()
        me = 2 * x + y
        sibling = (x, y, 1 - c)

        def remote(a, k, src, chip_id, half, to):
            return pltpu.make_async_remote_copy(src_ref=src, dst_ref=outs[a].at[chip_id, half], send_sem=send_sems.at[a, k],
                                                recv_sem=recv_sems.at[a, k], device_id=to, device_id_type=MESH)

        first = [remote(a, k, ins[a].at[c], me, c, (ch[0], ch[1], c)) for a in range(n) for k, ch in enumerate(others)]
        for cp in first:
            cp.start()
        passed = []
        for a in range(n):
            for k, ch in enumerate(others):
                cid = 2 * ch[0] + ch[1]
                remote(a, k, ins[a].at[c], cid, c, sibling).wait_recv()
                fwd = remote(a, 3 + k, outs[a].at[cid, c], cid, c, sibling)
                fwd.start()
                passed.append(fwd)
        for a in range(n):
            for k, ch in enumerate(others):
                cid = 2 * ch[0] + ch[1]
                remote(a, 3 + k, ins[a].at[c], cid, 1 - c, sibling).wait_recv()
        for cp in first + passed:
            cp.wait_send()

    outs = pl.pallas_call(body, out_shape=tuple(jax.ShapeDtypeStruct((N_CHIPS,) + a.shape, a.dtype) for a in arrs),
                          in_specs=[HBM] * n, out_specs=tuple([HBM] * n),
                          scratch_shapes=[pltpu.SemaphoreType.DMA((n, 6)), pltpu.SemaphoreType.DMA((n, 6))],
                          name=name)(*arrs)
    me = 2 * lax.axis_index("x") + lax.axis_index("y")
    return [lax.dynamic_update_index_in_dim(o, a, me, 0) for o, a in zip(outs, arrs)]


def exchange_sibling(name, gs):
    n = len(gs)

    def body(*refs):
        g_refs, o_refs = refs[:n], refs[n:2 * n]
        send_sems, recv_sems = refs[2 * n:]
        x, y, c, _ = _place()
        cps = [pltpu.make_async_remote_copy(src_ref=g_refs[a].at[j, 1 - c], dst_ref=o_refs[a].at[j], send_sem=send_sems.at[a, j],
                                            recv_sem=recv_sems.at[a, j], device_id=(x, y, 1 - c), device_id_type=MESH)
               for a in range(n) for j in range(N_CHIPS)]
        for cp in cps:
            cp.start()
        for cp in cps:
            cp.wait()

    return pl.pallas_call(body, out_shape=tuple(jax.ShapeDtypeStruct((N_CHIPS,) + g.shape[2:], g.dtype) for g in gs),
                          in_specs=[HBM] * n, out_specs=tuple([HBM] * n),
                          scratch_shapes=[pltpu.SemaphoreType.DMA((n, N_CHIPS)), pltpu.SemaphoreType.DMA((n, N_CHIPS))],
                          name=name)(*gs)


def scatter_chips(name, ss):
    n = len(ss)

    def body(*refs):
        s_refs, o_refs = refs[:n], refs[n:2 * n]
        send_sems, recv_sems = refs[2 * n:]
        x, y, c, others = _place()
        cps = [pltpu.make_async_remote_copy(src_ref=s_refs[a].at[2 * ch[0] + ch[1]], dst_ref=o_refs[a].at[k],
                                            send_sem=send_sems.at[a, k], recv_sem=recv_sems.at[a, k],
                                            device_id=(ch[0], ch[1], c), device_id_type=MESH)
               for a in range(n) for k, ch in enumerate(others)]
        for cp in cps:
            cp.start()
        for cp in cps:
            cp.wait()

    return pl.pallas_call(body, out_shape=tuple(jax.ShapeDtypeStruct((3,) + s.shape[1:], s.dtype) for s in ss),
                          in_specs=[HBM] * n, out_specs=tuple([HBM] * n),
                          scratch_shapes=[pltpu.SemaphoreType.DMA((n, 3)), pltpu.SemaphoreType.DMA((n, 3))], name=name)(*ss)


def share_sibling(name, ts):
    n = len(ts)

    def body(*refs):
        o_refs = refs[n:2 * n]
        send_sems, recv_sems = refs[2 * n:]
        x, y, c, _ = _place()
        cps = [pltpu.make_async_remote_copy(src_ref=o_refs[a].at[c], dst_ref=o_refs[a].at[c], send_sem=send_sems.at[a],
                                            recv_sem=recv_sems.at[a], device_id=(x, y, 1 - c), device_id_type=MESH)
               for a in range(n)]
        for cp in cps:
            cp.start()
        for a in range(n):
            pltpu.make_async_remote_copy(src_ref=o_refs[a].at[c], dst_ref=o_refs[a].at[1 - c], send_sem=send_sems.at[a],
                                         recv_sem=recv_sems.at[a], device_id=(x, y, 1 - c), device_id_type=MESH).wait_recv()
        for cp in cps:
            cp.wait_send()

    return pl.pallas_call(body, out_shape=tuple(jax.ShapeDtypeStruct(t.shape, t.dtype) for t in ts),
                          in_specs=[HBM] * n, out_specs=tuple([HBM] * n), input_output_aliases={a: a for a in range(n)},
                          scratch_shapes=[pltpu.SemaphoreType.DMA((n,)), pltpu.SemaphoreType.DMA((n,))],
                          name=name)(*ts)


def add_halves(name, g, recv, place, out_dtype, tr=512):
    _, _, R, C = g.shape
    tr = _tile(R, tr, 16)

    def body(p_ref, a_ref, b_ref, o_ref):
        o_ref[...] = (a_ref[...] + b_ref[...]).astype(o_ref.dtype)

    blk = pl.BlockSpec((None, tr, C), lambda j, i, p: (j, i, 0))
    gs = pltpu.PrefetchScalarGridSpec(num_scalar_prefetch=1, grid=(N_CHIPS, R // tr),
                                      in_specs=[pl.BlockSpec((None, None, tr, C), lambda j, i, p: (j, p[0], i, 0)), blk],
                                      out_specs=blk)
    return pl.pallas_call(body, grid_spec=gs, out_shape=jax.ShapeDtypeStruct((N_CHIPS, R, C), out_dtype),
                          compiler_params=_params(("parallel", "parallel")), name=name)(place, g, recv)


def add_chips(name, g, r1, r2, place, tr=512):
    _, _, R, C = g.shape
    tr = _tile(R, tr, 16)

    def body(p_ref, a_ref, b_ref, c_ref, o_ref):
        s = a_ref[...] + b_ref[...]
        o_ref[...] = ((s + c_ref[0].astype(F32)) + c_ref[1].astype(F32)) + c_ref[2].astype(F32)

    gs = pltpu.PrefetchScalarGridSpec(num_scalar_prefetch=1, grid=(R // tr,),
                                      in_specs=[pl.BlockSpec((None, None, tr, C), lambda i, p: (p[1], p[0], i, 0)),
                                                pl.BlockSpec((None, tr, C), lambda i, p: (p[1], i, 0)),
                                                pl.BlockSpec((3, tr, C), lambda i, p: (0, i, 0))],
                                      out_specs=pl.BlockSpec((None, tr, C), lambda i, p: (p[0], i, 0)))
    return pl.pallas_call(body, grid_spec=gs, out_shape=jax.ShapeDtypeStruct((2, R, C), F32),
                          compiler_params=_params(("parallel",)), name=name)(place, g, r1, r2)


def reduce_scatter(gs, wire_dtypes, place):
    r1 = exchange_sibling("rs_sibling", gs)
    s1 = [add_halves(f"rs_add_cores_{a}", g, r, place, dt) for a, (g, r, dt) in enumerate(zip(gs, r1, wire_dtypes))]
    r2 = scatter_chips("rs_chips", s1)
    tot = [add_chips(f"rs_add_chips_{a}", g, r, rr, place) for a, (g, r, rr) in enumerate(zip(gs, r1, r2))]
    return share_sibling("rs_share", tot)


def adamw(name, w, g, m, v, tr=512):
    shp = w.shape
    cols = shp[-1]
    w2, g2, m2, v2 = (a.reshape(-1, cols) for a in (w, g, m, v))
    rows = w2.shape[0]
    tr = _tile(rows, tr)

    def body(w_ref, g_ref, m_ref, v_ref, d_ref, mo_ref, vo_ref):
        g_ = g_ref[...]
        m_ = ADAM_B1 * m_ref[...] + (1.0 - ADAM_B1) * g_
        v_ = ADAM_B2 * v_ref[...] + (1.0 - ADAM_B2) * (g_ * g_)
        m_hat = m_ / (1.0 - ADAM_B1 ** ADAM_STEP)
        v_hat = v_ / (1.0 - ADAM_B2 ** ADAM_STEP)
        d_ref[...] = -ADAM_LR * (m_hat / (jnp.sqrt(v_hat) + ADAM_EPS) + ADAM_WD * w_ref[...])
        mo_ref[...] = m_
        vo_ref[...] = v_

    blk = pl.BlockSpec((tr, cols), lambda i: (i, 0))
    sds = jax.ShapeDtypeStruct((rows, cols), F32)
    d, mo, vo = pl.pallas_call(body, grid=(rows // tr,), in_specs=[blk] * 4, out_specs=(blk, blk, blk), out_shape=(sds, sds, sds),
                               compiler_params=_params(("parallel",)), name=name)(w2, g2, m2, v2)
    return d.reshape(shp), mo.reshape(shp), vo.reshape(shp)


PACK_ALIGN = 1024


def _pad_to(a, mult, axis=-1):
    axis = axis % a.ndim
    extra = (-a.shape[axis]) % mult
    if extra == 0:
        return a
    pads = [(0, 0)] * a.ndim
    pads[axis] = (0, extra)
    return jnp.pad(a, pads)


def _pack(pieces, lead, row_mult):
    nl = len(lead)
    flat, offs, sizes, off = [], [], [], 0
    for a in pieces:
        f = a.reshape(lead + (-1,))
        sizes.append(f.shape[-1])
        f = _pad_to(f, PACK_ALIGN)
        offs.append(off)
        off += f.shape[-1]
        flat.append(f)
    cat = _pad_to(jnp.concatenate(flat, axis=nl), 2 * row_mult * LANES)
    return cat.reshape(lead + (2, -1, LANES)), offs, sizes


def _unpack(packed, lead, offs, sizes, shapes):
    flat = packed.reshape(lead + (-1,))
    return [lax.slice_in_dim(flat, o, o + s, axis=len(lead)).reshape(lead + tuple(shp)) for o, s, shp in zip(offs, sizes, shapes)]


def _stack_grads(G, names):
    out = {}
    for n in names:
        parts = [g for g in G[n] if g is not None]
        if n in ('kv_wk', 'kv_wv'):
            out[n] = parts[0]
        elif n in REPLICATED:
            out[n] = jnp.stack(parts, axis=0).reshape(N_CHIPS, -1)
        elif n in ('a_dw_w', 'a_dw_b', 'a_ln_g', 'a_ln_b', 'a_pw2_b'):
            out[n] = _to_sm(jnp.stack(parts, axis=0), axis=-1)
        else:
            out[n] = jnp.stack(parts, axis=1)
    return out


def _whole_weights(big, small, rep, D):
    W = {}
    for n in BIG:
        a = big[n]
        W[n] = a[:, None] if n in ('kv_wk', 'kv_wv') else a
    W['a_pw1_b'] = small['a_pw1_b'][:, :, None, :]
    W['a_dw_w'] = _from_sm(small['a_dw_w'], axis=-1)
    for n in ('a_dw_b', 'a_ln_g', 'a_ln_b', 'a_pw2_b'):
        W[n] = _from_sm(small[n], axis=-1)[:, None, :]
    W['ffn_conv_w'] = small['ffn_conv_w']
    L, F = rep['ffn_conv_b'].shape
    W['ffn_conv_b'] = rep['ffn_conv_b'].reshape(L, N_CHIPS, 1, F // N_CHIPS)
    for n in ('ln_mix_g', 'ln_mix_b', 'ln_ffn_g', 'ln_ffn_b'):
        W[n] = rep[n][:, None, :]
    return W


SMALL = ('a_pw1_b', 'a_dw_w', 'a_dw_b', 'a_ln_g', 'a_ln_b', 'a_pw2_b', 'ffn_conv_w')


def _step(x, p, loss_target, w, m, v):
    S, D = x.shape[-2:]
    x2, tgt = x.reshape(S, D), loss_target.reshape(S, D)
    place = jnp.stack([lax.axis_index("c"), 2 * lax.axis_index("x") + lax.axis_index("y")]).astype(jnp.int32)

    big_in = [w[n].astype(BF16).reshape((2, -1) + w[n].shape[1:] if w[n].ndim == 3 else (2, -1, w[n].shape[-1])) for n in BIG]
    small_in, s_offs, s_sizes = _pack([w[n] for n in SMALL], (), 8)
    gathered = allgather_chips("gather_weights", big_in + [small_in])
    big = {n: g.reshape((N_CHIPS,) + w[n].shape) for n, g in zip(BIG, gathered[:-1])}
    small = dict(zip(SMALL, _unpack(gathered[-1], (N_CHIPS,), s_offs, s_sizes, [w[n].shape for n in SMALL])))
    W = _whole_weights(big, small, {n: w[n] for n in REPLICATED}, D)

    loss_sum, dx, G, gb = forward_backward(x2, p, tgt, W)
    loss = lax.psum(0.5 * loss_sum / D, ("x", "y", "c"))

    vectors = [n for n in WEIGHTS if n not in BIG]
    mats = [b.reshape(N_CHIPS, 2, b.shape[1] // 2, b.shape[2]) for b in gb.buf.values()]
    members = [gb.members[cols] for cols in gb.buf]
    g_sm = _stack_grads(G, vectors)
    packed, offs, sizes = _pack([g_sm[n] for n in vectors], (N_CHIPS,), 512)
    reduced = reduce_scatter(mats + [packed], [BF16] * len(mats) + [F32], place)
    shapes = [w[n].shape if n not in REPLICATED else (w[n].size // N_CHIPS,) for n in vectors]
    g_mine = dict(zip(vectors, _unpack(reduced[-1], (), offs, sizes, shapes)))
    for red, where in zip(reduced[:-1], members):
        rows = red.reshape(-1, red.shape[-1])
        for n, off, cnt in where:
            g_mine[n] = lax.slice_in_dim(rows, off, off + cnt, axis=0).reshape(w[n].shape)
    rep_in, r_offs, r_sizes = _pack([g_mine[n] for n in REPLICATED], (), 8)
    rep_all = allgather_chips("gather_replicated_grads", [rep_in])[0]
    for n, g in zip(REPLICATED, _unpack(rep_all, (N_CHIPS,), r_offs, r_sizes, [(w[n].size // N_CHIPS,) for n in REPLICATED])):
        g_mine[n] = g.reshape(w[n].shape)

    grads, deltas, new_m, new_v = [], [], [], []
    for n in WEIGHTS:
        d, mo, vo = adamw(f"adamw_{n}", w[n], g_mine[n], m[n], v[n])
        grads.append(g_mine[n])
        deltas.append(d)
        new_m.append(mo)
        new_v.append(vo)
    return (loss, dx.reshape(x.shape), *grads, *deltas, *new_m, *new_v)


def kernel(x, p, a_pw1_w, a_pw1_b, a_dw_w, a_dw_b, a_ln_g, a_ln_b, a_pw2_w, a_pw2_b, b_wq, kv_wk, kv_wv, b_wo, ln_mix_g, ln_mix_b, ffn_w_up, ffn_w_gate, ffn_conv_w, ffn_conv_b, ffn_w_down, ple_w_gate, ple_w_proj, ln_ffn_g, ln_ffn_b, loss_target, m_a_pw1_w, m_a_pw1_b, m_a_dw_w, m_a_dw_b, m_a_ln_g, m_a_ln_b, m_a_pw2_w, m_a_pw2_b, m_b_wq, m_kv_wk, m_kv_wv, m_b_wo, m_ln_mix_g, m_ln_mix_b, m_ffn_w_up, m_ffn_w_gate, m_ffn_conv_w, m_ffn_conv_b, m_ffn_w_down, m_ple_w_gate, m_ple_w_proj, m_ln_ffn_g, m_ln_ffn_b, v_a_pw1_w, v_a_pw1_b, v_a_dw_w, v_a_dw_b, v_a_ln_g, v_a_ln_b, v_a_pw2_w, v_a_pw2_b, v_b_wq, v_kv_wk, v_kv_wv, v_b_wo, v_ln_mix_g, v_ln_mix_b, v_ffn_w_up, v_ffn_w_gate, v_ffn_conv_w, v_ffn_conv_b, v_ffn_w_down, v_ple_w_gate, v_ple_w_proj, v_ln_ffn_g, v_ln_ffn_b):
    vals = dict(locals())
    w = {n: vals[n] for n in WEIGHTS}
    m = {n: vals["m_" + n] for n in WEIGHTS}
    v = {n: vals["v_" + n] for n in WEIGHTS}
    return _step(x, p, loss_target, w, m, v)
```

```python
import functools
import math

import jax
import jax.numpy as jnp
import numpy as np
from jax import lax
from jax.experimental import pallas as pl
from jax.experimental.pallas import tpu as pltpu

F32, BF16 = jnp.float32, jnp.bfloat16

HEAD_DIM = 64
LN_EPS = 1e-5
DEPTH = 4
N_A = DEPTH // 2
DN_ALPHA = (2.0 * DEPTH) ** 0.25
N_CHIPS = 4

ADAM_LR, ADAM_B1, ADAM_B2, ADAM_EPS, ADAM_WD, ADAM_STEP = 0.001, 0.9, 0.999, 1e-08, 0.01, 10

VMEM_LIMIT_BYTES = 56 * 2**20
LANES = 128
CONV_HALO = 32
FFN_HALO = 8

NN = (((1,), (0,)), ((), ()))
NT = (((1,), (1,)), ((), ()))
TN = (((0,), (0,)), ((), ()))

WEIGHTS = ['a_pw1_w', 'a_pw1_b', 'a_dw_w', 'a_dw_b', 'a_ln_g', 'a_ln_b', 'a_pw2_w', 'a_pw2_b', 'b_wq', 'kv_wk', 'kv_wv',
           'b_wo', 'ln_mix_g', 'ln_mix_b', 'ffn_w_up', 'ffn_w_gate', 'ffn_conv_w', 'ffn_conv_b', 'ffn_w_down', 'ple_w_gate',
           'ple_w_proj', 'ln_ffn_g', 'ln_ffn_b']
REPLICATED = ('ln_mix_g', 'ln_mix_b', 'ffn_conv_b', 'ln_ffn_g', 'ln_ffn_b')
BIG = ('a_pw1_w', 'a_pw2_w', 'b_wq', 'kv_wk', 'kv_wv', 'b_wo', 'ffn_w_up', 'ffn_w_gate', 'ffn_w_down', 'ple_w_gate',
       'ple_w_proj')


def _tile(n, pref, mult=8):
    t = min(n, pref)
    while t > 0:
        if n % t == 0 and t % mult == 0:
            return t
        t -= 1
    return n


def _params(sem):
    return pltpu.CompilerParams(dimension_semantics=sem, vmem_limit_bytes=VMEM_LIMIT_BYTES)


def _sigmoid(x):
    return 1.0 / (1.0 + jnp.exp(-x))


def _mm_call(name, grid, terms, out_spec, out_sds, dims, bias=None, res=None, res_alpha=1.0, out_scale=None, into=None):
    n_terms = len(terms)

    def body(*refs):
        o_ref = refs[-1]
        acc = None
        for t in range(n_terms):
            a = refs[2 * t][...].astype(BF16)
            b = refs[2 * t + 1][...].astype(BF16)
            d = lax.dot_general(a, b, dims, preferred_element_type=F32)
            acc = d if acc is None else acc + d
        k = 2 * n_terms
        if bias is not None:
            acc = acc + refs[k][...]
            k += 1
        if res is not None:
            acc = acc + res_alpha * refs[k][...]
        if out_scale is not None:
            acc = acc * out_scale
        o_ref[...] = acc.astype(o_ref.dtype)

    operands, specs = [], []
    for a, a_spec, b, b_spec in terms:
        operands += [a, b]
        specs += [a_spec, b_spec]
    for extra in (bias, res):
        if extra is not None:
            operands.append(extra[0])
            specs.append(extra[1])
    aliases = {}
    if into is not None:
        aliases = {len(operands): 0}
        operands.append(into)
        specs.append(pl.BlockSpec(memory_space=pl.ANY))
        out_sds = jax.ShapeDtypeStruct(into.shape, into.dtype)
    return pl.pallas_call(body, out_shape=out_sds, grid=grid, in_specs=specs, out_specs=out_spec, input_output_aliases=aliases,
                          compiler_params=_params(("parallel",) * len(grid)), name=name)(*operands)


def _mm_fanout(name, a, a_spec, w4, w_block, layer, dims, out_spec, out_sds, store, grid, bias4=None, res=None, res_alpha=1.0):
    def body(*refs):
        a_ref, w_refs, o_ref = refs[0], refs[1:1 + N_CHIPS], refs[-1]
        k = 1 + N_CHIPS
        b_refs = refs[k:k + N_CHIPS] if bias4 is not None else None
        k += N_CHIPS if bias4 is not None else 0
        av = a_ref[...].astype(BF16)
        for j in range(N_CHIPS):
            d = lax.dot_general(av, w_refs[j][...].astype(BF16), dims, preferred_element_type=F32)
            if b_refs is not None:
                d = d + b_refs[j][...]
            if res is not None:
                d = d + res_alpha * res[2](refs[k], j)
            store(o_ref, j, d)

    nd = len(grid)
    operands = [a] + [w4] * N_CHIPS
    specs = [a_spec] + [pl.BlockSpec((None, None) + w_block, lambda *g, j=j: (j, layer, 0, 0)) for j in range(N_CHIPS)]
    if bias4 is not None:
        operands += [bias4] * N_CHIPS
        specs += [pl.BlockSpec((None, None, 1, bias4.shape[-1]), lambda *g, j=j: (j, layer, 0, 0)) for j in range(N_CHIPS)]
    if res is not None:
        operands.append(res[0])
        specs.append(res[1])
    return pl.pallas_call(body, out_shape=out_sds, grid=grid, in_specs=specs, out_specs=out_spec,
                          compiler_params=_params(("parallel",) * nd), name=name)(*operands)


def _store_slot(o_ref, j, d):
    o_ref[j] = d.astype(o_ref.dtype)


def mm_colsm(name, x, w4, layer, bias4=None, tm=512):
    M, K = x.shape
    n = w4.shape[-1]
    tm = _tile(M, tm)
    return _mm_fanout(name, x, pl.BlockSpec((tm, K), lambda i: (i, 0)), w4, (K, n), layer, NN,
                      pl.BlockSpec((N_CHIPS, tm, n), lambda i: (0, i, 0)), jax.ShapeDtypeStruct((N_CHIPS, M, n), F32),
                      _store_slot, (M // tm,), bias4=bias4)


def mm_rowsm(name, a, w4, layer, a_sm=False, bias=None, out_dtype=F32, out_scale=None, tm=512):
    kc, N = w4.shape[-2:]
    M = a.shape[-2]
    tm = _tile(M, tm)
    terms = []
    for j in range(N_CHIPS):
        if a_sm:
            a_spec = pl.BlockSpec((None, tm, kc), lambda i, j=j: (j, i, 0))
        else:
            a_spec = pl.BlockSpec((tm, kc), lambda i, j=j: (i, j))
        terms.append((a, a_spec, w4, pl.BlockSpec((None, None, kc, N), lambda i, j=j: (j, layer, 0, 0))))
    b = None if bias is None else (bias, pl.BlockSpec((None, 1, N), lambda i: (layer, 0, 0)))
    return _mm_call(name, (M // tm,), terms, pl.BlockSpec((tm, N), lambda i: (i, 0)), jax.ShapeDtypeStruct((M, N), out_dtype),
                    NN, bias=b, out_scale=out_scale)


def mm_nt_rowsm(name, dy, w4, layer, out_sm=False, res=None, res_alpha=1.0, tm=512):
    kc, N = w4.shape[-2:]
    M = dy.shape[0]
    tm = _tile(M, tm)

    def store_cols(o_ref, j, d):
        o_ref[:, j * kc:(j + 1) * kc] = d

    if out_sm:
        out_spec, sds, store = pl.BlockSpec((N_CHIPS, tm, kc), lambda i: (0, i, 0)), jax.ShapeDtypeStruct((N_CHIPS, M, kc), F32), _store_slot
    else:
        out_spec, sds, store = pl.BlockSpec((tm, N_CHIPS * kc), lambda i: (i, 0)), jax.ShapeDtypeStruct((M, N_CHIPS * kc), F32), store_cols
    r = None if res is None else (res, pl.BlockSpec((tm, N_CHIPS * kc), lambda i: (i, 0)), lambda ref, j: ref[:, j * kc:(j + 1) * kc])
    return _mm_fanout(name, dy, pl.BlockSpec((tm, N), lambda i: (i, 0)), w4, (kc, N), layer, NT, out_spec, sds, store, (M // tm,),
                      res=r, res_alpha=res_alpha)


def mm_nt_colsm(name, dy_parts, w4, layer, res=None, res_alpha=1.0, tm=512):
    K, n = w4.shape[-2:]
    M = dy_parts[0][0].shape[1]
    tm = _tile(M, tm)
    terms = [(arr, pl.BlockSpec((None, tm, n), lambda i, idx=idx: (idx, i, 0)), w4,
              pl.BlockSpec((None, None, K, n), lambda i, j=j: (j, layer, 0, 0))) for j, (arr, idx) in enumerate(dy_parts)]
    r = None if res is None else (res, pl.BlockSpec((tm, K), lambda i: (i, 0)))
    return _mm_call(name, (M // tm,), terms, pl.BlockSpec((tm, K), lambda i: (i, 0)), jax.ShapeDtypeStruct((M, K), F32), NT,
                    res=r, res_alpha=res_alpha)


def _sm_parts(a):
    return [(a, j) for j in range(a.shape[0])]


def mm_tn_colsm(name, x, dy, into, off, j0=0, tk=512):
    M, K = x.shape
    nj, _, n = dy.shape
    tk = _tile(K, tk, LANES)
    assert off % tk == 0
    terms = [(x, pl.BlockSpec((M, tk), lambda j, k: (0, k)), dy, pl.BlockSpec((None, M, n), lambda j, k: (j, 0, 0)))]
    return _mm_call(name, (nj, K // tk), terms, pl.BlockSpec((None, tk, n), lambda j, k: (j + j0, off // tk + k, 0)),
                    None, TN, into=into)


def mm_tn_rowsm(name, a, dy, into, off, a_sm=False, tn=512):
    M, N = dy.shape
    kc = a.shape[-1] if a_sm else a.shape[-1] // N_CHIPS
    tn = _tile(N, tn, LANES)
    assert off % kc == 0
    a_spec = pl.BlockSpec((None, M, kc), lambda j, n: (j, 0, 0)) if a_sm else pl.BlockSpec((M, kc), lambda j, n: (0, j))
    terms = [(a, a_spec, dy, pl.BlockSpec((M, tn), lambda j, n: (0, n)))]
    return _mm_call(name, (N_CHIPS, N // tn), terms, pl.BlockSpec((None, kc, tn), lambda j, n: (j, off // kc, n)),
                    None, TN, into=into)


def mm_proj(name, p4, layer, w4, tm=512):
    S, P = p4.shape[-2:]
    n = w4.shape[-1]
    tm = _tile(S, tm)

    def store_cols(o_ref, j, d):
        o_ref[:, j * n:(j + 1) * n] = d

    return _mm_fanout(name, p4, pl.BlockSpec((None, None, tm, P), lambda i: (layer, 0, i, 0)), w4, (P, n), layer, NN,
                      pl.BlockSpec((tm, N_CHIPS * n), lambda i: (i, 0)), jax.ShapeDtypeStruct((S, N_CHIPS * n), F32),
                      store_cols, (S // tm,))


def mm_tn_proj(name, p4, layer, dpp, into, off):
    S, P = p4.shape[-2:]
    n = dpp.shape[-1] // N_CHIPS
    assert off % P == 0
    terms = [(p4, pl.BlockSpec((None, None, S, P), lambda j: (layer, 0, 0, 0)), dpp, pl.BlockSpec((S, n), lambda j: (0, j)))]
    return _mm_call(name, (N_CHIPS,), terms, pl.BlockSpec((None, P, n), lambda j: (j, off // P, 0)), None, TN, into=into)


def ln_fwd(name, x, mix, g, b, layer, tg=None, pp=None, tm=256):
    S, D = x.shape
    tm = _tile(S, tm, 16)
    ple = tg is not None

    def body(*refs):
        if ple:
            x_ref, m_ref, tg_ref, pp_ref, g_ref, b_ref, y_ref, yb_ref, xh_ref, rs_ref = refs
        else:
            x_ref, m_ref, g_ref, b_ref, y_ref, yb_ref, xh_ref, rs_ref = refs
        r = DN_ALPHA * x_ref[...] + m_ref[...]
        if ple:
            r = r + _sigmoid(tg_ref[...]) * pp_ref[...]
        mu = jnp.mean(r, axis=-1, keepdims=True)
        d = r - mu
        var = jnp.mean(d * d, axis=-1, keepdims=True)
        rstd = lax.rsqrt(var + LN_EPS)
        xh = d * rstd
        y = xh * g_ref[...] + b_ref[...]
        y_ref[...] = y
        yb_ref[...] = y.astype(BF16)
        xh_ref[...] = xh
        rs_ref[...] = rstd

    row = pl.BlockSpec((tm, D), lambda i: (i, 0))
    vec = pl.BlockSpec((None, 1, D), lambda i: (layer, 0, 0))
    ins = [x, mix] + ([tg, pp] if ple else []) + [g, b]
    specs = [row, row] + ([row, row] if ple else []) + [vec, vec]
    return pl.pallas_call(body, grid=(S // tm,), in_specs=specs,
                          out_specs=(row, row, row, pl.BlockSpec((tm, 1), lambda i: (i, 0))),
                          out_shape=(jax.ShapeDtypeStruct((S, D), F32), jax.ShapeDtypeStruct((S, D), BF16),
                                     jax.ShapeDtypeStruct((S, D), F32), jax.ShapeDtypeStruct((S, 1), F32)),
                          compiler_params=_params(("parallel",)), name=name)(*ins)


def ln_bwd(name, dy, xh, rstd, g, layer, tg=None, pp=None, tm=256):
    S, D = dy.shape
    tm = _tile(S, tm)
    ple = tg is not None

    def body(*refs):
        if ple:
            dy_ref, xh_ref, rs_ref, g_ref, tg_ref, pp_ref, dr_ref, acc_ref, dtg_ref, dpp_ref = refs
        else:
            dy_ref, xh_ref, rs_ref, g_ref, dr_ref, acc_ref = refs
        dy_, xh_ = dy_ref[...], xh_ref[...]
        dxh = dy_ * g_ref[...]
        m1 = jnp.mean(dxh, axis=-1, keepdims=True)
        m2 = jnp.mean(dxh * xh_, axis=-1, keepdims=True)
        dr = rs_ref[...] * (dxh - m1 - xh_ * m2)
        dr_ref[...] = dr

        @pl.when(pl.program_id(0) == 0)
        def _():
            acc_ref[...] = jnp.zeros_like(acc_ref)
        acc_ref[0:1, :] += jnp.sum(dy_ * xh_, axis=0, keepdims=True)
        acc_ref[1:2, :] += jnp.sum(dy_, axis=0, keepdims=True)
        acc_ref[2:3, :] += jnp.sum(dr, axis=0, keepdims=True)
        if ple:
            pg = _sigmoid(tg_ref[...])
            dtg_ref[...] = (dr * pp_ref[...] * pg * (1.0 - pg)).astype(BF16)
            dpp_ref[...] = (dr * pg).astype(BF16)

    row = pl.BlockSpec((tm, D), lambda i: (i, 0))
    ins = [dy, xh, rstd, g] + ([tg, pp] if ple else [])
    specs = [row, row, pl.BlockSpec((tm, 1), lambda i: (i, 0)), pl.BlockSpec((None, 1, D), lambda i: (layer, 0, 0))] + ([row, row] if ple else [])
    outs = [jax.ShapeDtypeStruct((S, D), F32), jax.ShapeDtypeStruct((8, D), F32)]
    out_specs = [row, pl.BlockSpec((8, D), lambda i: (0, 0))]
    if ple:
        outs += [jax.ShapeDtypeStruct((S, D), BF16)] * 2
        out_specs += [row, row]
    return pl.pallas_call(body, grid=(S // tm,), in_specs=specs, out_specs=tuple(out_specs), out_shape=tuple(outs),
                          compiler_params=_params(("arbitrary",)), name=name)(*ins)


def glu_fwd(name, h_sm, tm=512):
    _, S, n = h_sm.shape
    tm = _tile(S, tm)
    half = N_CHIPS // 2

    def body(a_ref, g_ref, u_ref):
        u_ref[...] = a_ref[...] * _sigmoid(g_ref[...])

    return pl.pallas_call(body, grid=(half, S // tm),
                          in_specs=[pl.BlockSpec((None, tm, n), lambda j, i: (j, i, 0)),
                                    pl.BlockSpec((None, tm, n), lambda j, i: (j + half, i, 0))],
                          out_specs=pl.BlockSpec((tm, n), lambda j, i: (i, j)),
                          out_shape=jax.ShapeDtypeStruct((S, half * n), F32),
                          compiler_params=_params(("parallel", "parallel")), name=name)(h_sm, h_sm)


def conv_ln_silu_fwd(name, u, w, b, g, beta, layer, ts=128):
    S, D = u.shape
    kw = w.shape[1]
    ts = _tile(S, ts, CONV_HALO)
    lc = LANES if D % LANES == 0 else D

    def body(h_ref, u_ref, w_ref, b_ref, g_ref, be_ref, c_ref, s_ref, win_ref):
        i = pl.program_id(0)
        win_ref[0:CONV_HALO, :] = jnp.where(i == 0, 0.0, h_ref[...])
        win_ref[CONV_HALO:, :] = u_ref[...]
        for cc in range(D // lc):
            cs = slice(cc * lc, (cc + 1) * lc)
            acc = jnp.zeros((ts, lc), F32) + b_ref[:, cs]
            for k in range(kw):
                off = CONV_HALO - (kw - 1) + k
                acc = acc + w_ref[k:k + 1, cs] * win_ref[off:off + ts, cs]
            c_ref[:, cs] = acc
        c = c_ref[...]
        mu = jnp.mean(c, axis=-1, keepdims=True)
        d = c - mu
        var = jnp.mean(d * d, axis=-1, keepdims=True)
        nrm = d * lax.rsqrt(var + LN_EPS) * g_ref[...] + be_ref[...]
        s_ref[...] = (nrm * _sigmoid(nrm)).astype(BF16)

    row = pl.BlockSpec((ts, D), lambda i: (i, 0))
    vec = pl.BlockSpec((None, 1, D), lambda i: (layer, 0, 0))
    halo = pl.BlockSpec((CONV_HALO, D), lambda i: (jnp.maximum(i * (ts // CONV_HALO) - 1, 0), 0))
    return pl.pallas_call(body, grid=(S // ts,),
                          in_specs=[halo, row, pl.BlockSpec((None, kw, D), lambda i: (layer, 0, 0)), vec, vec, vec],
                          out_specs=(row, row),
                          out_shape=(jax.ShapeDtypeStruct((S, D), F32), jax.ShapeDtypeStruct((S, D), BF16)),
                          scratch_shapes=[pltpu.VMEM((ts + CONV_HALO, D), F32)],
                          compiler_params=_params(("parallel",)), name=name)(u, u, w, b, g, beta)


def ln_silu_bwd(name, ds, c, g, beta, layer, tm=256):
    S, D = c.shape
    tm = _tile(S, tm)

    def body(ds_ref, c_ref, g_ref, be_ref, dc_ref, acc_ref):
        c_ = c_ref[...]
        mu = jnp.mean(c_, axis=-1, keepdims=True)
        d = c_ - mu
        var = jnp.mean(d * d, axis=-1, keepdims=True)
        rstd = lax.rsqrt(var + LN_EPS)
        xh = d * rstd
        nrm = xh * g_ref[...] + be_ref[...]
        sg = _sigmoid(nrm)
        dn = ds_ref[...] * (sg * (1.0 + nrm * (1.0 - sg)))
        dxh = dn * g_ref[...]
        m1 = jnp.mean(dxh, axis=-1, keepdims=True)
        m2 = jnp.mean(dxh * xh, axis=-1, keepdims=True)
        dc = rstd * (dxh - m1 - xh * m2)
        dc_ref[...] = dc

        @pl.when(pl.program_id(0) == 0)
        def _():
            acc_ref[...] = jnp.zeros_like(acc_ref)
        acc_ref[0:1, :] += jnp.sum(dn * xh, axis=0, keepdims=True)
        acc_ref[1:2, :] += jnp.sum(dn, axis=0, keepdims=True)
        acc_ref[2:3, :] += jnp.sum(dc, axis=0, keepdims=True)

    row = pl.BlockSpec((tm, D), lambda i: (i, 0))
    vec = pl.BlockSpec((None, 1, D), lambda i: (layer, 0, 0))
    return pl.pallas_call(body, grid=(S // tm,), in_specs=[row, row, vec, vec],
                          out_specs=(row, pl.BlockSpec((8, D), lambda i: (0, 0))),
                          out_shape=(jax.ShapeDtypeStruct((S, D), F32), jax.ShapeDtypeStruct((8, D), F32)),
                          compiler_params=_params(("arbitrary",)), name=name)(ds, c, g, beta)


def conv_glu_bwd(name, dc, u, h_sm, w, layer, ts=128):
    S, D = dc.shape
    kw = w.shape[1]
    half = N_CHIPS // 2
    n = D // half
    ts = _tile(S, ts, CONV_HALO)
    nblk = S // ts
    lc = LANES if n % LANES == 0 else n

    def body(dc_ref, dcn_ref, u_ref, up_ref, a_ref, g_ref, w_ref, da_ref, dg_ref, dw_ref, dba_ref, dbg_ref, dwin_ref, uwin_ref):
        i = pl.program_id(1)
        dwin_ref[0:ts, :] = dc_ref[...]
        dwin_ref[ts:, :] = jnp.where(i == nblk - 1, 0.0, dcn_ref[...])
        uwin_ref[0:CONV_HALO, :] = jnp.where(i == 0, 0.0, up_ref[...])
        uwin_ref[CONV_HALO:, :] = u_ref[...]

        @pl.when(i == 0)
        def _():
            dw_ref[...] = jnp.zeros_like(dw_ref)
            dba_ref[...] = jnp.zeros_like(dba_ref)
            dbg_ref[...] = jnp.zeros_like(dbg_ref)

        for cc in range(n // lc):
            cs = slice(cc * lc, (cc + 1) * lc)
            dcb = dwin_ref[0:ts, cs]
            du = jnp.zeros((ts, lc), F32)
            for k in range(kw):
                du = du + w_ref[k:k + 1, cs] * dwin_ref[kw - 1 - k:kw - 1 - k + ts, cs]
                off = CONV_HALO - (kw - 1) + k
                dw_ref[k:k + 1, cs] += jnp.sum(dcb * uwin_ref[off:off + ts, cs], axis=0, keepdims=True)
            a = a_ref[:, cs]
            sg = _sigmoid(g_ref[:, cs])
            da = du * sg
            dg = du * a * sg * (1.0 - sg)
            da_ref[:, cs] = da.astype(BF16)
            dg_ref[:, cs] = dg.astype(BF16)
            dba_ref[0:1, cs] += jnp.sum(da, axis=0, keepdims=True)
            dbg_ref[0:1, cs] += jnp.sum(dg, axis=0, keepdims=True)

    r = ts // CONV_HALO
    main = pl.BlockSpec((ts, n), lambda j, i: (i, j))
    nxt = pl.BlockSpec((CONV_HALO, n), lambda j, i: (jnp.minimum((i + 1) * r, S // CONV_HALO - 1), j))
    prv = pl.BlockSpec((CONV_HALO, n), lambda j, i: (jnp.maximum(i * r - 1, 0), j))
    sm_a = pl.BlockSpec((None, ts, n), lambda j, i: (j, i, 0))
    sm_g = pl.BlockSpec((None, ts, n), lambda j, i: (j + half, i, 0))
    acc8a = pl.BlockSpec((None, 8, n), lambda j, i: (j, 0, 0))
    acc8g = pl.BlockSpec((None, 8, n), lambda j, i: (j + half, 0, 0))
    da, dg, dw, dba, dbg = pl.pallas_call(
        body, grid=(half, nblk),
        in_specs=[main, nxt, main, prv, sm_a, sm_g, pl.BlockSpec((None, kw, n), lambda j, i: (layer, 0, j))],
        out_specs=(pl.BlockSpec((None, ts, n), lambda j, i: (j, i, 0)), pl.BlockSpec((None, ts, n), lambda j, i: (j, i, 0)),
                   pl.BlockSpec((None, 32, n), lambda j, i: (j, 0, 0)),
                   pl.BlockSpec((None, 8, n), lambda j, i: (j, 0, 0)), pl.BlockSpec((None, 8, n), lambda j, i: (j, 0, 0))),
        out_shape=(jax.ShapeDtypeStruct((half, S, n), BF16), jax.ShapeDtypeStruct((half, S, n), BF16),
                   jax.ShapeDtypeStruct((half, 32, n), F32),
                   jax.ShapeDtypeStruct((half, 8, n), F32), jax.ShapeDtypeStruct((half, 8, n), F32)),
        scratch_shapes=[pltpu.VMEM((ts + CONV_HALO, n), F32), pltpu.VMEM((ts + CONV_HALO, n), F32)],
        compiler_params=_params(("parallel", "arbitrary")), name=name)(dc, dc, u, u, h_sm, h_sm, w)
    del acc8a, acc8g
    return da, dg, dw, dba, dbg


ROW_CHUNK = 16


def _ffn_gc(win_ref, w_ref, b_ref, r0, rows, kw, base):
    gc = b_ref[...] + jnp.zeros((rows, win_ref.shape[1]), F32)
    for k in range(kw):
        off = r0 + base - (kw - 1) + k
        gc = gc + w_ref[k:k + 1, :] * win_ref[off:off + rows, :]
    return gc


def ffn_gate_fwd(name, up_sm, gp_sm, w4, b4, layer, ts=256):
    _, S, n = up_sm.shape
    kw = w4.shape[2]
    ts = _tile(S, ts, ROW_CHUNK)
    rc = ROW_CHUNK

    def body(up_ref, gp_ref, gph_ref, w_ref, b_ref, hf_ref, win_ref):
        i = pl.program_id(1)
        win_ref[0:FFN_HALO, :] = jnp.where(i == 0, 0.0, gph_ref[...])
        win_ref[FFN_HALO:, :] = gp_ref[...]
        for r0 in range(0, ts, rc):
            gc = _ffn_gc(win_ref, w_ref, b_ref, r0, rc, kw, FFN_HALO)
            hf_ref[r0:r0 + rc, :] = (gc * _sigmoid(gc) * up_ref[r0:r0 + rc, :]).astype(BF16)

    main = pl.BlockSpec((None, ts, n), lambda j, i: (j, i, 0))
    prv = pl.BlockSpec((None, FFN_HALO, n), lambda j, i: (j, jnp.maximum(i * (ts // FFN_HALO) - 1, 0), 0))
    return pl.pallas_call(body, grid=(N_CHIPS, S // ts),
                          in_specs=[main, main, prv, pl.BlockSpec((None, None, kw, n), lambda j, i: (j, layer, 0, 0)),
                                    pl.BlockSpec((None, None, 1, n), lambda j, i: (layer, j, 0, 0))],
                          out_specs=main, out_shape=jax.ShapeDtypeStruct((N_CHIPS, S, n), BF16),
                          scratch_shapes=[pltpu.VMEM((ts + FFN_HALO, n), F32)],
                          compiler_params=_params(("parallel", "parallel")), name=name)(up_sm, gp_sm, gp_sm, w4, b4)


def ffn_gate_bwd_a(name, dhf_sm, up_sm, gp_sm, w4, b4, layer, ts=256):
    _, S, n = up_sm.shape
    kw = w4.shape[2]
    ts = _tile(S, ts, ROW_CHUNK)
    rc = ROW_CHUNK

    def body(dhf_ref, up_ref, gp_ref, gph_ref, w_ref, b_ref, dup_ref, dgc_ref, win_ref):
        i = pl.program_id(1)
        win_ref[0:FFN_HALO, :] = jnp.where(i == 0, 0.0, gph_ref[...])
        win_ref[FFN_HALO:, :] = gp_ref[...]
        for r0 in range(0, ts, rc):
            rows = slice(r0, r0 + rc)
            gc = _ffn_gc(win_ref, w_ref, b_ref, r0, rc, kw, FFN_HALO)
            sg = _sigmoid(gc)
            dhf = dhf_ref[rows, :]
            dup_ref[rows, :] = (dhf * gc * sg).astype(BF16)
            dgc_ref[rows, :] = dhf * up_ref[rows, :] * (sg * (1.0 + gc * (1.0 - sg)))

    main = pl.BlockSpec((None, ts, n), lambda j, i: (j, i, 0))
    prv = pl.BlockSpec((None, FFN_HALO, n), lambda j, i: (j, jnp.maximum(i * (ts // FFN_HALO) - 1, 0), 0))
    return pl.pallas_call(body, grid=(N_CHIPS, S // ts),
                          in_specs=[main, main, main, prv, pl.BlockSpec((None, None, kw, n), lambda j, i: (j, layer, 0, 0)),
                                    pl.BlockSpec((None, None, 1, n), lambda j, i: (layer, j, 0, 0))],
                          out_specs=(main, main),
                          out_shape=(jax.ShapeDtypeStruct((N_CHIPS, S, n), BF16), jax.ShapeDtypeStruct((N_CHIPS, S, n), F32)),
                          scratch_shapes=[pltpu.VMEM((ts + FFN_HALO, n), F32)],
                          compiler_params=_params(("parallel", "parallel")), name=name)(dhf_sm, up_sm, gp_sm, gp_sm, w4, b4)


def ffn_gate_bwd_b(name, dgc_sm, gp_sm, w4, layer, ts=256):
    _, S, n = gp_sm.shape
    kw = w4.shape[2]
    ts = _tile(S, ts, ROW_CHUNK)
    rc = ROW_CHUNK
    nblk = S // ts

    def body(dgc_ref, dgn_ref, gp_ref, gph_ref, w_ref, dgp_ref, acc_ref, dwin_ref, gwin_ref):
        i = pl.program_id(1)
        dwin_ref[0:ts, :] = dgc_ref[...]
        dwin_ref[ts:, :] = jnp.where(i == nblk - 1, 0.0, dgn_ref[...])
        gwin_ref[0:FFN_HALO, :] = jnp.where(i == 0, 0.0, gph_ref[...])
        gwin_ref[FFN_HALO:, :] = gp_ref[...]

        @pl.when(i == 0)
        def _():
            acc_ref[...] = jnp.zeros_like(acc_ref)
        sums = [jnp.zeros((8, n), F32) for _ in range(kw + 1)]
        for r0 in range(0, ts, rc):
            dgc = dwin_ref[r0:r0 + rc, :]
            dgp = jnp.zeros((rc, n), F32)
            for k in range(kw):
                dgp = dgp + w_ref[k:k + 1, :] * dwin_ref[r0 + kw - 1 - k:r0 + kw - 1 - k + rc, :]
                off = r0 + FFN_HALO - (kw - 1) + k
                prod = dgc * gwin_ref[off:off + rc, :]
                sums[k] = sums[k] + prod[0:8, :] + prod[8:16, :]
            sums[kw] = sums[kw] + dgc[0:8, :] + dgc[8:16, :]
            dgp_ref[r0:r0 + rc, :] = dgp.astype(BF16)
        for k in range(kw):
            acc_ref[k:k + 1, :] += jnp.sum(sums[k], axis=0, keepdims=True)
        acc_ref[7:8, :] += jnp.sum(sums[kw], axis=0, keepdims=True)

    r = ts // FFN_HALO
    main = pl.BlockSpec((None, ts, n), lambda j, i: (j, i, 0))
    nxt = pl.BlockSpec((None, FFN_HALO, n), lambda j, i: (j, jnp.minimum((i + 1) * r, S // FFN_HALO - 1), 0))
    prv = pl.BlockSpec((None, FFN_HALO, n), lambda j, i: (j, jnp.maximum(i * r - 1, 0), 0))
    return pl.pallas_call(body, grid=(N_CHIPS, nblk),
                          in_specs=[main, nxt, main, prv, pl.BlockSpec((None, None, kw, n), lambda j, i: (j, layer, 0, 0))],
                          out_specs=(main, pl.BlockSpec((None, 8, n), lambda j, i: (j, 0, 0))),
                          out_shape=(jax.ShapeDtypeStruct((N_CHIPS, S, n), BF16), jax.ShapeDtypeStruct((N_CHIPS, 8, n), F32)),
                          scratch_shapes=[pltpu.VMEM((ts + FFN_HALO, n), F32), pltpu.VMEM((ts + FFN_HALO, n), F32)],
                          compiler_params=_params(("parallel", "arbitrary")), name=name)(dgc_sm, dgc_sm, gp_sm, gp_sm, w4)


def _neg_softplus(z):
    e = jnp.exp(-jnp.abs(z))
    return -(jnp.maximum(z, 0.0) + jnp.log(1.0 + e)), e


def _split_dot(x, t):
    hi = x.astype(BF16)
    lo = (x - hi.astype(F32)).astype(BF16)
    return jnp.dot(hi, t, preferred_element_type=F32) + jnp.dot(lo, t, preferred_element_type=F32)


STICK_GONE = -100.0
NOT_SWEPT = -1e30


def attn_fwd(name, q, k, v, bq=512, w=256):
    S, D = q.shape
    dh = HEAD_DIM
    hpb = LANES // dh
    bq = _tile(S, bq)
    w = _tile(bq, w)
    nsub = bq // w
    nkb = S // w

    def body(q_ref, k_ref, v_ref, o_ref, runs_ref, rs_ref):
        qi = pl.program_id(1)
        rr = lax.broadcasted_iota(jnp.int32, (w, w), 0)
        cc = lax.broadcasted_iota(jnp.int32, (w, w), 1)
        t_suf = (rr >= cc).astype(BF16)
        tq = qi * bq + lax.broadcasted_iota(jnp.int32, (bq, w), 0)
        tk = lax.broadcasted_iota(jnp.int32, (bq, w), 1)
        lane = lax.broadcasted_iota(jnp.int32, (bq, LANES), 1)
        ntot = (qi + 1) * nsub
        heads = [slice(hh * dh, (hh + 1) * dh) for hh in range(hpb)]
        qbs = [q_ref[:, hs] for hs in heads]
        for hh in range(hpb):
            rs_ref[hh] = jnp.where(lane < ntot, NOT_SWEPT, 0.0)

        def block(kb, carry, masked):
            kstart = pl.multiple_of(kb * w, w)
            if masked:
                m = (tk + kstart) < tq
            out = []
            for hh, hs in enumerate(heads):
                run, acc = carry[2 * hh], carry[2 * hh + 1]
                kblk = k_ref[pl.ds(kstart, w), hs]
                vblk = v_ref[pl.ds(kstart, w), hs]
                z = lax.dot_general(qbs[hh], kblk, NT, preferred_element_type=F32)
                lg, _ = _neg_softplus(z)
                if masked:
                    lg = jnp.where(m, lg, 0.0)
                cum = _split_dot(lg, t_suf) + run
                a = jnp.exp(z + cum)
                if masked:
                    a = jnp.where(m, a, 0.0)
                acc = acc + jnp.dot(a.astype(BF16), vblk, preferred_element_type=F32)
                run = cum[:, 0:1]
                rs_ref[hh] = jnp.where(lane == kb, run, rs_ref[hh])
                out += [run, acc]
            return tuple(out)

        carry = (jnp.zeros((bq, 1), F32), jnp.zeros((bq, dh), F32)) * hpb
        for sb in reversed(range(nsub)):
            carry = block(qi * nsub + sb, carry, True)

        def cond(c):
            alive = functools.reduce(jnp.maximum, [jnp.max(c[1 + 2 * hh]) for hh in range(hpb)])
            return jnp.logical_and(c[0] >= 0, alive > STICK_GONE)

        def step(c):
            return (c[0] - 1,) + block(c[0], c[1:], False)
        carry = lax.while_loop(cond, step, (qi * nsub - 1,) + carry)[1:]
        for hh, hs in enumerate(heads):
            o_ref[:, hs] = carry[2 * hh + 1]
            runs_ref[hh] = rs_ref[hh, :, 0:nkb]

    qs = pl.BlockSpec((bq, LANES), lambda h, i: (i, h))
    kv = pl.BlockSpec((S, LANES), lambda h, i: (0, h))
    return pl.pallas_call(body, grid=(D // LANES, S // bq), in_specs=[qs, kv, kv],
                          out_specs=(qs, pl.BlockSpec((hpb, bq, nkb), lambda h, i: (h, i, 0))),
                          out_shape=(jax.ShapeDtypeStruct((S, D), F32), jax.ShapeDtypeStruct((D // dh, S, nkb), F32)),
                          scratch_shapes=[pltpu.VMEM((hpb, bq, LANES), F32)],
                          compiler_params=_params(("parallel", "parallel")), name=name)(q, k, v)


def attn_bwd(name, q, k, v, do, runs, dk0=None, dv0=None, bq=512, w=256):
    S, D = q.shape
    dh = HEAD_DIM
    hpb = LANES // dh
    bq = _tile(S, bq)
    w = _tile(bq, w)
    nsub = bq // w
    nkb = S // w
    scale = 1.0 / math.sqrt(dh)
    init = dk0 is not None

    def body(*refs):
        if init:
            q_ref, k_ref, v_ref, do_ref, runs_ref, dk0_ref, dv0_ref, dq_ref, dk_ref, dv_ref, rs_ref = refs
        else:
            q_ref, k_ref, v_ref, do_ref, runs_ref, dq_ref, dk_ref, dv_ref, rs_ref = refs
        qi = pl.program_id(1)

        @pl.when(qi == 0)
        def _():
            dk_ref[...] = dk0_ref[...] if init else jnp.zeros_like(dk_ref)
            dv_ref[...] = dv0_ref[...] if init else jnp.zeros_like(dv_ref)

        rr = lax.broadcasted_iota(jnp.int32, (w, w), 0)
        cc = lax.broadcasted_iota(jnp.int32, (w, w), 1)
        t_suf = (rr >= cc).astype(BF16)
        t_pre = (rr <= cc).astype(BF16)
        tq = qi * bq + lax.broadcasted_iota(jnp.int32, (bq, w), 0)
        tk = lax.broadcasted_iota(jnp.int32, (bq, w), 1)
        lane = lax.broadcasted_iota(jnp.int32, (bq, LANES), 1)
        lane1 = lax.broadcasted_iota(jnp.int32, (1, LANES), 1)
        ntot = (qi + 1) * nsub
        heads = [slice(hh * dh, (hh + 1) * dh) for hh in range(hpb)]
        qbs = [q_ref[:, hs] for hs in heads]
        dobs = [do_ref[:, hs].astype(BF16) for hs in heads]
        kb0 = ntot - nsub
        for hh in range(hpb):
            rs_ref[hh] = jnp.zeros((bq, LANES), F32)
            rs_ref[hh, :, 0:nkb] = runs_ref[hh]
            colmax = jnp.max(rs_ref[hh], axis=0, keepdims=True)
            dead = jnp.logical_and(jnp.logical_and(lane1 >= 1, lane1 <= ntot), colmax <= STICK_GONE)
            kb0 = jnp.minimum(kb0, jnp.sum(dead.astype(jnp.int32)))

        def block(kb, carry, masked):
            kstart = pl.multiple_of(kb * w, w)
            if masked:
                m = (tk + kstart) < tq
            out = []
            for hh, hs in enumerate(heads):
                pg_run, dq = carry[2 * hh], carry[2 * hh + 1]
                qb, dob = qbs[hh], dobs[hh]
                kblk = k_ref[pl.ds(kstart, w), hs]
                vblk = v_ref[pl.ds(kstart, w), hs]
                right = jnp.sum(jnp.where(lane == kb + 1, rs_ref[hh], 0.0), axis=1, keepdims=True)
                z = lax.dot_general(qb, kblk, NT, preferred_element_type=F32)
                lg, e = _neg_softplus(z)
                if masked:
                    lg = jnp.where(m, lg, 0.0)
                a = jnp.exp(z + _split_dot(lg, t_suf) + right)
                if masked:
                    a = jnp.where(m, a, 0.0)
                da = lax.dot_general(dob, vblk, NT, preferred_element_type=F32)
                g = da * a
                pin = _split_dot(g, t_pre) + pg_run
                sig = jnp.where(z >= 0.0, 1.0, e) / (1.0 + e)
                dz = g - sig * pin
                if masked:
                    dz = jnp.where(m, dz, 0.0)
                dzb = dz.astype(BF16)
                dq = dq + jnp.dot(dzb, kblk, preferred_element_type=F32)
                dk_ref[pl.ds(kstart, w), hs] += lax.dot_general(dzb, qb, TN, preferred_element_type=F32)
                dv_ref[pl.ds(kstart, w), hs] += lax.dot_general(a.astype(BF16), dob, TN, preferred_element_type=F32)
                out += [pin[:, w - 1:w], dq]
            return tuple(out)

        carry = (jnp.zeros((bq, 1), F32), jnp.zeros((bq, dh), F32)) * hpb
        carry = lax.fori_loop(kb0, qi * nsub, lambda kb, c: block(kb, c, False), carry)
        for sb in range(nsub):
            carry = block(qi * nsub + sb, carry, True)
        for hh, hs in enumerate(heads):
            dq_ref[:, hs] = carry[2 * hh + 1] * scale

    qs = pl.BlockSpec((bq, LANES), lambda h, i: (i, h))
    kv = pl.BlockSpec((S, LANES), lambda h, i: (0, h))
    ins = [q, k, v, do, runs] + ([dk0, dv0] if init else [])
    specs = [qs, kv, kv, qs, pl.BlockSpec((hpb, bq, nkb), lambda h, i: (h, i, 0))] + ([kv, kv] if init else [])
    sds = jax.ShapeDtypeStruct((S, D), F32)
    return pl.pallas_call(body, grid=(D // LANES, S // bq), in_specs=specs, out_specs=(qs, kv, kv), out_shape=(sds, sds, sds),
                          scratch_shapes=[pltpu.VMEM((hpb, bq, LANES), F32)],
                          compiler_params=_params(("parallel", "arbitrary")), name=name)(*ins)


def loss_head(name, y, tgt, tm=512):
    S, D = y.shape
    tm = _tile(S, tm)

    def body(y_ref, t_ref, dy_ref, acc_ref):
        @pl.when(pl.program_id(0) == 0)
        def _():
            acc_ref[...] = jnp.zeros_like(acc_ref)
        e = y_ref[...] - t_ref[...]
        dy_ref[...] = e * (1.0 / D)
        acc_ref[...] += jnp.sum(e * e)

    row = pl.BlockSpec((tm, D), lambda i: (i, 0))
    return pl.pallas_call(body, grid=(S // tm,), in_specs=[row, row],
                          out_specs=(row, pl.BlockSpec((8, LANES), lambda i: (0, 0))),
                          out_shape=(jax.ShapeDtypeStruct((S, D), F32), jax.ShapeDtypeStruct((8, LANES), F32)),
                          compiler_params=_params(("arbitrary",)), name=name)(y, tgt)


def _to_sm(a, axis=-1):
    axis = axis % a.ndim
    shp = a.shape[:axis] + (N_CHIPS, a.shape[axis] // N_CHIPS) + a.shape[axis + 1:]
    return jnp.moveaxis(a.reshape(shp), axis, 0)


def _from_sm(a, axis=-1):
    nd = a.ndim - 1
    axis = axis % nd
    b = jnp.moveaxis(a, 0, axis)
    return b.reshape(b.shape[:axis] + (b.shape[axis] * b.shape[axis + 1],) + b.shape[axis + 2:])


class GradBuffers:
    def __init__(self, W):
        groups = {}
        for n in BIG:
            _, layers, rows, cols = W[n].shape
            groups.setdefault(cols, []).append((rows, n, layers))
        self.where, self.cols_of, self.buf, self.members = {}, {}, {}, {}
        for cols, items in groups.items():
            off, members = 0, []
            for rows, n, layers in sorted(items, key=lambda t: -t[0]):
                assert off % rows == 0
                self.where[n], self.cols_of[n] = (off, rows), cols
                members.append((n, off, rows * layers))
                off += rows * layers
            assert off % 32 == 0
            self.buf[cols] = lax.empty((N_CHIPS, off, cols), F32)
            self.members[cols] = members

    def put(self, n, layer, fn, **kw):
        cols = self.cols_of[n]
        off, rows = self.where[n]
        self.buf[cols] = fn(into=self.buf[cols], off=off + layer * rows, **kw)


def forward_backward(x, p4, tgt, W):
    S, D = x.shape
    scale = 1.0 / math.sqrt(HEAD_DIM)
    saved = []
    kh = vh = xb_kv = None
    xb = x.astype(BF16)
    for i in range(DEPTH):
        sv = {'xb': xb}
        if i < N_A:
            h_sm = mm_colsm(f"pw1_{i}", xb, W['a_pw1_w'], i, W['a_pw1_b'])
            u = glu_fwd(f"glu_{i}", h_sm)
            c, s = conv_ln_silu_fwd(f"convln_{i}", u, W['a_dw_w'], W['a_dw_b'], W['a_ln_g'], W['a_ln_b'], i)
            mix = mm_rowsm(f"pw2_{i}", s, W['a_pw2_w'], i, bias=W['a_pw2_b'])
            sv.update(h_sm=h_sm, u=u, c=c, s=s)
        else:
            j = i - N_A
            if kh is None:
                xb_kv = xb
                kh = mm_rowsm("wk", xb, W['kv_wk'], 0, out_dtype=BF16)
                vh = mm_rowsm("wv", xb, W['kv_wv'], 0, out_dtype=BF16)
            qh = mm_rowsm(f"wq_{j}", xb, W['b_wq'], j, out_dtype=BF16, out_scale=scale)
            o, runs = attn_fwd(f"attn_{j}", qh, kh, vh)
            mix = mm_rowsm(f"wo_{j}", o, W['b_wo'], j)
            sv.update(qh=qh, o=o, runs=runs)
        x1, x1b, xh1, rs1 = ln_fwd(f"lnmix_{i}", x, mix, W['ln_mix_g'], W['ln_mix_b'], i)
        up_sm = mm_colsm(f"up_{i}", x1b, W['ffn_w_up'], i)
        gp_sm = mm_colsm(f"gate_{i}", x1b, W['ffn_w_gate'], i)
        hf_sm = ffn_gate_fwd(f"ffngate_{i}", up_sm, gp_sm, W['ffn_conv_w'], W['ffn_conv_b'], i)
        ffn = mm_rowsm(f"down_{i}", hf_sm, W['ffn_w_down'], i, a_sm=True)
        tg = mm_rowsm(f"plegate_{i}", x1b, W['ple_w_gate'], i)
        pp = mm_proj(f"pleproj_{i}", p4, i, W['ple_w_proj'])
        x2, x2b, xh2, rs2 = ln_fwd(f"lnffn_{i}", x1, ffn, W['ln_ffn_g'], W['ln_ffn_b'], i, tg=tg, pp=pp)
        sv.update(x1b=x1b, xh1=xh1, rs1=rs1, up_sm=up_sm, gp_sm=gp_sm, hf_sm=hf_sm, tg=tg, pp=pp, xh2=xh2, rs2=rs2)
        saved.append(sv)
        x, xb = x2, x2b

    dx, lacc = loss_head("loss", x, tgt)
    loss_sum = lacc[0, 0]

    G = {n: [None] * DEPTH for n in WEIGHTS if n not in BIG}
    gb = GradBuffers(W)
    dk = dv = None
    for i in reversed(range(DEPTH)):
        sv = saved[i]
        dr, acc, dtg, dpp = ln_bwd(f"lnffn_b_{i}", dx, sv['xh2'], sv['rs2'], W['ln_ffn_g'], i, tg=sv['tg'], pp=sv['pp'])
        G['ln_ffn_g'][i], G['ln_ffn_b'][i] = acc[0], acc[1]
        gb.put('ple_w_proj', i, functools.partial(mm_tn_proj, f"dproj_{i}", p4, i, dpp))
        gb.put('ple_w_gate', i, functools.partial(mm_tn_rowsm, f"dplegate_{i}", sv['x1b'], dtg))
        gb.put('ffn_w_down', i, functools.partial(mm_tn_rowsm, f"ddown_{i}", sv['hf_sm'], dr, a_sm=True))
        dhf_sm = mm_nt_rowsm(f"dhf_{i}", dr, W['ffn_w_down'], i, out_sm=True)
        dup_sm, dgc_sm = ffn_gate_bwd_a(f"ffngate_ba_{i}", dhf_sm, sv['up_sm'], sv['gp_sm'], W['ffn_conv_w'], W['ffn_conv_b'], i)
        dgp_sm, cacc = ffn_gate_bwd_b(f"ffngate_bb_{i}", dgc_sm, sv['gp_sm'], W['ffn_conv_w'], i)
        kw = W['ffn_conv_w'].shape[2]
        G['ffn_conv_w'][i] = cacc[:, 0:kw, :]
        G['ffn_conv_b'][i] = cacc[:, 7, :].reshape(-1)
        gb.put('ffn_w_up', i, functools.partial(mm_tn_colsm, f"dup_{i}", sv['x1b'], dup_sm))
        gb.put('ffn_w_gate', i, functools.partial(mm_tn_colsm, f"dgate_{i}", sv['x1b'], dgp_sm))
        dx1 = mm_nt_rowsm(f"dx1a_{i}", dtg, W['ple_w_gate'], i, res=dr, res_alpha=DN_ALPHA)
        dx1 = mm_nt_colsm(f"dx1b_{i}", _sm_parts(dup_sm), W['ffn_w_up'], i, res=dx1)
        dx1 = mm_nt_colsm(f"dx1c_{i}", _sm_parts(dgp_sm), W['ffn_w_gate'], i, res=dx1)

        dr1, acc1 = ln_bwd(f"lnmix_b_{i}", dx1, sv['xh1'], sv['rs1'], W['ln_mix_g'], i)
        G['ln_mix_g'][i], G['ln_mix_b'][i] = acc1[0], acc1[1]
        xin = sv['xb']
        if i < N_A:
            G['a_pw2_b'][i] = acc1[2]
            gb.put('a_pw2_w', i, functools.partial(mm_tn_rowsm, f"dpw2_{i}", sv['s'], dr1))
            ds = mm_nt_rowsm(f"ds_{i}", dr1, W['a_pw2_w'], i)
            dc, cacc = ln_silu_bwd(f"lnsilu_b_{i}", ds, sv['c'], W['a_ln_g'], W['a_ln_b'], i)
            G['a_ln_g'][i], G['a_ln_b'][i], G['a_dw_b'][i] = cacc[0], cacc[1], cacc[2]
            da, dg, dw, dba, dbg = conv_glu_bwd(f"convglu_b_{i}", dc, sv['u'], sv['h_sm'], W['a_dw_w'], i)
            kw = W['a_dw_w'].shape[1]
            G['a_dw_w'][i] = _from_sm(dw[:, 0:kw, :], axis=-1)
            G['a_pw1_b'][i] = jnp.concatenate([dba[:, 0, :], dbg[:, 0, :]], axis=0)
            half = da.shape[0]
            gb.put('a_pw1_w', i, functools.partial(mm_tn_colsm, f"dpw1a_{i}", xin, da))
            gb.put('a_pw1_w', i, functools.partial(mm_tn_colsm, f"dpw1g_{i}", xin, dg), j0=half)
            dx = mm_nt_colsm(f"dxa_{i}", _sm_parts(da) + _sm_parts(dg), W['a_pw1_w'], i, res=dr1, res_alpha=DN_ALPHA)
        else:
            j = i - N_A
            gb.put('b_wo', j, functools.partial(mm_tn_rowsm, f"dwo_{j}", sv['o'], dr1))
            do = mm_nt_rowsm(f"do_{j}", dr1, W['b_wo'], j)
            dq, dk, dv = attn_bwd(f"attn_b_{j}", sv['qh'], kh, vh, do, sv['runs'], dk, dv)
            gb.put('b_wq', j, functools.partial(mm_tn_rowsm, f"dwq_{j}", xin, dq))
            dx = mm_nt_rowsm(f"dxq_{j}", dq, W['b_wq'], j, res=dr1, res_alpha=DN_ALPHA)
            if j == 0:
                gb.put('kv_wk', 0, functools.partial(mm_tn_rowsm, "dwk", xb_kv, dk))
                gb.put('kv_wv', 0, functools.partial(mm_tn_rowsm, "dwv", xb_kv, dv))
                dx = mm_nt_rowsm("dxk", dk, W['kv_wk'], 0, res=dx)
                dx = mm_nt_rowsm("dxv", dv, W['kv_wv'], 0, res=dx)
    return loss_sum, dx, G, gb


MESH = pl.DeviceIdType.MESH
HBM = pl.BlockSpec(memory_space=pltpu.HBM)


def _place():
    x, y, c = lax.axis_index("x"), lax.axis_index("y"), lax.axis_index("c")
    others = [(1 - x, y), (x, 1 - y), (1 - x, 1 - y)]
    return x, y, c, others


def allgather_chips(name, arrs):
    n = len(arrs)

    def body(*refs):
        ins, outs = refs[:n], refs[n:2 * n]
        send_sems, recv_sems = refs[2 * n:]
        x, y, c, others = _place()
        me = 2 * x + y
        sibling = (x, y, 1 - c)
        ids = [2 * ch[0] + ch[1] for ch in others]
        from_id = jnp.where(c == 0, ids[0], ids[1])
        to_chip = (jnp.where(c == 0, x, 1 - x), jnp.where(c == 0, 1 - y, y))

        def remote(a, k, src, chip_id, half, to):
            return pltpu.make_async_remote_copy(src_ref=src, dst_ref=outs[a].at[chip_id, half], send_sem=send_sems.at[a, k],
                                                recv_sem=recv_sems.at[a, k], device_id=to, device_id_type=MESH)

        sent = [remote(a, k, ins[a].at[c], me, c, (others[k][0], others[k][1], c)) for a in range(n) for k in range(2)]
        for cp in sent:
            cp.start()
        for a in range(n):
            for k in range(2):
                remote(a, k, ins[a].at[c], ids[k], c, sibling).wait_recv()
            sent.append(remote(a, 2, outs[a].at[from_id, c], from_id, c, (to_chip[0], to_chip[1], c)))
            sent[-1].start()
            for k in range(2):
                sent.append(remote(a, 3 + k, outs[a].at[ids[k], c], ids[k], c, sibling))
                sent[-1].start()
        for a in range(n):
            remote(a, 2, ins[a].at[c], ids[2], c, sibling).wait_recv()
            sent.append(remote(a, 5, outs[a].at[ids[2], c], ids[2], c, sibling))
            sent[-1].start()
        for a in range(n):
            for k in range(3):
                remote(a, 3 + k, ins[a].at[c], ids[k], 1 - c, sibling).wait_recv()
        for cp in sent:
            cp.wait_send()

    outs = pl.pallas_call(body, out_shape=tuple(jax.ShapeDtypeStruct((N_CHIPS,) + a.shape, a.dtype) for a in arrs),
                          in_specs=[HBM] * n, out_specs=tuple([HBM] * n),
                          scratch_shapes=[pltpu.SemaphoreType.DMA((n, 6)), pltpu.SemaphoreType.DMA((n, 6))],
                          name=name)(*arrs)
    me = 2 * lax.axis_index("x") + lax.axis_index("y")
    return [lax.dynamic_update_index_in_dim(o, a, me, 0) for o, a in zip(outs, arrs)]


def exchange_sibling(name, gs):
    n = len(gs)

    def body(*refs):
        g_refs, o_refs = refs[:n], refs[n:2 * n]
        send_sems, recv_sems = refs[2 * n:]
        x, y, c, _ = _place()
        cps = [pltpu.make_async_remote_copy(src_ref=g_refs[a].at[j, 1 - c], dst_ref=o_refs[a].at[j], send_sem=send_sems.at[a, j],
                                            recv_sem=recv_sems.at[a, j], device_id=(x, y, 1 - c), device_id_type=MESH)
               for a in range(n) for j in range(N_CHIPS)]
        for cp in cps:
            cp.start()
        for cp in cps:
            cp.wait()

    return pl.pallas_call(body, out_shape=tuple(jax.ShapeDtypeStruct((N_CHIPS,) + g.shape[2:], g.dtype) for g in gs),
                          in_specs=[HBM] * n, out_specs=tuple([HBM] * n),
                          scratch_shapes=[pltpu.SemaphoreType.DMA((n, N_CHIPS)), pltpu.SemaphoreType.DMA((n, N_CHIPS))],
                          name=name)(*gs)


def scatter_chips(name, ss):
    n = len(ss)

    def body(*refs):
        s_refs, o_refs = refs[:n], refs[n:2 * n]
        send_sems, recv_sems = refs[2 * n:]
        x, y, c, others = _place()
        cps = [pltpu.make_async_remote_copy(src_ref=s_refs[a].at[2 * ch[0] + ch[1]], dst_ref=o_refs[a].at[k],
                                            send_sem=send_sems.at[a, k], recv_sem=recv_sems.at[a, k],
                                            device_id=(ch[0], ch[1], c), device_id_type=MESH)
               for a in range(n) for k, ch in enumerate(others)]
        for cp in cps:
            cp.start()
        for cp in cps:
            cp.wait()

    return pl.pallas_call(body, out_shape=tuple(jax.ShapeDtypeStruct((3,) + s.shape[1:], s.dtype) for s in ss),
                          in_specs=[HBM] * n, out_specs=tuple([HBM] * n),
                          scratch_shapes=[pltpu.SemaphoreType.DMA((n, 3)), pltpu.SemaphoreType.DMA((n, 3))], name=name)(*ss)


def share_sibling(name, ts):
    n = len(ts)

    def body(*refs):
        o_refs = refs[n:2 * n]
        send_sems, recv_sems = refs[2 * n:]
        x, y, c, _ = _place()
        cps = [pltpu.make_async_remote_copy(src_ref=o_refs[a].at[c], dst_ref=o_refs[a].at[c], send_sem=send_sems.at[a],
                                            recv_sem=recv_sems.at[a], device_id=(x, y, 1 - c), device_id_type=MESH)
               for a in range(n)]
        for cp in cps:
            cp.start()
        for a in range(n):
            pltpu.make_async_remote_copy(src_ref=o_refs[a].at[c], dst_ref=o_refs[a].at[1 - c], send_sem=send_sems.at[a],
                                         recv_sem=recv_sems.at[a], device_id=(x, y, 1 - c), device_id_type=MESH).wait_recv()
        for cp in cps:
            cp.wait_send()

    return pl.pallas_call(body, out_shape=tuple(jax.ShapeDtypeStruct(t.shape, t.dtype) for t in ts),
                          in_specs=[HBM] * n, out_specs=tuple([HBM] * n), input_output_aliases={a: a for a in range(n)},
                          scratch_shapes=[pltpu.SemaphoreType.DMA((n,)), pltpu.SemaphoreType.DMA((n,))],
                          name=name)(*ts)


def add_halves(name, g, recv, place, out_dtype, tr=512):
    _, _, R, C = g.shape
    tr = _tile(R, tr, 16)

    def body(p_ref, a_ref, b_ref, o_ref):
        o_ref[...] = (a_ref[...] + b_ref[...]).astype(o_ref.dtype)

    blk = pl.BlockSpec((None, tr, C), lambda j, i, p: (j, i, 0))
    gs = pltpu.PrefetchScalarGridSpec(num_scalar_prefetch=1, grid=(N_CHIPS, R // tr),
                                      in_specs=[pl.BlockSpec((None, None, tr, C), lambda j, i, p: (j, p[0], i, 0)), blk],
                                      out_specs=blk)
    return pl.pallas_call(body, grid_spec=gs, out_shape=jax.ShapeDtypeStruct((N_CHIPS, R, C), out_dtype),
                          compiler_params=_params(("parallel", "parallel")), name=name)(place, g, recv)


def add_chips(name, g, r1, r2, place, tr=512):
    _, _, R, C = g.shape
    tr = _tile(R, tr, 16)

    def body(p_ref, a_ref, b_ref, c_ref, o_ref):
        s = a_ref[...] + b_ref[...]
        o_ref[...] = ((s + c_ref[0].astype(F32)) + c_ref[1].astype(F32)) + c_ref[2].astype(F32)

    gs = pltpu.PrefetchScalarGridSpec(num_scalar_prefetch=1, grid=(R // tr,),
                                      in_specs=[pl.BlockSpec((None, None, tr, C), lambda i, p: (p[1], p[0], i, 0)),
                                                pl.BlockSpec((None, tr, C), lambda i, p: (p[1], i, 0)),
                                                pl.BlockSpec((3, tr, C), lambda i, p: (0, i, 0))],
                                      out_specs=pl.BlockSpec((None, tr, C), lambda i, p: (p[0], i, 0)))
    return pl.pallas_call(body, grid_spec=gs, out_shape=jax.ShapeDtypeStruct((2, R, C), F32),
                          compiler_params=_params(("parallel",)), name=name)(place, g, r1, r2)


def reduce_scatter(gs, wire_dtypes, place):
    r1 = exchange_sibling("rs_sibling", gs)
    s1 = [add_halves(f"rs_add_cores_{a}", g, r, place, dt) for a, (g, r, dt) in enumerate(zip(gs, r1, wire_dtypes))]
    r2 = scatter_chips("rs_chips", s1)
    tot = [add_chips(f"rs_add_chips_{a}", g, r, rr, place) for a, (g, r, rr) in enumerate(zip(gs, r1, r2))]
    return share_sibling("rs_share", tot)


def adamw(name, w, g, m, v, tr=512):
    shp = w.shape
    cols = shp[-1]
    w2, g2, m2, v2 = (a.reshape(-1, cols) for a in (w, g, m, v))
    rows = w2.shape[0]
    tr = _tile(rows, tr)

    def body(w_ref, g_ref, m_ref, v_ref, d_ref, mo_ref, vo_ref):
        g_ = g_ref[...]
        m_ = ADAM_B1 * m_ref[...] + (1.0 - ADAM_B1) * g_
        v_ = ADAM_B2 * v_ref[...] + (1.0 - ADAM_B2) * (g_ * g_)
        m_hat = m_ / (1.0 - ADAM_B1 ** ADAM_STEP)
        v_hat = v_ / (1.0 - ADAM_B2 ** ADAM_STEP)
        d_ref[...] = -ADAM_LR * (m_hat / (jnp.sqrt(v_hat) + ADAM_EPS) + ADAM_WD * w_ref[...])
        mo_ref[...] = m_
        vo_ref[...] = v_

    blk = pl.BlockSpec((tr, cols), lambda i: (i, 0))
    sds = jax.ShapeDtypeStruct((rows, cols), F32)
    d, mo, vo = pl.pallas_call(body, grid=(rows // tr,), in_specs=[blk] * 4, out_specs=(blk, blk, blk), out_shape=(sds, sds, sds),
                               compiler_params=_params(("parallel",)), name=name)(w2, g2, m2, v2)
    return d.reshape(shp), mo.reshape(shp), vo.reshape(shp)


PACK_ALIGN = 1024


def _pad_to(a, mult, axis=-1):
    axis = axis % a.ndim
    extra = (-a.shape[axis]) % mult
    if extra == 0:
        return a
    pads = [(0, 0)] * a.ndim
    pads[axis] = (0, extra)
    return jnp.pad(a, pads)


def _pack(pieces, lead, row_mult):
    nl = len(lead)
    flat, offs, sizes, off = [], [], [], 0
    for a in pieces:
        f = a.reshape(lead + (-1,))
        sizes.append(f.shape[-1])
        f = _pad_to(f, PACK_ALIGN)
        offs.append(off)
        off += f.shape[-1]
        flat.append(f)
    cat = _pad_to(jnp.concatenate(flat, axis=nl), 2 * row_mult * LANES)
    return cat.reshape(lead + (2, -1, LANES)), offs, sizes


def _unpack(packed, lead, offs, sizes, shapes):
    flat = packed.reshape(lead + (-1,))
    return [lax.slice_in_dim(flat, o, o + s, axis=len(lead)).reshape(lead + tuple(shp)) for o, s, shp in zip(offs, sizes, shapes)]


def _stack_grads(G, names):
    out = {}
    for n in names:
        parts = [g for g in G[n] if g is not None]
        if n in ('kv_wk', 'kv_wv'):
            out[n] = parts[0]
        elif n in REPLICATED:
            out[n] = jnp.stack(parts, axis=0).reshape(N_CHIPS, -1)
        elif n in ('a_dw_w', 'a_dw_b', 'a_ln_g', 'a_ln_b', 'a_pw2_b'):
            out[n] = _to_sm(jnp.stack(parts, axis=0), axis=-1)
        else:
            out[n] = jnp.stack(parts, axis=1)
    return out


def _whole_weights(big, small, rep, D):
    W = {}
    for n in BIG:
        a = big[n]
        W[n] = a[:, None] if n in ('kv_wk', 'kv_wv') else a
    W['a_pw1_b'] = small['a_pw1_b'][:, :, None, :]
    W['a_dw_w'] = _from_sm(small['a_dw_w'], axis=-1)
    for n in ('a_dw_b', 'a_ln_g', 'a_ln_b', 'a_pw2_b'):
        W[n] = _from_sm(small[n], axis=-1)[:, None, :]
    W['ffn_conv_w'] = small['ffn_conv_w']
    L, F = rep['ffn_conv_b'].shape
    W['ffn_conv_b'] = rep['ffn_conv_b'].reshape(L, N_CHIPS, 1, F // N_CHIPS)
    for n in ('ln_mix_g', 'ln_mix_b', 'ln_ffn_g', 'ln_ffn_b'):
        W[n] = rep[n][:, None, :]
    return W


SMALL = ('a_pw1_b', 'a_dw_w', 'a_dw_b', 'a_ln_g', 'a_ln_b', 'a_pw2_b', 'ffn_conv_w')


def _step(x, p, loss_target, w, m, v):
    S, D = x.shape[-2:]
    x2, tgt = x.reshape(S, D), loss_target.reshape(S, D)
    place = jnp.stack([lax.axis_index("c"), 2 * lax.axis_index("x") + lax.axis_index("y")]).astype(jnp.int32)

    big_in = [w[n].astype(BF16).reshape((2, -1) + w[n].shape[1:] if w[n].ndim == 3 else (2, -1, w[n].shape[-1])) for n in BIG]
    small_in, s_offs, s_sizes = _pack([w[n] for n in SMALL], (), 8)
    gathered = allgather_chips("gather_weights", big_in + [small_in])
    big = {n: g.reshape((N_CHIPS,) + w[n].shape) for n, g in zip(BIG, gathered[:-1])}
    small = dict(zip(SMALL, _unpack(gathered[-1], (N_CHIPS,), s_offs, s_sizes, [w[n].shape for n in SMALL])))
    W = _whole_weights(big, small, {n: w[n] for n in REPLICATED}, D)

    loss_sum, dx, G, gb = forward_backward(x2, p, tgt, W)
    loss = lax.psum(0.5 * loss_sum / D, ("x", "y", "c"))

    vectors = [n for n in WEIGHTS if n not in BIG]
    mats = [b.reshape(N_CHIPS, 2, b.shape[1] // 2, b.shape[2]) for b in gb.buf.values()]
    members = [gb.members[cols] for cols in gb.buf]
    g_sm = _stack_grads(G, vectors)
    packed, offs, sizes = _pack([g_sm[n] for n in vectors], (N_CHIPS,), 512)
    reduced = reduce_scatter(mats + [packed], [BF16] * len(mats) + [F32], place)
    shapes = [w[n].shape if n not in REPLICATED else (w[n].size // N_CHIPS,) for n in vectors]
    g_mine = dict(zip(vectors, _unpack(reduced[-1], (), offs, sizes, shapes)))
    for red, where in zip(reduced[:-1], members):
        rows = red.reshape(-1, red.shape[-1])
        for n, off, cnt in where:
            g_mine[n] = lax.slice_in_dim(rows, off, off + cnt, axis=0).reshape(w[n].shape)
    rep_in, r_offs, r_sizes = _pack([g_mine[n] for n in REPLICATED], (), 8)
    rep_all = allgather_chips("gather_replicated_grads", [rep_in])[0]
    for n, g in zip(REPLICATED, _unpack(rep_all, (N_CHIPS,), r_offs, r_sizes, [(w[n].size // N_CHIPS,) for n in REPLICATED])):
        g_mine[n] = g.reshape(w[n].shape)

    grads, deltas, new_m, new_v = [], [], [], []
    for n in WEIGHTS:
        d, mo, vo = adamw(f"adamw_{n}", w[n], g_mine[n], m[n], v[n])
        grads.append(g_mine[n])
        deltas.append(d)
        new_m.append(mo)
        new_v.append(vo)
    return (loss, dx.reshape(x.shape), *grads, *deltas, *new_m, *new_v)


def kernel(x, p, a_pw1_w, a_pw1_b, a_dw_w, a_dw_b, a_ln_g, a_ln_b, a_pw2_w, a_pw2_b, b_wq, kv_wk, kv_wv, b_wo, ln_mix_g, ln_mix_b, ffn_w_up, ffn_w_gate, ffn_conv_w, ffn_conv_b, ffn_w_down, ple_w_gate, ple_w_proj, ln_ffn_g, ln_ffn_b, loss_target, m_a_pw1_w, m_a_pw1_b, m_a_dw_w, m_a_dw_b, m_a_ln_g, m_a_ln_b, m_a_pw2_w, m_a_pw2_b, m_b_wq, m_kv_wk, m_kv_wv, m_b_wo, m_ln_mix_g, m_ln_mix_b, m_ffn_w_up, m_ffn_w_gate, m_ffn_conv_w, m_ffn_conv_b, m_ffn_w_down, m_ple_w_gate, m_ple_w_proj, m_ln_ffn_g, m_ln_ffn_b, v_a_pw1_w, v_a_pw1_b, v_a_dw_w, v_a_dw_b, v_a_ln_g, v_a_ln_b, v_a_pw2_w, v_a_pw2_b, v_b_wq, v_kv_wk, v_kv_wv, v_b_wo, v_ln_mix_g, v_ln_mix_b, v_ffn_w_up, v_ffn_w_gate, v_ffn_conv_w, v_ffn_conv_b, v_ffn_w_down, v_ple_w_gate, v_ple_w_proj, v_ln_ffn_g, v_ln_ffn_b):
    vals = dict(locals())
    w = {n: vals[n] for n in WEIGHTS}
    m = {n: vals["m_" + n] for n in WEIGHTS}
    v = {n: vals["v_" + n] for n in WEIGHTS}
    return _step(x, p, loss_target, w, m, v)
```

```python
import functools
import math

import jax
import jax.numpy as jnp
import numpy as np
from jax import lax
from jax.experimental import pallas as pl
from jax.experimental.pallas import tpu as pltpu

F32, BF16 = jnp.float32, jnp.bfloat16

HEAD_DIM = 64
LN_EPS = 1e-5
DEPTH = 4
N_A = DEPTH // 2
DN_ALPHA = (2.0 * DEPTH) ** 0.25
N_CHIPS = 4

ADAM_LR, ADAM_B1, ADAM_B2, ADAM_EPS, ADAM_WD, ADAM_STEP = 0.001, 0.9, 0.999, 1e-08, 0.01, 10

VMEM_LIMIT_BYTES = 56 * 2**20
LANES = 128
CONV_HALO = 32
FFN_HALO = 16

NN = (((1,), (0,)), ((), ()))
NT = (((1,), (1,)), ((), ()))
TN = (((0,), (0,)), ((), ()))

WEIGHTS = ['a_pw1_w', 'a_pw1_b', 'a_dw_w', 'a_dw_b', 'a_ln_g', 'a_ln_b', 'a_pw2_w', 'a_pw2_b', 'b_wq', 'kv_wk', 'kv_wv',
           'b_wo', 'ln_mix_g', 'ln_mix_b', 'ffn_w_up', 'ffn_w_gate', 'ffn_conv_w', 'ffn_conv_b', 'ffn_w_down', 'ple_w_gate',
           'ple_w_proj', 'ln_ffn_g', 'ln_ffn_b']
REPLICATED = ('ln_mix_g', 'ln_mix_b', 'ffn_conv_b', 'ln_ffn_g', 'ln_ffn_b')
BIG = ('a_pw1_w', 'a_pw2_w', 'b_wq', 'kv_wk', 'kv_wv', 'b_wo', 'ffn_w_up', 'ffn_w_gate', 'ffn_w_down', 'ple_w_gate',
       'ple_w_proj')


def _tile(n, pref, mult=8):
    t = min(n, pref)
    while t > 0:
        if n % t == 0 and t % mult == 0:
            return t
        t -= 1
    return n


def _params(sem):
    return pltpu.CompilerParams(dimension_semantics=sem, vmem_limit_bytes=VMEM_LIMIT_BYTES)


def _sigmoid(x):
    return 1.0 / (1.0 + jnp.exp(-x))


def _mm_call(name, grid, terms, out_spec, out_sds, dims, bias=None, res=None, res_alpha=1.0, out_scale=None, into=None):
    n_terms = len(terms)

    def body(*refs):
        o_ref = refs[-1]
        acc = None
        for t in range(n_terms):
            a = refs[2 * t][...].astype(BF16)
            b = refs[2 * t + 1][...].astype(BF16)
            d = lax.dot_general(a, b, dims, preferred_element_type=F32)
            acc = d if acc is None else acc + d
        k = 2 * n_terms
        if bias is not None:
            acc = acc + refs[k][...]
            k += 1
        if res is not None:
            acc = acc + res_alpha * refs[k][...]
        if out_scale is not None:
            acc = acc * out_scale
        o_ref[...] = acc.astype(o_ref.dtype)

    operands, specs = [], []
    for a, a_spec, b, b_spec in terms:
        operands += [a, b]
        specs += [a_spec, b_spec]
    for extra in (bias, res):
        if extra is not None:
            operands.append(extra[0])
            specs.append(extra[1])
    aliases = {}
    if into is not None:
        aliases = {len(operands): 0}
        operands.append(into)
        specs.append(pl.BlockSpec(memory_space=pl.ANY))
        out_sds = jax.ShapeDtypeStruct(into.shape, into.dtype)
    return pl.pallas_call(body, out_shape=out_sds, grid=grid, in_specs=specs, out_specs=out_spec, input_output_aliases=aliases,
                          compiler_params=_params(("parallel",) * len(grid)), name=name)(*operands)


def _mm_fanout(name, a, a_spec, w4, w_block, layer, dims, out_spec, out_sds, store, grid, bias4=None, res=None, res_alpha=1.0):
    def body(*refs):
        a_ref, w_refs, o_ref = refs[0], refs[1:1 + N_CHIPS], refs[-1]
        k = 1 + N_CHIPS
        b_refs = refs[k:k + N_CHIPS] if bias4 is not None else None
        k += N_CHIPS if bias4 is not None else 0
        av = a_ref[...].astype(BF16)
        for j in range(N_CHIPS):
            d = lax.dot_general(av, w_refs[j][...].astype(BF16), dims, preferred_element_type=F32)
            if b_refs is not None:
                d = d + b_refs[j][...]
            if res is not None:
                d = d + res_alpha * res[2](refs[k], j)
            store(o_ref, j, d)

    nd = len(grid)
    operands = [a] + [w4] * N_CHIPS
    specs = [a_spec] + [pl.BlockSpec((None, None) + w_block, lambda *g, j=j: (j, layer, 0, 0)) for j in range(N_CHIPS)]
    if bias4 is not None:
        operands += [bias4] * N_CHIPS
        specs += [pl.BlockSpec((None, None, 1, bias4.shape[-1]), lambda *g, j=j: (j, layer, 0, 0)) for j in range(N_CHIPS)]
    if res is not None:
        operands.append(res[0])
        specs.append(res[1])
    return pl.pallas_call(body, out_shape=out_sds, grid=grid, in_specs=specs, out_specs=out_spec,
                          compiler_params=_params(("parallel",) * nd), name=name)(*operands)


def _store_slot(o_ref, j, d):
    o_ref[j] = d.astype(o_ref.dtype)


def mm_colsm(name, x, w4, layer, bias4=None, out_dtype=F32, tm=512):
    M, K = x.shape
    n = w4.shape[-1]
    tm = _tile(M, tm)
    return _mm_fanout(name, x, pl.BlockSpec((tm, K), lambda i: (i, 0)), w4, (K, n), layer, NN,
                      pl.BlockSpec((N_CHIPS, tm, n), lambda i: (0, i, 0)), jax.ShapeDtypeStruct((N_CHIPS, M, n), out_dtype),
                      _store_slot, (M // tm,), bias4=bias4)


def mm_rowsm(name, a, w4, layer, a_sm=False, bias=None, out_dtype=F32, out_scale=None, tm=512):
    kc, N = w4.shape[-2:]
    M = a.shape[-2]
    tm = _tile(M, tm)
    terms = []
    for j in range(N_CHIPS):
        if a_sm:
            a_spec = pl.BlockSpec((None, tm, kc), lambda i, j=j: (j, i, 0))
        else:
            a_spec = pl.BlockSpec((tm, kc), lambda i, j=j: (i, j))
        terms.append((a, a_spec, w4, pl.BlockSpec((None, None, kc, N), lambda i, j=j: (j, layer, 0, 0))))
    b = None if bias is None else (bias, pl.BlockSpec((None, 1, N), lambda i: (layer, 0, 0)))
    return _mm_call(name, (M // tm,), terms, pl.BlockSpec((tm, N), lambda i: (i, 0)), jax.ShapeDtypeStruct((M, N), out_dtype),
                    NN, bias=b, out_scale=out_scale)


def mm_nt_rowsm(name, dy, w4, layer, out_sm=False, res=None, res_alpha=1.0, out_dtype=F32, tm=512):
    kc, N = w4.shape[-2:]
    M = dy.shape[0]
    tm = _tile(M, tm)

    def store_cols(o_ref, j, d):
        o_ref[:, j * kc:(j + 1) * kc] = d

    if out_sm:
        out_spec, sds, store = pl.BlockSpec((N_CHIPS, tm, kc), lambda i: (0, i, 0)), jax.ShapeDtypeStruct((N_CHIPS, M, kc), out_dtype), _store_slot
    else:
        out_spec, sds, store = pl.BlockSpec((tm, N_CHIPS * kc), lambda i: (i, 0)), jax.ShapeDtypeStruct((M, N_CHIPS * kc), F32), store_cols
    r = None if res is None else (res, pl.BlockSpec((tm, N_CHIPS * kc), lambda i: (i, 0)), lambda ref, j: ref[:, j * kc:(j + 1) * kc])
    return _mm_fanout(name, dy, pl.BlockSpec((tm, N), lambda i: (i, 0)), w4, (kc, N), layer, NT, out_spec, sds, store, (M // tm,),
                      res=r, res_alpha=res_alpha)


def mm_nt_colsm(name, dy_parts, w4, layer, res=None, res_alpha=1.0, tm=512):
    K, n = w4.shape[-2:]
    M = dy_parts[0][0].shape[1]
    tm = _tile(M, tm)
    terms = [(arr, pl.BlockSpec((None, tm, n), lambda i, idx=idx: (idx, i, 0)), w4,
              pl.BlockSpec((None, None, K, n), lambda i, j=j: (j, layer, 0, 0))) for j, (arr, idx) in enumerate(dy_parts)]
    r = None if res is None else (res, pl.BlockSpec((tm, K), lambda i: (i, 0)))
    return _mm_call(name, (M // tm,), terms, pl.BlockSpec((tm, K), lambda i: (i, 0)), jax.ShapeDtypeStruct((M, K), F32), NT,
                    res=r, res_alpha=res_alpha)


def _sm_parts(a):
    return [(a, j) for j in range(a.shape[0])]


def mm_tn_colsm(name, x, dy, into, off, j0=0, tk=512):
    M, K = x.shape
    nj, _, n = dy.shape
    tk = _tile(K, tk, LANES)
    assert off % tk == 0
    terms = [(x, pl.BlockSpec((M, tk), lambda j, k: (0, k)), dy, pl.BlockSpec((None, M, n), lambda j, k: (j, 0, 0)))]
    return _mm_call(name, (nj, K // tk), terms, pl.BlockSpec((None, tk, n), lambda j, k: (j + j0, off // tk + k, 0)),
                    None, TN, into=into)


def mm_tn_rowsm(name, a, dy, into, off, a_sm=False, tn=512):
    M, N = dy.shape
    kc = a.shape[-1] if a_sm else a.shape[-1] // N_CHIPS
    tn = _tile(N, tn, LANES)
    assert off % kc == 0
    a_spec = pl.BlockSpec((None, M, kc), lambda j, n: (j, 0, 0)) if a_sm else pl.BlockSpec((M, kc), lambda j, n: (0, j))
    terms = [(a, a_spec, dy, pl.BlockSpec((M, tn), lambda j, n: (0, n)))]
    return _mm_call(name, (N_CHIPS, N // tn), terms, pl.BlockSpec((None, kc, tn), lambda j, n: (j, off // kc, n)),
                    None, TN, into=into)


def mm_tn_rowsm_fan(name, a, dy, into, off, tn=256):
    M, N = dy.shape
    kc = a.shape[-1] // N_CHIPS
    tn = _tile(N, tn, LANES)
    assert off % kc == 0

    def body(a_ref, dy_ref, into_ref, o_ref):
        dyb = dy_ref[...].astype(BF16)
        for j in range(N_CHIPS):
            aj = a_ref[:, j * kc:(j + 1) * kc].astype(BF16)
            o_ref[j] = lax.dot_general(aj, dyb, TN, preferred_element_type=F32)

    return pl.pallas_call(body, out_shape=jax.ShapeDtypeStruct(into.shape, into.dtype), grid=(N // tn,),
                          in_specs=[pl.BlockSpec((M, N_CHIPS * kc), lambda n: (0, 0)), pl.BlockSpec((M, tn), lambda n: (0, n)),
                                    pl.BlockSpec(memory_space=pl.ANY)],
                          out_specs=pl.BlockSpec((N_CHIPS, kc, tn), lambda n: (0, off // kc, n)), input_output_aliases={2: 0},
                          compiler_params=_params(("parallel",)), name=name)(a, dy, into)


def mm_proj(name, p4, layer, w4, tm=512):
    S, P = p4.shape[-2:]
    n = w4.shape[-1]
    tm = _tile(S, tm)

    def store_cols(o_ref, j, d):
        o_ref[:, j * n:(j + 1) * n] = d

    return _mm_fanout(name, p4, pl.BlockSpec((None, None, tm, P), lambda i: (layer, 0, i, 0)), w4, (P, n), layer, NN,
                      pl.BlockSpec((tm, N_CHIPS * n), lambda i: (i, 0)), jax.ShapeDtypeStruct((S, N_CHIPS * n), F32),
                      store_cols, (S // tm,))


def mm_tn_proj(name, p4, layer, dpp, into, off):
    S, P = p4.shape[-2:]
    n = dpp.shape[-1] // N_CHIPS
    assert off % P == 0
    terms = [(p4, pl.BlockSpec((None, None, S, P), lambda j: (layer, 0, 0, 0)), dpp, pl.BlockSpec((S, n), lambda j: (0, j)))]
    return _mm_call(name, (N_CHIPS,), terms, pl.BlockSpec((None, P, n), lambda j: (j, off // P, 0)), None, TN, into=into)


def ln_fwd(name, x, mix, g, b, layer, tg=None, pp=None, tm=256):
    S, D = x.shape
    tm = _tile(S, tm, 16)
    ple = tg is not None

    def body(*refs):
        if ple:
            x_ref, m_ref, tg_ref, pp_ref, g_ref, b_ref, y_ref, yb_ref, xh_ref, rs_ref = refs
        else:
            x_ref, m_ref, g_ref, b_ref, y_ref, yb_ref, xh_ref, rs_ref = refs
        r = DN_ALPHA * x_ref[...] + m_ref[...]
        if ple:
            r = r + _sigmoid(tg_ref[...]) * pp_ref[...]
        mu = jnp.mean(r, axis=-1, keepdims=True)
        d = r - mu
        var = jnp.mean(d * d, axis=-1, keepdims=True)
        rstd = lax.rsqrt(var + LN_EPS)
        xh = d * rstd
        y = xh * g_ref[...] + b_ref[...]
        y_ref[...] = y
        yb_ref[...] = y.astype(BF16)
        xh_ref[...] = xh
        rs_ref[...] = rstd

    row = pl.BlockSpec((tm, D), lambda i: (i, 0))
    vec = pl.BlockSpec((None, 1, D), lambda i: (layer, 0, 0))
    ins = [x, mix] + ([tg, pp] if ple else []) + [g, b]
    specs = [row, row] + ([row, row] if ple else []) + [vec, vec]
    return pl.pallas_call(body, grid=(S // tm,), in_specs=specs,
                          out_specs=(row, row, row, pl.BlockSpec((tm, 1), lambda i: (i, 0))),
                          out_shape=(jax.ShapeDtypeStruct((S, D), F32), jax.ShapeDtypeStruct((S, D), BF16),
                                     jax.ShapeDtypeStruct((S, D), F32), jax.ShapeDtypeStruct((S, 1), F32)),
                          compiler_params=_params(("parallel",)), name=name)(*ins)


def ln_bwd(name, dy, xh, rstd, g, layer, tg=None, pp=None, tm=256):
    S, D = dy.shape
    tm = _tile(S, tm)
    ple = tg is not None

    def body(*refs):
        if ple:
            dy_ref, xh_ref, rs_ref, g_ref, tg_ref, pp_ref, dr_ref, acc_ref, dtg_ref, dpp_ref = refs
        else:
            dy_ref, xh_ref, rs_ref, g_ref, dr_ref, acc_ref = refs
        dy_, xh_ = dy_ref[...], xh_ref[...]
        dxh = dy_ * g_ref[...]
        m1 = jnp.mean(dxh, axis=-1, keepdims=True)
        m2 = jnp.mean(dxh * xh_, axis=-1, keepdims=True)
        dr = rs_ref[...] * (dxh - m1 - xh_ * m2)
        dr_ref[...] = dr

        @pl.when(pl.program_id(0) == 0)
        def _():
            acc_ref[...] = jnp.zeros_like(acc_ref)
        acc_ref[0:1, :] += jnp.sum(dy_ * xh_, axis=0, keepdims=True)
        acc_ref[1:2, :] += jnp.sum(dy_, axis=0, keepdims=True)
        acc_ref[2:3, :] += jnp.sum(dr, axis=0, keepdims=True)
        if ple:
            pg = _sigmoid(tg_ref[...])
            dtg_ref[...] = (dr * pp_ref[...] * pg * (1.0 - pg)).astype(BF16)
            dpp_ref[...] = (dr * pg).astype(BF16)

    row = pl.BlockSpec((tm, D), lambda i: (i, 0))
    ins = [dy, xh, rstd, g] + ([tg, pp] if ple else [])
    specs = [row, row, pl.BlockSpec((tm, 1), lambda i: (i, 0)), pl.BlockSpec((None, 1, D), lambda i: (layer, 0, 0))] + ([row, row] if ple else [])
    outs = [jax.ShapeDtypeStruct((S, D), F32), jax.ShapeDtypeStruct((8, D), F32)]
    out_specs = [row, pl.BlockSpec((8, D), lambda i: (0, 0))]
    if ple:
        outs += [jax.ShapeDtypeStruct((S, D), BF16)] * 2
        out_specs += [row, row]
    return pl.pallas_call(body, grid=(S // tm,), in_specs=specs, out_specs=tuple(out_specs), out_shape=tuple(outs),
                          compiler_params=_params(("arbitrary",)), name=name)(*ins)


def glu_fwd(name, h_sm, tm=512):
    _, S, n = h_sm.shape
    tm = _tile(S, tm)
    half = N_CHIPS // 2

    def body(a_ref, g_ref, u_ref):
        u_ref[...] = a_ref[...] * _sigmoid(g_ref[...])

    return pl.pallas_call(body, grid=(half, S // tm),
                          in_specs=[pl.BlockSpec((None, tm, n), lambda j, i: (j, i, 0)),
                                    pl.BlockSpec((None, tm, n), lambda j, i: (j + half, i, 0))],
                          out_specs=pl.BlockSpec((tm, n), lambda j, i: (i, j)),
                          out_shape=jax.ShapeDtypeStruct((S, half * n), F32),
                          compiler_params=_params(("parallel", "parallel")), name=name)(h_sm, h_sm)


def conv_ln_silu_fwd(name, u, w, b, g, beta, layer, ts=128):
    S, D = u.shape
    kw = w.shape[1]
    ts = _tile(S, ts, CONV_HALO)
    lc = LANES if D % LANES == 0 else D

    def body(h_ref, u_ref, w_ref, b_ref, g_ref, be_ref, c_ref, s_ref, win_ref):
        i = pl.program_id(0)
        win_ref[0:CONV_HALO, :] = jnp.where(i == 0, 0.0, h_ref[...])
        win_ref[CONV_HALO:, :] = u_ref[...]
        for cc in range(D // lc):
            cs = slice(cc * lc, (cc + 1) * lc)
            acc = jnp.zeros((ts, lc), F32) + b_ref[:, cs]
            for k in range(kw):
                off = CONV_HALO - (kw - 1) + k
                acc = acc + w_ref[k:k + 1, cs] * win_ref[off:off + ts, cs]
            c_ref[:, cs] = acc
        c = c_ref[...]
        mu = jnp.mean(c, axis=-1, keepdims=True)
        d = c - mu
        var = jnp.mean(d * d, axis=-1, keepdims=True)
        nrm = d * lax.rsqrt(var + LN_EPS) * g_ref[...] + be_ref[...]
        s_ref[...] = (nrm * _sigmoid(nrm)).astype(BF16)

    row = pl.BlockSpec((ts, D), lambda i: (i, 0))
    vec = pl.BlockSpec((None, 1, D), lambda i: (layer, 0, 0))
    halo = pl.BlockSpec((CONV_HALO, D), lambda i: (jnp.maximum(i * (ts // CONV_HALO) - 1, 0), 0))
    return pl.pallas_call(body, grid=(S // ts,),
                          in_specs=[halo, row, pl.BlockSpec((None, kw, D), lambda i: (layer, 0, 0)), vec, vec, vec],
                          out_specs=(row, row),
                          out_shape=(jax.ShapeDtypeStruct((S, D), F32), jax.ShapeDtypeStruct((S, D), BF16)),
                          scratch_shapes=[pltpu.VMEM((ts + CONV_HALO, D), F32)],
                          compiler_params=_params(("parallel",)), name=name)(u, u, w, b, g, beta)


def ln_silu_bwd(name, ds, c, g, beta, layer, tm=256):
    S, D = c.shape
    tm = _tile(S, tm)

    def body(ds_ref, c_ref, g_ref, be_ref, dc_ref, acc_ref):
        c_ = c_ref[...]
        mu = jnp.mean(c_, axis=-1, keepdims=True)
        d = c_ - mu
        var = jnp.mean(d * d, axis=-1, keepdims=True)
        rstd = lax.rsqrt(var + LN_EPS)
        xh = d * rstd
        nrm = xh * g_ref[...] + be_ref[...]
        sg = _sigmoid(nrm)
        dn = ds_ref[...] * (sg * (1.0 + nrm * (1.0 - sg)))
        dxh = dn * g_ref[...]
        m1 = jnp.mean(dxh, axis=-1, keepdims=True)
        m2 = jnp.mean(dxh * xh, axis=-1, keepdims=True)
        dc = rstd * (dxh - m1 - xh * m2)
        dc_ref[...] = dc

        @pl.when(pl.program_id(0) == 0)
        def _():
            acc_ref[...] = jnp.zeros_like(acc_ref)
        acc_ref[0:1, :] += jnp.sum(dn * xh, axis=0, keepdims=True)
        acc_ref[1:2, :] += jnp.sum(dn, axis=0, keepdims=True)
        acc_ref[2:3, :] += jnp.sum(dc, axis=0, keepdims=True)

    row = pl.BlockSpec((tm, D), lambda i: (i, 0))
    vec = pl.BlockSpec((None, 1, D), lambda i: (layer, 0, 0))
    return pl.pallas_call(body, grid=(S // tm,), in_specs=[row, row, vec, vec],
                          out_specs=(row, pl.BlockSpec((8, D), lambda i: (0, 0))),
                          out_shape=(jax.ShapeDtypeStruct((S, D), F32), jax.ShapeDtypeStruct((8, D), F32)),
                          compiler_params=_params(("arbitrary",)), name=name)(ds, c, g, beta)


def conv_glu_bwd(name, dc, u, h_sm, w, layer, ts=128):
    S, D = dc.shape
    kw = w.shape[1]
    half = N_CHIPS // 2
    n = D // half
    ts = _tile(S, ts, CONV_HALO)
    nblk = S // ts
    lc = LANES if n % LANES == 0 else n

    def body(dc_ref, dcn_ref, u_ref, up_ref, a_ref, g_ref, w_ref, da_ref, dg_ref, dw_ref, dba_ref, dbg_ref, dwin_ref, uwin_ref):
        i = pl.program_id(1)
        dwin_ref[0:ts, :] = dc_ref[...]
        dwin_ref[ts:, :] = jnp.where(i == nblk - 1, 0.0, dcn_ref[...])
        uwin_ref[0:CONV_HALO, :] = jnp.where(i == 0, 0.0, up_ref[...])
        uwin_ref[CONV_HALO:, :] = u_ref[...]

        @pl.when(i == 0)
        def _():
            dw_ref[...] = jnp.zeros_like(dw_ref)
            dba_ref[...] = jnp.zeros_like(dba_ref)
            dbg_ref[...] = jnp.zeros_like(dbg_ref)

        for cc in range(n // lc):
            cs = slice(cc * lc, (cc + 1) * lc)
            dcb = dwin_ref[0:ts, cs]
            du = jnp.zeros((ts, lc), F32)
            for k in range(kw):
                du = du + w_ref[k:k + 1, cs] * dwin_ref[kw - 1 - k:kw - 1 - k + ts, cs]
                off = CONV_HALO - (kw - 1) + k
                dw_ref[k:k + 1, cs] += jnp.sum(dcb * uwin_ref[off:off + ts, cs], axis=0, keepdims=True)
            a = a_ref[:, cs]
            sg = _sigmoid(g_ref[:, cs])
            da = du * sg
            dg = du * a * sg * (1.0 - sg)
            da_ref[:, cs] = da.astype(BF16)
            dg_ref[:, cs] = dg.astype(BF16)
            dba_ref[0:1, cs] += jnp.sum(da, axis=0, keepdims=True)
            dbg_ref[0:1, cs] += jnp.sum(dg, axis=0, keepdims=True)

    r = ts // CONV_HALO
    main = pl.BlockSpec((ts, n), lambda j, i: (i, j))
    nxt = pl.BlockSpec((CONV_HALO, n), lambda j, i: (jnp.minimum((i + 1) * r, S // CONV_HALO - 1), j))
    prv = pl.BlockSpec((CONV_HALO, n), lambda j, i: (jnp.maximum(i * r - 1, 0), j))
    sm_a = pl.BlockSpec((None, ts, n), lambda j, i: (j, i, 0))
    sm_g = pl.BlockSpec((None, ts, n), lambda j, i: (j + half, i, 0))
    acc8a = pl.BlockSpec((None, 8, n), lambda j, i: (j, 0, 0))
    acc8g = pl.BlockSpec((None, 8, n), lambda j, i: (j + half, 0, 0))
    da, dg, dw, dba, dbg = pl.pallas_call(
        body, grid=(half, nblk),
        in_specs=[main, nxt, main, prv, sm_a, sm_g, pl.BlockSpec((None, kw, n), lambda j, i: (layer, 0, j))],
        out_specs=(pl.BlockSpec((None, ts, n), lambda j, i: (j, i, 0)), pl.BlockSpec((None, ts, n), lambda j, i: (j, i, 0)),
                   pl.BlockSpec((None, 32, n), lambda j, i: (j, 0, 0)),
                   pl.BlockSpec((None, 8, n), lambda j, i: (j, 0, 0)), pl.BlockSpec((None, 8, n), lambda j, i: (j, 0, 0))),
        out_shape=(jax.ShapeDtypeStruct((half, S, n), BF16), jax.ShapeDtypeStruct((half, S, n), BF16),
                   jax.ShapeDtypeStruct((half, 32, n), F32),
                   jax.ShapeDtypeStruct((half, 8, n), F32), jax.ShapeDtypeStruct((half, 8, n), F32)),
        scratch_shapes=[pltpu.VMEM((ts + CONV_HALO, n), F32), pltpu.VMEM((ts + CONV_HALO, n), F32)],
        compiler_params=_params(("parallel", "arbitrary")), name=name)(dc, dc, u, u, h_sm, h_sm, w)
    del acc8a, acc8g
    return da, dg, dw, dba, dbg


ROW_CHUNK = 16


def _ffn_gc(win_ref, w_ref, b_ref, r0, rows, kw, base):
    gc = b_ref[...] + jnp.zeros((rows, win_ref.shape[1]), F32)
    for k in range(kw):
        off = r0 + base - (kw - 1) + k
        gc = gc + w_ref[k:k + 1, :] * win_ref[off:off + rows, :]
    return gc


def ffn_gate_fwd(name, up_sm, gp_sm, w4, b4, layer, ts=256):
    _, S, n = up_sm.shape
    kw = w4.shape[2]
    ts = _tile(S, ts, ROW_CHUNK)
    rc = ROW_CHUNK

    def body(up_ref, gp_ref, gph_ref, w_ref, b_ref, hf_ref, win_ref):
        i = pl.program_id(1)
        win_ref[0:FFN_HALO, :] = jnp.where(i == 0, 0.0, gph_ref[...].astype(F32))
        win_ref[FFN_HALO:, :] = gp_ref[...].astype(F32)
        for r0 in range(0, ts, rc):
            gc = _ffn_gc(win_ref, w_ref, b_ref, r0, rc, kw, FFN_HALO)
            hf_ref[r0:r0 + rc, :] = (gc * _sigmoid(gc) * up_ref[r0:r0 + rc, :].astype(F32)).astype(BF16)

    main = pl.BlockSpec((None, ts, n), lambda j, i: (j, i, 0))
    prv = pl.BlockSpec((None, FFN_HALO, n), lambda j, i: (j, jnp.maximum(i * (ts // FFN_HALO) - 1, 0), 0))
    return pl.pallas_call(body, grid=(N_CHIPS, S // ts),
                          in_specs=[main, main, prv, pl.BlockSpec((None, None, kw, n), lambda j, i: (j, layer, 0, 0)),
                                    pl.BlockSpec((None, None, 1, n), lambda j, i: (layer, j, 0, 0))],
                          out_specs=main, out_shape=jax.ShapeDtypeStruct((N_CHIPS, S, n), BF16),
                          scratch_shapes=[pltpu.VMEM((ts + FFN_HALO, n), F32)],
                          compiler_params=_params(("parallel", "parallel")), name=name)(up_sm, gp_sm, gp_sm, w4, b4)


def ffn_gate_bwd_a(name, dhf_sm, up_sm, gp_sm, w4, b4, layer, ts=256):
    _, S, n = up_sm.shape
    kw = w4.shape[2]
    ts = _tile(S, ts, ROW_CHUNK)
    rc = ROW_CHUNK

    def body(dhf_ref, up_ref, gp_ref, gph_ref, w_ref, b_ref, dup_ref, dgc_ref, win_ref):
        i = pl.program_id(1)
        win_ref[0:FFN_HALO, :] = jnp.where(i == 0, 0.0, gph_ref[...].astype(F32))
        win_ref[FFN_HALO:, :] = gp_ref[...].astype(F32)
        for r0 in range(0, ts, rc):
            rows = slice(r0, r0 + rc)
            gc = _ffn_gc(win_ref, w_ref, b_ref, r0, rc, kw, FFN_HALO)
            sg = _sigmoid(gc)
            dhf = dhf_ref[rows, :].astype(F32)
            dup_ref[rows, :] = (dhf * gc * sg).astype(BF16)
            dgc_ref[rows, :] = (dhf * up_ref[rows, :].astype(F32) * (sg * (1.0 + gc * (1.0 - sg)))).astype(BF16)

    main = pl.BlockSpec((None, ts, n), lambda j, i: (j, i, 0))
    prv = pl.BlockSpec((None, FFN_HALO, n), lambda j, i: (j, jnp.maximum(i * (ts // FFN_HALO) - 1, 0), 0))
    return pl.pallas_call(body, grid=(N_CHIPS, S // ts),
                          in_specs=[main, main, main, prv, pl.BlockSpec((None, None, kw, n), lambda j, i: (j, layer, 0, 0)),
                                    pl.BlockSpec((None, None, 1, n), lambda j, i: (layer, j, 0, 0))],
                          out_specs=(main, main),
                          out_shape=(jax.ShapeDtypeStruct((N_CHIPS, S, n), BF16), jax.ShapeDtypeStruct((N_CHIPS, S, n), BF16)),
                          scratch_shapes=[pltpu.VMEM((ts + FFN_HALO, n), F32)],
                          compiler_params=_params(("parallel", "parallel")), name=name)(dhf_sm, up_sm, gp_sm, gp_sm, w4, b4)


def ffn_gate_bwd_b(name, dgc_sm, gp_sm, w4, layer, ts=256):
    _, S, n = gp_sm.shape
    kw = w4.shape[2]
    ts = _tile(S, ts, ROW_CHUNK)
    rc = ROW_CHUNK
    nblk = S // ts

    def body(dgc_ref, dgn_ref, gp_ref, gph_ref, w_ref, dgp_ref, acc_ref, dwin_ref, gwin_ref):
        i = pl.program_id(1)
        dwin_ref[0:ts, :] = dgc_ref[...].astype(F32)
        dwin_ref[ts:, :] = jnp.where(i == nblk - 1, 0.0, dgn_ref[...].astype(F32))
        gwin_ref[0:FFN_HALO, :] = jnp.where(i == 0, 0.0, gph_ref[...].astype(F32))
        gwin_ref[FFN_HALO:, :] = gp_ref[...].astype(F32)

        @pl.when(i == 0)
        def _():
            acc_ref[...] = jnp.zeros_like(acc_ref)
        sums = [jnp.zeros((8, n), F32) for _ in range(kw + 1)]
        for r0 in range(0, ts, rc):
            dgc = dwin_ref[r0:r0 + rc, :]
            dgp = jnp.zeros((rc, n), F32)
            for k in range(kw):
                dgp = dgp + w_ref[k:k + 1, :] * dwin_ref[r0 + kw - 1 - k:r0 + kw - 1 - k + rc, :]
                off = r0 + FFN_HALO - (kw - 1) + k
                prod = dgc * gwin_ref[off:off + rc, :]
                sums[k] = sums[k] + prod[0:8, :] + prod[8:16, :]
            sums[kw] = sums[kw] + dgc[0:8, :] + dgc[8:16, :]
            dgp_ref[r0:r0 + rc, :] = dgp.astype(BF16)
        for k in range(kw):
            acc_ref[k:k + 1, :] += jnp.sum(sums[k], axis=0, keepdims=True)
        acc_ref[7:8, :] += jnp.sum(sums[kw], axis=0, keepdims=True)

    r = ts // FFN_HALO
    main = pl.BlockSpec((None, ts, n), lambda j, i: (j, i, 0))
    nxt = pl.BlockSpec((None, FFN_HALO, n), lambda j, i: (j, jnp.minimum((i + 1) * r, S // FFN_HALO - 1), 0))
    prv = pl.BlockSpec((None, FFN_HALO, n), lambda j, i: (j, jnp.maximum(i * r - 1, 0), 0))
    return pl.pallas_call(body, grid=(N_CHIPS, nblk),
                          in_specs=[main, nxt, main, prv, pl.BlockSpec((None, None, kw, n), lambda j, i: (j, layer, 0, 0))],
                          out_specs=(main, pl.BlockSpec((None, 8, n), lambda j, i: (j, 0, 0))),
                          out_shape=(jax.ShapeDtypeStruct((N_CHIPS, S, n), BF16), jax.ShapeDtypeStruct((N_CHIPS, 8, n), F32)),
                          scratch_shapes=[pltpu.VMEM((ts + FFN_HALO, n), F32), pltpu.VMEM((ts + FFN_HALO, n), F32)],
                          compiler_params=_params(("parallel", "arbitrary")), name=name)(dgc_sm, dgc_sm, gp_sm, gp_sm, w4)


def _neg_softplus(z):
    e = jnp.exp(-jnp.abs(z))
    return -(jnp.maximum(z, 0.0) + jnp.log(1.0 + e)), e


def _split_dot(x, t):
    hi = x.astype(BF16)
    lo = (x - hi.astype(F32)).astype(BF16)
    return jnp.dot(hi, t, preferred_element_type=F32) + jnp.dot(lo, t, preferred_element_type=F32)


STICK_GONE = -100.0
NOT_SWEPT = -1e30


def attn_fwd(name, q, k, v, bq=512, w=256):
    S, D = q.shape
    dh = HEAD_DIM
    hpb = LANES // dh
    bq = _tile(S, bq)
    w = _tile(bq, w)
    nsub = bq // w
    nkb = S // w

    def body(q_ref, k_ref, v_ref, o_ref, runs_ref, rs_ref):
        qi = pl.program_id(1)
        rr = lax.broadcasted_iota(jnp.int32, (w, w), 0)
        cc = lax.broadcasted_iota(jnp.int32, (w, w), 1)
        t_suf = (rr >= cc).astype(BF16)
        tq = qi * bq + lax.broadcasted_iota(jnp.int32, (bq, w), 0)
        tk = lax.broadcasted_iota(jnp.int32, (bq, w), 1)
        lane = lax.broadcasted_iota(jnp.int32, (bq, LANES), 1)
        ntot = (qi + 1) * nsub
        heads = [slice(hh * dh, (hh + 1) * dh) for hh in range(hpb)]
        qbs = [q_ref[:, hs] for hs in heads]
        for hh in range(hpb):
            rs_ref[hh] = jnp.where(lane < ntot, NOT_SWEPT, 0.0)

        def block(kb, carry, masked):
            kstart = pl.multiple_of(kb * w, w)
            if masked:
                m = (tk + kstart) < tq
            out = []
            for hh, hs in enumerate(heads):
                run, acc = carry[2 * hh], carry[2 * hh + 1]
                kblk = k_ref[pl.ds(kstart, w), hs]
                vblk = v_ref[pl.ds(kstart, w), hs]
                z = lax.dot_general(qbs[hh], kblk, NT, preferred_element_type=F32)
                lg, _ = _neg_softplus(z)
                if masked:
                    lg = jnp.where(m, lg, 0.0)
                cum = _split_dot(lg, t_suf) + run
                a = jnp.exp(z + cum)
                if masked:
                    a = jnp.where(m, a, 0.0)
                acc = acc + jnp.dot(a.astype(BF16), vblk, preferred_element_type=F32)
                run = cum[:, 0:1]
                rs_ref[hh] = jnp.where(lane == kb, run, rs_ref[hh])
                out += [run, acc]
            return tuple(out)

        carry = (jnp.zeros((bq, 1), F32), jnp.zeros((bq, dh), F32)) * hpb
        for sb in reversed(range(nsub)):
            carry = block(qi * nsub + sb, carry, True)

        def cond(c):
            alive = functools.reduce(jnp.maximum, [jnp.max(c[1 + 2 * hh]) for hh in range(hpb)])
            return jnp.logical_and(c[0] >= 0, alive > STICK_GONE)

        def step(c):
            return (c[0] - 1,) + block(c[0], c[1:], False)
        carry = lax.while_loop(cond, step, (qi * nsub - 1,) + carry)[1:]
        for hh, hs in enumerate(heads):
            o_ref[:, hs] = carry[2 * hh + 1].astype(o_ref.dtype)
            runs_ref[hh] = rs_ref[hh, :, 0:nkb]

    qs = pl.BlockSpec((bq, LANES), lambda h, i: (i, h))
    kv = pl.BlockSpec((S, LANES), lambda h, i: (0, h))
    return pl.pallas_call(body, grid=(D // LANES, S // bq), in_specs=[qs, kv, kv],
                          out_specs=(qs, pl.BlockSpec((hpb, bq, nkb), lambda h, i: (h, i, 0))),
                          out_shape=(jax.ShapeDtypeStruct((S, D), BF16), jax.ShapeDtypeStruct((D // dh, S, nkb), F32)),
                          scratch_shapes=[pltpu.VMEM((hpb, bq, LANES), F32)],
                          compiler_params=_params(("parallel", "parallel")), name=name)(q, k, v)


def attn_bwd(name, q, k, v, do, runs, dk0=None, dv0=None, bq=512, w=256):
    S, D = q.shape
    dh = HEAD_DIM
    hpb = LANES // dh
    bq = _tile(S, bq)
    w = _tile(bq, w)
    nsub = bq // w
    nkb = S // w
    scale = 1.0 / math.sqrt(dh)
    init = dk0 is not None

    def body(*refs):
        if init:
            q_ref, k_ref, v_ref, do_ref, runs_ref, dk0_ref, dv0_ref, dq_ref, dk_ref, dv_ref, rs_ref = refs
        else:
            q_ref, k_ref, v_ref, do_ref, runs_ref, dq_ref, dk_ref, dv_ref, rs_ref = refs
        qi = pl.program_id(1)

        @pl.when(qi == 0)
        def _():
            dk_ref[...] = dk0_ref[...] if init else jnp.zeros_like(dk_ref)
            dv_ref[...] = dv0_ref[...] if init else jnp.zeros_like(dv_ref)

        rr = lax.broadcasted_iota(jnp.int32, (w, w), 0)
        cc = lax.broadcasted_iota(jnp.int32, (w, w), 1)
        t_suf = (rr >= cc).astype(BF16)
        t_pre = (rr <= cc).astype(BF16)
        tq = qi * bq + lax.broadcasted_iota(jnp.int32, (bq, w), 0)
        tk = lax.broadcasted_iota(jnp.int32, (bq, w), 1)
        lane = lax.broadcasted_iota(jnp.int32, (bq, LANES), 1)
        lane1 = lax.broadcasted_iota(jnp.int32, (1, LANES), 1)
        ntot = (qi + 1) * nsub
        heads = [slice(hh * dh, (hh + 1) * dh) for hh in range(hpb)]
        qbs = [q_ref[:, hs] for hs in heads]
        dobs = [do_ref[:, hs].astype(BF16) for hs in heads]
        kb0 = ntot - nsub
        for hh in range(hpb):
            rs_ref[hh] = jnp.zeros((bq, LANES), F32)
            rs_ref[hh, :, 0:nkb] = runs_ref[hh]
            colmax = jnp.max(rs_ref[hh], axis=0, keepdims=True)
            dead = jnp.logical_and(jnp.logical_and(lane1 >= 1, lane1 <= ntot), colmax <= STICK_GONE)
            kb0 = jnp.minimum(kb0, jnp.sum(dead.astype(jnp.int32)))

        def block(kb, carry, masked):
            kstart = pl.multiple_of(kb * w, w)
            if masked:
                m = (tk + kstart) < tq
            out = []
            for hh, hs in enumerate(heads):
                pg_run, dq = carry[2 * hh], carry[2 * hh + 1]
                qb, dob = qbs[hh], dobs[hh]
                kblk = k_ref[pl.ds(kstart, w), hs]
                vblk = v_ref[pl.ds(kstart, w), hs]
                right = jnp.sum(jnp.where(lane == kb + 1, rs_ref[hh], 0.0), axis=1, keepdims=True)
                z = lax.dot_general(qb, kblk, NT, preferred_element_type=F32)
                lg, e = _neg_softplus(z)
                if masked:
                    lg = jnp.where(m, lg, 0.0)
                a = jnp.exp(z + _split_dot(lg, t_suf) + right)
                if masked:
                    a = jnp.where(m, a, 0.0)
                da = lax.dot_general(dob, vblk, NT, preferred_element_type=F32)
                g = da * a
                pin = _split_dot(g, t_pre) + pg_run
                sig = jnp.where(z >= 0.0, 1.0, e) / (1.0 + e)
                dz = g - sig * pin
                if masked:
                    dz = jnp.where(m, dz, 0.0)
                dzb = dz.astype(BF16)
                dq = dq + jnp.dot(dzb, kblk, preferred_element_type=F32)
                dk_ref[pl.ds(kstart, w), hs] += lax.dot_general(dzb, qb, TN, preferred_element_type=F32)
                dv_ref[pl.ds(kstart, w), hs] += lax.dot_general(a.astype(BF16), dob, TN, preferred_element_type=F32)
                out += [pin[:, w - 1:w], dq]
            return tuple(out)

        carry = (jnp.zeros((bq, 1), F32), jnp.zeros((bq, dh), F32)) * hpb
        carry = lax.fori_loop(kb0, qi * nsub, lambda kb, c: block(kb, c, False), carry)
        for sb in range(nsub):
            carry = block(qi * nsub + sb, carry, True)
        for hh, hs in enumerate(heads):
            dq_ref[:, hs] = carry[2 * hh + 1] * scale

    qs = pl.BlockSpec((bq, LANES), lambda h, i: (i, h))
    kv = pl.BlockSpec((S, LANES), lambda h, i: (0, h))
    ins = [q, k, v, do, runs] + ([dk0, dv0] if init else [])
    specs = [qs, kv, kv, qs, pl.BlockSpec((hpb, bq, nkb), lambda h, i: (h, i, 0))] + ([kv, kv] if init else [])
    sds = jax.ShapeDtypeStruct((S, D), F32)
    return pl.pallas_call(body, grid=(D // LANES, S // bq), in_specs=specs, out_specs=(qs, kv, kv), out_shape=(sds, sds, sds),
                          scratch_shapes=[pltpu.VMEM((hpb, bq, LANES), F32)],
                          compiler_params=_params(("parallel", "arbitrary")), name=name)(*ins)


def loss_head(name, y, tgt, tm=512):
    S, D = y.shape
    tm = _tile(S, tm)

    def body(y_ref, t_ref, dy_ref, acc_ref):
        @pl.when(pl.program_id(0) == 0)
        def _():
            acc_ref[...] = jnp.zeros_like(acc_ref)
        e = y_ref[...] - t_ref[...]
        dy_ref[...] = e * (1.0 / D)
        acc_ref[...] += jnp.sum(e * e)

    row = pl.BlockSpec((tm, D), lambda i: (i, 0))
    return pl.pallas_call(body, grid=(S // tm,), in_specs=[row, row],
                          out_specs=(row, pl.BlockSpec((8, LANES), lambda i: (0, 0))),
                          out_shape=(jax.ShapeDtypeStruct((S, D), F32), jax.ShapeDtypeStruct((8, LANES), F32)),
                          compiler_params=_params(("arbitrary",)), name=name)(y, tgt)


def _to_sm(a, axis=-1):
    axis = axis % a.ndim
    shp = a.shape[:axis] + (N_CHIPS, a.shape[axis] // N_CHIPS) + a.shape[axis + 1:]
    return jnp.moveaxis(a.reshape(shp), axis, 0)


def _from_sm(a, axis=-1):
    nd = a.ndim - 1
    axis = axis % nd
    b = jnp.moveaxis(a, 0, axis)
    return b.reshape(b.shape[:axis] + (b.shape[axis] * b.shape[axis + 1],) + b.shape[axis + 2:])


class GradBuffers:
    def __init__(self, W):
        groups = {}
        for n in BIG:
            _, layers, rows, cols = W[n].shape
            groups.setdefault(cols, []).append((rows, n, layers))
        self.where, self.cols_of, self.buf, self.members = {}, {}, {}, {}
        for cols, items in groups.items():
            off, members = 0, []
            for rows, n, layers in sorted(items, key=lambda t: -t[0]):
                assert off % rows == 0
                self.where[n], self.cols_of[n] = (off, rows), cols
                members.append((n, off, rows * layers))
                off += rows * layers
            assert off % 32 == 0
            self.buf[cols] = lax.empty((N_CHIPS, off, cols), F32)
            self.members[cols] = members

    def put(self, n, layer, fn, **kw):
        cols = self.cols_of[n]
        off, rows = self.where[n]
        self.buf[cols] = fn(into=self.buf[cols], off=off + layer * rows, **kw)


def forward_backward(x, p4, tgt, W):
    S, D = x.shape
    scale = 1.0 / math.sqrt(HEAD_DIM)
    saved = []
    kh = vh = xb_kv = None
    xb = x.astype(BF16)
    for i in range(DEPTH):
        sv = {'xb': xb}
        if i < N_A:
            h_sm = mm_colsm(f"pw1_{i}", xb, W['a_pw1_w'], i, W['a_pw1_b'])
            u = glu_fwd(f"glu_{i}", h_sm)
            c, s = conv_ln_silu_fwd(f"convln_{i}", u, W['a_dw_w'], W['a_dw_b'], W['a_ln_g'], W['a_ln_b'], i)
            mix = mm_rowsm(f"pw2_{i}", s, W['a_pw2_w'], i, bias=W['a_pw2_b'])
            sv.update(h_sm=h_sm, u=u, c=c, s=s)
        else:
            j = i - N_A
            if kh is None:
                xb_kv = xb
                kh = mm_rowsm("wk", xb, W['kv_wk'], 0, out_dtype=BF16)
                vh = mm_rowsm("wv", xb, W['kv_wv'], 0, out_dtype=BF16)
            qh = mm_rowsm(f"wq_{j}", xb, W['b_wq'], j, out_dtype=BF16, out_scale=scale)
            o, runs = attn_fwd(f"attn_{j}", qh, kh, vh)
            mix = mm_rowsm(f"wo_{j}", o, W['b_wo'], j)
            sv.update(qh=qh, o=o, runs=runs)
        x1, x1b, xh1, rs1 = ln_fwd(f"lnmix_{i}", x, mix, W['ln_mix_g'], W['ln_mix_b'], i)
        up_sm = mm_colsm(f"up_{i}", x1b, W['ffn_w_up'], i, out_dtype=BF16)
        gp_sm = mm_colsm(f"gate_{i}", x1b, W['ffn_w_gate'], i, out_dtype=BF16)
        hf_sm = ffn_gate_fwd(f"ffngate_{i}", up_sm, gp_sm, W['ffn_conv_w'], W['ffn_conv_b'], i)
        ffn = mm_rowsm(f"down_{i}", hf_sm, W['ffn_w_down'], i, a_sm=True)
        tg = mm_rowsm(f"plegate_{i}", x1b, W['ple_w_gate'], i)
        pp = mm_proj(f"pleproj_{i}", p4, i, W['ple_w_proj'])
        x2, x2b, xh2, rs2 = ln_fwd(f"lnffn_{i}", x1, ffn, W['ln_ffn_g'], W['ln_ffn_b'], i, tg=tg, pp=pp)
        sv.update(x1b=x1b, xh1=xh1, rs1=rs1, up_sm=up_sm, gp_sm=gp_sm, hf_sm=hf_sm, tg=tg, pp=pp, xh2=xh2, rs2=rs2)
        saved.append(sv)
        x, xb = x2, x2b

    dx, lacc = loss_head("loss", x, tgt)
    loss_sum = lacc[0, 0]

    G = {n: [None] * DEPTH for n in WEIGHTS if n not in BIG}
    gb = GradBuffers(W)
    dk = dv = None
    for i in reversed(range(DEPTH)):
        sv = saved[i]
        dr, acc, dtg, dpp = ln_bwd(f"lnffn_b_{i}", dx, sv['xh2'], sv['rs2'], W['ln_ffn_g'], i, tg=sv['tg'], pp=sv['pp'])
        G['ln_ffn_g'][i], G['ln_ffn_b'][i] = acc[0], acc[1]
        gb.put('ple_w_proj', i, functools.partial(mm_tn_proj, f"dproj_{i}", p4, i, dpp))
        gb.put('ple_w_gate', i, functools.partial(mm_tn_rowsm_fan, f"dplegate_{i}", sv['x1b'], dtg))
        gb.put('ffn_w_down', i, functools.partial(mm_tn_rowsm, f"ddown_{i}", sv['hf_sm'], dr, a_sm=True))
        dhf_sm = mm_nt_rowsm(f"dhf_{i}", dr, W['ffn_w_down'], i, out_sm=True, out_dtype=BF16)
        dup_sm, dgc_sm = ffn_gate_bwd_a(f"ffngate_ba_{i}", dhf_sm, sv['up_sm'], sv['gp_sm'], W['ffn_conv_w'], W['ffn_conv_b'], i)
        dgp_sm, cacc = ffn_gate_bwd_b(f"ffngate_bb_{i}", dgc_sm, sv['gp_sm'], W['ffn_conv_w'], i)
        kw = W['ffn_conv_w'].shape[2]
        G['ffn_conv_w'][i] = cacc[:, 0:kw, :]
        G['ffn_conv_b'][i] = cacc[:, 7, :].reshape(-1)
        gb.put('ffn_w_up', i, functools.partial(mm_tn_colsm, f"dup_{i}", sv['x1b'], dup_sm))
        gb.put('ffn_w_gate', i, functools.partial(mm_tn_colsm, f"dgate_{i}", sv['x1b'], dgp_sm))
        dx1 = mm_nt_rowsm(f"dx1a_{i}", dtg, W['ple_w_gate'], i, res=dr, res_alpha=DN_ALPHA)
        dx1 = mm_nt_colsm(f"dx1b_{i}", _sm_parts(dup_sm), W['ffn_w_up'], i, res=dx1)
        dx1 = mm_nt_colsm(f"dx1c_{i}", _sm_parts(dgp_sm), W['ffn_w_gate'], i, res=dx1)

        dr1, acc1 = ln_bwd(f"lnmix_b_{i}", dx1, sv['xh1'], sv['rs1'], W['ln_mix_g'], i)
        G['ln_mix_g'][i], G['ln_mix_b'][i] = acc1[0], acc1[1]
        xin = sv['xb']
        if i < N_A:
            G['a_pw2_b'][i] = acc1[2]
            gb.put('a_pw2_w', i, functools.partial(mm_tn_rowsm_fan, f"dpw2_{i}", sv['s'], dr1))
            ds = mm_nt_rowsm(f"ds_{i}", dr1, W['a_pw2_w'], i)
            dc, cacc = ln_silu_bwd(f"lnsilu_b_{i}", ds, sv['c'], W['a_ln_g'], W['a_ln_b'], i)
            G['a_ln_g'][i], G['a_ln_b'][i], G['a_dw_b'][i] = cacc[0], cacc[1], cacc[2]
            da, dg, dw, dba, dbg = conv_glu_bwd(f"convglu_b_{i}", dc, sv['u'], sv['h_sm'], W['a_dw_w'], i)
            kw = W['a_dw_w'].shape[1]
            G['a_dw_w'][i] = _from_sm(dw[:, 0:kw, :], axis=-1)
            G['a_pw1_b'][i] = jnp.concatenate([dba[:, 0, :], dbg[:, 0, :]], axis=0)
            half = da.shape[0]
            gb.put('a_pw1_w', i, functools.partial(mm_tn_colsm, f"dpw1a_{i}", xin, da))
            gb.put('a_pw1_w', i, functools.partial(mm_tn_colsm, f"dpw1g_{i}", xin, dg), j0=half)
            dx = mm_nt_colsm(f"dxa_{i}", _sm_parts(da) + _sm_parts(dg), W['a_pw1_w'], i, res=dr1, res_alpha=DN_ALPHA)
        else:
            j = i - N_A
            gb.put('b_wo', j, functools.partial(mm_tn_rowsm_fan, f"dwo_{j}", sv['o'], dr1))
            do = mm_nt_rowsm(f"do_{j}", dr1, W['b_wo'], j)
            dq, dk, dv = attn_bwd(f"attn_b_{j}", sv['qh'], kh, vh, do, sv['runs'], dk, dv)
            gb.put('b_wq', j, functools.partial(mm_tn_rowsm_fan, f"dwq_{j}", xin, dq))
            dx = mm_nt_rowsm(f"dxq_{j}", dq, W['b_wq'], j, res=dr1, res_alpha=DN_ALPHA)
            if j == 0:
                gb.put('kv_wk', 0, functools.partial(mm_tn_rowsm_fan, "dwk", xb_kv, dk))
                gb.put('kv_wv', 0, functools.partial(mm_tn_rowsm_fan, "dwv", xb_kv, dv))
                dx = mm_nt_rowsm("dxk", dk, W['kv_wk'], 0, res=dx)
                dx = mm_nt_rowsm("dxv", dv, W['kv_wv'], 0, res=dx)
    return loss_sum, dx, G, gb


MESH = pl.DeviceIdType.MESH
HBM = pl.BlockSpec(memory_space=pltpu.HBM)


def _place():
    x, y, c = lax.axis_index("x"), lax.axis_index("y"), lax.axis_index("c")
    others = [(1 - x, y), (x, 1 - y), (1 - x, 1 - y)]
    return x, y, c, others


def allgather_chips(name, arrs):
    n = len(arrs)

    def body(*refs):
        ins, outs = refs[:n], refs[n:2 * n]
        send_sems, recv_sems = refs[2 * n:]
        x, y, c, others = _place()
        me = 2 * x + y
        sibling = (x, y, 1 - c)
        ids = [2 * ch[0] + ch[1] for ch in others]
        from_id = jnp.where(c == 0, ids[0], ids[1])
        to_chip = (jnp.where(c == 0, x, 1 - x), jnp.where(c == 0, 1 - y, y))

        def remote(a, k, src, chip_id, half, to):
            return pltpu.make_async_remote_copy(src_ref=src, dst_ref=outs[a].at[chip_id, half], send_sem=send_sems.at[a, k],
                                                recv_sem=recv_sems.at[a, k], device_id=to, device_id_type=MESH)

        sent = [remote(a, k, ins[a].at[c], me, c, (others[k][0], others[k][1], c)) for a in range(n) for k in range(2)]
        for cp in sent:
            cp.start()
        for a in range(n):
            for k in range(2):
                remote(a, k, ins[a].at[c], ids[k], c, sibling).wait_recv()
            sent.append(remote(a, 2, outs[a].at[from_id, c], from_id, c, (to_chip[0], to_chip[1], c)))
            sent[-1].start()
            for k in range(2):
                sent.append(remote(a, 3 + k, outs[a].at[ids[k], c], ids[k], c, sibling))
                sent[-1].start()
        for a in range(n):
            remote(a, 2, ins[a].at[c], ids[2], c, sibling).wait_recv()
            sent.append(remote(a, 5, outs[a].at[ids[2], c], ids[2], c, sibling))
            sent[-1].start()
        for a in range(n):
            for k in range(3):
                remote(a, 3 + k, ins[a].at[c], ids[k], 1 - c, sibling).wait_recv()
        for cp in sent:
            cp.wait_send()

    outs = pl.pallas_call(body, out_shape=tuple(jax.ShapeDtypeStruct((N_CHIPS,) + a.shape, a.dtype) for a in arrs),
                          in_specs=[HBM] * n, out_specs=tuple([HBM] * n),
                          scratch_shapes=[pltpu.SemaphoreType.DMA((n, 6)), pltpu.SemaphoreType.DMA((n, 6))],
                          name=name)(*arrs)
    me = 2 * lax.axis_index("x") + lax.axis_index("y")
    return [lax.dynamic_update_index_in_dim(o, a, me, 0) for o, a in zip(outs, arrs)]


def exchange_sibling(name, gs):
    n = len(gs)

    def body(*refs):
        g_refs, o_refs = refs[:n], refs[n:2 * n]
        send_sems, recv_sems = refs[2 * n:]
        x, y, c, _ = _place()
        cps = [pltpu.make_async_remote_copy(src_ref=g_refs[a].at[j, 1 - c], dst_ref=o_refs[a].at[j], send_sem=send_sems.at[a, j],
                                            recv_sem=recv_sems.at[a, j], device_id=(x, y, 1 - c), device_id_type=MESH)
               for a in range(n) for j in range(N_CHIPS)]
        for cp in cps:
            cp.start()
        for cp in cps:
            cp.wait()

    return pl.pallas_call(body, out_shape=tuple(jax.ShapeDtypeStruct((N_CHIPS,) + g.shape[2:], g.dtype) for g in gs),
                          in_specs=[HBM] * n, out_specs=tuple([HBM] * n),
                          scratch_shapes=[pltpu.SemaphoreType.DMA((n, N_CHIPS)), pltpu.SemaphoreType.DMA((n, N_CHIPS))],
                          name=name)(*gs)


def scatter_chips(name, ss):
    n = len(ss)

    def body(*refs):
        s_refs, o_refs = refs[:n], refs[n:2 * n]
        send_sems, recv_sems = refs[2 * n:]
        x, y, c, others = _place()
        cps = [pltpu.make_async_remote_copy(src_ref=s_refs[a].at[2 * ch[0] + ch[1]], dst_ref=o_refs[a].at[k],
                                            send_sem=send_sems.at[a, k], recv_sem=recv_sems.at[a, k],
                                            device_id=(ch[0], ch[1], c), device_id_type=MESH)
               for a in range(n) for k, ch in enumerate(others)]
        for cp in cps:
            cp.start()
        for cp in cps:
            cp.wait()

    return pl.pallas_call(body, out_shape=tuple(jax.ShapeDtypeStruct((3,) + s.shape[1:], s.dtype) for s in ss),
                          in_specs=[HBM] * n, out_specs=tuple([HBM] * n),
                          scratch_shapes=[pltpu.SemaphoreType.DMA((n, 3)), pltpu.SemaphoreType.DMA((n, 3))], name=name)(*ss)


def share_sibling(name, ts):
    n = len(ts)

    def body(*refs):
        o_refs = refs[n:2 * n]
        send_sems, recv_sems = refs[2 * n:]
        x, y, c, _ = _place()
        cps = [pltpu.make_async_remote_copy(src_ref=o_refs[a].at[c], dst_ref=o_refs[a].at[c], send_sem=send_sems.at[a],
                                            recv_sem=recv_sems.at[a], device_id=(x, y, 1 - c), device_id_type=MESH)
               for a in range(n)]
        for cp in cps:
            cp.start()
        for a in range(n):
            pltpu.make_async_remote_copy(src_ref=o_refs[a].at[c], dst_ref=o_refs[a].at[1 - c], send_sem=send_sems.at[a],
                                         recv_sem=recv_sems.at[a], device_id=(x, y, 1 - c), device_id_type=MESH).wait_recv()
        for cp in cps:
            cp.wait_send()

    return pl.pallas_call(body, out_shape=tuple(jax.ShapeDtypeStruct(t.shape, t.dtype) for t in ts),
                          in_specs=[HBM] * n, out_specs=tuple([HBM] * n), input_output_aliases={a: a for a in range(n)},
                          scratch_shapes=[pltpu.SemaphoreType.DMA((n,)), pltpu.SemaphoreType.DMA((n,))],
                          name=name)(*ts)


def add_halves(name, g, recv, place, out_dtype, tr=512):
    _, _, R, C = g.shape
    tr = _tile(R, tr, 16)

    def body(p_ref, a_ref, b_ref, o_ref):
        o_ref[...] = (a_ref[...] + b_ref[...]).astype(o_ref.dtype)

    blk = pl.BlockSpec((None, tr, C), lambda j, i, p: (j, i, 0))
    gs = pltpu.PrefetchScalarGridSpec(num_scalar_prefetch=1, grid=(N_CHIPS, R // tr),
                                      in_specs=[pl.BlockSpec((None, None, tr, C), lambda j, i, p: (j, p[0], i, 0)), blk],
                                      out_specs=blk)
    return pl.pallas_call(body, grid_spec=gs, out_shape=jax.ShapeDtypeStruct((N_CHIPS, R, C), out_dtype),
                          compiler_params=_params(("parallel", "parallel")), name=name)(place, g, recv)


def add_chips(name, g, r1, r2, place, tr=512):
    _, _, R, C = g.shape
    tr = _tile(R, tr, 16)

    def body(p_ref, a_ref, b_ref, c_ref, o_ref):
        s = a_ref[...] + b_ref[...]
        o_ref[...] = ((s + c_ref[0].astype(F32)) + c_ref[1].astype(F32)) + c_ref[2].astype(F32)

    gs = pltpu.PrefetchScalarGridSpec(num_scalar_prefetch=1, grid=(R // tr,),
                                      in_specs=[pl.BlockSpec((None, None, tr, C), lambda i, p: (p[1], p[0], i, 0)),
                                                pl.BlockSpec((None, tr, C), lambda i, p: (p[1], i, 0)),
                                                pl.BlockSpec((3, tr, C), lambda i, p: (0, i, 0))],
                                      out_specs=pl.BlockSpec((None, tr, C), lambda i, p: (p[0], i, 0)))
    return pl.pallas_call(body, grid_spec=gs, out_shape=jax.ShapeDtypeStruct((2, R, C), F32),
                          compiler_params=_params(("parallel",)), name=name)(place, g, r1, r2)


def reduce_scatter(gs, wire_dtypes, place):
    r1 = exchange_sibling("rs_sibling", gs)
    s1 = [add_halves(f"rs_add_cores_{a}", g, r, place, dt) for a, (g, r, dt) in enumerate(zip(gs, r1, wire_dtypes))]
    r2 = scatter_chips("rs_chips", s1)
    tot = [add_chips(f"rs_add_chips_{a}", g, r, rr, place) for a, (g, r, rr) in enumerate(zip(gs, r1, r2))]
    return share_sibling("rs_share", tot)


def adamw(name, w, g, m, v, tr=512):
    shp = w.shape
    cols = shp[-1]
    w2, g2, m2, v2 = (a.reshape(-1, cols) for a in (w, g, m, v))
    rows = w2.shape[0]
    tr = _tile(rows, tr)

    def body(w_ref, g_ref, m_ref, v_ref, d_ref, mo_ref, vo_ref):
        g_ = g_ref[...]
        m_ = ADAM_B1 * m_ref[...] + (1.0 - ADAM_B1) * g_
        v_ = ADAM_B2 * v_ref[...] + (1.0 - ADAM_B2) * (g_ * g_)
        m_hat = m_ / (1.0 - ADAM_B1 ** ADAM_STEP)
        v_hat = v_ / (1.0 - ADAM_B2 ** ADAM_STEP)
        d_ref[...] = -ADAM_LR * (m_hat / (jnp.sqrt(v_hat) + ADAM_EPS) + ADAM_WD * w_ref[...])
        mo_ref[...] = m_
        vo_ref[...] = v_

    blk = pl.BlockSpec((tr, cols), lambda i: (i, 0))
    sds = jax.ShapeDtypeStruct((rows, cols), F32)
    d, mo, vo = pl.pallas_call(body, grid=(rows // tr,), in_specs=[blk] * 4, out_specs=(blk, blk, blk), out_shape=(sds, sds, sds),
                               compiler_params=_params(("parallel",)), name=name)(w2, g2, m2, v2)
    return d.reshape(shp), mo.reshape(shp), vo.reshape(shp)


PACK_ALIGN = 1024


def _pad_to(a, mult, axis=-1):
    axis = axis % a.ndim
    extra = (-a.shape[axis]) % mult
    if extra == 0:
        return a
    pads = [(0, 0)] * a.ndim
    pads[axis] = (0, extra)
    return jnp.pad(a, pads)


def _pack(pieces, lead, row_mult):
    nl = len(lead)
    flat, offs, sizes, off = [], [], [], 0
    for a in pieces:
        f = a.reshape(lead + (-1,))
        sizes.append(f.shape[-1])
        f = _pad_to(f, PACK_ALIGN)
        offs.append(off)
        off += f.shape[-1]
        flat.append(f)
    cat = _pad_to(jnp.concatenate(flat, axis=nl), 2 * row_mult * LANES)
    return cat.reshape(lead + (2, -1, LANES)), offs, sizes


def _unpack(packed, lead, offs, sizes, shapes):
    flat = packed.reshape(lead + (-1,))
    return [lax.slice_in_dim(flat, o, o + s, axis=len(lead)).reshape(lead + tuple(shp)) for o, s, shp in zip(offs, sizes, shapes)]


def _stack_grads(G, names):
    out = {}
    for n in names:
        parts = [g for g in G[n] if g is not None]
        if n in ('kv_wk', 'kv_wv'):
            out[n] = parts[0]
        elif n in REPLICATED:
            out[n] = jnp.stack(parts, axis=0).reshape(N_CHIPS, -1)
        elif n in ('a_dw_w', 'a_dw_b', 'a_ln_g', 'a_ln_b', 'a_pw2_b'):
            out[n] = _to_sm(jnp.stack(parts, axis=0), axis=-1)
        else:
            out[n] = jnp.stack(parts, axis=1)
    return out


def _whole_weights(big, small, rep, D):
    W = {}
    for n in BIG:
        a = big[n]
        W[n] = a[:, None] if n in ('kv_wk', 'kv_wv') else a
    W['a_pw1_b'] = small['a_pw1_b'][:, :, None, :]
    W['a_dw_w'] = _from_sm(small['a_dw_w'], axis=-1)
    for n in ('a_dw_b', 'a_ln_g', 'a_ln_b', 'a_pw2_b'):
        W[n] = _from_sm(small[n], axis=-1)[:, None, :]
    W['ffn_conv_w'] = small['ffn_conv_w']
    L, F = rep['ffn_conv_b'].shape
    W['ffn_conv_b'] = rep['ffn_conv_b'].reshape(L, N_CHIPS, 1, F // N_CHIPS)
    for n in ('ln_mix_g', 'ln_mix_b', 'ln_ffn_g', 'ln_ffn_b'):
        W[n] = rep[n][:, None, :]
    return W


SMALL = ('a_pw1_b', 'a_dw_w', 'a_dw_b', 'a_ln_g', 'a_ln_b', 'a_pw2_b', 'ffn_conv_w')


def _step(x, p, loss_target, w, m, v):
    S, D = x.shape[-2:]
    x2, tgt = x.reshape(S, D), loss_target.reshape(S, D)
    place = jnp.stack([lax.axis_index("c"), 2 * lax.axis_index("x") + lax.axis_index("y")]).astype(jnp.int32)

    big_in = [w[n].astype(BF16).reshape((2, -1) + w[n].shape[1:] if w[n].ndim == 3 else (2, -1, w[n].shape[-1])) for n in BIG]
    small_in, s_offs, s_sizes = _pack([w[n] for n in SMALL], (), 8)
    gathered = allgather_chips("gather_weights", big_in + [small_in])
    big = {n: g.reshape((N_CHIPS,) + w[n].shape) for n, g in zip(BIG, gathered[:-1])}
    small = dict(zip(SMALL, _unpack(gathered[-1], (N_CHIPS,), s_offs, s_sizes, [w[n].shape for n in SMALL])))
    W = _whole_weights(big, small, {n: w[n] for n in REPLICATED}, D)

    loss_sum, dx, G, gb = forward_backward(x2, p, tgt, W)
    loss = lax.psum(0.5 * loss_sum / D, ("x", "y", "c"))

    vectors = [n for n in WEIGHTS if n not in BIG]
    mats = [b.reshape(N_CHIPS, 2, b.shape[1] // 2, b.shape[2]) for b in gb.buf.values()]
    members = [gb.members[cols] for cols in gb.buf]
    g_sm = _stack_grads(G, vectors)
    packed, offs, sizes = _pack([g_sm[n] for n in vectors], (N_CHIPS,), 512)
    reduced = reduce_scatter(mats + [packed], [BF16] * len(mats) + [F32], place)
    shapes = [w[n].shape if n not in REPLICATED else (w[n].size // N_CHIPS,) for n in vectors]
    g_mine = dict(zip(vectors, _unpack(reduced[-1], (), offs, sizes, shapes)))
    for red, where in zip(reduced[:-1], members):
        rows = red.reshape(-1, red.shape[-1])
        for n, off, cnt in where:
            g_mine[n] = lax.slice_in_dim(rows, off, off + cnt, axis=0).reshape(w[n].shape)
    rep_in, r_offs, r_sizes = _pack([g_mine[n] for n in REPLICATED], (), 8)
    rep_all = allgather_chips("gather_replicated_grads", [rep_in])[0]
    for n, g in zip(REPLICATED, _unpack(rep_all, (N_CHIPS,), r_offs, r_sizes, [(w[n].size // N_CHIPS,) for n in REPLICATED])):
        g_mine[n] = g.reshape(w[n].shape)

    grads, deltas, new_m, new_v = [], [], [], []
    for n in WEIGHTS:
        d, mo, vo = adamw(f"adamw_{n}", w[n], g_mine[n], m[n], v[n])
        grads.append(g_mine[n])
        deltas.append(d)
        new_m.append(mo)
        new_v.append(vo)
    return (loss, dx.reshape(x.shape), *grads, *deltas, *new_m, *new_v)


def kernel(x, p, a_pw1_w, a_pw1_b, a_dw_w, a_dw_b, a_ln_g, a_ln_b, a_pw2_w, a_pw2_b, b_wq, kv_wk, kv_wv, b_wo, ln_mix_g, ln_mix_b, ffn_w_up, ffn_w_gate, ffn_conv_w, ffn_conv_b, ffn_w_down, ple_w_gate, ple_w_proj, ln_ffn_g, ln_ffn_b, loss_target, m_a_pw1_w, m_a_pw1_b, m_a_dw_w, m_a_dw_b, m_a_ln_g, m_a_ln_b, m_a_pw2_w, m_a_pw2_b, m_b_wq, m_kv_wk, m_kv_wv, m_b_wo, m_ln_mix_g, m_ln_mix_b, m_ffn_w_up, m_ffn_w_gate, m_ffn_conv_w, m_ffn_conv_b, m_ffn_w_down, m_ple_w_gate, m_ple_w_proj, m_ln_ffn_g, m_ln_ffn_b, v_a_pw1_w, v_a_pw1_b, v_a_dw_w, v_a_dw_b, v_a_ln_g, v_a_ln_b, v_a_pw2_w, v_a_pw2_b, v_b_wq, v_kv_wk, v_kv_wv, v_b_wo, v_ln_mix_g, v_ln_mix_b, v_ffn_w_up, v_ffn_w_gate, v_ffn_conv_w, v_ffn_conv_b, v_ffn_w_down, v_ple_w_gate, v_ple_w_proj, v_ln_ffn_g, v_ln_ffn_b):
    vals = dict(locals())
    w = {n: vals[n] for n in WEIGHTS}
    m = {n: vals["m_" + n] for n in WEIGHTS}
    v = {n: vals["v_" + n] for n in WEIGHTS}
    return _step(x, p, loss_target, w, m, v)
```

```python
import functools
import math

import jax
import jax.numpy as jnp
import numpy as np
from jax import lax
from jax.experimental import pallas as pl
from jax.experimental.pallas import tpu as pltpu

F32, BF16 = jnp.float32, jnp.bfloat16

HEAD_DIM = 64
LN_EPS = 1e-5
DEPTH = 4
N_A = DEPTH // 2
DN_ALPHA = (2.0 * DEPTH) ** 0.25
N_CHIPS = 4

ADAM_LR, ADAM_B1, ADAM_B2, ADAM_EPS, ADAM_WD, ADAM_STEP = 0.001, 0.9, 0.999, 1e-08, 0.01, 10

VMEM_LIMIT_BYTES = 56 * 2**20
LANES = 128
CONV_HALO = 32
FFN_HALO = 16

NN = (((1,), (0,)), ((), ()))
NT = (((1,), (1,)), ((), ()))
TN = (((0,), (0,)), ((), ()))

WEIGHTS = ['a_pw1_w', 'a_pw1_b', 'a_dw_w', 'a_dw_b', 'a_ln_g', 'a_ln_b', 'a_pw2_w', 'a_pw2_b', 'b_wq', 'kv_wk', 'kv_wv',
           'b_wo', 'ln_mix_g', 'ln_mix_b', 'ffn_w_up', 'ffn_w_gate', 'ffn_conv_w', 'ffn_conv_b', 'ffn_w_down', 'ple_w_gate',
           'ple_w_proj', 'ln_ffn_g', 'ln_ffn_b']
REPLICATED = ('ln_mix_g', 'ln_mix_b', 'ffn_conv_b', 'ln_ffn_g', 'ln_ffn_b')
BIG = ('a_pw1_w', 'a_pw2_w', 'b_wq', 'kv_wk', 'kv_wv', 'b_wo', 'ffn_w_up', 'ffn_w_gate', 'ffn_w_down', 'ple_w_gate',
       'ple_w_proj')


def _tile(n, pref, mult=8):
    t = min(n, pref)
    while t > 0:
        if n % t == 0 and t % mult == 0:
            return t
        t -= 1
    return n


def _params(sem):
    return pltpu.CompilerParams(dimension_semantics=sem, vmem_limit_bytes=VMEM_LIMIT_BYTES)


def _sigmoid(x):
    return 1.0 / (1.0 + jnp.exp(-x))


def _mm_call(name, grid, terms, out_spec, out_sds, dims, bias=None, res=None, res_alpha=1.0, out_scale=None, into=None):
    n_terms = len(terms)

    def body(*refs):
        o_ref = refs[-1]
        acc = None
        for t in range(n_terms):
            a = refs[2 * t][...].astype(BF16)
            b = refs[2 * t + 1][...].astype(BF16)
            d = lax.dot_general(a, b, dims, preferred_element_type=F32)
            acc = d if acc is None else acc + d
        k = 2 * n_terms
        if bias is not None:
            acc = acc + refs[k][...]
            k += 1
        if res is not None:
            acc = acc + res_alpha * refs[k][...]
        if out_scale is not None:
            acc = acc * out_scale
        o_ref[...] = acc.astype(o_ref.dtype)

    operands, specs = [], []
    for a, a_spec, b, b_spec in terms:
        operands += [a, b]
        specs += [a_spec, b_spec]
    for extra in (bias, res):
        if extra is not None:
            operands.append(extra[0])
            specs.append(extra[1])
    aliases = {}
    if into is not None:
        aliases = {len(operands): 0}
        operands.append(into)
        specs.append(pl.BlockSpec(memory_space=pl.ANY))
        out_sds = jax.ShapeDtypeStruct(into.shape, into.dtype)
    return pl.pallas_call(body, out_shape=out_sds, grid=grid, in_specs=specs, out_specs=out_spec, input_output_aliases=aliases,
                          compiler_params=_params(("parallel",) * len(grid)), name=name)(*operands)


def _mm_fanout(name, a, a_spec, w4, w_block, layer, dims, out_spec, out_sds, store, grid, bias4=None, res=None, res_alpha=1.0):
    def body(*refs):
        a_ref, w_refs, o_ref = refs[0], refs[1:1 + N_CHIPS], refs[-1]
        k = 1 + N_CHIPS
        b_refs = refs[k:k + N_CHIPS] if bias4 is not None else None
        k += N_CHIPS if bias4 is not None else 0
        av = a_ref[...].astype(BF16)
        for j in range(N_CHIPS):
            d = lax.dot_general(av, w_refs[j][...].astype(BF16), dims, preferred_element_type=F32)
            if b_refs is not None:
                d = d + b_refs[j][...]
            if res is not None:
                d = d + res_alpha * res[2](refs[k], j)
            store(o_ref, j, d)

    nd = len(grid)
    operands = [a] + [w4] * N_CHIPS
    specs = [a_spec] + [pl.BlockSpec((None, None) + w_block, lambda *g, j=j: (j, layer, 0, 0)) for j in range(N_CHIPS)]
    if bias4 is not None:
        operands += [bias4] * N_CHIPS
        specs += [pl.BlockSpec((None, None, 1, bias4.shape[-1]), lambda *g, j=j: (j, layer, 0, 0)) for j in range(N_CHIPS)]
    if res is not None:
        operands.append(res[0])
        specs.append(res[1])
    return pl.pallas_call(body, out_shape=out_sds, grid=grid, in_specs=specs, out_specs=out_spec,
                          compiler_params=_params(("parallel",) * nd), name=name)(*operands)


def _store_slot(o_ref, j, d):
    o_ref[j] = d.astype(o_ref.dtype)


def mm_colsm(name, x, w4, layer, bias4=None, out_dtype=F32, tm=512):
    M, K = x.shape
    n = w4.shape[-1]
    tm = _tile(M, tm)
    return _mm_fanout(name, x, pl.BlockSpec((tm, K), lambda i: (i, 0)), w4, (K, n), layer, NN,
                      pl.BlockSpec((N_CHIPS, tm, n), lambda i: (0, i, 0)), jax.ShapeDtypeStruct((N_CHIPS, M, n), out_dtype),
                      _store_slot, (M // tm,), bias4=bias4)


def mm_rowsm(name, a, w4, layer, a_sm=False, bias=None, out_dtype=F32, out_scale=None, tm=512):
    kc, N = w4.shape[-2:]
    M = a.shape[-2]
    tm = _tile(M, tm)
    terms = []
    for j in range(N_CHIPS):
        if a_sm:
            a_spec = pl.BlockSpec((None, tm, kc), lambda i, j=j: (j, i, 0))
        else:
            a_spec = pl.BlockSpec((tm, kc), lambda i, j=j: (i, j))
        terms.append((a, a_spec, w4, pl.BlockSpec((None, None, kc, N), lambda i, j=j: (j, layer, 0, 0))))
    b = None if bias is None else (bias, pl.BlockSpec((None, 1, N), lambda i: (layer, 0, 0)))
    return _mm_call(name, (M // tm,), terms, pl.BlockSpec((tm, N), lambda i: (i, 0)), jax.ShapeDtypeStruct((M, N), out_dtype),
                    NN, bias=b, out_scale=out_scale)


def mm_nt_rowsm(name, dy, w4, layer, out_sm=False, res=None, res_alpha=1.0, out_dtype=F32, tm=512):
    kc, N = w4.shape[-2:]
    M = dy.shape[0]
    tm = _tile(M, tm)

    def store_cols(o_ref, j, d):
        o_ref[:, j * kc:(j + 1) * kc] = d

    if out_sm:
        out_spec, sds, store = pl.BlockSpec((N_CHIPS, tm, kc), lambda i: (0, i, 0)), jax.ShapeDtypeStruct((N_CHIPS, M, kc), out_dtype), _store_slot
    else:
        out_spec, sds, store = pl.BlockSpec((tm, N_CHIPS * kc), lambda i: (i, 0)), jax.ShapeDtypeStruct((M, N_CHIPS * kc), F32), store_cols
    r = None if res is None else (res, pl.BlockSpec((tm, N_CHIPS * kc), lambda i: (i, 0)), lambda ref, j: ref[:, j * kc:(j + 1) * kc])
    return _mm_fanout(name, dy, pl.BlockSpec((tm, N), lambda i: (i, 0)), w4, (kc, N), layer, NT, out_spec, sds, store, (M // tm,),
                      res=r, res_alpha=res_alpha)


def mm_nt_colsm(name, dy_parts, w4, layer, res=None, res_alpha=1.0, tm=512):
    K, n = w4.shape[-2:]
    M = dy_parts[0][0].shape[1]
    tm = _tile(M, tm)
    terms = [(arr, pl.BlockSpec((None, tm, n), lambda i, idx=idx: (idx, i, 0)), w4,
              pl.BlockSpec((None, None, K, n), lambda i, j=j: (j, layer, 0, 0))) for j, (arr, idx) in enumerate(dy_parts)]
    r = None if res is None else (res, pl.BlockSpec((tm, K), lambda i: (i, 0)))
    return _mm_call(name, (M // tm,), terms, pl.BlockSpec((tm, K), lambda i: (i, 0)), jax.ShapeDtypeStruct((M, K), F32), NT,
                    res=r, res_alpha=res_alpha)


def _sm_parts(a):
    return [(a, j) for j in range(a.shape[0])]


def mm_tn_colsm(name, x, dy, into, off, j0=0, tk=512):
    M, K = x.shape
    nj, _, n = dy.shape
    tk = _tile(K, tk, LANES)
    assert off % tk == 0
    terms = [(x, pl.BlockSpec((M, tk), lambda j, k: (0, k)), dy, pl.BlockSpec((None, M, n), lambda j, k: (j, 0, 0)))]
    return _mm_call(name, (nj, K // tk), terms, pl.BlockSpec((None, tk, n), lambda j, k: (j + j0, off // tk + k, 0)),
                    None, TN, into=into)


def mm_tn_rowsm(name, a, dy, into, off, a_sm=False, tn=512):
    M, N = dy.shape
    kc = a.shape[-1] if a_sm else a.shape[-1] // N_CHIPS
    tn = _tile(N, tn, LANES)
    assert off % kc == 0
    a_spec = pl.BlockSpec((None, M, kc), lambda j, n: (j, 0, 0)) if a_sm else pl.BlockSpec((M, kc), lambda j, n: (0, j))
    terms = [(a, a_spec, dy, pl.BlockSpec((M, tn), lambda j, n: (0, n)))]
    return _mm_call(name, (N_CHIPS, N // tn), terms, pl.BlockSpec((None, kc, tn), lambda j, n: (j, off // kc, n)),
                    None, TN, into=into)


def mm_tn_rowsm_fan(name, a, dy, into, off, tn=256):
    M, N = dy.shape
    kc = a.shape[-1] // N_CHIPS
    tn = _tile(N, tn, LANES)
    assert off % kc == 0

    def body(a_ref, dy_ref, into_ref, o_ref):
        dyb = dy_ref[...].astype(BF16)
        for j in range(N_CHIPS):
            aj = a_ref[:, j * kc:(j + 1) * kc].astype(BF16)
            o_ref[j] = lax.dot_general(aj, dyb, TN, preferred_element_type=F32)

    return pl.pallas_call(body, out_shape=jax.ShapeDtypeStruct(into.shape, into.dtype), grid=(N // tn,),
                          in_specs=[pl.BlockSpec((M, N_CHIPS * kc), lambda n: (0, 0)), pl.BlockSpec((M, tn), lambda n: (0, n)),
                                    pl.BlockSpec(memory_space=pl.ANY)],
                          out_specs=pl.BlockSpec((N_CHIPS, kc, tn), lambda n: (0, off // kc, n)), input_output_aliases={2: 0},
                          compiler_params=_params(("parallel",)), name=name)(a, dy, into)


def mm_proj(name, p4, layer, w4, tm=512):
    S, P = p4.shape[-2:]
    n = w4.shape[-1]
    tm = _tile(S, tm)

    def store_cols(o_ref, j, d):
        o_ref[:, j * n:(j + 1) * n] = d

    return _mm_fanout(name, p4, pl.BlockSpec((None, None, tm, P), lambda i: (layer, 0, i, 0)), w4, (P, n), layer, NN,
                      pl.BlockSpec((tm, N_CHIPS * n), lambda i: (i, 0)), jax.ShapeDtypeStruct((S, N_CHIPS * n), F32),
                      store_cols, (S // tm,))


def mm_tn_proj(name, p4, layer, dpp, into, off):
    S, P = p4.shape[-2:]
    n = dpp.shape[-1] // N_CHIPS
    assert off % P == 0
    terms = [(p4, pl.BlockSpec((None, None, S, P), lambda j: (layer, 0, 0, 0)), dpp, pl.BlockSpec((S, n), lambda j: (0, j)))]
    return _mm_call(name, (N_CHIPS,), terms, pl.BlockSpec((None, P, n), lambda j: (j, off // P, 0)), None, TN, into=into)


def ln_fwd(name, x, mix, g, b, layer, tg=None, pp=None, tm=256):
    S, D = x.shape
    tm = _tile(S, tm, 16)
    ple = tg is not None

    def body(*refs):
        if ple:
            x_ref, m_ref, tg_ref, pp_ref, g_ref, b_ref, y_ref, yb_ref, xh_ref, rs_ref = refs
        else:
            x_ref, m_ref, g_ref, b_ref, y_ref, yb_ref, xh_ref, rs_ref = refs
        r = DN_ALPHA * x_ref[...] + m_ref[...]
        if ple:
            r = r + _sigmoid(tg_ref[...]) * pp_ref[...]
        mu = jnp.mean(r, axis=-1, keepdims=True)
        d = r - mu
        var = jnp.mean(d * d, axis=-1, keepdims=True)
        rstd = lax.rsqrt(var + LN_EPS)
        xh = d * rstd
        y = xh * g_ref[...] + b_ref[...]
        y_ref[...] = y
        yb_ref[...] = y.astype(BF16)
        xh_ref[...] = xh
        rs_ref[...] = rstd

    row = pl.BlockSpec((tm, D), lambda i: (i, 0))
    vec = pl.BlockSpec((None, 1, D), lambda i: (layer, 0, 0))
    ins = [x, mix] + ([tg, pp] if ple else []) + [g, b]
    specs = [row, row] + ([row, row] if ple else []) + [vec, vec]
    return pl.pallas_call(body, grid=(S // tm,), in_specs=specs,
                          out_specs=(row, row, row, pl.BlockSpec((tm, 1), lambda i: (i, 0))),
                          out_shape=(jax.ShapeDtypeStruct((S, D), F32), jax.ShapeDtypeStruct((S, D), BF16),
                                     jax.ShapeDtypeStruct((S, D), F32), jax.ShapeDtypeStruct((S, 1), F32)),
                          compiler_params=_params(("parallel",)), name=name)(*ins)


def ln_bwd(name, dy, xh, rstd, g, layer, tg=None, pp=None, tm=256):
    S, D = dy.shape
    tm = _tile(S, tm)
    ple = tg is not None

    def body(*refs):
        if ple:
            dy_ref, xh_ref, rs_ref, g_ref, tg_ref, pp_ref, dr_ref, acc_ref, dtg_ref, dpp_ref = refs
        else:
            dy_ref, xh_ref, rs_ref, g_ref, dr_ref, acc_ref = refs
        dy_, xh_ = dy_ref[...], xh_ref[...]
        dxh = dy_ * g_ref[...]
        m1 = jnp.mean(dxh, axis=-1, keepdims=True)
        m2 = jnp.mean(dxh * xh_, axis=-1, keepdims=True)
        dr = rs_ref[...] * (dxh - m1 - xh_ * m2)
        dr_ref[...] = dr

        @pl.when(pl.program_id(0) == 0)
        def _():
            acc_ref[...] = jnp.zeros_like(acc_ref)
        acc_ref[0:1, :] += jnp.sum(dy_ * xh_, axis=0, keepdims=True)
        acc_ref[1:2, :] += jnp.sum(dy_, axis=0, keepdims=True)
        acc_ref[2:3, :] += jnp.sum(dr, axis=0, keepdims=True)
        if ple:
            pg = _sigmoid(tg_ref[...])
            dtg_ref[...] = (dr * pp_ref[...] * pg * (1.0 - pg)).astype(BF16)
            dpp_ref[...] = (dr * pg).astype(BF16)

    row = pl.BlockSpec((tm, D), lambda i: (i, 0))
    ins = [dy, xh, rstd, g] + ([tg, pp] if ple else [])
    specs = [row, row, pl.BlockSpec((tm, 1), lambda i: (i, 0)), pl.BlockSpec((None, 1, D), lambda i: (layer, 0, 0))] + ([row, row] if ple else [])
    outs = [jax.ShapeDtypeStruct((S, D), F32), jax.ShapeDtypeStruct((8, D), F32)]
    out_specs = [row, pl.BlockSpec((8, D), lambda i: (0, 0))]
    if ple:
        outs += [jax.ShapeDtypeStruct((S, D), BF16)] * 2
        out_specs += [row, row]
    return pl.pallas_call(body, grid=(S // tm,), in_specs=specs, out_specs=tuple(out_specs), out_shape=tuple(outs),
                          compiler_params=_params(("arbitrary",)), name=name)(*ins)


def glu_fwd(name, h_sm, tm=512):
    _, S, n = h_sm.shape
    tm = _tile(S, tm)
    half = N_CHIPS // 2

    def body(a_ref, g_ref, u_ref):
        u_ref[...] = a_ref[...] * _sigmoid(g_ref[...])

    return pl.pallas_call(body, grid=(half, S // tm),
                          in_specs=[pl.BlockSpec((None, tm, n), lambda j, i: (j, i, 0)),
                                    pl.BlockSpec((None, tm, n), lambda j, i: (j + half, i, 0))],
                          out_specs=pl.BlockSpec((tm, n), lambda j, i: (i, j)),
                          out_shape=jax.ShapeDtypeStruct((S, half * n), F32),
                          compiler_params=_params(("parallel", "parallel")), name=name)(h_sm, h_sm)


def conv_ln_silu_fwd(name, u, w, b, g, beta, layer, ts=128):
    S, D = u.shape
    kw = w.shape[1]
    ts = _tile(S, ts, CONV_HALO)
    lc = LANES if D % LANES == 0 else D

    def body(h_ref, u_ref, w_ref, b_ref, g_ref, be_ref, c_ref, s_ref, win_ref):
        i = pl.program_id(0)
        win_ref[0:CONV_HALO, :] = jnp.where(i == 0, 0.0, h_ref[...])
        win_ref[CONV_HALO:, :] = u_ref[...]
        for cc in range(D // lc):
            cs = slice(cc * lc, (cc + 1) * lc)
            acc = jnp.zeros((ts, lc), F32) + b_ref[:, cs]
            for k in range(kw):
                off = CONV_HALO - (kw - 1) + k
                acc = acc + w_ref[k:k + 1, cs] * win_ref[off:off + ts, cs]
            c_ref[:, cs] = acc
        c = c_ref[...]
        mu = jnp.mean(c, axis=-1, keepdims=True)
        d = c - mu
        var = jnp.mean(d * d, axis=-1, keepdims=True)
        nrm = d * lax.rsqrt(var + LN_EPS) * g_ref[...] + be_ref[...]
        s_ref[...] = (nrm * _sigmoid(nrm)).astype(BF16)

    row = pl.BlockSpec((ts, D), lambda i: (i, 0))
    vec = pl.BlockSpec((None, 1, D), lambda i: (layer, 0, 0))
    halo = pl.BlockSpec((CONV_HALO, D), lambda i: (jnp.maximum(i * (ts // CONV_HALO) - 1, 0), 0))
    return pl.pallas_call(body, grid=(S // ts,),
                          in_specs=[halo, row, pl.BlockSpec((None, kw, D), lambda i: (layer, 0, 0)), vec, vec, vec],
                          out_specs=(row, row),
                          out_shape=(jax.ShapeDtypeStruct((S, D), F32), jax.ShapeDtypeStruct((S, D), BF16)),
                          scratch_shapes=[pltpu.VMEM((ts + CONV_HALO, D), F32)],
                          compiler_params=_params(("parallel",)), name=name)(u, u, w, b, g, beta)


def ln_silu_bwd(name, ds, c, g, beta, layer, tm=256):
    S, D = c.shape
    tm = _tile(S, tm)

    def body(ds_ref, c_ref, g_ref, be_ref, dc_ref, acc_ref):
        c_ = c_ref[...]
        mu = jnp.mean(c_, axis=-1, keepdims=True)
        d = c_ - mu
        var = jnp.mean(d * d, axis=-1, keepdims=True)
        rstd = lax.rsqrt(var + LN_EPS)
        xh = d * rstd
        nrm = xh * g_ref[...] + be_ref[...]
        sg = _sigmoid(nrm)
        dn = ds_ref[...] * (sg * (1.0 + nrm * (1.0 - sg)))
        dxh = dn * g_ref[...]
        m1 = jnp.mean(dxh, axis=-1, keepdims=True)
        m2 = jnp.mean(dxh * xh, axis=-1, keepdims=True)
        dc = rstd * (dxh - m1 - xh * m2)
        dc_ref[...] = dc

        @pl.when(pl.program_id(0) == 0)
        def _():
            acc_ref[...] = jnp.zeros_like(acc_ref)
        acc_ref[0:1, :] += jnp.sum(dn * xh, axis=0, keepdims=True)
        acc_ref[1:2, :] += jnp.sum(dn, axis=0, keepdims=True)
        acc_ref[2:3, :] += jnp.sum(dc, axis=0, keepdims=True)

    row = pl.BlockSpec((tm, D), lambda i: (i, 0))
    vec = pl.BlockSpec((None, 1, D), lambda i: (layer, 0, 0))
    return pl.pallas_call(body, grid=(S // tm,), in_specs=[row, row, vec, vec],
                          out_specs=(row, pl.BlockSpec((8, D), lambda i: (0, 0))),
                          out_shape=(jax.ShapeDtypeStruct((S, D), F32), jax.ShapeDtypeStruct((8, D), F32)),
                          compiler_params=_params(("arbitrary",)), name=name)(ds, c, g, beta)


def conv_glu_bwd(name, dc, u, h_sm, w, layer, ts=128):
    S, D = dc.shape
    kw = w.shape[1]
    half = N_CHIPS // 2
    n = D // half
    ts = _tile(S, ts, CONV_HALO)
    nblk = S // ts
    lc = LANES if n % LANES == 0 else n

    def body(dc_ref, dcn_ref, u_ref, a_ref, g_ref, w_ref, da_ref, dg_ref, dw_ref, dba_ref, dbg_ref, dwin_ref):
        i = pl.program_id(1)
        dwin_ref[0:ts, :] = dc_ref[...]
        dwin_ref[ts:, :] = jnp.where(i == nblk - 1, 0.0, dcn_ref[...])

        @pl.when(i == 0)
        def _():
            dw_ref[...] = jnp.zeros_like(dw_ref)
            dba_ref[...] = jnp.zeros_like(dba_ref)
            dbg_ref[...] = jnp.zeros_like(dbg_ref)

        for cc in range(n // lc):
            cs = slice(cc * lc, (cc + 1) * lc)
            ub = u_ref[:, cs]
            du = jnp.zeros((ts, lc), F32)
            for k in range(kw):
                shifted = dwin_ref[kw - 1 - k:kw - 1 - k + ts, cs]
                du = du + w_ref[k:k + 1, cs] * shifted
                dw_ref[k:k + 1, cs] += jnp.sum(shifted * ub, axis=0, keepdims=True)
            a = a_ref[:, cs]
            sg = _sigmoid(g_ref[:, cs])
            da = du * sg
            dg = du * a * sg * (1.0 - sg)
            da_ref[:, cs] = da.astype(BF16)
            dg_ref[:, cs] = dg.astype(BF16)
            dba_ref[0:1, cs] += jnp.sum(da, axis=0, keepdims=True)
            dbg_ref[0:1, cs] += jnp.sum(dg, axis=0, keepdims=True)

    r = ts // CONV_HALO
    main = pl.BlockSpec((ts, n), lambda j, i: (i, j))
    nxt = pl.BlockSpec((CONV_HALO, n), lambda j, i: (jnp.minimum((i + 1) * r, S // CONV_HALO - 1), j))
    sm_a = pl.BlockSpec((None, ts, n), lambda j, i: (j, i, 0))
    sm_g = pl.BlockSpec((None, ts, n), lambda j, i: (j + half, i, 0))
    da, dg, dw, dba, dbg = pl.pallas_call(
        body, grid=(half, nblk),
        in_specs=[main, nxt, main, sm_a, sm_g, pl.BlockSpec((None, kw, n), lambda j, i: (layer, 0, j))],
        out_specs=(pl.BlockSpec((None, ts, n), lambda j, i: (j, i, 0)), pl.BlockSpec((None, ts, n), lambda j, i: (j, i, 0)),
                   pl.BlockSpec((None, 32, n), lambda j, i: (j, 0, 0)),
                   pl.BlockSpec((None, 8, n), lambda j, i: (j, 0, 0)), pl.BlockSpec((None, 8, n), lambda j, i: (j, 0, 0))),
        out_shape=(jax.ShapeDtypeStruct((half, S, n), BF16), jax.ShapeDtypeStruct((half, S, n), BF16),
                   jax.ShapeDtypeStruct((half, 32, n), F32),
                   jax.ShapeDtypeStruct((half, 8, n), F32), jax.ShapeDtypeStruct((half, 8, n), F32)),
        scratch_shapes=[pltpu.VMEM((ts + CONV_HALO, n), F32)],
        compiler_params=_params(("parallel", "arbitrary")), name=name)(dc, dc, u, h_sm, h_sm, w)
    return da, dg, dw, dba, dbg


ROW_CHUNK = 16


def _ffn_gc(win_ref, w_ref, b_ref, r0, rows, kw, base):
    gc = b_ref[...] + jnp.zeros((rows, win_ref.shape[1]), F32)
    for k in range(kw):
        off = r0 + base - (kw - 1) + k
        gc = gc + w_ref[k:k + 1, :] * win_ref[off:off + rows, :]
    return gc


def ffn_gate_fwd(name, up_sm, gp_sm, w4, b4, layer, ts=256):
    _, S, n = up_sm.shape
    kw = w4.shape[2]
    ts = _tile(S, ts, ROW_CHUNK)
    rc = ROW_CHUNK

    def body(up_ref, gp_ref, gph_ref, w_ref, b_ref, hf_ref, gc_ref, win_ref):
        i = pl.program_id(1)
        win_ref[0:FFN_HALO, :] = jnp.where(i == 0, 0.0, gph_ref[...].astype(F32))
        win_ref[FFN_HALO:, :] = gp_ref[...].astype(F32)
        for r0 in range(0, ts, rc):
            gc = _ffn_gc(win_ref, w_ref, b_ref, r0, rc, kw, FFN_HALO)
            gc_ref[r0:r0 + rc, :] = gc.astype(BF16)
            hf_ref[r0:r0 + rc, :] = (gc * _sigmoid(gc) * up_ref[r0:r0 + rc, :].astype(F32)).astype(BF16)

    main = pl.BlockSpec((None, ts, n), lambda j, i: (j, i, 0))
    prv = pl.BlockSpec((None, FFN_HALO, n), lambda j, i: (j, jnp.maximum(i * (ts // FFN_HALO) - 1, 0), 0))
    sds = jax.ShapeDtypeStruct((N_CHIPS, S, n), BF16)
    return pl.pallas_call(body, grid=(N_CHIPS, S // ts),
                          in_specs=[main, main, prv, pl.BlockSpec((None, None, kw, n), lambda j, i: (j, layer, 0, 0)),
                                    pl.BlockSpec((None, None, 1, n), lambda j, i: (layer, j, 0, 0))],
                          out_specs=(main, main), out_shape=(sds, sds),
                          scratch_shapes=[pltpu.VMEM((ts + FFN_HALO, n), F32)],
                          compiler_params=_params(("parallel", "parallel")), name=name)(up_sm, gp_sm, gp_sm, w4, b4)


def ffn_gate_bwd_a(name, dhf_sm, up_sm, gc_sm, ts=256):
    _, S, n = up_sm.shape
    ts = _tile(S, ts, ROW_CHUNK)
    rc = ROW_CHUNK

    def body(dhf_ref, up_ref, gc_ref, dup_ref, dgc_ref):
        for r0 in range(0, ts, rc):
            rows = slice(r0, r0 + rc)
            gc = gc_ref[rows, :].astype(F32)
            sg = _sigmoid(gc)
            dhf = dhf_ref[rows, :].astype(F32)
            dup_ref[rows, :] = (dhf * gc * sg).astype(BF16)
            dgc_ref[rows, :] = (dhf * up_ref[rows, :].astype(F32) * (sg * (1.0 + gc * (1.0 - sg)))).astype(BF16)

    main = pl.BlockSpec((None, ts, n), lambda j, i: (j, i, 0))
    sds = jax.ShapeDtypeStruct((N_CHIPS, S, n), BF16)
    return pl.pallas_call(body, grid=(N_CHIPS, S // ts), in_specs=[main, main, main], out_specs=(main, main), out_shape=(sds, sds),
                          compiler_params=_params(("parallel", "parallel")), name=name)(dhf_sm, up_sm, gc_sm)


def ffn_gate_bwd_b(name, dgc_sm, gp_sm, w4, layer, ts=256):
    _, S, n = gp_sm.shape
    kw = w4.shape[2]
    ts = _tile(S, ts, ROW_CHUNK)
    rc = ROW_CHUNK
    nblk = S // ts

    def body(dgc_ref, dgn_ref, gp_ref, w_ref, dgp_ref, acc_ref, dwin_ref):
        i = pl.program_id(1)
        dwin_ref[0:ts, :] = dgc_ref[...].astype(F32)
        dwin_ref[ts:, :] = jnp.where(i == nblk - 1, 0.0, dgn_ref[...].astype(F32))

        @pl.when(i == 0)
        def _():
            acc_ref[...] = jnp.zeros_like(acc_ref)
        sums = [jnp.zeros((8, n), F32) for _ in range(kw + 1)]
        for r0 in range(0, ts, rc):
            gp = gp_ref[r0:r0 + rc, :].astype(F32)
            dgp = jnp.zeros((rc, n), F32)
            for k in range(kw):
                d = kw - 1 - k
                shifted = dwin_ref[r0 + d:r0 + d + rc, :]
                dgp = dgp + w_ref[k:k + 1, :] * shifted
                prod = shifted * gp
                sums[k] = sums[k] + prod[0:8, :] + prod[8:16, :]
                if d == 0:
                    sums[kw] = sums[kw] + shifted[0:8, :] + shifted[8:16, :]
            dgp_ref[r0:r0 + rc, :] = dgp.astype(BF16)
        for k in range(kw):
            acc_ref[k:k + 1, :] += jnp.sum(sums[k], axis=0, keepdims=True)
        acc_ref[7:8, :] += jnp.sum(sums[kw], axis=0, keepdims=True)

    r = ts // FFN_HALO
    main = pl.BlockSpec((None, ts, n), lambda j, i: (j, i, 0))
    nxt = pl.BlockSpec((None, FFN_HALO, n), lambda j, i: (j, jnp.minimum((i + 1) * r, S // FFN_HALO - 1), 0))
    return pl.pallas_call(body, grid=(N_CHIPS, nblk),
                          in_specs=[main, nxt, main, pl.BlockSpec((None, None, kw, n), lambda j, i: (j, layer, 0, 0))],
                          out_specs=(main, pl.BlockSpec((None, 8, n), lambda j, i: (j, 0, 0))),
                          out_shape=(jax.ShapeDtypeStruct((N_CHIPS, S, n), BF16), jax.ShapeDtypeStruct((N_CHIPS, 8, n), F32)),
                          scratch_shapes=[pltpu.VMEM((ts + FFN_HALO, n), F32)],
                          compiler_params=_params(("parallel", "arbitrary")), name=name)(dgc_sm, dgc_sm, gp_sm, w4)


def _neg_softplus(z):
    e = jnp.exp(-jnp.abs(z))
    return -(jnp.maximum(z, 0.0) + jnp.log(1.0 + e)), e


def _split_dot(x, t):
    hi = x.astype(BF16)
    lo = (x - hi.astype(F32)).astype(BF16)
    return jnp.dot(hi, t, preferred_element_type=F32) + jnp.dot(lo, t, preferred_element_type=F32)


STICK_GONE = -100.0
NOT_SWEPT = -1e30


def attn_fwd(name, q, k, v, bq=512, w=256):
    S, D = q.shape
    dh = HEAD_DIM
    hpb = LANES // dh
    bq = _tile(S, bq)
    w = _tile(bq, w)
    nsub = bq // w
    nkb = S // w

    def body(q_ref, k_ref, v_ref, o_ref, runs_ref, rs_ref):
        qi = pl.program_id(1)
        rr = lax.broadcasted_iota(jnp.int32, (w, w), 0)
        cc = lax.broadcasted_iota(jnp.int32, (w, w), 1)
        t_suf = (rr >= cc).astype(BF16)
        tq = qi * bq + lax.broadcasted_iota(jnp.int32, (bq, w), 0)
        tk = lax.broadcasted_iota(jnp.int32, (bq, w), 1)
        lane = lax.broadcasted_iota(jnp.int32, (bq, LANES), 1)
        ntot = (qi + 1) * nsub
        heads = [slice(hh * dh, (hh + 1) * dh) for hh in range(hpb)]
        qbs = [q_ref[:, hs] for hs in heads]
        for hh in range(hpb):
            rs_ref[hh] = jnp.where(lane < ntot, NOT_SWEPT, 0.0)

        def block(kb, carry, masked):
            kstart = pl.multiple_of(kb * w, w)
            if masked:
                m = (tk + kstart) < tq
            out = []
            for hh, hs in enumerate(heads):
                run, acc = carry[2 * hh], carry[2 * hh + 1]
                kblk = k_ref[pl.ds(kstart, w), hs]
                vblk = v_ref[pl.ds(kstart, w), hs]
                z = lax.dot_general(qbs[hh], kblk, NT, preferred_element_type=F32)
                lg, _ = _neg_softplus(z)
                if masked:
                    lg = jnp.where(m, lg, 0.0)
                cum = _split_dot(lg, t_suf) + run
                a = jnp.exp(z + cum)
                if masked:
                    a = jnp.where(m, a, 0.0)
                acc = acc + jnp.dot(a.astype(BF16), vblk, preferred_element_type=F32)
                run = cum[:, 0:1]
                rs_ref[hh] = jnp.where(lane == kb, run, rs_ref[hh])
                out += [run, acc]
            return tuple(out)

        carry = (jnp.zeros((bq, 1), F32), jnp.zeros((bq, dh), F32)) * hpb
        for sb in reversed(range(nsub)):
            carry = block(qi * nsub + sb, carry, True)

        def cond(c):
            alive = functools.reduce(jnp.maximum, [jnp.max(c[1 + 2 * hh]) for hh in range(hpb)])
            return jnp.logical_and(c[0] >= 0, alive > STICK_GONE)

        def step(c):
            return (c[0] - 1,) + block(c[0], c[1:], False)
        carry = lax.while_loop(cond, step, (qi * nsub - 1,) + carry)[1:]
        for hh, hs in enumerate(heads):
            o_ref[:, hs] = carry[2 * hh + 1].astype(o_ref.dtype)
            runs_ref[hh] = rs_ref[hh, :, 0:nkb]

    qs = pl.BlockSpec((bq, LANES), lambda h, i: (i, h))
    kv = pl.BlockSpec((S, LANES), lambda h, i: (0, h))
    return pl.pallas_call(body, grid=(D // LANES, S // bq), in_specs=[qs, kv, kv],
                          out_specs=(qs, pl.BlockSpec((hpb, bq, nkb), lambda h, i: (h, i, 0))),
                          out_shape=(jax.ShapeDtypeStruct((S, D), BF16), jax.ShapeDtypeStruct((D // dh, S, nkb), F32)),
                          scratch_shapes=[pltpu.VMEM((hpb, bq, LANES), F32)],
                          compiler_params=_params(("parallel", "parallel")), name=name)(q, k, v)


def attn_bwd(name, q, k, v, do, runs, dk0=None, dv0=None, bq=512, w=256):
    S, D = q.shape
    dh = HEAD_DIM
    hpb = LANES // dh
    bq = _tile(S, bq)
    w = _tile(bq, w)
    nsub = bq // w
    nkb = S // w
    scale = 1.0 / math.sqrt(dh)
    init = dk0 is not None

    def body(*refs):
        if init:
            q_ref, k_ref, v_ref, do_ref, runs_ref, dk0_ref, dv0_ref, dq_ref, dk_ref, dv_ref, rs_ref = refs
        else:
            q_ref, k_ref, v_ref, do_ref, runs_ref, dq_ref, dk_ref, dv_ref, rs_ref = refs
        qi = pl.program_id(1)

        @pl.when(qi == 0)
        def _():
            dk_ref[...] = dk0_ref[...] if init else jnp.zeros_like(dk_ref)
            dv_ref[...] = dv0_ref[...] if init else jnp.zeros_like(dv_ref)

        rr = lax.broadcasted_iota(jnp.int32, (w, w), 0)
        cc = lax.broadcasted_iota(jnp.int32, (w, w), 1)
        t_suf = (rr >= cc).astype(BF16)
        t_pre = (rr <= cc).astype(BF16)
        tq = qi * bq + lax.broadcasted_iota(jnp.int32, (bq, w), 0)
        tk = lax.broadcasted_iota(jnp.int32, (bq, w), 1)
        lane = lax.broadcasted_iota(jnp.int32, (bq, LANES), 1)
        lane1 = lax.broadcasted_iota(jnp.int32, (1, LANES), 1)
        ntot = (qi + 1) * nsub
        heads = [slice(hh * dh, (hh + 1) * dh) for hh in range(hpb)]
        qbs = [q_ref[:, hs] for hs in heads]
        dobs = [do_ref[:, hs].astype(BF16) for hs in heads]
        kb0 = ntot - nsub
        for hh in range(hpb):
            rs_ref[hh] = jnp.zeros((bq, LANES), F32)
            rs_ref[hh, :, 0:nkb] = runs_ref[hh]
            colmax = jnp.max(rs_ref[hh], axis=0, keepdims=True)
            dead = jnp.logical_and(jnp.logical_and(lane1 >= 1, lane1 <= ntot), colmax <= STICK_GONE)
            kb0 = jnp.minimum(kb0, jnp.sum(dead.astype(jnp.int32)))

        def block(kb, carry, masked):
            kstart = pl.multiple_of(kb * w, w)
            if masked:
                m = (tk + kstart) < tq
            out = []
            for hh, hs in enumerate(heads):
                pg_run, dq = carry[2 * hh], carry[2 * hh + 1]
                qb, dob = qbs[hh], dobs[hh]
                kblk = k_ref[pl.ds(kstart, w), hs]
                vblk = v_ref[pl.ds(kstart, w), hs]
                right = jnp.sum(jnp.where(lane == kb + 1, rs_ref[hh], 0.0), axis=1, keepdims=True)
                z = lax.dot_general(qb, kblk, NT, preferred_element_type=F32)
                lg, e = _neg_softplus(z)
                if masked:
                    lg = jnp.where(m, lg, 0.0)
                a = jnp.exp(z + _split_dot(lg, t_suf) + right)
                if masked:
                    a = jnp.where(m, a, 0.0)
                da = lax.dot_general(dob, vblk, NT, preferred_element_type=F32)
                g = da * a
                pin = _split_dot(g, t_pre) + pg_run
                sig = jnp.where(z >= 0.0, 1.0, e) / (1.0 + e)
                dz = g - sig * pin
                if masked:
                    dz = jnp.where(m, dz, 0.0)
                dzb = dz.astype(BF16)
                dq = dq + jnp.dot(dzb, kblk, preferred_element_type=F32)
                dk_ref[pl.ds(kstart, w), hs] += lax.dot_general(dzb, qb, TN, preferred_element_type=F32)
                dv_ref[pl.ds(kstart, w), hs] += lax.dot_general(a.astype(BF16), dob, TN, preferred_element_type=F32)
                out += [pin[:, w - 1:w], dq]
            return tuple(out)

        carry = (jnp.zeros((bq, 1), F32), jnp.zeros((bq, dh), F32)) * hpb
        carry = lax.fori_loop(kb0, qi * nsub, lambda kb, c: block(kb, c, False), carry)
        for sb in range(nsub):
            carry = block(qi * nsub + sb, carry, True)
        for hh, hs in enumerate(heads):
            dq_ref[:, hs] = carry[2 * hh + 1] * scale

    qs = pl.BlockSpec((bq, LANES), lambda h, i: (i, h))
    kv = pl.BlockSpec((S, LANES), lambda h, i: (0, h))
    ins = [q, k, v, do, runs] + ([dk0, dv0] if init else [])
    specs = [qs, kv, kv, qs, pl.BlockSpec((hpb, bq, nkb), lambda h, i: (h, i, 0))] + ([kv, kv] if init else [])
    sds = jax.ShapeDtypeStruct((S, D), F32)
    return pl.pallas_call(body, grid=(D // LANES, S // bq), in_specs=specs, out_specs=(qs, kv, kv), out_shape=(sds, sds, sds),
                          scratch_shapes=[pltpu.VMEM((hpb, bq, LANES), F32)],
                          compiler_params=_params(("parallel", "arbitrary")), name=name)(*ins)


def loss_head(name, y, tgt, tm=512):
    S, D = y.shape
    tm = _tile(S, tm)

    def body(y_ref, t_ref, dy_ref, acc_ref):
        @pl.when(pl.program_id(0) == 0)
        def _():
            acc_ref[...] = jnp.zeros_like(acc_ref)
        e = y_ref[...] - t_ref[...]
        dy_ref[...] = e * (1.0 / D)
        acc_ref[...] += jnp.sum(e * e)

    row = pl.BlockSpec((tm, D), lambda i: (i, 0))
    return pl.pallas_call(body, grid=(S // tm,), in_specs=[row, row],
                          out_specs=(row, pl.BlockSpec((8, LANES), lambda i: (0, 0))),
                          out_shape=(jax.ShapeDtypeStruct((S, D), F32), jax.ShapeDtypeStruct((8, LANES), F32)),
                          compiler_params=_params(("arbitrary",)), name=name)(y, tgt)


def _to_sm(a, axis=-1):
    axis = axis % a.ndim
    shp = a.shape[:axis] + (N_CHIPS, a.shape[axis] // N_CHIPS) + a.shape[axis + 1:]
    return jnp.moveaxis(a.reshape(shp), axis, 0)


def _from_sm(a, axis=-1):
    nd = a.ndim - 1
    axis = axis % nd
    b = jnp.moveaxis(a, 0, axis)
    return b.reshape(b.shape[:axis] + (b.shape[axis] * b.shape[axis + 1],) + b.shape[axis + 2:])


class GradBuffers:
    def __init__(self, W):
        groups = {}
        for n in BIG:
            _, layers, rows, cols = W[n].shape
            groups.setdefault(cols, []).append((rows, n, layers))
        self.where, self.cols_of, self.buf, self.members = {}, {}, {}, {}
        for cols, items in groups.items():
            off, members = 0, []
            for rows, n, layers in sorted(items, key=lambda t: -t[0]):
                assert off % rows == 0
                self.where[n], self.cols_of[n] = (off, rows), cols
                members.append((n, off, rows * layers))
                off += rows * layers
            assert off % 32 == 0
            self.buf[cols] = lax.empty((N_CHIPS, off, cols), F32)
            self.members[cols] = members

    def put(self, n, layer, fn, **kw):
        cols = self.cols_of[n]
        off, rows = self.where[n]
        self.buf[cols] = fn(into=self.buf[cols], off=off + layer * rows, **kw)


def forward_backward(x, p4, tgt, W):
    S, D = x.shape
    scale = 1.0 / math.sqrt(HEAD_DIM)
    saved = []
    kh = vh = xb_kv = None
    xb = x.astype(BF16)
    for i in range(DEPTH):
        sv = {'xb': xb}
        if i < N_A:
            h_sm = mm_colsm(f"pw1_{i}", xb, W['a_pw1_w'], i, W['a_pw1_b'])
            u = glu_fwd(f"glu_{i}", h_sm)
            c, s = conv_ln_silu_fwd(f"convln_{i}", u, W['a_dw_w'], W['a_dw_b'], W['a_ln_g'], W['a_ln_b'], i)
            mix = mm_rowsm(f"pw2_{i}", s, W['a_pw2_w'], i, bias=W['a_pw2_b'])
            sv.update(h_sm=h_sm, u=u, c=c, s=s)
        else:
            j = i - N_A
            if kh is None:
                xb_kv = xb
                kh = mm_rowsm("wk", xb, W['kv_wk'], 0, out_dtype=BF16)
                vh = mm_rowsm("wv", xb, W['kv_wv'], 0, out_dtype=BF16)
            qh = mm_rowsm(f"wq_{j}", xb, W['b_wq'], j, out_dtype=BF16, out_scale=scale)
            o, runs = attn_fwd(f"attn_{j}", qh, kh, vh)
            mix = mm_rowsm(f"wo_{j}", o, W['b_wo'], j)
            sv.update(qh=qh, o=o, runs=runs)
        x1, x1b, xh1, rs1 = ln_fwd(f"lnmix_{i}", x, mix, W['ln_mix_g'], W['ln_mix_b'], i)
        up_sm = mm_colsm(f"up_{i}", x1b, W['ffn_w_up'], i, out_dtype=BF16)
        gp_sm = mm_colsm(f"gate_{i}", x1b, W['ffn_w_gate'], i, out_dtype=BF16)
        hf_sm, gc_sm = ffn_gate_fwd(f"ffngate_{i}", up_sm, gp_sm, W['ffn_conv_w'], W['ffn_conv_b'], i)
        ffn = mm_rowsm(f"down_{i}", hf_sm, W['ffn_w_down'], i, a_sm=True)
        tg = mm_rowsm(f"plegate_{i}", x1b, W['ple_w_gate'], i)
        pp = mm_proj(f"pleproj_{i}", p4, i, W['ple_w_proj'])
        x2, x2b, xh2, rs2 = ln_fwd(f"lnffn_{i}", x1, ffn, W['ln_ffn_g'], W['ln_ffn_b'], i, tg=tg, pp=pp)
        sv.update(x1b=x1b, xh1=xh1, rs1=rs1, up_sm=up_sm, gp_sm=gp_sm, gc_sm=gc_sm, hf_sm=hf_sm, tg=tg, pp=pp, xh2=xh2, rs2=rs2)
        saved.append(sv)
        x, xb = x2, x2b

    dx, lacc = loss_head("loss", x, tgt)
    loss_sum = lacc[0, 0]

    G = {n: [None] * DEPTH for n in WEIGHTS if n not in BIG}
    gb = GradBuffers(W)
    dk = dv = None
    for i in reversed(range(DEPTH)):
        sv = saved[i]
        dr, acc, dtg, dpp = ln_bwd(f"lnffn_b_{i}", dx, sv['xh2'], sv['rs2'], W['ln_ffn_g'], i, tg=sv['tg'], pp=sv['pp'])
        G['ln_ffn_g'][i], G['ln_ffn_b'][i] = acc[0], acc[1]
        gb.put('ple_w_proj', i, functools.partial(mm_tn_proj, f"dproj_{i}", p4, i, dpp))
        gb.put('ple_w_gate', i, functools.partial(mm_tn_rowsm_fan, f"dplegate_{i}", sv['x1b'], dtg))
        gb.put('ffn_w_down', i, functools.partial(mm_tn_rowsm, f"ddown_{i}", sv['hf_sm'], dr, a_sm=True))
        dhf_sm = mm_nt_rowsm(f"dhf_{i}", dr, W['ffn_w_down'], i, out_sm=True, out_dtype=BF16)
        dup_sm, dgc_sm = ffn_gate_bwd_a(f"ffngate_ba_{i}", dhf_sm, sv['up_sm'], sv['gc_sm'])
        dgp_sm, cacc = ffn_gate_bwd_b(f"ffngate_bb_{i}", dgc_sm, sv['gp_sm'], W['ffn_conv_w'], i)
        kw = W['ffn_conv_w'].shape[2]
        G['ffn_conv_w'][i] = cacc[:, 0:kw, :]
        G['ffn_conv_b'][i] = cacc[:, 7, :].reshape(-1)
        gb.put('ffn_w_up', i, functools.partial(mm_tn_colsm, f"dup_{i}", sv['x1b'], dup_sm))
        gb.put('ffn_w_gate', i, functools.partial(mm_tn_colsm, f"dgate_{i}", sv['x1b'], dgp_sm))
        dx1 = mm_nt_rowsm(f"dx1a_{i}", dtg, W['ple_w_gate'], i, res=dr, res_alpha=DN_ALPHA)
        dx1 = mm_nt_colsm(f"dx1b_{i}", _sm_parts(dup_sm), W['ffn_w_up'], i, res=dx1)
        dx1 = mm_nt_colsm(f"dx1c_{i}", _sm_parts(dgp_sm), W['ffn_w_gate'], i, res=dx1)

        dr1, acc1 = ln_bwd(f"lnmix_b_{i}", dx1, sv['xh1'], sv['rs1'], W['ln_mix_g'], i)
        G['ln_mix_g'][i], G['ln_mix_b'][i] = acc1[0], acc1[1]
        xin = sv['xb']
        if i < N_A:
            G['a_pw2_b'][i] = acc1[2]
            gb.put('a_pw2_w', i, functools.partial(mm_tn_rowsm_fan, f"dpw2_{i}", sv['s'], dr1))
            ds = mm_nt_rowsm(f"ds_{i}", dr1, W['a_pw2_w'], i)
            dc, cacc = ln_silu_bwd(f"lnsilu_b_{i}", ds, sv['c'], W['a_ln_g'], W['a_ln_b'], i)
            G['a_ln_g'][i], G['a_ln_b'][i], G['a_dw_b'][i] = cacc[0], cacc[1], cacc[2]
            da, dg, dw, dba, dbg = conv_glu_bwd(f"convglu_b_{i}", dc, sv['u'], sv['h_sm'], W['a_dw_w'], i)
            kw = W['a_dw_w'].shape[1]
            G['a_dw_w'][i] = _from_sm(dw[:, 0:kw, :], axis=-1)
            G['a_pw1_b'][i] = jnp.concatenate([dba[:, 0, :], dbg[:, 0, :]], axis=0)
            half = da.shape[0]
            gb.put('a_pw1_w', i, functools.partial(mm_tn_colsm, f"dpw1a_{i}", xin, da))
            gb.put('a_pw1_w', i, functools.partial(mm_tn_colsm, f"dpw1g_{i}", xin, dg), j0=half)
            dx = mm_nt_colsm(f"dxa_{i}", _sm_parts(da) + _sm_parts(dg), W['a_pw1_w'], i, res=dr1, res_alpha=DN_ALPHA)
        else:
            j = i - N_A
            gb.put('b_wo', j, functools.partial(mm_tn_rowsm_fan, f"dwo_{j}", sv['o'], dr1))
            do = mm_nt_rowsm(f"do_{j}", dr1, W['b_wo'], j)
            dq, dk, dv = attn_bwd(f"attn_b_{j}", sv['qh'], kh, vh, do, sv['runs'], dk, dv)
            gb.put('b_wq', j, functools.partial(mm_tn_rowsm_fan, f"dwq_{j}", xin, dq))
            dx = mm_nt_rowsm(f"dxq_{j}", dq, W['b_wq'], j, res=dr1, res_alpha=DN_ALPHA)
            if j == 0:
                gb.put('kv_wk', 0, functools.partial(mm_tn_rowsm_fan, "dwk", xb_kv, dk))
                gb.put('kv_wv', 0, functools.partial(mm_tn_rowsm_fan, "dwv", xb_kv, dv))
                dx = mm_nt_rowsm("dxk", dk, W['kv_wk'], 0, res=dx)
                dx = mm_nt_rowsm("dxv", dv, W['kv_wv'], 0, res=dx)
    return loss_sum, dx, G, gb


MESH = pl.DeviceIdType.MESH
HBM = pl.BlockSpec(memory_space=pltpu.HBM)


def _place():
    x, y, c = lax.axis_index("x"), lax.axis_index("y"), lax.axis_index("c")
    others = [(1 - x, y), (x, 1 - y), (1 - x, 1 - y)]
    return x, y, c, others


def allgather_chips(name, arrs):
    n = len(arrs)

    def body(*refs):
        ins, outs = refs[:n], refs[n:2 * n]
        send_sems, recv_sems = refs[2 * n:]
        x, y, c, others = _place()
        me = 2 * x + y
        sibling = (x, y, 1 - c)
        ids = [2 * ch[0] + ch[1] for ch in others]
        from_id = jnp.where(c == 0, ids[0], ids[1])
        to_chip = (jnp.where(c == 0, x, 1 - x), jnp.where(c == 0, 1 - y, y))

        def remote(a, k, src, chip_id, half, to):
            return pltpu.make_async_remote_copy(src_ref=src, dst_ref=outs[a].at[chip_id, half], send_sem=send_sems.at[a, k],
                                                recv_sem=recv_sems.at[a, k], device_id=to, device_id_type=MESH)

        sent = [remote(a, k, ins[a].at[c], me, c, (others[k][0], others[k][1], c)) for a in range(n) for k in range(2)]
        for cp in sent:
            cp.start()
        for a in range(n):
            for k in range(2):
                remote(a, k, ins[a].at[c], ids[k], c, sibling).wait_recv()
            sent.append(remote(a, 2, outs[a].at[from_id, c], from_id, c, (to_chip[0], to_chip[1], c)))
            sent[-1].start()
            for k in range(2):
                sent.append(remote(a, 3 + k, outs[a].at[ids[k], c], ids[k], c, sibling))
                sent[-1].start()
        for a in range(n):
            remote(a, 2, ins[a].at[c], ids[2], c, sibling).wait_recv()
            sent.append(remote(a, 5, outs[a].at[ids[2], c], ids[2], c, sibling))
            sent[-1].start()
        for a in range(n):
            for k in range(3):
                remote(a, 3 + k, ins[a].at[c], ids[k], 1 - c, sibling).wait_recv()
        for cp in sent:
            cp.wait_send()

    outs = pl.pallas_call(body, out_shape=tuple(jax.ShapeDtypeStruct((N_CHIPS,) + a.shape, a.dtype) for a in arrs),
                          in_specs=[HBM] * n, out_specs=tuple([HBM] * n),
                          scratch_shapes=[pltpu.SemaphoreType.DMA((n, 6)), pltpu.SemaphoreType.DMA((n, 6))],
                          name=name)(*arrs)
    me = 2 * lax.axis_index("x") + lax.axis_index("y")
    return [lax.dynamic_update_index_in_dim(o, a, me, 0) for o, a in zip(outs, arrs)]


def exchange_sibling(name, gs):
    n = len(gs)

    def body(*refs):
        g_refs, o_refs = refs[:n], refs[n:2 * n]
        send_sems, recv_sems = refs[2 * n:]
        x, y, c, _ = _place()
        cps = [pltpu.make_async_remote_copy(src_ref=g_refs[a].at[j, 1 - c], dst_ref=o_refs[a].at[j], send_sem=send_sems.at[a, j],
                                            recv_sem=recv_sems.at[a, j], device_id=(x, y, 1 - c), device_id_type=MESH)
               for a in range(n) for j in range(N_CHIPS)]
        for cp in cps:
            cp.start()
        for cp in cps:
            cp.wait()

    return pl.pallas_call(body, out_shape=tuple(jax.ShapeDtypeStruct((N_CHIPS,) + g.shape[2:], g.dtype) for g in gs),
                          in_specs=[HBM] * n, out_specs=tuple([HBM] * n),
                          scratch_shapes=[pltpu.SemaphoreType.DMA((n, N_CHIPS)), pltpu.SemaphoreType.DMA((n, N_CHIPS))],
                          name=name)(*gs)


def scatter_chips(name, ss):
    n = len(ss)

    def body(*refs):
        s_refs, o_refs = refs[:n], refs[n:2 * n]
        send_sems, recv_sems = refs[2 * n:]
        x, y, c, others = _place()
        cps = [pltpu.make_async_remote_copy(src_ref=s_refs[a].at[2 * ch[0] + ch[1]], dst_ref=o_refs[a].at[k],
                                            send_sem=send_sems.at[a, k], recv_sem=recv_sems.at[a, k],
                                            device_id=(ch[0], ch[1], c), device_id_type=MESH)
               for a in range(n) for k, ch in enumerate(others)]
        for cp in cps:
            cp.start()
        for cp in cps:
            cp.wait()

    return pl.pallas_call(body, out_shape=tuple(jax.ShapeDtypeStruct((3,) + s.shape[1:], s.dtype) for s in ss),
                          in_specs=[HBM] * n, out_specs=tuple([HBM] * n),
                          scratch_shapes=[pltpu.SemaphoreType.DMA((n, 3)), pltpu.SemaphoreType.DMA((n, 3))], name=name)(*ss)


def share_sibling(name, ts):
    n = len(ts)

    def body(*refs):
        o_refs = refs[n:2 * n]
        send_sems, recv_sems = refs[2 * n:]
        x, y, c, _ = _place()
        cps = [pltpu.make_async_remote_copy(src_ref=o_refs[a].at[c], dst_ref=o_refs[a].at[c], send_sem=send_sems.at[a],
                                            recv_sem=recv_sems.at[a], device_id=(x, y, 1 - c), device_id_type=MESH)
               for a in range(n)]
        for cp in cps:
            cp.start()
        for a in range(n):
            pltpu.make_async_remote_copy(src_ref=o_refs[a].at[c], dst_ref=o_refs[a].at[1 - c], send_sem=send_sems.at[a],
                                         recv_sem=recv_sems.at[a], device_id=(x, y, 1 - c), device_id_type=MESH).wait_recv()
        for cp in cps:
            cp.wait_send()

    return pl.pallas_call(body, out_shape=tuple(jax.ShapeDtypeStruct(t.shape, t.dtype) for t in ts),
                          in_specs=[HBM] * n, out_specs=tuple([HBM] * n), input_output_aliases={a: a for a in range(n)},
                          scratch_shapes=[pltpu.SemaphoreType.DMA((n,)), pltpu.SemaphoreType.DMA((n,))],
                          name=name)(*ts)


def add_halves(name, g, recv, place, out_dtype, tr=512):
    _, _, R, C = g.shape
    tr = _tile(R, tr, 16)

    def body(p_ref, a_ref, b_ref, o_ref):
        o_ref[...] = (a_ref[...] + b_ref[...]).astype(o_ref.dtype)

    blk = pl.BlockSpec((None, tr, C), lambda j, i, p: (j, i, 0))
    gs = pltpu.PrefetchScalarGridSpec(num_scalar_prefetch=1, grid=(N_CHIPS, R // tr),
                                      in_specs=[pl.BlockSpec((None, None, tr, C), lambda j, i, p: (j, p[0], i, 0)), blk],
                                      out_specs=blk)
    return pl.pallas_call(body, grid_spec=gs, out_shape=jax.ShapeDtypeStruct((N_CHIPS, R, C), out_dtype),
                          compiler_params=_params(("parallel", "parallel")), name=name)(place, g, recv)


def add_chips(name, g, r1, r2, place, tr=512):
    _, _, R, C = g.shape
    tr = _tile(R, tr, 16)

    def body(p_ref, a_ref, b_ref, c_ref, o_ref):
        s = a_ref[...] + b_ref[...]
        o_ref[...] = ((s + c_ref[0].astype(F32)) + c_ref[1].astype(F32)) + c_ref[2].astype(F32)

    gs = pltpu.PrefetchScalarGridSpec(num_scalar_prefetch=1, grid=(R // tr,),
                                      in_specs=[pl.BlockSpec((None, None, tr, C), lambda i, p: (p[1], p[0], i, 0)),
                                                pl.BlockSpec((None, tr, C), lambda i, p: (p[1], i, 0)),
                                                pl.BlockSpec((3, tr, C), lambda i, p: (0, i, 0))],
                                      out_specs=pl.BlockSpec((None, tr, C), lambda i, p: (p[0], i, 0)))
    return pl.pallas_call(body, grid_spec=gs, out_shape=jax.ShapeDtypeStruct((2, R, C), F32),
                          compiler_params=_params(("parallel",)), name=name)(place, g, r1, r2)


def reduce_scatter(gs, wire_dtypes, place):
    r1 = exchange_sibling("rs_sibling", gs)
    s1 = [add_halves(f"rs_add_cores_{a}", g, r, place, dt) for a, (g, r, dt) in enumerate(zip(gs, r1, wire_dtypes))]
    r2 = scatter_chips("rs_chips", s1)
    tot = [add_chips(f"rs_add_chips_{a}", g, r, rr, place) for a, (g, r, rr) in enumerate(zip(gs, r1, r2))]
    return share_sibling("rs_share", tot)


def adamw(name, w, g, m, v, tr=512):
    shp = w.shape
    cols = shp[-1]
    w2, g2, m2, v2 = (a.reshape(-1, cols) for a in (w, g, m, v))
    rows = w2.shape[0]
    tr = _tile(rows, tr)

    def body(w_ref, g_ref, m_ref, v_ref, d_ref, mo_ref, vo_ref):
        g_ = g_ref[...]
        m_ = ADAM_B1 * m_ref[...] + (1.0 - ADAM_B1) * g_
        v_ = ADAM_B2 * v_ref[...] + (1.0 - ADAM_B2) * (g_ * g_)
        m_hat = m_ / (1.0 - ADAM_B1 ** ADAM_STEP)
        v_hat = v_ / (1.0 - ADAM_B2 ** ADAM_STEP)
        d_ref[...] = -ADAM_LR * (m_hat / (jnp.sqrt(v_hat) + ADAM_EPS) + ADAM_WD * w_ref[...])
        mo_ref[...] = m_
        vo_ref[...] = v_

    blk = pl.BlockSpec((tr, cols), lambda i: (i, 0))
    sds = jax.ShapeDtypeStruct((rows, cols), F32)
    d, mo, vo = pl.pallas_call(body, grid=(rows // tr,), in_specs=[blk] * 4, out_specs=(blk, blk, blk), out_shape=(sds, sds, sds),
                               compiler_params=_params(("parallel",)), name=name)(w2, g2, m2, v2)
    return d.reshape(shp), mo.reshape(shp), vo.reshape(shp)


PACK_ALIGN = 1024


def _pad_to(a, mult, axis=-1):
    axis = axis % a.ndim
    extra = (-a.shape[axis]) % mult
    if extra == 0:
        return a
    pads = [(0, 0)] * a.ndim
    pads[axis] = (0, extra)
    return jnp.pad(a, pads)


def _pack(pieces, lead, row_mult):
    nl = len(lead)
    flat, offs, sizes, off = [], [], [], 0
    for a in pieces:
        f = a.reshape(lead + (-1,))
        sizes.append(f.shape[-1])
        f = _pad_to(f, PACK_ALIGN)
        offs.append(off)
        off += f.shape[-1]
        flat.append(f)
    cat = _pad_to(jnp.concatenate(flat, axis=nl), 2 * row_mult * LANES)
    return cat.reshape(lead + (2, -1, LANES)), offs, sizes


def _unpack(packed, lead, offs, sizes, shapes):
    flat = packed.reshape(lead + (-1,))
    return [lax.slice_in_dim(flat, o, o + s, axis=len(lead)).reshape(lead + tuple(shp)) for o, s, shp in zip(offs, sizes, shapes)]


def _stack_grads(G, names):
    out = {}
    for n in names:
        parts = [g for g in G[n] if g is not None]
        if n in ('kv_wk', 'kv_wv'):
            out[n] = parts[0]
        elif n in REPLICATED:
            out[n] = jnp.stack(parts, axis=0).reshape(N_CHIPS, -1)
        elif n in ('a_dw_w', 'a_dw_b', 'a_ln_g', 'a_ln_b', 'a_pw2_b'):
            out[n] = _to_sm(jnp.stack(parts, axis=0), axis=-1)
        else:
            out[n] = jnp.stack(parts, axis=1)
    return out


def _whole_weights(big, small, rep, D):
    W = {}
    for n in BIG:
        a = big[n]
        W[n] = a[:, None] if n in ('kv_wk', 'kv_wv') else a
    W['a_pw1_b'] = small['a_pw1_b'][:, :, None, :]
    W['a_dw_w'] = _from_sm(small['a_dw_w'], axis=-1)
    for n in ('a_dw_b', 'a_ln_g', 'a_ln_b', 'a_pw2_b'):
        W[n] = _from_sm(small[n], axis=-1)[:, None, :]
    W['ffn_conv_w'] = small['ffn_conv_w']
    L, F = rep['ffn_conv_b'].shape
    W['ffn_conv_b'] = rep['ffn_conv_b'].reshape(L, N_CHIPS, 1, F // N_CHIPS)
    for n in ('ln_mix_g', 'ln_mix_b', 'ln_ffn_g', 'ln_ffn_b'):
        W[n] = rep[n][:, None, :]
    return W


SMALL = ('a_pw1_b', 'a_dw_w', 'a_dw_b', 'a_ln_g', 'a_ln_b', 'a_pw2_b', 'ffn_conv_w')


def _step(x, p, loss_target, w, m, v):
    S, D = x.shape[-2:]
    x2, tgt = x.reshape(S, D), loss_target.reshape(S, D)
    place = jnp.stack([lax.axis_index("c"), 2 * lax.axis_index("x") + lax.axis_index("y")]).astype(jnp.int32)

    big_in = [w[n].astype(BF16).reshape((2, -1) + w[n].shape[1:] if w[n].ndim == 3 else (2, -1, w[n].shape[-1])) for n in BIG]
    small_in, s_offs, s_sizes = _pack([w[n] for n in SMALL], (), 8)
    gathered = allgather_chips("gather_weights", big_in + [small_in])
    big = {n: g.reshape((N_CHIPS,) + w[n].shape) for n, g in zip(BIG, gathered[:-1])}
    small = dict(zip(SMALL, _unpack(gathered[-1], (N_CHIPS,), s_offs, s_sizes, [w[n].shape for n in SMALL])))
    W = _whole_weights(big, small, {n: w[n] for n in REPLICATED}, D)

    loss_sum, dx, G, gb = forward_backward(x2, p, tgt, W)
    loss = lax.psum(0.5 * loss_sum / D, ("x", "y", "c"))

    vectors = [n for n in WEIGHTS if n not in BIG]
    mats = [b.reshape(N_CHIPS, 2, b.shape[1] // 2, b.shape[2]) for b in gb.buf.values()]
    members = [gb.members[cols] for cols in gb.buf]
    g_sm = _stack_grads(G, vectors)
    packed, offs, sizes = _pack([g_sm[n] for n in vectors], (N_CHIPS,), 512)
    reduced = reduce_scatter(mats + [packed], [BF16] * len(mats) + [F32], place)
    shapes = [w[n].shape if n not in REPLICATED else (w[n].size // N_CHIPS,) for n in vectors]
    g_mine = dict(zip(vectors, _unpack(reduced[-1], (), offs, sizes, shapes)))
    for red, where in zip(reduced[:-1], members):
        rows = red.reshape(-1, red.shape[-1])
        for n, off, cnt in where:
            g_mine[n] = lax.slice_in_dim(rows, off, off + cnt, axis=0).reshape(w[n].shape)
    rep_in, r_offs, r_sizes = _pack([g_mine[n] for n in REPLICATED], (), 8)
    rep_all = allgather_chips("gather_replicated_grads", [rep_in])[0]
    for n, g in zip(REPLICATED, _unpack(rep_all, (N_CHIPS,), r_offs, r_sizes, [(w[n].size // N_CHIPS,) for n in REPLICATED])):
        g_mine[n] = g.reshape(w[n].shape)

    grads, deltas, new_m, new_v = [], [], [], []
    for n in WEIGHTS:
        d, mo, vo = adamw(f"adamw_{n}", w[n], g_mine[n], m[n], v[n])
        grads.append(g_mine[n])
        deltas.append(d)
        new_m.append(mo)
        new_v.append(vo)
    return (loss, dx.reshape(x.shape), *grads, *deltas, *new_m, *new_v)


def kernel(x, p, a_pw1_w, a_pw1_b, a_dw_w, a_dw_b, a_ln_g, a_ln_b, a_pw2_w, a_pw2_b, b_wq, kv_wk, kv_wv, b_wo, ln_mix_g, ln_mix_b, ffn_w_up, ffn_w_gate, ffn_conv_w, ffn_conv_b, ffn_w_down, ple_w_gate, ple_w_proj, ln_ffn_g, ln_ffn_b, loss_target, m_a_pw1_w, m_a_pw1_b, m_a_dw_w, m_a_dw_b, m_a_ln_g, m_a_ln_b, m_a_pw2_w, m_a_pw2_b, m_b_wq, m_kv_wk, m_kv_wv, m_b_wo, m_ln_mix_g, m_ln_mix_b, m_ffn_w_up, m_ffn_w_gate, m_ffn_conv_w, m_ffn_conv_b, m_ffn_w_down, m_ple_w_gate, m_ple_w_proj, m_ln_ffn_g, m_ln_ffn_b, v_a_pw1_w, v_a_pw1_b, v_a_dw_w, v_a_dw_b, v_a_ln_g, v_a_ln_b, v_a_pw2_w, v_a_pw2_b, v_b_wq, v_kv_wk, v_kv_wv, v_b_wo, v_ln_mix_g, v_ln_mix_b, v_ffn_w_up, v_ffn_w_gate, v_ffn_conv_w, v_ffn_conv_b, v_ffn_w_down, v_ple_w_gate, v_ple_w_proj, v_ln_ffn_g, v_ln_ffn_b):
    vals = dict(locals())
    w = {n: vals[n] for n in WEIGHTS}
    m = {n: vals["m_" + n] for n in WEIGHTS}
    v = {n: vals["v_" + n] for n in WEIGHTS}
    return _step(x, p, loss_target, w, m, v)
```

```python
import functools
import math

import jax
import jax.numpy as jnp
import numpy as np
from jax import lax
from jax.experimental import pallas as pl
from jax.experimental.pallas import tpu as pltpu

F32, BF16 = jnp.float32, jnp.bfloat16

HEAD_DIM = 64
LN_EPS = 1e-5
DEPTH = 4
N_A = DEPTH // 2
DN_ALPHA = (2.0 * DEPTH) ** 0.25
N_CHIPS = 4

ADAM_LR, ADAM_B1, ADAM_B2, ADAM_EPS, ADAM_WD, ADAM_STEP = 0.001, 0.9, 0.999, 1e-08, 0.01, 10

VMEM_LIMIT_BYTES = 56 * 2**20
LANES = 128
CONV_HALO = 32
FFN_HALO = 16

NN = (((1,), (0,)), ((), ()))
NT = (((1,), (1,)), ((), ()))
TN = (((0,), (0,)), ((), ()))

WEIGHTS = ['a_pw1_w', 'a_pw1_b', 'a_dw_w', 'a_dw_b', 'a_ln_g', 'a_ln_b', 'a_pw2_w', 'a_pw2_b', 'b_wq', 'kv_wk', 'kv_wv',
           'b_wo', 'ln_mix_g', 'ln_mix_b', 'ffn_w_up', 'ffn_w_gate', 'ffn_conv_w', 'ffn_conv_b', 'ffn_w_down', 'ple_w_gate',
           'ple_w_proj', 'ln_ffn_g', 'ln_ffn_b']
REPLICATED = ('ln_mix_g', 'ln_mix_b', 'ffn_conv_b', 'ln_ffn_g', 'ln_ffn_b')
BIG = ('a_pw1_w', 'a_pw2_w', 'b_wq', 'kv_wk', 'kv_wv', 'b_wo', 'ffn_w_up', 'ffn_w_gate', 'ffn_w_down', 'ple_w_gate',
       'ple_w_proj')


def _tile(n, pref, mult=8):
    t = min(n, pref)
    while t > 0:
        if n % t == 0 and t % mult == 0:
            return t
        t -= 1
    return n


def _params(sem):
    return pltpu.CompilerParams(dimension_semantics=sem, vmem_limit_bytes=VMEM_LIMIT_BYTES)


def _sigmoid(x):
    return 1.0 / (1.0 + jnp.exp(-x))


def _mm_call(name, grid, terms, out_spec, out_sds, dims, bias=None, res=None, res_alpha=1.0, out_scale=None, into=None):
    n_terms = len(terms)

    def body(*refs):
        o_ref = refs[-1]
        acc = None
        for t in range(n_terms):
            a = refs[2 * t][...].astype(BF16)
            b = refs[2 * t + 1][...].astype(BF16)
            d = lax.dot_general(a, b, dims, preferred_element_type=F32)
            acc = d if acc is None else acc + d
        k = 2 * n_terms
        if bias is not None:
            acc = acc + refs[k][...]
            k += 1
        if res is not None:
            acc = acc + res_alpha * refs[k][...]
        if out_scale is not None:
            acc = acc * out_scale
        o_ref[...] = acc.astype(o_ref.dtype)

    operands, specs = [], []
    for a, a_spec, b, b_spec in terms:
        operands += [a, b]
        specs += [a_spec, b_spec]
    for extra in (bias, res):
        if extra is not None:
            operands.append(extra[0])
            specs.append(extra[1])
    aliases = {}
    if into is not None:
        aliases = {len(operands): 0}
        operands.append(into)
        specs.append(pl.BlockSpec(memory_space=pl.ANY))
        out_sds = jax.ShapeDtypeStruct(into.shape, into.dtype)
    return pl.pallas_call(body, out_shape=out_sds, grid=grid, in_specs=specs, out_specs=out_spec, input_output_aliases=aliases,
                          compiler_params=_params(("parallel",) * len(grid)), name=name)(*operands)


def _mm_fanout(name, a, a_spec, w4, w_block, layer, dims, out_spec, out_sds, store, grid, bias4=None, res=None, res_alpha=1.0):
    def body(*refs):
        a_ref, w_refs, o_ref = refs[0], refs[1:1 + N_CHIPS], refs[-1]
        k = 1 + N_CHIPS
        b_refs = refs[k:k + N_CHIPS] if bias4 is not None else None
        k += N_CHIPS if bias4 is not None else 0
        av = a_ref[...].astype(BF16)
        for j in range(N_CHIPS):
            d = lax.dot_general(av, w_refs[j][...].astype(BF16), dims, preferred_element_type=F32)
            if b_refs is not None:
                d = d + b_refs[j][...]
            if res is not None:
                d = d + res_alpha * res[2](refs[k], j)
            store(o_ref, j, d)

    nd = len(grid)
    operands = [a] + [w4] * N_CHIPS
    specs = [a_spec] + [pl.BlockSpec((None, None) + w_block, lambda *g, j=j: (j, layer, 0, 0)) for j in range(N_CHIPS)]
    if bias4 is not None:
        operands += [bias4] * N_CHIPS
        specs += [pl.BlockSpec((None, None, 1, bias4.shape[-1]), lambda *g, j=j: (j, layer, 0, 0)) for j in range(N_CHIPS)]
    if res is not None:
        operands.append(res[0])
        specs.append(res[1])
    return pl.pallas_call(body, out_shape=out_sds, grid=grid, in_specs=specs, out_specs=out_spec,
                          compiler_params=_params(("parallel",) * nd), name=name)(*operands)


def _store_slot(o_ref, j, d):
    o_ref[j] = d.astype(o_ref.dtype)


def mm_colsm(name, x, w4, layer, bias4=None, out_dtype=F32, tm=1024):
    M, K = x.shape
    n = w4.shape[-1]
    tm = _tile(M, tm)
    return _mm_fanout(name, x, pl.BlockSpec((tm, K), lambda i: (i, 0)), w4, (K, n), layer, NN,
                      pl.BlockSpec((N_CHIPS, tm, n), lambda i: (0, i, 0)), jax.ShapeDtypeStruct((N_CHIPS, M, n), out_dtype),
                      _store_slot, (M // tm,), bias4=bias4)


def mm_rowsm(name, a, w4, layer, a_sm=False, bias=None, out_dtype=F32, out_scale=None, tm=512):
    kc, N = w4.shape[-2:]
    M = a.shape[-2]
    tm = _tile(M, tm)
    terms = []
    for j in range(N_CHIPS):
        if a_sm:
            a_spec = pl.BlockSpec((None, tm, kc), lambda i, j=j: (j, i, 0))
        else:
            a_spec = pl.BlockSpec((tm, kc), lambda i, j=j: (i, j))
        terms.append((a, a_spec, w4, pl.BlockSpec((None, None, kc, N), lambda i, j=j: (j, layer, 0, 0))))
    b = None if bias is None else (bias, pl.BlockSpec((None, 1, N), lambda i: (layer, 0, 0)))
    return _mm_call(name, (M // tm,), terms, pl.BlockSpec((tm, N), lambda i: (i, 0)), jax.ShapeDtypeStruct((M, N), out_dtype),
                    NN, bias=b, out_scale=out_scale)


def mm_nt_rowsm(name, dy, w4, layer, out_sm=False, res=None, res_alpha=1.0, out_dtype=F32, tm=1024):
    kc, N = w4.shape[-2:]
    M = dy.shape[0]
    tm = _tile(M, tm)

    def store_cols(o_ref, j, d):
        o_ref[:, j * kc:(j + 1) * kc] = d

    if out_sm:
        out_spec, sds, store = pl.BlockSpec((N_CHIPS, tm, kc), lambda i: (0, i, 0)), jax.ShapeDtypeStruct((N_CHIPS, M, kc), out_dtype), _store_slot
    else:
        out_spec, sds, store = pl.BlockSpec((tm, N_CHIPS * kc), lambda i: (i, 0)), jax.ShapeDtypeStruct((M, N_CHIPS * kc), F32), store_cols
    r = None if res is None else (res, pl.BlockSpec((tm, N_CHIPS * kc), lambda i: (i, 0)), lambda ref, j: ref[:, j * kc:(j + 1) * kc])
    return _mm_fanout(name, dy, pl.BlockSpec((tm, N), lambda i: (i, 0)), w4, (kc, N), layer, NT, out_spec, sds, store, (M // tm,),
                      res=r, res_alpha=res_alpha)


def mm_nt_colsm(name, dy_parts, w4, layer, res=None, res_alpha=1.0, tm=512):
    K, n = w4.shape[-2:]
    M = dy_parts[0][0].shape[1]
    tm = _tile(M, tm)
    terms = [(arr, pl.BlockSpec((None, tm, n), lambda i, idx=idx: (idx, i, 0)), w4,
              pl.BlockSpec((None, None, K, n), lambda i, j=j: (j, layer, 0, 0))) for j, (arr, idx) in enumerate(dy_parts)]
    r = None if res is None else (res, pl.BlockSpec((tm, K), lambda i: (i, 0)))
    return _mm_call(name, (M // tm,), terms, pl.BlockSpec((tm, K), lambda i: (i, 0)), jax.ShapeDtypeStruct((M, K), F32), NT,
                    res=r, res_alpha=res_alpha)


def _sm_parts(a):
    return [(a, j) for j in range(a.shape[0])]


def mm_tn_colsm(name, x, dy, into, off, j0=0, tk=512):
    M, K = x.shape
    nj, _, n = dy.shape
    tk = _tile(K, tk, LANES)
    assert off % tk == 0
    terms = [(x, pl.BlockSpec((M, tk), lambda j, k: (0, k)), dy, pl.BlockSpec((None, M, n), lambda j, k: (j, 0, 0)))]
    return _mm_call(name, (nj, K // tk), terms, pl.BlockSpec((None, tk, n), lambda j, k: (j + j0, off // tk + k, 0)),
                    None, TN, into=into)


def mm_tn_rowsm(name, a, dy, into, off, a_sm=False, tn=512):
    M, N = dy.shape
    kc = a.shape[-1] if a_sm else a.shape[-1] // N_CHIPS
    tn = _tile(N, tn, LANES)
    assert off % kc == 0
    a_spec = pl.BlockSpec((None, M, kc), lambda j, n: (j, 0, 0)) if a_sm else pl.BlockSpec((M, kc), lambda j, n: (0, j))
    terms = [(a, a_spec, dy, pl.BlockSpec((M, tn), lambda j, n: (0, n)))]
    return _mm_call(name, (N_CHIPS, N // tn), terms, pl.BlockSpec((None, kc, tn), lambda j, n: (j, off // kc, n)),
                    None, TN, into=into)


def mm_tn_rowsm_fan(name, a, dy, into, off, tn=256):
    M, N = dy.shape
    kc = a.shape[-1] // N_CHIPS
    tn = _tile(N, tn, LANES)
    assert off % kc == 0

    def body(a_ref, dy_ref, into_ref, o_ref):
        dyb = dy_ref[...].astype(BF16)
        for j in range(N_CHIPS):
            aj = a_ref[:, j * kc:(j + 1) * kc].astype(BF16)
            o_ref[j] = lax.dot_general(aj, dyb, TN, preferred_element_type=F32)

    return pl.pallas_call(body, out_shape=jax.ShapeDtypeStruct(into.shape, into.dtype), grid=(N // tn,),
                          in_specs=[pl.BlockSpec((M, N_CHIPS * kc), lambda n: (0, 0)), pl.BlockSpec((M, tn), lambda n: (0, n)),
                                    pl.BlockSpec(memory_space=pl.ANY)],
                          out_specs=pl.BlockSpec((N_CHIPS, kc, tn), lambda n: (0, off // kc, n)), input_output_aliases={2: 0},
                          compiler_params=_params(("parallel",)), name=name)(a, dy, into)


def mm_proj(name, p4, layer, w4, tm=512):
    S, P = p4.shape[-2:]
    n = w4.shape[-1]
    tm = _tile(S, tm)

    def store_cols(o_ref, j, d):
        o_ref[:, j * n:(j + 1) * n] = d

    return _mm_fanout(name, p4, pl.BlockSpec((None, None, tm, P), lambda i: (layer, 0, i, 0)), w4, (P, n), layer, NN,
                      pl.BlockSpec((tm, N_CHIPS * n), lambda i: (i, 0)), jax.ShapeDtypeStruct((S, N_CHIPS * n), F32),
                      store_cols, (S // tm,))


def mm_tn_proj(name, p4, layer, dpp, into, off):
    S, P = p4.shape[-2:]
    n = dpp.shape[-1] // N_CHIPS
    assert off % P == 0
    terms = [(p4, pl.BlockSpec((None, None, S, P), lambda j: (layer, 0, 0, 0)), dpp, pl.BlockSpec((S, n), lambda j: (0, j)))]
    return _mm_call(name, (N_CHIPS,), terms, pl.BlockSpec((None, P, n), lambda j: (j, off // P, 0)), None, TN, into=into)


def ln_fwd(name, x, mix, g, b, layer, tg=None, pp=None, tm=512):
    S, D = x.shape
    tm = _tile(S, tm, 16)
    ple = tg is not None

    def body(*refs):
        if ple:
            x_ref, m_ref, tg_ref, pp_ref, g_ref, b_ref, y_ref, yb_ref, xh_ref, rs_ref = refs
        else:
            x_ref, m_ref, g_ref, b_ref, y_ref, yb_ref, xh_ref, rs_ref = refs
        r = DN_ALPHA * x_ref[...] + m_ref[...]
        if ple:
            r = r + _sigmoid(tg_ref[...]) * pp_ref[...]
        mu = jnp.mean(r, axis=-1, keepdims=True)
        d = r - mu
        var = jnp.mean(d * d, axis=-1, keepdims=True)
        rstd = lax.rsqrt(var + LN_EPS)
        xh = d * rstd
        y = xh * g_ref[...] + b_ref[...]
        y_ref[...] = y
        yb_ref[...] = y.astype(BF16)
        xh_ref[...] = xh
        rs_ref[...] = rstd

    row = pl.BlockSpec((tm, D), lambda i: (i, 0))
    vec = pl.BlockSpec((None, 1, D), lambda i: (layer, 0, 0))
    ins = [x, mix] + ([tg, pp] if ple else []) + [g, b]
    specs = [row, row] + ([row, row] if ple else []) + [vec, vec]
    return pl.pallas_call(body, grid=(S // tm,), in_specs=specs,
                          out_specs=(row, row, row, pl.BlockSpec((tm, 1), lambda i: (i, 0))),
                          out_shape=(jax.ShapeDtypeStruct((S, D), F32), jax.ShapeDtypeStruct((S, D), BF16),
                                     jax.ShapeDtypeStruct((S, D), F32), jax.ShapeDtypeStruct((S, 1), F32)),
                          compiler_params=_params(("parallel",)), name=name)(*ins)


def ln_bwd(name, dy, xh, rstd, g, layer, tg=None, pp=None, tm=512):
    S, D = dy.shape
    tm = _tile(S, tm)
    ple = tg is not None

    def body(*refs):
        if ple:
            dy_ref, xh_ref, rs_ref, g_ref, tg_ref, pp_ref, dr_ref, acc_ref, dtg_ref, dpp_ref = refs
        else:
            dy_ref, xh_ref, rs_ref, g_ref, dr_ref, acc_ref = refs
        dy_, xh_ = dy_ref[...], xh_ref[...]
        dxh = dy_ * g_ref[...]
        m1 = jnp.mean(dxh, axis=-1, keepdims=True)
        m2 = jnp.mean(dxh * xh_, axis=-1, keepdims=True)
        dr = rs_ref[...] * (dxh - m1 - xh_ * m2)
        dr_ref[...] = dr

        @pl.when(pl.program_id(0) == 0)
        def _():
            acc_ref[...] = jnp.zeros_like(acc_ref)
        acc_ref[0:1, :] += jnp.sum(dy_ * xh_, axis=0, keepdims=True)
        acc_ref[1:2, :] += jnp.sum(dy_, axis=0, keepdims=True)
        acc_ref[2:3, :] += jnp.sum(dr, axis=0, keepdims=True)
        if ple:
            pg = _sigmoid(tg_ref[...])
            dtg_ref[...] = (dr * pp_ref[...] * pg * (1.0 - pg)).astype(BF16)
            dpp_ref[...] = (dr * pg).astype(BF16)

    row = pl.BlockSpec((tm, D), lambda i: (i, 0))
    ins = [dy, xh, rstd, g] + ([tg, pp] if ple else [])
    specs = [row, row, pl.BlockSpec((tm, 1), lambda i: (i, 0)), pl.BlockSpec((None, 1, D), lambda i: (layer, 0, 0))] + ([row, row] if ple else [])
    outs = [jax.ShapeDtypeStruct((S, D), F32), jax.ShapeDtypeStruct((8, D), F32)]
    out_specs = [row, pl.BlockSpec((8, D), lambda i: (0, 0))]
    if ple:
        outs += [jax.ShapeDtypeStruct((S, D), BF16)] * 2
        out_specs += [row, row]
    return pl.pallas_call(body, grid=(S // tm,), in_specs=specs, out_specs=tuple(out_specs), out_shape=tuple(outs),
                          compiler_params=_params(("arbitrary",)), name=name)(*ins)


def glu_fwd(name, h_sm, tm=512):
    _, S, n = h_sm.shape
    tm = _tile(S, tm)
    half = N_CHIPS // 2

    def body(a_ref, g_ref, u_ref):
        u_ref[...] = a_ref[...] * _sigmoid(g_ref[...])

    return pl.pallas_call(body, grid=(half, S // tm),
                          in_specs=[pl.BlockSpec((None, tm, n), lambda j, i: (j, i, 0)),
                                    pl.BlockSpec((None, tm, n), lambda j, i: (j + half, i, 0))],
                          out_specs=pl.BlockSpec((tm, n), lambda j, i: (i, j)),
                          out_shape=jax.ShapeDtypeStruct((S, half * n), F32),
                          compiler_params=_params(("parallel", "parallel")), name=name)(h_sm, h_sm)


def conv_ln_silu_fwd(name, u, w, b, g, beta, layer, ts=128):
    S, D = u.shape
    kw = w.shape[1]
    ts = _tile(S, ts, CONV_HALO)
    lc = LANES if D % LANES == 0 else D

    def body(h_ref, u_ref, w_ref, b_ref, g_ref, be_ref, c_ref, s_ref, win_ref):
        i = pl.program_id(0)
        win_ref[0:CONV_HALO, :] = jnp.where(i == 0, 0.0, h_ref[...])
        win_ref[CONV_HALO:, :] = u_ref[...]
        for cc in range(D // lc):
            cs = slice(cc * lc, (cc + 1) * lc)
            acc = jnp.zeros((ts, lc), F32) + b_ref[:, cs]
            for k in range(kw):
                off = CONV_HALO - (kw - 1) + k
                acc = acc + w_ref[k:k + 1, cs] * win_ref[off:off + ts, cs]
            c_ref[:, cs] = acc
        c = c_ref[...]
        mu = jnp.mean(c, axis=-1, keepdims=True)
        d = c - mu
        var = jnp.mean(d * d, axis=-1, keepdims=True)
        nrm = d * lax.rsqrt(var + LN_EPS) * g_ref[...] + be_ref[...]
        s_ref[...] = (nrm * _sigmoid(nrm)).astype(BF16)

    row = pl.BlockSpec((ts, D), lambda i: (i, 0))
    vec = pl.BlockSpec((None, 1, D), lambda i: (layer, 0, 0))
    halo = pl.BlockSpec((CONV_HALO, D), lambda i: (jnp.maximum(i * (ts // CONV_HALO) - 1, 0), 0))
    return pl.pallas_call(body, grid=(S // ts,),
                          in_specs=[halo, row, pl.BlockSpec((None, kw, D), lambda i: (layer, 0, 0)), vec, vec, vec],
                          out_specs=(row, row),
                          out_shape=(jax.ShapeDtypeStruct((S, D), F32), jax.ShapeDtypeStruct((S, D), BF16)),
                          scratch_shapes=[pltpu.VMEM((ts + CONV_HALO, D), F32)],
                          compiler_params=_params(("parallel",)), name=name)(u, u, w, b, g, beta)


def ln_silu_bwd(name, ds, c, g, beta, layer, tm=256):
    S, D = c.shape
    tm = _tile(S, tm)

    def body(ds_ref, c_ref, g_ref, be_ref, dc_ref, acc_ref):
        c_ = c_ref[...]
        mu = jnp.mean(c_, axis=-1, keepdims=True)
        d = c_ - mu
        var = jnp.mean(d * d, axis=-1, keepdims=True)
        rstd = lax.rsqrt(var + LN_EPS)
        xh = d * rstd
        nrm = xh * g_ref[...] + be_ref[...]
        sg = _sigmoid(nrm)
        dn = ds_ref[...] * (sg * (1.0 + nrm * (1.0 - sg)))
        dxh = dn * g_ref[...]
        m1 = jnp.mean(dxh, axis=-1, keepdims=True)
        m2 = jnp.mean(dxh * xh, axis=-1, keepdims=True)
        dc = rstd * (dxh - m1 - xh * m2)
        dc_ref[...] = dc

        @pl.when(pl.program_id(0) == 0)
        def _():
            acc_ref[...] = jnp.zeros_like(acc_ref)
        acc_ref[0:1, :] += jnp.sum(dn * xh, axis=0, keepdims=True)
        acc_ref[1:2, :] += jnp.sum(dn, axis=0, keepdims=True)
        acc_ref[2:3, :] += jnp.sum(dc, axis=0, keepdims=True)

    row = pl.BlockSpec((tm, D), lambda i: (i, 0))
    vec = pl.BlockSpec((None, 1, D), lambda i: (layer, 0, 0))
    return pl.pallas_call(body, grid=(S // tm,), in_specs=[row, row, vec, vec],
                          out_specs=(row, pl.BlockSpec((8, D), lambda i: (0, 0))),
                          out_shape=(jax.ShapeDtypeStruct((S, D), F32), jax.ShapeDtypeStruct((8, D), F32)),
                          compiler_params=_params(("arbitrary",)), name=name)(ds, c, g, beta)


def conv_glu_bwd(name, dc, u, h_sm, w, layer, ts=128):
    S, D = dc.shape
    kw = w.shape[1]
    half = N_CHIPS // 2
    n = D // half
    ts = _tile(S, ts, CONV_HALO)
    nblk = S // ts
    lc = LANES if n % LANES == 0 else n

    def body(dc_ref, dcn_ref, u_ref, a_ref, g_ref, w_ref, da_ref, dg_ref, dw_ref, dba_ref, dbg_ref, dwin_ref):
        i = pl.program_id(1)
        dwin_ref[0:ts, :] = dc_ref[...]
        dwin_ref[ts:, :] = jnp.where(i == nblk - 1, 0.0, dcn_ref[...])

        @pl.when(i == 0)
        def _():
            dw_ref[...] = jnp.zeros_like(dw_ref)
            dba_ref[...] = jnp.zeros_like(dba_ref)
            dbg_ref[...] = jnp.zeros_like(dbg_ref)

        for cc in range(n // lc):
            cs = slice(cc * lc, (cc + 1) * lc)
            ub = u_ref[:, cs]
            du = jnp.zeros((ts, lc), F32)
            for k in range(kw):
                shifted = dwin_ref[kw - 1 - k:kw - 1 - k + ts, cs]
                du = du + w_ref[k:k + 1, cs] * shifted
                dw_ref[k:k + 1, cs] += jnp.sum(shifted * ub, axis=0, keepdims=True)
            a = a_ref[:, cs]
            sg = _sigmoid(g_ref[:, cs])
            da = du * sg
            dg = du * a * sg * (1.0 - sg)
            da_ref[:, cs] = da.astype(BF16)
            dg_ref[:, cs] = dg.astype(BF16)
            dba_ref[0:1, cs] += jnp.sum(da, axis=0, keepdims=True)
            dbg_ref[0:1, cs] += jnp.sum(dg, axis=0, keepdims=True)

    r = ts // CONV_HALO
    main = pl.BlockSpec((ts, n), lambda j, i: (i, j))
    nxt = pl.BlockSpec((CONV_HALO, n), lambda j, i: (jnp.minimum((i + 1) * r, S // CONV_HALO - 1), j))
    sm_a = pl.BlockSpec((None, ts, n), lambda j, i: (j, i, 0))
    sm_g = pl.BlockSpec((None, ts, n), lambda j, i: (j + half, i, 0))
    da, dg, dw, dba, dbg = pl.pallas_call(
        body, grid=(half, nblk),
        in_specs=[main, nxt, main, sm_a, sm_g, pl.BlockSpec((None, kw, n), lambda j, i: (layer, 0, j))],
        out_specs=(pl.BlockSpec((None, ts, n), lambda j, i: (j, i, 0)), pl.BlockSpec((None, ts, n), lambda j, i: (j, i, 0)),
                   pl.BlockSpec((None, 32, n), lambda j, i: (j, 0, 0)),
                   pl.BlockSpec((None, 8, n), lambda j, i: (j, 0, 0)), pl.BlockSpec((None, 8, n), lambda j, i: (j, 0, 0))),
        out_shape=(jax.ShapeDtypeStruct((half, S, n), BF16), jax.ShapeDtypeStruct((half, S, n), BF16),
                   jax.ShapeDtypeStruct((half, 32, n), F32),
                   jax.ShapeDtypeStruct((half, 8, n), F32), jax.ShapeDtypeStruct((half, 8, n), F32)),
        scratch_shapes=[pltpu.VMEM((ts + CONV_HALO, n), F32)],
        compiler_params=_params(("parallel", "arbitrary")), name=name)(dc, dc, u, h_sm, h_sm, w)
    return da, dg, dw, dba, dbg


ROW_CHUNK = 16


def _ffn_gc(win_ref, w_ref, b_ref, r0, rows, kw, base):
    gc = b_ref[...] + jnp.zeros((rows, win_ref.shape[1]), F32)
    for k in range(kw):
        off = r0 + base - (kw - 1) + k
        gc = gc + w_ref[k:k + 1, :] * win_ref[off:off + rows, :]
    return gc


def ffn_gate_fwd(name, up_sm, gp_sm, w4, b4, layer, ts=256):
    _, S, n = up_sm.shape
    kw = w4.shape[2]
    ts = _tile(S, ts, ROW_CHUNK)
    rc = ROW_CHUNK

    def body(up_ref, gp_ref, gph_ref, w_ref, b_ref, hf_ref, gc_ref, win_ref):
        i = pl.program_id(1)
        win_ref[0:FFN_HALO, :] = jnp.where(i == 0, 0.0, gph_ref[...].astype(F32))
        win_ref[FFN_HALO:, :] = gp_ref[...].astype(F32)
        for r0 in range(0, ts, rc):
            gc = _ffn_gc(win_ref, w_ref, b_ref, r0, rc, kw, FFN_HALO)
            gc_ref[r0:r0 + rc, :] = gc.astype(BF16)
            hf_ref[r0:r0 + rc, :] = (gc * _sigmoid(gc) * up_ref[r0:r0 + rc, :].astype(F32)).astype(BF16)

    main = pl.BlockSpec((None, ts, n), lambda j, i: (j, i, 0))
    prv = pl.BlockSpec((None, FFN_HALO, n), lambda j, i: (j, jnp.maximum(i * (ts // FFN_HALO) - 1, 0), 0))
    sds = jax.ShapeDtypeStruct((N_CHIPS, S, n), BF16)
    return pl.pallas_call(body, grid=(N_CHIPS, S // ts),
                          in_specs=[main, main, prv, pl.BlockSpec((None, None, kw, n), lambda j, i: (j, layer, 0, 0)),
                                    pl.BlockSpec((None, None, 1, n), lambda j, i: (layer, j, 0, 0))],
                          out_specs=(main, main), out_shape=(sds, sds),
                          scratch_shapes=[pltpu.VMEM((ts + FFN_HALO, n), F32)],
                          compiler_params=_params(("parallel", "parallel")), name=name)(up_sm, gp_sm, gp_sm, w4, b4)


def ffn_gate_bwd_a(name, dhf_sm, up_sm, gc_sm, ts=256):
    _, S, n = up_sm.shape
    ts = _tile(S, ts, ROW_CHUNK)
    rc = ROW_CHUNK

    def body(dhf_ref, up_ref, gc_ref, dup_ref, dgc_ref):
        for r0 in range(0, ts, rc):
            rows = slice(r0, r0 + rc)
            gc = gc_ref[rows, :].astype(F32)
            sg = _sigmoid(gc)
            dhf = dhf_ref[rows, :].astype(F32)
            dup_ref[rows, :] = (dhf * gc * sg).astype(BF16)
            dgc_ref[rows, :] = (dhf * up_ref[rows, :].astype(F32) * (sg * (1.0 + gc * (1.0 - sg)))).astype(BF16)

    main = pl.BlockSpec((None, ts, n), lambda j, i: (j, i, 0))
    sds = jax.ShapeDtypeStruct((N_CHIPS, S, n), BF16)
    return pl.pallas_call(body, grid=(N_CHIPS, S // ts), in_specs=[main, main, main], out_specs=(main, main), out_shape=(sds, sds),
                          compiler_params=_params(("parallel", "parallel")), name=name)(dhf_sm, up_sm, gc_sm)


def ffn_gate_bwd_b(name, dgc_sm, gp_sm, w4, layer, ts=256):
    _, S, n = gp_sm.shape
    kw = w4.shape[2]
    ts = _tile(S, ts, ROW_CHUNK)
    rc = ROW_CHUNK
    nblk = S // ts

    def body(dgc_ref, dgn_ref, gp_ref, w_ref, dgp_ref, acc_ref, dwin_ref):
        i = pl.program_id(1)
        dwin_ref[0:ts, :] = dgc_ref[...].astype(F32)
        dwin_ref[ts:, :] = jnp.where(i == nblk - 1, 0.0, dgn_ref[...].astype(F32))

        @pl.when(i == 0)
        def _():
            acc_ref[...] = jnp.zeros_like(acc_ref)
        sums = [jnp.zeros((8, n), F32) for _ in range(kw + 1)]
        for r0 in range(0, ts, rc):
            gp = gp_ref[r0:r0 + rc, :].astype(F32)
            dgp = jnp.zeros((rc, n), F32)
            for k in range(kw):
                d = kw - 1 - k
                shifted = dwin_ref[r0 + d:r0 + d + rc, :]
                dgp = dgp + w_ref[k:k + 1, :] * shifted
                prod = shifted * gp
                sums[k] = sums[k] + prod[0:8, :] + prod[8:16, :]
                if d == 0:
                    sums[kw] = sums[kw] + shifted[0:8, :] + shifted[8:16, :]
            dgp_ref[r0:r0 + rc, :] = dgp.astype(BF16)
        for k in range(kw):
            acc_ref[k:k + 1, :] += jnp.sum(sums[k], axis=0, keepdims=True)
        acc_ref[7:8, :] += jnp.sum(sums[kw], axis=0, keepdims=True)

    r = ts // FFN_HALO
    main = pl.BlockSpec((None, ts, n), lambda j, i: (j, i, 0))
    nxt = pl.BlockSpec((None, FFN_HALO, n), lambda j, i: (j, jnp.minimum((i + 1) * r, S // FFN_HALO - 1), 0))
    return pl.pallas_call(body, grid=(N_CHIPS, nblk),
                          in_specs=[main, nxt, main, pl.BlockSpec((None, None, kw, n), lambda j, i: (j, layer, 0, 0))],
                          out_specs=(main, pl.BlockSpec((None, 8, n), lambda j, i: (j, 0, 0))),
                          out_shape=(jax.ShapeDtypeStruct((N_CHIPS, S, n), BF16), jax.ShapeDtypeStruct((N_CHIPS, 8, n), F32)),
                          scratch_shapes=[pltpu.VMEM((ts + FFN_HALO, n), F32)],
                          compiler_params=_params(("parallel", "arbitrary")), name=name)(dgc_sm, dgc_sm, gp_sm, w4)


def _neg_softplus(z):
    e = jnp.exp(-jnp.abs(z))
    return -(jnp.maximum(z, 0.0) + jnp.log(1.0 + e)), e


def _split_dot(x, t):
    hi = x.astype(BF16)
    lo = (x - hi.astype(F32)).astype(BF16)
    return jnp.dot(hi, t, preferred_element_type=F32) + jnp.dot(lo, t, preferred_element_type=F32)


STICK_GONE = -100.0
NOT_SWEPT = -1e30


def attn_fwd(name, q, k, v, bq=512, w=256):
    S, D = q.shape
    dh = HEAD_DIM
    hpb = LANES // dh
    bq = _tile(S, bq)
    w = _tile(bq, w)
    nsub = bq // w
    nkb = S // w

    def body(q_ref, k_ref, v_ref, o_ref, runs_ref, rs_ref):
        qi = pl.program_id(1)
        rr = lax.broadcasted_iota(jnp.int32, (w, w), 0)
        cc = lax.broadcasted_iota(jnp.int32, (w, w), 1)
        t_suf = (rr >= cc).astype(BF16)
        tq = qi * bq + lax.broadcasted_iota(jnp.int32, (bq, w), 0)
        tk = lax.broadcasted_iota(jnp.int32, (bq, w), 1)
        lane = lax.broadcasted_iota(jnp.int32, (bq, LANES), 1)
        ntot = (qi + 1) * nsub
        heads = [slice(hh * dh, (hh + 1) * dh) for hh in range(hpb)]
        qbs = [q_ref[:, hs] for hs in heads]
        for hh in range(hpb):
            rs_ref[hh] = jnp.where(lane < ntot, NOT_SWEPT, 0.0)

        def block(kb, carry, masked):
            kstart = pl.multiple_of(kb * w, w)
            if masked:
                m = (tk + kstart) < tq
            out = []
            for hh, hs in enumerate(heads):
                run, acc = carry[2 * hh], carry[2 * hh + 1]
                kblk = k_ref[pl.ds(kstart, w), hs]
                vblk = v_ref[pl.ds(kstart, w), hs]
                z = lax.dot_general(qbs[hh], kblk, NT, preferred_element_type=F32)
                lg, _ = _neg_softplus(z)
                if masked:
                    lg = jnp.where(m, lg, 0.0)
                cum = _split_dot(lg, t_suf) + run
                a = jnp.exp(z + cum)
                if masked:
                    a = jnp.where(m, a, 0.0)
                acc = acc + jnp.dot(a.astype(BF16), vblk, preferred_element_type=F32)
                run = cum[:, 0:1]
                rs_ref[hh] = jnp.where(lane == kb, run, rs_ref[hh])
                out += [run, acc]
            return tuple(out)

        carry = (jnp.zeros((bq, 1), F32), jnp.zeros((bq, dh), F32)) * hpb
        for sb in reversed(range(nsub)):
            carry = block(qi * nsub + sb, carry, True)

        def cond(c):
            alive = functools.reduce(jnp.maximum, [jnp.max(c[1 + 2 * hh]) for hh in range(hpb)])
            return jnp.logical_and(c[0] >= 0, alive > STICK_GONE)

        def step(c):
            return (c[0] - 1,) + block(c[0], c[1:], False)
        carry = lax.while_loop(cond, step, (qi * nsub - 1,) + carry)[1:]
        for hh, hs in enumerate(heads):
            o_ref[:, hs] = carry[2 * hh + 1].astype(o_ref.dtype)
            runs_ref[hh] = rs_ref[hh, :, 0:nkb]

    qs = pl.BlockSpec((bq, LANES), lambda h, i: (i, h))
    kv = pl.BlockSpec((S, LANES), lambda h, i: (0, h))
    return pl.pallas_call(body, grid=(D // LANES, S // bq), in_specs=[qs, kv, kv],
                          out_specs=(qs, pl.BlockSpec((hpb, bq, nkb), lambda h, i: (h, i, 0))),
                          out_shape=(jax.ShapeDtypeStruct((S, D), BF16), jax.ShapeDtypeStruct((D // dh, S, nkb), F32)),
                          scratch_shapes=[pltpu.VMEM((hpb, bq, LANES), F32)],
                          compiler_params=_params(("parallel", "parallel")), name=name)(q, k, v)


def attn_bwd(name, q, k, v, do, runs, dk0=None, dv0=None, bq=512, w=256):
    S, D = q.shape
    dh = HEAD_DIM
    hpb = LANES // dh
    bq = _tile(S, bq)
    w = _tile(bq, w)
    nsub = bq // w
    nkb = S // w
    scale = 1.0 / math.sqrt(dh)
    init = dk0 is not None

    def body(*refs):
        if init:
            q_ref, k_ref, v_ref, do_ref, runs_ref, dk0_ref, dv0_ref, dq_ref, dk_ref, dv_ref, rs_ref = refs
        else:
            q_ref, k_ref, v_ref, do_ref, runs_ref, dq_ref, dk_ref, dv_ref, rs_ref = refs
        qi = pl.program_id(1)

        @pl.when(qi == 0)
        def _():
            dk_ref[...] = dk0_ref[...] if init else jnp.zeros_like(dk_ref)
            dv_ref[...] = dv0_ref[...] if init else jnp.zeros_like(dv_ref)

        rr = lax.broadcasted_iota(jnp.int32, (w, w), 0)
        cc = lax.broadcasted_iota(jnp.int32, (w, w), 1)
        t_suf = (rr >= cc).astype(BF16)
        t_pre = (rr <= cc).astype(BF16)
        tq = qi * bq + lax.broadcasted_iota(jnp.int32, (bq, w), 0)
        tk = lax.broadcasted_iota(jnp.int32, (bq, w), 1)
        lane = lax.broadcasted_iota(jnp.int32, (bq, LANES), 1)
        lane1 = lax.broadcasted_iota(jnp.int32, (1, LANES), 1)
        ntot = (qi + 1) * nsub
        heads = [slice(hh * dh, (hh + 1) * dh) for hh in range(hpb)]
        qbs = [q_ref[:, hs] for hs in heads]
        dobs = [do_ref[:, hs].astype(BF16) for hs in heads]
        kb0 = ntot - nsub
        for hh in range(hpb):
            rs_ref[hh] = jnp.zeros((bq, LANES), F32)
            rs_ref[hh, :, 0:nkb] = runs_ref[hh]
            colmax = jnp.max(rs_ref[hh], axis=0, keepdims=True)
            dead = jnp.logical_and(jnp.logical_and(lane1 >= 1, lane1 <= ntot), colmax <= STICK_GONE)
            kb0 = jnp.minimum(kb0, jnp.sum(dead.astype(jnp.int32)))

        def block(kb, carry, masked):
            kstart = pl.multiple_of(kb * w, w)
            if masked:
                m = (tk + kstart) < tq
            out = []
            for hh, hs in enumerate(heads):
                pg_run, dq = carry[2 * hh], carry[2 * hh + 1]
                qb, dob = qbs[hh], dobs[hh]
                kblk = k_ref[pl.ds(kstart, w), hs]
                vblk = v_ref[pl.ds(kstart, w), hs]
                right = jnp.sum(jnp.where(lane == kb + 1, rs_ref[hh], 0.0), axis=1, keepdims=True)
                z = lax.dot_general(qb, kblk, NT, preferred_element_type=F32)
                lg, e = _neg_softplus(z)
                if masked:
                    lg = jnp.where(m, lg, 0.0)
                a = jnp.exp(z + _split_dot(lg, t_suf) + right)
                if masked:
                    a = jnp.where(m, a, 0.0)
                da = lax.dot_general(dob, vblk, NT, preferred_element_type=F32)
                g = da * a
                pin = _split_dot(g, t_pre) + pg_run
                sig = jnp.where(z >= 0.0, 1.0, e) / (1.0 + e)
                dz = g - sig * pin
                if masked:
                    dz = jnp.where(m, dz, 0.0)
                dzb = dz.astype(BF16)
                dq = dq + jnp.dot(dzb, kblk, preferred_element_type=F32)
                dk_ref[pl.ds(kstart, w), hs] += lax.dot_general(dzb, qb, TN, preferred_element_type=F32)
                dv_ref[pl.ds(kstart, w), hs] += lax.dot_general(a.astype(BF16), dob, TN, preferred_element_type=F32)
                out += [pin[:, w - 1:w], dq]
            return tuple(out)

        carry = (jnp.zeros((bq, 1), F32), jnp.zeros((bq, dh), F32)) * hpb
        carry = lax.fori_loop(kb0, qi * nsub, lambda kb, c: block(kb, c, False), carry)
        for sb in range(nsub):
            carry = block(qi * nsub + sb, carry, True)
        for hh, hs in enumerate(heads):
            dq_ref[:, hs] = carry[2 * hh + 1] * scale

    qs = pl.BlockSpec((bq, LANES), lambda h, i: (i, h))
    kv = pl.BlockSpec((S, LANES), lambda h, i: (0, h))
    ins = [q, k, v, do, runs] + ([dk0, dv0] if init else [])
    specs = [qs, kv, kv, qs, pl.BlockSpec((hpb, bq, nkb), lambda h, i: (h, i, 0))] + ([kv, kv] if init else [])
    sds = jax.ShapeDtypeStruct((S, D), F32)
    return pl.pallas_call(body, grid=(D // LANES, S // bq), in_specs=specs, out_specs=(qs, kv, kv), out_shape=(sds, sds, sds),
                          scratch_shapes=[pltpu.VMEM((hpb, bq, LANES), F32)],
                          compiler_params=_params(("parallel", "arbitrary")), name=name)(*ins)


def loss_head(name, y, tgt, tm=512):
    S, D = y.shape
    tm = _tile(S, tm)

    def body(y_ref, t_ref, dy_ref, acc_ref):
        @pl.when(pl.program_id(0) == 0)
        def _():
            acc_ref[...] = jnp.zeros_like(acc_ref)
        e = y_ref[...] - t_ref[...]
        dy_ref[...] = e * (1.0 / D)
        acc_ref[...] += jnp.sum(e * e)

    row = pl.BlockSpec((tm, D), lambda i: (i, 0))
    return pl.pallas_call(body, grid=(S // tm,), in_specs=[row, row],
                          out_specs=(row, pl.BlockSpec((8, LANES), lambda i: (0, 0))),
                          out_shape=(jax.ShapeDtypeStruct((S, D), F32), jax.ShapeDtypeStruct((8, LANES), F32)),
                          compiler_params=_params(("arbitrary",)), name=name)(y, tgt)


def _to_sm(a, axis=-1):
    axis = axis % a.ndim
    shp = a.shape[:axis] + (N_CHIPS, a.shape[axis] // N_CHIPS) + a.shape[axis + 1:]
    return jnp.moveaxis(a.reshape(shp), axis, 0)


def _from_sm(a, axis=-1):
    nd = a.ndim - 1
    axis = axis % nd
    b = jnp.moveaxis(a, 0, axis)
    return b.reshape(b.shape[:axis] + (b.shape[axis] * b.shape[axis + 1],) + b.shape[axis + 2:])


class GradBuffers:
    def __init__(self, W):
        groups = {}
        for n in BIG:
            _, layers, rows, cols = W[n].shape
            groups.setdefault(cols, []).append((rows, n, layers))
        self.where, self.cols_of, self.buf, self.members = {}, {}, {}, {}
        for cols, items in groups.items():
            off, members = 0, []
            for rows, n, layers in sorted(items, key=lambda t: -t[0]):
                assert off % rows == 0
                self.where[n], self.cols_of[n] = (off, rows), cols
                members.append((n, off, rows * layers))
                off += rows * layers
            assert off % 32 == 0
            self.buf[cols] = lax.empty((N_CHIPS, off, cols), F32)
            self.members[cols] = members

    def put(self, n, layer, fn, **kw):
        cols = self.cols_of[n]
        off, rows = self.where[n]
        self.buf[cols] = fn(into=self.buf[cols], off=off + layer * rows, **kw)


def forward_backward(x, p4, tgt, W):
    S, D = x.shape
    scale = 1.0 / math.sqrt(HEAD_DIM)
    saved = []
    kh = vh = xb_kv = None
    xb = x.astype(BF16)
    for i in range(DEPTH):
        sv = {'xb': xb}
        if i < N_A:
            h_sm = mm_colsm(f"pw1_{i}", xb, W['a_pw1_w'], i, W['a_pw1_b'])
            u = glu_fwd(f"glu_{i}", h_sm)
            c, s = conv_ln_silu_fwd(f"convln_{i}", u, W['a_dw_w'], W['a_dw_b'], W['a_ln_g'], W['a_ln_b'], i)
            mix = mm_rowsm(f"pw2_{i}", s, W['a_pw2_w'], i, bias=W['a_pw2_b'])
            sv.update(h_sm=h_sm, u=u, c=c, s=s)
        else:
            j = i - N_A
            if kh is None:
                xb_kv = xb
                kh = mm_rowsm("wk", xb, W['kv_wk'], 0, out_dtype=BF16)
                vh = mm_rowsm("wv", xb, W['kv_wv'], 0, out_dtype=BF16)
            qh = mm_rowsm(f"wq_{j}", xb, W['b_wq'], j, out_dtype=BF16, out_scale=scale)
            o, runs = attn_fwd(f"attn_{j}", qh, kh, vh)
            mix = mm_rowsm(f"wo_{j}", o, W['b_wo'], j)
            sv.update(qh=qh, o=o, runs=runs)
        x1, x1b, xh1, rs1 = ln_fwd(f"lnmix_{i}", x, mix, W['ln_mix_g'], W['ln_mix_b'], i)
        up_sm = mm_colsm(f"up_{i}", x1b, W['ffn_w_up'], i, out_dtype=BF16)
        gp_sm = mm_colsm(f"gate_{i}", x1b, W['ffn_w_gate'], i, out_dtype=BF16)
        hf_sm, gc_sm = ffn_gate_fwd(f"ffngate_{i}", up_sm, gp_sm, W['ffn_conv_w'], W['ffn_conv_b'], i)
        ffn = mm_rowsm(f"down_{i}", hf_sm, W['ffn_w_down'], i, a_sm=True)
        tg = mm_rowsm(f"plegate_{i}", x1b, W['ple_w_gate'], i)
        pp = mm_proj(f"pleproj_{i}", p4, i, W['ple_w_proj'])
        x2, x2b, xh2, rs2 = ln_fwd(f"lnffn_{i}", x1, ffn, W['ln_ffn_g'], W['ln_ffn_b'], i, tg=tg, pp=pp)
        sv.update(x1b=x1b, xh1=xh1, rs1=rs1, up_sm=up_sm, gp_sm=gp_sm, gc_sm=gc_sm, hf_sm=hf_sm, tg=tg, pp=pp, xh2=xh2, rs2=rs2)
        saved.append(sv)
        x, xb = x2, x2b

    dx, lacc = loss_head("loss", x, tgt)
    loss_sum = lacc[0, 0]

    G = {n: [None] * DEPTH for n in WEIGHTS if n not in BIG}
    gb = GradBuffers(W)
    dk = dv = None
    for i in reversed(range(DEPTH)):
        sv = saved[i]
        dr, acc, dtg, dpp = ln_bwd(f"lnffn_b_{i}", dx, sv['xh2'], sv['rs2'], W['ln_ffn_g'], i, tg=sv['tg'], pp=sv['pp'])
        G['ln_ffn_g'][i], G['ln_ffn_b'][i] = acc[0], acc[1]
        gb.put('ple_w_proj', i, functools.partial(mm_tn_proj, f"dproj_{i}", p4, i, dpp))
        gb.put('ple_w_gate', i, functools.partial(mm_tn_rowsm_fan, f"dplegate_{i}", sv['x1b'], dtg))
        gb.put('ffn_w_down', i, functools.partial(mm_tn_rowsm, f"ddown_{i}", sv['hf_sm'], dr, a_sm=True))
        dhf_sm = mm_nt_rowsm(f"dhf_{i}", dr, W['ffn_w_down'], i, out_sm=True, out_dtype=BF16)
        dup_sm, dgc_sm = ffn_gate_bwd_a(f"ffngate_ba_{i}", dhf_sm, sv['up_sm'], sv['gc_sm'])
        dgp_sm, cacc = ffn_gate_bwd_b(f"ffngate_bb_{i}", dgc_sm, sv['gp_sm'], W['ffn_conv_w'], i)
        kw = W['ffn_conv_w'].shape[2]
        G['ffn_conv_w'][i] = cacc[:, 0:kw, :]
        G['ffn_conv_b'][i] = cacc[:, 7, :].reshape(-1)
        gb.put('ffn_w_up', i, functools.partial(mm_tn_colsm, f"dup_{i}", sv['x1b'], dup_sm))
        gb.put('ffn_w_gate', i, functools.partial(mm_tn_colsm, f"dgate_{i}", sv['x1b'], dgp_sm))
        dx1 = mm_nt_rowsm(f"dx1a_{i}", dtg, W['ple_w_gate'], i, res=dr, res_alpha=DN_ALPHA)
        dx1 = mm_nt_colsm(f"dx1b_{i}", _sm_parts(dup_sm), W['ffn_w_up'], i, res=dx1)
        dx1 = mm_nt_colsm(f"dx1c_{i}", _sm_parts(dgp_sm), W['ffn_w_gate'], i, res=dx1)

        dr1, acc1 = ln_bwd(f"lnmix_b_{i}", dx1, sv['xh1'], sv['rs1'], W['ln_mix_g'], i)
        G['ln_mix_g'][i], G['ln_mix_b'][i] = acc1[0], acc1[1]
        xin = sv['xb']
        if i < N_A:
            G['a_pw2_b'][i] = acc1[2]
            gb.put('a_pw2_w', i, functools.partial(mm_tn_rowsm_fan, f"dpw2_{i}", sv['s'], dr1))
            ds = mm_nt_rowsm(f"ds_{i}", dr1, W['a_pw2_w'], i)
            dc, cacc = ln_silu_bwd(f"lnsilu_b_{i}", ds, sv['c'], W['a_ln_g'], W['a_ln_b'], i)
            G['a_ln_g'][i], G['a_ln_b'][i], G['a_dw_b'][i] = cacc[0], cacc[1], cacc[2]
            da, dg, dw, dba, dbg = conv_glu_bwd(f"convglu_b_{i}", dc, sv['u'], sv['h_sm'], W['a_dw_w'], i)
            kw = W['a_dw_w'].shape[1]
            G['a_dw_w'][i] = _from_sm(dw[:, 0:kw, :], axis=-1)
            G['a_pw1_b'][i] = jnp.concatenate([dba[:, 0, :], dbg[:, 0, :]], axis=0)
            half = da.shape[0]
            gb.put('a_pw1_w', i, functools.partial(mm_tn_colsm, f"dpw1a_{i}", xin, da))
            gb.put('a_pw1_w', i, functools.partial(mm_tn_colsm, f"dpw1g_{i}", xin, dg), j0=half)
            dx = mm_nt_colsm(f"dxa_{i}", _sm_parts(da) + _sm_parts(dg), W['a_pw1_w'], i, res=dr1, res_alpha=DN_ALPHA)
        else:
            j = i - N_A
            gb.put('b_wo', j, functools.partial(mm_tn_rowsm_fan, f"dwo_{j}", sv['o'], dr1))
            do = mm_nt_rowsm(f"do_{j}", dr1, W['b_wo'], j)
            dq, dk, dv = attn_bwd(f"attn_b_{j}", sv['qh'], kh, vh, do, sv['runs'], dk, dv)
            gb.put('b_wq', j, functools.partial(mm_tn_rowsm_fan, f"dwq_{j}", xin, dq))
            dx = mm_nt_rowsm(f"dxq_{j}", dq, W['b_wq'], j, res=dr1, res_alpha=DN_ALPHA)
            if j == 0:
                gb.put('kv_wk', 0, functools.partial(mm_tn_rowsm_fan, "dwk", xb_kv, dk))
                gb.put('kv_wv', 0, functools.partial(mm_tn_rowsm_fan, "dwv", xb_kv, dv))
                dx = mm_nt_rowsm("dxk", dk, W['kv_wk'], 0, res=dx)
                dx = mm_nt_rowsm("dxv", dv, W['kv_wv'], 0, res=dx)
    return loss_sum, dx, G, gb


MESH = pl.DeviceIdType.MESH
HBM = pl.BlockSpec(memory_space=pltpu.HBM)


def _place():
    x, y, c = lax.axis_index("x"), lax.axis_index("y"), lax.axis_index("c")
    others = [(1 - x, y), (x, 1 - y), (1 - x, 1 - y)]
    return x, y, c, others


def allgather_chips(name, arrs):
    n = len(arrs)

    def body(*refs):
        ins, outs = refs[:n], refs[n:2 * n]
        send_sems, recv_sems = refs[2 * n:]
        x, y, c, others = _place()
        me = 2 * x + y
        sibling = (x, y, 1 - c)
        ids = [2 * ch[0] + ch[1] for ch in others]
        from_id = jnp.where(c == 0, ids[0], ids[1])
        to_chip = (jnp.where(c == 0, x, 1 - x), jnp.where(c == 0, 1 - y, y))

        def remote(a, k, src, chip_id, half, to):
            return pltpu.make_async_remote_copy(src_ref=src, dst_ref=outs[a].at[chip_id, half], send_sem=send_sems.at[a, k],
                                                recv_sem=recv_sems.at[a, k], device_id=to, device_id_type=MESH)

        sent = [remote(a, k, ins[a].at[c], me, c, (others[k][0], others[k][1], c)) for a in range(n) for k in range(2)]
        for cp in sent:
            cp.start()
        for a in range(n):
            for k in range(2):
                remote(a, k, ins[a].at[c], ids[k], c, sibling).wait_recv()
            sent.append(remote(a, 2, outs[a].at[from_id, c], from_id, c, (to_chip[0], to_chip[1], c)))
            sent[-1].start()
            for k in range(2):
                sent.append(remote(a, 3 + k, outs[a].at[ids[k], c], ids[k], c, sibling))
                sent[-1].start()
        for a in range(n):
            remote(a, 2, ins[a].at[c], ids[2], c, sibling).wait_recv()
            sent.append(remote(a, 5, outs[a].at[ids[2], c], ids[2], c, sibling))
            sent[-1].start()
        for a in range(n):
            for k in range(3):
                remote(a, 3 + k, ins[a].at[c], ids[k], 1 - c, sibling).wait_recv()
        for cp in sent:
            cp.wait_send()

    outs = pl.pallas_call(body, out_shape=tuple(jax.ShapeDtypeStruct((N_CHIPS,) + a.shape, a.dtype) for a in arrs),
                          in_specs=[HBM] * n, out_specs=tuple([HBM] * n),
                          scratch_shapes=[pltpu.SemaphoreType.DMA((n, 6)), pltpu.SemaphoreType.DMA((n, 6))],
                          name=name)(*arrs)
    me = 2 * lax.axis_index("x") + lax.axis_index("y")
    return [lax.dynamic_update_index_in_dim(o, a, me, 0) for o, a in zip(outs, arrs)]


def exchange_sibling(name, gs):
    n = len(gs)

    def body(*refs):
        g_refs, o_refs = refs[:n], refs[n:2 * n]
        send_sems, recv_sems = refs[2 * n:]
        x, y, c, _ = _place()
        cps = [pltpu.make_async_remote_copy(src_ref=g_refs[a].at[j, 1 - c], dst_ref=o_refs[a].at[j], send_sem=send_sems.at[a, j],
                                            recv_sem=recv_sems.at[a, j], device_id=(x, y, 1 - c), device_id_type=MESH)
               for a in range(n) for j in range(N_CHIPS)]
        for cp in cps:
            cp.start()
        for cp in cps:
            cp.wait()

    return pl.pallas_call(body, out_shape=tuple(jax.ShapeDtypeStruct((N_CHIPS,) + g.shape[2:], g.dtype) for g in gs),
                          in_specs=[HBM] * n, out_specs=tuple([HBM] * n),
                          scratch_shapes=[pltpu.SemaphoreType.DMA((n, N_CHIPS)), pltpu.SemaphoreType.DMA((n, N_CHIPS))],
                          name=name)(*gs)


def _ring_peers():
    x, y, c, _ = _place()
    first = (jnp.where(c == 0, 1 - x, x), jnp.where(c == 0, y, 1 - y))
    second = (jnp.where(c == 0, x, 1 - x), jnp.where(c == 0, 1 - y, y))
    return c, first, second, 2 * (1 - x) + (1 - y)


def exchange_first(name, ss):
    n = len(ss)

    def body(*refs):
        s_refs, o_refs = refs[:n], refs[n:2 * n]
        send_sems, recv_sems = refs[2 * n:]
        c, first, _, diag = _ring_peers()
        cps = [pltpu.make_async_remote_copy(src_ref=s_refs[a].at[slot], dst_ref=o_refs[a].at[k], send_sem=send_sems.at[a, k],
                                            recv_sem=recv_sems.at[a, k], device_id=(first[0], first[1], c), device_id_type=MESH)
               for a in range(n) for k, slot in enumerate((2 * first[0] + first[1], diag))]
        for cp in cps:
            cp.start()
        for cp in cps:
            cp.wait()

    return pl.pallas_call(body, out_shape=tuple(jax.ShapeDtypeStruct((2,) + s.shape[1:], s.dtype) for s in ss),
                          in_specs=[HBM] * n, out_specs=tuple([HBM] * n),
                          scratch_shapes=[pltpu.SemaphoreType.DMA((n, 2)), pltpu.SemaphoreType.DMA((n, 2))], name=name)(*ss)


def exchange_second(name, ts):
    n = len(ts)

    def body(*refs):
        t_refs, o_refs = refs[:n], refs[n:2 * n]
        send_sems, recv_sems = refs[2 * n:]
        c, _, second, _ = _ring_peers()
        cps = [pltpu.make_async_remote_copy(src_ref=t_refs[a], dst_ref=o_refs[a], send_sem=send_sems.at[a], recv_sem=recv_sems.at[a],
                                            device_id=(second[0], second[1], c), device_id_type=MESH) for a in range(n)]
        for cp in cps:
            cp.start()
        for cp in cps:
            cp.wait()

    return pl.pallas_call(body, out_shape=tuple(jax.ShapeDtypeStruct(t.shape, t.dtype) for t in ts),
                          in_specs=[HBM] * n, out_specs=tuple([HBM] * n),
                          scratch_shapes=[pltpu.SemaphoreType.DMA((n,)), pltpu.SemaphoreType.DMA((n,))], name=name)(*ts)


def share_sibling(name, ts):
    n = len(ts)

    def body(*refs):
        o_refs = refs[n:2 * n]
        send_sems, recv_sems = refs[2 * n:]
        x, y, c, _ = _place()
        cps = [pltpu.make_async_remote_copy(src_ref=o_refs[a].at[c], dst_ref=o_refs[a].at[c], send_sem=send_sems.at[a],
                                            recv_sem=recv_sems.at[a], device_id=(x, y, 1 - c), device_id_type=MESH)
               for a in range(n)]
        for cp in cps:
            cp.start()
        for a in range(n):
            pltpu.make_async_remote_copy(src_ref=o_refs[a].at[c], dst_ref=o_refs[a].at[1 - c], send_sem=send_sems.at[a],
                                         recv_sem=recv_sems.at[a], device_id=(x, y, 1 - c), device_id_type=MESH).wait_recv()
        for cp in cps:
            cp.wait_send()

    return pl.pallas_call(body, out_shape=tuple(jax.ShapeDtypeStruct(t.shape, t.dtype) for t in ts),
                          in_specs=[HBM] * n, out_specs=tuple([HBM] * n), input_output_aliases={a: a for a in range(n)},
                          scratch_shapes=[pltpu.SemaphoreType.DMA((n,)), pltpu.SemaphoreType.DMA((n,))],
                          name=name)(*ts)


def add_halves(name, g, recv, place, out_dtype, tr=512):
    _, _, R, C = g.shape
    tr = _tile(R, tr, 16)

    def body(p_ref, a_ref, b_ref, o_ref):
        o_ref[...] = (a_ref[...] + b_ref[...]).astype(o_ref.dtype)

    blk = pl.BlockSpec((None, tr, C), lambda j, i, p: (j, i, 0))
    gs = pltpu.PrefetchScalarGridSpec(num_scalar_prefetch=1, grid=(N_CHIPS, R // tr),
                                      in_specs=[pl.BlockSpec((None, None, tr, C), lambda j, i, p: (j, p[0], i, 0)), blk],
                                      out_specs=blk)
    return pl.pallas_call(body, grid_spec=gs, out_shape=jax.ShapeDtypeStruct((N_CHIPS, R, C), out_dtype),
                          compiler_params=_params(("parallel", "parallel")), name=name)(place, g, recv)


def add_pass_on(name, s, got, place, tr=512):
    _, R, C = s.shape
    tr = _tile(R, tr, 16)

    def body(p_ref, a_ref, b_ref, o_ref):
        o_ref[...] = (a_ref[...].astype(F32) + b_ref[...].astype(F32)).astype(o_ref.dtype)

    gs = pltpu.PrefetchScalarGridSpec(num_scalar_prefetch=1, grid=(R // tr,),
                                      in_specs=[pl.BlockSpec((None, tr, C), lambda i, p: (p[2], i, 0)),
                                                pl.BlockSpec((None, tr, C), lambda i, p: (1, i, 0))],
                                      out_specs=pl.BlockSpec((tr, C), lambda i, p: (i, 0)))
    return pl.pallas_call(body, grid_spec=gs, out_shape=jax.ShapeDtypeStruct((R, C), s.dtype),
                          compiler_params=_params(("parallel",)), name=name)(place, s, got)


def add_chips(name, g, r1, got1, got2, place, tr=512):
    _, _, R, C = g.shape
    tr = _tile(R, tr, 16)

    def body(p_ref, a_ref, b_ref, c_ref, d_ref, o_ref):
        o_ref[...] = ((a_ref[...] + b_ref[...]) + c_ref[...].astype(F32)) + d_ref[...].astype(F32)

    gs = pltpu.PrefetchScalarGridSpec(num_scalar_prefetch=1, grid=(R // tr,),
                                      in_specs=[pl.BlockSpec((None, None, tr, C), lambda i, p: (p[1], p[0], i, 0)),
                                                pl.BlockSpec((None, tr, C), lambda i, p: (p[1], i, 0)),
                                                pl.BlockSpec((None, tr, C), lambda i, p: (0, i, 0)),
                                                pl.BlockSpec((tr, C), lambda i, p: (i, 0))],
                                      out_specs=pl.BlockSpec((None, tr, C), lambda i, p: (p[0], i, 0)))
    return pl.pallas_call(body, grid_spec=gs, out_shape=jax.ShapeDtypeStruct((2, R, C), F32),
                          compiler_params=_params(("parallel",)), name=name)(place, g, r1, got1, got2)


def reduce_scatter(gs, wire_dtypes, place):
    r1 = exchange_sibling("rs_sibling", gs)
    s1 = [add_halves(f"rs_add_cores_{a}", g, r, place, dt) for a, (g, r, dt) in enumerate(zip(gs, r1, wire_dtypes))]
    got1 = exchange_first("rs_first", s1)
    t = [add_pass_on(f"rs_add_pass_{a}", s, g1, place) for a, (s, g1) in enumerate(zip(s1, got1))]
    got2 = exchange_second("rs_second", t)
    tot = [add_chips(f"rs_add_chips_{a}", g, r, g1, g2, place) for a, (g, r, g1, g2) in enumerate(zip(gs, r1, got1, got2))]
    return share_sibling("rs_share", tot)


def adamw(name, w, g, m, v, tr=512):
    shp = w.shape
    cols = shp[-1]
    w2, g2, m2, v2 = (a.reshape(-1, cols) for a in (w, g, m, v))
    rows = w2.shape[0]
    tr = _tile(rows, tr)

    def body(w_ref, g_ref, m_ref, v_ref, d_ref, mo_ref, vo_ref):
        g_ = g_ref[...]
        m_ = ADAM_B1 * m_ref[...] + (1.0 - ADAM_B1) * g_
        v_ = ADAM_B2 * v_ref[...] + (1.0 - ADAM_B2) * (g_ * g_)
        m_hat = m_ / (1.0 - ADAM_B1 ** ADAM_STEP)
        v_hat = v_ / (1.0 - ADAM_B2 ** ADAM_STEP)
        d_ref[...] = -ADAM_LR * (m_hat / (jnp.sqrt(v_hat) + ADAM_EPS) + ADAM_WD * w_ref[...])
        mo_ref[...] = m_
        vo_ref[...] = v_

    blk = pl.BlockSpec((tr, cols), lambda i: (i, 0))
    sds = jax.ShapeDtypeStruct((rows, cols), F32)
    d, mo, vo = pl.pallas_call(body, grid=(rows // tr,), in_specs=[blk] * 4, out_specs=(blk, blk, blk), out_shape=(sds, sds, sds),
                               compiler_params=_params(("parallel",)), name=name)(w2, g2, m2, v2)
    return d.reshape(shp), mo.reshape(shp), vo.reshape(shp)


PACK_ALIGN = 1024


def _pad_to(a, mult, axis=-1):
    axis = axis % a.ndim
    extra = (-a.shape[axis]) % mult
    if extra == 0:
        return a
    pads = [(0, 0)] * a.ndim
    pads[axis] = (0, extra)
    return jnp.pad(a, pads)


def _pack(pieces, lead, row_mult):
    nl = len(lead)
    flat, offs, sizes, off = [], [], [], 0
    for a in pieces:
        f = a.reshape(lead + (-1,))
        sizes.append(f.shape[-1])
        f = _pad_to(f, PACK_ALIGN)
        offs.append(off)
        off += f.shape[-1]
        flat.append(f)
    cat = _pad_to(jnp.concatenate(flat, axis=nl), 2 * row_mult * LANES)
    return cat.reshape(lead + (2, -1, LANES)), offs, sizes


def _unpack(packed, lead, offs, sizes, shapes):
    flat = packed.reshape(lead + (-1,))
    return [lax.slice_in_dim(flat, o, o + s, axis=len(lead)).reshape(lead + tuple(shp)) for o, s, shp in zip(offs, sizes, shapes)]


def _stack_grads(G, names):
    out = {}
    for n in names:
        parts = [g for g in G[n] if g is not None]
        if n in ('kv_wk', 'kv_wv'):
            out[n] = parts[0]
        elif n in REPLICATED:
            out[n] = jnp.stack(parts, axis=0).reshape(N_CHIPS, -1)
        elif n in ('a_dw_w', 'a_dw_b', 'a_ln_g', 'a_ln_b', 'a_pw2_b'):
            out[n] = _to_sm(jnp.stack(parts, axis=0), axis=-1)
        else:
            out[n] = jnp.stack(parts, axis=1)
    return out


def _whole_weights(big, small, rep, D):
    W = {}
    for n in BIG:
        a = big[n]
        W[n] = a[:, None] if n in ('kv_wk', 'kv_wv') else a
    W['a_pw1_b'] = small['a_pw1_b'][:, :, None, :]
    W['a_dw_w'] = _from_sm(small['a_dw_w'], axis=-1)
    for n in ('a_dw_b', 'a_ln_g', 'a_ln_b', 'a_pw2_b'):
        W[n] = _from_sm(small[n], axis=-1)[:, None, :]
    W['ffn_conv_w'] = small['ffn_conv_w']
    L, F = rep['ffn_conv_b'].shape
    W['ffn_conv_b'] = rep['ffn_conv_b'].reshape(L, N_CHIPS, 1, F // N_CHIPS)
    for n in ('ln_mix_g', 'ln_mix_b', 'ln_ffn_g', 'ln_ffn_b'):
        W[n] = rep[n][:, None, :]
    return W


SMALL = ('a_pw1_b', 'a_dw_w', 'a_dw_b', 'a_ln_g', 'a_ln_b', 'a_pw2_b', 'ffn_conv_w')


def _step(x, p, loss_target, w, m, v):
    S, D = x.shape[-2:]
    x2, tgt = x.reshape(S, D), loss_target.reshape(S, D)
    ax, ay, ac = lax.axis_index("x"), lax.axis_index("y"), lax.axis_index("c")
    second = jnp.where(ac == 0, 2 * ax + (1 - ay), 2 * (1 - ax) + ay)
    place = jnp.stack([ac, 2 * ax + ay, second]).astype(jnp.int32)

    big_in = [w[n].astype(BF16).reshape((2, -1) + w[n].shape[1:] if w[n].ndim == 3 else (2, -1, w[n].shape[-1])) for n in BIG]
    small_in, s_offs, s_sizes = _pack([w[n] for n in SMALL], (), 8)
    gathered = allgather_chips("gather_weights", big_in + [small_in])
    big = {n: g.reshape((N_CHIPS,) + w[n].shape) for n, g in zip(BIG, gathered[:-1])}
    small = dict(zip(SMALL, _unpack(gathered[-1], (N_CHIPS,), s_offs, s_sizes, [w[n].shape for n in SMALL])))
    W = _whole_weights(big, small, {n: w[n] for n in REPLICATED}, D)

    loss_sum, dx, G, gb = forward_backward(x2, p, tgt, W)
    loss = lax.psum(0.5 * loss_sum / D, ("x", "y", "c"))

    vectors = [n for n in WEIGHTS if n not in BIG]
    mats = [b.reshape(N_CHIPS, 2, b.shape[1] // 2, b.shape[2]) for b in gb.buf.values()]
    members = [gb.members[cols] for cols in gb.buf]
    g_sm = _stack_grads(G, vectors)
    packed, offs, sizes = _pack([g_sm[n] for n in vectors], (N_CHIPS,), 512)
    reduced = reduce_scatter(mats + [packed], [BF16] * len(mats) + [F32], place)
    shapes = [w[n].shape if n not in REPLICATED else (w[n].size // N_CHIPS,) for n in vectors]
    g_mine = dict(zip(vectors, _unpack(reduced[-1], (), offs, sizes, shapes)))
    for red, where in zip(reduced[:-1], members):
        rows = red.reshape(-1, red.shape[-1])
        for n, off, cnt in where:
            g_mine[n] = lax.slice_in_dim(rows, off, off + cnt, axis=0).reshape(w[n].shape)
    rep_in, r_offs, r_sizes = _pack([g_mine[n] for n in REPLICATED], (), 8)
    rep_all = allgather_chips("gather_replicated_grads", [rep_in])[0]
    for n, g in zip(REPLICATED, _unpack(rep_all, (N_CHIPS,), r_offs, r_sizes, [(w[n].size // N_CHIPS,) for n in REPLICATED])):
        g_mine[n] = g.reshape(w[n].shape)

    grads, deltas, new_m, new_v = [], [], [], []
    for n in WEIGHTS:
        d, mo, vo = adamw(f"adamw_{n}", w[n], g_mine[n], m[n], v[n])
        grads.append(g_mine[n])
        deltas.append(d)
        new_m.append(mo)
        new_v.append(vo)
    return (loss, dx.reshape(x.shape), *grads, *deltas, *new_m, *new_v)


def kernel(x, p, a_pw1_w, a_pw1_b, a_dw_w, a_dw_b, a_ln_g, a_ln_b, a_pw2_w, a_pw2_b, b_wq, kv_wk, kv_wv, b_wo, ln_mix_g, ln_mix_b, ffn_w_up, ffn_w_gate, ffn_conv_w, ffn_conv_b, ffn_w_down, ple_w_gate, ple_w_proj, ln_ffn_g, ln_ffn_b, loss_target, m_a_pw1_w, m_a_pw1_b, m_a_dw_w, m_a_dw_b, m_a_ln_g, m_a_ln_b, m_a_pw2_w, m_a_pw2_b, m_b_wq, m_kv_wk, m_kv_wv, m_b_wo, m_ln_mix_g, m_ln_mix_b, m_ffn_w_up, m_ffn_w_gate, m_ffn_conv_w, m_ffn_conv_b, m_ffn_w_down, m_ple_w_gate, m_ple_w_proj, m_ln_ffn_g, m_ln_ffn_b, v_a_pw1_w, v_a_pw1_b, v_a_dw_w, v_a_dw_b, v_a_ln_g, v_a_ln_b, v_a_pw2_w, v_a_pw2_b, v_b_wq, v_kv_wk, v_kv_wv, v_b_wo, v_ln_mix_g, v_ln_mix_b, v_ffn_w_up, v_ffn_w_gate, v_ffn_conv_w, v_ffn_conv_b, v_ffn_w_down, v_ple_w_gate, v_ple_w_proj, v_ln_ffn_g, v_ln_ffn_b):
    vals = dict(locals())
    w = {n: vals[n] for n in WEIGHTS}
    m = {n: vals["m_" + n] for n in WEIGHTS}
    v = {n: vals["v_" + n] for n in WEIGHTS}
    return _step(x, p, loss_target, w, m, v)
```

```python
import functools
import math

import jax
import jax.numpy as jnp
import numpy as np
from jax import lax
from jax.experimental import pallas as pl
from jax.experimental.pallas import tpu as pltpu

F32, BF16 = jnp.float32, jnp.bfloat16

HEAD_DIM = 64
LN_EPS = 1e-5
DEPTH = 4
N_A = DEPTH // 2
DN_ALPHA = (2.0 * DEPTH) ** 0.25
N_CHIPS = 4

ADAM_LR, ADAM_B1, ADAM_B2, ADAM_EPS, ADAM_WD, ADAM_STEP = 0.001, 0.9, 0.999, 1e-08, 0.01, 10

VMEM_LIMIT_BYTES = 56 * 2**20
LANES = 128
CONV_HALO = 32
FFN_HALO = 16

NN = (((1,), (0,)), ((), ()))
NT = (((1,), (1,)), ((), ()))
TN = (((0,), (0,)), ((), ()))

WEIGHTS = ['a_pw1_w', 'a_pw1_b', 'a_dw_w', 'a_dw_b', 'a_ln_g', 'a_ln_b', 'a_pw2_w', 'a_pw2_b', 'b_wq', 'kv_wk', 'kv_wv',
           'b_wo', 'ln_mix_g', 'ln_mix_b', 'ffn_w_up', 'ffn_w_gate', 'ffn_conv_w', 'ffn_conv_b', 'ffn_w_down', 'ple_w_gate',
           'ple_w_proj', 'ln_ffn_g', 'ln_ffn_b']
REPLICATED = ('ln_mix_g', 'ln_mix_b', 'ffn_conv_b', 'ln_ffn_g', 'ln_ffn_b')
BIG = ('a_pw1_w', 'a_pw2_w', 'b_wq', 'kv_wk', 'kv_wv', 'b_wo', 'ffn_w_up', 'ffn_w_gate', 'ffn_w_down', 'ple_w_gate',
       'ple_w_proj')


def _tile(n, pref, mult=8):
    t = min(n, pref)
    while t > 0:
        if n % t == 0 and t % mult == 0:
            return t
        t -= 1
    return n


def _params(sem):
    return pltpu.CompilerParams(dimension_semantics=sem, vmem_limit_bytes=VMEM_LIMIT_BYTES)


def _sigmoid(x):
    return 1.0 / (1.0 + jnp.exp(-x))


def _mm_call(name, grid, terms, out_spec, out_sds, dims, bias=None, res=None, res_alpha=1.0, out_scale=None, into=None):
    n_terms = len(terms)

    def body(*refs):
        o_ref = refs[-1]
        acc = None
        for t in range(n_terms):
            a = refs[2 * t][...].astype(BF16)
            b = refs[2 * t + 1][...].astype(BF16)
            d = lax.dot_general(a, b, dims, preferred_element_type=F32)
            acc = d if acc is None else acc + d
        k = 2 * n_terms
        if bias is not None:
            acc = acc + refs[k][...]
            k += 1
        if res is not None:
            acc = acc + res_alpha * refs[k][...]
        if out_scale is not None:
            acc = acc * out_scale
        o_ref[...] = acc.astype(o_ref.dtype)

    operands, specs = [], []
    for a, a_spec, b, b_spec in terms:
        operands += [a, b]
        specs += [a_spec, b_spec]
    for extra in (bias, res):
        if extra is not None:
            operands.append(extra[0])
            specs.append(extra[1])
    aliases = {}
    if into is not None:
        aliases = {len(operands): 0}
        operands.append(into)
        specs.append(pl.BlockSpec(memory_space=pl.ANY))
        out_sds = jax.ShapeDtypeStruct(into.shape, into.dtype)
    return pl.pallas_call(body, out_shape=out_sds, grid=grid, in_specs=specs, out_specs=out_spec, input_output_aliases=aliases,
                          compiler_params=_params(("parallel",) * len(grid)), name=name)(*operands)


def _mm_fanout(name, a, a_spec, w4, w_block, layer, dims, out_spec, out_sds, store, grid, bias4=None, res=None, res_alpha=1.0):
    def body(*refs):
        a_ref, w_refs, o_ref = refs[0], refs[1:1 + N_CHIPS], refs[-1]
        k = 1 + N_CHIPS
        b_refs = refs[k:k + N_CHIPS] if bias4 is not None else None
        k += N_CHIPS if bias4 is not None else 0
        av = a_ref[...].astype(BF16)
        for j in range(N_CHIPS):
            d = lax.dot_general(av, w_refs[j][...].astype(BF16), dims, preferred_element_type=F32)
            if b_refs is not None:
                d = d + b_refs[j][...]
            if res is not None:
                d = d + res_alpha * res[2](refs[k], j)
            store(o_ref, j, d)

    nd = len(grid)
    operands = [a] + [w4] * N_CHIPS
    specs = [a_spec] + [pl.BlockSpec((None, None) + w_block, lambda *g, j=j: (j, layer, 0, 0)) for j in range(N_CHIPS)]
    if bias4 is not None:
        operands += [bias4] * N_CHIPS
        specs += [pl.BlockSpec((None, None, 1, bias4.shape[-1]), lambda *g, j=j: (j, layer, 0, 0)) for j in range(N_CHIPS)]
    if res is not None:
        operands.append(res[0])
        specs.append(res[1])
    return pl.pallas_call(body, out_shape=out_sds, grid=grid, in_specs=specs, out_specs=out_spec,
                          compiler_params=_params(("parallel",) * nd), name=name)(*operands)


def _store_slot(o_ref, j, d):
    o_ref[j] = d.astype(o_ref.dtype)


def mm_colsm(name, x, w4, layer, bias4=None, out_dtype=F32, tm=1024):
    M, K = x.shape
    n = w4.shape[-1]
    tm = _tile(M, tm)
    return _mm_fanout(name, x, pl.BlockSpec((tm, K), lambda i: (i, 0)), w4, (K, n), layer, NN,
                      pl.BlockSpec((N_CHIPS, tm, n), lambda i: (0, i, 0)), jax.ShapeDtypeStruct((N_CHIPS, M, n), out_dtype),
                      _store_slot, (M // tm,), bias4=bias4)


def mm_rowsm(name, a, w4, layer, a_sm=False, bias=None, out_dtype=F32, out_scale=None, tm=512):
    kc, N = w4.shape[-2:]
    M = a.shape[-2]
    tm = _tile(M, tm)
    terms = []
    for j in range(N_CHIPS):
        if a_sm:
            a_spec = pl.BlockSpec((None, tm, kc), lambda i, j=j: (j, i, 0))
        else:
            a_spec = pl.BlockSpec((tm, kc), lambda i, j=j: (i, j))
        terms.append((a, a_spec, w4, pl.BlockSpec((None, None, kc, N), lambda i, j=j: (j, layer, 0, 0))))
    b = None if bias is None else (bias, pl.BlockSpec((None, 1, N), lambda i: (layer, 0, 0)))
    return _mm_call(name, (M // tm,), terms, pl.BlockSpec((tm, N), lambda i: (i, 0)), jax.ShapeDtypeStruct((M, N), out_dtype),
                    NN, bias=b, out_scale=out_scale)


def mm_nt_rowsm(name, dy, w4, layer, out_sm=False, res=None, res_alpha=1.0, out_dtype=F32, tm=1024):
    kc, N = w4.shape[-2:]
    M = dy.shape[0]
    tm = _tile(M, tm)

    def store_cols(o_ref, j, d):
        o_ref[:, j * kc:(j + 1) * kc] = d

    if out_sm:
        out_spec, sds, store = pl.BlockSpec((N_CHIPS, tm, kc), lambda i: (0, i, 0)), jax.ShapeDtypeStruct((N_CHIPS, M, kc), out_dtype), _store_slot
    else:
        out_spec, sds, store = pl.BlockSpec((tm, N_CHIPS * kc), lambda i: (i, 0)), jax.ShapeDtypeStruct((M, N_CHIPS * kc), F32), store_cols
    r = None if res is None else (res, pl.BlockSpec((tm, N_CHIPS * kc), lambda i: (i, 0)), lambda ref, j: ref[:, j * kc:(j + 1) * kc])
    return _mm_fanout(name, dy, pl.BlockSpec((tm, N), lambda i: (i, 0)), w4, (kc, N), layer, NT, out_spec, sds, store, (M // tm,),
                      res=r, res_alpha=res_alpha)


def mm_nt_colsm(name, dy_parts, w4, layer, res=None, res_alpha=1.0, tm=512):
    K, n = w4.shape[-2:]
    M = dy_parts[0][0].shape[1]
    tm = _tile(M, tm)
    terms = [(arr, pl.BlockSpec((None, tm, n), lambda i, idx=idx: (idx, i, 0)), w4,
              pl.BlockSpec((None, None, K, n), lambda i, j=j: (j, layer, 0, 0))) for j, (arr, idx) in enumerate(dy_parts)]
    r = None if res is None else (res, pl.BlockSpec((tm, K), lambda i: (i, 0)))
    return _mm_call(name, (M // tm,), terms, pl.BlockSpec((tm, K), lambda i: (i, 0)), jax.ShapeDtypeStruct((M, K), F32), NT,
                    res=r, res_alpha=res_alpha)


def _sm_parts(a):
    return [(a, j) for j in range(a.shape[0])]


def mm_tn_colsm(name, x, dy, into, off, j0=0, tk=512):
    M, K = x.shape
    nj, _, n = dy.shape
    tk = _tile(K, tk, LANES)
    assert off % tk == 0
    terms = [(x, pl.BlockSpec((M, tk), lambda j, k: (0, k)), dy, pl.BlockSpec((None, M, n), lambda j, k: (j, 0, 0)))]
    return _mm_call(name, (nj, K // tk), terms, pl.BlockSpec((None, tk, n), lambda j, k: (j + j0, off // tk + k, 0)),
                    None, TN, into=into)


def mm_tn_rowsm(name, a, dy, into, off, a_sm=False, tn=512):
    M, N = dy.shape
    kc = a.shape[-1] if a_sm else a.shape[-1] // N_CHIPS
    tn = _tile(N, tn, LANES)
    assert off % kc == 0
    a_spec = pl.BlockSpec((None, M, kc), lambda j, n: (j, 0, 0)) if a_sm else pl.BlockSpec((M, kc), lambda j, n: (0, j))
    terms = [(a, a_spec, dy, pl.BlockSpec((M, tn), lambda j, n: (0, n)))]
    return _mm_call(name, (N_CHIPS, N // tn), terms, pl.BlockSpec((None, kc, tn), lambda j, n: (j, off // kc, n)),
                    None, TN, into=into)


def mm_tn_rowsm_fan(name, a, dy, into, off, tn=256):
    M, N = dy.shape
    kc = a.shape[-1] // N_CHIPS
    tn = _tile(N, tn, LANES)
    assert off % kc == 0

    def body(a_ref, dy_ref, into_ref, o_ref):
        dyb = dy_ref[...].astype(BF16)
        for j in range(N_CHIPS):
            aj = a_ref[:, j * kc:(j + 1) * kc].astype(BF16)
            o_ref[j] = lax.dot_general(aj, dyb, TN, preferred_element_type=F32)

    return pl.pallas_call(body, out_shape=jax.ShapeDtypeStruct(into.shape, into.dtype), grid=(N // tn,),
                          in_specs=[pl.BlockSpec((M, N_CHIPS * kc), lambda n: (0, 0)), pl.BlockSpec((M, tn), lambda n: (0, n)),
                                    pl.BlockSpec(memory_space=pl.ANY)],
                          out_specs=pl.BlockSpec((N_CHIPS, kc, tn), lambda n: (0, off // kc, n)), input_output_aliases={2: 0},
                          compiler_params=_params(("parallel",)), name=name)(a, dy, into)


def mm_proj(name, p4, layer, w4, tm=512):
    S, P = p4.shape[-2:]
    n = w4.shape[-1]
    tm = _tile(S, tm)

    def store_cols(o_ref, j, d):
        o_ref[:, j * n:(j + 1) * n] = d

    return _mm_fanout(name, p4, pl.BlockSpec((None, None, tm, P), lambda i: (layer, 0, i, 0)), w4, (P, n), layer, NN,
                      pl.BlockSpec((tm, N_CHIPS * n), lambda i: (i, 0)), jax.ShapeDtypeStruct((S, N_CHIPS * n), F32),
                      store_cols, (S // tm,))


def mm_tn_proj(name, p4, layer, dpp, into, off):
    S, P = p4.shape[-2:]
    n = dpp.shape[-1] // N_CHIPS
    assert off % P == 0
    terms = [(p4, pl.BlockSpec((None, None, S, P), lambda j: (layer, 0, 0, 0)), dpp, pl.BlockSpec((S, n), lambda j: (0, j)))]
    return _mm_call(name, (N_CHIPS,), terms, pl.BlockSpec((None, P, n), lambda j: (j, off // P, 0)), None, TN, into=into)


def ln_fwd(name, x, mix, g, b, layer, tg=None, pp=None, tm=512):
    S, D = x.shape
    tm = _tile(S, tm, 16)
    ple = tg is not None

    def body(*refs):
        if ple:
            x_ref, m_ref, tg_ref, pp_ref, g_ref, b_ref, y_ref, yb_ref, xh_ref, rs_ref = refs
        else:
            x_ref, m_ref, g_ref, b_ref, y_ref, yb_ref, xh_ref, rs_ref = refs
        r = DN_ALPHA * x_ref[...] + m_ref[...]
        if ple:
            r = r + _sigmoid(tg_ref[...]) * pp_ref[...]
        mu = jnp.mean(r, axis=-1, keepdims=True)
        d = r - mu
        var = jnp.mean(d * d, axis=-1, keepdims=True)
        rstd = lax.rsqrt(var + LN_EPS)
        xh = d * rstd
        y = xh * g_ref[...] + b_ref[...]
        y_ref[...] = y
        yb_ref[...] = y.astype(BF16)
        xh_ref[...] = xh
        rs_ref[...] = rstd

    row = pl.BlockSpec((tm, D), lambda i: (i, 0))
    vec = pl.BlockSpec((None, 1, D), lambda i: (layer, 0, 0))
    ins = [x, mix] + ([tg, pp] if ple else []) + [g, b]
    specs = [row, row] + ([row, row] if ple else []) + [vec, vec]
    return pl.pallas_call(body, grid=(S // tm,), in_specs=specs,
                          out_specs=(row, row, row, pl.BlockSpec((tm, 1), lambda i: (i, 0))),
                          out_shape=(jax.ShapeDtypeStruct((S, D), F32), jax.ShapeDtypeStruct((S, D), BF16),
                                     jax.ShapeDtypeStruct((S, D), F32), jax.ShapeDtypeStruct((S, 1), F32)),
                          compiler_params=_params(("parallel",)), name=name)(*ins)


def ln_bwd(name, dy, xh, rstd, g, layer, tg=None, pp=None, tm=512):
    S, D = dy.shape
    tm = _tile(S, tm)
    ple = tg is not None

    def body(*refs):
        if ple:
            dy_ref, xh_ref, rs_ref, g_ref, tg_ref, pp_ref, dr_ref, acc_ref, dtg_ref, dpp_ref = refs
        else:
            dy_ref, xh_ref, rs_ref, g_ref, dr_ref, acc_ref = refs
        dy_, xh_ = dy_ref[...], xh_ref[...]
        dxh = dy_ * g_ref[...]
        m1 = jnp.mean(dxh, axis=-1, keepdims=True)
        m2 = jnp.mean(dxh * xh_, axis=-1, keepdims=True)
        dr = rs_ref[...] * (dxh - m1 - xh_ * m2)
        dr_ref[...] = dr

        @pl.when(pl.program_id(0) == 0)
        def _():
            acc_ref[...] = jnp.zeros_like(acc_ref)
        acc_ref[0:1, :] += jnp.sum(dy_ * xh_, axis=0, keepdims=True)
        acc_ref[1:2, :] += jnp.sum(dy_, axis=0, keepdims=True)
        acc_ref[2:3, :] += jnp.sum(dr, axis=0, keepdims=True)
        if ple:
            pg = _sigmoid(tg_ref[...])
            dtg_ref[...] = (dr * pp_ref[...] * pg * (1.0 - pg)).astype(BF16)
            dpp_ref[...] = (dr * pg).astype(BF16)

    row = pl.BlockSpec((tm, D), lambda i: (i, 0))
    ins = [dy, xh, rstd, g] + ([tg, pp] if ple else [])
    specs = [row, row, pl.BlockSpec((tm, 1), lambda i: (i, 0)), pl.BlockSpec((None, 1, D), lambda i: (layer, 0, 0))] + ([row, row] if ple else [])
    outs = [jax.ShapeDtypeStruct((S, D), F32), jax.ShapeDtypeStruct((8, D), F32)]
    out_specs = [row, pl.BlockSpec((8, D), lambda i: (0, 0))]
    if ple:
        outs += [jax.ShapeDtypeStruct((S, D), BF16)] * 2
        out_specs += [row, row]
    return pl.pallas_call(body, grid=(S // tm,), in_specs=specs, out_specs=tuple(out_specs), out_shape=tuple(outs),
                          compiler_params=_params(("arbitrary",)), name=name)(*ins)


def glu_fwd(name, h_sm, tm=512):
    _, S, n = h_sm.shape
    tm = _tile(S, tm)
    half = N_CHIPS // 2

    def body(a_ref, g_ref, u_ref):
        u_ref[...] = a_ref[...] * _sigmoid(g_ref[...])

    return pl.pallas_call(body, grid=(half, S // tm),
                          in_specs=[pl.BlockSpec((None, tm, n), lambda j, i: (j, i, 0)),
                                    pl.BlockSpec((None, tm, n), lambda j, i: (j + half, i, 0))],
                          out_specs=pl.BlockSpec((tm, n), lambda j, i: (i, j)),
                          out_shape=jax.ShapeDtypeStruct((S, half * n), F32),
                          compiler_params=_params(("parallel", "parallel")), name=name)(h_sm, h_sm)


def conv_ln_silu_fwd(name, u, w, b, g, beta, layer, ts=128):
    S, D = u.shape
    kw = w.shape[1]
    ts = _tile(S, ts, CONV_HALO)
    lc = LANES if D % LANES == 0 else D

    def body(h_ref, u_ref, w_ref, b_ref, g_ref, be_ref, c_ref, s_ref, win_ref):
        i = pl.program_id(0)
        win_ref[0:CONV_HALO, :] = jnp.where(i == 0, 0.0, h_ref[...])
        win_ref[CONV_HALO:, :] = u_ref[...]
        for cc in range(D // lc):
            cs = slice(cc * lc, (cc + 1) * lc)
            acc = jnp.zeros((ts, lc), F32) + b_ref[:, cs]
            for k in range(kw):
                off = CONV_HALO - (kw - 1) + k
                acc = acc + w_ref[k:k + 1, cs] * win_ref[off:off + ts, cs]
            c_ref[:, cs] = acc
        c = c_ref[...]
        mu = jnp.mean(c, axis=-1, keepdims=True)
        d = c - mu
        var = jnp.mean(d * d, axis=-1, keepdims=True)
        nrm = d * lax.rsqrt(var + LN_EPS) * g_ref[...] + be_ref[...]
        s_ref[...] = (nrm * _sigmoid(nrm)).astype(BF16)

    row = pl.BlockSpec((ts, D), lambda i: (i, 0))
    vec = pl.BlockSpec((None, 1, D), lambda i: (layer, 0, 0))
    halo = pl.BlockSpec((CONV_HALO, D), lambda i: (jnp.maximum(i * (ts // CONV_HALO) - 1, 0), 0))
    return pl.pallas_call(body, grid=(S // ts,),
                          in_specs=[halo, row, pl.BlockSpec((None, kw, D), lambda i: (layer, 0, 0)), vec, vec, vec],
                          out_specs=(row, row),
                          out_shape=(jax.ShapeDtypeStruct((S, D), F32), jax.ShapeDtypeStruct((S, D), BF16)),
                          scratch_shapes=[pltpu.VMEM((ts + CONV_HALO, D), F32)],
                          compiler_params=_params(("parallel",)), name=name)(u, u, w, b, g, beta)


def ln_silu_bwd(name, ds, c, g, beta, layer, tm=256):
    S, D = c.shape
    tm = _tile(S, tm)

    def body(ds_ref, c_ref, g_ref, be_ref, dc_ref, acc_ref):
        c_ = c_ref[...]
        mu = jnp.mean(c_, axis=-1, keepdims=True)
        d = c_ - mu
        var = jnp.mean(d * d, axis=-1, keepdims=True)
        rstd = lax.rsqrt(var + LN_EPS)
        xh = d * rstd
        nrm = xh * g_ref[...] + be_ref[...]
        sg = _sigmoid(nrm)
        dn = ds_ref[...] * (sg * (1.0 + nrm * (1.0 - sg)))
        dxh = dn * g_ref[...]
        m1 = jnp.mean(dxh, axis=-1, keepdims=True)
        m2 = jnp.mean(dxh * xh, axis=-1, keepdims=True)
        dc = rstd * (dxh - m1 - xh * m2)
        dc_ref[...] = dc

        @pl.when(pl.program_id(0) == 0)
        def _():
            acc_ref[...] = jnp.zeros_like(acc_ref)
        acc_ref[0:1, :] += jnp.sum(dn * xh, axis=0, keepdims=True)
        acc_ref[1:2, :] += jnp.sum(dn, axis=0, keepdims=True)
        acc_ref[2:3, :] += jnp.sum(dc, axis=0, keepdims=True)

    row = pl.BlockSpec((tm, D), lambda i: (i, 0))
    vec = pl.BlockSpec((None, 1, D), lambda i: (layer, 0, 0))
    return pl.pallas_call(body, grid=(S // tm,), in_specs=[row, row, vec, vec],
                          out_specs=(row, pl.BlockSpec((8, D), lambda i: (0, 0))),
                          out_shape=(jax.ShapeDtypeStruct((S, D), F32), jax.ShapeDtypeStruct((8, D), F32)),
                          compiler_params=_params(("arbitrary",)), name=name)(ds, c, g, beta)


def conv_glu_bwd(name, dc, u, h_sm, w, layer, ts=128):
    S, D = dc.shape
    kw = w.shape[1]
    half = N_CHIPS // 2
    n = D // half
    ts = _tile(S, ts, CONV_HALO)
    nblk = S // ts
    lc = LANES if n % LANES == 0 else n

    def body(dc_ref, dcn_ref, u_ref, a_ref, g_ref, w_ref, da_ref, dg_ref, dw_ref, dba_ref, dbg_ref, dwin_ref):
        i = pl.program_id(1)
        dwin_ref[0:ts, :] = dc_ref[...]
        dwin_ref[ts:, :] = jnp.where(i == nblk - 1, 0.0, dcn_ref[...])

        @pl.when(i == 0)
        def _():
            dw_ref[...] = jnp.zeros_like(dw_ref)
            dba_ref[...] = jnp.zeros_like(dba_ref)
            dbg_ref[...] = jnp.zeros_like(dbg_ref)

        for cc in range(n // lc):
            cs = slice(cc * lc, (cc + 1) * lc)
            ub = u_ref[:, cs]
            du = jnp.zeros((ts, lc), F32)
            for k in range(kw):
                shifted = dwin_ref[kw - 1 - k:kw - 1 - k + ts, cs]
                du = du + w_ref[k:k + 1, cs] * shifted
                dw_ref[k:k + 1, cs] += jnp.sum(shifted * ub, axis=0, keepdims=True)
            a = a_ref[:, cs]
            sg = _sigmoid(g_ref[:, cs])
            da = du * sg
            dg = du * a * sg * (1.0 - sg)
            da_ref[:, cs] = da.astype(BF16)
            dg_ref[:, cs] = dg.astype(BF16)
            dba_ref[0:1, cs] += jnp.sum(da, axis=0, keepdims=True)
            dbg_ref[0:1, cs] += jnp.sum(dg, axis=0, keepdims=True)

    r = ts // CONV_HALO
    main = pl.BlockSpec((ts, n), lambda j, i: (i, j))
    nxt = pl.BlockSpec((CONV_HALO, n), lambda j, i: (jnp.minimum((i + 1) * r, S // CONV_HALO - 1), j))
    sm_a = pl.BlockSpec((None, ts, n), lambda j, i: (j, i, 0))
    sm_g = pl.BlockSpec((None, ts, n), lambda j, i: (j + half, i, 0))
    da, dg, dw, dba, dbg = pl.pallas_call(
        body, grid=(half, nblk),
        in_specs=[main, nxt, main, sm_a, sm_g, pl.BlockSpec((None, kw, n), lambda j, i: (layer, 0, j))],
        out_specs=(pl.BlockSpec((None, ts, n), lambda j, i: (j, i, 0)), pl.BlockSpec((None, ts, n), lambda j, i: (j, i, 0)),
                   pl.BlockSpec((None, 32, n), lambda j, i: (j, 0, 0)),
                   pl.BlockSpec((None, 8, n), lambda j, i: (j, 0, 0)), pl.BlockSpec((None, 8, n), lambda j, i: (j, 0, 0))),
        out_shape=(jax.ShapeDtypeStruct((half, S, n), BF16), jax.ShapeDtypeStruct((half, S, n), BF16),
                   jax.ShapeDtypeStruct((half, 32, n), F32),
                   jax.ShapeDtypeStruct((half, 8, n), F32), jax.ShapeDtypeStruct((half, 8, n), F32)),
        scratch_shapes=[pltpu.VMEM((ts + CONV_HALO, n), F32)],
        compiler_params=_params(("parallel", "arbitrary")), name=name)(dc, dc, u, h_sm, h_sm, w)
    return da, dg, dw, dba, dbg


ROW_CHUNK = 16


def _ffn_gc(win_ref, w_ref, b_ref, r0, rows, kw, base):
    gc = b_ref[...] + jnp.zeros((rows, win_ref.shape[1]), F32)
    for k in range(kw):
        off = r0 + base - (kw - 1) + k
        gc = gc + w_ref[k:k + 1, :] * win_ref[off:off + rows, :]
    return gc


def ffn_gate_fwd(name, up_sm, gp_sm, w4, b4, layer, ts=256):
    _, S, n = up_sm.shape
    kw = w4.shape[2]
    ts = _tile(S, ts, ROW_CHUNK)
    rc = ROW_CHUNK

    def body(up_ref, gp_ref, gph_ref, w_ref, b_ref, hf_ref, gc_ref, win_ref):
        i = pl.program_id(1)
        win_ref[0:FFN_HALO, :] = jnp.where(i == 0, 0.0, gph_ref[...].astype(F32))
        win_ref[FFN_HALO:, :] = gp_ref[...].astype(F32)
        for r0 in range(0, ts, rc):
            gc = _ffn_gc(win_ref, w_ref, b_ref, r0, rc, kw, FFN_HALO)
            gc_ref[r0:r0 + rc, :] = gc.astype(BF16)
            hf_ref[r0:r0 + rc, :] = (gc * _sigmoid(gc) * up_ref[r0:r0 + rc, :].astype(F32)).astype(BF16)

    main = pl.BlockSpec((None, ts, n), lambda j, i: (j, i, 0))
    prv = pl.BlockSpec((None, FFN_HALO, n), lambda j, i: (j, jnp.maximum(i * (ts // FFN_HALO) - 1, 0), 0))
    sds = jax.ShapeDtypeStruct((N_CHIPS, S, n), BF16)
    return pl.pallas_call(body, grid=(N_CHIPS, S // ts),
                          in_specs=[main, main, prv, pl.BlockSpec((None, None, kw, n), lambda j, i: (j, layer, 0, 0)),
                                    pl.BlockSpec((None, None, 1, n), lambda j, i: (layer, j, 0, 0))],
                          out_specs=(main, main), out_shape=(sds, sds),
                          scratch_shapes=[pltpu.VMEM((ts + FFN_HALO, n), F32)],
                          compiler_params=_params(("parallel", "parallel")), name=name)(up_sm, gp_sm, gp_sm, w4, b4)


def ffn_gate_bwd_a(name, dhf_sm, up_sm, gc_sm, ts=256):
    _, S, n = up_sm.shape
    ts = _tile(S, ts, ROW_CHUNK)
    rc = ROW_CHUNK

    def body(dhf_ref, up_ref, gc_ref, dup_ref, dgc_ref):
        for r0 in range(0, ts, rc):
            rows = slice(r0, r0 + rc)
            gc = gc_ref[rows, :].astype(F32)
            sg = _sigmoid(gc)
            dhf = dhf_ref[rows, :].astype(F32)
            dup_ref[rows, :] = (dhf * gc * sg).astype(BF16)
            dgc_ref[rows, :] = (dhf * up_ref[rows, :].astype(F32) * (sg * (1.0 + gc * (1.0 - sg)))).astype(BF16)

    main = pl.BlockSpec((None, ts, n), lambda j, i: (j, i, 0))
    sds = jax.ShapeDtypeStruct((N_CHIPS, S, n), BF16)
    return pl.pallas_call(body, grid=(N_CHIPS, S // ts), in_specs=[main, main, main], out_specs=(main, main), out_shape=(sds, sds),
                          compiler_params=_params(("parallel", "parallel")), name=name)(dhf_sm, up_sm, gc_sm)


def ffn_up_gate_fwd(name, xb, w_up, w_gate, cw4, cb4, layer, tm=256):
    S, K = xb.shape
    n = w_up.shape[-1]
    kw = cw4.shape[2]
    tm = _tile(S, tm, ROW_CHUNK)
    rc = ROW_CHUNK

    def body(*refs):
        x_ref, xh_ref = refs[0:2]
        wu, wg, cw, cb = refs[2:6], refs[6:10], refs[10:14], refs[14:18]
        up_ref, gp_ref, gc_ref, hf_ref, win_ref = refs[18:]
        i = pl.program_id(0)
        xv = x_ref[...]
        xhalo = xh_ref[...]
        for j in range(N_CHIPS):
            wgj = wg[j][...]
            gp = jnp.dot(xv, wgj, preferred_element_type=F32)
            up = jnp.dot(xv, wu[j][...], preferred_element_type=F32)
            gph = jnp.dot(xhalo, wgj, preferred_element_type=F32)
            gpb = gp.astype(BF16)
            gp_ref[j] = gpb
            up_ref[j] = up.astype(BF16)
            win_ref[0:FFN_HALO, :] = jnp.where(i == 0, 0.0, gph.astype(BF16).astype(F32))
            win_ref[FFN_HALO:, :] = gpb.astype(F32)
            for r0 in range(0, tm, rc):
                gc = _ffn_gc(win_ref, cw[j], cb[j], r0, rc, kw, FFN_HALO)
                gc_ref[j, r0:r0 + rc, :] = gc.astype(BF16)
                hf_ref[j, r0:r0 + rc, :] = (gc * _sigmoid(gc) * up_ref[j, r0:r0 + rc, :].astype(F32)).astype(BF16)

    r = tm // FFN_HALO
    out = pl.BlockSpec((N_CHIPS, tm, n), lambda i: (0, i, 0))
    sds = jax.ShapeDtypeStruct((N_CHIPS, S, n), BF16)
    specs = [pl.BlockSpec((tm, K), lambda i: (i, 0)), pl.BlockSpec((FFN_HALO, K), lambda i: (jnp.maximum(i * r - 1, 0), 0))]
    specs += [pl.BlockSpec((None, None, K, n), lambda i, j=j: (j, layer, 0, 0)) for j in range(N_CHIPS)] * 2
    specs += [pl.BlockSpec((None, None, kw, n), lambda i, j=j: (j, layer, 0, 0)) for j in range(N_CHIPS)]
    specs += [pl.BlockSpec((None, None, 1, n), lambda i, j=j: (layer, j, 0, 0)) for j in range(N_CHIPS)]
    return pl.pallas_call(body, grid=(S // tm,), in_specs=specs, out_specs=(out, out, out, out), out_shape=(sds, sds, sds, sds),
                          scratch_shapes=[pltpu.VMEM((tm + FFN_HALO, n), F32)],
                          compiler_params=_params(("parallel",)), name=name)(
                              xb, xb, *([w_up] * N_CHIPS), *([w_gate] * N_CHIPS), *([cw4] * N_CHIPS), *([cb4] * N_CHIPS))


def ffn_dhf_gate_bwd(name, dr, w_down, layer, up_sm, gc_sm, tm=512):
    n, N = w_down.shape[-2:]
    S = dr.shape[0]
    tm = _tile(S, tm, ROW_CHUNK)

    def body(dr_ref, w0, w1, w2, w3, up_ref, gc_ref, dup_ref, dgc_ref):
        a = dr_ref[...].astype(BF16)
        for j, w_ref in enumerate((w0, w1, w2, w3)):
            dhf = lax.dot_general(a, w_ref[...], NT, preferred_element_type=F32)
            gc = gc_ref[j].astype(F32)
            sg = _sigmoid(gc)
            dup_ref[j] = (dhf * gc * sg).astype(BF16)
            dgc_ref[j] = (dhf * up_ref[j].astype(F32) * (sg * (1.0 + gc * (1.0 - sg)))).astype(BF16)

    blk = pl.BlockSpec((N_CHIPS, tm, n), lambda i: (0, i, 0))
    sds = jax.ShapeDtypeStruct((N_CHIPS, S, n), BF16)
    specs = [pl.BlockSpec((tm, N), lambda i: (i, 0))]
    specs += [pl.BlockSpec((None, None, n, N), lambda i, j=j: (j, layer, 0, 0)) for j in range(N_CHIPS)] + [blk, blk]
    return pl.pallas_call(body, grid=(S // tm,), in_specs=specs, out_specs=(blk, blk), out_shape=(sds, sds),
                          compiler_params=_params(("parallel",)), name=name)(dr, *([w_down] * N_CHIPS), up_sm, gc_sm)


def ffn_gate_bwd_b(name, dgc_sm, gp_sm, w4, layer, ts=256):
    _, S, n = gp_sm.shape
    kw = w4.shape[2]
    ts = _tile(S, ts, ROW_CHUNK)
    rc = ROW_CHUNK
    nblk = S // ts

    def body(dgc_ref, dgn_ref, gp_ref, w_ref, dgp_ref, acc_ref, dwin_ref):
        i = pl.program_id(1)
        dwin_ref[0:ts, :] = dgc_ref[...].astype(F32)
        dwin_ref[ts:, :] = jnp.where(i == nblk - 1, 0.0, dgn_ref[...].astype(F32))

        @pl.when(i == 0)
        def _():
            acc_ref[...] = jnp.zeros_like(acc_ref)
        sums = [jnp.zeros((8, n), F32) for _ in range(kw + 1)]
        for r0 in range(0, ts, rc):
            gp = gp_ref[r0:r0 + rc, :].astype(F32)
            dgp = jnp.zeros((rc, n), F32)
            for k in range(kw):
                d = kw - 1 - k
                shifted = dwin_ref[r0 + d:r0 + d + rc, :]
                dgp = dgp + w_ref[k:k + 1, :] * shifted
                prod = shifted * gp
                sums[k] = sums[k] + prod[0:8, :] + prod[8:16, :]
                if d == 0:
                    sums[kw] = sums[kw] + shifted[0:8, :] + shifted[8:16, :]
            dgp_ref[r0:r0 + rc, :] = dgp.astype(BF16)
        for k in range(kw):
            acc_ref[k:k + 1, :] += jnp.sum(sums[k], axis=0, keepdims=True)
        acc_ref[7:8, :] += jnp.sum(sums[kw], axis=0, keepdims=True)

    r = ts // FFN_HALO
    main = pl.BlockSpec((None, ts, n), lambda j, i: (j, i, 0))
    nxt = pl.BlockSpec((None, FFN_HALO, n), lambda j, i: (j, jnp.minimum((i + 1) * r, S // FFN_HALO - 1), 0))
    return pl.pallas_call(body, grid=(N_CHIPS, nblk),
                          in_specs=[main, nxt, main, pl.BlockSpec((None, None, kw, n), lambda j, i: (j, layer, 0, 0))],
                          out_specs=(main, pl.BlockSpec((None, 8, n), lambda j, i: (j, 0, 0))),
                          out_shape=(jax.ShapeDtypeStruct((N_CHIPS, S, n), BF16), jax.ShapeDtypeStruct((N_CHIPS, 8, n), F32)),
                          scratch_shapes=[pltpu.VMEM((ts + FFN_HALO, n), F32)],
                          compiler_params=_params(("parallel", "arbitrary")), name=name)(dgc_sm, dgc_sm, gp_sm, w4)


def _neg_softplus(z):
    e = jnp.exp(-jnp.abs(z))
    return -(jnp.maximum(z, 0.0) + jnp.log(1.0 + e)), e


def _split_dot(x, t):
    hi = x.astype(BF16)
    lo = (x - hi.astype(F32)).astype(BF16)
    return jnp.dot(hi, t, preferred_element_type=F32) + jnp.dot(lo, t, preferred_element_type=F32)


STICK_GONE = -100.0
NOT_SWEPT = -1e30


def attn_fwd(name, q, k, v, bq=512, w=256):
    S, D = q.shape
    dh = HEAD_DIM
    hpb = LANES // dh
    bq = _tile(S, bq)
    w = _tile(bq, w)
    nsub = bq // w
    nkb = S // w

    def body(q_ref, k_ref, v_ref, o_ref, runs_ref, rs_ref):
        qi = pl.program_id(1)
        rr = lax.broadcasted_iota(jnp.int32, (w, w), 0)
        cc = lax.broadcasted_iota(jnp.int32, (w, w), 1)
        t_suf = (rr >= cc).astype(BF16)
        tq = qi * bq + lax.broadcasted_iota(jnp.int32, (bq, w), 0)
        tk = lax.broadcasted_iota(jnp.int32, (bq, w), 1)
        lane = lax.broadcasted_iota(jnp.int32, (bq, LANES), 1)
        ntot = (qi + 1) * nsub
        heads = [slice(hh * dh, (hh + 1) * dh) for hh in range(hpb)]
        qbs = [q_ref[:, hs] for hs in heads]
        for hh in range(hpb):
            rs_ref[hh] = jnp.where(lane < ntot, NOT_SWEPT, 0.0)

        def block(kb, carry, masked):
            kstart = pl.multiple_of(kb * w, w)
            if masked:
                m = (tk + kstart) < tq
            out = []
            for hh, hs in enumerate(heads):
                run, acc = carry[2 * hh], carry[2 * hh + 1]
                kblk = k_ref[pl.ds(kstart, w), hs]
                vblk = v_ref[pl.ds(kstart, w), hs]
                z = lax.dot_general(qbs[hh], kblk, NT, preferred_element_type=F32)
                lg, _ = _neg_softplus(z)
                if masked:
                    lg = jnp.where(m, lg, 0.0)
                cum = _split_dot(lg, t_suf) + run
                a = jnp.exp(z + cum)
                if masked:
                    a = jnp.where(m, a, 0.0)
                acc = acc + jnp.dot(a.astype(BF16), vblk, preferred_element_type=F32)
                run = cum[:, 0:1]
                rs_ref[hh] = jnp.where(lane == kb, run, rs_ref[hh])
                out += [run, acc]
            return tuple(out)

        carry = (jnp.zeros((bq, 1), F32), jnp.zeros((bq, dh), F32)) * hpb
        for sb in reversed(range(nsub)):
            carry = block(qi * nsub + sb, carry, True)

        def cond(c):
            alive = functools.reduce(jnp.maximum, [jnp.max(c[1 + 2 * hh]) for hh in range(hpb)])
            return jnp.logical_and(c[0] >= 0, alive > STICK_GONE)

        def step(c):
            return (c[0] - 1,) + block(c[0], c[1:], False)
        carry = lax.while_loop(cond, step, (qi * nsub - 1,) + carry)[1:]
        for hh, hs in enumerate(heads):
            o_ref[:, hs] = carry[2 * hh + 1].astype(o_ref.dtype)
            runs_ref[hh] = rs_ref[hh, :, 0:nkb]

    qs = pl.BlockSpec((bq, LANES), lambda h, i: (i, h))
    kv = pl.BlockSpec((S, LANES), lambda h, i: (0, h))
    return pl.pallas_call(body, grid=(D // LANES, S // bq), in_specs=[qs, kv, kv],
                          out_specs=(qs, pl.BlockSpec((hpb, bq, nkb), lambda h, i: (h, i, 0))),
                          out_shape=(jax.ShapeDtypeStruct((S, D), BF16), jax.ShapeDtypeStruct((D // dh, S, nkb), F32)),
                          scratch_shapes=[pltpu.VMEM((hpb, bq, LANES), F32)],
                          compiler_params=_params(("parallel", "parallel")), name=name)(q, k, v)


def attn_bwd(name, q, k, v, do, runs, dk0=None, dv0=None, bq=512, w=256):
    S, D = q.shape
    dh = HEAD_DIM
    hpb = LANES // dh
    bq = _tile(S, bq)
    w = _tile(bq, w)
    nsub = bq // w
    nkb = S // w
    scale = 1.0 / math.sqrt(dh)
    init = dk0 is not None

    def body(*refs):
        if init:
            q_ref, k_ref, v_ref, do_ref, runs_ref, dk0_ref, dv0_ref, dq_ref, dk_ref, dv_ref, rs_ref = refs
        else:
            q_ref, k_ref, v_ref, do_ref, runs_ref, dq_ref, dk_ref, dv_ref, rs_ref = refs
        qi = pl.program_id(1)

        @pl.when(qi == 0)
        def _():
            dk_ref[...] = dk0_ref[...] if init else jnp.zeros_like(dk_ref)
            dv_ref[...] = dv0_ref[...] if init else jnp.zeros_like(dv_ref)

        rr = lax.broadcasted_iota(jnp.int32, (w, w), 0)
        cc = lax.broadcasted_iota(jnp.int32, (w, w), 1)
        t_suf = (rr >= cc).astype(BF16)
        t_pre = (rr <= cc).astype(BF16)
        tq = qi * bq + lax.broadcasted_iota(jnp.int32, (bq, w), 0)
        tk = lax.broadcasted_iota(jnp.int32, (bq, w), 1)
        lane = lax.broadcasted_iota(jnp.int32, (bq, LANES), 1)
        lane1 = lax.broadcasted_iota(jnp.int32, (1, LANES), 1)
        ntot = (qi + 1) * nsub
        heads = [slice(hh * dh, (hh + 1) * dh) for hh in range(hpb)]
        qbs = [q_ref[:, hs] for hs in heads]
        dobs = [do_ref[:, hs].astype(BF16) for hs in heads]
        kb0 = ntot - nsub
        for hh in range(hpb):
            rs_ref[hh] = jnp.zeros((bq, LANES), F32)
            rs_ref[hh, :, 0:nkb] = runs_ref[hh]
            colmax = jnp.max(rs_ref[hh], axis=0, keepdims=True)
            dead = jnp.logical_and(jnp.logical_and(lane1 >= 1, lane1 <= ntot), colmax <= STICK_GONE)
            kb0 = jnp.minimum(kb0, jnp.sum(dead.astype(jnp.int32)))

        def block(kb, carry, masked):
            kstart = pl.multiple_of(kb * w, w)
            if masked:
                m = (tk + kstart) < tq
            out = []
            for hh, hs in enumerate(heads):
                pg_run, dq = carry[2 * hh], carry[2 * hh + 1]
                qb, dob = qbs[hh], dobs[hh]
                kblk = k_ref[pl.ds(kstart, w), hs]
                vblk = v_ref[pl.ds(kstart, w), hs]
                right = jnp.sum(jnp.where(lane == kb + 1, rs_ref[hh], 0.0), axis=1, keepdims=True)
                z = lax.dot_general(qb, kblk, NT, preferred_element_type=F32)
                lg, e = _neg_softplus(z)
                if masked:
                    lg = jnp.where(m, lg, 0.0)
                a = jnp.exp(z + _split_dot(lg, t_suf) + right)
                if masked:
                    a = jnp.where(m, a, 0.0)
                da = lax.dot_general(dob, vblk, NT, preferred_element_type=F32)
                g = da * a
                pin = _split_dot(g, t_pre) + pg_run
                sig = jnp.where(z >= 0.0, 1.0, e) / (1.0 + e)
                dz = g - sig * pin
                if masked:
                    dz = jnp.where(m, dz, 0.0)
                dzb = dz.astype(BF16)
                dq = dq + jnp.dot(dzb, kblk, preferred_element_type=F32)
                dk_ref[pl.ds(kstart, w), hs] += lax.dot_general(dzb, qb, TN, preferred_element_type=F32)
                dv_ref[pl.ds(kstart, w), hs] += lax.dot_general(a.astype(BF16), dob, TN, preferred_element_type=F32)
                out += [pin[:, w - 1:w], dq]
            return tuple(out)

        carry = (jnp.zeros((bq, 1), F32), jnp.zeros((bq, dh), F32)) * hpb
        carry = lax.fori_loop(kb0, qi * nsub, lambda kb, c: block(kb, c, False), carry)
        for sb in range(nsub):
            carry = block(qi * nsub + sb, carry, True)
        for hh, hs in enumerate(heads):
            dq_ref[:, hs] = carry[2 * hh + 1] * scale

    qs = pl.BlockSpec((bq, LANES), lambda h, i: (i, h))
    kv = pl.BlockSpec((S, LANES), lambda h, i: (0, h))
    ins = [q, k, v, do, runs] + ([dk0, dv0] if init else [])
    specs = [qs, kv, kv, qs, pl.BlockSpec((hpb, bq, nkb), lambda h, i: (h, i, 0))] + ([kv, kv] if init else [])
    sds = jax.ShapeDtypeStruct((S, D), F32)
    return pl.pallas_call(body, grid=(D // LANES, S // bq), in_specs=specs, out_specs=(qs, kv, kv), out_shape=(sds, sds, sds),
                          scratch_shapes=[pltpu.VMEM((hpb, bq, LANES), F32)],
                          compiler_params=_params(("parallel", "arbitrary")), name=name)(*ins)


def loss_head(name, y, tgt, tm=512):
    S, D = y.shape
    tm = _tile(S, tm)

    def body(y_ref, t_ref, dy_ref, acc_ref):
        @pl.when(pl.program_id(0) == 0)
        def _():
            acc_ref[...] = jnp.zeros_like(acc_ref)
        e = y_ref[...] - t_ref[...]
        dy_ref[...] = e * (1.0 / D)
        acc_ref[...] += jnp.sum(e * e)

    row = pl.BlockSpec((tm, D), lambda i: (i, 0))
    return pl.pallas_call(body, grid=(S // tm,), in_specs=[row, row],
                          out_specs=(row, pl.BlockSpec((8, LANES), lambda i: (0, 0))),
                          out_shape=(jax.ShapeDtypeStruct((S, D), F32), jax.ShapeDtypeStruct((8, LANES), F32)),
                          compiler_params=_params(("arbitrary",)), name=name)(y, tgt)


def _to_sm(a, axis=-1):
    axis = axis % a.ndim
    shp = a.shape[:axis] + (N_CHIPS, a.shape[axis] // N_CHIPS) + a.shape[axis + 1:]
    return jnp.moveaxis(a.reshape(shp), axis, 0)


def _from_sm(a, axis=-1):
    nd = a.ndim - 1
    axis = axis % nd
    b = jnp.moveaxis(a, 0, axis)
    return b.reshape(b.shape[:axis] + (b.shape[axis] * b.shape[axis + 1],) + b.shape[axis + 2:])


class GradBuffers:
    def __init__(self, W):
        groups = {}
        for n in BIG:
            _, layers, rows, cols = W[n].shape
            groups.setdefault(cols, []).append((rows, n, layers))
        self.where, self.cols_of, self.buf, self.members = {}, {}, {}, {}
        for cols, items in groups.items():
            off, members = 0, []
            for rows, n, layers in sorted(items, key=lambda t: -t[0]):
                assert off % rows == 0
                self.where[n], self.cols_of[n] = (off, rows), cols
                members.append((n, off, rows * layers))
                off += rows * layers
            assert off % 32 == 0
            self.buf[cols] = lax.empty((N_CHIPS, off, cols), F32)
            self.members[cols] = members

    def put(self, n, layer, fn, **kw):
        cols = self.cols_of[n]
        off, rows = self.where[n]
        self.buf[cols] = fn(into=self.buf[cols], off=off + layer * rows, **kw)


def forward_backward(x, p4, tgt, W):
    S, D = x.shape
    scale = 1.0 / math.sqrt(HEAD_DIM)
    saved = []
    kh = vh = xb_kv = None
    xb = x.astype(BF16)
    for i in range(DEPTH):
        sv = {'xb': xb}
        if i < N_A:
            h_sm = mm_colsm(f"pw1_{i}", xb, W['a_pw1_w'], i, W['a_pw1_b'])
            u = glu_fwd(f"glu_{i}", h_sm)
            c, s = conv_ln_silu_fwd(f"convln_{i}", u, W['a_dw_w'], W['a_dw_b'], W['a_ln_g'], W['a_ln_b'], i)
            mix = mm_rowsm(f"pw2_{i}", s, W['a_pw2_w'], i, bias=W['a_pw2_b'])
            sv.update(h_sm=h_sm, u=u, c=c, s=s)
        else:
            j = i - N_A
            if kh is None:
                xb_kv = xb
                kh = mm_rowsm("wk", xb, W['kv_wk'], 0, out_dtype=BF16)
                vh = mm_rowsm("wv", xb, W['kv_wv'], 0, out_dtype=BF16)
            qh = mm_rowsm(f"wq_{j}", xb, W['b_wq'], j, out_dtype=BF16, out_scale=scale)
            o, runs = attn_fwd(f"attn_{j}", qh, kh, vh)
            mix = mm_rowsm(f"wo_{j}", o, W['b_wo'], j)
            sv.update(qh=qh, o=o, runs=runs)
        x1, x1b, xh1, rs1 = ln_fwd(f"lnmix_{i}", x, mix, W['ln_mix_g'], W['ln_mix_b'], i)
        up_sm, gp_sm, gc_sm, hf_sm = ffn_up_gate_fwd(f"upgate_{i}", x1b, W['ffn_w_up'], W['ffn_w_gate'], W['ffn_conv_w'],
                                                     W['ffn_conv_b'], i)
        ffn = mm_rowsm(f"down_{i}", hf_sm, W['ffn_w_down'], i, a_sm=True)
        tg = mm_rowsm(f"plegate_{i}", x1b, W['ple_w_gate'], i)
        pp = mm_proj(f"pleproj_{i}", p4, i, W['ple_w_proj'])
        x2, x2b, xh2, rs2 = ln_fwd(f"lnffn_{i}", x1, ffn, W['ln_ffn_g'], W['ln_ffn_b'], i, tg=tg, pp=pp)
        sv.update(x1b=x1b, xh1=xh1, rs1=rs1, up_sm=up_sm, gp_sm=gp_sm, gc_sm=gc_sm, hf_sm=hf_sm, tg=tg, pp=pp, xh2=xh2, rs2=rs2)
        saved.append(sv)
        x, xb = x2, x2b

    dx, lacc = loss_head("loss", x, tgt)
    loss_sum = lacc[0, 0]

    G = {n: [None] * DEPTH for n in WEIGHTS if n not in BIG}
    gb = GradBuffers(W)
    dk = dv = None
    for i in reversed(range(DEPTH)):
        sv = saved[i]
        dr, acc, dtg, dpp = ln_bwd(f"lnffn_b_{i}", dx, sv['xh2'], sv['rs2'], W['ln_ffn_g'], i, tg=sv['tg'], pp=sv['pp'])
        G['ln_ffn_g'][i], G['ln_ffn_b'][i] = acc[0], acc[1]
        gb.put('ple_w_proj', i, functools.partial(mm_tn_proj, f"dproj_{i}", p4, i, dpp))
        gb.put('ple_w_gate', i, functools.partial(mm_tn_rowsm_fan, f"dplegate_{i}", sv['x1b'], dtg))
        gb.put('ffn_w_down', i, functools.partial(mm_tn_rowsm, f"ddown_{i}", sv['hf_sm'], dr, a_sm=True))
        dup_sm, dgc_sm = ffn_dhf_gate_bwd(f"dhfgate_{i}", dr, W['ffn_w_down'], i, sv['up_sm'], sv['gc_sm'])
        dgp_sm, cacc = ffn_gate_bwd_b(f"ffngate_bb_{i}", dgc_sm, sv['gp_sm'], W['ffn_conv_w'], i)
        kw = W['ffn_conv_w'].shape[2]
        G['ffn_conv_w'][i] = cacc[:, 0:kw, :]
        G['ffn_conv_b'][i] = cacc[:, 7, :].reshape(-1)
        gb.put('ffn_w_up', i, functools.partial(mm_tn_colsm, f"dup_{i}", sv['x1b'], dup_sm))
        gb.put('ffn_w_gate', i, functools.partial(mm_tn_colsm, f"dgate_{i}", sv['x1b'], dgp_sm))
        dx1 = mm_nt_rowsm(f"dx1a_{i}", dtg, W['ple_w_gate'], i, res=dr, res_alpha=DN_ALPHA)
        dx1 = mm_nt_colsm(f"dx1b_{i}", _sm_parts(dup_sm), W['ffn_w_up'], i, res=dx1)
        dx1 = mm_nt_colsm(f"dx1c_{i}", _sm_parts(dgp_sm), W['ffn_w_gate'], i, res=dx1)

        dr1, acc1 = ln_bwd(f"lnmix_b_{i}", dx1, sv['xh1'], sv['rs1'], W['ln_mix_g'], i)
        G['ln_mix_g'][i], G['ln_mix_b'][i] = acc1[0], acc1[1]
        xin = sv['xb']
        if i < N_A:
            G['a_pw2_b'][i] = acc1[2]
            gb.put('a_pw2_w', i, functools.partial(mm_tn_rowsm_fan, f"dpw2_{i}", sv['s'], dr1))
            ds = mm_nt_rowsm(f"ds_{i}", dr1, W['a_pw2_w'], i)
            dc, cacc = ln_silu_bwd(f"lnsilu_b_{i}", ds, sv['c'], W['a_ln_g'], W['a_ln_b'], i)
            G['a_ln_g'][i], G['a_ln_b'][i], G['a_dw_b'][i] = cacc[0], cacc[1], cacc[2]
            da, dg, dw, dba, dbg = conv_glu_bwd(f"convglu_b_{i}", dc, sv['u'], sv['h_sm'], W['a_dw_w'], i)
            kw = W['a_dw_w'].shape[1]
            G['a_dw_w'][i] = _from_sm(dw[:, 0:kw, :], axis=-1)
            G['a_pw1_b'][i] = jnp.concatenate([dba[:, 0, :], dbg[:, 0, :]], axis=0)
            half = da.shape[0]
            gb.put('a_pw1_w', i, functools.partial(mm_tn_colsm, f"dpw1a_{i}", xin, da))
            gb.put('a_pw1_w', i, functools.partial(mm_tn_colsm, f"dpw1g_{i}", xin, dg), j0=half)
            dx = mm_nt_colsm(f"dxa_{i}", _sm_parts(da) + _sm_parts(dg), W['a_pw1_w'], i, res=dr1, res_alpha=DN_ALPHA)
        else:
            j = i - N_A
            gb.put('b_wo', j, functools.partial(mm_tn_rowsm_fan, f"dwo_{j}", sv['o'], dr1))
            do = mm_nt_rowsm(f"do_{j}", dr1, W['b_wo'], j)
            dq, dk, dv = attn_bwd(f"attn_b_{j}", sv['qh'], kh, vh, do, sv['runs'], dk, dv)
            gb.put('b_wq', j, functools.partial(mm_tn_rowsm_fan, f"dwq_{j}", xin, dq))
            dx = mm_nt_rowsm(f"dxq_{j}", dq, W['b_wq'], j, res=dr1, res_alpha=DN_ALPHA)
            if j == 0:
                gb.put('kv_wk', 0, functools.partial(mm_tn_rowsm_fan, "dwk", xb_kv, dk))
                gb.put('kv_wv', 0, functools.partial(mm_tn_rowsm_fan, "dwv", xb_kv, dv))
                dx = mm_nt_rowsm("dxk", dk, W['kv_wk'], 0, res=dx)
                dx = mm_nt_rowsm("dxv", dv, W['kv_wv'], 0, res=dx)
    return loss_sum, dx, G, gb


MESH = pl.DeviceIdType.MESH
HBM = pl.BlockSpec(memory_space=pltpu.HBM)


def _place():
    x, y, c = lax.axis_index("x"), lax.axis_index("y"), lax.axis_index("c")
    others = [(1 - x, y), (x, 1 - y), (1 - x, 1 - y)]
    return x, y, c, others


def allgather_chips(name, arrs):
    n = len(arrs)

    def body(*refs):
        ins, outs = refs[:n], refs[n:2 * n]
        send_sems, recv_sems = refs[2 * n:]
        x, y, c, others = _place()
        me = 2 * x + y
        sibling = (x, y, 1 - c)
        ids = [2 * ch[0] + ch[1] for ch in others]
        from_id = jnp.where(c == 0, ids[0], ids[1])
        to_chip = (jnp.where(c == 0, x, 1 - x), jnp.where(c == 0, 1 - y, y))

        def remote(a, k, src, chip_id, half, to):
            return pltpu.make_async_remote_copy(src_ref=src, dst_ref=outs[a].at[chip_id, half], send_sem=send_sems.at[a, k],
                                                recv_sem=recv_sems.at[a, k], device_id=to, device_id_type=MESH)

        sent = [remote(a, k, ins[a].at[c], me, c, (others[k][0], others[k][1], c)) for a in range(n) for k in range(2)]
        for cp in sent:
            cp.start()
        for a in range(n):
            for k in range(2):
                remote(a, k, ins[a].at[c], ids[k], c, sibling).wait_recv()
            sent.append(remote(a, 2, outs[a].at[from_id, c], from_id, c, (to_chip[0], to_chip[1], c)))
            sent[-1].start()
            for k in range(2):
                sent.append(remote(a, 3 + k, outs[a].at[ids[k], c], ids[k], c, sibling))
                sent[-1].start()
        for a in range(n):
            remote(a, 2, ins[a].at[c], ids[2], c, sibling).wait_recv()
            sent.append(remote(a, 5, outs[a].at[ids[2], c], ids[2], c, sibling))
            sent[-1].start()
        for a in range(n):
            for k in range(3):
                remote(a, 3 + k, ins[a].at[c], ids[k], 1 - c, sibling).wait_recv()
        for cp in sent:
            cp.wait_send()

    outs = pl.pallas_call(body, out_shape=tuple(jax.ShapeDtypeStruct((N_CHIPS,) + a.shape, a.dtype) for a in arrs),
                          in_specs=[HBM] * n, out_specs=tuple([HBM] * n),
                          scratch_shapes=[pltpu.SemaphoreType.DMA((n, 6)), pltpu.SemaphoreType.DMA((n, 6))],
                          name=name)(*arrs)
    me = 2 * lax.axis_index("x") + lax.axis_index("y")
    return [lax.dynamic_update_index_in_dim(o, a, me, 0) for o, a in zip(outs, arrs)]


def exchange_sibling(name, gs):
    n = len(gs)

    def body(*refs):
        g_refs, o_refs = refs[:n], refs[n:2 * n]
        send_sems, recv_sems = refs[2 * n:]
        x, y, c, _ = _place()
        cps = [pltpu.make_async_remote_copy(src_ref=g_refs[a].at[j, 1 - c], dst_ref=o_refs[a].at[j], send_sem=send_sems.at[a, j],
                                            recv_sem=recv_sems.at[a, j], device_id=(x, y, 1 - c), device_id_type=MESH)
               for a in range(n) for j in range(N_CHIPS)]
        for cp in cps:
            cp.start()
        for cp in cps:
            cp.wait()

    return pl.pallas_call(body, out_shape=tuple(jax.ShapeDtypeStruct((N_CHIPS,) + g.shape[2:], g.dtype) for g in gs),
                          in_specs=[HBM] * n, out_specs=tuple([HBM] * n),
                          scratch_shapes=[pltpu.SemaphoreType.DMA((n, N_CHIPS)), pltpu.SemaphoreType.DMA((n, N_CHIPS))],
                          name=name)(*gs)


def _ring_peers():
    x, y, c, _ = _place()
    first = (jnp.where(c == 0, 1 - x, x), jnp.where(c == 0, y, 1 - y))
    second = (jnp.where(c == 0, x, 1 - x), jnp.where(c == 0, 1 - y, y))
    return c, first, second, 2 * (1 - x) + (1 - y)


def exchange_first(name, ss):
    n = len(ss)

    def body(*refs):
        s_refs, o_refs = refs[:n], refs[n:2 * n]
        send_sems, recv_sems = refs[2 * n:]
        c, first, _, diag = _ring_peers()
        cps = [pltpu.make_async_remote_copy(src_ref=s_refs[a].at[slot], dst_ref=o_refs[a].at[k], send_sem=send_sems.at[a, k],
                                            recv_sem=recv_sems.at[a, k], device_id=(first[0], first[1], c), device_id_type=MESH)
               for a in range(n) for k, slot in enumerate((2 * first[0] + first[1], diag))]
        for cp in cps:
            cp.start()
        for cp in cps:
            cp.wait()

    return pl.pallas_call(body, out_shape=tuple(jax.ShapeDtypeStruct((2,) + s.shape[1:], s.dtype) for s in ss),
                          in_specs=[HBM] * n, out_specs=tuple([HBM] * n),
                          scratch_shapes=[pltpu.SemaphoreType.DMA((n, 2)), pltpu.SemaphoreType.DMA((n, 2))], name=name)(*ss)


def exchange_second(name, ts):
    n = len(ts)

    def body(*refs):
        t_refs, o_refs = refs[:n], refs[n:2 * n]
        send_sems, recv_sems = refs[2 * n:]
        c, _, second, _ = _ring_peers()
        cps = [pltpu.make_async_remote_copy(src_ref=t_refs[a], dst_ref=o_refs[a], send_sem=send_sems.at[a], recv_sem=recv_sems.at[a],
                                            device_id=(second[0], second[1], c), device_id_type=MESH) for a in range(n)]
        for cp in cps:
            cp.start()
        for cp in cps:
            cp.wait()

    return pl.pallas_call(body, out_shape=tuple(jax.ShapeDtypeStruct(t.shape, t.dtype) for t in ts),
                          in_specs=[HBM] * n, out_specs=tuple([HBM] * n),
                          scratch_shapes=[pltpu.SemaphoreType.DMA((n,)), pltpu.SemaphoreType.DMA((n,))], name=name)(*ts)


def share_sibling(name, ts):
    n = len(ts)

    def body(*refs):
        o_refs = refs[n:2 * n]
        send_sems, recv_sems = refs[2 * n:]
        x, y, c, _ = _place()
        cps = [pltpu.make_async_remote_copy(src_ref=o_refs[a].at[c], dst_ref=o_refs[a].at[c], send_sem=send_sems.at[a],
                                            recv_sem=recv_sems.at[a], device_id=(x, y, 1 - c), device_id_type=MESH)
               for a in range(n)]
        for cp in cps:
            cp.start()
        for a in range(n):
            pltpu.make_async_remote_copy(src_ref=o_refs[a].at[c], dst_ref=o_refs[a].at[1 - c], send_sem=send_sems.at[a],
                                         recv_sem=recv_sems.at[a], device_id=(x, y, 1 - c), device_id_type=MESH).wait_recv()
        for cp in cps:
            cp.wait_send()

    return pl.pallas_call(body, out_shape=tuple(jax.ShapeDtypeStruct(t.shape, t.dtype) for t in ts),
                          in_specs=[HBM] * n, out_specs=tuple([HBM] * n), input_output_aliases={a: a for a in range(n)},
                          scratch_shapes=[pltpu.SemaphoreType.DMA((n,)), pltpu.SemaphoreType.DMA((n,))],
                          name=name)(*ts)


def add_halves(name, g, recv, place, out_dtype, tr=512):
    _, _, R, C = g.shape
    tr = _tile(R, tr, 16)

    def body(p_ref, a_ref, b_ref, o_ref):
        o_ref[...] = (a_ref[...] + b_ref[...]).astype(o_ref.dtype)

    blk = pl.BlockSpec((None, tr, C), lambda j, i, p: (j, i, 0))
    gs = pltpu.PrefetchScalarGridSpec(num_scalar_prefetch=1, grid=(N_CHIPS, R // tr),
                                      in_specs=[pl.BlockSpec((None, None, tr, C), lambda j, i, p: (j, p[0], i, 0)), blk],
                                      out_specs=blk)
    return pl.pallas_call(body, grid_spec=gs, out_shape=jax.ShapeDtypeStruct((N_CHIPS, R, C), out_dtype),
                          compiler_params=_params(("parallel", "parallel")), name=name)(place, g, recv)


def add_pass_on(name, s, got, place, tr=512):
    _, R, C = s.shape
    tr = _tile(R, tr, 16)

    def body(p_ref, a_ref, b_ref, o_ref):
        o_ref[...] = (a_ref[...].astype(F32) + b_ref[...].astype(F32)).astype(o_ref.dtype)

    gs = pltpu.PrefetchScalarGridSpec(num_scalar_prefetch=1, grid=(R // tr,),
                                      in_specs=[pl.BlockSpec((None, tr, C), lambda i, p: (p[2], i, 0)),
                                                pl.BlockSpec((None, tr, C), lambda i, p: (1, i, 0))],
                                      out_specs=pl.BlockSpec((tr, C), lambda i, p: (i, 0)))
    return pl.pallas_call(body, grid_spec=gs, out_shape=jax.ShapeDtypeStruct((R, C), s.dtype),
                          compiler_params=_params(("parallel",)), name=name)(place, s, got)


def add_chips(name, g, r1, got1, got2, place, tr=512):
    _, _, R, C = g.shape
    tr = _tile(R, tr, 16)

    def body(p_ref, a_ref, b_ref, c_ref, d_ref, o_ref):
        o_ref[...] = ((a_ref[...] + b_ref[...]) + c_ref[...].astype(F32)) + d_ref[...].astype(F32)

    gs = pltpu.PrefetchScalarGridSpec(num_scalar_prefetch=1, grid=(R // tr,),
                                      in_specs=[pl.BlockSpec((None, None, tr, C), lambda i, p: (p[1], p[0], i, 0)),
                                                pl.BlockSpec((None, tr, C), lambda i, p: (p[1], i, 0)),
                                                pl.BlockSpec((None, tr, C), lambda i, p: (0, i, 0)),
                                                pl.BlockSpec((tr, C), lambda i, p: (i, 0))],
                                      out_specs=pl.BlockSpec((None, tr, C), lambda i, p: (p[0], i, 0)))
    return pl.pallas_call(body, grid_spec=gs, out_shape=jax.ShapeDtypeStruct((2, R, C), F32),
                          compiler_params=_params(("parallel",)), name=name)(place, g, r1, got1, got2)


def reduce_scatter(gs, wire_dtypes, place):
    r1 = exchange_sibling("rs_sibling", gs)
    s1 = [add_halves(f"rs_add_cores_{a}", g, r, place, dt) for a, (g, r, dt) in enumerate(zip(gs, r1, wire_dtypes))]
    got1 = exchange_first("rs_first", s1)
    t = [add_pass_on(f"rs_add_pass_{a}", s, g1, place) for a, (s, g1) in enumerate(zip(s1, got1))]
    got2 = exchange_second("rs_second", t)
    tot = [add_chips(f"rs_add_chips_{a}", g, r, g1, g2, place) for a, (g, r, g1, g2) in enumerate(zip(gs, r1, got1, got2))]
    return share_sibling("rs_share", tot)


def adamw(name, w, g, m, v, tr=512):
    shp = w.shape
    cols = shp[-1]
    w2, g2, m2, v2 = (a.reshape(-1, cols) for a in (w, g, m, v))
    rows = w2.shape[0]
    tr = _tile(rows, tr)

    def body(w_ref, g_ref, m_ref, v_ref, d_ref, mo_ref, vo_ref):
        g_ = g_ref[...]
        m_ = ADAM_B1 * m_ref[...] + (1.0 - ADAM_B1) * g_
        v_ = ADAM_B2 * v_ref[...] + (1.0 - ADAM_B2) * (g_ * g_)
        m_hat = m_ / (1.0 - ADAM_B1 ** ADAM_STEP)
        v_hat = v_ / (1.0 - ADAM_B2 ** ADAM_STEP)
        d_ref[...] = -ADAM_LR * (m_hat / (jnp.sqrt(v_hat) + ADAM_EPS) + ADAM_WD * w_ref[...])
        mo_ref[...] = m_
        vo_ref[...] = v_

    blk = pl.BlockSpec((tr, cols), lambda i: (i, 0))
    sds = jax.ShapeDtypeStruct((rows, cols), F32)
    d, mo, vo = pl.pallas_call(body, grid=(rows // tr,), in_specs=[blk] * 4, out_specs=(blk, blk, blk), out_shape=(sds, sds, sds),
                               compiler_params=_params(("parallel",)), name=name)(w2, g2, m2, v2)
    return d.reshape(shp), mo.reshape(shp), vo.reshape(shp)


PACK_ALIGN = 1024


def _pad_to(a, mult, axis=-1):
    axis = axis % a.ndim
    extra = (-a.shape[axis]) % mult
    if extra == 0:
        return a
    pads = [(0, 0)] * a.ndim
    pads[axis] = (0, extra)
    return jnp.pad(a, pads)


def _pack(pieces, lead, row_mult):
    nl = len(lead)
    flat, offs, sizes, off = [], [], [], 0
    for a in pieces:
        f = a.reshape(lead + (-1,))
        sizes.append(f.shape[-1])
        f = _pad_to(f, PACK_ALIGN)
        offs.append(off)
        off += f.shape[-1]
        flat.append(f)
    cat = _pad_to(jnp.concatenate(flat, axis=nl), 2 * row_mult * LANES)
    return cat.reshape(lead + (2, -1, LANES)), offs, sizes


def _unpack(packed, lead, offs, sizes, shapes):
    flat = packed.reshape(lead + (-1,))
    return [lax.slice_in_dim(flat, o, o + s, axis=len(lead)).reshape(lead + tuple(shp)) for o, s, shp in zip(offs, sizes, shapes)]


def _stack_grads(G, names):
    out = {}
    for n in names:
        parts = [g for g in G[n] if g is not None]
        if n in ('kv_wk', 'kv_wv'):
            out[n] = parts[0]
        elif n in REPLICATED:
            out[n] = jnp.stack(parts, axis=0).reshape(N_CHIPS, -1)
        elif n in ('a_dw_w', 'a_dw_b', 'a_ln_g', 'a_ln_b', 'a_pw2_b'):
            out[n] = _to_sm(jnp.stack(parts, axis=0), axis=-1)
        else:
            out[n] = jnp.stack(parts, axis=1)
    return out


def _whole_weights(big, small, rep, D):
    W = {}
    for n in BIG:
        a = big[n]
        W[n] = a[:, None] if n in ('kv_wk', 'kv_wv') else a
    W['a_pw1_b'] = small['a_pw1_b'][:, :, None, :]
    W['a_dw_w'] = _from_sm(small['a_dw_w'], axis=-1)
    for n in ('a_dw_b', 'a_ln_g', 'a_ln_b', 'a_pw2_b'):
        W[n] = _from_sm(small[n], axis=-1)[:, None, :]
    W['ffn_conv_w'] = small['ffn_conv_w']
    L, F = rep['ffn_conv_b'].shape
    W['ffn_conv_b'] = rep['ffn_conv_b'].reshape(L, N_CHIPS, 1, F // N_CHIPS)
    for n in ('ln_mix_g', 'ln_mix_b', 'ln_ffn_g', 'ln_ffn_b'):
        W[n] = rep[n][:, None, :]
    return W


SMALL = ('a_pw1_b', 'a_dw_w', 'a_dw_b', 'a_ln_g', 'a_ln_b', 'a_pw2_b', 'ffn_conv_w')


def _step(x, p, loss_target, w, m, v):
    S, D = x.shape[-2:]
    x2, tgt = x.reshape(S, D), loss_target.reshape(S, D)
    ax, ay, ac = lax.axis_index("x"), lax.axis_index("y"), lax.axis_index("c")
    second = jnp.where(ac == 0, 2 * ax + (1 - ay), 2 * (1 - ax) + ay)
    place = jnp.stack([ac, 2 * ax + ay, second]).astype(jnp.int32)

    big_in = [w[n].astype(BF16).reshape((2, -1) + w[n].shape[1:] if w[n].ndim == 3 else (2, -1, w[n].shape[-1])) for n in BIG]
    small_in, s_offs, s_sizes = _pack([w[n] for n in SMALL], (), 8)
    gathered = allgather_chips("gather_weights", big_in + [small_in])
    big = {n: g.reshape((N_CHIPS,) + w[n].shape) for n, g in zip(BIG, gathered[:-1])}
    small = dict(zip(SMALL, _unpack(gathered[-1], (N_CHIPS,), s_offs, s_sizes, [w[n].shape for n in SMALL])))
    W = _whole_weights(big, small, {n: w[n] for n in REPLICATED}, D)

    loss_sum, dx, G, gb = forward_backward(x2, p, tgt, W)
    loss = lax.psum(0.5 * loss_sum / D, ("x", "y", "c"))

    vectors = [n for n in WEIGHTS if n not in BIG]
    mats = [b.reshape(N_CHIPS, 2, b.shape[1] // 2, b.shape[2]) for b in gb.buf.values()]
    members = [gb.members[cols] for cols in gb.buf]
    g_sm = _stack_grads(G, vectors)
    packed, offs, sizes = _pack([g_sm[n] for n in vectors], (N_CHIPS,), 512)
    reduced = reduce_scatter(mats + [packed], [BF16] * len(mats) + [F32], place)
    shapes = [w[n].shape if n not in REPLICATED else (w[n].size // N_CHIPS,) for n in vectors]
    g_mine = dict(zip(vectors, _unpack(reduced[-1], (), offs, sizes, shapes)))
    for red, where in zip(reduced[:-1], members):
        rows = red.reshape(-1, red.shape[-1])
        for n, off, cnt in where:
            g_mine[n] = lax.slice_in_dim(rows, off, off + cnt, axis=0).reshape(w[n].shape)
    rep_in, r_offs, r_sizes = _pack([g_mine[n] for n in REPLICATED], (), 8)
    rep_all = allgather_chips("gather_replicated_grads", [rep_in])[0]
    for n, g in zip(REPLICATED, _unpack(rep_all, (N_CHIPS,), r_offs, r_sizes, [(w[n].size // N_CHIPS,) for n in REPLICATED])):
        g_mine[n] = g.reshape(w[n].shape)

    grads, deltas, new_m, new_v = [], [], [], []
    for n in WEIGHTS:
        d, mo, vo = adamw(f"adamw_{n}", w[n], g_mine[n], m[n], v[n])
        grads.append(g_mine[n])
        deltas.append(d)
        new_m.append(mo)
        new_v.append(vo)
    return (loss, dx.reshape(x.shape), *grads, *deltas, *new_m, *new_v)


def kernel(x, p, a_pw1_w, a_pw1_b, a_dw_w, a_dw_b, a_ln_g, a_ln_b, a_pw2_w, a_pw2_b, b_wq, kv_wk, kv_wv, b_wo, ln_mix_g, ln_mix_b, ffn_w_up, ffn_w_gate, ffn_conv_w, ffn_conv_b, ffn_w_down, ple_w_gate, ple_w_proj, ln_ffn_g, ln_ffn_b, loss_target, m_a_pw1_w, m_a_pw1_b, m_a_dw_w, m_a_dw_b, m_a_ln_g, m_a_ln_b, m_a_pw2_w, m_a_pw2_b, m_b_wq, m_kv_wk, m_kv_wv, m_b_wo, m_ln_mix_g, m_ln_mix_b, m_ffn_w_up, m_ffn_w_gate, m_ffn_conv_w, m_ffn_conv_b, m_ffn_w_down, m_ple_w_gate, m_ple_w_proj, m_ln_ffn_g, m_ln_ffn_b, v_a_pw1_w, v_a_pw1_b, v_a_dw_w, v_a_dw_b, v_a_ln_g, v_a_ln_b, v_a_pw2_w, v_a_pw2_b, v_b_wq, v_kv_wk, v_kv_wv, v_b_wo, v_ln_mix_g, v_ln_mix_b, v_ffn_w_up, v_ffn_w_gate, v_ffn_conv_w, v_ffn_conv_b, v_ffn_w_down, v_ple_w_gate, v_ple_w_proj, v_ln_ffn_g, v_ln_ffn_b):
    vals = dict(locals())
    w = {n: vals[n] for n in WEIGHTS}
    m = {n: vals["m_" + n] for n in WEIGHTS}
    v = {n: vals["v_" + n] for n in WEIGHTS}
    return _step(x, p, loss_target, w, m, v)
```

```python
import functools
import math

import jax
import jax.numpy as jnp
import numpy as np
from jax import lax
from jax.experimental import pallas as pl
from jax.experimental.pallas import tpu as pltpu

F32, BF16 = jnp.float32, jnp.bfloat16

HEAD_DIM = 64
LN_EPS = 1e-5
DEPTH = 4
N_A = DEPTH // 2
DN_ALPHA = (2.0 * DEPTH) ** 0.25
N_CHIPS = 4

ADAM_LR, ADAM_B1, ADAM_B2, ADAM_EPS, ADAM_WD, ADAM_STEP = 0.001, 0.9, 0.999, 1e-08, 0.01, 10

VMEM_LIMIT_BYTES = 56 * 2**20
LANES = 128
CONV_HALO = 32
FFN_HALO = 16

NN = (((1,), (0,)), ((), ()))
NT = (((1,), (1,)), ((), ()))
TN = (((0,), (0,)), ((), ()))

WEIGHTS = ['a_pw1_w', 'a_pw1_b', 'a_dw_w', 'a_dw_b', 'a_ln_g', 'a_ln_b', 'a_pw2_w', 'a_pw2_b', 'b_wq', 'kv_wk', 'kv_wv',
           'b_wo', 'ln_mix_g', 'ln_mix_b', 'ffn_w_up', 'ffn_w_gate', 'ffn_conv_w', 'ffn_conv_b', 'ffn_w_down', 'ple_w_gate',
           'ple_w_proj', 'ln_ffn_g', 'ln_ffn_b']
REPLICATED = ('ln_mix_g', 'ln_mix_b', 'ffn_conv_b', 'ln_ffn_g', 'ln_ffn_b')
BIG = ('a_pw1_w', 'a_pw2_w', 'b_wq', 'kv_wk', 'kv_wv', 'b_wo', 'ffn_w_up', 'ffn_w_gate', 'ffn_w_down', 'ple_w_gate',
       'ple_w_proj')


def _tile(n, pref, mult=8):
    t = min(n, pref)
    while t > 0:
        if n % t == 0 and t % mult == 0:
            return t
        t -= 1
    return n


def _params(sem):
    return pltpu.CompilerParams(dimension_semantics=sem, vmem_limit_bytes=VMEM_LIMIT_BYTES)


def _sigmoid(x):
    return 1.0 / (1.0 + jnp.exp(-x))


def _mm_call(name, grid, terms, out_spec, out_sds, dims, bias=None, res=None, res_alpha=1.0, out_scale=None, into=None):
    n_terms = len(terms)

    def body(*refs):
        o_ref = refs[-1]
        acc = None
        for t in range(n_terms):
            a = refs[2 * t][...].astype(BF16)
            b = refs[2 * t + 1][...].astype(BF16)
            d = lax.dot_general(a, b, dims, preferred_element_type=F32)
            acc = d if acc is None else acc + d
        k = 2 * n_terms
        if bias is not None:
            acc = acc + refs[k][...]
            k += 1
        if res is not None:
            acc = acc + res_alpha * refs[k][...]
        if out_scale is not None:
            acc = acc * out_scale
        o_ref[...] = acc.astype(o_ref.dtype)

    operands, specs = [], []
    for a, a_spec, b, b_spec in terms:
        operands += [a, b]
        specs += [a_spec, b_spec]
    for extra in (bias, res):
        if extra is not None:
            operands.append(extra[0])
            specs.append(extra[1])
    aliases = {}
    if into is not None:
        aliases = {len(operands): 0}
        operands.append(into)
        specs.append(pl.BlockSpec(memory_space=pl.ANY))
        out_sds = jax.ShapeDtypeStruct(into.shape, into.dtype)
    return pl.pallas_call(body, out_shape=out_sds, grid=grid, in_specs=specs, out_specs=out_spec, input_output_aliases=aliases,
                          compiler_params=_params(("parallel",) * len(grid)), name=name)(*operands)


def _mm_fanout(name, a, a_spec, w4, w_block, layer, dims, out_spec, out_sds, store, grid, bias4=None, res=None, res_alpha=1.0):
    def body(*refs):
        a_ref, w_refs, o_ref = refs[0], refs[1:1 + N_CHIPS], refs[-1]
        k = 1 + N_CHIPS
        b_refs = refs[k:k + N_CHIPS] if bias4 is not None else None
        k += N_CHIPS if bias4 is not None else 0
        av = a_ref[...].astype(BF16)
        for j in range(N_CHIPS):
            d = lax.dot_general(av, w_refs[j][...].astype(BF16), dims, preferred_element_type=F32)
            if b_refs is not None:
                d = d + b_refs[j][...]
            if res is not None:
                d = d + res_alpha * res[2](refs[k], j)
            store(o_ref, j, d)

    nd = len(grid)
    operands = [a] + [w4] * N_CHIPS
    specs = [a_spec] + [pl.BlockSpec((None, None) + w_block, lambda *g, j=j: (j, layer, 0, 0)) for j in range(N_CHIPS)]
    if bias4 is not None:
        operands += [bias4] * N_CHIPS
        specs += [pl.BlockSpec((None, None, 1, bias4.shape[-1]), lambda *g, j=j: (j, layer, 0, 0)) for j in range(N_CHIPS)]
    if res is not None:
        operands.append(res[0])
        specs.append(res[1])
    return pl.pallas_call(body, out_shape=out_sds, grid=grid, in_specs=specs, out_specs=out_spec,
                          compiler_params=_params(("parallel",) * nd), name=name)(*operands)


def _store_slot(o_ref, j, d):
    o_ref[j] = d.astype(o_ref.dtype)


def mm_colsm(name, x, w4, layer, bias4=None, out_dtype=F32, tm=1024):
    M, K = x.shape
    n = w4.shape[-1]
    tm = _tile(M, tm)
    return _mm_fanout(name, x, pl.BlockSpec((tm, K), lambda i: (i, 0)), w4, (K, n), layer, NN,
                      pl.BlockSpec((N_CHIPS, tm, n), lambda i: (0, i, 0)), jax.ShapeDtypeStruct((N_CHIPS, M, n), out_dtype),
                      _store_slot, (M // tm,), bias4=bias4)


def mm_rowsm(name, a, w4, layer, a_sm=False, bias=None, out_dtype=F32, out_scale=None, tm=512):
    kc, N = w4.shape[-2:]
    M = a.shape[-2]
    tm = _tile(M, tm)
    terms = []
    for j in range(N_CHIPS):
        if a_sm:
            a_spec = pl.BlockSpec((None, tm, kc), lambda i, j=j: (j, i, 0))
        else:
            a_spec = pl.BlockSpec((tm, kc), lambda i, j=j: (i, j))
        terms.append((a, a_spec, w4, pl.BlockSpec((None, None, kc, N), lambda i, j=j: (j, layer, 0, 0))))
    b = None if bias is None else (bias, pl.BlockSpec((None, 1, N), lambda i: (layer, 0, 0)))
    return _mm_call(name, (M // tm,), terms, pl.BlockSpec((tm, N), lambda i: (i, 0)), jax.ShapeDtypeStruct((M, N), out_dtype),
                    NN, bias=b, out_scale=out_scale)


def mm_rowsm_ln(name, a, w4, layer, x, g, b, ln_layer, a_sm=False, bias=None, tg=None, pp=None, tm=256):
    kc, N = w4.shape[-2:]
    M = a.shape[-2]
    tm = _tile(M, tm, 16)
    ple = tg is not None
    nb = bias is not None

    def body(*refs):
        a_refs, w_refs = refs[0:N_CHIPS], refs[N_CHIPS:2 * N_CHIPS]
        k = 2 * N_CHIPS
        acc = None
        for j in range(N_CHIPS):
            d = jnp.dot(a_refs[j][...].astype(BF16), w_refs[j][...].astype(BF16), preferred_element_type=F32)
            acc = d if acc is None else acc + d
        if nb:
            acc = acc + refs[k][...]
            k += 1
        r = DN_ALPHA * refs[k][...] + acc
        k += 1
        if ple:
            r = r + _sigmoid(refs[k][...]) * refs[k + 1][...]
            k += 2
        g_ref, b_ref, y_ref, yb_ref, xh_ref, rs_ref = refs[k:]
        mu = jnp.mean(r, axis=-1, keepdims=True)
        dd = r - mu
        var = jnp.mean(dd * dd, axis=-1, keepdims=True)
        rstd = lax.rsqrt(var + LN_EPS)
        xh = dd * rstd
        y = xh * g_ref[...] + b_ref[...]
        y_ref[...] = y
        yb_ref[...] = y.astype(BF16)
        xh_ref[...] = xh
        rs_ref[...] = rstd

    row = pl.BlockSpec((tm, N), lambda i: (i, 0))
    vec = pl.BlockSpec((None, 1, N), lambda i: (ln_layer, 0, 0))
    if a_sm:
        a_specs = [pl.BlockSpec((None, tm, kc), lambda i, j=j: (j, i, 0)) for j in range(N_CHIPS)]
    else:
        a_specs = [pl.BlockSpec((tm, kc), lambda i, j=j: (i, j)) for j in range(N_CHIPS)]
    w_specs = [pl.BlockSpec((None, None, kc, N), lambda i, j=j: (j, layer, 0, 0)) for j in range(N_CHIPS)]
    ins = [a] * N_CHIPS + [w4] * N_CHIPS + ([bias] if nb else []) + [x] + ([tg, pp] if ple else []) + [g, b]
    specs = a_specs + w_specs + ([pl.BlockSpec((None, 1, N), lambda i: (layer, 0, 0))] if nb else []) + [row] + ([row, row] if ple else []) + [vec, vec]
    return pl.pallas_call(body, grid=(M // tm,), in_specs=specs,
                          out_specs=(row, row, row, pl.BlockSpec((tm, 1), lambda i: (i, 0))),
                          out_shape=(jax.ShapeDtypeStruct((M, N), F32), jax.ShapeDtypeStruct((M, N), BF16),
                                     jax.ShapeDtypeStruct((M, N), F32), jax.ShapeDtypeStruct((M, 1), F32)),
                          compiler_params=_params(("parallel",)), name=name)(*ins)


def mm_pw1_glu(name, xb, w4, layer, bias4, tm=512):
    M, K = xb.shape
    n = w4.shape[-1]
    half = N_CHIPS // 2
    tm = _tile(M, tm)

    def body(*refs):
        x_ref, w_refs, b_refs = refs[0], refs[1:1 + N_CHIPS], refs[1 + N_CHIPS:1 + 2 * N_CHIPS]
        h_ref, u_ref = refs[-2:]
        xv = x_ref[...].astype(BF16)
        for j in range(N_CHIPS):
            h_ref[j] = jnp.dot(xv, w_refs[j][...].astype(BF16), preferred_element_type=F32) + b_refs[j][...]
        for j in range(half):
            u_ref[:, j * n:(j + 1) * n] = h_ref[j] * _sigmoid(h_ref[j + half])

    specs = [pl.BlockSpec((tm, K), lambda i: (i, 0))]
    specs += [pl.BlockSpec((None, None, K, n), lambda i, j=j: (j, layer, 0, 0)) for j in range(N_CHIPS)]
    specs += [pl.BlockSpec((None, None, 1, n), lambda i, j=j: (j, layer, 0, 0)) for j in range(N_CHIPS)]
    return pl.pallas_call(body, grid=(M // tm,), in_specs=specs,
                          out_specs=(pl.BlockSpec((N_CHIPS, tm, n), lambda i: (0, i, 0)), pl.BlockSpec((tm, half * n), lambda i: (i, 0))),
                          out_shape=(jax.ShapeDtypeStruct((N_CHIPS, M, n), F32), jax.ShapeDtypeStruct((M, half * n), F32)),
                          compiler_params=_params(("parallel",)), name=name)(xb, *([w4] * N_CHIPS), *([bias4] * N_CHIPS))


def mm_nt_rowsm(name, dy, w4, layer, out_sm=False, res=None, res_alpha=1.0, out_dtype=F32, tm=1024):
    kc, N = w4.shape[-2:]
    M = dy.shape[0]
    tm = _tile(M, tm)

    def store_cols(o_ref, j, d):
        o_ref[:, j * kc:(j + 1) * kc] = d

    if out_sm:
        out_spec, sds, store = pl.BlockSpec((N_CHIPS, tm, kc), lambda i: (0, i, 0)), jax.ShapeDtypeStruct((N_CHIPS, M, kc), out_dtype), _store_slot
    else:
        out_spec, sds, store = pl.BlockSpec((tm, N_CHIPS * kc), lambda i: (i, 0)), jax.ShapeDtypeStruct((M, N_CHIPS * kc), F32), store_cols
    r = None if res is None else (res, pl.BlockSpec((tm, N_CHIPS * kc), lambda i: (i, 0)), lambda ref, j: ref[:, j * kc:(j + 1) * kc])
    return _mm_fanout(name, dy, pl.BlockSpec((tm, N), lambda i: (i, 0)), w4, (kc, N), layer, NT, out_spec, sds, store, (M // tm,),
                      res=r, res_alpha=res_alpha)


def mm_nt_colsm(name, dy_parts, w4, layer, res=None, res_alpha=1.0, tm=512):
    K, n = w4.shape[-2:]
    M = dy_parts[0][0].shape[1]
    tm = _tile(M, tm)
    terms = [(arr, pl.BlockSpec((None, tm, n), lambda i, idx=idx: (idx, i, 0)), w4,
              pl.BlockSpec((None, None, K, n), lambda i, j=j: (j, layer, 0, 0))) for j, (arr, idx) in enumerate(dy_parts)]
    r = None if res is None else (res, pl.BlockSpec((tm, K), lambda i: (i, 0)))
    return _mm_call(name, (M // tm,), terms, pl.BlockSpec((tm, K), lambda i: (i, 0)), jax.ShapeDtypeStruct((M, K), F32), NT,
                    res=r, res_alpha=res_alpha)


def _sm_parts(a):
    return [(a, j) for j in range(a.shape[0])]


def mm_tn_colsm(name, x, dy, into, off, j0=0, tk=512):
    M, K = x.shape
    nj, _, n = dy.shape
    tk = _tile(K, tk, LANES)
    assert off % tk == 0
    terms = [(x, pl.BlockSpec((M, tk), lambda j, k: (0, k)), dy, pl.BlockSpec((None, M, n), lambda j, k: (j, 0, 0)))]
    return _mm_call(name, (nj, K // tk), terms, pl.BlockSpec((None, tk, n), lambda j, k: (j + j0, off // tk + k, 0)),
                    None, TN, into=into)


def mm_tn_rowsm(name, a, dy, into, off, a_sm=False, tn=512):
    M, N = dy.shape
    kc = a.shape[-1] if a_sm else a.shape[-1] // N_CHIPS
    tn = _tile(N, tn, LANES)
    assert off % kc == 0
    a_spec = pl.BlockSpec((None, M, kc), lambda j, n: (j, 0, 0)) if a_sm else pl.BlockSpec((M, kc), lambda j, n: (0, j))
    terms = [(a, a_spec, dy, pl.BlockSpec((M, tn), lambda j, n: (0, n)))]
    return _mm_call(name, (N_CHIPS, N // tn), terms, pl.BlockSpec((None, kc, tn), lambda j, n: (j, off // kc, n)),
                    None, TN, into=into)


def mm_tn_rowsm_fan(name, a, dy, into, off, tn=256):
    M, N = dy.shape
    kc = a.shape[-1] // N_CHIPS
    tn = _tile(N, tn, LANES)
    assert off % kc == 0

    def body(a_ref, dy_ref, into_ref, o_ref):
        dyb = dy_ref[...].astype(BF16)
        for j in range(N_CHIPS):
            aj = a_ref[:, j * kc:(j + 1) * kc].astype(BF16)
            o_ref[j] = lax.dot_general(aj, dyb, TN, preferred_element_type=F32)

    return pl.pallas_call(body, out_shape=jax.ShapeDtypeStruct(into.shape, into.dtype), grid=(N // tn,),
                          in_specs=[pl.BlockSpec((M, N_CHIPS * kc), lambda n: (0, 0)), pl.BlockSpec((M, tn), lambda n: (0, n)),
                                    pl.BlockSpec(memory_space=pl.ANY)],
                          out_specs=pl.BlockSpec((N_CHIPS, kc, tn), lambda n: (0, off // kc, n)), input_output_aliases={2: 0},
                          compiler_params=_params(("parallel",)), name=name)(a, dy, into)


def mm_proj(name, p4, layer, w4, tm=512):
    S, P = p4.shape[-2:]
    n = w4.shape[-1]
    tm = _tile(S, tm)

    def store_cols(o_ref, j, d):
        o_ref[:, j * n:(j + 1) * n] = d

    return _mm_fanout(name, p4, pl.BlockSpec((None, None, tm, P), lambda i: (layer, 0, i, 0)), w4, (P, n), layer, NN,
                      pl.BlockSpec((tm, N_CHIPS * n), lambda i: (i, 0)), jax.ShapeDtypeStruct((S, N_CHIPS * n), F32),
                      store_cols, (S // tm,))


def mm_tn_proj(name, p4, layer, dpp, into, off):
    S, P = p4.shape[-2:]
    n = dpp.shape[-1] // N_CHIPS
    assert off % P == 0
    terms = [(p4, pl.BlockSpec((None, None, S, P), lambda j: (layer, 0, 0, 0)), dpp, pl.BlockSpec((S, n), lambda j: (0, j)))]
    return _mm_call(name, (N_CHIPS,), terms, pl.BlockSpec((None, P, n), lambda j: (j, off // P, 0)), None, TN, into=into)


def ln_fwd(name, x, mix, g, b, layer, tg=None, pp=None, tm=512):
    S, D = x.shape
    tm = _tile(S, tm, 16)
    ple = tg is not None

    def body(*refs):
        if ple:
            x_ref, m_ref, tg_ref, pp_ref, g_ref, b_ref, y_ref, yb_ref, xh_ref, rs_ref = refs
        else:
            x_ref, m_ref, g_ref, b_ref, y_ref, yb_ref, xh_ref, rs_ref = refs
        r = DN_ALPHA * x_ref[...] + m_ref[...]
        if ple:
            r = r + _sigmoid(tg_ref[...]) * pp_ref[...]
        mu = jnp.mean(r, axis=-1, keepdims=True)
        d = r - mu
        var = jnp.mean(d * d, axis=-1, keepdims=True)
        rstd = lax.rsqrt(var + LN_EPS)
        xh = d * rstd
        y = xh * g_ref[...] + b_ref[...]
        y_ref[...] = y
        yb_ref[...] = y.astype(BF16)
        xh_ref[...] = xh
        rs_ref[...] = rstd

    row = pl.BlockSpec((tm, D), lambda i: (i, 0))
    vec = pl.BlockSpec((None, 1, D), lambda i: (layer, 0, 0))
    ins = [x, mix] + ([tg, pp] if ple else []) + [g, b]
    specs = [row, row] + ([row, row] if ple else []) + [vec, vec]
    return pl.pallas_call(body, grid=(S // tm,), in_specs=specs,
                          out_specs=(row, row, row, pl.BlockSpec((tm, 1), lambda i: (i, 0))),
                          out_shape=(jax.ShapeDtypeStruct((S, D), F32), jax.ShapeDtypeStruct((S, D), BF16),
                                     jax.ShapeDtypeStruct((S, D), F32), jax.ShapeDtypeStruct((S, 1), F32)),
                          compiler_params=_params(("parallel",)), name=name)(*ins)


def ln_bwd(name, dy, xh, rstd, g, layer, tg=None, pp=None, tm=512):
    S, D = dy.shape
    tm = _tile(S, tm)
    ple = tg is not None

    def body(*refs):
        if ple:
            dy_ref, xh_ref, rs_ref, g_ref, tg_ref, pp_ref, dr_ref, acc_ref, dtg_ref, dpp_ref = refs
        else:
            dy_ref, xh_ref, rs_ref, g_ref, dr_ref, acc_ref = refs
        dy_, xh_ = dy_ref[...], xh_ref[...]
        dxh = dy_ * g_ref[...]
        m1 = jnp.mean(dxh, axis=-1, keepdims=True)
        m2 = jnp.mean(dxh * xh_, axis=-1, keepdims=True)
        dr = rs_ref[...] * (dxh - m1 - xh_ * m2)
        dr_ref[...] = dr

        @pl.when(pl.program_id(0) == 0)
        def _():
            acc_ref[...] = jnp.zeros_like(acc_ref)
        acc_ref[0:1, :] += jnp.sum(dy_ * xh_, axis=0, keepdims=True)
        acc_ref[1:2, :] += jnp.sum(dy_, axis=0, keepdims=True)
        acc_ref[2:3, :] += jnp.sum(dr, axis=0, keepdims=True)
        if ple:
            pg = _sigmoid(tg_ref[...])
            dtg_ref[...] = (dr * pp_ref[...] * pg * (1.0 - pg)).astype(BF16)
            dpp_ref[...] = (dr * pg).astype(BF16)

    row = pl.BlockSpec((tm, D), lambda i: (i, 0))
    ins = [dy, xh, rstd, g] + ([tg, pp] if ple else [])
    specs = [row, row, pl.BlockSpec((tm, 1), lambda i: (i, 0)), pl.BlockSpec((None, 1, D), lambda i: (layer, 0, 0))] + ([row, row] if ple else [])
    outs = [jax.ShapeDtypeStruct((S, D), F32), jax.ShapeDtypeStruct((8, D), F32)]
    out_specs = [row, pl.BlockSpec((8, D), lambda i: (0, 0))]
    if ple:
        outs += [jax.ShapeDtypeStruct((S, D), BF16)] * 2
        out_specs += [row, row]
    return pl.pallas_call(body, grid=(S // tm,), in_specs=specs, out_specs=tuple(out_specs), out_shape=tuple(outs),
                          compiler_params=_params(("arbitrary",)), name=name)(*ins)


def glu_fwd(name, h_sm, tm=512):
    _, S, n = h_sm.shape
    tm = _tile(S, tm)
    half = N_CHIPS // 2

    def body(a_ref, g_ref, u_ref):
        u_ref[...] = a_ref[...] * _sigmoid(g_ref[...])

    return pl.pallas_call(body, grid=(half, S // tm),
                          in_specs=[pl.BlockSpec((None, tm, n), lambda j, i: (j, i, 0)),
                                    pl.BlockSpec((None, tm, n), lambda j, i: (j + half, i, 0))],
                          out_specs=pl.BlockSpec((tm, n), lambda j, i: (i, j)),
                          out_shape=jax.ShapeDtypeStruct((S, half * n), F32),
                          compiler_params=_params(("parallel", "parallel")), name=name)(h_sm, h_sm)


def conv_ln_silu_fwd(name, u, w, b, g, beta, layer, ts=128):
    S, D = u.shape
    kw = w.shape[1]
    ts = _tile(S, ts, CONV_HALO)
    lc = LANES if D % LANES == 0 else D

    def body(h_ref, u_ref, w_ref, b_ref, g_ref, be_ref, c_ref, s_ref, win_ref):
        i = pl.program_id(0)
        win_ref[0:CONV_HALO, :] = jnp.where(i == 0, 0.0, h_ref[...])
        win_ref[CONV_HALO:, :] = u_ref[...]
        for cc in range(D // lc):
            cs = slice(cc * lc, (cc + 1) * lc)
            acc = jnp.zeros((ts, lc), F32) + b_ref[:, cs]
            for k in range(kw):
                off = CONV_HALO - (kw - 1) + k
                acc = acc + w_ref[k:k + 1, cs] * win_ref[off:off + ts, cs]
            c_ref[:, cs] = acc
        c = c_ref[...]
        mu = jnp.mean(c, axis=-1, keepdims=True)
        d = c - mu
        var = jnp.mean(d * d, axis=-1, keepdims=True)
        nrm = d * lax.rsqrt(var + LN_EPS) * g_ref[...] + be_ref[...]
        s_ref[...] = (nrm * _sigmoid(nrm)).astype(BF16)

    row = pl.BlockSpec((ts, D), lambda i: (i, 0))
    vec = pl.BlockSpec((None, 1, D), lambda i: (layer, 0, 0))
    halo = pl.BlockSpec((CONV_HALO, D), lambda i: (jnp.maximum(i * (ts // CONV_HALO) - 1, 0), 0))
    return pl.pallas_call(body, grid=(S // ts,),
                          in_specs=[halo, row, pl.BlockSpec((None, kw, D), lambda i: (layer, 0, 0)), vec, vec, vec],
                          out_specs=(row, row),
                          out_shape=(jax.ShapeDtypeStruct((S, D), F32), jax.ShapeDtypeStruct((S, D), BF16)),
                          scratch_shapes=[pltpu.VMEM((ts + CONV_HALO, D), F32)],
                          compiler_params=_params(("parallel",)), name=name)(u, u, w, b, g, beta)


def ln_silu_bwd(name, ds, c, g, beta, layer, tm=256):
    S, D = c.shape
    tm = _tile(S, tm)

    def body(ds_ref, c_ref, g_ref, be_ref, dc_ref, acc_ref):
        c_ = c_ref[...]
        mu = jnp.mean(c_, axis=-1, keepdims=True)
        d = c_ - mu
        var = jnp.mean(d * d, axis=-1, keepdims=True)
        rstd = lax.rsqrt(var + LN_EPS)
        xh = d * rstd
        nrm = xh * g_ref[...] + be_ref[...]
        sg = _sigmoid(nrm)
        dn = ds_ref[...] * (sg * (1.0 + nrm * (1.0 - sg)))
        dxh = dn * g_ref[...]
        m1 = jnp.mean(dxh, axis=-1, keepdims=True)
        m2 = jnp.mean(dxh * xh, axis=-1, keepdims=True)
        dc = rstd * (dxh - m1 - xh * m2)
        dc_ref[...] = dc

        @pl.when(pl.program_id(0) == 0)
        def _():
            acc_ref[...] = jnp.zeros_like(acc_ref)
        acc_ref[0:1, :] += jnp.sum(dn * xh, axis=0, keepdims=True)
        acc_ref[1:2, :] += jnp.sum(dn, axis=0, keepdims=True)
        acc_ref[2:3, :] += jnp.sum(dc, axis=0, keepdims=True)

    row = pl.BlockSpec((tm, D), lambda i: (i, 0))
    vec = pl.BlockSpec((None, 1, D), lambda i: (layer, 0, 0))
    return pl.pallas_call(body, grid=(S // tm,), in_specs=[row, row, vec, vec],
                          out_specs=(row, pl.BlockSpec((8, D), lambda i: (0, 0))),
                          out_shape=(jax.ShapeDtypeStruct((S, D), F32), jax.ShapeDtypeStruct((8, D), F32)),
                          compiler_params=_params(("arbitrary",)), name=name)(ds, c, g, beta)


def conv_glu_bwd(name, dc, u, h_sm, w, layer, ts=128):
    S, D = dc.shape
    kw = w.shape[1]
    half = N_CHIPS // 2
    n = D // half
    ts = _tile(S, ts, CONV_HALO)
    nblk = S // ts
    lc = LANES if n % LANES == 0 else n

    def body(dc_ref, dcn_ref, u_ref, a_ref, g_ref, w_ref, da_ref, dg_ref, dw_ref, dba_ref, dbg_ref, dwin_ref):
        i = pl.program_id(1)
        dwin_ref[0:ts, :] = dc_ref[...]
        dwin_ref[ts:, :] = jnp.where(i == nblk - 1, 0.0, dcn_ref[...])

        @pl.when(i == 0)
        def _():
            dw_ref[...] = jnp.zeros_like(dw_ref)
            dba_ref[...] = jnp.zeros_like(dba_ref)
            dbg_ref[...] = jnp.zeros_like(dbg_ref)

        for cc in range(n // lc):
            cs = slice(cc * lc, (cc + 1) * lc)
            ub = u_ref[:, cs]
            du = jnp.zeros((ts, lc), F32)
            for k in range(kw):
                shifted = dwin_ref[kw - 1 - k:kw - 1 - k + ts, cs]
                du = du + w_ref[k:k + 1, cs] * shifted
                dw_ref[k:k + 1, cs] += jnp.sum(shifted * ub, axis=0, keepdims=True)
            a = a_ref[:, cs]
            sg = _sigmoid(g_ref[:, cs])
            da = du * sg
            dg = du * a * sg * (1.0 - sg)
            da_ref[:, cs] = da.astype(BF16)
            dg_ref[:, cs] = dg.astype(BF16)
            dba_ref[0:1, cs] += jnp.sum(da, axis=0, keepdims=True)
            dbg_ref[0:1, cs] += jnp.sum(dg, axis=0, keepdims=True)

    r = ts // CONV_HALO
    main = pl.BlockSpec((ts, n), lambda j, i: (i, j))
    nxt = pl.BlockSpec((CONV_HALO, n), lambda j, i: (jnp.minimum((i + 1) * r, S // CONV_HALO - 1), j))
    sm_a = pl.BlockSpec((None, ts, n), lambda j, i: (j, i, 0))
    sm_g = pl.BlockSpec((None, ts, n), lambda j, i: (j + half, i, 0))
    da, dg, dw, dba, dbg = pl.pallas_call(
        body, grid=(half, nblk),
        in_specs=[main, nxt, main, sm_a, sm_g, pl.BlockSpec((None, kw, n), lambda j, i: (layer, 0, j))],
        out_specs=(pl.BlockSpec((None, ts, n), lambda j, i: (j, i, 0)), pl.BlockSpec((None, ts, n), lambda j, i: (j, i, 0)),
                   pl.BlockSpec((None, 32, n), lambda j, i: (j, 0, 0)),
                   pl.BlockSpec((None, 8, n), lambda j, i: (j, 0, 0)), pl.BlockSpec((None, 8, n), lambda j, i: (j, 0, 0))),
        out_shape=(jax.ShapeDtypeStruct((half, S, n), BF16), jax.ShapeDtypeStruct((half, S, n), BF16),
                   jax.ShapeDtypeStruct((half, 32, n), F32),
                   jax.ShapeDtypeStruct((half, 8, n), F32), jax.ShapeDtypeStruct((half, 8, n), F32)),
        scratch_shapes=[pltpu.VMEM((ts + CONV_HALO, n), F32)],
        compiler_params=_params(("parallel", "arbitrary")), name=name)(dc, dc, u, h_sm, h_sm, w)
    return da, dg, dw, dba, dbg


ROW_CHUNK = 16


def _ffn_gc(win_ref, w_ref, b_ref, r0, rows, kw, base):
    gc = b_ref[...] + jnp.zeros((rows, win_ref.shape[1]), F32)
    for k in range(kw):
        off = r0 + base - (kw - 1) + k
        gc = gc + w_ref[k:k + 1, :] * win_ref[off:off + rows, :]
    return gc


def ffn_gate_fwd(name, up_sm, gp_sm, w4, b4, layer, ts=256):
    _, S, n = up_sm.shape
    kw = w4.shape[2]
    ts = _tile(S, ts, ROW_CHUNK)
    rc = ROW_CHUNK

    def body(up_ref, gp_ref, gph_ref, w_ref, b_ref, hf_ref, gc_ref, win_ref):
        i = pl.program_id(1)
        win_ref[0:FFN_HALO, :] = jnp.where(i == 0, 0.0, gph_ref[...].astype(F32))
        win_ref[FFN_HALO:, :] = gp_ref[...].astype(F32)
        for r0 in range(0, ts, rc):
            gc = _ffn_gc(win_ref, w_ref, b_ref, r0, rc, kw, FFN_HALO)
            gc_ref[r0:r0 + rc, :] = gc.astype(BF16)
            hf_ref[r0:r0 + rc, :] = (gc * _sigmoid(gc) * up_ref[r0:r0 + rc, :].astype(F32)).astype(BF16)

    main = pl.BlockSpec((None, ts, n), lambda j, i: (j, i, 0))
    prv = pl.BlockSpec((None, FFN_HALO, n), lambda j, i: (j, jnp.maximum(i * (ts // FFN_HALO) - 1, 0), 0))
    sds = jax.ShapeDtypeStruct((N_CHIPS, S, n), BF16)
    return pl.pallas_call(body, grid=(N_CHIPS, S // ts),
                          in_specs=[main, main, prv, pl.BlockSpec((None, None, kw, n), lambda j, i: (j, layer, 0, 0)),
                                    pl.BlockSpec((None, None, 1, n), lambda j, i: (layer, j, 0, 0))],
                          out_specs=(main, main), out_shape=(sds, sds),
                          scratch_shapes=[pltpu.VMEM((ts + FFN_HALO, n), F32)],
                          compiler_params=_params(("parallel", "parallel")), name=name)(up_sm, gp_sm, gp_sm, w4, b4)


def ffn_gate_bwd_a(name, dhf_sm, up_sm, gc_sm, ts=256):
    _, S, n = up_sm.shape
    ts = _tile(S, ts, ROW_CHUNK)
    rc = ROW_CHUNK

    def body(dhf_ref, up_ref, gc_ref, dup_ref, dgc_ref):
        for r0 in range(0, ts, rc):
            rows = slice(r0, r0 + rc)
            gc = gc_ref[rows, :].astype(F32)
            sg = _sigmoid(gc)
            dhf = dhf_ref[rows, :].astype(F32)
            dup_ref[rows, :] = (dhf * gc * sg).astype(BF16)
            dgc_ref[rows, :] = (dhf * up_ref[rows, :].astype(F32) * (sg * (1.0 + gc * (1.0 - sg)))).astype(BF16)

    main = pl.BlockSpec((None, ts, n), lambda j, i: (j, i, 0))
    sds = jax.ShapeDtypeStruct((N_CHIPS, S, n), BF16)
    return pl.pallas_call(body, grid=(N_CHIPS, S // ts), in_specs=[main, main, main], out_specs=(main, main), out_shape=(sds, sds),
                          compiler_params=_params(("parallel", "parallel")), name=name)(dhf_sm, up_sm, gc_sm)


def ffn_up_gate_fwd(name, xb, w_up, w_gate, cw4, cb4, layer, tm=256):
    S, K = xb.shape
    n = w_up.shape[-1]
    kw = cw4.shape[2]
    tm = _tile(S, tm, ROW_CHUNK)
    rc = ROW_CHUNK

    def body(*refs):
        x_ref, xh_ref = refs[0:2]
        wu, wg, cw, cb = refs[2:6], refs[6:10], refs[10:14], refs[14:18]
        up_ref, gp_ref, gc_ref, hf_ref, win_ref = refs[18:]
        i = pl.program_id(0)
        xv = x_ref[...]
        xhalo = xh_ref[...]
        for j in range(N_CHIPS):
            wgj = wg[j][...]
            gp = jnp.dot(xv, wgj, preferred_element_type=F32)
            up = jnp.dot(xv, wu[j][...], preferred_element_type=F32)
            gph = jnp.dot(xhalo, wgj, preferred_element_type=F32)
            gpb = gp.astype(BF16)
            gp_ref[j] = gpb
            up_ref[j] = up.astype(BF16)
            win_ref[0:FFN_HALO, :] = jnp.where(i == 0, 0.0, gph.astype(BF16).astype(F32))
            win_ref[FFN_HALO:, :] = gpb.astype(F32)
            for r0 in range(0, tm, rc):
                gc = _ffn_gc(win_ref, cw[j], cb[j], r0, rc, kw, FFN_HALO)
                gc_ref[j, r0:r0 + rc, :] = gc.astype(BF16)
                hf_ref[j, r0:r0 + rc, :] = (gc * _sigmoid(gc) * up_ref[j, r0:r0 + rc, :].astype(F32)).astype(BF16)

    r = tm // FFN_HALO
    out = pl.BlockSpec((N_CHIPS, tm, n), lambda i: (0, i, 0))
    sds = jax.ShapeDtypeStruct((N_CHIPS, S, n), BF16)
    specs = [pl.BlockSpec((tm, K), lambda i: (i, 0)), pl.BlockSpec((FFN_HALO, K), lambda i: (jnp.maximum(i * r - 1, 0), 0))]
    specs += [pl.BlockSpec((None, None, K, n), lambda i, j=j: (j, layer, 0, 0)) for j in range(N_CHIPS)] * 2
    specs += [pl.BlockSpec((None, None, kw, n), lambda i, j=j: (j, layer, 0, 0)) for j in range(N_CHIPS)]
    specs += [pl.BlockSpec((None, None, 1, n), lambda i, j=j: (layer, j, 0, 0)) for j in range(N_CHIPS)]
    return pl.pallas_call(body, grid=(S // tm,), in_specs=specs, out_specs=(out, out, out, out), out_shape=(sds, sds, sds, sds),
                          scratch_shapes=[pltpu.VMEM((tm + FFN_HALO, n), F32)],
                          compiler_params=_params(("parallel",)), name=name)(
                              xb, xb, *([w_up] * N_CHIPS), *([w_gate] * N_CHIPS), *([cw4] * N_CHIPS), *([cb4] * N_CHIPS))


def ffn_gate_conv_bwd(name, dr, w_down, cw4, layer, up_sm, gc_sm, gp_sm, tm=256):
    n, N = w_down.shape[-2:]
    S = dr.shape[0]
    kw = cw4.shape[2]
    tm = _tile(S, tm, ROW_CHUNK)
    rc = ROW_CHUNK
    nblk = S // tm

    def dgc_of(dhf, up, gc):
        sg = _sigmoid(gc)
        return dhf * up * (sg * (1.0 + gc * (1.0 - sg))), sg

    def body(*refs):
        dr_ref, drn_ref = refs[0:2]
        wd, cw = refs[2:6], refs[6:10]
        up_ref, gc_ref, gp_ref, upn_ref, gcn_ref, dup_ref, dgp_ref, acc_ref, win_ref = refs[10:]
        i = pl.program_id(0)

        @pl.when(i == 0)
        def _():
            acc_ref[...] = jnp.zeros_like(acc_ref)
        a = dr_ref[...].astype(BF16)
        an = drn_ref[...].astype(BF16)
        for j in range(N_CHIPS):
            wj = wd[j][...]
            dhf = lax.dot_general(a, wj, NT, preferred_element_type=F32)
            gc = gc_ref[j].astype(F32)
            dgc, sg = dgc_of(dhf, up_ref[j].astype(F32), gc)
            dup_ref[j] = (dhf * gc * sg).astype(BF16)
            win_ref[0:tm, :] = dgc
            dhfn = lax.dot_general(an, wj, NT, preferred_element_type=F32)
            dgcn, _ = dgc_of(dhfn, upn_ref[j].astype(F32), gcn_ref[j].astype(F32))
            win_ref[tm:, :] = jnp.where(i == nblk - 1, 0.0, dgcn)
            sums = [jnp.zeros((8, n), F32) for _ in range(kw + 1)]
            for r0 in range(0, tm, rc):
                gp = gp_ref[j, r0:r0 + rc, :].astype(F32)
                dgp = jnp.zeros((rc, n), F32)
                for k in range(kw):
                    d = kw - 1 - k
                    shifted = win_ref[r0 + d:r0 + d + rc, :]
                    dgp = dgp + cw[j][k:k + 1, :] * shifted
                    prod = shifted * gp
                    sums[k] = sums[k] + prod[0:8, :] + prod[8:16, :]
                    if d == 0:
                        sums[kw] = sums[kw] + shifted[0:8, :] + shifted[8:16, :]
                dgp_ref[j, r0:r0 + rc, :] = dgp.astype(BF16)
            for k in range(kw):
                acc_ref[j, k:k + 1, :] += jnp.sum(sums[k], axis=0, keepdims=True)
            acc_ref[j, 7:8, :] += jnp.sum(sums[kw], axis=0, keepdims=True)

    r = tm // FFN_HALO
    nxt_row = lambda i: jnp.minimum((i + 1) * r, S // FFN_HALO - 1)
    blk = pl.BlockSpec((N_CHIPS, tm, n), lambda i: (0, i, 0))
    halo = pl.BlockSpec((N_CHIPS, FFN_HALO, n), lambda i: (0, nxt_row(i), 0))
    sds = jax.ShapeDtypeStruct((N_CHIPS, S, n), BF16)
    specs = [pl.BlockSpec((tm, N), lambda i: (i, 0)), pl.BlockSpec((FFN_HALO, N), lambda i: (nxt_row(i), 0))]
    specs += [pl.BlockSpec((None, None, n, N), lambda i, j=j: (j, layer, 0, 0)) for j in range(N_CHIPS)]
    specs += [pl.BlockSpec((None, None, kw, n), lambda i, j=j: (j, layer, 0, 0)) for j in range(N_CHIPS)]
    specs += [blk, blk, blk, halo, halo]
    return pl.pallas_call(body, grid=(nblk,), in_specs=specs,
                          out_specs=(blk, blk, pl.BlockSpec((N_CHIPS, 8, n), lambda i: (0, 0, 0))),
                          out_shape=(sds, sds, jax.ShapeDtypeStruct((N_CHIPS, 8, n), F32)),
                          scratch_shapes=[pltpu.VMEM((tm + FFN_HALO, n), F32)],
                          compiler_params=_params(("arbitrary",)), name=name)(
                              dr, dr, *([w_down] * N_CHIPS), *([cw4] * N_CHIPS), up_sm, gc_sm, gp_sm, up_sm, gc_sm)


def ffn_gate_bwd_b(name, dgc_sm, gp_sm, w4, layer, ts=256):
    _, S, n = gp_sm.shape
    kw = w4.shape[2]
    ts = _tile(S, ts, ROW_CHUNK)
    rc = ROW_CHUNK
    nblk = S // ts

    def body(dgc_ref, dgn_ref, gp_ref, w_ref, dgp_ref, acc_ref, dwin_ref):
        i = pl.program_id(1)
        dwin_ref[0:ts, :] = dgc_ref[...].astype(F32)
        dwin_ref[ts:, :] = jnp.where(i == nblk - 1, 0.0, dgn_ref[...].astype(F32))

        @pl.when(i == 0)
        def _():
            acc_ref[...] = jnp.zeros_like(acc_ref)
        sums = [jnp.zeros((8, n), F32) for _ in range(kw + 1)]
        for r0 in range(0, ts, rc):
            gp = gp_ref[r0:r0 + rc, :].astype(F32)
            dgp = jnp.zeros((rc, n), F32)
            for k in range(kw):
                d = kw - 1 - k
                shifted = dwin_ref[r0 + d:r0 + d + rc, :]
                dgp = dgp + w_ref[k:k + 1, :] * shifted
                prod = shifted * gp
                sums[k] = sums[k] + prod[0:8, :] + prod[8:16, :]
                if d == 0:
                    sums[kw] = sums[kw] + shifted[0:8, :] + shifted[8:16, :]
            dgp_ref[r0:r0 + rc, :] = dgp.astype(BF16)
        for k in range(kw):
            acc_ref[k:k + 1, :] += jnp.sum(sums[k], axis=0, keepdims=True)
        acc_ref[7:8, :] += jnp.sum(sums[kw], axis=0, keepdims=True)

    r = ts // FFN_HALO
    main = pl.BlockSpec((None, ts, n), lambda j, i: (j, i, 0))
    nxt = pl.BlockSpec((None, FFN_HALO, n), lambda j, i: (j, jnp.minimum((i + 1) * r, S // FFN_HALO - 1), 0))
    return pl.pallas_call(body, grid=(N_CHIPS, nblk),
                          in_specs=[main, nxt, main, pl.BlockSpec((None, None, kw, n), lambda j, i: (j, layer, 0, 0))],
                          out_specs=(main, pl.BlockSpec((None, 8, n), lambda j, i: (j, 0, 0))),
                          out_shape=(jax.ShapeDtypeStruct((N_CHIPS, S, n), BF16), jax.ShapeDtypeStruct((N_CHIPS, 8, n), F32)),
                          scratch_shapes=[pltpu.VMEM((ts + FFN_HALO, n), F32)],
                          compiler_params=_params(("parallel", "arbitrary")), name=name)(dgc_sm, dgc_sm, gp_sm, w4)


def _neg_softplus(z):
    e = jnp.exp(-jnp.abs(z))
    return -(jnp.maximum(z, 0.0) + jnp.log(1.0 + e)), e


def _split_dot(x, t):
    hi = x.astype(BF16)
    lo = (x - hi.astype(F32)).astype(BF16)
    return jnp.dot(hi, t, preferred_element_type=F32) + jnp.dot(lo, t, preferred_element_type=F32)


STICK_GONE = -100.0
NOT_SWEPT = -1e30


def attn_fwd(name, q, k, v, bq=512, w=256):
    S, D = q.shape
    dh = HEAD_DIM
    hpb = LANES // dh
    bq = _tile(S, bq)
    w = _tile(bq, w)
    nsub = bq // w
    nkb = S // w

    def body(q_ref, k_ref, v_ref, o_ref, runs_ref, rs_ref):
        qi = pl.program_id(1)
        rr = lax.broadcasted_iota(jnp.int32, (w, w), 0)
        cc = lax.broadcasted_iota(jnp.int32, (w, w), 1)
        t_suf = (rr >= cc).astype(BF16)
        tq = qi * bq + lax.broadcasted_iota(jnp.int32, (bq, w), 0)
        tk = lax.broadcasted_iota(jnp.int32, (bq, w), 1)
        lane = lax.broadcasted_iota(jnp.int32, (bq, LANES), 1)
        ntot = (qi + 1) * nsub
        heads = [slice(hh * dh, (hh + 1) * dh) for hh in range(hpb)]
        qbs = [q_ref[:, hs] for hs in heads]
        for hh in range(hpb):
            rs_ref[hh] = jnp.where(lane < ntot, NOT_SWEPT, 0.0)

        def block(kb, carry, masked):
            kstart = pl.multiple_of(kb * w, w)
            if masked:
                m = (tk + kstart) < tq
            out = []
            for hh, hs in enumerate(heads):
                run, acc = carry[2 * hh], carry[2 * hh + 1]
                kblk = k_ref[pl.ds(kstart, w), hs]
                vblk = v_ref[pl.ds(kstart, w), hs]
                z = lax.dot_general(qbs[hh], kblk, NT, preferred_element_type=F32)
                lg, _ = _neg_softplus(z)
                if masked:
                    lg = jnp.where(m, lg, 0.0)
                cum = _split_dot(lg, t_suf) + run
                a = jnp.exp(z + cum)
                if masked:
                    a = jnp.where(m, a, 0.0)
                acc = acc + jnp.dot(a.astype(BF16), vblk, preferred_element_type=F32)
                run = cum[:, 0:1]
                rs_ref[hh] = jnp.where(lane == kb, run, rs_ref[hh])
                out += [run, acc]
            return tuple(out)

        carry = (jnp.zeros((bq, 1), F32), jnp.zeros((bq, dh), F32)) * hpb
        for sb in reversed(range(nsub)):
            carry = block(qi * nsub + sb, carry, True)

        def cond(c):
            alive = functools.reduce(jnp.maximum, [jnp.max(c[1 + 2 * hh]) for hh in range(hpb)])
            return jnp.logical_and(c[0] >= 0, alive > STICK_GONE)

        def step(c):
            return (c[0] - 1,) + block(c[0], c[1:], False)
        carry = lax.while_loop(cond, step, (qi * nsub - 1,) + carry)[1:]
        for hh, hs in enumerate(heads):
            o_ref[:, hs] = carry[2 * hh + 1].astype(o_ref.dtype)
            runs_ref[hh] = rs_ref[hh, :, 0:nkb]

    qs = pl.BlockSpec((bq, LANES), lambda h, i: (i, h))
    kv = pl.BlockSpec((S, LANES), lambda h, i: (0, h))
    return pl.pallas_call(body, grid=(D // LANES, S // bq), in_specs=[qs, kv, kv],
                          out_specs=(qs, pl.BlockSpec((hpb, bq, nkb), lambda h, i: (h, i, 0))),
                          out_shape=(jax.ShapeDtypeStruct((S, D), BF16), jax.ShapeDtypeStruct((D // dh, S, nkb), F32)),
                          scratch_shapes=[pltpu.VMEM((hpb, bq, LANES), F32)],
                          compiler_params=_params(("parallel", "parallel")), name=name)(q, k, v)


def attn_bwd(name, q, k, v, do, runs, dk0=None, dv0=None, bq=512, w=256):
    S, D = q.shape
    dh = HEAD_DIM
    hpb = LANES // dh
    bq = _tile(S, bq)
    w = _tile(bq, w)
    nsub = bq // w
    nkb = S // w
    scale = 1.0 / math.sqrt(dh)
    init = dk0 is not None

    def body(*refs):
        if init:
            q_ref, k_ref, v_ref, do_ref, runs_ref, dk0_ref, dv0_ref, dq_ref, dk_ref, dv_ref, rs_ref = refs
        else:
            q_ref, k_ref, v_ref, do_ref, runs_ref, dq_ref, dk_ref, dv_ref, rs_ref = refs
        qi = pl.program_id(1)

        @pl.when(qi == 0)
        def _():
            dk_ref[...] = dk0_ref[...] if init else jnp.zeros_like(dk_ref)
            dv_ref[...] = dv0_ref[...] if init else jnp.zeros_like(dv_ref)

        rr = lax.broadcasted_iota(jnp.int32, (w, w), 0)
        cc = lax.broadcasted_iota(jnp.int32, (w, w), 1)
        t_suf = (rr >= cc).astype(BF16)
        t_pre = (rr <= cc).astype(BF16)
        tq = qi * bq + lax.broadcasted_iota(jnp.int32, (bq, w), 0)
        tk = lax.broadcasted_iota(jnp.int32, (bq, w), 1)
        lane = lax.broadcasted_iota(jnp.int32, (bq, LANES), 1)
        lane1 = lax.broadcasted_iota(jnp.int32, (1, LANES), 1)
        ntot = (qi + 1) * nsub
        heads = [slice(hh * dh, (hh + 1) * dh) for hh in range(hpb)]
        qbs = [q_ref[:, hs] for hs in heads]
        dobs = [do_ref[:, hs].astype(BF16) for hs in heads]
        kb0 = ntot - nsub
        for hh in range(hpb):
            rs_ref[hh] = jnp.zeros((bq, LANES), F32)
            rs_ref[hh, :, 0:nkb] = runs_ref[hh]
            colmax = jnp.max(rs_ref[hh], axis=0, keepdims=True)
            dead = jnp.logical_and(jnp.logical_and(lane1 >= 1, lane1 <= ntot), colmax <= STICK_GONE)
            kb0 = jnp.minimum(kb0, jnp.sum(dead.astype(jnp.int32)))

        def block(kb, carry, masked):
            kstart = pl.multiple_of(kb * w, w)
            if masked:
                m = (tk + kstart) < tq
            out = []
            for hh, hs in enumerate(heads):
                pg_run, dq = carry[2 * hh], carry[2 * hh + 1]
                qb, dob = qbs[hh], dobs[hh]
                kblk = k_ref[pl.ds(kstart, w), hs]
                vblk = v_ref[pl.ds(kstart, w), hs]
                right = jnp.sum(jnp.where(lane == kb + 1, rs_ref[hh], 0.0), axis=1, keepdims=True)
                z = lax.dot_general(qb, kblk, NT, preferred_element_type=F32)
                lg, e = _neg_softplus(z)
                if masked:
                    lg = jnp.where(m, lg, 0.0)
                a = jnp.exp(z + _split_dot(lg, t_suf) + right)
                if masked:
                    a = jnp.where(m, a, 0.0)
                da = lax.dot_general(dob, vblk, NT, preferred_element_type=F32)
                g = da * a
                pin = _split_dot(g, t_pre) + pg_run
                sig = jnp.where(z >= 0.0, 1.0, e) / (1.0 + e)
                dz = g - sig * pin
                if masked:
                    dz = jnp.where(m, dz, 0.0)
                dzb = dz.astype(BF16)
                dq = dq + jnp.dot(dzb, kblk, preferred_element_type=F32)
                dk_ref[pl.ds(kstart, w), hs] += lax.dot_general(dzb, qb, TN, preferred_element_type=F32)
                dv_ref[pl.ds(kstart, w), hs] += lax.dot_general(a.astype(BF16), dob, TN, preferred_element_type=F32)
                out += [pin[:, w - 1:w], dq]
            return tuple(out)

        carry = (jnp.zeros((bq, 1), F32), jnp.zeros((bq, dh), F32)) * hpb
        carry = lax.fori_loop(kb0, qi * nsub, lambda kb, c: block(kb, c, False), carry)
        for sb in range(nsub):
            carry = block(qi * nsub + sb, carry, True)
        for hh, hs in enumerate(heads):
            dq_ref[:, hs] = carry[2 * hh + 1] * scale

    qs = pl.BlockSpec((bq, LANES), lambda h, i: (i, h))
    kv = pl.BlockSpec((S, LANES), lambda h, i: (0, h))
    ins = [q, k, v, do, runs] + ([dk0, dv0] if init else [])
    specs = [qs, kv, kv, qs, pl.BlockSpec((hpb, bq, nkb), lambda h, i: (h, i, 0))] + ([kv, kv] if init else [])
    sds = jax.ShapeDtypeStruct((S, D), F32)
    return pl.pallas_call(body, grid=(D // LANES, S // bq), in_specs=specs, out_specs=(qs, kv, kv), out_shape=(sds, sds, sds),
                          scratch_shapes=[pltpu.VMEM((hpb, bq, LANES), F32)],
                          compiler_params=_params(("parallel", "arbitrary")), name=name)(*ins)


def loss_head(name, y, tgt, tm=512):
    S, D = y.shape
    tm = _tile(S, tm)

    def body(y_ref, t_ref, dy_ref, acc_ref):
        @pl.when(pl.program_id(0) == 0)
        def _():
            acc_ref[...] = jnp.zeros_like(acc_ref)
        e = y_ref[...] - t_ref[...]
        dy_ref[...] = e * (1.0 / D)
        acc_ref[...] += jnp.sum(e * e)

    row = pl.BlockSpec((tm, D), lambda i: (i, 0))
    return pl.pallas_call(body, grid=(S // tm,), in_specs=[row, row],
                          out_specs=(row, pl.BlockSpec((8, LANES), lambda i: (0, 0))),
                          out_shape=(jax.ShapeDtypeStruct((S, D), F32), jax.ShapeDtypeStruct((8, LANES), F32)),
                          compiler_params=_params(("arbitrary",)), name=name)(y, tgt)


def _to_sm(a, axis=-1):
    axis = axis % a.ndim
    shp = a.shape[:axis] + (N_CHIPS, a.shape[axis] // N_CHIPS) + a.shape[axis + 1:]
    return jnp.moveaxis(a.reshape(shp), axis, 0)


def _from_sm(a, axis=-1):
    nd = a.ndim - 1
    axis = axis % nd
    b = jnp.moveaxis(a, 0, axis)
    return b.reshape(b.shape[:axis] + (b.shape[axis] * b.shape[axis + 1],) + b.shape[axis + 2:])


class GradBuffers:
    def __init__(self, W):
        groups = {}
        for n in BIG:
            _, layers, rows, cols = W[n].shape
            groups.setdefault(cols, []).append((rows, n, layers))
        self.where, self.cols_of, self.buf, self.members = {}, {}, {}, {}
        for cols, items in groups.items():
            off, members = 0, []
            for rows, n, layers in sorted(items, key=lambda t: -t[0]):
                assert off % rows == 0
                self.where[n], self.cols_of[n] = (off, rows), cols
                members.append((n, off, rows * layers))
                off += rows * layers
            assert off % 32 == 0
            self.buf[cols] = lax.empty((N_CHIPS, off, cols), F32)
            self.members[cols] = members

    def put(self, n, layer, fn, **kw):
        cols = self.cols_of[n]
        off, rows = self.where[n]
        self.buf[cols] = fn(into=self.buf[cols], off=off + layer * rows, **kw)


def forward_backward(x, p4, tgt, W):
    S, D = x.shape
    scale = 1.0 / math.sqrt(HEAD_DIM)
    saved = []
    kh = vh = xb_kv = None
    xb = x.astype(BF16)
    for i in range(DEPTH):
        sv = {'xb': xb}
        if i < N_A:
            h_sm, u = mm_pw1_glu(f"pw1glu_{i}", xb, W['a_pw1_w'], i, W['a_pw1_b'])
            c, s = conv_ln_silu_fwd(f"convln_{i}", u, W['a_dw_w'], W['a_dw_b'], W['a_ln_g'], W['a_ln_b'], i)
            x1, x1b, xh1, rs1 = mm_rowsm_ln(f"pw2ln_{i}", s, W['a_pw2_w'], i, x, W['ln_mix_g'], W['ln_mix_b'], i, bias=W['a_pw2_b'])
            sv.update(h_sm=h_sm, u=u, c=c, s=s)
        else:
            j = i - N_A
            if kh is None:
                xb_kv = xb
                kh = mm_rowsm("wk", xb, W['kv_wk'], 0, out_dtype=BF16)
                vh = mm_rowsm("wv", xb, W['kv_wv'], 0, out_dtype=BF16)
            qh = mm_rowsm(f"wq_{j}", xb, W['b_wq'], j, out_dtype=BF16, out_scale=scale)
            o, runs = attn_fwd(f"attn_{j}", qh, kh, vh)
            x1, x1b, xh1, rs1 = mm_rowsm_ln(f"woln_{j}", o, W['b_wo'], j, x, W['ln_mix_g'], W['ln_mix_b'], i)
            sv.update(qh=qh, o=o, runs=runs)
        up_sm, gp_sm, gc_sm, hf_sm = ffn_up_gate_fwd(f"upgate_{i}", x1b, W['ffn_w_up'], W['ffn_w_gate'], W['ffn_conv_w'],
                                                     W['ffn_conv_b'], i)
        tg = mm_rowsm(f"plegate_{i}", x1b, W['ple_w_gate'], i)
        pp = mm_proj(f"pleproj_{i}", p4, i, W['ple_w_proj'])
        x2, x2b, xh2, rs2 = mm_rowsm_ln(f"downln_{i}", hf_sm, W['ffn_w_down'], i, x1, W['ln_ffn_g'], W['ln_ffn_b'], i, a_sm=True,
                                        tg=tg, pp=pp)
        sv.update(x1b=x1b, xh1=xh1, rs1=rs1, up_sm=up_sm, gp_sm=gp_sm, gc_sm=gc_sm, hf_sm=hf_sm, tg=tg, pp=pp, xh2=xh2, rs2=rs2)
        saved.append(sv)
        x, xb = x2, x2b

    dx, lacc = loss_head("loss", x, tgt)
    loss_sum = lacc[0, 0]

    G = {n: [None] * DEPTH for n in WEIGHTS if n not in BIG}
    gb = GradBuffers(W)
    dk = dv = None
    for i in reversed(range(DEPTH)):
        sv = saved[i]
        dr, acc, dtg, dpp = ln_bwd(f"lnffn_b_{i}", dx, sv['xh2'], sv['rs2'], W['ln_ffn_g'], i, tg=sv['tg'], pp=sv['pp'])
        G['ln_ffn_g'][i], G['ln_ffn_b'][i] = acc[0], acc[1]
        gb.put('ple_w_proj', i, functools.partial(mm_tn_proj, f"dproj_{i}", p4, i, dpp))
        gb.put('ple_w_gate', i, functools.partial(mm_tn_rowsm_fan, f"dplegate_{i}", sv['x1b'], dtg))
        gb.put('ffn_w_down', i, functools.partial(mm_tn_rowsm, f"ddown_{i}", sv['hf_sm'], dr, a_sm=True))
        dup_sm, dgp_sm, cacc = ffn_gate_conv_bwd(f"gateconv_b_{i}", dr, W['ffn_w_down'], W['ffn_conv_w'], i, sv['up_sm'],
                                                 sv['gc_sm'], sv['gp_sm'])
        kw = W['ffn_conv_w'].shape[2]
        G['ffn_conv_w'][i] = cacc[:, 0:kw, :]
        G['ffn_conv_b'][i] = cacc[:, 7, :].reshape(-1)
        gb.put('ffn_w_up', i, functools.partial(mm_tn_colsm, f"dup_{i}", sv['x1b'], dup_sm))
        gb.put('ffn_w_gate', i, functools.partial(mm_tn_colsm, f"dgate_{i}", sv['x1b'], dgp_sm))
        dx1 = mm_nt_rowsm(f"dx1a_{i}", dtg, W['ple_w_gate'], i, res=dr, res_alpha=DN_ALPHA)
        dx1 = mm_nt_colsm(f"dx1b_{i}", _sm_parts(dup_sm), W['ffn_w_up'], i, res=dx1)
        dx1 = mm_nt_colsm(f"dx1c_{i}", _sm_parts(dgp_sm), W['ffn_w_gate'], i, res=dx1)

        dr1, acc1 = ln_bwd(f"lnmix_b_{i}", dx1, sv['xh1'], sv['rs1'], W['ln_mix_g'], i)
        G['ln_mix_g'][i], G['ln_mix_b'][i] = acc1[0], acc1[1]
        xin = sv['xb']
        if i < N_A:
            G['a_pw2_b'][i] = acc1[2]
            gb.put('a_pw2_w', i, functools.partial(mm_tn_rowsm_fan, f"dpw2_{i}", sv['s'], dr1))
            ds = mm_nt_rowsm(f"ds_{i}", dr1, W['a_pw2_w'], i)
            dc, cacc = ln_silu_bwd(f"lnsilu_b_{i}", ds, sv['c'], W['a_ln_g'], W['a_ln_b'], i)
            G['a_ln_g'][i], G['a_ln_b'][i], G['a_dw_b'][i] = cacc[0], cacc[1], cacc[2]
            da, dg, dw, dba, dbg = conv_glu_bwd(f"convglu_b_{i}", dc, sv['u'], sv['h_sm'], W['a_dw_w'], i)
            kw = W['a_dw_w'].shape[1]
            G['a_dw_w'][i] = _from_sm(dw[:, 0:kw, :], axis=-1)
            G['a_pw1_b'][i] = jnp.concatenate([dba[:, 0, :], dbg[:, 0, :]], axis=0)
            half = da.shape[0]
            gb.put('a_pw1_w', i, functools.partial(mm_tn_colsm, f"dpw1a_{i}", xin, da))
            gb.put('a_pw1_w', i, functools.partial(mm_tn_colsm, f"dpw1g_{i}", xin, dg), j0=half)
            dx = mm_nt_colsm(f"dxa_{i}", _sm_parts(da) + _sm_parts(dg), W['a_pw1_w'], i, res=dr1, res_alpha=DN_ALPHA)
        else:
            j = i - N_A
            gb.put('b_wo', j, functools.partial(mm_tn_rowsm_fan, f"dwo_{j}", sv['o'], dr1))
            do = mm_nt_rowsm(f"do_{j}", dr1, W['b_wo'], j)
            dq, dk, dv = attn_bwd(f"attn_b_{j}", sv['qh'], kh, vh, do, sv['runs'], dk, dv)
            gb.put('b_wq', j, functools.partial(mm_tn_rowsm_fan, f"dwq_{j}", xin, dq))
            dx = mm_nt_rowsm(f"dxq_{j}", dq, W['b_wq'], j, res=dr1, res_alpha=DN_ALPHA)
            if j == 0:
                gb.put('kv_wk', 0, functools.partial(mm_tn_rowsm_fan, "dwk", xb_kv, dk))
                gb.put('kv_wv', 0, functools.partial(mm_tn_rowsm_fan, "dwv", xb_kv, dv))
                dx = mm_nt_rowsm("dxk", dk, W['kv_wk'], 0, res=dx)
                dx = mm_nt_rowsm("dxv", dv, W['kv_wv'], 0, res=dx)
    return loss_sum, dx, G, gb


MESH = pl.DeviceIdType.MESH
HBM = pl.BlockSpec(memory_space=pltpu.HBM)


def _place():
    x, y, c = lax.axis_index("x"), lax.axis_index("y"), lax.axis_index("c")
    others = [(1 - x, y), (x, 1 - y), (1 - x, 1 - y)]
    return x, y, c, others


def allgather_chips(name, arrs):
    n = len(arrs)

    def body(*refs):
        ins, outs = refs[:n], refs[n:2 * n]
        send_sems, recv_sems = refs[2 * n:]
        x, y, c, others = _place()
        me = 2 * x + y
        sibling = (x, y, 1 - c)
        ids = [2 * ch[0] + ch[1] for ch in others]
        from_id = jnp.where(c == 0, ids[0], ids[1])
        to_chip = (jnp.where(c == 0, x, 1 - x), jnp.where(c == 0, 1 - y, y))

        def remote(a, k, src, chip_id, half, to):
            return pltpu.make_async_remote_copy(src_ref=src, dst_ref=outs[a].at[chip_id, half], send_sem=send_sems.at[a, k],
                                                recv_sem=recv_sems.at[a, k], device_id=to, device_id_type=MESH)

        sent = [remote(a, k, ins[a].at[c], me, c, (others[k][0], others[k][1], c)) for a in range(n) for k in range(2)]
        for cp in sent:
            cp.start()
        for a in range(n):
            for k in range(2):
                remote(a, k, ins[a].at[c], ids[k], c, sibling).wait_recv()
            sent.append(remote(a, 2, outs[a].at[from_id, c], from_id, c, (to_chip[0], to_chip[1], c)))
            sent[-1].start()
            for k in range(2):
                sent.append(remote(a, 3 + k, outs[a].at[ids[k], c], ids[k], c, sibling))
                sent[-1].start()
        for a in range(n):
            remote(a, 2, ins[a].at[c], ids[2], c, sibling).wait_recv()
            sent.append(remote(a, 5, outs[a].at[ids[2], c], ids[2], c, sibling))
            sent[-1].start()
        for a in range(n):
            for k in range(3):
                remote(a, 3 + k, ins[a].at[c], ids[k], 1 - c, sibling).wait_recv()
        for cp in sent:
            cp.wait_send()

    outs = pl.pallas_call(body, out_shape=tuple(jax.ShapeDtypeStruct((N_CHIPS,) + a.shape, a.dtype) for a in arrs),
                          in_specs=[HBM] * n, out_specs=tuple([HBM] * n),
                          scratch_shapes=[pltpu.SemaphoreType.DMA((n, 6)), pltpu.SemaphoreType.DMA((n, 6))],
                          name=name)(*arrs)
    me = 2 * lax.axis_index("x") + lax.axis_index("y")
    return [lax.dynamic_update_index_in_dim(o, a, me, 0) for o, a in zip(outs, arrs)]


def exchange_sibling(name, gs):
    n = len(gs)

    def body(*refs):
        g_refs, o_refs = refs[:n], refs[n:2 * n]
        send_sems, recv_sems = refs[2 * n:]
        x, y, c, _ = _place()
        cps = [pltpu.make_async_remote_copy(src_ref=g_refs[a].at[j, 1 - c], dst_ref=o_refs[a].at[j], send_sem=send_sems.at[a, j],
                                            recv_sem=recv_sems.at[a, j], device_id=(x, y, 1 - c), device_id_type=MESH)
               for a in range(n) for j in range(N_CHIPS)]
        for cp in cps:
            cp.start()
        for cp in cps:
            cp.wait()

    return pl.pallas_call(body, out_shape=tuple(jax.ShapeDtypeStruct((N_CHIPS,) + g.shape[2:], g.dtype) for g in gs),
                          in_specs=[HBM] * n, out_specs=tuple([HBM] * n),
                          scratch_shapes=[pltpu.SemaphoreType.DMA((n, N_CHIPS)), pltpu.SemaphoreType.DMA((n, N_CHIPS))],
                          name=name)(*gs)


def _ring_peers():
    x, y, c, _ = _place()
    first = (jnp.where(c == 0, 1 - x, x), jnp.where(c == 0, y, 1 - y))
    second = (jnp.where(c == 0, x, 1 - x), jnp.where(c == 0, 1 - y, y))
    return c, first, second, 2 * (1 - x) + (1 - y)


def exchange_first(name, ss):
    n = len(ss)

    def body(*refs):
        s_refs, o_refs = refs[:n], refs[n:2 * n]
        send_sems, recv_sems = refs[2 * n:]
        c, first, _, diag = _ring_peers()
        cps = [pltpu.make_async_remote_copy(src_ref=s_refs[a].at[slot], dst_ref=o_refs[a].at[k], send_sem=send_sems.at[a, k],
                                            recv_sem=recv_sems.at[a, k], device_id=(first[0], first[1], c), device_id_type=MESH)
               for a in range(n) for k, slot in enumerate((2 * first[0] + first[1], diag))]
        for cp in cps:
            cp.start()
        for cp in cps:
            cp.wait()

    return pl.pallas_call(body, out_shape=tuple(jax.ShapeDtypeStruct((2,) + s.shape[1:], s.dtype) for s in ss),
                          in_specs=[HBM] * n, out_specs=tuple([HBM] * n),
                          scratch_shapes=[pltpu.SemaphoreType.DMA((n, 2)), pltpu.SemaphoreType.DMA((n, 2))], name=name)(*ss)


def exchange_second(name, ts):
    n = len(ts)

    def body(*refs):
        t_refs, o_refs = refs[:n], refs[n:2 * n]
        send_sems, recv_sems = refs[2 * n:]
        c, _, second, _ = _ring_peers()
        cps = [pltpu.make_async_remote_copy(src_ref=t_refs[a], dst_ref=o_refs[a], send_sem=send_sems.at[a], recv_sem=recv_sems.at[a],
                                            device_id=(second[0], second[1], c), device_id_type=MESH) for a in range(n)]
        for cp in cps:
            cp.start()
        for cp in cps:
            cp.wait()

    return pl.pallas_call(body, out_shape=tuple(jax.ShapeDtypeStruct(t.shape, t.dtype) for t in ts),
                          in_specs=[HBM] * n, out_specs=tuple([HBM] * n),
                          scratch_shapes=[pltpu.SemaphoreType.DMA((n,)), pltpu.SemaphoreType.DMA((n,))], name=name)(*ts)


def share_sibling(name, ts):
    n = len(ts)

    def body(*refs):
        o_refs = refs[n:2 * n]
        send_sems, recv_sems = refs[2 * n:]
        x, y, c, _ = _place()
        cps = [pltpu.make_async_remote_copy(src_ref=o_refs[a].at[c], dst_ref=o_refs[a].at[c], send_sem=send_sems.at[a],
                                            recv_sem=recv_sems.at[a], device_id=(x, y, 1 - c), device_id_type=MESH)
               for a in range(n)]
        for cp in cps:
            cp.start()
        for a in range(n):
            pltpu.make_async_remote_copy(src_ref=o_refs[a].at[c], dst_ref=o_refs[a].at[1 - c], send_sem=send_sems.at[a],
                                         recv_sem=recv_sems.at[a], device_id=(x, y, 1 - c), device_id_type=MESH).wait_recv()
        for cp in cps:
            cp.wait_send()

    return pl.pallas_call(body, out_shape=tuple(jax.ShapeDtypeStruct(t.shape, t.dtype) for t in ts),
                          in_specs=[HBM] * n, out_specs=tuple([HBM] * n), input_output_aliases={a: a for a in range(n)},
                          scratch_shapes=[pltpu.SemaphoreType.DMA((n,)), pltpu.SemaphoreType.DMA((n,))],
                          name=name)(*ts)


def add_halves(name, g, recv, place, out_dtype, tr=512):
    _, _, R, C = g.shape
    tr = _tile(R, tr, 16)

    def body(p_ref, a_ref, b_ref, o_ref):
        o_ref[...] = (a_ref[...] + b_ref[...]).astype(o_ref.dtype)

    blk = pl.BlockSpec((None, tr, C), lambda j, i, p: (j, i, 0))
    gs = pltpu.PrefetchScalarGridSpec(num_scalar_prefetch=1, grid=(N_CHIPS, R // tr),
                                      in_specs=[pl.BlockSpec((None, None, tr, C), lambda j, i, p: (j, p[0], i, 0)), blk],
                                      out_specs=blk)
    return pl.pallas_call(body, grid_spec=gs, out_shape=jax.ShapeDtypeStruct((N_CHIPS, R, C), out_dtype),
                          compiler_params=_params(("parallel", "parallel")), name=name)(place, g, recv)


def add_pass_on(name, s, got, place, tr=512):
    _, R, C = s.shape
    tr = _tile(R, tr, 16)

    def body(p_ref, a_ref, b_ref, o_ref):
        o_ref[...] = (a_ref[...].astype(F32) + b_ref[...].astype(F32)).astype(o_ref.dtype)

    gs = pltpu.PrefetchScalarGridSpec(num_scalar_prefetch=1, grid=(R // tr,),
                                      in_specs=[pl.BlockSpec((None, tr, C), lambda i, p: (p[2], i, 0)),
                                                pl.BlockSpec((None, tr, C), lambda i, p: (1, i, 0))],
                                      out_specs=pl.BlockSpec((tr, C), lambda i, p: (i, 0)))
    return pl.pallas_call(body, grid_spec=gs, out_shape=jax.ShapeDtypeStruct((R, C), s.dtype),
                          compiler_params=_params(("parallel",)), name=name)(place, s, got)


def add_chips(name, g, r1, got1, got2, place, tr=512):
    _, _, R, C = g.shape
    tr = _tile(R, tr, 16)

    def body(p_ref, a_ref, b_ref, c_ref, d_ref, o_ref):
        o_ref[...] = ((a_ref[...] + b_ref[...]) + c_ref[...].astype(F32)) + d_ref[...].astype(F32)

    gs = pltpu.PrefetchScalarGridSpec(num_scalar_prefetch=1, grid=(R // tr,),
                                      in_specs=[pl.BlockSpec((None, None, tr, C), lambda i, p: (p[1], p[0], i, 0)),
                                                pl.BlockSpec((None, tr, C), lambda i, p: (p[1], i, 0)),
                                                pl.BlockSpec((None, tr, C), lambda i, p: (0, i, 0)),
                                                pl.BlockSpec((tr, C), lambda i, p: (i, 0))],
                                      out_specs=pl.BlockSpec((None, tr, C), lambda i, p: (p[0], i, 0)))
    return pl.pallas_call(body, grid_spec=gs, out_shape=jax.ShapeDtypeStruct((2, R, C), F32),
                          compiler_params=_params(("parallel",)), name=name)(place, g, r1, got1, got2)


def reduce_scatter(gs, wire_dtypes, place):
    r1 = exchange_sibling("rs_sibling", gs)
    s1 = [add_halves(f"rs_add_cores_{a}", g, r, place, dt) for a, (g, r, dt) in enumerate(zip(gs, r1, wire_dtypes))]
    got1 = exchange_first("rs_first", s1)
    t = [add_pass_on(f"rs_add_pass_{a}", s, g1, place) for a, (s, g1) in enumerate(zip(s1, got1))]
    got2 = exchange_second("rs_second", t)
    tot = [add_chips(f"rs_add_chips_{a}", g, r, g1, g2, place) for a, (g, r, g1, g2) in enumerate(zip(gs, r1, got1, got2))]
    return share_sibling("rs_share", tot)


def adamw(name, w, g, m, v, tr=512):
    shp = w.shape
    cols = shp[-1]
    w2, g2, m2, v2 = (a.reshape(-1, cols) for a in (w, g, m, v))
    rows = w2.shape[0]
    tr = _tile(rows, tr)

    def body(w_ref, g_ref, m_ref, v_ref, d_ref, mo_ref, vo_ref):
        g_ = g_ref[...]
        m_ = ADAM_B1 * m_ref[...] + (1.0 - ADAM_B1) * g_
        v_ = ADAM_B2 * v_ref[...] + (1.0 - ADAM_B2) * (g_ * g_)
        m_hat = m_ / (1.0 - ADAM_B1 ** ADAM_STEP)
        v_hat = v_ / (1.0 - ADAM_B2 ** ADAM_STEP)
        d_ref[...] = -ADAM_LR * (m_hat / (jnp.sqrt(v_hat) + ADAM_EPS) + ADAM_WD * w_ref[...])
        mo_ref[...] = m_
        vo_ref[...] = v_

    blk = pl.BlockSpec((tr, cols), lambda i: (i, 0))
    sds = jax.ShapeDtypeStruct((rows, cols), F32)
    d, mo, vo = pl.pallas_call(body, grid=(rows // tr,), in_specs=[blk] * 4, out_specs=(blk, blk, blk), out_shape=(sds, sds, sds),
                               compiler_params=_params(("parallel",)), name=name)(w2, g2, m2, v2)
    return d.reshape(shp), mo.reshape(shp), vo.reshape(shp)


PACK_ALIGN = 1024


def _pad_to(a, mult, axis=-1):
    axis = axis % a.ndim
    extra = (-a.shape[axis]) % mult
    if extra == 0:
        return a
    pads = [(0, 0)] * a.ndim
    pads[axis] = (0, extra)
    return jnp.pad(a, pads)


def _pack(pieces, lead, row_mult):
    nl = len(lead)
    flat, offs, sizes, off = [], [], [], 0
    for a in pieces:
        f = a.reshape(lead + (-1,))
        sizes.append(f.shape[-1])
        f = _pad_to(f, PACK_ALIGN)
        offs.append(off)
        off += f.shape[-1]
        flat.append(f)
    cat = _pad_to(jnp.concatenate(flat, axis=nl), 2 * row_mult * LANES)
    return cat.reshape(lead + (2, -1, LANES)), offs, sizes


def _unpack(packed, lead, offs, sizes, shapes):
    flat = packed.reshape(lead + (-1,))
    return [lax.slice_in_dim(flat, o, o + s, axis=len(lead)).reshape(lead + tuple(shp)) for o, s, shp in zip(offs, sizes, shapes)]


def _stack_grads(G, names):
    out = {}
    for n in names:
        parts = [g for g in G[n] if g is not None]
        if n in ('kv_wk', 'kv_wv'):
            out[n] = parts[0]
        elif n in REPLICATED:
            out[n] = jnp.stack(parts, axis=0).reshape(N_CHIPS, -1)
        elif n in ('a_dw_w', 'a_dw_b', 'a_ln_g', 'a_ln_b', 'a_pw2_b'):
            out[n] = _to_sm(jnp.stack(parts, axis=0), axis=-1)
        else:
            out[n] = jnp.stack(parts, axis=1)
    return out


def _whole_weights(big, small, rep, D):
    W = {}
    for n in BIG:
        a = big[n]
        W[n] = a[:, None] if n in ('kv_wk', 'kv_wv') else a
    W['a_pw1_b'] = small['a_pw1_b'][:, :, None, :]
    W['a_dw_w'] = _from_sm(small['a_dw_w'], axis=-1)
    for n in ('a_dw_b', 'a_ln_g', 'a_ln_b', 'a_pw2_b'):
        W[n] = _from_sm(small[n], axis=-1)[:, None, :]
    W['ffn_conv_w'] = small['ffn_conv_w']
    L, F = rep['ffn_conv_b'].shape
    W['ffn_conv_b'] = rep['ffn_conv_b'].reshape(L, N_CHIPS, 1, F // N_CHIPS)
    for n in ('ln_mix_g', 'ln_mix_b', 'ln_ffn_g', 'ln_ffn_b'):
        W[n] = rep[n][:, None, :]
    return W


SMALL = ('a_pw1_b', 'a_dw_w', 'a_dw_b', 'a_ln_g', 'a_ln_b', 'a_pw2_b', 'ffn_conv_w')


def _step(x, p, loss_target, w, m, v):
    S, D = x.shape[-2:]
    x2, tgt = x.reshape(S, D), loss_target.reshape(S, D)
    ax, ay, ac = lax.axis_index("x"), lax.axis_index("y"), lax.axis_index("c")
    second = jnp.where(ac == 0, 2 * ax + (1 - ay), 2 * (1 - ax) + ay)
    place = jnp.stack([ac, 2 * ax + ay, second]).astype(jnp.int32)

    big_in = [w[n].astype(BF16).reshape((2, -1) + w[n].shape[1:] if w[n].ndim == 3 else (2, -1, w[n].shape[-1])) for n in BIG]
    small_in, s_offs, s_sizes = _pack([w[n] for n in SMALL], (), 8)
    gathered = allgather_chips("gather_weights", big_in + [small_in])
    big = {n: g.reshape((N_CHIPS,) + w[n].shape) for n, g in zip(BIG, gathered[:-1])}
    small = dict(zip(SMALL, _unpack(gathered[-1], (N_CHIPS,), s_offs, s_sizes, [w[n].shape for n in SMALL])))
    W = _whole_weights(big, small, {n: w[n] for n in REPLICATED}, D)

    loss_sum, dx, G, gb = forward_backward(x2, p, tgt, W)
    loss = lax.psum(0.5 * loss_sum / D, ("x", "y", "c"))

    vectors = [n for n in WEIGHTS if n not in BIG]
    mats = [b.reshape(N_CHIPS, 2, b.shape[1] // 2, b.shape[2]) for b in gb.buf.values()]
    members = [gb.members[cols] for cols in gb.buf]
    g_sm = _stack_grads(G, vectors)
    packed, offs, sizes = _pack([g_sm[n] for n in vectors], (N_CHIPS,), 512)
    reduced = reduce_scatter(mats + [packed], [BF16] * len(mats) + [F32], place)
    shapes = [w[n].shape if n not in REPLICATED else (w[n].size // N_CHIPS,) for n in vectors]
    g_mine = dict(zip(vectors, _unpack(reduced[-1], (), offs, sizes, shapes)))
    for red, where in zip(reduced[:-1], members):
        rows = red.reshape(-1, red.shape[-1])
        for n, off, cnt in where:
            g_mine[n] = lax.slice_in_dim(rows, off, off + cnt, axis=0).reshape(w[n].shape)
    rep_in, r_offs, r_sizes = _pack([g_mine[n] for n in REPLICATED], (), 8)
    rep_all = allgather_chips("gather_replicated_grads", [rep_in])[0]
    for n, g in zip(REPLICATED, _unpack(rep_all, (N_CHIPS,), r_offs, r_sizes, [(w[n].size // N_CHIPS,) for n in REPLICATED])):
        g_mine[n] = g.reshape(w[n].shape)

    grads, deltas, new_m, new_v = [], [], [], []
    for n in WEIGHTS:
        d, mo, vo = adamw(f"adamw_{n}", w[n], g_mine[n], m[n], v[n])
        grads.append(g_mine[n])
        deltas.append(d)
        new_m.append(mo)
        new_v.append(vo)
    return (loss, dx.reshape(x.shape), *grads, *deltas, *new_m, *new_v)


def kernel(x, p, a_pw1_w, a_pw1_b, a_dw_w, a_dw_b, a_ln_g, a_ln_b, a_pw2_w, a_pw2_b, b_wq, kv_wk, kv_wv, b_wo, ln_mix_g, ln_mix_b, ffn_w_up, ffn_w_gate, ffn_conv_w, ffn_conv_b, ffn_w_down, ple_w_gate, ple_w_proj, ln_ffn_g, ln_ffn_b, loss_target, m_a_pw1_w, m_a_pw1_b, m_a_dw_w, m_a_dw_b, m_a_ln_g, m_a_ln_b, m_a_pw2_w, m_a_pw2_b, m_b_wq, m_kv_wk, m_kv_wv, m_b_wo, m_ln_mix_g, m_ln_mix_b, m_ffn_w_up, m_ffn_w_gate, m_ffn_conv_w, m_ffn_conv_b, m_ffn_w_down, m_ple_w_gate, m_ple_w_proj, m_ln_ffn_g, m_ln_ffn_b, v_a_pw1_w, v_a_pw1_b, v_a_dw_w, v_a_dw_b, v_a_ln_g, v_a_ln_b, v_a_pw2_w, v_a_pw2_b, v_b_wq, v_kv_wk, v_kv_wv, v_b_wo, v_ln_mix_g, v_ln_mix_b, v_ffn_w_up, v_ffn_w_gate, v_ffn_conv_w, v_ffn_conv_b, v_ffn_w_down, v_ple_w_gate, v_ple_w_proj, v_ln_ffn_g, v_ln_ffn_b):
    vals = dict(locals())
    w = {n: vals[n] for n in WEIGHTS}
    m = {n: vals["m_" + n] for n in WEIGHTS}
    v = {n: vals["v_" + n] for n in WEIGHTS}
    return _step(x, p, loss_target, w, m, v)
```

```python
import functools
import math

import jax
import jax.numpy as jnp
import numpy as np
from jax import lax
from jax.experimental import pallas as pl
from jax.experimental.pallas import tpu as pltpu

F32, BF16 = jnp.float32, jnp.bfloat16

HEAD_DIM = 64
LN_EPS = 1e-5
DEPTH = 4
N_A = DEPTH // 2
DN_ALPHA = (2.0 * DEPTH) ** 0.25
N_CHIPS = 4

ADAM_LR, ADAM_B1, ADAM_B2, ADAM_EPS, ADAM_WD, ADAM_STEP = 0.001, 0.9, 0.999, 1e-08, 0.01, 10

VMEM_LIMIT_BYTES = 56 * 2**20
LANES = 128
CONV_HALO = 32
FFN_HALO = 16

NN = (((1,), (0,)), ((), ()))
NT = (((1,), (1,)), ((), ()))
TN = (((0,), (0,)), ((), ()))

WEIGHTS = ['a_pw1_w', 'a_pw1_b', 'a_dw_w', 'a_dw_b', 'a_ln_g', 'a_ln_b', 'a_pw2_w', 'a_pw2_b', 'b_wq', 'kv_wk', 'kv_wv',
           'b_wo', 'ln_mix_g', 'ln_mix_b', 'ffn_w_up', 'ffn_w_gate', 'ffn_conv_w', 'ffn_conv_b', 'ffn_w_down', 'ple_w_gate',
           'ple_w_proj', 'ln_ffn_g', 'ln_ffn_b']
REPLICATED = ('ln_mix_g', 'ln_mix_b', 'ffn_conv_b', 'ln_ffn_g', 'ln_ffn_b')
BIG = ('a_pw1_w', 'a_pw2_w', 'b_wq', 'kv_wk', 'kv_wv', 'b_wo', 'ffn_w_up', 'ffn_w_gate', 'ffn_w_down', 'ple_w_gate',
       'ple_w_proj')


def _tile(n, pref, mult=8):
    t = min(n, pref)
    while t > 0:
        if n % t == 0 and t % mult == 0:
            return t
        t -= 1
    return n


def _params(sem):
    return pltpu.CompilerParams(dimension_semantics=sem, vmem_limit_bytes=VMEM_LIMIT_BYTES)


def _sigmoid(x):
    return 0.5 * jnp.tanh(0.5 * x) + 0.5


def _mm_call(name, grid, terms, out_spec, out_sds, dims, bias=None, res=None, res_alpha=1.0, out_scale=None, into=None):
    n_terms = len(terms)

    def body(*refs):
        o_ref = refs[-1]
        acc = None
        for t in range(n_terms):
            a = refs[2 * t][...].astype(BF16)
            b = refs[2 * t + 1][...].astype(BF16)
            d = lax.dot_general(a, b, dims, preferred_element_type=F32)
            acc = d if acc is None else acc + d
        k = 2 * n_terms
        if bias is not None:
            acc = acc + refs[k][...]
            k += 1
        if res is not None:
            acc = acc + res_alpha * refs[k][...]
        if out_scale is not None:
            acc = acc * out_scale
        o_ref[...] = acc.astype(o_ref.dtype)

    operands, specs = [], []
    for a, a_spec, b, b_spec in terms:
        operands += [a, b]
        specs += [a_spec, b_spec]
    for extra in (bias, res):
        if extra is not None:
            operands.append(extra[0])
            specs.append(extra[1])
    aliases = {}
    if into is not None:
        aliases = {len(operands): 0}
        operands.append(into)
        specs.append(pl.BlockSpec(memory_space=pl.ANY))
        out_sds = jax.ShapeDtypeStruct(into.shape, into.dtype)
    return pl.pallas_call(body, out_shape=out_sds, grid=grid, in_specs=specs, out_specs=out_spec, input_output_aliases=aliases,
                          compiler_params=_params(("parallel",) * len(grid)), name=name)(*operands)


def _mm_fanout(name, a, a_spec, w4, w_block, layer, dims, out_spec, out_sds, store, grid, bias4=None, res=None, res_alpha=1.0):
    def body(*refs):
        a_ref, w_refs, o_ref = refs[0], refs[1:1 + N_CHIPS], refs[-1]
        k = 1 + N_CHIPS
        b_refs = refs[k:k + N_CHIPS] if bias4 is not None else None
        k += N_CHIPS if bias4 is not None else 0
        av = a_ref[...].astype(BF16)
        for j in range(N_CHIPS):
            d = lax.dot_general(av, w_refs[j][...].astype(BF16), dims, preferred_element_type=F32)
            if b_refs is not None:
                d = d + b_refs[j][...]
            if res is not None:
                d = d + res_alpha * res[2](refs[k], j)
            store(o_ref, j, d)

    nd = len(grid)
    operands = [a] + [w4] * N_CHIPS
    specs = [a_spec] + [pl.BlockSpec((None, None) + w_block, lambda *g, j=j: (j, layer, 0, 0)) for j in range(N_CHIPS)]
    if bias4 is not None:
        operands += [bias4] * N_CHIPS
        specs += [pl.BlockSpec((None, None, 1, bias4.shape[-1]), lambda *g, j=j: (j, layer, 0, 0)) for j in range(N_CHIPS)]
    if res is not None:
        operands.append(res[0])
        specs.append(res[1])
    return pl.pallas_call(body, out_shape=out_sds, grid=grid, in_specs=specs, out_specs=out_spec,
                          compiler_params=_params(("parallel",) * nd), name=name)(*operands)


def _store_slot(o_ref, j, d):
    o_ref[j] = d.astype(o_ref.dtype)


def mm_colsm(name, x, w4, layer, bias4=None, out_dtype=F32, tm=1024):
    M, K = x.shape
    n = w4.shape[-1]
    tm = _tile(M, tm)
    return _mm_fanout(name, x, pl.BlockSpec((tm, K), lambda i: (i, 0)), w4, (K, n), layer, NN,
                      pl.BlockSpec((N_CHIPS, tm, n), lambda i: (0, i, 0)), jax.ShapeDtypeStruct((N_CHIPS, M, n), out_dtype),
                      _store_slot, (M // tm,), bias4=bias4)


def mm_rowsm(name, a, w4, layer, a_sm=False, bias=None, out_dtype=F32, out_scale=None, tm=512):
    kc, N = w4.shape[-2:]
    M = a.shape[-2]
    tm = _tile(M, tm)
    terms = []
    for j in range(N_CHIPS):
        if a_sm:
            a_spec = pl.BlockSpec((None, tm, kc), lambda i, j=j: (j, i, 0))
        else:
            a_spec = pl.BlockSpec((tm, kc), lambda i, j=j: (i, j))
        terms.append((a, a_spec, w4, pl.BlockSpec((None, None, kc, N), lambda i, j=j: (j, layer, 0, 0))))
    b = None if bias is None else (bias, pl.BlockSpec((None, 1, N), lambda i: (layer, 0, 0)))
    return _mm_call(name, (M // tm,), terms, pl.BlockSpec((tm, N), lambda i: (i, 0)), jax.ShapeDtypeStruct((M, N), out_dtype),
                    NN, bias=b, out_scale=out_scale)


def mm_rowsm_ln(name, a, w4, layer, x, g, b, ln_layer, a_sm=False, bias=None, tg=None, pp=None, tm=256):
    kc, N = w4.shape[-2:]
    M = a.shape[-2]
    tm = _tile(M, tm, 16)
    ple = tg is not None
    nb = bias is not None

    def body(*refs):
        a_refs, w_refs = refs[0:N_CHIPS], refs[N_CHIPS:2 * N_CHIPS]
        k = 2 * N_CHIPS
        acc = None
        for j in range(N_CHIPS):
            d = jnp.dot(a_refs[j][...].astype(BF16), w_refs[j][...].astype(BF16), preferred_element_type=F32)
            acc = d if acc is None else acc + d
        if nb:
            acc = acc + refs[k][...]
            k += 1
        r = DN_ALPHA * refs[k][...] + acc
        k += 1
        if ple:
            r = r + _sigmoid(refs[k][...]) * refs[k + 1][...]
            k += 2
        g_ref, b_ref, y_ref, yb_ref, xh_ref, rs_ref = refs[k:]
        mu = jnp.mean(r, axis=-1, keepdims=True)
        dd = r - mu
        var = jnp.mean(dd * dd, axis=-1, keepdims=True)
        rstd = lax.rsqrt(var + LN_EPS)
        xh = dd * rstd
        y = xh * g_ref[...] + b_ref[...]
        y_ref[...] = y
        yb_ref[...] = y.astype(BF16)
        xh_ref[...] = xh
        rs_ref[...] = rstd

    row = pl.BlockSpec((tm, N), lambda i: (i, 0))
    vec = pl.BlockSpec((None, 1, N), lambda i: (ln_layer, 0, 0))
    if a_sm:
        a_specs = [pl.BlockSpec((None, tm, kc), lambda i, j=j: (j, i, 0)) for j in range(N_CHIPS)]
    else:
        a_specs = [pl.BlockSpec((tm, kc), lambda i, j=j: (i, j)) for j in range(N_CHIPS)]
    w_specs = [pl.BlockSpec((None, None, kc, N), lambda i, j=j: (j, layer, 0, 0)) for j in range(N_CHIPS)]
    ins = [a] * N_CHIPS + [w4] * N_CHIPS + ([bias] if nb else []) + [x] + ([tg, pp] if ple else []) + [g, b]
    specs = a_specs + w_specs + ([pl.BlockSpec((None, 1, N), lambda i: (layer, 0, 0))] if nb else []) + [row] + ([row, row] if ple else []) + [vec, vec]
    return pl.pallas_call(body, grid=(M // tm,), in_specs=specs,
                          out_specs=(row, row, row, pl.BlockSpec((tm, 1), lambda i: (i, 0))),
                          out_shape=(jax.ShapeDtypeStruct((M, N), F32), jax.ShapeDtypeStruct((M, N), BF16),
                                     jax.ShapeDtypeStruct((M, N), F32), jax.ShapeDtypeStruct((M, 1), F32)),
                          compiler_params=_params(("parallel",)), name=name)(*ins)


def mm_pw1_glu(name, xb, w4, layer, bias4, tm=512):
    M, K = xb.shape
    n = w4.shape[-1]
    half = N_CHIPS // 2
    tm = _tile(M, tm)

    def body(*refs):
        x_ref, w_refs, b_refs = refs[0], refs[1:1 + N_CHIPS], refs[1 + N_CHIPS:1 + 2 * N_CHIPS]
        h_ref, u_ref = refs[-2:]
        xv = x_ref[...].astype(BF16)
        for j in range(N_CHIPS):
            h_ref[j] = jnp.dot(xv, w_refs[j][...].astype(BF16), preferred_element_type=F32) + b_refs[j][...]
        for j in range(half):
            u_ref[:, j * n:(j + 1) * n] = h_ref[j] * _sigmoid(h_ref[j + half])

    specs = [pl.BlockSpec((tm, K), lambda i: (i, 0))]
    specs += [pl.BlockSpec((None, None, K, n), lambda i, j=j: (j, layer, 0, 0)) for j in range(N_CHIPS)]
    specs += [pl.BlockSpec((None, None, 1, n), lambda i, j=j: (j, layer, 0, 0)) for j in range(N_CHIPS)]
    return pl.pallas_call(body, grid=(M // tm,), in_specs=specs,
                          out_specs=(pl.BlockSpec((N_CHIPS, tm, n), lambda i: (0, i, 0)), pl.BlockSpec((tm, half * n), lambda i: (i, 0))),
                          out_shape=(jax.ShapeDtypeStruct((N_CHIPS, M, n), F32), jax.ShapeDtypeStruct((M, half * n), F32)),
                          compiler_params=_params(("parallel",)), name=name)(xb, *([w4] * N_CHIPS), *([bias4] * N_CHIPS))


def mm_nt_rowsm(name, dy, w4, layer, out_sm=False, res=None, res_alpha=1.0, out_dtype=F32, tm=1024):
    kc, N = w4.shape[-2:]
    M = dy.shape[0]
    tm = _tile(M, tm)

    def store_cols(o_ref, j, d):
        o_ref[:, j * kc:(j + 1) * kc] = d

    if out_sm:
        out_spec, sds, store = pl.BlockSpec((N_CHIPS, tm, kc), lambda i: (0, i, 0)), jax.ShapeDtypeStruct((N_CHIPS, M, kc), out_dtype), _store_slot
    else:
        out_spec, sds, store = pl.BlockSpec((tm, N_CHIPS * kc), lambda i: (i, 0)), jax.ShapeDtypeStruct((M, N_CHIPS * kc), F32), store_cols
    r = None if res is None else (res, pl.BlockSpec((tm, N_CHIPS * kc), lambda i: (i, 0)), lambda ref, j: ref[:, j * kc:(j + 1) * kc])
    return _mm_fanout(name, dy, pl.BlockSpec((tm, N), lambda i: (i, 0)), w4, (kc, N), layer, NT, out_spec, sds, store, (M // tm,),
                      res=r, res_alpha=res_alpha)


def mm_nt_colsm(name, dy_parts, w4, layer, res=None, res_alpha=1.0, tm=512):
    K, n = w4.shape[-2:]
    M = dy_parts[0][0].shape[1]
    tm = _tile(M, tm)
    terms = [(arr, pl.BlockSpec((None, tm, n), lambda i, idx=idx: (idx, i, 0)), w4,
              pl.BlockSpec((None, None, K, n), lambda i, j=j: (j, layer, 0, 0))) for j, (arr, idx) in enumerate(dy_parts)]
    r = None if res is None else (res, pl.BlockSpec((tm, K), lambda i: (i, 0)))
    return _mm_call(name, (M // tm,), terms, pl.BlockSpec((tm, K), lambda i: (i, 0)), jax.ShapeDtypeStruct((M, K), F32), NT,
                    res=r, res_alpha=res_alpha)


def _sm_parts(a):
    return [(a, j) for j in range(a.shape[0])]


def mm_tn_colsm(name, x, dy, into, off, j0=0, tk=512):
    M, K = x.shape
    nj, _, n = dy.shape
    tk = _tile(K, tk, LANES)
    assert off % tk == 0
    terms = [(x, pl.BlockSpec((M, tk), lambda j, k: (0, k)), dy, pl.BlockSpec((None, M, n), lambda j, k: (j, 0, 0)))]
    return _mm_call(name, (nj, K // tk), terms, pl.BlockSpec((None, tk, n), lambda j, k: (j + j0, off // tk + k, 0)),
                    None, TN, into=into)


def mm_tn_rowsm(name, a, dy, into, off, a_sm=False, tn=512):
    M, N = dy.shape
    kc = a.shape[-1] if a_sm else a.shape[-1] // N_CHIPS
    tn = _tile(N, tn, LANES)
    assert off % kc == 0
    a_spec = pl.BlockSpec((None, M, kc), lambda j, n: (j, 0, 0)) if a_sm else pl.BlockSpec((M, kc), lambda j, n: (0, j))
    terms = [(a, a_spec, dy, pl.BlockSpec((M, tn), lambda j, n: (0, n)))]
    return _mm_call(name, (N_CHIPS, N // tn), terms, pl.BlockSpec((None, kc, tn), lambda j, n: (j, off // kc, n)),
                    None, TN, into=into)


def mm_tn_rowsm_fan(name, a, dy, into, off, tn=256):
    M, N = dy.shape
    kc = a.shape[-1] // N_CHIPS
    tn = _tile(N, tn, LANES)
    assert off % kc == 0

    def body(a_ref, dy_ref, into_ref, o_ref):
        dyb = dy_ref[...].astype(BF16)
        for j in range(N_CHIPS):
            aj = a_ref[:, j * kc:(j + 1) * kc].astype(BF16)
            o_ref[j] = lax.dot_general(aj, dyb, TN, preferred_element_type=F32).astype(o_ref.dtype)

    return pl.pallas_call(body, out_shape=jax.ShapeDtypeStruct(into.shape, into.dtype), grid=(N // tn,),
                          in_specs=[pl.BlockSpec((M, N_CHIPS * kc), lambda n: (0, 0)), pl.BlockSpec((M, tn), lambda n: (0, n)),
                                    pl.BlockSpec(memory_space=pl.ANY)],
                          out_specs=pl.BlockSpec((N_CHIPS, kc, tn), lambda n: (0, off // kc, n)), input_output_aliases={2: 0},
                          compiler_params=_params(("parallel",)), name=name)(a, dy, into)


def mm_proj(name, p4, layer, w4, tm=512):
    S, P = p4.shape[-2:]
    n = w4.shape[-1]
    tm = _tile(S, tm)

    def store_cols(o_ref, j, d):
        o_ref[:, j * n:(j + 1) * n] = d

    return _mm_fanout(name, p4, pl.BlockSpec((None, None, tm, P), lambda i: (layer, 0, i, 0)), w4, (P, n), layer, NN,
                      pl.BlockSpec((tm, N_CHIPS * n), lambda i: (i, 0)), jax.ShapeDtypeStruct((S, N_CHIPS * n), F32),
                      store_cols, (S // tm,))


def mm_tn_proj(name, p4, layer, dpp, into, off):
    S, P = p4.shape[-2:]
    n = dpp.shape[-1] // N_CHIPS
    assert off % P == 0
    terms = [(p4, pl.BlockSpec((None, None, S, P), lambda j: (layer, 0, 0, 0)), dpp, pl.BlockSpec((S, n), lambda j: (0, j)))]
    return _mm_call(name, (N_CHIPS,), terms, pl.BlockSpec((None, P, n), lambda j: (j, off // P, 0)), None, TN, into=into)


def ln_fwd(name, x, mix, g, b, layer, tg=None, pp=None, tm=512):
    S, D = x.shape
    tm = _tile(S, tm, 16)
    ple = tg is not None

    def body(*refs):
        if ple:
            x_ref, m_ref, tg_ref, pp_ref, g_ref, b_ref, y_ref, yb_ref, xh_ref, rs_ref = refs
        else:
            x_ref, m_ref, g_ref, b_ref, y_ref, yb_ref, xh_ref, rs_ref = refs
        r = DN_ALPHA * x_ref[...] + m_ref[...]
        if ple:
            r = r + _sigmoid(tg_ref[...]) * pp_ref[...]
        mu = jnp.mean(r, axis=-1, keepdims=True)
        d = r - mu
        var = jnp.mean(d * d, axis=-1, keepdims=True)
        rstd = lax.rsqrt(var + LN_EPS)
        xh = d * rstd
        y = xh * g_ref[...] + b_ref[...]
        y_ref[...] = y
        yb_ref[...] = y.astype(BF16)
        xh_ref[...] = xh
        rs_ref[...] = rstd

    row = pl.BlockSpec((tm, D), lambda i: (i, 0))
    vec = pl.BlockSpec((None, 1, D), lambda i: (layer, 0, 0))
    ins = [x, mix] + ([tg, pp] if ple else []) + [g, b]
    specs = [row, row] + ([row, row] if ple else []) + [vec, vec]
    return pl.pallas_call(body, grid=(S // tm,), in_specs=specs,
                          out_specs=(row, row, row, pl.BlockSpec((tm, 1), lambda i: (i, 0))),
                          out_shape=(jax.ShapeDtypeStruct((S, D), F32), jax.ShapeDtypeStruct((S, D), BF16),
                                     jax.ShapeDtypeStruct((S, D), F32), jax.ShapeDtypeStruct((S, 1), F32)),
                          compiler_params=_params(("parallel",)), name=name)(*ins)


def ln_bwd(name, dy, xh, rstd, g, layer, tg=None, pp=None, tm=512):
    S, D = dy.shape
    tm = _tile(S, tm)
    ple = tg is not None

    def body(*refs):
        if ple:
            dy_ref, xh_ref, rs_ref, g_ref, tg_ref, pp_ref, dr_ref, acc_ref, dtg_ref, dpp_ref = refs
        else:
            dy_ref, xh_ref, rs_ref, g_ref, dr_ref, acc_ref = refs
        dy_, xh_ = dy_ref[...], xh_ref[...]
        dxh = dy_ * g_ref[...]
        m1 = jnp.mean(dxh, axis=-1, keepdims=True)
        m2 = jnp.mean(dxh * xh_, axis=-1, keepdims=True)
        dr = rs_ref[...] * (dxh - m1 - xh_ * m2)
        dr_ref[...] = dr

        @pl.when(pl.program_id(0) == 0)
        def _():
            acc_ref[...] = jnp.zeros_like(acc_ref)
        acc_ref[0:1, :] += jnp.sum(dy_ * xh_, axis=0, keepdims=True)
        acc_ref[1:2, :] += jnp.sum(dy_, axis=0, keepdims=True)
        acc_ref[2:3, :] += jnp.sum(dr, axis=0, keepdims=True)
        if ple:
            pg = _sigmoid(tg_ref[...])
            dtg_ref[...] = (dr * pp_ref[...] * pg * (1.0 - pg)).astype(BF16)
            dpp_ref[...] = (dr * pg).astype(BF16)

    row = pl.BlockSpec((tm, D), lambda i: (i, 0))
    ins = [dy, xh, rstd, g] + ([tg, pp] if ple else [])
    specs = [row, row, pl.BlockSpec((tm, 1), lambda i: (i, 0)), pl.BlockSpec((None, 1, D), lambda i: (layer, 0, 0))] + ([row, row] if ple else [])
    outs = [jax.ShapeDtypeStruct((S, D), F32), jax.ShapeDtypeStruct((8, D), F32)]
    out_specs = [row, pl.BlockSpec((8, D), lambda i: (0, 0))]
    if ple:
        outs += [jax.ShapeDtypeStruct((S, D), BF16)] * 2
        out_specs += [row, row]
    return pl.pallas_call(body, grid=(S // tm,), in_specs=specs, out_specs=tuple(out_specs), out_shape=tuple(outs),
                          compiler_params=_params(("arbitrary",)), name=name)(*ins)


def glu_fwd(name, h_sm, tm=512):
    _, S, n = h_sm.shape
    tm = _tile(S, tm)
    half = N_CHIPS // 2

    def body(a_ref, g_ref, u_ref):
        u_ref[...] = a_ref[...] * _sigmoid(g_ref[...])

    return pl.pallas_call(body, grid=(half, S // tm),
                          in_specs=[pl.BlockSpec((None, tm, n), lambda j, i: (j, i, 0)),
                                    pl.BlockSpec((None, tm, n), lambda j, i: (j + half, i, 0))],
                          out_specs=pl.BlockSpec((tm, n), lambda j, i: (i, j)),
                          out_shape=jax.ShapeDtypeStruct((S, half * n), F32),
                          compiler_params=_params(("parallel", "parallel")), name=name)(h_sm, h_sm)


def conv_ln_silu_fwd(name, u, w, b, g, beta, layer, ts=128):
    S, D = u.shape
    kw = w.shape[1]
    ts = _tile(S, ts, CONV_HALO)
    lc = LANES if D % LANES == 0 else D

    def body(h_ref, u_ref, w_ref, b_ref, g_ref, be_ref, c_ref, s_ref, win_ref):
        i = pl.program_id(0)
        win_ref[0:CONV_HALO, :] = jnp.where(i == 0, 0.0, h_ref[...])
        win_ref[CONV_HALO:, :] = u_ref[...]
        for cc in range(D // lc):
            cs = slice(cc * lc, (cc + 1) * lc)
            acc = jnp.zeros((ts, lc), F32) + b_ref[:, cs]
            for k in range(kw):
                off = CONV_HALO - (kw - 1) + k
                acc = acc + w_ref[k:k + 1, cs] * win_ref[off:off + ts, cs]
            c_ref[:, cs] = acc
        c = c_ref[...]
        mu = jnp.mean(c, axis=-1, keepdims=True)
        d = c - mu
        var = jnp.mean(d * d, axis=-1, keepdims=True)
        nrm = d * lax.rsqrt(var + LN_EPS) * g_ref[...] + be_ref[...]
        s_ref[...] = (nrm * _sigmoid(nrm)).astype(BF16)

    row = pl.BlockSpec((ts, D), lambda i: (i, 0))
    vec = pl.BlockSpec((None, 1, D), lambda i: (layer, 0, 0))
    halo = pl.BlockSpec((CONV_HALO, D), lambda i: (jnp.maximum(i * (ts // CONV_HALO) - 1, 0), 0))
    return pl.pallas_call(body, grid=(S // ts,),
                          in_specs=[halo, row, pl.BlockSpec((None, kw, D), lambda i: (layer, 0, 0)), vec, vec, vec],
                          out_specs=(row, row),
                          out_shape=(jax.ShapeDtypeStruct((S, D), F32), jax.ShapeDtypeStruct((S, D), BF16)),
                          scratch_shapes=[pltpu.VMEM((ts + CONV_HALO, D), F32)],
                          compiler_params=_params(("parallel",)), name=name)(u, u, w, b, g, beta)


def ln_silu_bwd(name, ds, c, g, beta, layer, tm=256):
    S, D = c.shape
    tm = _tile(S, tm)

    def body(ds_ref, c_ref, g_ref, be_ref, dc_ref, acc_ref):
        c_ = c_ref[...]
        mu = jnp.mean(c_, axis=-1, keepdims=True)
        d = c_ - mu
        var = jnp.mean(d * d, axis=-1, keepdims=True)
        rstd = lax.rsqrt(var + LN_EPS)
        xh = d * rstd
        nrm = xh * g_ref[...] + be_ref[...]
        sg = _sigmoid(nrm)
        dn = ds_ref[...] * (sg * (1.0 + nrm * (1.0 - sg)))
        dxh = dn * g_ref[...]
        m1 = jnp.mean(dxh, axis=-1, keepdims=True)
        m2 = jnp.mean(dxh * xh, axis=-1, keepdims=True)
        dc = rstd * (dxh - m1 - xh * m2)
        dc_ref[...] = dc

        @pl.when(pl.program_id(0) == 0)
        def _():
            acc_ref[...] = jnp.zeros_like(acc_ref)
        acc_ref[0:1, :] += jnp.sum(dn * xh, axis=0, keepdims=True)
        acc_ref[1:2, :] += jnp.sum(dn, axis=0, keepdims=True)
        acc_ref[2:3, :] += jnp.sum(dc, axis=0, keepdims=True)

    row = pl.BlockSpec((tm, D), lambda i: (i, 0))
    vec = pl.BlockSpec((None, 1, D), lambda i: (layer, 0, 0))
    return pl.pallas_call(body, grid=(S // tm,), in_specs=[row, row, vec, vec],
                          out_specs=(row, pl.BlockSpec((8, D), lambda i: (0, 0))),
                          out_shape=(jax.ShapeDtypeStruct((S, D), F32), jax.ShapeDtypeStruct((8, D), F32)),
                          compiler_params=_params(("arbitrary",)), name=name)(ds, c, g, beta)


def conv_glu_bwd(name, dc, u, h_sm, w, layer, ts=128):
    S, D = dc.shape
    kw = w.shape[1]
    half = N_CHIPS // 2
    n = D // half
    ts = _tile(S, ts, CONV_HALO)
    nblk = S // ts
    lc = LANES if n % LANES == 0 else n

    def body(dc_ref, dcn_ref, u_ref, a_ref, g_ref, w_ref, da_ref, dg_ref, dw_ref, dba_ref, dbg_ref, dwin_ref):
        i = pl.program_id(1)
        dwin_ref[0:ts, :] = dc_ref[...]
        dwin_ref[ts:, :] = jnp.where(i == nblk - 1, 0.0, dcn_ref[...])

        @pl.when(i == 0)
        def _():
            dw_ref[...] = jnp.zeros_like(dw_ref)
            dba_ref[...] = jnp.zeros_like(dba_ref)
            dbg_ref[...] = jnp.zeros_like(dbg_ref)

        for cc in range(n // lc):
            cs = slice(cc * lc, (cc + 1) * lc)
            ub = u_ref[:, cs]
            du = jnp.zeros((ts, lc), F32)
            for k in range(kw):
                shifted = dwin_ref[kw - 1 - k:kw - 1 - k + ts, cs]
                du = du + w_ref[k:k + 1, cs] * shifted
                dw_ref[k:k + 1, cs] += jnp.sum(shifted * ub, axis=0, keepdims=True)
            a = a_ref[:, cs]
            sg = _sigmoid(g_ref[:, cs])
            da = du * sg
            dg = du * a * sg * (1.0 - sg)
            da_ref[:, cs] = da.astype(BF16)
            dg_ref[:, cs] = dg.astype(BF16)
            dba_ref[0:1, cs] += jnp.sum(da, axis=0, keepdims=True)
            dbg_ref[0:1, cs] += jnp.sum(dg, axis=0, keepdims=True)

    r = ts // CONV_HALO
    main = pl.BlockSpec((ts, n), lambda j, i: (i, j))
    nxt = pl.BlockSpec((CONV_HALO, n), lambda j, i: (jnp.minimum((i + 1) * r, S // CONV_HALO - 1), j))
    sm_a = pl.BlockSpec((None, ts, n), lambda j, i: (j, i, 0))
    sm_g = pl.BlockSpec((None, ts, n), lambda j, i: (j + half, i, 0))
    da, dg, dw, dba, dbg = pl.pallas_call(
        body, grid=(half, nblk),
        in_specs=[main, nxt, main, sm_a, sm_g, pl.BlockSpec((None, kw, n), lambda j, i: (layer, 0, j))],
        out_specs=(pl.BlockSpec((None, ts, n), lambda j, i: (j, i, 0)), pl.BlockSpec((None, ts, n), lambda j, i: (j, i, 0)),
                   pl.BlockSpec((None, 32, n), lambda j, i: (j, 0, 0)),
                   pl.BlockSpec((None, 8, n), lambda j, i: (j, 0, 0)), pl.BlockSpec((None, 8, n), lambda j, i: (j, 0, 0))),
        out_shape=(jax.ShapeDtypeStruct((half, S, n), BF16), jax.ShapeDtypeStruct((half, S, n), BF16),
                   jax.ShapeDtypeStruct((half, 32, n), F32),
                   jax.ShapeDtypeStruct((half, 8, n), F32), jax.ShapeDtypeStruct((half, 8, n), F32)),
        scratch_shapes=[pltpu.VMEM((ts + CONV_HALO, n), F32)],
        compiler_params=_params(("parallel", "arbitrary")), name=name)(dc, dc, u, h_sm, h_sm, w)
    return da, dg, dw, dba, dbg


ROW_CHUNK = 16


def _ffn_gc(win_ref, w_ref, b_ref, r0, rows, kw, base):
    gc = b_ref[...] + jnp.zeros((rows, win_ref.shape[1]), F32)
    for k in range(kw):
        off = r0 + base - (kw - 1) + k
        gc = gc + w_ref[k:k + 1, :] * win_ref[off:off + rows, :]
    return gc


def ffn_gate_fwd(name, up_sm, gp_sm, w4, b4, layer, ts=256):
    _, S, n = up_sm.shape
    kw = w4.shape[2]
    ts = _tile(S, ts, ROW_CHUNK)
    rc = ROW_CHUNK

    def body(up_ref, gp_ref, gph_ref, w_ref, b_ref, hf_ref, gc_ref, win_ref):
        i = pl.program_id(1)
        win_ref[0:FFN_HALO, :] = jnp.where(i == 0, 0.0, gph_ref[...].astype(F32))
        win_ref[FFN_HALO:, :] = gp_ref[...].astype(F32)
        for r0 in range(0, ts, rc):
            gc = _ffn_gc(win_ref, w_ref, b_ref, r0, rc, kw, FFN_HALO)
            gc_ref[r0:r0 + rc, :] = gc.astype(BF16)
            hf_ref[r0:r0 + rc, :] = (gc * _sigmoid(gc) * up_ref[r0:r0 + rc, :].astype(F32)).astype(BF16)

    main = pl.BlockSpec((None, ts, n), lambda j, i: (j, i, 0))
    prv = pl.BlockSpec((None, FFN_HALO, n), lambda j, i: (j, jnp.maximum(i * (ts // FFN_HALO) - 1, 0), 0))
    sds = jax.ShapeDtypeStruct((N_CHIPS, S, n), BF16)
    return pl.pallas_call(body, grid=(N_CHIPS, S // ts),
                          in_specs=[main, main, prv, pl.BlockSpec((None, None, kw, n), lambda j, i: (j, layer, 0, 0)),
                                    pl.BlockSpec((None, None, 1, n), lambda j, i: (layer, j, 0, 0))],
                          out_specs=(main, main), out_shape=(sds, sds),
                          scratch_shapes=[pltpu.VMEM((ts + FFN_HALO, n), F32)],
                          compiler_params=_params(("parallel", "parallel")), name=name)(up_sm, gp_sm, gp_sm, w4, b4)


def ffn_gate_bwd_a(name, dhf_sm, up_sm, gc_sm, ts=256):
    _, S, n = up_sm.shape
    ts = _tile(S, ts, ROW_CHUNK)
    rc = ROW_CHUNK

    def body(dhf_ref, up_ref, gc_ref, dup_ref, dgc_ref):
        for r0 in range(0, ts, rc):
            rows = slice(r0, r0 + rc)
            gc = gc_ref[rows, :].astype(F32)
            sg = _sigmoid(gc)
            dhf = dhf_ref[rows, :].astype(F32)
            dup_ref[rows, :] = (dhf * gc * sg).astype(BF16)
            dgc_ref[rows, :] = (dhf * up_ref[rows, :].astype(F32) * (sg * (1.0 + gc * (1.0 - sg)))).astype(BF16)

    main = pl.BlockSpec((None, ts, n), lambda j, i: (j, i, 0))
    sds = jax.ShapeDtypeStruct((N_CHIPS, S, n), BF16)
    return pl.pallas_call(body, grid=(N_CHIPS, S // ts), in_specs=[main, main, main], out_specs=(main, main), out_shape=(sds, sds),
                          compiler_params=_params(("parallel", "parallel")), name=name)(dhf_sm, up_sm, gc_sm)


def ffn_up_gate_fwd(name, xb, w_up, w_gate, cw4, cb4, layer, tm=256):
    S, K = xb.shape
    n = w_up.shape[-1]
    kw = cw4.shape[2]
    tm = _tile(S, tm, ROW_CHUNK)
    rc = ROW_CHUNK

    def body(*refs):
        x_ref, xh_ref = refs[0:2]
        wu, wg, cw, cb = refs[2:6], refs[6:10], refs[10:14], refs[14:18]
        up_ref, gp_ref, gc_ref, hf_ref, win_ref = refs[18:]
        i = pl.program_id(0)
        xv = x_ref[...]
        xhalo = xh_ref[...]
        for j in range(N_CHIPS):
            wgj = wg[j][...]
            gp = jnp.dot(xv, wgj, preferred_element_type=F32)
            up = jnp.dot(xv, wu[j][...], preferred_element_type=F32)
            gph = jnp.dot(xhalo, wgj, preferred_element_type=F32)
            gpb = gp.astype(BF16)
            gp_ref[j] = gpb
            up_ref[j] = up.astype(BF16)
            win_ref[0:FFN_HALO, :] = jnp.where(i == 0, 0.0, gph.astype(BF16).astype(F32))
            win_ref[FFN_HALO:, :] = gpb.astype(F32)
            for r0 in range(0, tm, rc):
                gc = _ffn_gc(win_ref, cw[j], cb[j], r0, rc, kw, FFN_HALO)
                gc_ref[j, r0:r0 + rc, :] = gc.astype(BF16)
                hf_ref[j, r0:r0 + rc, :] = (gc * _sigmoid(gc) * up_ref[j, r0:r0 + rc, :].astype(F32)).astype(BF16)

    r = tm // FFN_HALO
    out = pl.BlockSpec((N_CHIPS, tm, n), lambda i: (0, i, 0))
    sds = jax.ShapeDtypeStruct((N_CHIPS, S, n), BF16)
    specs = [pl.BlockSpec((tm, K), lambda i: (i, 0)), pl.BlockSpec((FFN_HALO, K), lambda i: (jnp.maximum(i * r - 1, 0), 0))]
    specs += [pl.BlockSpec((None, None, K, n), lambda i, j=j: (j, layer, 0, 0)) for j in range(N_CHIPS)] * 2
    specs += [pl.BlockSpec((None, None, kw, n), lambda i, j=j: (j, layer, 0, 0)) for j in range(N_CHIPS)]
    specs += [pl.BlockSpec((None, None, 1, n), lambda i, j=j: (layer, j, 0, 0)) for j in range(N_CHIPS)]
    return pl.pallas_call(body, grid=(S // tm,), in_specs=specs, out_specs=(out, out, out, out), out_shape=(sds, sds, sds, sds),
                          scratch_shapes=[pltpu.VMEM((tm + FFN_HALO, n), F32)],
                          compiler_params=_params(("parallel",)), name=name)(
                              xb, xb, *([w_up] * N_CHIPS), *([w_gate] * N_CHIPS), *([cw4] * N_CHIPS), *([cb4] * N_CHIPS))


def ffn_gate_conv_bwd(name, dr, w_down, cw4, layer, up_sm, gc_sm, gp_sm, tm=256):
    n, N = w_down.shape[-2:]
    S = dr.shape[0]
    kw = cw4.shape[2]
    tm = _tile(S, tm, ROW_CHUNK)
    rc = ROW_CHUNK
    nblk = S // tm

    def dgc_of(dhf, up, gc):
        sg = _sigmoid(gc)
        return dhf * up * (sg * (1.0 + gc * (1.0 - sg))), sg

    def body(*refs):
        dr_ref, drn_ref = refs[0:2]
        wd, cw = refs[2:6], refs[6:10]
        up_ref, gc_ref, gp_ref, upn_ref, gcn_ref, dup_ref, dgp_ref, acc_ref, win_ref = refs[10:]
        i = pl.program_id(0)

        @pl.when(i == 0)
        def _():
            acc_ref[...] = jnp.zeros_like(acc_ref)
        a = dr_ref[...].astype(BF16)
        an = drn_ref[...].astype(BF16)
        for j in range(N_CHIPS):
            wj = wd[j][...]
            dhf = lax.dot_general(a, wj, NT, preferred_element_type=F32)
            gc = gc_ref[j].astype(F32)
            dgc, sg = dgc_of(dhf, up_ref[j].astype(F32), gc)
            dup_ref[j] = (dhf * gc * sg).astype(BF16)
            win_ref[0:tm, :] = dgc
            dhfn = lax.dot_general(an, wj, NT, preferred_element_type=F32)
            dgcn, _ = dgc_of(dhfn, upn_ref[j].astype(F32), gcn_ref[j].astype(F32))
            win_ref[tm:, :] = jnp.where(i == nblk - 1, 0.0, dgcn)
            sums = [jnp.zeros((8, n), F32) for _ in range(kw + 1)]
            for r0 in range(0, tm, rc):
                gp = gp_ref[j, r0:r0 + rc, :].astype(F32)
                dgp = jnp.zeros((rc, n), F32)
                for k in range(kw):
                    d = kw - 1 - k
                    shifted = win_ref[r0 + d:r0 + d + rc, :]
                    dgp = dgp + cw[j][k:k + 1, :] * shifted
                    prod = shifted * gp
                    sums[k] = sums[k] + prod[0:8, :] + prod[8:16, :]
                    if d == 0:
                        sums[kw] = sums[kw] + shifted[0:8, :] + shifted[8:16, :]
                dgp_ref[j, r0:r0 + rc, :] = dgp.astype(BF16)
            for k in range(kw):
                acc_ref[j, k:k + 1, :] += jnp.sum(sums[k], axis=0, keepdims=True)
            acc_ref[j, 7:8, :] += jnp.sum(sums[kw], axis=0, keepdims=True)

    r = tm // FFN_HALO
    nxt_row = lambda i: jnp.minimum((i + 1) * r, S // FFN_HALO - 1)
    blk = pl.BlockSpec((N_CHIPS, tm, n), lambda i: (0, i, 0))
    halo = pl.BlockSpec((N_CHIPS, FFN_HALO, n), lambda i: (0, nxt_row(i), 0))
    sds = jax.ShapeDtypeStruct((N_CHIPS, S, n), BF16)
    specs = [pl.BlockSpec((tm, N), lambda i: (i, 0)), pl.BlockSpec((FFN_HALO, N), lambda i: (nxt_row(i), 0))]
    specs += [pl.BlockSpec((None, None, n, N), lambda i, j=j: (j, layer, 0, 0)) for j in range(N_CHIPS)]
    specs += [pl.BlockSpec((None, None, kw, n), lambda i, j=j: (j, layer, 0, 0)) for j in range(N_CHIPS)]
    specs += [blk, blk, blk, halo, halo]
    return pl.pallas_call(body, grid=(nblk,), in_specs=specs,
                          out_specs=(blk, blk, pl.BlockSpec((N_CHIPS, 8, n), lambda i: (0, 0, 0))),
                          out_shape=(sds, sds, jax.ShapeDtypeStruct((N_CHIPS, 8, n), F32)),
                          scratch_shapes=[pltpu.VMEM((tm + FFN_HALO, n), F32)],
                          compiler_params=_params(("arbitrary",)), name=name)(
                              dr, dr, *([w_down] * N_CHIPS), *([cw4] * N_CHIPS), up_sm, gc_sm, gp_sm, up_sm, gc_sm)


def ffn_gate_bwd_b(name, dgc_sm, gp_sm, w4, layer, ts=256):
    _, S, n = gp_sm.shape
    kw = w4.shape[2]
    ts = _tile(S, ts, ROW_CHUNK)
    rc = ROW_CHUNK
    nblk = S // ts

    def body(dgc_ref, dgn_ref, gp_ref, w_ref, dgp_ref, acc_ref, dwin_ref):
        i = pl.program_id(1)
        dwin_ref[0:ts, :] = dgc_ref[...].astype(F32)
        dwin_ref[ts:, :] = jnp.where(i == nblk - 1, 0.0, dgn_ref[...].astype(F32))

        @pl.when(i == 0)
        def _():
            acc_ref[...] = jnp.zeros_like(acc_ref)
        sums = [jnp.zeros((8, n), F32) for _ in range(kw + 1)]
        for r0 in range(0, ts, rc):
            gp = gp_ref[r0:r0 + rc, :].astype(F32)
            dgp = jnp.zeros((rc, n), F32)
            for k in range(kw):
                d = kw - 1 - k
                shifted = dwin_ref[r0 + d:r0 + d + rc, :]
                dgp = dgp + w_ref[k:k + 1, :] * shifted
                prod = shifted * gp
                sums[k] = sums[k] + prod[0:8, :] + prod[8:16, :]
                if d == 0:
                    sums[kw] = sums[kw] + shifted[0:8, :] + shifted[8:16, :]
            dgp_ref[r0:r0 + rc, :] = dgp.astype(BF16)
        for k in range(kw):
            acc_ref[k:k + 1, :] += jnp.sum(sums[k], axis=0, keepdims=True)
        acc_ref[7:8, :] += jnp.sum(sums[kw], axis=0, keepdims=True)

    r = ts // FFN_HALO
    main = pl.BlockSpec((None, ts, n), lambda j, i: (j, i, 0))
    nxt = pl.BlockSpec((None, FFN_HALO, n), lambda j, i: (j, jnp.minimum((i + 1) * r, S // FFN_HALO - 1), 0))
    return pl.pallas_call(body, grid=(N_CHIPS, nblk),
                          in_specs=[main, nxt, main, pl.BlockSpec((None, None, kw, n), lambda j, i: (j, layer, 0, 0))],
                          out_specs=(main, pl.BlockSpec((None, 8, n), lambda j, i: (j, 0, 0))),
                          out_shape=(jax.ShapeDtypeStruct((N_CHIPS, S, n), BF16), jax.ShapeDtypeStruct((N_CHIPS, 8, n), F32)),
                          scratch_shapes=[pltpu.VMEM((ts + FFN_HALO, n), F32)],
                          compiler_params=_params(("parallel", "arbitrary")), name=name)(dgc_sm, dgc_sm, gp_sm, w4)


def _neg_softplus(z):
    e = jnp.exp(-jnp.abs(z))
    return -(jnp.maximum(z, 0.0) + jnp.log(1.0 + e)), e


def _split_dot(x, t):
    hi = x.astype(BF16)
    lo = (x - hi.astype(F32)).astype(BF16)
    return jnp.dot(hi, t, preferred_element_type=F32) + jnp.dot(lo, t, preferred_element_type=F32)


STICK_GONE = -100.0
NOT_SWEPT = -1e30


def attn_fwd(name, q, k, v, bq=512, w=256):
    S, D = q.shape
    dh = HEAD_DIM
    hpb = LANES // dh
    bq = _tile(S, bq)
    w = _tile(bq, w)
    nsub = bq // w
    nkb = S // w

    def body(q_ref, k_ref, v_ref, o_ref, runs_ref, rs_ref):
        qi = pl.program_id(1)
        rr = lax.broadcasted_iota(jnp.int32, (w, w), 0)
        cc = lax.broadcasted_iota(jnp.int32, (w, w), 1)
        t_suf = (rr >= cc).astype(BF16)
        tq = qi * bq + lax.broadcasted_iota(jnp.int32, (bq, w), 0)
        tk = lax.broadcasted_iota(jnp.int32, (bq, w), 1)
        lane = lax.broadcasted_iota(jnp.int32, (bq, LANES), 1)
        ntot = (qi + 1) * nsub
        heads = [slice(hh * dh, (hh + 1) * dh) for hh in range(hpb)]
        qbs = [q_ref[:, hs] for hs in heads]
        for hh in range(hpb):
            rs_ref[hh] = jnp.where(lane < ntot, NOT_SWEPT, 0.0)

        def block(kb, carry, masked):
            kstart = pl.multiple_of(kb * w, w)
            if masked:
                m = (tk + kstart) < tq
            out = []
            for hh, hs in enumerate(heads):
                run, acc = carry[2 * hh], carry[2 * hh + 1]
                kblk = k_ref[pl.ds(kstart, w), hs]
                vblk = v_ref[pl.ds(kstart, w), hs]
                z = lax.dot_general(qbs[hh], kblk, NT, preferred_element_type=F32)
                lg, _ = _neg_softplus(z)
                if masked:
                    lg = jnp.where(m, lg, 0.0)
                cum = _split_dot(lg, t_suf) + run
                a = jnp.exp(z + cum)
                if masked:
                    a = jnp.where(m, a, 0.0)
                acc = acc + jnp.dot(a.astype(BF16), vblk, preferred_element_type=F32)
                run = cum[:, 0:1]
                rs_ref[hh] = jnp.where(lane == kb, run, rs_ref[hh])
                out += [run, acc]
            return tuple(out)

        carry = (jnp.zeros((bq, 1), F32), jnp.zeros((bq, dh), F32)) * hpb
        for sb in reversed(range(nsub)):
            carry = block(qi * nsub + sb, carry, True)

        def cond(c):
            alive = functools.reduce(jnp.maximum, [jnp.max(c[1 + 2 * hh]) for hh in range(hpb)])
            return jnp.logical_and(c[0] >= 0, alive > STICK_GONE)

        def step(c):
            return (c[0] - 1,) + block(c[0], c[1:], False)
        carry = lax.while_loop(cond, step, (qi * nsub - 1,) + carry)[1:]
        for hh, hs in enumerate(heads):
            o_ref[:, hs] = carry[2 * hh + 1].astype(o_ref.dtype)
            runs_ref[hh] = rs_ref[hh, :, 0:nkb]

    qs = pl.BlockSpec((bq, LANES), lambda h, i: (i, h))
    kv = pl.BlockSpec((S, LANES), lambda h, i: (0, h))
    return pl.pallas_call(body, grid=(D // LANES, S // bq), in_specs=[qs, kv, kv],
                          out_specs=(qs, pl.BlockSpec((hpb, bq, nkb), lambda h, i: (h, i, 0))),
                          out_shape=(jax.ShapeDtypeStruct((S, D), BF16), jax.ShapeDtypeStruct((D // dh, S, nkb), F32)),
                          scratch_shapes=[pltpu.VMEM((hpb, bq, LANES), F32)],
                          compiler_params=_params(("parallel", "parallel")), name=name)(q, k, v)


def attn_bwd(name, q, k, v, do, runs, dk0=None, dv0=None, bq=512, w=256):
    S, D = q.shape
    dh = HEAD_DIM
    hpb = LANES // dh
    bq = _tile(S, bq)
    w = _tile(bq, w)
    nsub = bq // w
    nkb = S // w
    scale = 1.0 / math.sqrt(dh)
    init = dk0 is not None

    def body(*refs):
        if init:
            q_ref, k_ref, v_ref, do_ref, runs_ref, dk0_ref, dv0_ref, dq_ref, dk_ref, dv_ref, rs_ref = refs
        else:
            q_ref, k_ref, v_ref, do_ref, runs_ref, dq_ref, dk_ref, dv_ref, rs_ref = refs
        qi = pl.program_id(1)

        @pl.when(qi == 0)
        def _():
            dk_ref[...] = dk0_ref[...] if init else jnp.zeros_like(dk_ref)
            dv_ref[...] = dv0_ref[...] if init else jnp.zeros_like(dv_ref)

        rr = lax.broadcasted_iota(jnp.int32, (w, w), 0)
        cc = lax.broadcasted_iota(jnp.int32, (w, w), 1)
        t_suf = (rr >= cc).astype(BF16)
        t_pre = (rr <= cc).astype(BF16)
        tq = qi * bq + lax.broadcasted_iota(jnp.int32, (bq, w), 0)
        tk = lax.broadcasted_iota(jnp.int32, (bq, w), 1)
        lane = lax.broadcasted_iota(jnp.int32, (bq, LANES), 1)
        lane1 = lax.broadcasted_iota(jnp.int32, (1, LANES), 1)
        ntot = (qi + 1) * nsub
        heads = [slice(hh * dh, (hh + 1) * dh) for hh in range(hpb)]
        qbs = [q_ref[:, hs] for hs in heads]
        dobs = [do_ref[:, hs].astype(BF16) for hs in heads]
        kb0 = ntot - nsub
        for hh in range(hpb):
            rs_ref[hh] = jnp.zeros((bq, LANES), F32)
            rs_ref[hh, :, 0:nkb] = runs_ref[hh]
            colmax = jnp.max(rs_ref[hh], axis=0, keepdims=True)
            dead = jnp.logical_and(jnp.logical_and(lane1 >= 1, lane1 <= ntot), colmax <= STICK_GONE)
            kb0 = jnp.minimum(kb0, jnp.sum(dead.astype(jnp.int32)))

        def block(kb, carry, masked):
            kstart = pl.multiple_of(kb * w, w)
            if masked:
                m = (tk + kstart) < tq
            out = []
            for hh, hs in enumerate(heads):
                pg_run, dq = carry[2 * hh], carry[2 * hh + 1]
                qb, dob = qbs[hh], dobs[hh]
                kblk = k_ref[pl.ds(kstart, w), hs]
                vblk = v_ref[pl.ds(kstart, w), hs]
                right = jnp.sum(jnp.where(lane == kb + 1, rs_ref[hh], 0.0), axis=1, keepdims=True)
                z = lax.dot_general(qb, kblk, NT, preferred_element_type=F32)
                lg, e = _neg_softplus(z)
                if masked:
                    lg = jnp.where(m, lg, 0.0)
                a = jnp.exp(z + _split_dot(lg, t_suf) + right)
                if masked:
                    a = jnp.where(m, a, 0.0)
                da = lax.dot_general(dob, vblk, NT, preferred_element_type=F32)
                g = da * a
                pin = _split_dot(g, t_pre) + pg_run
                sig = jnp.where(z >= 0.0, 1.0, e) / (1.0 + e)
                dz = g - sig * pin
                if masked:
                    dz = jnp.where(m, dz, 0.0)
                dzb = dz.astype(BF16)
                dq = dq + jnp.dot(dzb, kblk, preferred_element_type=F32)
                dk_ref[pl.ds(kstart, w), hs] += lax.dot_general(dzb, qb, TN, preferred_element_type=F32)
                dv_ref[pl.ds(kstart, w), hs] += lax.dot_general(a.astype(BF16), dob, TN, preferred_element_type=F32)
                out += [pin[:, w - 1:w], dq]
            return tuple(out)

        carry = (jnp.zeros((bq, 1), F32), jnp.zeros((bq, dh), F32)) * hpb
        carry = lax.fori_loop(kb0, qi * nsub, lambda kb, c: block(kb, c, False), carry)
        for sb in range(nsub):
            carry = block(qi * nsub + sb, carry, True)
        for hh, hs in enumerate(heads):
            dq_ref[:, hs] = carry[2 * hh + 1] * scale

    qs = pl.BlockSpec((bq, LANES), lambda h, i: (i, h))
    kv = pl.BlockSpec((S, LANES), lambda h, i: (0, h))
    ins = [q, k, v, do, runs] + ([dk0, dv0] if init else [])
    specs = [qs, kv, kv, qs, pl.BlockSpec((hpb, bq, nkb), lambda h, i: (h, i, 0))] + ([kv, kv] if init else [])
    sds = jax.ShapeDtypeStruct((S, D), F32)
    return pl.pallas_call(body, grid=(D // LANES, S // bq), in_specs=specs, out_specs=(qs, kv, kv), out_shape=(sds, sds, sds),
                          scratch_shapes=[pltpu.VMEM((hpb, bq, LANES), F32)],
                          compiler_params=_params(("parallel", "arbitrary")), name=name)(*ins)


def loss_head(name, y, tgt, tm=512):
    S, D = y.shape
    tm = _tile(S, tm)

    def body(y_ref, t_ref, dy_ref, acc_ref):
        @pl.when(pl.program_id(0) == 0)
        def _():
            acc_ref[...] = jnp.zeros_like(acc_ref)
        e = y_ref[...] - t_ref[...]
        dy_ref[...] = e * (1.0 / D)
        acc_ref[...] += jnp.sum(e * e)

    row = pl.BlockSpec((tm, D), lambda i: (i, 0))
    return pl.pallas_call(body, grid=(S // tm,), in_specs=[row, row],
                          out_specs=(row, pl.BlockSpec((8, LANES), lambda i: (0, 0))),
                          out_shape=(jax.ShapeDtypeStruct((S, D), F32), jax.ShapeDtypeStruct((8, LANES), F32)),
                          compiler_params=_params(("arbitrary",)), name=name)(y, tgt)


def _to_sm(a, axis=-1):
    axis = axis % a.ndim
    shp = a.shape[:axis] + (N_CHIPS, a.shape[axis] // N_CHIPS) + a.shape[axis + 1:]
    return jnp.moveaxis(a.reshape(shp), axis, 0)


def _from_sm(a, axis=-1):
    nd = a.ndim - 1
    axis = axis % nd
    b = jnp.moveaxis(a, 0, axis)
    return b.reshape(b.shape[:axis] + (b.shape[axis] * b.shape[axis + 1],) + b.shape[axis + 2:])


class GradBuffers:
    def __init__(self, W):
        groups = {}
        for n in BIG:
            _, layers, rows, cols = W[n].shape
            groups.setdefault(cols, []).append((rows, n, layers))
        self.where, self.cols_of, self.buf, self.members = {}, {}, {}, {}
        for cols, items in groups.items():
            off, members = 0, []
            for rows, n, layers in sorted(items, key=lambda t: -t[0]):
                assert off % rows == 0
                self.where[n], self.cols_of[n] = (off, rows), cols
                members.append((n, off, rows * layers))
                off += rows * layers
            assert off % 32 == 0
            self.buf[cols] = lax.empty((N_CHIPS, off, cols), BF16)
            self.members[cols] = members

    def put(self, n, layer, fn, **kw):
        cols = self.cols_of[n]
        off, rows = self.where[n]
        self.buf[cols] = fn(into=self.buf[cols], off=off + layer * rows, **kw)


def forward_backward(x, p4, tgt, W):
    S, D = x.shape
    scale = 1.0 / math.sqrt(HEAD_DIM)
    saved = []
    kh = vh = xb_kv = None
    xb = x.astype(BF16)
    for i in range(DEPTH):
        sv = {'xb': xb}
        if i < N_A:
            h_sm, u = mm_pw1_glu(f"pw1glu_{i}", xb, W['a_pw1_w'], i, W['a_pw1_b'])
            c, s = conv_ln_silu_fwd(f"convln_{i}", u, W['a_dw_w'], W['a_dw_b'], W['a_ln_g'], W['a_ln_b'], i)
            x1, x1b, xh1, rs1 = mm_rowsm_ln(f"pw2ln_{i}", s, W['a_pw2_w'], i, x, W['ln_mix_g'], W['ln_mix_b'], i, bias=W['a_pw2_b'])
            sv.update(h_sm=h_sm, u=u, c=c, s=s)
        else:
            j = i - N_A
            if kh is None:
                xb_kv = xb
                kh = mm_rowsm("wk", xb, W['kv_wk'], 0, out_dtype=BF16)
                vh = mm_rowsm("wv", xb, W['kv_wv'], 0, out_dtype=BF16)
            qh = mm_rowsm(f"wq_{j}", xb, W['b_wq'], j, out_dtype=BF16, out_scale=scale)
            o, runs = attn_fwd(f"attn_{j}", qh, kh, vh)
            x1, x1b, xh1, rs1 = mm_rowsm_ln(f"woln_{j}", o, W['b_wo'], j, x, W['ln_mix_g'], W['ln_mix_b'], i)
            sv.update(qh=qh, o=o, runs=runs)
        up_sm, gp_sm, gc_sm, hf_sm = ffn_up_gate_fwd(f"upgate_{i}", x1b, W['ffn_w_up'], W['ffn_w_gate'], W['ffn_conv_w'],
                                                     W['ffn_conv_b'], i)
        tg = mm_rowsm(f"plegate_{i}", x1b, W['ple_w_gate'], i)
        pp = mm_proj(f"pleproj_{i}", p4, i, W['ple_w_proj'])
        x2, x2b, xh2, rs2 = mm_rowsm_ln(f"downln_{i}", hf_sm, W['ffn_w_down'], i, x1, W['ln_ffn_g'], W['ln_ffn_b'], i, a_sm=True,
                                        tg=tg, pp=pp)
        sv.update(x1b=x1b, xh1=xh1, rs1=rs1, up_sm=up_sm, gp_sm=gp_sm, gc_sm=gc_sm, hf_sm=hf_sm, tg=tg, pp=pp, xh2=xh2, rs2=rs2)
        saved.append(sv)
        x, xb = x2, x2b

    dx, lacc = loss_head("loss", x, tgt)
    loss_sum = lacc[0, 0]

    G = {n: [None] * DEPTH for n in WEIGHTS if n not in BIG}
    gb = GradBuffers(W)
    dk = dv = None
    for i in reversed(range(DEPTH)):
        sv = saved[i]
        dr, acc, dtg, dpp = ln_bwd(f"lnffn_b_{i}", dx, sv['xh2'], sv['rs2'], W['ln_ffn_g'], i, tg=sv['tg'], pp=sv['pp'])
        G['ln_ffn_g'][i], G['ln_ffn_b'][i] = acc[0], acc[1]
        gb.put('ple_w_proj', i, functools.partial(mm_tn_proj, f"dproj_{i}", p4, i, dpp))
        gb.put('ple_w_gate', i, functools.partial(mm_tn_rowsm_fan, f"dplegate_{i}", sv['x1b'], dtg))
        gb.put('ffn_w_down', i, functools.partial(mm_tn_rowsm, f"ddown_{i}", sv['hf_sm'], dr, a_sm=True))
        dup_sm, dgp_sm, cacc = ffn_gate_conv_bwd(f"gateconv_b_{i}", dr, W['ffn_w_down'], W['ffn_conv_w'], i, sv['up_sm'],
                                                 sv['gc_sm'], sv['gp_sm'])
        kw = W['ffn_conv_w'].shape[2]
        G['ffn_conv_w'][i] = cacc[:, 0:kw, :]
        G['ffn_conv_b'][i] = cacc[:, 7, :].reshape(-1)
        gb.put('ffn_w_up', i, functools.partial(mm_tn_colsm, f"dup_{i}", sv['x1b'], dup_sm))
        gb.put('ffn_w_gate', i, functools.partial(mm_tn_colsm, f"dgate_{i}", sv['x1b'], dgp_sm))
        dx1 = mm_nt_rowsm(f"dx1a_{i}", dtg, W['ple_w_gate'], i, res=dr, res_alpha=DN_ALPHA)
        dx1 = mm_nt_colsm(f"dx1b_{i}", _sm_parts(dup_sm), W['ffn_w_up'], i, res=dx1)
        dx1 = mm_nt_colsm(f"dx1c_{i}", _sm_parts(dgp_sm), W['ffn_w_gate'], i, res=dx1)

        dr1, acc1 = ln_bwd(f"lnmix_b_{i}", dx1, sv['xh1'], sv['rs1'], W['ln_mix_g'], i)
        G['ln_mix_g'][i], G['ln_mix_b'][i] = acc1[0], acc1[1]
        xin = sv['xb']
        if i < N_A:
            G['a_pw2_b'][i] = acc1[2]
            gb.put('a_pw2_w', i, functools.partial(mm_tn_rowsm_fan, f"dpw2_{i}", sv['s'], dr1))
            ds = mm_nt_rowsm(f"ds_{i}", dr1, W['a_pw2_w'], i)
            dc, cacc = ln_silu_bwd(f"lnsilu_b_{i}", ds, sv['c'], W['a_ln_g'], W['a_ln_b'], i)
            G['a_ln_g'][i], G['a_ln_b'][i], G['a_dw_b'][i] = cacc[0], cacc[1], cacc[2]
            da, dg, dw, dba, dbg = conv_glu_bwd(f"convglu_b_{i}", dc, sv['u'], sv['h_sm'], W['a_dw_w'], i)
            kw = W['a_dw_w'].shape[1]
            G['a_dw_w'][i] = _from_sm(dw[:, 0:kw, :], axis=-1)
            G['a_pw1_b'][i] = jnp.concatenate([dba[:, 0, :], dbg[:, 0, :]], axis=0)
            half = da.shape[0]
            gb.put('a_pw1_w', i, functools.partial(mm_tn_colsm, f"dpw1a_{i}", xin, da))
            gb.put('a_pw1_w', i, functools.partial(mm_tn_colsm, f"dpw1g_{i}", xin, dg), j0=half)
            dx = mm_nt_colsm(f"dxa_{i}", _sm_parts(da) + _sm_parts(dg), W['a_pw1_w'], i, res=dr1, res_alpha=DN_ALPHA)
        else:
            j = i - N_A
            gb.put('b_wo', j, functools.partial(mm_tn_rowsm_fan, f"dwo_{j}", sv['o'], dr1))
            do = mm_nt_rowsm(f"do_{j}", dr1, W['b_wo'], j)
            dq, dk, dv = attn_bwd(f"attn_b_{j}", sv['qh'], kh, vh, do, sv['runs'], dk, dv)
            gb.put('b_wq', j, functools.partial(mm_tn_rowsm_fan, f"dwq_{j}", xin, dq))
            dx = mm_nt_rowsm(f"dxq_{j}", dq, W['b_wq'], j, res=dr1, res_alpha=DN_ALPHA)
            if j == 0:
                gb.put('kv_wk', 0, functools.partial(mm_tn_rowsm_fan, "dwk", xb_kv, dk))
                gb.put('kv_wv', 0, functools.partial(mm_tn_rowsm_fan, "dwv", xb_kv, dv))
                dx = mm_nt_rowsm("dxk", dk, W['kv_wk'], 0, res=dx)
                dx = mm_nt_rowsm("dxv", dv, W['kv_wv'], 0, res=dx)
    return loss_sum, dx, G, gb


MESH = pl.DeviceIdType.MESH
HBM = pl.BlockSpec(memory_space=pltpu.HBM)


def _place():
    x, y, c = lax.axis_index("x"), lax.axis_index("y"), lax.axis_index("c")
    others = [(1 - x, y), (x, 1 - y), (1 - x, 1 - y)]
    return x, y, c, others


def allgather_chips(name, arrs):
    n = len(arrs)

    def body(*refs):
        ins, outs = refs[:n], refs[n:2 * n]
        send_sems, recv_sems = refs[2 * n:]
        x, y, c, others = _place()
        me = 2 * x + y
        sibling = (x, y, 1 - c)
        ids = [2 * ch[0] + ch[1] for ch in others]
        from_id = jnp.where(c == 0, ids[0], ids[1])
        to_chip = (jnp.where(c == 0, x, 1 - x), jnp.where(c == 0, 1 - y, y))

        def remote(a, k, src, chip_id, half, to):
            return pltpu.make_async_remote_copy(src_ref=src, dst_ref=outs[a].at[chip_id, half], send_sem=send_sems.at[a, k],
                                                recv_sem=recv_sems.at[a, k], device_id=to, device_id_type=MESH)

        sent = [remote(a, k, ins[a].at[c], me, c, (others[k][0], others[k][1], c)) for a in range(n) for k in range(2)]
        for cp in sent:
            cp.start()
        for a in range(n):
            for k in range(2):
                remote(a, k, ins[a].at[c], ids[k], c, sibling).wait_recv()
            sent.append(remote(a, 2, outs[a].at[from_id, c], from_id, c, (to_chip[0], to_chip[1], c)))
            sent[-1].start()
            for k in range(2):
                sent.append(remote(a, 3 + k, outs[a].at[ids[k], c], ids[k], c, sibling))
                sent[-1].start()
        for a in range(n):
            remote(a, 2, ins[a].at[c], ids[2], c, sibling).wait_recv()
            sent.append(remote(a, 5, outs[a].at[ids[2], c], ids[2], c, sibling))
            sent[-1].start()
        for a in range(n):
            for k in range(3):
                remote(a, 3 + k, ins[a].at[c], ids[k], 1 - c, sibling).wait_recv()
        for cp in sent:
            cp.wait_send()

    outs = pl.pallas_call(body, out_shape=tuple(jax.ShapeDtypeStruct((N_CHIPS,) + a.shape, a.dtype) for a in arrs),
                          in_specs=[HBM] * n, out_specs=tuple([HBM] * n),
                          scratch_shapes=[pltpu.SemaphoreType.DMA((n, 6)), pltpu.SemaphoreType.DMA((n, 6))],
                          name=name)(*arrs)
    me = 2 * lax.axis_index("x") + lax.axis_index("y")
    return [lax.dynamic_update_index_in_dim(o, a, me, 0) for o, a in zip(outs, arrs)]


def exchange_sibling(name, gs):
    n = len(gs)

    def body(*refs):
        g_refs, o_refs = refs[:n], refs[n:2 * n]
        send_sems, recv_sems = refs[2 * n:]
        x, y, c, _ = _place()
        cps = [pltpu.make_async_remote_copy(src_ref=g_refs[a].at[j, 1 - c], dst_ref=o_refs[a].at[j], send_sem=send_sems.at[a, j],
                                            recv_sem=recv_sems.at[a, j], device_id=(x, y, 1 - c), device_id_type=MESH)
               for a in range(n) for j in range(N_CHIPS)]
        for cp in cps:
            cp.start()
        for cp in cps:
            cp.wait()

    return pl.pallas_call(body, out_shape=tuple(jax.ShapeDtypeStruct((N_CHIPS,) + g.shape[2:], g.dtype) for g in gs),
                          in_specs=[HBM] * n, out_specs=tuple([HBM] * n),
                          scratch_shapes=[pltpu.SemaphoreType.DMA((n, N_CHIPS)), pltpu.SemaphoreType.DMA((n, N_CHIPS))],
                          name=name)(*gs)


def _ring_peers():
    x, y, c, _ = _place()
    first = (jnp.where(c == 0, 1 - x, x), jnp.where(c == 0, y, 1 - y))
    second = (jnp.where(c == 0, x, 1 - x), jnp.where(c == 0, 1 - y, y))
    return c, first, second, 2 * (1 - x) + (1 - y)


def exchange_first(name, ss):
    n = len(ss)

    def body(*refs):
        s_refs, o_refs = refs[:n], refs[n:2 * n]
        send_sems, recv_sems = refs[2 * n:]
        c, first, _, diag = _ring_peers()
        cps = [pltpu.make_async_remote_copy(src_ref=s_refs[a].at[slot], dst_ref=o_refs[a].at[k], send_sem=send_sems.at[a, k],
                                            recv_sem=recv_sems.at[a, k], device_id=(first[0], first[1], c), device_id_type=MESH)
               for a in range(n) for k, slot in enumerate((2 * first[0] + first[1], diag))]
        for cp in cps:
            cp.start()
        for cp in cps:
            cp.wait()

    return pl.pallas_call(body, out_shape=tuple(jax.ShapeDtypeStruct((2,) + s.shape[1:], s.dtype) for s in ss),
                          in_specs=[HBM] * n, out_specs=tuple([HBM] * n),
                          scratch_shapes=[pltpu.SemaphoreType.DMA((n, 2)), pltpu.SemaphoreType.DMA((n, 2))], name=name)(*ss)


def exchange_second(name, ts):
    n = len(ts)

    def body(*refs):
        t_refs, o_refs = refs[:n], refs[n:2 * n]
        send_sems, recv_sems = refs[2 * n:]
        c, _, second, _ = _ring_peers()
        cps = [pltpu.make_async_remote_copy(src_ref=t_refs[a], dst_ref=o_refs[a], send_sem=send_sems.at[a], recv_sem=recv_sems.at[a],
                                            device_id=(second[0], second[1], c), device_id_type=MESH) for a in range(n)]
        for cp in cps:
            cp.start()
        for cp in cps:
            cp.wait()

    return pl.pallas_call(body, out_shape=tuple(jax.ShapeDtypeStruct(t.shape, t.dtype) for t in ts),
                          in_specs=[HBM] * n, out_specs=tuple([HBM] * n),
                          scratch_shapes=[pltpu.SemaphoreType.DMA((n,)), pltpu.SemaphoreType.DMA((n,))], name=name)(*ts)


def share_sibling(name, ts):
    n = len(ts)

    def body(*refs):
        o_refs = refs[n:2 * n]
        send_sems, recv_sems = refs[2 * n:]
        x, y, c, _ = _place()
        cps = [pltpu.make_async_remote_copy(src_ref=o_refs[a].at[c], dst_ref=o_refs[a].at[c], send_sem=send_sems.at[a],
                                            recv_sem=recv_sems.at[a], device_id=(x, y, 1 - c), device_id_type=MESH)
               for a in range(n)]
        for cp in cps:
            cp.start()
        for a in range(n):
            pltpu.make_async_remote_copy(src_ref=o_refs[a].at[c], dst_ref=o_refs[a].at[1 - c], send_sem=send_sems.at[a],
                                         recv_sem=recv_sems.at[a], device_id=(x, y, 1 - c), device_id_type=MESH).wait_recv()
        for cp in cps:
            cp.wait_send()

    return pl.pallas_call(body, out_shape=tuple(jax.ShapeDtypeStruct(t.shape, t.dtype) for t in ts),
                          in_specs=[HBM] * n, out_specs=tuple([HBM] * n), input_output_aliases={a: a for a in range(n)},
                          scratch_shapes=[pltpu.SemaphoreType.DMA((n,)), pltpu.SemaphoreType.DMA((n,))],
                          name=name)(*ts)


def add_halves(name, g, recv, place, out_dtype, tr=512):
    _, _, R, C = g.shape
    tr = _tile(R, tr, 16)

    def body(p_ref, a_ref, b_ref, o_ref):
        o_ref[...] = (a_ref[...].astype(F32) + b_ref[...].astype(F32)).astype(o_ref.dtype)

    blk = pl.BlockSpec((None, tr, C), lambda j, i, p: (j, i, 0))
    gs = pltpu.PrefetchScalarGridSpec(num_scalar_prefetch=1, grid=(N_CHIPS, R // tr),
                                      in_specs=[pl.BlockSpec((None, None, tr, C), lambda j, i, p: (j, p[0], i, 0)), blk],
                                      out_specs=blk)
    return pl.pallas_call(body, grid_spec=gs, out_shape=jax.ShapeDtypeStruct((N_CHIPS, R, C), out_dtype),
                          compiler_params=_params(("parallel", "parallel")), name=name)(place, g, recv)


def add_pass_on(name, s, got, place, tr=512):
    _, R, C = s.shape
    tr = _tile(R, tr, 16)

    def body(p_ref, a_ref, b_ref, o_ref):
        o_ref[...] = (a_ref[...].astype(F32) + b_ref[...].astype(F32)).astype(o_ref.dtype)

    gs = pltpu.PrefetchScalarGridSpec(num_scalar_prefetch=1, grid=(R // tr,),
                                      in_specs=[pl.BlockSpec((None, tr, C), lambda i, p: (p[2], i, 0)),
                                                pl.BlockSpec((None, tr, C), lambda i, p: (1, i, 0))],
                                      out_specs=pl.BlockSpec((tr, C), lambda i, p: (i, 0)))
    return pl.pallas_call(body, grid_spec=gs, out_shape=jax.ShapeDtypeStruct((R, C), s.dtype),
                          compiler_params=_params(("parallel",)), name=name)(place, s, got)


def add_chips(name, g, r1, got1, got2, place, tr=512):
    _, _, R, C = g.shape
    tr = _tile(R, tr, 16)

    def body(p_ref, a_ref, b_ref, c_ref, d_ref, o_ref):
        o_ref[...] = ((a_ref[...].astype(F32) + b_ref[...].astype(F32)) + c_ref[...].astype(F32)) + d_ref[...].astype(F32)

    gs = pltpu.PrefetchScalarGridSpec(num_scalar_prefetch=1, grid=(R // tr,),
                                      in_specs=[pl.BlockSpec((None, None, tr, C), lambda i, p: (p[1], p[0], i, 0)),
                                                pl.BlockSpec((None, tr, C), lambda i, p: (p[1], i, 0)),
                                                pl.BlockSpec((None, tr, C), lambda i, p: (0, i, 0)),
                                                pl.BlockSpec((tr, C), lambda i, p: (i, 0))],
                                      out_specs=pl.BlockSpec((None, tr, C), lambda i, p: (p[0], i, 0)))
    return pl.pallas_call(body, grid_spec=gs, out_shape=jax.ShapeDtypeStruct((2, R, C), F32),
                          compiler_params=_params(("parallel",)), name=name)(place, g, r1, got1, got2)


def reduce_scatter(gs, wire_dtypes, place):
    r1 = exchange_sibling("rs_sibling", gs)
    s1 = [add_halves(f"rs_add_cores_{a}", g, r, place, dt) for a, (g, r, dt) in enumerate(zip(gs, r1, wire_dtypes))]
    got1 = exchange_first("rs_first", s1)
    t = [add_pass_on(f"rs_add_pass_{a}", s, g1, place) for a, (s, g1) in enumerate(zip(s1, got1))]
    got2 = exchange_second("rs_second", t)
    tot = [add_chips(f"rs_add_chips_{a}", g, r, g1, g2, place) for a, (g, r, g1, g2) in enumerate(zip(gs, r1, got1, got2))]
    return share_sibling("rs_share", tot)


def adamw(name, w, g, m, v, tr=512):
    shp = w.shape
    cols = shp[-1]
    w2, g2, m2, v2 = (a.reshape(-1, cols) for a in (w, g, m, v))
    rows = w2.shape[0]
    tr = _tile(rows, tr)

    def body(w_ref, g_ref, m_ref, v_ref, d_ref, mo_ref, vo_ref):
        g_ = g_ref[...]
        m_ = ADAM_B1 * m_ref[...] + (1.0 - ADAM_B1) * g_
        v_ = ADAM_B2 * v_ref[...] + (1.0 - ADAM_B2) * (g_ * g_)
        m_hat = m_ / (1.0 - ADAM_B1 ** ADAM_STEP)
        v_hat = v_ / (1.0 - ADAM_B2 ** ADAM_STEP)
        d_ref[...] = -ADAM_LR * (m_hat / (jnp.sqrt(v_hat) + ADAM_EPS) + ADAM_WD * w_ref[...])
        mo_ref[...] = m_
        vo_ref[...] = v_

    blk = pl.BlockSpec((tr, cols), lambda i: (i, 0))
    sds = jax.ShapeDtypeStruct((rows, cols), F32)
    d, mo, vo = pl.pallas_call(body, grid=(rows // tr,), in_specs=[blk] * 4, out_specs=(blk, blk, blk), out_shape=(sds, sds, sds),
                               compiler_params=_params(("parallel",)), name=name)(w2, g2, m2, v2)
    return d.reshape(shp), mo.reshape(shp), vo.reshape(shp)


PACK_ALIGN = 1024


def _pad_to(a, mult, axis=-1):
    axis = axis % a.ndim
    extra = (-a.shape[axis]) % mult
    if extra == 0:
        return a
    pads = [(0, 0)] * a.ndim
    pads[axis] = (0, extra)
    return jnp.pad(a, pads)


def _pack(pieces, lead, row_mult):
    nl = len(lead)
    flat, offs, sizes, off = [], [], [], 0
    for a in pieces:
        f = a.reshape(lead + (-1,))
        sizes.append(f.shape[-1])
        f = _pad_to(f, PACK_ALIGN)
        offs.append(off)
        off += f.shape[-1]
        flat.append(f)
    cat = _pad_to(jnp.concatenate(flat, axis=nl), 2 * row_mult * LANES)
    return cat.reshape(lead + (2, -1, LANES)), offs, sizes


def _unpack(packed, lead, offs, sizes, shapes):
    flat = packed.reshape(lead + (-1,))
    return [lax.slice_in_dim(flat, o, o + s, axis=len(lead)).reshape(lead + tuple(shp)) for o, s, shp in zip(offs, sizes, shapes)]


def _stack_grads(G, names):
    out = {}
    for n in names:
        parts = [g for g in G[n] if g is not None]
        if n in ('kv_wk', 'kv_wv'):
            out[n] = parts[0]
        elif n in REPLICATED:
            out[n] = jnp.stack(parts, axis=0).reshape(N_CHIPS, -1)
        elif n in ('a_dw_w', 'a_dw_b', 'a_ln_g', 'a_ln_b', 'a_pw2_b'):
            out[n] = _to_sm(jnp.stack(parts, axis=0), axis=-1)
        else:
            out[n] = jnp.stack(parts, axis=1)
    return out


def _whole_weights(big, small, rep, D):
    W = {}
    for n in BIG:
        a = big[n]
        W[n] = a[:, None] if n in ('kv_wk', 'kv_wv') else a
    W['a_pw1_b'] = small['a_pw1_b'][:, :, None, :]
    W['a_dw_w'] = _from_sm(small['a_dw_w'], axis=-1)
    for n in ('a_dw_b', 'a_ln_g', 'a_ln_b', 'a_pw2_b'):
        W[n] = _from_sm(small[n], axis=-1)[:, None, :]
    W['ffn_conv_w'] = small['ffn_conv_w']
    L, F = rep['ffn_conv_b'].shape
    W['ffn_conv_b'] = rep['ffn_conv_b'].reshape(L, N_CHIPS, 1, F // N_CHIPS)
    for n in ('ln_mix_g', 'ln_mix_b', 'ln_ffn_g', 'ln_ffn_b'):
        W[n] = rep[n][:, None, :]
    return W


SMALL = ('a_pw1_b', 'a_dw_w', 'a_dw_b', 'a_ln_g', 'a_ln_b', 'a_pw2_b', 'ffn_conv_w')


def _step(x, p, loss_target, w, m, v):
    S, D = x.shape[-2:]
    x2, tgt = x.reshape(S, D), loss_target.reshape(S, D)
    ax, ay, ac = lax.axis_index("x"), lax.axis_index("y"), lax.axis_index("c")
    second = jnp.where(ac == 0, 2 * ax + (1 - ay), 2 * (1 - ax) + ay)
    place = jnp.stack([ac, 2 * ax + ay, second]).astype(jnp.int32)

    big_in = [w[n].astype(BF16).reshape((2, -1) + w[n].shape[1:] if w[n].ndim == 3 else (2, -1, w[n].shape[-1])) for n in BIG]
    small_in, s_offs, s_sizes = _pack([w[n] for n in SMALL], (), 8)
    gathered = allgather_chips("gather_weights", big_in + [small_in])
    big = {n: g.reshape((N_CHIPS,) + w[n].shape) for n, g in zip(BIG, gathered[:-1])}
    small = dict(zip(SMALL, _unpack(gathered[-1], (N_CHIPS,), s_offs, s_sizes, [w[n].shape for n in SMALL])))
    W = _whole_weights(big, small, {n: w[n] for n in REPLICATED}, D)

    loss_sum, dx, G, gb = forward_backward(x2, p, tgt, W)
    loss = lax.psum(0.5 * loss_sum / D, ("x", "y", "c"))

    vectors = [n for n in WEIGHTS if n not in BIG]
    mats = [b.reshape(N_CHIPS, 2, b.shape[1] // 2, b.shape[2]) for b in gb.buf.values()]
    members = [gb.members[cols] for cols in gb.buf]
    g_sm = _stack_grads(G, vectors)
    packed, offs, sizes = _pack([g_sm[n] for n in vectors], (N_CHIPS,), 512)
    reduced = reduce_scatter(mats + [packed], [BF16] * len(mats) + [F32], place)
    shapes = [w[n].shape if n not in REPLICATED else (w[n].size // N_CHIPS,) for n in vectors]
    g_mine = dict(zip(vectors, _unpack(reduced[-1], (), offs, sizes, shapes)))
    for red, where in zip(reduced[:-1], members):
        rows = red.reshape(-1, red.shape[-1])
        for n, off, cnt in where:
            g_mine[n] = lax.slice_in_dim(rows, off, off + cnt, axis=0).reshape(w[n].shape)
    rep_in, r_offs, r_sizes = _pack([g_mine[n] for n in REPLICATED], (), 8)
    rep_all = allgather_chips("gather_replicated_grads", [rep_in])[0]
    for n, g in zip(REPLICATED, _unpack(rep_all, (N_CHIPS,), r_offs, r_sizes, [(w[n].size // N_CHIPS,) for n in REPLICATED])):
        g_mine[n] = g.reshape(w[n].shape)

    grads, deltas, new_m, new_v = [], [], [], []
    for n in WEIGHTS:
        d, mo, vo = adamw(f"adamw_{n}", w[n], g_mine[n], m[n], v[n])
        grads.append(g_mine[n])
        deltas.append(d)
        new_m.append(mo)
        new_v.append(vo)
    return (loss, dx.reshape(x.shape), *grads, *deltas, *new_m, *new_v)


def kernel(x, p, a_pw1_w, a_pw1_b, a_dw_w, a_dw_b, a_ln_g, a_ln_b, a_pw2_w, a_pw2_b, b_wq, kv_wk, kv_wv, b_wo, ln_mix_g, ln_mix_b, ffn_w_up, ffn_w_gate, ffn_conv_w, ffn_conv_b, ffn_w_down, ple_w_gate, ple_w_proj, ln_ffn_g, ln_ffn_b, loss_target, m_a_pw1_w, m_a_pw1_b, m_a_dw_w, m_a_dw_b, m_a_ln_g, m_a_ln_b, m_a_pw2_w, m_a_pw2_b, m_b_wq, m_kv_wk, m_kv_wv, m_b_wo, m_ln_mix_g, m_ln_mix_b, m_ffn_w_up, m_ffn_w_gate, m_ffn_conv_w, m_ffn_conv_b, m_ffn_w_down, m_ple_w_gate, m_ple_w_proj, m_ln_ffn_g, m_ln_ffn_b, v_a_pw1_w, v_a_pw1_b, v_a_dw_w, v_a_dw_b, v_a_ln_g, v_a_ln_b, v_a_pw2_w, v_a_pw2_b, v_b_wq, v_kv_wk, v_kv_wv, v_b_wo, v_ln_mix_g, v_ln_mix_b, v_ffn_w_up, v_ffn_w_gate, v_ffn_conv_w, v_ffn_conv_b, v_ffn_w_down, v_ple_w_gate, v_ple_w_proj, v_ln_ffn_g, v_ln_ffn_b):
    vals = dict(locals())
    w = {n: vals[n] for n in WEIGHTS}
    m = {n: vals["m_" + n] for n in WEIGHTS}
    v = {n: vals["v_" + n] for n in WEIGHTS}
    return _step(x, p, loss_target, w, m, v)
```

```python
import functools
import math

import jax
import jax.numpy as jnp
import numpy as np
from jax import lax
from jax.experimental import pallas as pl
from jax.experimental.pallas import tpu as pltpu

F32, BF16 = jnp.float32, jnp.bfloat16

HEAD_DIM = 64
LN_EPS = 1e-5
DEPTH = 4
N_A = DEPTH // 2
DN_ALPHA = (2.0 * DEPTH) ** 0.25
N_CHIPS = 4

ADAM_LR, ADAM_B1, ADAM_B2, ADAM_EPS, ADAM_WD, ADAM_STEP = 0.001, 0.9, 0.999, 1e-08, 0.01, 10

VMEM_LIMIT_BYTES = 56 * 2**20
LANES = 128
SUBLANES = 8
CONV_HALO = 32
FFN_HALO = 16

NN = (((1,), (0,)), ((), ()))
NT = (((1,), (1,)), ((), ()))
TN = (((0,), (0,)), ((), ()))

WEIGHTS = ['a_pw1_w', 'a_pw1_b', 'a_dw_w', 'a_dw_b', 'a_ln_g', 'a_ln_b', 'a_pw2_w', 'a_pw2_b', 'b_wq', 'kv_wk', 'kv_wv',
           'b_wo', 'ln_mix_g', 'ln_mix_b', 'ffn_w_up', 'ffn_w_gate', 'ffn_conv_w', 'ffn_conv_b', 'ffn_w_down', 'ple_w_gate',
           'ple_w_proj', 'ln_ffn_g', 'ln_ffn_b']
REPLICATED = ('ln_mix_g', 'ln_mix_b', 'ffn_conv_b', 'ln_ffn_g', 'ln_ffn_b')
BIG = ('a_pw1_w', 'a_pw2_w', 'b_wq', 'kv_wk', 'kv_wv', 'b_wo', 'ffn_w_up', 'ffn_w_gate', 'ffn_w_down', 'ple_w_gate',
       'ple_w_proj')


def _tile(n, pref, mult=8):
    t = min(n, pref)
    while t > 0:
        if n % t == 0 and t % mult == 0:
            return t
        t -= 1
    return n


def _params(sem):
    return pltpu.CompilerParams(dimension_semantics=sem, vmem_limit_bytes=VMEM_LIMIT_BYTES)


def _sigmoid(x):
    return 0.5 * jnp.tanh(0.5 * x) + 0.5


def _mm_call(name, grid, terms, out_spec, out_sds, dims, bias=None, res=None, res_alpha=1.0, out_scale=None, into=None):
    n_terms = len(terms)

    def body(*refs):
        o_ref = refs[-1]
        acc = None
        for t in range(n_terms):
            a = refs[2 * t][...].astype(BF16)
            b = refs[2 * t + 1][...].astype(BF16)
            d = lax.dot_general(a, b, dims, preferred_element_type=F32)
            acc = d if acc is None else acc + d
        k = 2 * n_terms
        if bias is not None:
            acc = acc + refs[k][...]
            k += 1
        if res is not None:
            acc = acc + res_alpha * refs[k][...]
        if out_scale is not None:
            acc = acc * out_scale
        o_ref[...] = acc.astype(o_ref.dtype)

    operands, specs = [], []
    for a, a_spec, b, b_spec in terms:
        operands += [a, b]
        specs += [a_spec, b_spec]
    for extra in (bias, res):
        if extra is not None:
            operands.append(extra[0])
            specs.append(extra[1])
    aliases = {}
    if into is not None:
        aliases = {len(operands): 0}
        operands.append(into)
        specs.append(pl.BlockSpec(memory_space=pl.ANY))
        out_sds = jax.ShapeDtypeStruct(into.shape, into.dtype)
    return pl.pallas_call(body, out_shape=out_sds, grid=grid, in_specs=specs, out_specs=out_spec, input_output_aliases=aliases,
                          compiler_params=_params(("parallel",) * len(grid)), name=name)(*operands)


def _mm_fanout(name, a, a_spec, w4, w_block, layer, dims, out_spec, out_sds, store, grid, bias4=None, res=None, res_alpha=1.0):
    def body(*refs):
        a_ref, w_refs, o_ref = refs[0], refs[1:1 + N_CHIPS], refs[-1]
        k = 1 + N_CHIPS
        b_refs = refs[k:k + N_CHIPS] if bias4 is not None else None
        k += N_CHIPS if bias4 is not None else 0
        av = a_ref[...].astype(BF16)
        for j in range(N_CHIPS):
            d = lax.dot_general(av, w_refs[j][...].astype(BF16), dims, preferred_element_type=F32)
            if b_refs is not None:
                d = d + b_refs[j][...]
            if res is not None:
                d = d + res_alpha * res[2](refs[k], j)
            store(o_ref, j, d)

    nd = len(grid)
    operands = [a] + [w4] * N_CHIPS
    specs = [a_spec] + [pl.BlockSpec((None, None) + w_block, lambda *g, j=j: (j, layer, 0, 0)) for j in range(N_CHIPS)]
    if bias4 is not None:
        operands += [bias4] * N_CHIPS
        specs += [pl.BlockSpec((None, None, 1, bias4.shape[-1]), lambda *g, j=j: (j, layer, 0, 0)) for j in range(N_CHIPS)]
    if res is not None:
        operands.append(res[0])
        specs.append(res[1])
    return pl.pallas_call(body, out_shape=out_sds, grid=grid, in_specs=specs, out_specs=out_spec,
                          compiler_params=_params(("parallel",) * nd), name=name)(*operands)


def _store_slot(o_ref, j, d):
    o_ref[j] = d.astype(o_ref.dtype)


def mm_colsm(name, x, w4, layer, bias4=None, out_dtype=F32, tm=1024):
    M, K = x.shape
    n = w4.shape[-1]
    tm = _tile(M, tm)
    return _mm_fanout(name, x, pl.BlockSpec((tm, K), lambda i: (i, 0)), w4, (K, n), layer, NN,
                      pl.BlockSpec((N_CHIPS, tm, n), lambda i: (0, i, 0)), jax.ShapeDtypeStruct((N_CHIPS, M, n), out_dtype),
                      _store_slot, (M // tm,), bias4=bias4)


def mm_rowsm(name, a, w4, layer, a_sm=False, bias=None, out_dtype=F32, out_scale=None, tm=512):
    kc, N = w4.shape[-2:]
    M = a.shape[-2]
    tm = _tile(M, tm)
    terms = []
    for j in range(N_CHIPS):
        if a_sm:
            a_spec = pl.BlockSpec((None, tm, kc), lambda i, j=j: (j, i, 0))
        else:
            a_spec = pl.BlockSpec((tm, kc), lambda i, j=j: (i, j))
        terms.append((a, a_spec, w4, pl.BlockSpec((None, None, kc, N), lambda i, j=j: (j, layer, 0, 0))))
    b = None if bias is None else (bias, pl.BlockSpec((None, 1, N), lambda i: (layer, 0, 0)))
    return _mm_call(name, (M // tm,), terms, pl.BlockSpec((tm, N), lambda i: (i, 0)), jax.ShapeDtypeStruct((M, N), out_dtype),
                    NN, bias=b, out_scale=out_scale)


def mm_rowsm_ln(name, a, w4, layer, x, g, b, ln_layer, a_sm=False, bias=None, tg=None, pp=None, tm=256):
    kc, N = w4.shape[-2:]
    M = a.shape[-2]
    tm = _tile(M, tm, 16)
    ple = tg is not None
    nb = bias is not None

    def body(*refs):
        a_refs, w_refs = refs[0:N_CHIPS], refs[N_CHIPS:2 * N_CHIPS]
        k = 2 * N_CHIPS
        acc = None
        for j in range(N_CHIPS):
            d = jnp.dot(a_refs[j][...].astype(BF16), w_refs[j][...].astype(BF16), preferred_element_type=F32)
            acc = d if acc is None else acc + d
        if nb:
            acc = acc + refs[k][...]
            k += 1
        r = DN_ALPHA * refs[k][...] + acc
        k += 1
        if ple:
            r = r + _sigmoid(refs[k][...]) * refs[k + 1][...]
            k += 2
        g_ref, b_ref, y_ref, yb_ref, xh_ref, rs_ref = refs[k:]
        mu = jnp.mean(r, axis=-1, keepdims=True)
        dd = r - mu
        var = jnp.mean(dd * dd, axis=-1, keepdims=True)
        rstd = lax.rsqrt(var + LN_EPS)
        xh = dd * rstd
        y = xh * g_ref[...] + b_ref[...]
        y_ref[...] = y
        yb_ref[...] = y.astype(BF16)
        xh_ref[...] = xh
        rs_ref[...] = rstd

    row = pl.BlockSpec((tm, N), lambda i: (i, 0))
    vec = pl.BlockSpec((None, 1, N), lambda i: (ln_layer, 0, 0))
    if a_sm:
        a_specs = [pl.BlockSpec((None, tm, kc), lambda i, j=j: (j, i, 0)) for j in range(N_CHIPS)]
    else:
        a_specs = [pl.BlockSpec((tm, kc), lambda i, j=j: (i, j)) for j in range(N_CHIPS)]
    w_specs = [pl.BlockSpec((None, None, kc, N), lambda i, j=j: (j, layer, 0, 0)) for j in range(N_CHIPS)]
    ins = [a] * N_CHIPS + [w4] * N_CHIPS + ([bias] if nb else []) + [x] + ([tg, pp] if ple else []) + [g, b]
    specs = a_specs + w_specs + ([pl.BlockSpec((None, 1, N), lambda i: (layer, 0, 0))] if nb else []) + [row] + ([row, row] if ple else []) + [vec, vec]
    return pl.pallas_call(body, grid=(M // tm,), in_specs=specs,
                          out_specs=(row, row, row, pl.BlockSpec((tm, 1), lambda i: (i, 0))),
                          out_shape=(jax.ShapeDtypeStruct((M, N), F32), jax.ShapeDtypeStruct((M, N), BF16),
                                     jax.ShapeDtypeStruct((M, N), F32), jax.ShapeDtypeStruct((M, 1), F32)),
                          compiler_params=_params(("parallel",)), name=name)(*ins)


def mm_pw1_glu(name, xb, w4, layer, bias4, tm=512):
    M, K = xb.shape
    n = w4.shape[-1]
    half = N_CHIPS // 2
    tm = _tile(M, tm)

    def body(*refs):
        x_ref, w_refs, b_refs = refs[0], refs[1:1 + N_CHIPS], refs[1 + N_CHIPS:1 + 2 * N_CHIPS]
        h_ref, u_ref = refs[-2:]
        xv = x_ref[...].astype(BF16)
        for j in range(N_CHIPS):
            h_ref[j] = jnp.dot(xv, w_refs[j][...].astype(BF16), preferred_element_type=F32) + b_refs[j][...]
        for j in range(half):
            u_ref[:, j * n:(j + 1) * n] = h_ref[j] * _sigmoid(h_ref[j + half])

    specs = [pl.BlockSpec((tm, K), lambda i: (i, 0))]
    specs += [pl.BlockSpec((None, None, K, n), lambda i, j=j: (j, layer, 0, 0)) for j in range(N_CHIPS)]
    specs += [pl.BlockSpec((None, None, 1, n), lambda i, j=j: (j, layer, 0, 0)) for j in range(N_CHIPS)]
    return pl.pallas_call(body, grid=(M // tm,), in_specs=specs,
                          out_specs=(pl.BlockSpec((N_CHIPS, tm, n), lambda i: (0, i, 0)), pl.BlockSpec((tm, half * n), lambda i: (i, 0))),
                          out_shape=(jax.ShapeDtypeStruct((N_CHIPS, M, n), F32), jax.ShapeDtypeStruct((M, half * n), F32)),
                          compiler_params=_params(("parallel",)), name=name)(xb, *([w4] * N_CHIPS), *([bias4] * N_CHIPS))


def mm_nt_rowsm(name, dy, w4, layer, out_sm=False, res=None, res_alpha=1.0, out_dtype=F32, tm=1024):
    kc, N = w4.shape[-2:]
    M = dy.shape[0]
    tm = _tile(M, tm)

    def store_cols(o_ref, j, d):
        o_ref[:, j * kc:(j + 1) * kc] = d

    if out_sm:
        out_spec, sds, store = pl.BlockSpec((N_CHIPS, tm, kc), lambda i: (0, i, 0)), jax.ShapeDtypeStruct((N_CHIPS, M, kc), out_dtype), _store_slot
    else:
        out_spec, sds, store = pl.BlockSpec((tm, N_CHIPS * kc), lambda i: (i, 0)), jax.ShapeDtypeStruct((M, N_CHIPS * kc), F32), store_cols
    r = None if res is None else (res, pl.BlockSpec((tm, N_CHIPS * kc), lambda i: (i, 0)), lambda ref, j: ref[:, j * kc:(j + 1) * kc])
    return _mm_fanout(name, dy, pl.BlockSpec((tm, N), lambda i: (i, 0)), w4, (kc, N), layer, NT, out_spec, sds, store, (M // tm,),
                      res=r, res_alpha=res_alpha)


def mm_nt_colsm(name, dy_parts, w4, layer, res=None, res_alpha=1.0, tm=512):
    K, n = w4.shape[-2:]
    M = dy_parts[0][0].shape[1]
    tm = _tile(M, tm)
    terms = [(arr, pl.BlockSpec((None, tm, n), lambda i, idx=idx: (idx, i, 0)), w4,
              pl.BlockSpec((None, None, K, n), lambda i, j=j: (j, layer, 0, 0))) for j, (arr, idx) in enumerate(dy_parts)]
    r = None if res is None else (res, pl.BlockSpec((tm, K), lambda i: (i, 0)))
    return _mm_call(name, (M // tm,), terms, pl.BlockSpec((tm, K), lambda i: (i, 0)), jax.ShapeDtypeStruct((M, K), F32), NT,
                    res=r, res_alpha=res_alpha)


def _sm_parts(a):
    return [(a, j) for j in range(a.shape[0])]


def mm_tn_colsm(name, x, dy, into, off, j0=0, tk=512):
    M, K = x.shape
    nj, _, n = dy.shape
    tk = _tile(K, tk, LANES)
    assert off % tk == 0
    terms = [(x, pl.BlockSpec((M, tk), lambda j, k: (0, k)), dy, pl.BlockSpec((None, M, n), lambda j, k: (j, 0, 0)))]
    return _mm_call(name, (nj, K // tk), terms, pl.BlockSpec((None, tk, n), lambda j, k: (j + j0, off // tk + k, 0)),
                    None, TN, into=into)


def mm_tn_rowsm(name, a, dy, into, off, a_sm=False, tn=512):
    M, N = dy.shape
    kc = a.shape[-1] if a_sm else a.shape[-1] // N_CHIPS
    tn = _tile(N, tn, LANES)
    assert off % kc == 0
    a_spec = pl.BlockSpec((None, M, kc), lambda j, n: (j, 0, 0)) if a_sm else pl.BlockSpec((M, kc), lambda j, n: (0, j))
    terms = [(a, a_spec, dy, pl.BlockSpec((M, tn), lambda j, n: (0, n)))]
    return _mm_call(name, (N_CHIPS, N // tn), terms, pl.BlockSpec((None, kc, tn), lambda j, n: (j, off // kc, n)),
                    None, TN, into=into)


def mm_tn_rowsm_fan(name, a, dy, into, off, tn=256):
    M, N = dy.shape
    kc = a.shape[-1] // N_CHIPS
    tn = _tile(N, tn, LANES)
    assert off % kc == 0

    def body(a_ref, dy_ref, into_ref, o_ref):
        dyb = dy_ref[...].astype(BF16)
        for j in range(N_CHIPS):
            aj = a_ref[:, j * kc:(j + 1) * kc].astype(BF16)
            o_ref[j] = lax.dot_general(aj, dyb, TN, preferred_element_type=F32).astype(o_ref.dtype)

    return pl.pallas_call(body, out_shape=jax.ShapeDtypeStruct(into.shape, into.dtype), grid=(N // tn,),
                          in_specs=[pl.BlockSpec((M, N_CHIPS * kc), lambda n: (0, 0)), pl.BlockSpec((M, tn), lambda n: (0, n)),
                                    pl.BlockSpec(memory_space=pl.ANY)],
                          out_specs=pl.BlockSpec((N_CHIPS, kc, tn), lambda n: (0, off // kc, n)), input_output_aliases={2: 0},
                          compiler_params=_params(("parallel",)), name=name)(a, dy, into)


def mm_proj(name, p4, layer, w4, tm=512):
    S, P = p4.shape[-2:]
    n = w4.shape[-1]
    tm = _tile(S, tm)

    def store_cols(o_ref, j, d):
        o_ref[:, j * n:(j + 1) * n] = d

    return _mm_fanout(name, p4, pl.BlockSpec((None, None, tm, P), lambda i: (layer, 0, i, 0)), w4, (P, n), layer, NN,
                      pl.BlockSpec((tm, N_CHIPS * n), lambda i: (i, 0)), jax.ShapeDtypeStruct((S, N_CHIPS * n), F32),
                      store_cols, (S // tm,))


def mm_tn_proj(name, p4, layer, dpp, into, off):
    S, P = p4.shape[-2:]
    n = dpp.shape[-1] // N_CHIPS
    assert off % P == 0
    terms = [(p4, pl.BlockSpec((None, None, S, P), lambda j: (layer, 0, 0, 0)), dpp, pl.BlockSpec((S, n), lambda j: (0, j)))]
    return _mm_call(name, (N_CHIPS,), terms, pl.BlockSpec((None, P, n), lambda j: (j, off // P, 0)), None, TN, into=into)


def ln_fwd(name, x, mix, g, b, layer, tg=None, pp=None, tm=512):
    S, D = x.shape
    tm = _tile(S, tm, 16)
    ple = tg is not None

    def body(*refs):
        if ple:
            x_ref, m_ref, tg_ref, pp_ref, g_ref, b_ref, y_ref, yb_ref, xh_ref, rs_ref = refs
        else:
            x_ref, m_ref, g_ref, b_ref, y_ref, yb_ref, xh_ref, rs_ref = refs
        r = DN_ALPHA * x_ref[...] + m_ref[...]
        if ple:
            r = r + _sigmoid(tg_ref[...]) * pp_ref[...]
        mu = jnp.mean(r, axis=-1, keepdims=True)
        d = r - mu
        var = jnp.mean(d * d, axis=-1, keepdims=True)
        rstd = lax.rsqrt(var + LN_EPS)
        xh = d * rstd
        y = xh * g_ref[...] + b_ref[...]
        y_ref[...] = y
        yb_ref[...] = y.astype(BF16)
        xh_ref[...] = xh
        rs_ref[...] = rstd

    row = pl.BlockSpec((tm, D), lambda i: (i, 0))
    vec = pl.BlockSpec((None, 1, D), lambda i: (layer, 0, 0))
    ins = [x, mix] + ([tg, pp] if ple else []) + [g, b]
    specs = [row, row] + ([row, row] if ple else []) + [vec, vec]
    return pl.pallas_call(body, grid=(S // tm,), in_specs=specs,
                          out_specs=(row, row, row, pl.BlockSpec((tm, 1), lambda i: (i, 0))),
                          out_shape=(jax.ShapeDtypeStruct((S, D), F32), jax.ShapeDtypeStruct((S, D), BF16),
                                     jax.ShapeDtypeStruct((S, D), F32), jax.ShapeDtypeStruct((S, 1), F32)),
                          compiler_params=_params(("parallel",)), name=name)(*ins)


def ln_bwd(name, dy, xh, rstd, g, layer, tg=None, pp=None, tm=512):
    S, D = dy.shape
    tm = _tile(S, tm)
    ple = tg is not None

    def body(*refs):
        if ple:
            dy_ref, xh_ref, rs_ref, g_ref, tg_ref, pp_ref, dr_ref, acc_ref, dtg_ref, dpp_ref = refs
        else:
            dy_ref, xh_ref, rs_ref, g_ref, dr_ref, acc_ref = refs
        dy_, xh_ = dy_ref[...], xh_ref[...]
        dxh = dy_ * g_ref[...]
        m1 = jnp.mean(dxh, axis=-1, keepdims=True)
        m2 = jnp.mean(dxh * xh_, axis=-1, keepdims=True)
        dr = rs_ref[...] * (dxh - m1 - xh_ * m2)
        dr_ref[...] = dr

        @pl.when(pl.program_id(0) == 0)
        def _():
            acc_ref[...] = jnp.zeros_like(acc_ref)
        acc_ref[0:1, :] += jnp.sum(dy_ * xh_, axis=0, keepdims=True)
        acc_ref[1:2, :] += jnp.sum(dy_, axis=0, keepdims=True)
        acc_ref[2:3, :] += jnp.sum(dr, axis=0, keepdims=True)
        if ple:
            pg = _sigmoid(tg_ref[...])
            dtg_ref[...] = (dr * pp_ref[...] * pg * (1.0 - pg)).astype(BF16)
            dpp_ref[...] = (dr * pg).astype(BF16)

    row = pl.BlockSpec((tm, D), lambda i: (i, 0))
    ins = [dy, xh, rstd, g] + ([tg, pp] if ple else [])
    specs = [row, row, pl.BlockSpec((tm, 1), lambda i: (i, 0)), pl.BlockSpec((None, 1, D), lambda i: (layer, 0, 0))] + ([row, row] if ple else [])
    outs = [jax.ShapeDtypeStruct((S, D), F32), jax.ShapeDtypeStruct((8, D), F32)]
    out_specs = [row, pl.BlockSpec((8, D), lambda i: (0, 0))]
    if ple:
        outs += [jax.ShapeDtypeStruct((S, D), BF16)] * 2
        out_specs += [row, row]
    return pl.pallas_call(body, grid=(S // tm,), in_specs=specs, out_specs=tuple(out_specs), out_shape=tuple(outs),
                          compiler_params=_params(("arbitrary",)), name=name)(*ins)


def glu_fwd(name, h_sm, tm=512):
    _, S, n = h_sm.shape
    tm = _tile(S, tm)
    half = N_CHIPS // 2

    def body(a_ref, g_ref, u_ref):
        u_ref[...] = a_ref[...] * _sigmoid(g_ref[...])

    return pl.pallas_call(body, grid=(half, S // tm),
                          in_specs=[pl.BlockSpec((None, tm, n), lambda j, i: (j, i, 0)),
                                    pl.BlockSpec((None, tm, n), lambda j, i: (j + half, i, 0))],
                          out_specs=pl.BlockSpec((tm, n), lambda j, i: (i, j)),
                          out_shape=jax.ShapeDtypeStruct((S, half * n), F32),
                          compiler_params=_params(("parallel", "parallel")), name=name)(h_sm, h_sm)


def conv_ln_silu_fwd(name, u, w, b, g, beta, layer, ts=128):
    S, D = u.shape
    kw = w.shape[1]
    ts = _tile(S, ts, CONV_HALO)
    lc = LANES if D % LANES == 0 else D

    def body(h_ref, u_ref, w_ref, b_ref, g_ref, be_ref, c_ref, s_ref, win_ref, rot_ref):
        i = pl.program_id(0)
        win_ref[0:CONV_HALO, :] = jnp.where(i == 0, 0.0, h_ref[...])
        win_ref[CONV_HALO:, :] = u_ref[...]
        for cc in range(D // lc):
            cs = slice(cc * lc, (cc + 1) * lc)
            for sub in range(1, SUBLANES):
                rot_ref[sub] = win_ref[sub:sub + ts + CONV_HALO - SUBLANES, cs]
            acc = jnp.zeros((ts, lc), F32) + b_ref[:, cs]
            for k in range(kw):
                whole, sub = divmod(CONV_HALO - (kw - 1) + k, SUBLANES)
                r0 = whole * SUBLANES
                acc = acc + w_ref[k:k + 1, cs] * (win_ref[r0:r0 + ts, cs] if sub == 0 else rot_ref[sub, r0:r0 + ts, :])
            c_ref[:, cs] = acc
        c = c_ref[...]
        mu = jnp.mean(c, axis=-1, keepdims=True)
        d = c - mu
        var = jnp.mean(d * d, axis=-1, keepdims=True)
        nrm = d * lax.rsqrt(var + LN_EPS) * g_ref[...] + be_ref[...]
        s_ref[...] = (nrm * _sigmoid(nrm)).astype(BF16)

    row = pl.BlockSpec((ts, D), lambda i: (i, 0))
    vec = pl.BlockSpec((None, 1, D), lambda i: (layer, 0, 0))
    halo = pl.BlockSpec((CONV_HALO, D), lambda i: (jnp.maximum(i * (ts // CONV_HALO) - 1, 0), 0))
    return pl.pallas_call(body, grid=(S // ts,),
                          in_specs=[halo, row, pl.BlockSpec((None, kw, D), lambda i: (layer, 0, 0)), vec, vec, vec],
                          out_specs=(row, row),
                          out_shape=(jax.ShapeDtypeStruct((S, D), F32), jax.ShapeDtypeStruct((S, D), BF16)),
                          scratch_shapes=[pltpu.VMEM((ts + CONV_HALO, D), F32),
                                          pltpu.VMEM((SUBLANES, ts + CONV_HALO - SUBLANES, lc), F32)],
                          compiler_params=_params(("parallel",)), name=name)(u, u, w, b, g, beta)


def ln_silu_bwd(name, ds, c, g, beta, layer, tm=256):
    S, D = c.shape
    tm = _tile(S, tm)

    def body(ds_ref, c_ref, g_ref, be_ref, dc_ref, acc_ref):
        c_ = c_ref[...]
        mu = jnp.mean(c_, axis=-1, keepdims=True)
        d = c_ - mu
        var = jnp.mean(d * d, axis=-1, keepdims=True)
        rstd = lax.rsqrt(var + LN_EPS)
        xh = d * rstd
        nrm = xh * g_ref[...] + be_ref[...]
        sg = _sigmoid(nrm)
        dn = ds_ref[...] * (sg * (1.0 + nrm * (1.0 - sg)))
        dxh = dn * g_ref[...]
        m1 = jnp.mean(dxh, axis=-1, keepdims=True)
        m2 = jnp.mean(dxh * xh, axis=-1, keepdims=True)
        dc = rstd * (dxh - m1 - xh * m2)
        dc_ref[...] = dc

        @pl.when(pl.program_id(0) == 0)
        def _():
            acc_ref[...] = jnp.zeros_like(acc_ref)
        acc_ref[0:1, :] += jnp.sum(dn * xh, axis=0, keepdims=True)
        acc_ref[1:2, :] += jnp.sum(dn, axis=0, keepdims=True)
        acc_ref[2:3, :] += jnp.sum(dc, axis=0, keepdims=True)

    row = pl.BlockSpec((tm, D), lambda i: (i, 0))
    vec = pl.BlockSpec((None, 1, D), lambda i: (layer, 0, 0))
    return pl.pallas_call(body, grid=(S // tm,), in_specs=[row, row, vec, vec],
                          out_specs=(row, pl.BlockSpec((8, D), lambda i: (0, 0))),
                          out_shape=(jax.ShapeDtypeStruct((S, D), F32), jax.ShapeDtypeStruct((8, D), F32)),
                          compiler_params=_params(("arbitrary",)), name=name)(ds, c, g, beta)


def conv_glu_bwd(name, dc, u, h_sm, w, layer, ts=128):
    S, D = dc.shape
    kw = w.shape[1]
    half = N_CHIPS // 2
    n = D // half
    ts = _tile(S, ts, CONV_HALO)
    nblk = S // ts
    lc = LANES if n % LANES == 0 else n

    def body(dc_ref, dcn_ref, u_ref, a_ref, g_ref, w_ref, da_ref, dg_ref, dw_ref, dba_ref, dbg_ref, dwin_ref, rot_ref):
        i = pl.program_id(1)
        dwin_ref[0:ts, :] = dc_ref[...]
        dwin_ref[ts:, :] = jnp.where(i == nblk - 1, 0.0, dcn_ref[...])

        @pl.when(i == 0)
        def _():
            dw_ref[...] = jnp.zeros_like(dw_ref)
            dba_ref[...] = jnp.zeros_like(dba_ref)
            dbg_ref[...] = jnp.zeros_like(dbg_ref)

        for cc in range(n // lc):
            cs = slice(cc * lc, (cc + 1) * lc)
            for sub in range(1, SUBLANES):
                rot_ref[sub] = dwin_ref[sub:sub + ts + CONV_HALO - SUBLANES, cs]
            ub = u_ref[:, cs]
            du = jnp.zeros((ts, lc), F32)
            for k in range(kw):
                whole, sub = divmod(kw - 1 - k, SUBLANES)
                r0 = whole * SUBLANES
                shifted = dwin_ref[r0:r0 + ts, cs] if sub == 0 else rot_ref[sub, r0:r0 + ts, :]
                du = du + w_ref[k:k + 1, cs] * shifted
                dw_ref[k:k + 1, cs] += jnp.sum(shifted * ub, axis=0, keepdims=True)
            a = a_ref[:, cs]
            sg = _sigmoid(g_ref[:, cs])
            da = du * sg
            dg = du * a * sg * (1.0 - sg)
            da_ref[:, cs] = da.astype(BF16)
            dg_ref[:, cs] = dg.astype(BF16)
            dba_ref[0:1, cs] += jnp.sum(da, axis=0, keepdims=True)
            dbg_ref[0:1, cs] += jnp.sum(dg, axis=0, keepdims=True)

    r = ts // CONV_HALO
    main = pl.BlockSpec((ts, n), lambda j, i: (i, j))
    nxt = pl.BlockSpec((CONV_HALO, n), lambda j, i: (jnp.minimum((i + 1) * r, S // CONV_HALO - 1), j))
    sm_a = pl.BlockSpec((None, ts, n), lambda j, i: (j, i, 0))
    sm_g = pl.BlockSpec((None, ts, n), lambda j, i: (j + half, i, 0))
    da, dg, dw, dba, dbg = pl.pallas_call(
        body, grid=(half, nblk),
        in_specs=[main, nxt, main, sm_a, sm_g, pl.BlockSpec((None, kw, n), lambda j, i: (layer, 0, j))],
        out_specs=(pl.BlockSpec((None, ts, n), lambda j, i: (j, i, 0)), pl.BlockSpec((None, ts, n), lambda j, i: (j, i, 0)),
                   pl.BlockSpec((None, 32, n), lambda j, i: (j, 0, 0)),
                   pl.BlockSpec((None, 8, n), lambda j, i: (j, 0, 0)), pl.BlockSpec((None, 8, n), lambda j, i: (j, 0, 0))),
        out_shape=(jax.ShapeDtypeStruct((half, S, n), BF16), jax.ShapeDtypeStruct((half, S, n), BF16),
                   jax.ShapeDtypeStruct((half, 32, n), F32),
                   jax.ShapeDtypeStruct((half, 8, n), F32), jax.ShapeDtypeStruct((half, 8, n), F32)),
        scratch_shapes=[pltpu.VMEM((ts + CONV_HALO, n), F32), pltpu.VMEM((SUBLANES, ts + CONV_HALO - SUBLANES, lc), F32)],
        compiler_params=_params(("parallel", "arbitrary")), name=name)(dc, dc, u, h_sm, h_sm, w)
    return da, dg, dw, dba, dbg


ROW_CHUNK = 16


def _ffn_gc(win_ref, w_ref, b_ref, r0, rows, kw, base):
    gc = b_ref[...] + jnp.zeros((rows, win_ref.shape[1]), F32)
    for k in range(kw):
        off = r0 + base - (kw - 1) + k
        gc = gc + w_ref[k:k + 1, :] * win_ref[off:off + rows, :]
    return gc


def ffn_gate_fwd(name, up_sm, gp_sm, w4, b4, layer, ts=256):
    _, S, n = up_sm.shape
    kw = w4.shape[2]
    ts = _tile(S, ts, ROW_CHUNK)
    rc = ROW_CHUNK

    def body(up_ref, gp_ref, gph_ref, w_ref, b_ref, hf_ref, gc_ref, win_ref):
        i = pl.program_id(1)
        win_ref[0:FFN_HALO, :] = jnp.where(i == 0, 0.0, gph_ref[...].astype(F32))
        win_ref[FFN_HALO:, :] = gp_ref[...].astype(F32)
        for r0 in range(0, ts, rc):
            gc = _ffn_gc(win_ref, w_ref, b_ref, r0, rc, kw, FFN_HALO)
            gc_ref[r0:r0 + rc, :] = gc.astype(BF16)
            hf_ref[r0:r0 + rc, :] = (gc * _sigmoid(gc) * up_ref[r0:r0 + rc, :].astype(F32)).astype(BF16)

    main = pl.BlockSpec((None, ts, n), lambda j, i: (j, i, 0))
    prv = pl.BlockSpec((None, FFN_HALO, n), lambda j, i: (j, jnp.maximum(i * (ts // FFN_HALO) - 1, 0), 0))
    sds = jax.ShapeDtypeStruct((N_CHIPS, S, n), BF16)
    return pl.pallas_call(body, grid=(N_CHIPS, S // ts),
                          in_specs=[main, main, prv, pl.BlockSpec((None, None, kw, n), lambda j, i: (j, layer, 0, 0)),
                                    pl.BlockSpec((None, None, 1, n), lambda j, i: (layer, j, 0, 0))],
                          out_specs=(main, main), out_shape=(sds, sds),
                          scratch_shapes=[pltpu.VMEM((ts + FFN_HALO, n), F32)],
                          compiler_params=_params(("parallel", "parallel")), name=name)(up_sm, gp_sm, gp_sm, w4, b4)


def ffn_gate_bwd_a(name, dhf_sm, up_sm, gc_sm, ts=256):
    _, S, n = up_sm.shape
    ts = _tile(S, ts, ROW_CHUNK)
    rc = ROW_CHUNK

    def body(dhf_ref, up_ref, gc_ref, dup_ref, dgc_ref):
        for r0 in range(0, ts, rc):
            rows = slice(r0, r0 + rc)
            gc = gc_ref[rows, :].astype(F32)
            sg = _sigmoid(gc)
            dhf = dhf_ref[rows, :].astype(F32)
            dup_ref[rows, :] = (dhf * gc * sg).astype(BF16)
            dgc_ref[rows, :] = (dhf * up_ref[rows, :].astype(F32) * (sg * (1.0 + gc * (1.0 - sg)))).astype(BF16)

    main = pl.BlockSpec((None, ts, n), lambda j, i: (j, i, 0))
    sds = jax.ShapeDtypeStruct((N_CHIPS, S, n), BF16)
    return pl.pallas_call(body, grid=(N_CHIPS, S // ts), in_specs=[main, main, main], out_specs=(main, main), out_shape=(sds, sds),
                          compiler_params=_params(("parallel", "parallel")), name=name)(dhf_sm, up_sm, gc_sm)


def ffn_up_gate_fwd(name, xb, w_up, w_gate, cw4, cb4, layer, tm=256):
    S, K = xb.shape
    n = w_up.shape[-1]
    kw = cw4.shape[2]
    tm = _tile(S, tm, ROW_CHUNK)
    rc = ROW_CHUNK

    def body(*refs):
        x_ref, xh_ref = refs[0:2]
        wu, wg, cw, cb = refs[2:6], refs[6:10], refs[10:14], refs[14:18]
        up_ref, gp_ref, gc_ref, hf_ref, win_ref = refs[18:]
        i = pl.program_id(0)
        xv = x_ref[...]
        xhalo = xh_ref[...]
        for j in range(N_CHIPS):
            wgj = wg[j][...]
            gp = jnp.dot(xv, wgj, preferred_element_type=F32)
            up = jnp.dot(xv, wu[j][...], preferred_element_type=F32)
            gph = jnp.dot(xhalo, wgj, preferred_element_type=F32)
            gpb = gp.astype(BF16)
            gp_ref[j] = gpb
            up_ref[j] = up.astype(BF16)
            win_ref[0:FFN_HALO, :] = jnp.where(i == 0, 0.0, gph.astype(BF16).astype(F32))
            win_ref[FFN_HALO:, :] = gpb.astype(F32)
            for r0 in range(0, tm, rc):
                gc = _ffn_gc(win_ref, cw[j], cb[j], r0, rc, kw, FFN_HALO)
                gc_ref[j, r0:r0 + rc, :] = gc.astype(BF16)
                hf_ref[j, r0:r0 + rc, :] = (gc * _sigmoid(gc) * up_ref[j, r0:r0 + rc, :].astype(F32)).astype(BF16)

    r = tm // FFN_HALO
    out = pl.BlockSpec((N_CHIPS, tm, n), lambda i: (0, i, 0))
    sds = jax.ShapeDtypeStruct((N_CHIPS, S, n), BF16)
    specs = [pl.BlockSpec((tm, K), lambda i: (i, 0)), pl.BlockSpec((FFN_HALO, K), lambda i: (jnp.maximum(i * r - 1, 0), 0))]
    specs += [pl.BlockSpec((None, None, K, n), lambda i, j=j: (j, layer, 0, 0)) for j in range(N_CHIPS)] * 2
    specs += [pl.BlockSpec((None, None, kw, n), lambda i, j=j: (j, layer, 0, 0)) for j in range(N_CHIPS)]
    specs += [pl.BlockSpec((None, None, 1, n), lambda i, j=j: (layer, j, 0, 0)) for j in range(N_CHIPS)]
    return pl.pallas_call(body, grid=(S // tm,), in_specs=specs, out_specs=(out, out, out, out), out_shape=(sds, sds, sds, sds),
                          scratch_shapes=[pltpu.VMEM((tm + FFN_HALO, n), F32)],
                          compiler_params=_params(("parallel",)), name=name)(
                              xb, xb, *([w_up] * N_CHIPS), *([w_gate] * N_CHIPS), *([cw4] * N_CHIPS), *([cb4] * N_CHIPS))


def ffn_gate_conv_bwd(name, dr, w_down, cw4, layer, up_sm, gc_sm, gp_sm, tm=256):
    n, N = w_down.shape[-2:]
    S = dr.shape[0]
    kw = cw4.shape[2]
    tm = _tile(S, tm, ROW_CHUNK)
    rc = ROW_CHUNK
    nblk = S // tm

    def dgc_of(dhf, up, gc):
        sg = _sigmoid(gc)
        return dhf * up * (sg * (1.0 + gc * (1.0 - sg))), sg

    def body(*refs):
        dr_ref, drn_ref = refs[0:2]
        wd, cw = refs[2:6], refs[6:10]
        up_ref, gc_ref, gp_ref, upn_ref, gcn_ref, dup_ref, dgp_ref, acc_ref, win_ref = refs[10:]
        i = pl.program_id(0)

        @pl.when(i == 0)
        def _():
            acc_ref[...] = jnp.zeros_like(acc_ref)
        a = dr_ref[...].astype(BF16)
        an = drn_ref[...].astype(BF16)
        for j in range(N_CHIPS):
            wj = wd[j][...]
            dhf = lax.dot_general(a, wj, NT, preferred_element_type=F32)
            gc = gc_ref[j].astype(F32)
            dgc, sg = dgc_of(dhf, up_ref[j].astype(F32), gc)
            dup_ref[j] = (dhf * gc * sg).astype(BF16)
            win_ref[0:tm, :] = dgc
            dhfn = lax.dot_general(an, wj, NT, preferred_element_type=F32)
            dgcn, _ = dgc_of(dhfn, upn_ref[j].astype(F32), gcn_ref[j].astype(F32))
            win_ref[tm:, :] = jnp.where(i == nblk - 1, 0.0, dgcn)
            sums = [jnp.zeros((8, n), F32) for _ in range(kw + 1)]
            for r0 in range(0, tm, rc):
                gp = gp_ref[j, r0:r0 + rc, :].astype(F32)
                dgp = jnp.zeros((rc, n), F32)
                for k in range(kw):
                    d = kw - 1 - k
                    shifted = win_ref[r0 + d:r0 + d + rc, :]
                    dgp = dgp + cw[j][k:k + 1, :] * shifted
                    prod = shifted * gp
                    sums[k] = sums[k] + prod[0:8, :] + prod[8:16, :]
                    if d == 0:
                        sums[kw] = sums[kw] + shifted[0:8, :] + shifted[8:16, :]
                dgp_ref[j, r0:r0 + rc, :] = dgp.astype(BF16)
            for k in range(kw):
                acc_ref[j, k:k + 1, :] += jnp.sum(sums[k], axis=0, keepdims=True)
            acc_ref[j, 7:8, :] += jnp.sum(sums[kw], axis=0, keepdims=True)

    r = tm // FFN_HALO
    nxt_row = lambda i: jnp.minimum((i + 1) * r, S // FFN_HALO - 1)
    blk = pl.BlockSpec((N_CHIPS, tm, n), lambda i: (0, i, 0))
    halo = pl.BlockSpec((N_CHIPS, FFN_HALO, n), lambda i: (0, nxt_row(i), 0))
    sds = jax.ShapeDtypeStruct((N_CHIPS, S, n), BF16)
    specs = [pl.BlockSpec((tm, N), lambda i: (i, 0)), pl.BlockSpec((FFN_HALO, N), lambda i: (nxt_row(i), 0))]
    specs += [pl.BlockSpec((None, None, n, N), lambda i, j=j: (j, layer, 0, 0)) for j in range(N_CHIPS)]
    specs += [pl.BlockSpec((None, None, kw, n), lambda i, j=j: (j, layer, 0, 0)) for j in range(N_CHIPS)]
    specs += [blk, blk, blk, halo, halo]
    return pl.pallas_call(body, grid=(nblk,), in_specs=specs,
                          out_specs=(blk, blk, pl.BlockSpec((N_CHIPS, 8, n), lambda i: (0, 0, 0))),
                          out_shape=(sds, sds, jax.ShapeDtypeStruct((N_CHIPS, 8, n), F32)),
                          scratch_shapes=[pltpu.VMEM((tm + FFN_HALO, n), F32)],
                          compiler_params=_params(("arbitrary",)), name=name)(
                              dr, dr, *([w_down] * N_CHIPS), *([cw4] * N_CHIPS), up_sm, gc_sm, gp_sm, up_sm, gc_sm)


def ffn_gate_bwd_b(name, dgc_sm, gp_sm, w4, layer, ts=256):
    _, S, n = gp_sm.shape
    kw = w4.shape[2]
    ts = _tile(S, ts, ROW_CHUNK)
    rc = ROW_CHUNK
    nblk = S // ts

    def body(dgc_ref, dgn_ref, gp_ref, w_ref, dgp_ref, acc_ref, dwin_ref):
        i = pl.program_id(1)
        dwin_ref[0:ts, :] = dgc_ref[...].astype(F32)
        dwin_ref[ts:, :] = jnp.where(i == nblk - 1, 0.0, dgn_ref[...].astype(F32))

        @pl.when(i == 0)
        def _():
            acc_ref[...] = jnp.zeros_like(acc_ref)
        sums = [jnp.zeros((8, n), F32) for _ in range(kw + 1)]
        for r0 in range(0, ts, rc):
            gp = gp_ref[r0:r0 + rc, :].astype(F32)
            dgp = jnp.zeros((rc, n), F32)
            for k in range(kw):
                d = kw - 1 - k
                shifted = dwin_ref[r0 + d:r0 + d + rc, :]
                dgp = dgp + w_ref[k:k + 1, :] * shifted
                prod = shifted * gp
                sums[k] = sums[k] + prod[0:8, :] + prod[8:16, :]
                if d == 0:
                    sums[kw] = sums[kw] + shifted[0:8, :] + shifted[8:16, :]
            dgp_ref[r0:r0 + rc, :] = dgp.astype(BF16)
        for k in range(kw):
            acc_ref[k:k + 1, :] += jnp.sum(sums[k], axis=0, keepdims=True)
        acc_ref[7:8, :] += jnp.sum(sums[kw], axis=0, keepdims=True)

    r = ts // FFN_HALO
    main = pl.BlockSpec((None, ts, n), lambda j, i: (j, i, 0))
    nxt = pl.BlockSpec((None, FFN_HALO, n), lambda j, i: (j, jnp.minimum((i + 1) * r, S // FFN_HALO - 1), 0))
    return pl.pallas_call(body, grid=(N_CHIPS, nblk),
                          in_specs=[main, nxt, main, pl.BlockSpec((None, None, kw, n), lambda j, i: (j, layer, 0, 0))],
                          out_specs=(main, pl.BlockSpec((None, 8, n), lambda j, i: (j, 0, 0))),
                          out_shape=(jax.ShapeDtypeStruct((N_CHIPS, S, n), BF16), jax.ShapeDtypeStruct((N_CHIPS, 8, n), F32)),
                          scratch_shapes=[pltpu.VMEM((ts + FFN_HALO, n), F32)],
                          compiler_params=_params(("parallel", "arbitrary")), name=name)(dgc_sm, dgc_sm, gp_sm, w4)


def _neg_softplus(z):
    e = jnp.exp(-jnp.abs(z))
    return -(jnp.maximum(z, 0.0) + jnp.log(1.0 + e)), e


def _split_dot(x, t):
    hi = x.astype(BF16)
    lo = (x - hi.astype(F32)).astype(BF16)
    return jnp.dot(hi, t, preferred_element_type=F32) + jnp.dot(lo, t, preferred_element_type=F32)


STICK_GONE = -100.0
NOT_SWEPT = -1e30


def attn_fwd(name, q, k, v, bq=512, w=256):
    S, D = q.shape
    dh = HEAD_DIM
    hpb = LANES // dh
    bq = _tile(S, bq)
    w = _tile(bq, w)
    nsub = bq // w
    nkb = S // w

    def body(q_ref, k_ref, v_ref, o_ref, runs_ref, rs_ref):
        qi = pl.program_id(1)
        rr = lax.broadcasted_iota(jnp.int32, (w, w), 0)
        cc = lax.broadcasted_iota(jnp.int32, (w, w), 1)
        t_suf = (rr >= cc).astype(BF16)
        tq = qi * bq + lax.broadcasted_iota(jnp.int32, (bq, w), 0)
        tk = lax.broadcasted_iota(jnp.int32, (bq, w), 1)
        lane = lax.broadcasted_iota(jnp.int32, (bq, LANES), 1)
        ntot = (qi + 1) * nsub
        heads = [slice(hh * dh, (hh + 1) * dh) for hh in range(hpb)]
        qbs = [q_ref[:, hs] for hs in heads]
        for hh in range(hpb):
            rs_ref[hh] = jnp.where(lane < ntot, NOT_SWEPT, 0.0)

        def block(kb, carry, masked):
            kstart = pl.multiple_of(kb * w, w)
            if masked:
                m = (tk + kstart) < tq
            out = []
            for hh, hs in enumerate(heads):
                run, acc = carry[2 * hh], carry[2 * hh + 1]
                kblk = k_ref[pl.ds(kstart, w), hs]
                vblk = v_ref[pl.ds(kstart, w), hs]
                z = lax.dot_general(qbs[hh], kblk, NT, preferred_element_type=F32)
                lg, _ = _neg_softplus(z)
                if masked:
                    lg = jnp.where(m, lg, 0.0)
                cum = _split_dot(lg, t_suf) + run
                a = jnp.exp(z + cum)
                if masked:
                    a = jnp.where(m, a, 0.0)
                acc = acc + jnp.dot(a.astype(BF16), vblk, preferred_element_type=F32)
                run = cum[:, 0:1]
                rs_ref[hh] = jnp.where(lane == kb, run, rs_ref[hh])
                out += [run, acc]
            return tuple(out)

        carry = (jnp.zeros((bq, 1), F32), jnp.zeros((bq, dh), F32)) * hpb
        for sb in reversed(range(nsub)):
            carry = block(qi * nsub + sb, carry, True)

        def cond(c):
            alive = functools.reduce(jnp.maximum, [jnp.max(c[1 + 2 * hh]) for hh in range(hpb)])
            return jnp.logical_and(c[0] >= 0, alive > STICK_GONE)

        def step(c):
            return (c[0] - 1,) + block(c[0], c[1:], False)
        carry = lax.while_loop(cond, step, (qi * nsub - 1,) + carry)[1:]
        for hh, hs in enumerate(heads):
            o_ref[:, hs] = carry[2 * hh + 1].astype(o_ref.dtype)
            runs_ref[hh] = rs_ref[hh, :, 0:nkb]

    qs = pl.BlockSpec((bq, LANES), lambda h, i: (i, h))
    kv = pl.BlockSpec((S, LANES), lambda h, i: (0, h))
    return pl.pallas_call(body, grid=(D // LANES, S // bq), in_specs=[qs, kv, kv],
                          out_specs=(qs, pl.BlockSpec((hpb, bq, nkb), lambda h, i: (h, i, 0))),
                          out_shape=(jax.ShapeDtypeStruct((S, D), BF16), jax.ShapeDtypeStruct((D // dh, S, nkb), F32)),
                          scratch_shapes=[pltpu.VMEM((hpb, bq, LANES), F32)],
                          compiler_params=_params(("parallel", "parallel")), name=name)(q, k, v)


def attn_bwd(name, q, k, v, do, runs, dk0=None, dv0=None, bq=512, w=256):
    S, D = q.shape
    dh = HEAD_DIM
    hpb = LANES // dh
    bq = _tile(S, bq)
    w = _tile(bq, w)
    nsub = bq // w
    nkb = S // w
    scale = 1.0 / math.sqrt(dh)
    init = dk0 is not None

    def body(*refs):
        if init:
            q_ref, k_ref, v_ref, do_ref, runs_ref, dk0_ref, dv0_ref, dq_ref, dk_ref, dv_ref, rs_ref = refs
        else:
            q_ref, k_ref, v_ref, do_ref, runs_ref, dq_ref, dk_ref, dv_ref, rs_ref = refs
        qi = pl.program_id(1)

        @pl.when(qi == 0)
        def _():
            dk_ref[...] = dk0_ref[...] if init else jnp.zeros_like(dk_ref)
            dv_ref[...] = dv0_ref[...] if init else jnp.zeros_like(dv_ref)

        rr = lax.broadcasted_iota(jnp.int32, (w, w), 0)
        cc = lax.broadcasted_iota(jnp.int32, (w, w), 1)
        t_suf = (rr >= cc).astype(BF16)
        t_pre = (rr <= cc).astype(BF16)
        tq = qi * bq + lax.broadcasted_iota(jnp.int32, (bq, w), 0)
        tk = lax.broadcasted_iota(jnp.int32, (bq, w), 1)
        lane = lax.broadcasted_iota(jnp.int32, (bq, LANES), 1)
        lane1 = lax.broadcasted_iota(jnp.int32, (1, LANES), 1)
        ntot = (qi + 1) * nsub
        heads = [slice(hh * dh, (hh + 1) * dh) for hh in range(hpb)]
        qbs = [q_ref[:, hs] for hs in heads]
        dobs = [do_ref[:, hs].astype(BF16) for hs in heads]
        kb0 = ntot - nsub
        for hh in range(hpb):
            rs_ref[hh] = jnp.zeros((bq, LANES), F32)
            rs_ref[hh, :, 0:nkb] = runs_ref[hh]
            colmax = jnp.max(rs_ref[hh], axis=0, keepdims=True)
            dead = jnp.logical_and(jnp.logical_and(lane1 >= 1, lane1 <= ntot), colmax <= STICK_GONE)
            kb0 = jnp.minimum(kb0, jnp.sum(dead.astype(jnp.int32)))

        def block(kb, carry, masked):
            kstart = pl.multiple_of(kb * w, w)
            if masked:
                m = (tk + kstart) < tq
            out = []
            for hh, hs in enumerate(heads):
                pg_run, dq = carry[2 * hh], carry[2 * hh + 1]
                qb, dob = qbs[hh], dobs[hh]
                kblk = k_ref[pl.ds(kstart, w), hs]
                vblk = v_ref[pl.ds(kstart, w), hs]
                right = jnp.sum(jnp.where(lane == kb + 1, rs_ref[hh], 0.0), axis=1, keepdims=True)
                z = lax.dot_general(qb, kblk, NT, preferred_element_type=F32)
                lg, _ = _neg_softplus(z)
                sig = 1.0 - jnp.exp(lg)
                if masked:
                    lg = jnp.where(m, lg, 0.0)
                a = jnp.exp(z + _split_dot(lg, t_suf) + right)
                if masked:
                    a = jnp.where(m, a, 0.0)
                da = lax.dot_general(dob, vblk, NT, preferred_element_type=F32)
                g = da * a
                pin = _split_dot(g, t_pre) + pg_run
                dz = g - sig * pin
                if masked:
                    dz = jnp.where(m, dz, 0.0)
                dzb = dz.astype(BF16)
                dq = dq + jnp.dot(dzb, kblk, preferred_element_type=F32)
                dk_ref[pl.ds(kstart, w), hs] += lax.dot_general(dzb, qb, TN, preferred_element_type=F32)
                dv_ref[pl.ds(kstart, w), hs] += lax.dot_general(a.astype(BF16), dob, TN, preferred_element_type=F32)
                out += [pin[:, w - 1:w], dq]
            return tuple(out)

        carry = (jnp.zeros((bq, 1), F32), jnp.zeros((bq, dh), F32)) * hpb
        carry = lax.fori_loop(kb0, qi * nsub, lambda kb, c: block(kb, c, False), carry)
        for sb in range(nsub):
            carry = block(qi * nsub + sb, carry, True)
        for hh, hs in enumerate(heads):
            dq_ref[:, hs] = carry[2 * hh + 1] * scale

    qs = pl.BlockSpec((bq, LANES), lambda h, i: (i, h))
    kv = pl.BlockSpec((S, LANES), lambda h, i: (0, h))
    ins = [q, k, v, do, runs] + ([dk0, dv0] if init else [])
    specs = [qs, kv, kv, qs, pl.BlockSpec((hpb, bq, nkb), lambda h, i: (h, i, 0))] + ([kv, kv] if init else [])
    sds = jax.ShapeDtypeStruct((S, D), F32)
    return pl.pallas_call(body, grid=(D // LANES, S // bq), in_specs=specs, out_specs=(qs, kv, kv), out_shape=(sds, sds, sds),
                          scratch_shapes=[pltpu.VMEM((hpb, bq, LANES), F32)],
                          compiler_params=_params(("parallel", "arbitrary")), name=name)(*ins)


def loss_head(name, y, tgt, tm=512):
    S, D = y.shape
    tm = _tile(S, tm)

    def body(y_ref, t_ref, dy_ref, acc_ref):
        @pl.when(pl.program_id(0) == 0)
        def _():
            acc_ref[...] = jnp.zeros_like(acc_ref)
        e = y_ref[...] - t_ref[...]
        dy_ref[...] = e * (1.0 / D)
        acc_ref[...] += jnp.sum(e * e)

    row = pl.BlockSpec((tm, D), lambda i: (i, 0))
    return pl.pallas_call(body, grid=(S // tm,), in_specs=[row, row],
                          out_specs=(row, pl.BlockSpec((8, LANES), lambda i: (0, 0))),
                          out_shape=(jax.ShapeDtypeStruct((S, D), F32), jax.ShapeDtypeStruct((8, LANES), F32)),
                          compiler_params=_params(("arbitrary",)), name=name)(y, tgt)


def _to_sm(a, axis=-1):
    axis = axis % a.ndim
    shp = a.shape[:axis] + (N_CHIPS, a.shape[axis] // N_CHIPS) + a.shape[axis + 1:]
    return jnp.moveaxis(a.reshape(shp), axis, 0)


def _from_sm(a, axis=-1):
    nd = a.ndim - 1
    axis = axis % nd
    b = jnp.moveaxis(a, 0, axis)
    return b.reshape(b.shape[:axis] + (b.shape[axis] * b.shape[axis + 1],) + b.shape[axis + 2:])


class GradBuffers:
    def __init__(self, W):
        groups = {}
        for n in BIG:
            _, layers, rows, cols = W[n].shape
            groups.setdefault(cols, []).append((rows, n, layers))
        self.where, self.cols_of, self.buf, self.members = {}, {}, {}, {}
        for cols, items in groups.items():
            off, members = 0, []
            for rows, n, layers in sorted(items, key=lambda t: -t[0]):
                assert off % rows == 0
                self.where[n], self.cols_of[n] = (off, rows), cols
                members.append((n, off, rows * layers))
                off += rows * layers
            assert off % 32 == 0
            self.buf[cols] = lax.empty((N_CHIPS, off, cols), BF16)
            self.members[cols] = members

    def put(self, n, layer, fn, **kw):
        cols = self.cols_of[n]
        off, rows = self.where[n]
        self.buf[cols] = fn(into=self.buf[cols], off=off + layer * rows, **kw)


def forward_backward(x, p4, tgt, W):
    S, D = x.shape
    scale = 1.0 / math.sqrt(HEAD_DIM)
    saved = []
    kh = vh = xb_kv = None
    xb = x.astype(BF16)
    for i in range(DEPTH):
        sv = {'xb': xb}
        if i < N_A:
            h_sm, u = mm_pw1_glu(f"pw1glu_{i}", xb, W['a_pw1_w'], i, W['a_pw1_b'])
            c, s = conv_ln_silu_fwd(f"convln_{i}", u, W['a_dw_w'], W['a_dw_b'], W['a_ln_g'], W['a_ln_b'], i)
            x1, x1b, xh1, rs1 = mm_rowsm_ln(f"pw2ln_{i}", s, W['a_pw2_w'], i, x, W['ln_mix_g'], W['ln_mix_b'], i, bias=W['a_pw2_b'])
            sv.update(h_sm=h_sm, u=u, c=c, s=s)
        else:
            j = i - N_A
            if kh is None:
                xb_kv = xb
                kh = mm_rowsm("wk", xb, W['kv_wk'], 0, out_dtype=BF16)
                vh = mm_rowsm("wv", xb, W['kv_wv'], 0, out_dtype=BF16)
            qh = mm_rowsm(f"wq_{j}", xb, W['b_wq'], j, out_dtype=BF16, out_scale=scale)
            o, runs = attn_fwd(f"attn_{j}", qh, kh, vh)
            x1, x1b, xh1, rs1 = mm_rowsm_ln(f"woln_{j}", o, W['b_wo'], j, x, W['ln_mix_g'], W['ln_mix_b'], i)
            sv.update(qh=qh, o=o, runs=runs)
        up_sm, gp_sm, gc_sm, hf_sm = ffn_up_gate_fwd(f"upgate_{i}", x1b, W['ffn_w_up'], W['ffn_w_gate'], W['ffn_conv_w'],
                                                     W['ffn_conv_b'], i)
        tg = mm_rowsm(f"plegate_{i}", x1b, W['ple_w_gate'], i)
        pp = mm_proj(f"pleproj_{i}", p4, i, W['ple_w_proj'])
        x2, x2b, xh2, rs2 = mm_rowsm_ln(f"downln_{i}", hf_sm, W['ffn_w_down'], i, x1, W['ln_ffn_g'], W['ln_ffn_b'], i, a_sm=True,
                                        tg=tg, pp=pp)
        sv.update(x1b=x1b, xh1=xh1, rs1=rs1, up_sm=up_sm, gp_sm=gp_sm, gc_sm=gc_sm, hf_sm=hf_sm, tg=tg, pp=pp, xh2=xh2, rs2=rs2)
        saved.append(sv)
        x, xb = x2, x2b

    dx, lacc = loss_head("loss", x, tgt)
    loss_sum = lacc[0, 0]

    G = {n: [None] * DEPTH for n in WEIGHTS if n not in BIG}
    gb = GradBuffers(W)
    dk = dv = None
    for i in reversed(range(DEPTH)):
        sv = saved[i]
        dr, acc, dtg, dpp = ln_bwd(f"lnffn_b_{i}", dx, sv['xh2'], sv['rs2'], W['ln_ffn_g'], i, tg=sv['tg'], pp=sv['pp'])
        G['ln_ffn_g'][i], G['ln_ffn_b'][i] = acc[0], acc[1]
        gb.put('ple_w_proj', i, functools.partial(mm_tn_proj, f"dproj_{i}", p4, i, dpp))
        gb.put('ple_w_gate', i, functools.partial(mm_tn_rowsm_fan, f"dplegate_{i}", sv['x1b'], dtg))
        gb.put('ffn_w_down', i, functools.partial(mm_tn_rowsm, f"ddown_{i}", sv['hf_sm'], dr, a_sm=True))
        dup_sm, dgp_sm, cacc = ffn_gate_conv_bwd(f"gateconv_b_{i}", dr, W['ffn_w_down'], W['ffn_conv_w'], i, sv['up_sm'],
                                                 sv['gc_sm'], sv['gp_sm'])
        kw = W['ffn_conv_w'].shape[2]
        G['ffn_conv_w'][i] = cacc[:, 0:kw, :]
        G['ffn_conv_b'][i] = cacc[:, 7, :].reshape(-1)
        gb.put('ffn_w_up', i, functools.partial(mm_tn_colsm, f"dup_{i}", sv['x1b'], dup_sm))
        gb.put('ffn_w_gate', i, functools.partial(mm_tn_colsm, f"dgate_{i}", sv['x1b'], dgp_sm))
        dx1 = mm_nt_rowsm(f"dx1a_{i}", dtg, W['ple_w_gate'], i, res=dr, res_alpha=DN_ALPHA)
        dx1 = mm_nt_colsm(f"dx1b_{i}", _sm_parts(dup_sm), W['ffn_w_up'], i, res=dx1)
        dx1 = mm_nt_colsm(f"dx1c_{i}", _sm_parts(dgp_sm), W['ffn_w_gate'], i, res=dx1)

        dr1, acc1 = ln_bwd(f"lnmix_b_{i}", dx1, sv['xh1'], sv['rs1'], W['ln_mix_g'], i)
        G['ln_mix_g'][i], G['ln_mix_b'][i] = acc1[0], acc1[1]
        xin = sv['xb']
        if i < N_A:
            G['a_pw2_b'][i] = acc1[2]
            gb.put('a_pw2_w', i, functools.partial(mm_tn_rowsm_fan, f"dpw2_{i}", sv['s'], dr1))
            ds = mm_nt_rowsm(f"ds_{i}", dr1, W['a_pw2_w'], i)
            dc, cacc = ln_silu_bwd(f"lnsilu_b_{i}", ds, sv['c'], W['a_ln_g'], W['a_ln_b'], i)
            G['a_ln_g'][i], G['a_ln_b'][i], G['a_dw_b'][i] = cacc[0], cacc[1], cacc[2]
            da, dg, dw, dba, dbg = conv_glu_bwd(f"convglu_b_{i}", dc, sv['u'], sv['h_sm'], W['a_dw_w'], i)
            kw = W['a_dw_w'].shape[1]
            G['a_dw_w'][i] = _from_sm(dw[:, 0:kw, :], axis=-1)
            G['a_pw1_b'][i] = jnp.concatenate([dba[:, 0, :], dbg[:, 0, :]], axis=0)
            half = da.shape[0]
            gb.put('a_pw1_w', i, functools.partial(mm_tn_colsm, f"dpw1a_{i}", xin, da))
            gb.put('a_pw1_w', i, functools.partial(mm_tn_colsm, f"dpw1g_{i}", xin, dg), j0=half)
            dx = mm_nt_colsm(f"dxa_{i}", _sm_parts(da) + _sm_parts(dg), W['a_pw1_w'], i, res=dr1, res_alpha=DN_ALPHA)
        else:
            j = i - N_A
            gb.put('b_wo', j, functools.partial(mm_tn_rowsm_fan, f"dwo_{j}", sv['o'], dr1))
            do = mm_nt_rowsm(f"do_{j}", dr1, W['b_wo'], j)
            dq, dk, dv = attn_bwd(f"attn_b_{j}", sv['qh'], kh, vh, do, sv['runs'], dk, dv)
            gb.put('b_wq', j, functools.partial(mm_tn_rowsm_fan, f"dwq_{j}", xin, dq))
            dx = mm_nt_rowsm(f"dxq_{j}", dq, W['b_wq'], j, res=dr1, res_alpha=DN_ALPHA)
            if j == 0:
                gb.put('kv_wk', 0, functools.partial(mm_tn_rowsm_fan, "dwk", xb_kv, dk))
                gb.put('kv_wv', 0, functools.partial(mm_tn_rowsm_fan, "dwv", xb_kv, dv))
                dx = mm_nt_rowsm("dxk", dk, W['kv_wk'], 0, res=dx)
                dx = mm_nt_rowsm("dxv", dv, W['kv_wv'], 0, res=dx)
    return loss_sum, dx, G, gb


MESH = pl.DeviceIdType.MESH
HBM = pl.BlockSpec(memory_space=pltpu.HBM)


def _place():
    x, y, c = lax.axis_index("x"), lax.axis_index("y"), lax.axis_index("c")
    others = [(1 - x, y), (x, 1 - y), (1 - x, 1 - y)]
    return x, y, c, others


def allgather_chips(name, arrs):
    n = len(arrs)

    def body(*refs):
        ins, outs = refs[:n], refs[n:2 * n]
        send_sems, recv_sems = refs[2 * n:]
        x, y, c, others = _place()
        me = 2 * x + y
        sibling = (x, y, 1 - c)
        ids = [2 * ch[0] + ch[1] for ch in others]
        from_id = jnp.where(c == 0, ids[0], ids[1])
        to_chip = (jnp.where(c == 0, x, 1 - x), jnp.where(c == 0, 1 - y, y))

        def remote(a, k, src, chip_id, half, to):
            return pltpu.make_async_remote_copy(src_ref=src, dst_ref=outs[a].at[chip_id, half], send_sem=send_sems.at[a, k],
                                                recv_sem=recv_sems.at[a, k], device_id=to, device_id_type=MESH)

        sent = [remote(a, k, ins[a].at[c], me, c, (others[k][0], others[k][1], c)) for a in range(n) for k in range(2)]
        for cp in sent:
            cp.start()
        for a in range(n):
            for k in range(2):
                remote(a, k, ins[a].at[c], ids[k], c, sibling).wait_recv()
            sent.append(remote(a, 2, outs[a].at[from_id, c], from_id, c, (to_chip[0], to_chip[1], c)))
            sent[-1].start()
            for k in range(2):
                sent.append(remote(a, 3 + k, outs[a].at[ids[k], c], ids[k], c, sibling))
                sent[-1].start()
        for a in range(n):
            remote(a, 2, ins[a].at[c], ids[2], c, sibling).wait_recv()
            sent.append(remote(a, 5, outs[a].at[ids[2], c], ids[2], c, sibling))
            sent[-1].start()
        for a in range(n):
            for k in range(3):
                remote(a, 3 + k, ins[a].at[c], ids[k], 1 - c, sibling).wait_recv()
        for cp in sent:
            cp.wait_send()

    outs = pl.pallas_call(body, out_shape=tuple(jax.ShapeDtypeStruct((N_CHIPS,) + a.shape, a.dtype) for a in arrs),
                          in_specs=[HBM] * n, out_specs=tuple([HBM] * n),
                          scratch_shapes=[pltpu.SemaphoreType.DMA((n, 6)), pltpu.SemaphoreType.DMA((n, 6))],
                          name=name)(*arrs)
    me = 2 * lax.axis_index("x") + lax.axis_index("y")
    return [lax.dynamic_update_index_in_dim(o, a, me, 0) for o, a in zip(outs, arrs)]


def exchange_sibling(name, gs):
    n = len(gs)

    def body(*refs):
        g_refs, o_refs = refs[:n], refs[n:2 * n]
        send_sems, recv_sems = refs[2 * n:]
        x, y, c, _ = _place()
        cps = [pltpu.make_async_remote_copy(src_ref=g_refs[a].at[j, 1 - c], dst_ref=o_refs[a].at[j], send_sem=send_sems.at[a, j],
                                            recv_sem=recv_sems.at[a, j], device_id=(x, y, 1 - c), device_id_type=MESH)
               for a in range(n) for j in range(N_CHIPS)]
        for cp in cps:
            cp.start()
        for cp in cps:
            cp.wait()

    return pl.pallas_call(body, out_shape=tuple(jax.ShapeDtypeStruct((N_CHIPS,) + g.shape[2:], g.dtype) for g in gs),
                          in_specs=[HBM] * n, out_specs=tuple([HBM] * n),
                          scratch_shapes=[pltpu.SemaphoreType.DMA((n, N_CHIPS)), pltpu.SemaphoreType.DMA((n, N_CHIPS))],
                          name=name)(*gs)


def _ring_peers():
    x, y, c, _ = _place()
    first = (jnp.where(c == 0, 1 - x, x), jnp.where(c == 0, y, 1 - y))
    second = (jnp.where(c == 0, x, 1 - x), jnp.where(c == 0, 1 - y, y))
    return c, first, second, 2 * (1 - x) + (1 - y)


def exchange_first(name, ss):
    n = len(ss)

    def body(*refs):
        s_refs, o_refs = refs[:n], refs[n:2 * n]
        send_sems, recv_sems = refs[2 * n:]
        c, first, _, diag = _ring_peers()
        cps = [pltpu.make_async_remote_copy(src_ref=s_refs[a].at[slot], dst_ref=o_refs[a].at[k], send_sem=send_sems.at[a, k],
                                            recv_sem=recv_sems.at[a, k], device_id=(first[0], first[1], c), device_id_type=MESH)
               for a in range(n) for k, slot in enumerate((2 * first[0] + first[1], diag))]
        for cp in cps:
            cp.start()
        for cp in cps:
            cp.wait()

    return pl.pallas_call(body, out_shape=tuple(jax.ShapeDtypeStruct((2,) + s.shape[1:], s.dtype) for s in ss),
                          in_specs=[HBM] * n, out_specs=tuple([HBM] * n),
                          scratch_shapes=[pltpu.SemaphoreType.DMA((n, 2)), pltpu.SemaphoreType.DMA((n, 2))], name=name)(*ss)


def exchange_second(name, ts):
    n = len(ts)

    def body(*refs):
        t_refs, o_refs = refs[:n], refs[n:2 * n]
        send_sems, recv_sems = refs[2 * n:]
        c, _, second, _ = _ring_peers()
        cps = [pltpu.make_async_remote_copy(src_ref=t_refs[a], dst_ref=o_refs[a], send_sem=send_sems.at[a], recv_sem=recv_sems.at[a],
                                            device_id=(second[0], second[1], c), device_id_type=MESH) for a in range(n)]
        for cp in cps:
            cp.start()
        for cp in cps:
            cp.wait()

    return pl.pallas_call(body, out_shape=tuple(jax.ShapeDtypeStruct(t.shape, t.dtype) for t in ts),
                          in_specs=[HBM] * n, out_specs=tuple([HBM] * n),
                          scratch_shapes=[pltpu.SemaphoreType.DMA((n,)), pltpu.SemaphoreType.DMA((n,))], name=name)(*ts)


def share_sibling(name, ts):
    n = len(ts)

    def body(*refs):
        o_refs = refs[n:2 * n]
        send_sems, recv_sems = refs[2 * n:]
        x, y, c, _ = _place()
        cps = [pltpu.make_async_remote_copy(src_ref=o_refs[a].at[c], dst_ref=o_refs[a].at[c], send_sem=send_sems.at[a],
                                            recv_sem=recv_sems.at[a], device_id=(x, y, 1 - c), device_id_type=MESH)
               for a in range(n)]
        for cp in cps:
            cp.start()
        for a in range(n):
            pltpu.make_async_remote_copy(src_ref=o_refs[a].at[c], dst_ref=o_refs[a].at[1 - c], send_sem=send_sems.at[a],
                                         recv_sem=recv_sems.at[a], device_id=(x, y, 1 - c), device_id_type=MESH).wait_recv()
        for cp in cps:
            cp.wait_send()

    return pl.pallas_call(body, out_shape=tuple(jax.ShapeDtypeStruct(t.shape, t.dtype) for t in ts),
                          in_specs=[HBM] * n, out_specs=tuple([HBM] * n), input_output_aliases={a: a for a in range(n)},
                          scratch_shapes=[pltpu.SemaphoreType.DMA((n,)), pltpu.SemaphoreType.DMA((n,))],
                          name=name)(*ts)


def add_halves(name, g, recv, place, out_dtype, tr=512):
    _, _, R, C = g.shape
    tr = _tile(R, tr, 16)

    def body(p_ref, a_ref, b_ref, o_ref):
        o_ref[...] = (a_ref[...].astype(F32) + b_ref[...].astype(F32)).astype(o_ref.dtype)

    blk = pl.BlockSpec((None, tr, C), lambda j, i, p: (j, i, 0))
    gs = pltpu.PrefetchScalarGridSpec(num_scalar_prefetch=1, grid=(N_CHIPS, R // tr),
                                      in_specs=[pl.BlockSpec((None, None, tr, C), lambda j, i, p: (j, p[0], i, 0)), blk],
                                      out_specs=blk)
    return pl.pallas_call(body, grid_spec=gs, out_shape=jax.ShapeDtypeStruct((N_CHIPS, R, C), out_dtype),
                          compiler_params=_params(("parallel", "parallel")), name=name)(place, g, recv)


def add_pass_on(name, s, got, place, tr=512):
    _, R, C = s.shape
    tr = _tile(R, tr, 16)

    def body(p_ref, a_ref, b_ref, o_ref):
        o_ref[...] = (a_ref[...].astype(F32) + b_ref[...].astype(F32)).astype(o_ref.dtype)

    gs = pltpu.PrefetchScalarGridSpec(num_scalar_prefetch=1, grid=(R // tr,),
                                      in_specs=[pl.BlockSpec((None, tr, C), lambda i, p: (p[2], i, 0)),
                                                pl.BlockSpec((None, tr, C), lambda i, p: (1, i, 0))],
                                      out_specs=pl.BlockSpec((tr, C), lambda i, p: (i, 0)))
    return pl.pallas_call(body, grid_spec=gs, out_shape=jax.ShapeDtypeStruct((R, C), s.dtype),
                          compiler_params=_params(("parallel",)), name=name)(place, s, got)


def add_chips(name, g, r1, got1, got2, place, tr=512):
    _, _, R, C = g.shape
    tr = _tile(R, tr, 16)

    def body(p_ref, a_ref, b_ref, c_ref, d_ref, o_ref):
        o_ref[...] = ((a_ref[...].astype(F32) + b_ref[...].astype(F32)) + c_ref[...].astype(F32)) + d_ref[...].astype(F32)

    gs = pltpu.PrefetchScalarGridSpec(num_scalar_prefetch=1, grid=(R // tr,),
                                      in_specs=[pl.BlockSpec((None, None, tr, C), lambda i, p: (p[1], p[0], i, 0)),
                                                pl.BlockSpec((None, tr, C), lambda i, p: (p[1], i, 0)),
                                                pl.BlockSpec((None, tr, C), lambda i, p: (0, i, 0)),
                                                pl.BlockSpec((tr, C), lambda i, p: (i, 0))],
                                      out_specs=pl.BlockSpec((None, tr, C), lambda i, p: (p[0], i, 0)))
    return pl.pallas_call(body, grid_spec=gs, out_shape=jax.ShapeDtypeStruct((2, R, C), F32),
                          compiler_params=_params(("parallel",)), name=name)(place, g, r1, got1, got2)


def reduce_scatter(gs, wire_dtypes, place):
    r1 = exchange_sibling("rs_sibling", gs)
    s1 = [add_halves(f"rs_add_cores_{a}", g, r, place, dt) for a, (g, r, dt) in enumerate(zip(gs, r1, wire_dtypes))]
    got1 = exchange_first("rs_first", s1)
    t = [add_pass_on(f"rs_add_pass_{a}", s, g1, place) for a, (s, g1) in enumerate(zip(s1, got1))]
    got2 = exchange_second("rs_second", t)
    tot = [add_chips(f"rs_add_chips_{a}", g, r, g1, g2, place) for a, (g, r, g1, g2) in enumerate(zip(gs, r1, got1, got2))]
    return share_sibling("rs_share", tot)


def adamw(name, w, g, m, v, tr=512):
    shp = w.shape
    cols = shp[-1]
    w2, g2, m2, v2 = (a.reshape(-1, cols) for a in (w, g, m, v))
    rows = w2.shape[0]
    tr = _tile(rows, tr)

    def body(w_ref, g_ref, m_ref, v_ref, d_ref, mo_ref, vo_ref):
        g_ = g_ref[...]
        m_ = ADAM_B1 * m_ref[...] + (1.0 - ADAM_B1) * g_
        v_ = ADAM_B2 * v_ref[...] + (1.0 - ADAM_B2) * (g_ * g_)
        m_hat = m_ / (1.0 - ADAM_B1 ** ADAM_STEP)
        v_hat = v_ / (1.0 - ADAM_B2 ** ADAM_STEP)
        d_ref[...] = -ADAM_LR * (m_hat / (jnp.sqrt(v_hat) + ADAM_EPS) + ADAM_WD * w_ref[...])
        mo_ref[...] = m_
        vo_ref[...] = v_

    blk = pl.BlockSpec((tr, cols), lambda i: (i, 0))
    sds = jax.ShapeDtypeStruct((rows, cols), F32)
    d, mo, vo = pl.pallas_call(body, grid=(rows // tr,), in_specs=[blk] * 4, out_specs=(blk, blk, blk), out_shape=(sds, sds, sds),
                               compiler_params=_params(("parallel",)), name=name)(w2, g2, m2, v2)
    return d.reshape(shp), mo.reshape(shp), vo.reshape(shp)


PACK_ALIGN = 1024


def _pad_to(a, mult, axis=-1):
    axis = axis % a.ndim
    extra = (-a.shape[axis]) % mult
    if extra == 0:
        return a
    pads = [(0, 0)] * a.ndim
    pads[axis] = (0, extra)
    return jnp.pad(a, pads)


def _pack(pieces, lead, row_mult):
    nl = len(lead)
    flat, offs, sizes, off = [], [], [], 0
    for a in pieces:
        f = a.reshape(lead + (-1,))
        sizes.append(f.shape[-1])
        f = _pad_to(f, PACK_ALIGN)
        offs.append(off)
        off += f.shape[-1]
        flat.append(f)
    cat = _pad_to(jnp.concatenate(flat, axis=nl), 2 * row_mult * LANES)
    return cat.reshape(lead + (2, -1, LANES)), offs, sizes


def _unpack(packed, lead, offs, sizes, shapes):
    flat = packed.reshape(lead + (-1,))
    return [lax.slice_in_dim(flat, o, o + s, axis=len(lead)).reshape(lead + tuple(shp)) for o, s, shp in zip(offs, sizes, shapes)]


def _stack_grads(G, names):
    out = {}
    for n in names:
        parts = [g for g in G[n] if g is not None]
        if n in ('kv_wk', 'kv_wv'):
            out[n] = parts[0]
        elif n in REPLICATED:
            out[n] = jnp.stack(parts, axis=0).reshape(N_CHIPS, -1)
        elif n in ('a_dw_w', 'a_dw_b', 'a_ln_g', 'a_ln_b', 'a_pw2_b'):
            out[n] = _to_sm(jnp.stack(parts, axis=0), axis=-1)
        else:
            out[n] = jnp.stack(parts, axis=1)
    return out


def _whole_weights(big, small, rep, D):
    W = {}
    for n in BIG:
        a = big[n]
        W[n] = a[:, None] if n in ('kv_wk', 'kv_wv') else a
    W['a_pw1_b'] = small['a_pw1_b'][:, :, None, :]
    W['a_dw_w'] = _from_sm(small['a_dw_w'], axis=-1)
    for n in ('a_dw_b', 'a_ln_g', 'a_ln_b', 'a_pw2_b'):
        W[n] = _from_sm(small[n], axis=-1)[:, None, :]
    W['ffn_conv_w'] = small['ffn_conv_w']
    L, F = rep['ffn_conv_b'].shape
    W['ffn_conv_b'] = rep['ffn_conv_b'].reshape(L, N_CHIPS, 1, F // N_CHIPS)
    for n in ('ln_mix_g', 'ln_mix_b', 'ln_ffn_g', 'ln_ffn_b'):
        W[n] = rep[n][:, None, :]
    return W


SMALL = ('a_pw1_b', 'a_dw_w', 'a_dw_b', 'a_ln_g', 'a_ln_b', 'a_pw2_b', 'ffn_conv_w')


def _step(x, p, loss_target, w, m, v):
    S, D = x.shape[-2:]
    x2, tgt = x.reshape(S, D), loss_target.reshape(S, D)
    ax, ay, ac = lax.axis_index("x"), lax.axis_index("y"), lax.axis_index("c")
    second = jnp.where(ac == 0, 2 * ax + (1 - ay), 2 * (1 - ax) + ay)
    place = jnp.stack([ac, 2 * ax + ay, second]).astype(jnp.int32)

    big_in = [w[n].astype(BF16).reshape((2, -1) + w[n].shape[1:] if w[n].ndim == 3 else (2, -1, w[n].shape[-1])) for n in BIG]
    small_in, s_offs, s_sizes = _pack([w[n] for n in SMALL], (), 8)
    gathered = allgather_chips("gather_weights", big_in + [small_in])
    big = {n: g.reshape((N_CHIPS,) + w[n].shape) for n, g in zip(BIG, gathered[:-1])}
    small = dict(zip(SMALL, _unpack(gathered[-1], (N_CHIPS,), s_offs, s_sizes, [w[n].shape for n in SMALL])))
    W = _whole_weights(big, small, {n: w[n] for n in REPLICATED}, D)

    loss_sum, dx, G, gb = forward_backward(x2, p, tgt, W)
    loss = lax.psum(0.5 * loss_sum / D, ("x", "y", "c"))

    vectors = [n for n in WEIGHTS if n not in BIG]
    mats = [b.reshape(N_CHIPS, 2, b.shape[1] // 2, b.shape[2]) for b in gb.buf.values()]
    members = [gb.members[cols] for cols in gb.buf]
    g_sm = _stack_grads(G, vectors)
    packed, offs, sizes = _pack([g_sm[n] for n in vectors], (N_CHIPS,), 512)
    reduced = reduce_scatter(mats + [packed], [BF16] * len(mats) + [F32], place)
    shapes = [w[n].shape if n not in REPLICATED else (w[n].size // N_CHIPS,) for n in vectors]
    g_mine = dict(zip(vectors, _unpack(reduced[-1], (), offs, sizes, shapes)))
    for red, where in zip(reduced[:-1], members):
        rows = red.reshape(-1, red.shape[-1])
        for n, off, cnt in where:
            g_mine[n] = lax.slice_in_dim(rows, off, off + cnt, axis=0).reshape(w[n].shape)
    rep_in, r_offs, r_sizes = _pack([g_mine[n] for n in REPLICATED], (), 8)
    rep_all = allgather_chips("gather_replicated_grads", [rep_in])[0]
    for n, g in zip(REPLICATED, _unpack(rep_all, (N_CHIPS,), r_offs, r_sizes, [(w[n].size // N_CHIPS,) for n in REPLICATED])):
        g_mine[n] = g.reshape(w[n].shape)

    grads, deltas, new_m, new_v = [], [], [], []
    for n in WEIGHTS:
        d, mo, vo = adamw(f"adamw_{n}", w[n], g_mine[n], m[n], v[n])
        grads.append(g_mine[n])
        deltas.append(d)
        new_m.append(mo)
        new_v.append(vo)
    return (loss, dx.reshape(x.shape), *grads, *deltas, *new_m, *new_v)


def kernel(x, p, a_pw1_w, a_pw1_b, a_dw_w, a_dw_b, a_ln_g, a_ln_b, a_pw2_w, a_pw2_b, b_wq, kv_wk, kv_wv, b_wo, ln_mix_g, ln_mix_b, ffn_w_up, ffn_w_gate, ffn_conv_w, ffn_conv_b, ffn_w_down, ple_w_gate, ple_w_proj, ln_ffn_g, ln_ffn_b, loss_target, m_a_pw1_w, m_a_pw1_b, m_a_dw_w, m_a_dw_b, m_a_ln_g, m_a_ln_b, m_a_pw2_w, m_a_pw2_b, m_b_wq, m_kv_wk, m_kv_wv, m_b_wo, m_ln_mix_g, m_ln_mix_b, m_ffn_w_up, m_ffn_w_gate, m_ffn_conv_w, m_ffn_conv_b, m_ffn_w_down, m_ple_w_gate, m_ple_w_proj, m_ln_ffn_g, m_ln_ffn_b, v_a_pw1_w, v_a_pw1_b, v_a_dw_w, v_a_dw_b, v_a_ln_g, v_a_ln_b, v_a_pw2_w, v_a_pw2_b, v_b_wq, v_kv_wk, v_kv_wv, v_b_wo, v_ln_mix_g, v_ln_mix_b, v_ffn_w_up, v_ffn_w_gate, v_ffn_conv_w, v_ffn_conv_b, v_ffn_w_down, v_ple_w_gate, v_ple_w_proj, v_ln_ffn_g, v_ln_ffn_b):
    vals = dict(locals())
    w = {n: vals[n] for n in WEIGHTS}
    m = {n: vals["m_" + n] for n in WEIGHTS}
    v = {n: vals["v_" + n] for n in WEIGHTS}
    return _step(x, p, loss_target, w, m, v)
```

```python
import functools
import math

import jax
import jax.numpy as jnp
import numpy as np
from jax import lax
from jax.experimental import pallas as pl
from jax.experimental.pallas import tpu as pltpu

F32, BF16 = jnp.float32, jnp.bfloat16

HEAD_DIM = 64
LN_EPS = 1e-5
DEPTH = 4
N_A = DEPTH // 2
DN_ALPHA = (2.0 * DEPTH) ** 0.25
N_CHIPS = 4

ADAM_LR, ADAM_B1, ADAM_B2, ADAM_EPS, ADAM_WD, ADAM_STEP = 0.001, 0.9, 0.999, 1e-08, 0.01, 10

VMEM_LIMIT_BYTES = 56 * 2**20
LANES = 128
SUBLANES = 8
CONV_HALO = 32
FFN_HALO = 16

NN = (((1,), (0,)), ((), ()))
NT = (((1,), (1,)), ((), ()))
TN = (((0,), (0,)), ((), ()))

WEIGHTS = ['a_pw1_w', 'a_pw1_b', 'a_dw_w', 'a_dw_b', 'a_ln_g', 'a_ln_b', 'a_pw2_w', 'a_pw2_b', 'b_wq', 'kv_wk', 'kv_wv',
           'b_wo', 'ln_mix_g', 'ln_mix_b', 'ffn_w_up', 'ffn_w_gate', 'ffn_conv_w', 'ffn_conv_b', 'ffn_w_down', 'ple_w_gate',
           'ple_w_proj', 'ln_ffn_g', 'ln_ffn_b']
REPLICATED = ('ln_mix_g', 'ln_mix_b', 'ffn_conv_b', 'ln_ffn_g', 'ln_ffn_b')
BIG = ('a_pw1_w', 'a_pw2_w', 'b_wq', 'kv_wk', 'kv_wv', 'b_wo', 'ffn_w_up', 'ffn_w_gate', 'ffn_w_down', 'ple_w_gate',
       'ple_w_proj')


def _tile(n, pref, mult=8):
    t = min(n, pref)
    while t > 0:
        if n % t == 0 and t % mult == 0:
            return t
        t -= 1
    return n


def _params(sem):
    return pltpu.CompilerParams(dimension_semantics=sem, vmem_limit_bytes=VMEM_LIMIT_BYTES)


def _sigmoid(x):
    return 0.5 * jnp.tanh(0.5 * x) + 0.5


def _mm_call(name, grid, terms, out_spec, out_sds, dims, bias=None, res=None, res_alpha=1.0, out_scale=None, into=None):
    n_terms = len(terms)

    def body(*refs):
        o_ref = refs[-1]
        acc = None
        for t in range(n_terms):
            a = refs[2 * t][...].astype(BF16)
            b = refs[2 * t + 1][...].astype(BF16)
            d = lax.dot_general(a, b, dims, preferred_element_type=F32)
            acc = d if acc is None else acc + d
        k = 2 * n_terms
        if bias is not None:
            acc = acc + refs[k][...]
            k += 1
        if res is not None:
            acc = acc + res_alpha * refs[k][...]
        if out_scale is not None:
            acc = acc * out_scale
        o_ref[...] = acc.astype(o_ref.dtype)

    operands, specs = [], []
    for a, a_spec, b, b_spec in terms:
        operands += [a, b]
        specs += [a_spec, b_spec]
    for extra in (bias, res):
        if extra is not None:
            operands.append(extra[0])
            specs.append(extra[1])
    aliases = {}
    if into is not None:
        aliases = {len(operands): 0}
        operands.append(into)
        specs.append(pl.BlockSpec(memory_space=pl.ANY))
        out_sds = jax.ShapeDtypeStruct(into.shape, into.dtype)
    return pl.pallas_call(body, out_shape=out_sds, grid=grid, in_specs=specs, out_specs=out_spec, input_output_aliases=aliases,
                          compiler_params=_params(("parallel",) * len(grid)), name=name)(*operands)


def _mm_fanout(name, a, a_spec, w4, w_block, layer, dims, out_spec, out_sds, store, grid, bias4=None, res=None, res_alpha=1.0):
    def body(*refs):
        a_ref, w_refs, o_ref = refs[0], refs[1:1 + N_CHIPS], refs[-1]
        k = 1 + N_CHIPS
        b_refs = refs[k:k + N_CHIPS] if bias4 is not None else None
        k += N_CHIPS if bias4 is not None else 0
        av = a_ref[...].astype(BF16)
        for j in range(N_CHIPS):
            d = lax.dot_general(av, w_refs[j][...].astype(BF16), dims, preferred_element_type=F32)
            if b_refs is not None:
                d = d + b_refs[j][...]
            if res is not None:
                d = d + res_alpha * res[2](refs[k], j)
            store(o_ref, j, d)

    nd = len(grid)
    operands = [a] + [w4] * N_CHIPS
    specs = [a_spec] + [pl.BlockSpec((None, None) + w_block, lambda *g, j=j: (j, layer, 0, 0)) for j in range(N_CHIPS)]
    if bias4 is not None:
        operands += [bias4] * N_CHIPS
        specs += [pl.BlockSpec((None, None, 1, bias4.shape[-1]), lambda *g, j=j: (j, layer, 0, 0)) for j in range(N_CHIPS)]
    if res is not None:
        operands.append(res[0])
        specs.append(res[1])
    return pl.pallas_call(body, out_shape=out_sds, grid=grid, in_specs=specs, out_specs=out_spec,
                          compiler_params=_params(("parallel",) * nd), name=name)(*operands)


def _store_slot(o_ref, j, d):
    o_ref[j] = d.astype(o_ref.dtype)


def mm_colsm(name, x, w4, layer, bias4=None, out_dtype=F32, tm=1024):
    M, K = x.shape
    n = w4.shape[-1]
    tm = _tile(M, tm)
    return _mm_fanout(name, x, pl.BlockSpec((tm, K), lambda i: (i, 0)), w4, (K, n), layer, NN,
                      pl.BlockSpec((N_CHIPS, tm, n), lambda i: (0, i, 0)), jax.ShapeDtypeStruct((N_CHIPS, M, n), out_dtype),
                      _store_slot, (M // tm,), bias4=bias4)


def mm_rowsm(name, a, w4, layer, a_sm=False, bias=None, out_dtype=F32, out_scale=None, tm=512):
    kc, N = w4.shape[-2:]
    M = a.shape[-2]
    tm = _tile(M, tm)
    terms = []
    for j in range(N_CHIPS):
        if a_sm:
            a_spec = pl.BlockSpec((None, tm, kc), lambda i, j=j: (j, i, 0))
        else:
            a_spec = pl.BlockSpec((tm, kc), lambda i, j=j: (i, j))
        terms.append((a, a_spec, w4, pl.BlockSpec((None, None, kc, N), lambda i, j=j: (j, layer, 0, 0))))
    b = None if bias is None else (bias, pl.BlockSpec((None, 1, N), lambda i: (layer, 0, 0)))
    return _mm_call(name, (M // tm,), terms, pl.BlockSpec((tm, N), lambda i: (i, 0)), jax.ShapeDtypeStruct((M, N), out_dtype),
                    NN, bias=b, out_scale=out_scale)


def mm_rowsm_ln(name, a, w4, layer, x, g, b, ln_layer, a_sm=False, bias=None, tg=None, pp=None, tm=256):
    kc, N = w4.shape[-2:]
    M = a.shape[-2]
    tm = _tile(M, tm, 16)
    ple = tg is not None
    nb = bias is not None

    def body(*refs):
        a_refs, w_refs = refs[0:N_CHIPS], refs[N_CHIPS:2 * N_CHIPS]
        k = 2 * N_CHIPS
        acc = None
        for j in range(N_CHIPS):
            d = jnp.dot(a_refs[j][...].astype(BF16), w_refs[j][...].astype(BF16), preferred_element_type=F32)
            acc = d if acc is None else acc + d
        if nb:
            acc = acc + refs[k][...]
            k += 1
        r = DN_ALPHA * refs[k][...] + acc
        k += 1
        if ple:
            r = r + _sigmoid(refs[k][...]) * refs[k + 1][...]
            k += 2
        g_ref, b_ref, y_ref, yb_ref, xh_ref, rs_ref = refs[k:]
        mu = jnp.mean(r, axis=-1, keepdims=True)
        dd = r - mu
        var = jnp.mean(dd * dd, axis=-1, keepdims=True)
        rstd = lax.rsqrt(var + LN_EPS)
        xh = dd * rstd
        y = xh * g_ref[...] + b_ref[...]
        y_ref[...] = y
        yb_ref[...] = y.astype(BF16)
        xh_ref[...] = xh
        rs_ref[...] = rstd

    row = pl.BlockSpec((tm, N), lambda i: (i, 0))
    vec = pl.BlockSpec((None, 1, N), lambda i: (ln_layer, 0, 0))
    if a_sm:
        a_specs = [pl.BlockSpec((None, tm, kc), lambda i, j=j: (j, i, 0)) for j in range(N_CHIPS)]
    else:
        a_specs = [pl.BlockSpec((tm, kc), lambda i, j=j: (i, j)) for j in range(N_CHIPS)]
    w_specs = [pl.BlockSpec((None, None, kc, N), lambda i, j=j: (j, layer, 0, 0)) for j in range(N_CHIPS)]
    ins = [a] * N_CHIPS + [w4] * N_CHIPS + ([bias] if nb else []) + [x] + ([tg, pp] if ple else []) + [g, b]
    specs = a_specs + w_specs + ([pl.BlockSpec((None, 1, N), lambda i: (layer, 0, 0))] if nb else []) + [row] + ([row, row] if ple else []) + [vec, vec]
    return pl.pallas_call(body, grid=(M // tm,), in_specs=specs,
                          out_specs=(row, row, row, pl.BlockSpec((tm, 1), lambda i: (i, 0))),
                          out_shape=(jax.ShapeDtypeStruct((M, N), F32), jax.ShapeDtypeStruct((M, N), BF16),
                                     jax.ShapeDtypeStruct((M, N), F32), jax.ShapeDtypeStruct((M, 1), F32)),
                          compiler_params=_params(("parallel",)), name=name)(*ins)


def mm_pw1_glu(name, xb, w4, layer, bias4, tm=512):
    M, K = xb.shape
    n = w4.shape[-1]
    half = N_CHIPS // 2
    tm = _tile(M, tm)

    def body(*refs):
        x_ref, w_refs, b_refs = refs[0], refs[1:1 + N_CHIPS], refs[1 + N_CHIPS:1 + 2 * N_CHIPS]
        h_ref, u_ref = refs[-2:]
        xv = x_ref[...].astype(BF16)
        for j in range(N_CHIPS):
            h_ref[j] = jnp.dot(xv, w_refs[j][...].astype(BF16), preferred_element_type=F32) + b_refs[j][...]
        for j in range(half):
            u_ref[:, j * n:(j + 1) * n] = h_ref[j] * _sigmoid(h_ref[j + half])

    specs = [pl.BlockSpec((tm, K), lambda i: (i, 0))]
    specs += [pl.BlockSpec((None, None, K, n), lambda i, j=j: (j, layer, 0, 0)) for j in range(N_CHIPS)]
    specs += [pl.BlockSpec((None, None, 1, n), lambda i, j=j: (j, layer, 0, 0)) for j in range(N_CHIPS)]
    return pl.pallas_call(body, grid=(M // tm,), in_specs=specs,
                          out_specs=(pl.BlockSpec((N_CHIPS, tm, n), lambda i: (0, i, 0)), pl.BlockSpec((tm, half * n), lambda i: (i, 0))),
                          out_shape=(jax.ShapeDtypeStruct((N_CHIPS, M, n), F32), jax.ShapeDtypeStruct((M, half * n), F32)),
                          compiler_params=_params(("parallel",)), name=name)(xb, *([w4] * N_CHIPS), *([bias4] * N_CHIPS))


def mm_nt_rowsm(name, dy, w4, layer, out_sm=False, res=None, res_alpha=1.0, out_dtype=F32, tm=1024):
    kc, N = w4.shape[-2:]
    M = dy.shape[0]
    tm = _tile(M, tm)

    def store_cols(o_ref, j, d):
        o_ref[:, j * kc:(j + 1) * kc] = d

    if out_sm:
        out_spec, sds, store = pl.BlockSpec((N_CHIPS, tm, kc), lambda i: (0, i, 0)), jax.ShapeDtypeStruct((N_CHIPS, M, kc), out_dtype), _store_slot
    else:
        out_spec, sds, store = pl.BlockSpec((tm, N_CHIPS * kc), lambda i: (i, 0)), jax.ShapeDtypeStruct((M, N_CHIPS * kc), F32), store_cols
    r = None if res is None else (res, pl.BlockSpec((tm, N_CHIPS * kc), lambda i: (i, 0)), lambda ref, j: ref[:, j * kc:(j + 1) * kc])
    return _mm_fanout(name, dy, pl.BlockSpec((tm, N), lambda i: (i, 0)), w4, (kc, N), layer, NT, out_spec, sds, store, (M // tm,),
                      res=r, res_alpha=res_alpha)


def mm_nt_colsm(name, dy_parts, w4, layer, res=None, res_alpha=1.0, tm=512):
    K, n = w4.shape[-2:]
    M = dy_parts[0][0].shape[1]
    tm = _tile(M, tm)
    terms = [(arr, pl.BlockSpec((None, tm, n), lambda i, idx=idx: (idx, i, 0)), w4,
              pl.BlockSpec((None, None, K, n), lambda i, j=j: (j, layer, 0, 0))) for j, (arr, idx) in enumerate(dy_parts)]
    r = None if res is None else (res, pl.BlockSpec((tm, K), lambda i: (i, 0)))
    return _mm_call(name, (M // tm,), terms, pl.BlockSpec((tm, K), lambda i: (i, 0)), jax.ShapeDtypeStruct((M, K), F32), NT,
                    res=r, res_alpha=res_alpha)


def _sm_parts(a):
    return [(a, j) for j in range(a.shape[0])]


def mm_tn_colsm(name, x, dy, into, off, j0=0, tk=512):
    M, K = x.shape
    nj, _, n = dy.shape
    tk = _tile(K, tk, LANES)
    assert off % tk == 0
    terms = [(x, pl.BlockSpec((M, tk), lambda j, k: (0, k)), dy, pl.BlockSpec((None, M, n), lambda j, k: (j, 0, 0)))]
    return _mm_call(name, (nj, K // tk), terms, pl.BlockSpec((None, tk, n), lambda j, k: (j + j0, off // tk + k, 0)),
                    None, TN, into=into)


def mm_tn_rowsm(name, a, dy, into, off, a_sm=False, tn=512):
    M, N = dy.shape
    kc = a.shape[-1] if a_sm else a.shape[-1] // N_CHIPS
    tn = _tile(N, tn, LANES)
    assert off % kc == 0
    a_spec = pl.BlockSpec((None, M, kc), lambda j, n: (j, 0, 0)) if a_sm else pl.BlockSpec((M, kc), lambda j, n: (0, j))
    terms = [(a, a_spec, dy, pl.BlockSpec((M, tn), lambda j, n: (0, n)))]
    return _mm_call(name, (N_CHIPS, N // tn), terms, pl.BlockSpec((None, kc, tn), lambda j, n: (j, off // kc, n)),
                    None, TN, into=into)


def mm_tn_rowsm_fan(name, a, dy, into, off, tn=256):
    M, N = dy.shape
    kc = a.shape[-1] // N_CHIPS
    tn = _tile(N, tn, LANES)
    assert off % kc == 0

    def body(a_ref, dy_ref, into_ref, o_ref):
        dyb = dy_ref[...].astype(BF16)
        for j in range(N_CHIPS):
            aj = a_ref[:, j * kc:(j + 1) * kc].astype(BF16)
            o_ref[j] = lax.dot_general(aj, dyb, TN, preferred_element_type=F32).astype(o_ref.dtype)

    return pl.pallas_call(body, out_shape=jax.ShapeDtypeStruct(into.shape, into.dtype), grid=(N // tn,),
                          in_specs=[pl.BlockSpec((M, N_CHIPS * kc), lambda n: (0, 0)), pl.BlockSpec((M, tn), lambda n: (0, n)),
                                    pl.BlockSpec(memory_space=pl.ANY)],
                          out_specs=pl.BlockSpec((N_CHIPS, kc, tn), lambda n: (0, off // kc, n)), input_output_aliases={2: 0},
                          compiler_params=_params(("parallel",)), name=name)(a, dy, into)


def mm_proj(name, p4, layer, w4, tm=512):
    S, P = p4.shape[-2:]
    n = w4.shape[-1]
    tm = _tile(S, tm)

    def store_cols(o_ref, j, d):
        o_ref[:, j * n:(j + 1) * n] = d

    return _mm_fanout(name, p4, pl.BlockSpec((None, None, tm, P), lambda i: (layer, 0, i, 0)), w4, (P, n), layer, NN,
                      pl.BlockSpec((tm, N_CHIPS * n), lambda i: (i, 0)), jax.ShapeDtypeStruct((S, N_CHIPS * n), F32),
                      store_cols, (S // tm,))


def mm_tn_proj(name, p4, layer, dpp, into, off):
    S, P = p4.shape[-2:]
    n = dpp.shape[-1] // N_CHIPS
    assert off % P == 0
    terms = [(p4, pl.BlockSpec((None, None, S, P), lambda j: (layer, 0, 0, 0)), dpp, pl.BlockSpec((S, n), lambda j: (0, j)))]
    return _mm_call(name, (N_CHIPS,), terms, pl.BlockSpec((None, P, n), lambda j: (j, off // P, 0)), None, TN, into=into)


def ln_fwd(name, x, mix, g, b, layer, tg=None, pp=None, tm=512):
    S, D = x.shape
    tm = _tile(S, tm, 16)
    ple = tg is not None

    def body(*refs):
        if ple:
            x_ref, m_ref, tg_ref, pp_ref, g_ref, b_ref, y_ref, yb_ref, xh_ref, rs_ref = refs
        else:
            x_ref, m_ref, g_ref, b_ref, y_ref, yb_ref, xh_ref, rs_ref = refs
        r = DN_ALPHA * x_ref[...] + m_ref[...]
        if ple:
            r = r + _sigmoid(tg_ref[...]) * pp_ref[...]
        mu = jnp.mean(r, axis=-1, keepdims=True)
        d = r - mu
        var = jnp.mean(d * d, axis=-1, keepdims=True)
        rstd = lax.rsqrt(var + LN_EPS)
        xh = d * rstd
        y = xh * g_ref[...] + b_ref[...]
        y_ref[...] = y
        yb_ref[...] = y.astype(BF16)
        xh_ref[...] = xh
        rs_ref[...] = rstd

    row = pl.BlockSpec((tm, D), lambda i: (i, 0))
    vec = pl.BlockSpec((None, 1, D), lambda i: (layer, 0, 0))
    ins = [x, mix] + ([tg, pp] if ple else []) + [g, b]
    specs = [row, row] + ([row, row] if ple else []) + [vec, vec]
    return pl.pallas_call(body, grid=(S // tm,), in_specs=specs,
                          out_specs=(row, row, row, pl.BlockSpec((tm, 1), lambda i: (i, 0))),
                          out_shape=(jax.ShapeDtypeStruct((S, D), F32), jax.ShapeDtypeStruct((S, D), BF16),
                                     jax.ShapeDtypeStruct((S, D), F32), jax.ShapeDtypeStruct((S, 1), F32)),
                          compiler_params=_params(("parallel",)), name=name)(*ins)


def ln_bwd(name, dy, xh, rstd, g, layer, tg=None, pp=None, tm=512):
    S, D = dy.shape
    tm = _tile(S, tm)
    ple = tg is not None

    def body(*refs):
        if ple:
            dy_ref, xh_ref, rs_ref, g_ref, tg_ref, pp_ref, dr_ref, acc_ref, dtg_ref, dpp_ref = refs
        else:
            dy_ref, xh_ref, rs_ref, g_ref, dr_ref, acc_ref = refs
        dy_, xh_ = dy_ref[...], xh_ref[...]
        dxh = dy_ * g_ref[...]
        m1 = jnp.mean(dxh, axis=-1, keepdims=True)
        m2 = jnp.mean(dxh * xh_, axis=-1, keepdims=True)
        dr = rs_ref[...] * (dxh - m1 - xh_ * m2)
        dr_ref[...] = dr

        @pl.when(pl.program_id(0) == 0)
        def _():
            acc_ref[...] = jnp.zeros_like(acc_ref)
        acc_ref[0:1, :] += jnp.sum(dy_ * xh_, axis=0, keepdims=True)
        acc_ref[1:2, :] += jnp.sum(dy_, axis=0, keepdims=True)
        acc_ref[2:3, :] += jnp.sum(dr, axis=0, keepdims=True)
        if ple:
            pg = _sigmoid(tg_ref[...])
            dtg_ref[...] = (dr * pp_ref[...] * pg * (1.0 - pg)).astype(BF16)
            dpp_ref[...] = (dr * pg).astype(BF16)

    row = pl.BlockSpec((tm, D), lambda i: (i, 0))
    ins = [dy, xh, rstd, g] + ([tg, pp] if ple else [])
    specs = [row, row, pl.BlockSpec((tm, 1), lambda i: (i, 0)), pl.BlockSpec((None, 1, D), lambda i: (layer, 0, 0))] + ([row, row] if ple else [])
    outs = [jax.ShapeDtypeStruct((S, D), F32), jax.ShapeDtypeStruct((8, D), F32)]
    out_specs = [row, pl.BlockSpec((8, D), lambda i: (0, 0))]
    if ple:
        outs += [jax.ShapeDtypeStruct((S, D), BF16)] * 2
        out_specs += [row, row]
    return pl.pallas_call(body, grid=(S // tm,), in_specs=specs, out_specs=tuple(out_specs), out_shape=tuple(outs),
                          compiler_params=_params(("arbitrary",)), name=name)(*ins)


def glu_fwd(name, h_sm, tm=512):
    _, S, n = h_sm.shape
    tm = _tile(S, tm)
    half = N_CHIPS // 2

    def body(a_ref, g_ref, u_ref):
        u_ref[...] = a_ref[...] * _sigmoid(g_ref[...])

    return pl.pallas_call(body, grid=(half, S // tm),
                          in_specs=[pl.BlockSpec((None, tm, n), lambda j, i: (j, i, 0)),
                                    pl.BlockSpec((None, tm, n), lambda j, i: (j + half, i, 0))],
                          out_specs=pl.BlockSpec((tm, n), lambda j, i: (i, j)),
                          out_shape=jax.ShapeDtypeStruct((S, half * n), F32),
                          compiler_params=_params(("parallel", "parallel")), name=name)(h_sm, h_sm)


def conv_ln_silu_fwd(name, u, w, b, g, beta, layer, ts=128):
    S, D = u.shape
    kw = w.shape[1]
    ts = _tile(S, ts, CONV_HALO)
    lc = LANES if D % LANES == 0 else D

    def body(h_ref, u_ref, w_ref, b_ref, g_ref, be_ref, c_ref, s_ref, win_ref, rot_ref):
        i = pl.program_id(0)
        win_ref[0:CONV_HALO, :] = jnp.where(i == 0, 0.0, h_ref[...])
        win_ref[CONV_HALO:, :] = u_ref[...]
        for cc in range(D // lc):
            cs = slice(cc * lc, (cc + 1) * lc)
            for sub in range(1, SUBLANES):
                rot_ref[sub] = win_ref[sub:sub + ts + CONV_HALO - SUBLANES, cs]
            acc = jnp.zeros((ts, lc), F32) + b_ref[:, cs]
            for k in range(kw):
                whole, sub = divmod(CONV_HALO - (kw - 1) + k, SUBLANES)
                r0 = whole * SUBLANES
                acc = acc + w_ref[k:k + 1, cs] * (win_ref[r0:r0 + ts, cs] if sub == 0 else rot_ref[sub, r0:r0 + ts, :])
            c_ref[:, cs] = acc
        c = c_ref[...]
        mu = jnp.mean(c, axis=-1, keepdims=True)
        d = c - mu
        var = jnp.mean(d * d, axis=-1, keepdims=True)
        nrm = d * lax.rsqrt(var + LN_EPS) * g_ref[...] + be_ref[...]
        s_ref[...] = (nrm * _sigmoid(nrm)).astype(BF16)

    row = pl.BlockSpec((ts, D), lambda i: (i, 0))
    vec = pl.BlockSpec((None, 1, D), lambda i: (layer, 0, 0))
    halo = pl.BlockSpec((CONV_HALO, D), lambda i: (jnp.maximum(i * (ts // CONV_HALO) - 1, 0), 0))
    return pl.pallas_call(body, grid=(S // ts,),
                          in_specs=[halo, row, pl.BlockSpec((None, kw, D), lambda i: (layer, 0, 0)), vec, vec, vec],
                          out_specs=(row, row),
                          out_shape=(jax.ShapeDtypeStruct((S, D), F32), jax.ShapeDtypeStruct((S, D), BF16)),
                          scratch_shapes=[pltpu.VMEM((ts + CONV_HALO, D), F32),
                                          pltpu.VMEM((SUBLANES, ts + CONV_HALO - SUBLANES, lc), F32)],
                          compiler_params=_params(("parallel",)), name=name)(u, u, w, b, g, beta)


def ln_silu_bwd(name, ds, c, g, beta, layer, tm=256):
    S, D = c.shape
    tm = _tile(S, tm)

    def body(ds_ref, c_ref, g_ref, be_ref, dc_ref, acc_ref):
        c_ = c_ref[...]
        mu = jnp.mean(c_, axis=-1, keepdims=True)
        d = c_ - mu
        var = jnp.mean(d * d, axis=-1, keepdims=True)
        rstd = lax.rsqrt(var + LN_EPS)
        xh = d * rstd
        nrm = xh * g_ref[...] + be_ref[...]
        sg = _sigmoid(nrm)
        dn = ds_ref[...] * (sg * (1.0 + nrm * (1.0 - sg)))
        dxh = dn * g_ref[...]
        m1 = jnp.mean(dxh, axis=-1, keepdims=True)
        m2 = jnp.mean(dxh * xh, axis=-1, keepdims=True)
        dc = rstd * (dxh - m1 - xh * m2)
        dc_ref[...] = dc

        @pl.when(pl.program_id(0) == 0)
        def _():
            acc_ref[...] = jnp.zeros_like(acc_ref)
        acc_ref[0:1, :] += jnp.sum(dn * xh, axis=0, keepdims=True)
        acc_ref[1:2, :] += jnp.sum(dn, axis=0, keepdims=True)
        acc_ref[2:3, :] += jnp.sum(dc, axis=0, keepdims=True)

    row = pl.BlockSpec((tm, D), lambda i: (i, 0))
    vec = pl.BlockSpec((None, 1, D), lambda i: (layer, 0, 0))
    return pl.pallas_call(body, grid=(S // tm,), in_specs=[row, row, vec, vec],
                          out_specs=(row, pl.BlockSpec((8, D), lambda i: (0, 0))),
                          out_shape=(jax.ShapeDtypeStruct((S, D), F32), jax.ShapeDtypeStruct((8, D), F32)),
                          compiler_params=_params(("arbitrary",)), name=name)(ds, c, g, beta)


def conv_glu_bwd(name, dc, u, h_sm, w, layer, ts=128):
    S, D = dc.shape
    kw = w.shape[1]
    half = N_CHIPS // 2
    n = D // half
    ts = _tile(S, ts, CONV_HALO)
    nblk = S // ts
    lc = LANES if n % LANES == 0 else n

    def body(dc_ref, dcn_ref, u_ref, a_ref, g_ref, w_ref, da_ref, dg_ref, dw_ref, dba_ref, dbg_ref, dwin_ref, rot_ref):
        i = pl.program_id(1)
        dwin_ref[0:ts, :] = dc_ref[...]
        dwin_ref[ts:, :] = jnp.where(i == nblk - 1, 0.0, dcn_ref[...])

        @pl.when(i == 0)
        def _():
            dw_ref[...] = jnp.zeros_like(dw_ref)
            dba_ref[...] = jnp.zeros_like(dba_ref)
            dbg_ref[...] = jnp.zeros_like(dbg_ref)

        for cc in range(n // lc):
            cs = slice(cc * lc, (cc + 1) * lc)
            for sub in range(1, SUBLANES):
                rot_ref[sub] = dwin_ref[sub:sub + ts + CONV_HALO - SUBLANES, cs]
            ub = u_ref[:, cs]
            du = jnp.zeros((ts, lc), F32)
            for k in range(kw):
                whole, sub = divmod(kw - 1 - k, SUBLANES)
                r0 = whole * SUBLANES
                shifted = dwin_ref[r0:r0 + ts, cs] if sub == 0 else rot_ref[sub, r0:r0 + ts, :]
                du = du + w_ref[k:k + 1, cs] * shifted
                dw_ref[k:k + 1, cs] += jnp.sum(shifted * ub, axis=0, keepdims=True)
            a = a_ref[:, cs]
            sg = _sigmoid(g_ref[:, cs])
            da = du * sg
            dg = du * a * sg * (1.0 - sg)
            da_ref[:, cs] = da.astype(BF16)
            dg_ref[:, cs] = dg.astype(BF16)
            dba_ref[0:1, cs] += jnp.sum(da, axis=0, keepdims=True)
            dbg_ref[0:1, cs] += jnp.sum(dg, axis=0, keepdims=True)

    r = ts // CONV_HALO
    main = pl.BlockSpec((ts, n), lambda j, i: (i, j))
    nxt = pl.BlockSpec((CONV_HALO, n), lambda j, i: (jnp.minimum((i + 1) * r, S // CONV_HALO - 1), j))
    sm_a = pl.BlockSpec((None, ts, n), lambda j, i: (j, i, 0))
    sm_g = pl.BlockSpec((None, ts, n), lambda j, i: (j + half, i, 0))
    da, dg, dw, dba, dbg = pl.pallas_call(
        body, grid=(half, nblk),
        in_specs=[main, nxt, main, sm_a, sm_g, pl.BlockSpec((None, kw, n), lambda j, i: (layer, 0, j))],
        out_specs=(pl.BlockSpec((None, ts, n), lambda j, i: (j, i, 0)), pl.BlockSpec((None, ts, n), lambda j, i: (j, i, 0)),
                   pl.BlockSpec((None, 32, n), lambda j, i: (j, 0, 0)),
                   pl.BlockSpec((None, 8, n), lambda j, i: (j, 0, 0)), pl.BlockSpec((None, 8, n), lambda j, i: (j, 0, 0))),
        out_shape=(jax.ShapeDtypeStruct((half, S, n), BF16), jax.ShapeDtypeStruct((half, S, n), BF16),
                   jax.ShapeDtypeStruct((half, 32, n), F32),
                   jax.ShapeDtypeStruct((half, 8, n), F32), jax.ShapeDtypeStruct((half, 8, n), F32)),
        scratch_shapes=[pltpu.VMEM((ts + CONV_HALO, n), F32), pltpu.VMEM((SUBLANES, ts + CONV_HALO - SUBLANES, lc), F32)],
        compiler_params=_params(("parallel", "arbitrary")), name=name)(dc, dc, u, h_sm, h_sm, w)
    return da, dg, dw, dba, dbg


ROW_CHUNK = 16


def _ffn_gc(win_ref, w_ref, b_ref, r0, rows, kw, base):
    gc = b_ref[...] + jnp.zeros((rows, win_ref.shape[1]), F32)
    for k in range(kw):
        off = r0 + base - (kw - 1) + k
        gc = gc + w_ref[k:k + 1, :] * win_ref[off:off + rows, :]
    return gc


def ffn_gate_fwd(name, up_sm, gp_sm, w4, b4, layer, ts=256):
    _, S, n = up_sm.shape
    kw = w4.shape[2]
    ts = _tile(S, ts, ROW_CHUNK)
    rc = ROW_CHUNK

    def body(up_ref, gp_ref, gph_ref, w_ref, b_ref, hf_ref, gc_ref, win_ref):
        i = pl.program_id(1)
        win_ref[0:FFN_HALO, :] = jnp.where(i == 0, 0.0, gph_ref[...].astype(F32))
        win_ref[FFN_HALO:, :] = gp_ref[...].astype(F32)
        for r0 in range(0, ts, rc):
            gc = _ffn_gc(win_ref, w_ref, b_ref, r0, rc, kw, FFN_HALO)
            gc_ref[r0:r0 + rc, :] = gc.astype(BF16)
            hf_ref[r0:r0 + rc, :] = (gc * _sigmoid(gc) * up_ref[r0:r0 + rc, :].astype(F32)).astype(BF16)

    main = pl.BlockSpec((None, ts, n), lambda j, i: (j, i, 0))
    prv = pl.BlockSpec((None, FFN_HALO, n), lambda j, i: (j, jnp.maximum(i * (ts // FFN_HALO) - 1, 0), 0))
    sds = jax.ShapeDtypeStruct((N_CHIPS, S, n), BF16)
    return pl.pallas_call(body, grid=(N_CHIPS, S // ts),
                          in_specs=[main, main, prv, pl.BlockSpec((None, None, kw, n), lambda j, i: (j, layer, 0, 0)),
                                    pl.BlockSpec((None, None, 1, n), lambda j, i: (layer, j, 0, 0))],
                          out_specs=(main, main), out_shape=(sds, sds),
                          scratch_shapes=[pltpu.VMEM((ts + FFN_HALO, n), F32)],
                          compiler_params=_params(("parallel", "parallel")), name=name)(up_sm, gp_sm, gp_sm, w4, b4)


def ffn_gate_bwd_a(name, dhf_sm, up_sm, gc_sm, ts=256):
    _, S, n = up_sm.shape
    ts = _tile(S, ts, ROW_CHUNK)
    rc = ROW_CHUNK

    def body(dhf_ref, up_ref, gc_ref, dup_ref, dgc_ref):
        for r0 in range(0, ts, rc):
            rows = slice(r0, r0 + rc)
            gc = gc_ref[rows, :].astype(F32)
            sg = _sigmoid(gc)
            dhf = dhf_ref[rows, :].astype(F32)
            dup_ref[rows, :] = (dhf * gc * sg).astype(BF16)
            dgc_ref[rows, :] = (dhf * up_ref[rows, :].astype(F32) * (sg * (1.0 + gc * (1.0 - sg)))).astype(BF16)

    main = pl.BlockSpec((None, ts, n), lambda j, i: (j, i, 0))
    sds = jax.ShapeDtypeStruct((N_CHIPS, S, n), BF16)
    return pl.pallas_call(body, grid=(N_CHIPS, S // ts), in_specs=[main, main, main], out_specs=(main, main), out_shape=(sds, sds),
                          compiler_params=_params(("parallel", "parallel")), name=name)(dhf_sm, up_sm, gc_sm)


def ffn_up_gate_fwd(name, xb, w_up, w_gate, cw4, cb4, layer, tm=256):
    S, K = xb.shape
    n = w_up.shape[-1]
    kw = cw4.shape[2]
    tm = _tile(S, tm, ROW_CHUNK)
    rc = ROW_CHUNK

    def body(*refs):
        x_ref, xh_ref = refs[0:2]
        wu, wg, cw, cb = refs[2:6], refs[6:10], refs[10:14], refs[14:18]
        up_ref, gp_ref, gc_ref, hf_ref, win_ref = refs[18:]
        i = pl.program_id(0)
        xv = x_ref[...]
        xhalo = xh_ref[...]
        for j in range(N_CHIPS):
            wgj = wg[j][...]
            gp = jnp.dot(xv, wgj, preferred_element_type=F32)
            up = jnp.dot(xv, wu[j][...], preferred_element_type=F32)
            gph = jnp.dot(xhalo, wgj, preferred_element_type=F32)
            gpb = gp.astype(BF16)
            gp_ref[j] = gpb
            up_ref[j] = up.astype(BF16)
            win_ref[0:FFN_HALO, :] = jnp.where(i == 0, 0.0, gph.astype(BF16).astype(F32))
            win_ref[FFN_HALO:, :] = gpb.astype(F32)
            for r0 in range(0, tm, rc):
                gc = _ffn_gc(win_ref, cw[j], cb[j], r0, rc, kw, FFN_HALO)
                gc_ref[j, r0:r0 + rc, :] = gc.astype(BF16)
                hf_ref[j, r0:r0 + rc, :] = (gc * _sigmoid(gc) * up_ref[j, r0:r0 + rc, :].astype(F32)).astype(BF16)

    r = tm // FFN_HALO
    out = pl.BlockSpec((N_CHIPS, tm, n), lambda i: (0, i, 0))
    sds = jax.ShapeDtypeStruct((N_CHIPS, S, n), BF16)
    specs = [pl.BlockSpec((tm, K), lambda i: (i, 0)), pl.BlockSpec((FFN_HALO, K), lambda i: (jnp.maximum(i * r - 1, 0), 0))]
    specs += [pl.BlockSpec((None, None, K, n), lambda i, j=j: (j, layer, 0, 0)) for j in range(N_CHIPS)] * 2
    specs += [pl.BlockSpec((None, None, kw, n), lambda i, j=j: (j, layer, 0, 0)) for j in range(N_CHIPS)]
    specs += [pl.BlockSpec((None, None, 1, n), lambda i, j=j: (layer, j, 0, 0)) for j in range(N_CHIPS)]
    return pl.pallas_call(body, grid=(S // tm,), in_specs=specs, out_specs=(out, out, out, out), out_shape=(sds, sds, sds, sds),
                          scratch_shapes=[pltpu.VMEM((tm + FFN_HALO, n), F32)],
                          compiler_params=_params(("parallel",)), name=name)(
                              xb, xb, *([w_up] * N_CHIPS), *([w_gate] * N_CHIPS), *([cw4] * N_CHIPS), *([cb4] * N_CHIPS))


def ffn_gate_conv_bwd(name, dr, w_down, cw4, layer, up_sm, gc_sm, gp_sm, tm=256):
    n, N = w_down.shape[-2:]
    S = dr.shape[0]
    kw = cw4.shape[2]
    tm = _tile(S, tm, ROW_CHUNK)
    rc = ROW_CHUNK
    nblk = S // tm

    def dgc_of(dhf, up, gc):
        sg = _sigmoid(gc)
        return dhf * up * (sg * (1.0 + gc * (1.0 - sg))), sg

    def body(*refs):
        dr_ref, drn_ref = refs[0:2]
        wd, cw = refs[2:6], refs[6:10]
        up_ref, gc_ref, gp_ref, upn_ref, gcn_ref, dup_ref, dgp_ref, acc_ref, win_ref = refs[10:]
        i = pl.program_id(0)

        @pl.when(i == 0)
        def _():
            acc_ref[...] = jnp.zeros_like(acc_ref)
        a = dr_ref[...].astype(BF16)
        an = drn_ref[...].astype(BF16)
        for j in range(N_CHIPS):
            wj = wd[j][...]
            dhf = lax.dot_general(a, wj, NT, preferred_element_type=F32)
            gc = gc_ref[j].astype(F32)
            dgc, sg = dgc_of(dhf, up_ref[j].astype(F32), gc)
            dup_ref[j] = (dhf * gc * sg).astype(BF16)
            win_ref[0:tm, :] = dgc
            dhfn = lax.dot_general(an, wj, NT, preferred_element_type=F32)
            dgcn, _ = dgc_of(dhfn, upn_ref[j].astype(F32), gcn_ref[j].astype(F32))
            win_ref[tm:, :] = jnp.where(i == nblk - 1, 0.0, dgcn)
            sums = [jnp.zeros((8, n), F32) for _ in range(kw + 1)]
            for r0 in range(0, tm, rc):
                gp = gp_ref[j, r0:r0 + rc, :].astype(F32)
                dgp = jnp.zeros((rc, n), F32)
                for k in range(kw):
                    d = kw - 1 - k
                    shifted = win_ref[r0 + d:r0 + d + rc, :]
                    dgp = dgp + cw[j][k:k + 1, :] * shifted
                    prod = shifted * gp
                    sums[k] = sums[k] + prod[0:8, :] + prod[8:16, :]
                    if d == 0:
                        sums[kw] = sums[kw] + shifted[0:8, :] + shifted[8:16, :]
                dgp_ref[j, r0:r0 + rc, :] = dgp.astype(BF16)
            for k in range(kw):
                acc_ref[j, k:k + 1, :] += jnp.sum(sums[k], axis=0, keepdims=True)
            acc_ref[j, 7:8, :] += jnp.sum(sums[kw], axis=0, keepdims=True)

    r = tm // FFN_HALO
    nxt_row = lambda i: jnp.minimum((i + 1) * r, S // FFN_HALO - 1)
    blk = pl.BlockSpec((N_CHIPS, tm, n), lambda i: (0, i, 0))
    halo = pl.BlockSpec((N_CHIPS, FFN_HALO, n), lambda i: (0, nxt_row(i), 0))
    sds = jax.ShapeDtypeStruct((N_CHIPS, S, n), BF16)
    specs = [pl.BlockSpec((tm, N), lambda i: (i, 0)), pl.BlockSpec((FFN_HALO, N), lambda i: (nxt_row(i), 0))]
    specs += [pl.BlockSpec((None, None, n, N), lambda i, j=j: (j, layer, 0, 0)) for j in range(N_CHIPS)]
    specs += [pl.BlockSpec((None, None, kw, n), lambda i, j=j: (j, layer, 0, 0)) for j in range(N_CHIPS)]
    specs += [blk, blk, blk, halo, halo]
    return pl.pallas_call(body, grid=(nblk,), in_specs=specs,
                          out_specs=(blk, blk, pl.BlockSpec((N_CHIPS, 8, n), lambda i: (0, 0, 0))),
                          out_shape=(sds, sds, jax.ShapeDtypeStruct((N_CHIPS, 8, n), F32)),
                          scratch_shapes=[pltpu.VMEM((tm + FFN_HALO, n), F32)],
                          compiler_params=_params(("arbitrary",)), name=name)(
                              dr, dr, *([w_down] * N_CHIPS), *([cw4] * N_CHIPS), up_sm, gc_sm, gp_sm, up_sm, gc_sm)


def ffn_gate_bwd_b(name, dgc_sm, gp_sm, w4, layer, ts=256):
    _, S, n = gp_sm.shape
    kw = w4.shape[2]
    ts = _tile(S, ts, ROW_CHUNK)
    rc = ROW_CHUNK
    nblk = S // ts

    def body(dgc_ref, dgn_ref, gp_ref, w_ref, dgp_ref, acc_ref, dwin_ref):
        i = pl.program_id(1)
        dwin_ref[0:ts, :] = dgc_ref[...].astype(F32)
        dwin_ref[ts:, :] = jnp.where(i == nblk - 1, 0.0, dgn_ref[...].astype(F32))

        @pl.when(i == 0)
        def _():
            acc_ref[...] = jnp.zeros_like(acc_ref)
        sums = [jnp.zeros((8, n), F32) for _ in range(kw + 1)]
        for r0 in range(0, ts, rc):
            gp = gp_ref[r0:r0 + rc, :].astype(F32)
            dgp = jnp.zeros((rc, n), F32)
            for k in range(kw):
                d = kw - 1 - k
                shifted = dwin_ref[r0 + d:r0 + d + rc, :]
                dgp = dgp + w_ref[k:k + 1, :] * shifted
                prod = shifted * gp
                sums[k] = sums[k] + prod[0:8, :] + prod[8:16, :]
                if d == 0:
                    sums[kw] = sums[kw] + shifted[0:8, :] + shifted[8:16, :]
            dgp_ref[r0:r0 + rc, :] = dgp.astype(BF16)
        for k in range(kw):
            acc_ref[k:k + 1, :] += jnp.sum(sums[k], axis=0, keepdims=True)
        acc_ref[7:8, :] += jnp.sum(sums[kw], axis=0, keepdims=True)

    r = ts // FFN_HALO
    main = pl.BlockSpec((None, ts, n), lambda j, i: (j, i, 0))
    nxt = pl.BlockSpec((None, FFN_HALO, n), lambda j, i: (j, jnp.minimum((i + 1) * r, S // FFN_HALO - 1), 0))
    return pl.pallas_call(body, grid=(N_CHIPS, nblk),
                          in_specs=[main, nxt, main, pl.BlockSpec((None, None, kw, n), lambda j, i: (j, layer, 0, 0))],
                          out_specs=(main, pl.BlockSpec((None, 8, n), lambda j, i: (j, 0, 0))),
                          out_shape=(jax.ShapeDtypeStruct((N_CHIPS, S, n), BF16), jax.ShapeDtypeStruct((N_CHIPS, 8, n), F32)),
                          scratch_shapes=[pltpu.VMEM((ts + FFN_HALO, n), F32)],
                          compiler_params=_params(("parallel", "arbitrary")), name=name)(dgc_sm, dgc_sm, gp_sm, w4)


def _neg_softplus(z):
    e = jnp.exp(-jnp.abs(z))
    return -(jnp.maximum(z, 0.0) + jnp.log(1.0 + e)), e


def _split_dot(x, t):
    hi = x.astype(BF16)
    lo = (x - hi.astype(F32)).astype(BF16)
    return jnp.dot(hi, t, preferred_element_type=F32) + jnp.dot(lo, t, preferred_element_type=F32)


STICK_GONE = -100.0
NOT_SWEPT = -1e30


def attn_fwd(name, q, k, v, bq=512, w=256):
    S, D = q.shape
    dh = HEAD_DIM
    hpb = LANES // dh
    bq = _tile(S, bq)
    w = _tile(bq, w)
    nsub = bq // w
    nkb = S // w

    def body(q_ref, k_ref, v_ref, o_ref, runs_ref, rs_ref):
        qi = pl.program_id(1)
        rr = lax.broadcasted_iota(jnp.int32, (w, w), 0)
        cc = lax.broadcasted_iota(jnp.int32, (w, w), 1)
        t_suf = (rr >= cc).astype(BF16)
        tq = qi * bq + lax.broadcasted_iota(jnp.int32, (bq, w), 0)
        tk = lax.broadcasted_iota(jnp.int32, (bq, w), 1)
        lane = lax.broadcasted_iota(jnp.int32, (bq, LANES), 1)
        ntot = (qi + 1) * nsub
        heads = [slice(hh * dh, (hh + 1) * dh) for hh in range(hpb)]
        qbs = [q_ref[:, hs] for hs in heads]
        for hh in range(hpb):
            rs_ref[hh] = jnp.where(lane < ntot, NOT_SWEPT, 0.0)

        def block(kb, carry, masked):
            kstart = pl.multiple_of(kb * w, w)
            if masked:
                m = (tk + kstart) < tq
            out = []
            for hh, hs in enumerate(heads):
                run, acc = carry[2 * hh], carry[2 * hh + 1]
                kblk = k_ref[pl.ds(kstart, w), hs]
                vblk = v_ref[pl.ds(kstart, w), hs]
                z = lax.dot_general(qbs[hh], kblk, NT, preferred_element_type=F32)
                lg, _ = _neg_softplus(z)
                if masked:
                    lg = jnp.where(m, lg, 0.0)
                cum = _split_dot(lg, t_suf) + run
                a = jnp.exp(z + cum)
                if masked:
                    a = jnp.where(m, a, 0.0)
                acc = acc + jnp.dot(a.astype(BF16), vblk, preferred_element_type=F32)
                run = cum[:, 0:1]
                rs_ref[hh] = jnp.where(lane == kb, run, rs_ref[hh])
                out += [run, acc]
            return tuple(out)

        carry = (jnp.zeros((bq, 1), F32), jnp.zeros((bq, dh), F32)) * hpb
        for sb in reversed(range(nsub)):
            carry = block(qi * nsub + sb, carry, True)

        def cond(c):
            alive = functools.reduce(jnp.maximum, [jnp.max(c[1 + 2 * hh]) for hh in range(hpb)])
            return jnp.logical_and(c[0] >= 0, alive > STICK_GONE)

        def step(c):
            return (c[0] - 1,) + block(c[0], c[1:], False)
        carry = lax.while_loop(cond, step, (qi * nsub - 1,) + carry)[1:]
        for hh, hs in enumerate(heads):
            o_ref[:, hs] = carry[2 * hh + 1].astype(o_ref.dtype)
            runs_ref[hh] = rs_ref[hh, :, 0:nkb]

    qs = pl.BlockSpec((bq, LANES), lambda h, i: (i, h))
    kv = pl.BlockSpec((S, LANES), lambda h, i: (0, h))
    return pl.pallas_call(body, grid=(D // LANES, S // bq), in_specs=[qs, kv, kv],
                          out_specs=(qs, pl.BlockSpec((hpb, bq, nkb), lambda h, i: (h, i, 0))),
                          out_shape=(jax.ShapeDtypeStruct((S, D), BF16), jax.ShapeDtypeStruct((D // dh, S, nkb), F32)),
                          scratch_shapes=[pltpu.VMEM((hpb, bq, LANES), F32)],
                          compiler_params=_params(("parallel", "parallel")), name=name)(q, k, v)


def attn_bwd(name, q, k, v, do, runs, dk0=None, dv0=None, bq=512, w=256):
    S, D = q.shape
    dh = HEAD_DIM
    hpb = LANES // dh
    bq = _tile(S, bq)
    w = _tile(bq, w)
    nsub = bq // w
    nkb = S // w
    scale = 1.0 / math.sqrt(dh)
    init = dk0 is not None

    def body(*refs):
        if init:
            q_ref, k_ref, v_ref, do_ref, runs_ref, dk0_ref, dv0_ref, dq_ref, dk_ref, dv_ref, rs_ref = refs
        else:
            q_ref, k_ref, v_ref, do_ref, runs_ref, dq_ref, dk_ref, dv_ref, rs_ref = refs
        qi = pl.program_id(1)

        @pl.when(qi == 0)
        def _():
            dk_ref[...] = dk0_ref[...] if init else jnp.zeros_like(dk_ref)
            dv_ref[...] = dv0_ref[...] if init else jnp.zeros_like(dv_ref)

        rr = lax.broadcasted_iota(jnp.int32, (w, w), 0)
        cc = lax.broadcasted_iota(jnp.int32, (w, w), 1)
        t_suf = (rr >= cc).astype(BF16)
        t_pre = (rr <= cc).astype(BF16)
        tq = qi * bq + lax.broadcasted_iota(jnp.int32, (bq, w), 0)
        tk = lax.broadcasted_iota(jnp.int32, (bq, w), 1)
        lane = lax.broadcasted_iota(jnp.int32, (bq, LANES), 1)
        lane1 = lax.broadcasted_iota(jnp.int32, (1, LANES), 1)
        ntot = (qi + 1) * nsub
        heads = [slice(hh * dh, (hh + 1) * dh) for hh in range(hpb)]
        qbs = [q_ref[:, hs] for hs in heads]
        dobs = [do_ref[:, hs].astype(BF16) for hs in heads]
        kb0 = ntot - nsub
        for hh in range(hpb):
            rs_ref[hh] = jnp.zeros((bq, LANES), F32)
            rs_ref[hh, :, 0:nkb] = runs_ref[hh]
            colmax = jnp.max(rs_ref[hh], axis=0, keepdims=True)
            dead = jnp.logical_and(jnp.logical_and(lane1 >= 1, lane1 <= ntot), colmax <= STICK_GONE)
            kb0 = jnp.minimum(kb0, jnp.sum(dead.astype(jnp.int32)))

        def block(kb, carry, masked):
            kstart = pl.multiple_of(kb * w, w)
            if masked:
                m = (tk + kstart) < tq
            out = []
            for hh, hs in enumerate(heads):
                pg_run, dq = carry[2 * hh], carry[2 * hh + 1]
                qb, dob = qbs[hh], dobs[hh]
                kblk = k_ref[pl.ds(kstart, w), hs]
                vblk = v_ref[pl.ds(kstart, w), hs]
                right = jnp.sum(jnp.where(lane == kb + 1, rs_ref[hh], 0.0), axis=1, keepdims=True)
                z = lax.dot_general(qb, kblk, NT, preferred_element_type=F32)
                lg, _ = _neg_softplus(z)
                sig = 1.0 - jnp.exp(lg)
                if masked:
                    lg = jnp.where(m, lg, 0.0)
                a = jnp.exp(z + _split_dot(lg, t_suf) + right)
                if masked:
                    a = jnp.where(m, a, 0.0)
                da = lax.dot_general(dob, vblk, NT, preferred_element_type=F32)
                g = da * a
                pin = _split_dot(g, t_pre) + pg_run
                dz = g - sig * pin
                if masked:
                    dz = jnp.where(m, dz, 0.0)
                dzb = dz.astype(BF16)
                dq = dq + jnp.dot(dzb, kblk, preferred_element_type=F32)
                dk_ref[pl.ds(kstart, w), hs] += lax.dot_general(dzb, qb, TN, preferred_element_type=F32)
                dv_ref[pl.ds(kstart, w), hs] += lax.dot_general(a.astype(BF16), dob, TN, preferred_element_type=F32)
                out += [pin[:, w - 1:w], dq]
            return tuple(out)

        carry = (jnp.zeros((bq, 1), F32), jnp.zeros((bq, dh), F32)) * hpb
        carry = lax.fori_loop(kb0, qi * nsub, lambda kb, c: block(kb, c, False), carry)
        for sb in range(nsub):
            carry = block(qi * nsub + sb, carry, True)
        for hh, hs in enumerate(heads):
            dq_ref[:, hs] = carry[2 * hh + 1] * scale

    qs = pl.BlockSpec((bq, LANES), lambda h, i: (i, h))
    kv = pl.BlockSpec((S, LANES), lambda h, i: (0, h))
    ins = [q, k, v, do, runs] + ([dk0, dv0] if init else [])
    specs = [qs, kv, kv, qs, pl.BlockSpec((hpb, bq, nkb), lambda h, i: (h, i, 0))] + ([kv, kv] if init else [])
    sds = jax.ShapeDtypeStruct((S, D), F32)
    return pl.pallas_call(body, grid=(D // LANES, S // bq), in_specs=specs, out_specs=(qs, kv, kv), out_shape=(sds, sds, sds),
                          scratch_shapes=[pltpu.VMEM((hpb, bq, LANES), F32)],
                          compiler_params=_params(("parallel", "arbitrary")), name=name)(*ins)


def loss_head(name, y, tgt, tm=512):
    S, D = y.shape
    tm = _tile(S, tm)

    def body(y_ref, t_ref, dy_ref, acc_ref):
        @pl.when(pl.program_id(0) == 0)
        def _():
            acc_ref[...] = jnp.zeros_like(acc_ref)
        e = y_ref[...] - t_ref[...]
        dy_ref[...] = e * (1.0 / D)
        acc_ref[...] += jnp.sum(e * e)

    row = pl.BlockSpec((tm, D), lambda i: (i, 0))
    return pl.pallas_call(body, grid=(S // tm,), in_specs=[row, row],
                          out_specs=(row, pl.BlockSpec((8, LANES), lambda i: (0, 0))),
                          out_shape=(jax.ShapeDtypeStruct((S, D), F32), jax.ShapeDtypeStruct((8, LANES), F32)),
                          compiler_params=_params(("arbitrary",)), name=name)(y, tgt)


def _to_sm(a, axis=-1):
    axis = axis % a.ndim
    shp = a.shape[:axis] + (N_CHIPS, a.shape[axis] // N_CHIPS) + a.shape[axis + 1:]
    return jnp.moveaxis(a.reshape(shp), axis, 0)


def _from_sm(a, axis=-1):
    nd = a.ndim - 1
    axis = axis % nd
    b = jnp.moveaxis(a, 0, axis)
    return b.reshape(b.shape[:axis] + (b.shape[axis] * b.shape[axis + 1],) + b.shape[axis + 2:])


class GradBuffers:
    def __init__(self, W):
        groups = {}
        for n in BIG:
            _, layers, rows, cols = W[n].shape
            groups.setdefault(cols, []).append((rows, n, layers))
        self.where, self.cols_of, self.buf, self.members = {}, {}, {}, {}
        for cols, items in groups.items():
            off, members = 0, []
            for rows, n, layers in sorted(items, key=lambda t: -t[0]):
                assert off % rows == 0
                self.where[n], self.cols_of[n] = (off, rows), cols
                members.append((n, off, rows * layers))
                off += rows * layers
            assert off % 32 == 0
            self.buf[cols] = lax.empty((N_CHIPS, off, cols), BF16)
            self.members[cols] = members

    def put(self, n, layer, fn, **kw):
        cols = self.cols_of[n]
        off, rows = self.where[n]
        self.buf[cols] = fn(into=self.buf[cols], off=off + layer * rows, **kw)


def forward_backward(x, p4, tgt, W):
    S, D = x.shape
    scale = 1.0 / math.sqrt(HEAD_DIM)
    saved = []
    kh = vh = xb_kv = None
    xb = x.astype(BF16)
    for i in range(DEPTH):
        sv = {'xb': xb}
        if i < N_A:
            h_sm, u = mm_pw1_glu(f"pw1glu_{i}", xb, W['a_pw1_w'], i, W['a_pw1_b'])
            c, s = conv_ln_silu_fwd(f"convln_{i}", u, W['a_dw_w'], W['a_dw_b'], W['a_ln_g'], W['a_ln_b'], i)
            x1, x1b, xh1, rs1 = mm_rowsm_ln(f"pw2ln_{i}", s, W['a_pw2_w'], i, x, W['ln_mix_g'], W['ln_mix_b'], i, bias=W['a_pw2_b'])
            sv.update(h_sm=h_sm, u=u, c=c, s=s)
        else:
            j = i - N_A
            if kh is None:
                xb_kv = xb
                kh = mm_rowsm("wk", xb, W['kv_wk'], 0, out_dtype=BF16)
                vh = mm_rowsm("wv", xb, W['kv_wv'], 0, out_dtype=BF16)
            qh = mm_rowsm(f"wq_{j}", xb, W['b_wq'], j, out_dtype=BF16, out_scale=scale)
            o, runs = attn_fwd(f"attn_{j}", qh, kh, vh)
            x1, x1b, xh1, rs1 = mm_rowsm_ln(f"woln_{j}", o, W['b_wo'], j, x, W['ln_mix_g'], W['ln_mix_b'], i)
            sv.update(qh=qh, o=o, runs=runs)
        up_sm, gp_sm, gc_sm, hf_sm = ffn_up_gate_fwd(f"upgate_{i}", x1b, W['ffn_w_up'], W['ffn_w_gate'], W['ffn_conv_w'],
                                                     W['ffn_conv_b'], i)
        tg = mm_rowsm(f"plegate_{i}", x1b, W['ple_w_gate'], i)
        pp = mm_proj(f"pleproj_{i}", p4, i, W['ple_w_proj'])
        x2, x2b, xh2, rs2 = mm_rowsm_ln(f"downln_{i}", hf_sm, W['ffn_w_down'], i, x1, W['ln_ffn_g'], W['ln_ffn_b'], i, a_sm=True,
                                        tg=tg, pp=pp)
        sv.update(x1b=x1b, xh1=xh1, rs1=rs1, up_sm=up_sm, gp_sm=gp_sm, gc_sm=gc_sm, hf_sm=hf_sm, tg=tg, pp=pp, xh2=xh2, rs2=rs2)
        saved.append(sv)
        x, xb = x2, x2b

    dx, lacc = loss_head("loss", x, tgt)
    loss_sum = lacc[0, 0]

    G = {n: [None] * DEPTH for n in WEIGHTS if n not in BIG}
    gb = GradBuffers(W)
    dk = dv = None
    for i in reversed(range(DEPTH)):
        sv = saved[i]
        dr, acc, dtg, dpp = ln_bwd(f"lnffn_b_{i}", dx, sv['xh2'], sv['rs2'], W['ln_ffn_g'], i, tg=sv['tg'], pp=sv['pp'])
        G['ln_ffn_g'][i], G['ln_ffn_b'][i] = acc[0], acc[1]
        gb.put('ple_w_proj', i, functools.partial(mm_tn_proj, f"dproj_{i}", p4, i, dpp))
        gb.put('ple_w_gate', i, functools.partial(mm_tn_rowsm_fan, f"dplegate_{i}", sv['x1b'], dtg))
        gb.put('ffn_w_down', i, functools.partial(mm_tn_rowsm, f"ddown_{i}", sv['hf_sm'], dr, a_sm=True))
        dup_sm, dgp_sm, cacc = ffn_gate_conv_bwd(f"gateconv_b_{i}", dr, W['ffn_w_down'], W['ffn_conv_w'], i, sv['up_sm'],
                                                 sv['gc_sm'], sv['gp_sm'])
        kw = W['ffn_conv_w'].shape[2]
        G['ffn_conv_w'][i] = cacc[:, 0:kw, :]
        G['ffn_conv_b'][i] = cacc[:, 7, :].reshape(-1)
        gb.put('ffn_w_up', i, functools.partial(mm_tn_colsm, f"dup_{i}", sv['x1b'], dup_sm))
        gb.put('ffn_w_gate', i, functools.partial(mm_tn_colsm, f"dgate_{i}", sv['x1b'], dgp_sm))
        dx1 = mm_nt_rowsm(f"dx1a_{i}", dtg, W['ple_w_gate'], i, res=dr, res_alpha=DN_ALPHA)
        dx1 = mm_nt_colsm(f"dx1b_{i}", _sm_parts(dup_sm), W['ffn_w_up'], i, res=dx1)
        dx1 = mm_nt_colsm(f"dx1c_{i}", _sm_parts(dgp_sm), W['ffn_w_gate'], i, res=dx1)

        dr1, acc1 = ln_bwd(f"lnmix_b_{i}", dx1, sv['xh1'], sv['rs1'], W['ln_mix_g'], i)
        G['ln_mix_g'][i], G['ln_mix_b'][i] = acc1[0], acc1[1]
        xin = sv['xb']
        if i < N_A:
            G['a_pw2_b'][i] = acc1[2]
            gb.put('a_pw2_w', i, functools.partial(mm_tn_rowsm_fan, f"dpw2_{i}", sv['s'], dr1))
            ds = mm_nt_rowsm(f"ds_{i}", dr1, W['a_pw2_w'], i)
            dc, cacc = ln_silu_bwd(f"lnsilu_b_{i}", ds, sv['c'], W['a_ln_g'], W['a_ln_b'], i)
            G['a_ln_g'][i], G['a_ln_b'][i], G['a_dw_b'][i] = cacc[0], cacc[1], cacc[2]
            da, dg, dw, dba, dbg = conv_glu_bwd(f"convglu_b_{i}", dc, sv['u'], sv['h_sm'], W['a_dw_w'], i)
            kw = W['a_dw_w'].shape[1]
            G['a_dw_w'][i] = _from_sm(dw[:, 0:kw, :], axis=-1)
            G['a_pw1_b'][i] = jnp.concatenate([dba[:, 0, :], dbg[:, 0, :]], axis=0)
            half = da.shape[0]
            gb.put('a_pw1_w', i, functools.partial(mm_tn_colsm, f"dpw1a_{i}", xin, da))
            gb.put('a_pw1_w', i, functools.partial(mm_tn_colsm, f"dpw1g_{i}", xin, dg), j0=half)
            dx = mm_nt_colsm(f"dxa_{i}", _sm_parts(da) + _sm_parts(dg), W['a_pw1_w'], i, res=dr1, res_alpha=DN_ALPHA)
        else:
            j = i - N_A
            gb.put('b_wo', j, functools.partial(mm_tn_rowsm_fan, f"dwo_{j}", sv['o'], dr1))
            do = mm_nt_rowsm(f"do_{j}", dr1, W['b_wo'], j)
            dq, dk, dv = attn_bwd(f"attn_b_{j}", sv['qh'], kh, vh, do, sv['runs'], dk, dv)
            gb.put('b_wq', j, functools.partial(mm_tn_rowsm_fan, f"dwq_{j}", xin, dq))
            dx = mm_nt_rowsm(f"dxq_{j}", dq, W['b_wq'], j, res=dr1, res_alpha=DN_ALPHA)
            if j == 0:
                gb.put('kv_wk', 0, functools.partial(mm_tn_rowsm_fan, "dwk", xb_kv, dk))
                gb.put('kv_wv', 0, functools.partial(mm_tn_rowsm_fan, "dwv", xb_kv, dv))
                dx = mm_nt_rowsm("dxk", dk, W['kv_wk'], 0, res=dx)
                dx = mm_nt_rowsm("dxv", dv, W['kv_wv'], 0, res=dx)
    return loss_sum, dx, G, gb


MESH = pl.DeviceIdType.MESH
HBM = pl.BlockSpec(memory_space=pltpu.HBM)


def _place():
    x, y, c = lax.axis_index("x"), lax.axis_index("y"), lax.axis_index("c")
    others = [(1 - x, y), (x, 1 - y), (1 - x, 1 - y)]
    return x, y, c, others


def allgather_chips(name, arrs):
    n = len(arrs)

    def body(*refs):
        ins, outs = refs[:n], refs[n:2 * n]
        send_sems, recv_sems = refs[2 * n:]
        x, y, c, others = _place()
        me = 2 * x + y
        sibling = (x, y, 1 - c)
        ids = [2 * ch[0] + ch[1] for ch in others]
        from_id = jnp.where(c == 0, ids[0], ids[1])
        to_chip = (jnp.where(c == 0, x, 1 - x), jnp.where(c == 0, 1 - y, y))

        def remote(a, k, src, chip_id, half, to):
            return pltpu.make_async_remote_copy(src_ref=src, dst_ref=outs[a].at[chip_id, half], send_sem=send_sems.at[a, k],
                                                recv_sem=recv_sems.at[a, k], device_id=to, device_id_type=MESH)

        sent = [remote(a, k, ins[a].at[c], me, c, (others[k][0], others[k][1], c)) for a in range(n) for k in range(2)]
        for cp in sent:
            cp.start()
        for a in range(n):
            for k in range(2):
                remote(a, k, ins[a].at[c], ids[k], c, sibling).wait_recv()
            sent.append(remote(a, 2, outs[a].at[from_id, c], from_id, c, (to_chip[0], to_chip[1], c)))
            sent[-1].start()
            for k in range(2):
                sent.append(remote(a, 3 + k, outs[a].at[ids[k], c], ids[k], c, sibling))
                sent[-1].start()
        for a in range(n):
            remote(a, 2, ins[a].at[c], ids[2], c, sibling).wait_recv()
            sent.append(remote(a, 5, outs[a].at[ids[2], c], ids[2], c, sibling))
            sent[-1].start()
        for a in range(n):
            for k in range(3):
                remote(a, 3 + k, ins[a].at[c], ids[k], 1 - c, sibling).wait_recv()
        for cp in sent:
            cp.wait_send()

    outs = pl.pallas_call(body, out_shape=tuple(jax.ShapeDtypeStruct((N_CHIPS,) + a.shape, a.dtype) for a in arrs),
                          in_specs=[HBM] * n, out_specs=tuple([HBM] * n),
                          scratch_shapes=[pltpu.SemaphoreType.DMA((n, 6)), pltpu.SemaphoreType.DMA((n, 6))],
                          name=name)(*arrs)
    me = 2 * lax.axis_index("x") + lax.axis_index("y")
    return [lax.dynamic_update_index_in_dim(o, a, me, 0) for o, a in zip(outs, arrs)]


def exchange_sibling(name, gs):
    n = len(gs)

    def body(*refs):
        g_refs, o_refs = refs[:n], refs[n:2 * n]
        send_sems, recv_sems = refs[2 * n:]
        x, y, c, _ = _place()
        cps = [pltpu.make_async_remote_copy(src_ref=g_refs[a].at[j, 1 - c], dst_ref=o_refs[a].at[j], send_sem=send_sems.at[a, j],
                                            recv_sem=recv_sems.at[a, j], device_id=(x, y, 1 - c), device_id_type=MESH)
               for a in range(n) for j in range(N_CHIPS)]
        for cp in cps:
            cp.start()
        for cp in cps:
            cp.wait()

    return pl.pallas_call(body, out_shape=tuple(jax.ShapeDtypeStruct((N_CHIPS,) + g.shape[2:], g.dtype) for g in gs),
                          in_specs=[HBM] * n, out_specs=tuple([HBM] * n),
                          scratch_shapes=[pltpu.SemaphoreType.DMA((n, N_CHIPS)), pltpu.SemaphoreType.DMA((n, N_CHIPS))],
                          name=name)(*gs)


def _ring_peers():
    x, y, c, _ = _place()
    first = (jnp.where(c == 0, 1 - x, x), jnp.where(c == 0, y, 1 - y))
    second = (jnp.where(c == 0, x, 1 - x), jnp.where(c == 0, 1 - y, y))
    return c, first, second, 2 * (1 - x) + (1 - y)


def exchange_first(name, ss):
    n = len(ss)

    def body(*refs):
        s_refs, o_refs = refs[:n], refs[n:2 * n]
        send_sems, recv_sems = refs[2 * n:]
        c, first, _, diag = _ring_peers()
        cps = [pltpu.make_async_remote_copy(src_ref=s_refs[a].at[slot], dst_ref=o_refs[a].at[k], send_sem=send_sems.at[a, k],
                                            recv_sem=recv_sems.at[a, k], device_id=(first[0], first[1], c), device_id_type=MESH)
               for a in range(n) for k, slot in enumerate((2 * first[0] + first[1], diag))]
        for cp in cps:
            cp.start()
        for cp in cps:
            cp.wait()

    return pl.pallas_call(body, out_shape=tuple(jax.ShapeDtypeStruct((2,) + s.shape[1:], s.dtype) for s in ss),
                          in_specs=[HBM] * n, out_specs=tuple([HBM] * n),
                          scratch_shapes=[pltpu.SemaphoreType.DMA((n, 2)), pltpu.SemaphoreType.DMA((n, 2))], name=name)(*ss)


def exchange_second(name, ts):
    n = len(ts)

    def body(*refs):
        t_refs, o_refs = refs[:n], refs[n:2 * n]
        send_sems, recv_sems = refs[2 * n:]
        c, _, second, _ = _ring_peers()
        cps = [pltpu.make_async_remote_copy(src_ref=t_refs[a], dst_ref=o_refs[a], send_sem=send_sems.at[a], recv_sem=recv_sems.at[a],
                                            device_id=(second[0], second[1], c), device_id_type=MESH) for a in range(n)]
        for cp in cps:
            cp.start()
        for cp in cps:
            cp.wait()

    return pl.pallas_call(body, out_shape=tuple(jax.ShapeDtypeStruct(t.shape, t.dtype) for t in ts),
                          in_specs=[HBM] * n, out_specs=tuple([HBM] * n),
                          scratch_shapes=[pltpu.SemaphoreType.DMA((n,)), pltpu.SemaphoreType.DMA((n,))], name=name)(*ts)


def share_sibling(name, ts):
    n = len(ts)

    def body(*refs):
        o_refs = refs[n:2 * n]
        send_sems, recv_sems = refs[2 * n:]
        x, y, c, _ = _place()
        cps = [pltpu.make_async_remote_copy(src_ref=o_refs[a].at[c], dst_ref=o_refs[a].at[c], send_sem=send_sems.at[a],
                                            recv_sem=recv_sems.at[a], device_id=(x, y, 1 - c), device_id_type=MESH)
               for a in range(n)]
        for cp in cps:
            cp.start()
        for a in range(n):
            pltpu.make_async_remote_copy(src_ref=o_refs[a].at[c], dst_ref=o_refs[a].at[1 - c], send_sem=send_sems.at[a],
                                         recv_sem=recv_sems.at[a], device_id=(x, y, 1 - c), device_id_type=MESH).wait_recv()
        for cp in cps:
            cp.wait_send()

    return pl.pallas_call(body, out_shape=tuple(jax.ShapeDtypeStruct(t.shape, t.dtype) for t in ts),
                          in_specs=[HBM] * n, out_specs=tuple([HBM] * n), input_output_aliases={a: a for a in range(n)},
                          scratch_shapes=[pltpu.SemaphoreType.DMA((n,)), pltpu.SemaphoreType.DMA((n,))],
                          name=name)(*ts)


def add_halves(name, g, recv, place, out_dtype, tr=512):
    _, _, R, C = g.shape
    tr = _tile(R, tr, 16)

    def body(p_ref, a_ref, b_ref, o_ref):
        o_ref[...] = (a_ref[...].astype(F32) + b_ref[...].astype(F32)).astype(o_ref.dtype)

    blk = pl.BlockSpec((None, tr, C), lambda j, i, p: (j, i, 0))
    gs = pltpu.PrefetchScalarGridSpec(num_scalar_prefetch=1, grid=(N_CHIPS, R // tr),
                                      in_specs=[pl.BlockSpec((None, None, tr, C), lambda j, i, p: (j, p[0], i, 0)), blk],
                                      out_specs=blk)
    return pl.pallas_call(body, grid_spec=gs, out_shape=jax.ShapeDtypeStruct((N_CHIPS, R, C), out_dtype),
                          compiler_params=_params(("parallel", "parallel")), name=name)(place, g, recv)


def add_pass_on(name, s, got, place, tr=512):
    _, R, C = s.shape
    tr = _tile(R, tr, 16)

    def body(p_ref, a_ref, b_ref, o_ref):
        o_ref[...] = (a_ref[...].astype(F32) + b_ref[...].astype(F32)).astype(o_ref.dtype)

    gs = pltpu.PrefetchScalarGridSpec(num_scalar_prefetch=1, grid=(R // tr,),
                                      in_specs=[pl.BlockSpec((None, tr, C), lambda i, p: (p[2], i, 0)),
                                                pl.BlockSpec((None, tr, C), lambda i, p: (1, i, 0))],
                                      out_specs=pl.BlockSpec((tr, C), lambda i, p: (i, 0)))
    return pl.pallas_call(body, grid_spec=gs, out_shape=jax.ShapeDtypeStruct((R, C), s.dtype),
                          compiler_params=_params(("parallel",)), name=name)(place, s, got)


def add_chips(name, g, r1, got1, got2, place, tr=512):
    _, _, R, C = g.shape
    tr = _tile(R, tr, 16)

    def body(p_ref, a_ref, b_ref, c_ref, d_ref, o_ref):
        o_ref[...] = ((a_ref[...].astype(F32) + b_ref[...].astype(F32)) + c_ref[...].astype(F32)) + d_ref[...].astype(F32)

    gs = pltpu.PrefetchScalarGridSpec(num_scalar_prefetch=1, grid=(R // tr,),
                                      in_specs=[pl.BlockSpec((None, None, tr, C), lambda i, p: (p[1], p[0], i, 0)),
                                                pl.BlockSpec((None, tr, C), lambda i, p: (p[1], i, 0)),
                                                pl.BlockSpec((None, tr, C), lambda i, p: (0, i, 0)),
                                                pl.BlockSpec((tr, C), lambda i, p: (i, 0))],
                                      out_specs=pl.BlockSpec((None, tr, C), lambda i, p: (p[0], i, 0)))
    return pl.pallas_call(body, grid_spec=gs, out_shape=jax.ShapeDtypeStruct((2, R, C), F32),
                          compiler_params=_params(("parallel",)), name=name)(place, g, r1, got1, got2)


def reduce_scatter(gs, wire_dtypes, place):
    r1 = exchange_sibling("rs_sibling", gs)
    s1 = [add_halves(f"rs_add_cores_{a}", g, r, place, dt) for a, (g, r, dt) in enumerate(zip(gs, r1, wire_dtypes))]
    got1 = exchange_first("rs_first", s1)
    t = [add_pass_on(f"rs_add_pass_{a}", s, g1, place) for a, (s, g1) in enumerate(zip(s1, got1))]
    got2 = exchange_second("rs_second", t)
    tot = [add_chips(f"rs_add_chips_{a}", g, r, g1, g2, place) for a, (g, r, g1, g2) in enumerate(zip(gs, r1, got1, got2))]
    return share_sibling("rs_share", tot)


def adamw(name, w, g, m, v, tr=512):
    shp = w.shape
    cols = shp[-1]
    if w.ndim == 3 and shp[1] % 8 == 0:
        layers, rows_l = shp[0], shp[1]
        g = g.reshape(-1, cols)
        tr = _tile(rows_l, tr)
        per = rows_l // tr
        blk3 = pl.BlockSpec((None, tr, cols), lambda l, i: (l, i, 0))
        gblk = pl.BlockSpec((tr, cols), lambda l, i: (l * per + i, 0))
        sds = jax.ShapeDtypeStruct(shp, F32)
        return pl.pallas_call(_adamw_body(), grid=(layers, per), in_specs=[blk3, gblk, blk3, blk3], out_specs=(blk3, blk3, blk3),
                              out_shape=(sds, sds, sds), compiler_params=_params(("parallel", "parallel")), name=name)(w, g, m, v)
    w2, g2, m2, v2 = (a.reshape(-1, cols) for a in (w, g, m, v))
    rows = w2.shape[0]
    tr = _tile(rows, tr)
    blk = pl.BlockSpec((tr, cols), lambda i: (i, 0))
    sds = jax.ShapeDtypeStruct((rows, cols), F32)
    d, mo, vo = pl.pallas_call(_adamw_body(), grid=(rows // tr,), in_specs=[blk] * 4, out_specs=(blk, blk, blk), out_shape=(sds, sds, sds),
                               compiler_params=_params(("parallel",)), name=name)(w2, g2, m2, v2)
    return d.reshape(shp), mo.reshape(shp), vo.reshape(shp)


def _adamw_body():
    def body(w_ref, g_ref, m_ref, v_ref, d_ref, mo_ref, vo_ref):
        g_ = g_ref[...]
        m_ = ADAM_B1 * m_ref[...] + (1.0 - ADAM_B1) * g_
        v_ = ADAM_B2 * v_ref[...] + (1.0 - ADAM_B2) * (g_ * g_)
        m_hat = m_ / (1.0 - ADAM_B1 ** ADAM_STEP)
        v_hat = v_ / (1.0 - ADAM_B2 ** ADAM_STEP)
        d_ref[...] = -ADAM_LR * (m_hat / (jnp.sqrt(v_hat) + ADAM_EPS) + ADAM_WD * w_ref[...])
        mo_ref[...] = m_
        vo_ref[...] = v_
    return body


PACK_ALIGN = 1024


def _pad_to(a, mult, axis=-1):
    axis = axis % a.ndim
    extra = (-a.shape[axis]) % mult
    if extra == 0:
        return a
    pads = [(0, 0)] * a.ndim
    pads[axis] = (0, extra)
    return jnp.pad(a, pads)


def _pack(pieces, lead, row_mult):
    nl = len(lead)
    flat, offs, sizes, off = [], [], [], 0
    for a in pieces:
        f = a.reshape(lead + (-1,))
        sizes.append(f.shape[-1])
        f = _pad_to(f, PACK_ALIGN)
        offs.append(off)
        off += f.shape[-1]
        flat.append(f)
    cat = _pad_to(jnp.concatenate(flat, axis=nl), 2 * row_mult * LANES)
    return cat.reshape(lead + (2, -1, LANES)), offs, sizes


def _unpack(packed, lead, offs, sizes, shapes):
    flat = packed.reshape(lead + (-1,))
    return [lax.slice_in_dim(flat, o, o + s, axis=len(lead)).reshape(lead + tuple(shp)) for o, s, shp in zip(offs, sizes, shapes)]


def _stack_grads(G, names):
    out = {}
    for n in names:
        parts = [g for g in G[n] if g is not None]
        if n in ('kv_wk', 'kv_wv'):
            out[n] = parts[0]
        elif n in REPLICATED:
            out[n] = jnp.stack(parts, axis=0).reshape(N_CHIPS, -1)
        elif n in ('a_dw_w', 'a_dw_b', 'a_ln_g', 'a_ln_b', 'a_pw2_b'):
            out[n] = _to_sm(jnp.stack(parts, axis=0), axis=-1)
        else:
            out[n] = jnp.stack(parts, axis=1)
    return out


def _whole_weights(big, small, rep, D):
    W = {}
    for n in BIG:
        a = big[n]
        W[n] = a[:, None] if n in ('kv_wk', 'kv_wv') else a
    W['a_pw1_b'] = small['a_pw1_b'][:, :, None, :]
    W['a_dw_w'] = _from_sm(small['a_dw_w'], axis=-1)
    for n in ('a_dw_b', 'a_ln_g', 'a_ln_b', 'a_pw2_b'):
        W[n] = _from_sm(small[n], axis=-1)[:, None, :]
    W['ffn_conv_w'] = small['ffn_conv_w']
    L, F = rep['ffn_conv_b'].shape
    W['ffn_conv_b'] = rep['ffn_conv_b'].reshape(L, N_CHIPS, 1, F // N_CHIPS)
    for n in ('ln_mix_g', 'ln_mix_b', 'ln_ffn_g', 'ln_ffn_b'):
        W[n] = rep[n][:, None, :]
    return W


SMALL = ('a_pw1_b', 'a_dw_w', 'a_dw_b', 'a_ln_g', 'a_ln_b', 'a_pw2_b', 'ffn_conv_w')


def _step(x, p, loss_target, w, m, v):
    S, D = x.shape[-2:]
    x2, tgt = x.reshape(S, D), loss_target.reshape(S, D)
    ax, ay, ac = lax.axis_index("x"), lax.axis_index("y"), lax.axis_index("c")
    second = jnp.where(ac == 0, 2 * ax + (1 - ay), 2 * (1 - ax) + ay)
    place = jnp.stack([ac, 2 * ax + ay, second]).astype(jnp.int32)

    big_in = [w[n].astype(BF16).reshape((2, -1) + w[n].shape[1:] if w[n].ndim == 3 else (2, -1, w[n].shape[-1])) for n in BIG]
    small_in, s_offs, s_sizes = _pack([w[n] for n in SMALL], (), 8)
    gathered = allgather_chips("gather_weights", big_in + [small_in])
    big = {n: g.reshape((N_CHIPS,) + w[n].shape) for n, g in zip(BIG, gathered[:-1])}
    small = dict(zip(SMALL, _unpack(gathered[-1], (N_CHIPS,), s_offs, s_sizes, [w[n].shape for n in SMALL])))
    W = _whole_weights(big, small, {n: w[n] for n in REPLICATED}, D)

    loss_sum, dx, G, gb = forward_backward(x2, p, tgt, W)
    loss = lax.psum(0.5 * loss_sum / D, ("x", "y", "c"))

    vectors = [n for n in WEIGHTS if n not in BIG]
    mats = [b.reshape(N_CHIPS, 2, b.shape[1] // 2, b.shape[2]) for b in gb.buf.values()]
    members = [gb.members[cols] for cols in gb.buf]
    g_sm = _stack_grads(G, vectors)
    packed, offs, sizes = _pack([g_sm[n] for n in vectors], (N_CHIPS,), 512)
    reduced = reduce_scatter(mats + [packed], [BF16] * len(mats) + [F32], place)
    shapes = [w[n].shape if n not in REPLICATED else (w[n].size // N_CHIPS,) for n in vectors]
    g_mine = dict(zip(vectors, _unpack(reduced[-1], (), offs, sizes, shapes)))
    for red, where in zip(reduced[:-1], members):
        rows = red.reshape(-1, red.shape[-1])
        for n, off, cnt in where:
            g_mine[n] = lax.slice_in_dim(rows, off, off + cnt, axis=0).reshape(w[n].shape)
    rep_in, r_offs, r_sizes = _pack([g_mine[n] for n in REPLICATED], (), 8)
    rep_all = allgather_chips("gather_replicated_grads", [rep_in])[0]
    for n, g in zip(REPLICATED, _unpack(rep_all, (N_CHIPS,), r_offs, r_sizes, [(w[n].size // N_CHIPS,) for n in REPLICATED])):
        g_mine[n] = g.reshape(w[n].shape)

    grads, deltas, new_m, new_v = [], [], [], []
    for n in WEIGHTS:
        d, mo, vo = adamw(f"adamw_{n}", w[n], g_mine[n], m[n], v[n])
        grads.append(g_mine[n])
        deltas.append(d)
        new_m.append(mo)
        new_v.append(vo)
    return (loss, dx.reshape(x.shape), *grads, *deltas, *new_m, *new_v)


def kernel(x, p, a_pw1_w, a_pw1_b, a_dw_w, a_dw_b, a_ln_g, a_ln_b, a_pw2_w, a_pw2_b, b_wq, kv_wk, kv_wv, b_wo, ln_mix_g, ln_mix_b, ffn_w_up, ffn_w_gate, ffn_conv_w, ffn_conv_b, ffn_w_down, ple_w_gate, ple_w_proj, ln_ffn_g, ln_ffn_b, loss_target, m_a_pw1_w, m_a_pw1_b, m_a_dw_w, m_a_dw_b, m_a_ln_g, m_a_ln_b, m_a_pw2_w, m_a_pw2_b, m_b_wq, m_kv_wk, m_kv_wv, m_b_wo, m_ln_mix_g, m_ln_mix_b, m_ffn_w_up, m_ffn_w_gate, m_ffn_conv_w, m_ffn_conv_b, m_ffn_w_down, m_ple_w_gate, m_ple_w_proj, m_ln_ffn_g, m_ln_ffn_b, v_a_pw1_w, v_a_pw1_b, v_a_dw_w, v_a_dw_b, v_a_ln_g, v_a_ln_b, v_a_pw2_w, v_a_pw2_b, v_b_wq, v_kv_wk, v_kv_wv, v_b_wo, v_ln_mix_g, v_ln_mix_b, v_ffn_w_up, v_ffn_w_gate, v_ffn_conv_w, v_ffn_conv_b, v_ffn_w_down, v_ple_w_gate, v_ple_w_proj, v_ln_ffn_g, v_ln_ffn_b):
    vals = dict(locals())
    w = {n: vals[n] for n in WEIGHTS}
    m = {n: vals["m_" + n] for n in WEIGHTS}
    v = {n: vals["v_" + n] for n in WEIGHTS}
    return _step(x, p, loss_target, w, m, v)
```

```python
import functools
import math

import jax
import jax.numpy as jnp
import numpy as np
from jax import lax
from jax.experimental import pallas as pl
from jax.experimental.pallas import tpu as pltpu

F32, BF16 = jnp.float32, jnp.bfloat16

HEAD_DIM = 64
LN_EPS = 1e-5
DEPTH = 4
N_A = DEPTH // 2
DN_ALPHA = (2.0 * DEPTH) ** 0.25
N_CHIPS = 4

ADAM_LR, ADAM_B1, ADAM_B2, ADAM_EPS, ADAM_WD, ADAM_STEP = 0.001, 0.9, 0.999, 1e-08, 0.01, 10

VMEM_LIMIT_BYTES = 56 * 2**20
LANES = 128
SUBLANES = 8
CONV_HALO = 32
FFN_HALO = 16

NN = (((1,), (0,)), ((), ()))
NT = (((1,), (1,)), ((), ()))
TN = (((0,), (0,)), ((), ()))

WEIGHTS = ['a_pw1_w', 'a_pw1_b', 'a_dw_w', 'a_dw_b', 'a_ln_g', 'a_ln_b', 'a_pw2_w', 'a_pw2_b', 'b_wq', 'kv_wk', 'kv_wv',
           'b_wo', 'ln_mix_g', 'ln_mix_b', 'ffn_w_up', 'ffn_w_gate', 'ffn_conv_w', 'ffn_conv_b', 'ffn_w_down', 'ple_w_gate',
           'ple_w_proj', 'ln_ffn_g', 'ln_ffn_b']
REPLICATED = ('ln_mix_g', 'ln_mix_b', 'ffn_conv_b', 'ln_ffn_g', 'ln_ffn_b')
BIG = ('a_pw1_w', 'a_pw2_w', 'b_wq', 'kv_wk', 'kv_wv', 'b_wo', 'ffn_w_up', 'ffn_w_gate', 'ffn_w_down', 'ple_w_gate',
       'ple_w_proj')


def _tile(n, pref, mult=8):
    t = min(n, pref)
    while t > 0:
        if n % t == 0 and t % mult == 0:
            return t
        t -= 1
    return n


def _params(sem):
    return pltpu.CompilerParams(dimension_semantics=sem, vmem_limit_bytes=VMEM_LIMIT_BYTES)


def _sigmoid(x):
    return 0.5 * jnp.tanh(0.5 * x) + 0.5


def _mm_call(name, grid, terms, out_spec, out_sds, dims, bias=None, res=None, res_alpha=1.0, out_scale=None, into=None):
    n_terms = len(terms)

    def body(*refs):
        o_ref = refs[-1]
        acc = None
        for t in range(n_terms):
            a = refs[2 * t][...].astype(BF16)
            b = refs[2 * t + 1][...].astype(BF16)
            d = lax.dot_general(a, b, dims, preferred_element_type=F32)
            acc = d if acc is None else acc + d
        k = 2 * n_terms
        if bias is not None:
            acc = acc + refs[k][...]
            k += 1
        if res is not None:
            acc = acc + res_alpha * refs[k][...]
        if out_scale is not None:
            acc = acc * out_scale
        o_ref[...] = acc.astype(o_ref.dtype)

    operands, specs = [], []
    for a, a_spec, b, b_spec in terms:
        operands += [a, b]
        specs += [a_spec, b_spec]
    for extra in (bias, res):
        if extra is not None:
            operands.append(extra[0])
            specs.append(extra[1])
    aliases = {}
    if into is not None:
        aliases = {len(operands): 0}
        operands.append(into)
        specs.append(pl.BlockSpec(memory_space=pl.ANY))
        out_sds = jax.ShapeDtypeStruct(into.shape, into.dtype)
    return pl.pallas_call(body, out_shape=out_sds, grid=grid, in_specs=specs, out_specs=out_spec, input_output_aliases=aliases,
                          compiler_params=_params(("parallel",) * len(grid)), name=name)(*operands)


def _mm_fanout(name, a, a_spec, w4, w_block, layer, dims, out_spec, out_sds, store, grid, bias4=None, res=None, res_alpha=1.0):
    def body(*refs):
        a_ref, w_refs, o_ref = refs[0], refs[1:1 + N_CHIPS], refs[-1]
        k = 1 + N_CHIPS
        b_refs = refs[k:k + N_CHIPS] if bias4 is not None else None
        k += N_CHIPS if bias4 is not None else 0
        av = a_ref[...].astype(BF16)
        for j in range(N_CHIPS):
            d = lax.dot_general(av, w_refs[j][...].astype(BF16), dims, preferred_element_type=F32)
            if b_refs is not None:
                d = d + b_refs[j][...]
            if res is not None:
                d = d + res_alpha * res[2](refs[k], j)
            store(o_ref, j, d)

    nd = len(grid)
    operands = [a] + [w4] * N_CHIPS
    specs = [a_spec] + [pl.BlockSpec((None, None) + w_block, lambda *g, j=j: (j, layer, 0, 0)) for j in range(N_CHIPS)]
    if bias4 is not None:
        operands += [bias4] * N_CHIPS
        specs += [pl.BlockSpec((None, None, 1, bias4.shape[-1]), lambda *g, j=j: (j, layer, 0, 0)) for j in range(N_CHIPS)]
    if res is not None:
        operands.append(res[0])
        specs.append(res[1])
    return pl.pallas_call(body, out_shape=out_sds, grid=grid, in_specs=specs, out_specs=out_spec,
                          compiler_params=_params(("parallel",) * nd), name=name)(*operands)


def _store_slot(o_ref, j, d):
    o_ref[j] = d.astype(o_ref.dtype)


def mm_rowsm(name, a, w4, layer, a_sm=False, bias=None, out_dtype=F32, out_scale=None, tm=512):
    kc, N = w4.shape[-2:]
    M = a.shape[-2]
    tm = _tile(M, tm)
    terms = []
    for j in range(N_CHIPS):
        if a_sm:
            a_spec = pl.BlockSpec((None, tm, kc), lambda i, j=j: (j, i, 0))
        else:
            a_spec = pl.BlockSpec((tm, kc), lambda i, j=j: (i, j))
        terms.append((a, a_spec, w4, pl.BlockSpec((None, None, kc, N), lambda i, j=j: (j, layer, 0, 0))))
    b = None if bias is None else (bias, pl.BlockSpec((None, 1, N), lambda i: (layer, 0, 0)))
    return _mm_call(name, (M // tm,), terms, pl.BlockSpec((tm, N), lambda i: (i, 0)), jax.ShapeDtypeStruct((M, N), out_dtype),
                    NN, bias=b, out_scale=out_scale)


def mm_rowsm_ln(name, a, w4, layer, x, g, b, ln_layer, a_sm=False, bias=None, tg=None, pp=None, tm=512):
    kc, N = w4.shape[-2:]
    M = a.shape[-2]
    tm = _tile(M, tm, 16)
    ple = tg is not None
    nb = bias is not None

    def body(*refs):
        a_refs, w_refs = refs[0:N_CHIPS], refs[N_CHIPS:2 * N_CHIPS]
        k = 2 * N_CHIPS
        acc = None
        for j in range(N_CHIPS):
            d = jnp.dot(a_refs[j][...].astype(BF16), w_refs[j][...].astype(BF16), preferred_element_type=F32)
            acc = d if acc is None else acc + d
        if nb:
            acc = acc + refs[k][...]
            k += 1
        r = DN_ALPHA * refs[k][...] + acc
        k += 1
        if ple:
            r = r + _sigmoid(refs[k][...]) * refs[k + 1][...]
            k += 2
        g_ref, b_ref, y_ref, yb_ref, xh_ref, rs_ref = refs[k:]
        mu = jnp.mean(r, axis=-1, keepdims=True)
        dd = r - mu
        var = jnp.mean(dd * dd, axis=-1, keepdims=True)
        rstd = lax.rsqrt(var + LN_EPS)
        xh = dd * rstd
        y = xh * g_ref[...] + b_ref[...]
        y_ref[...] = y
        yb_ref[...] = y.astype(BF16)
        xh_ref[...] = xh
        rs_ref[...] = rstd

    row = pl.BlockSpec((tm, N), lambda i: (i, 0))
    vec = pl.BlockSpec((None, 1, N), lambda i: (ln_layer, 0, 0))
    if a_sm:
        a_specs = [pl.BlockSpec((None, tm, kc), lambda i, j=j: (j, i, 0)) for j in range(N_CHIPS)]
    else:
        a_specs = [pl.BlockSpec((tm, kc), lambda i, j=j: (i, j)) for j in range(N_CHIPS)]
    w_specs = [pl.BlockSpec((None, None, kc, N), lambda i, j=j: (j, layer, 0, 0)) for j in range(N_CHIPS)]
    ins = [a] * N_CHIPS + [w4] * N_CHIPS + ([bias] if nb else []) + [x] + ([tg, pp] if ple else []) + [g, b]
    specs = a_specs + w_specs + ([pl.BlockSpec((None, 1, N), lambda i: (layer, 0, 0))] if nb else []) + [row] + ([row, row] if ple else []) + [vec, vec]
    return pl.pallas_call(body, grid=(M // tm,), in_specs=specs,
                          out_specs=(row, row, row, pl.BlockSpec((tm, 1), lambda i: (i, 0))),
                          out_shape=(jax.ShapeDtypeStruct((M, N), F32), jax.ShapeDtypeStruct((M, N), BF16),
                                     jax.ShapeDtypeStruct((M, N), F32), jax.ShapeDtypeStruct((M, 1), F32)),
                          compiler_params=_params(("parallel",)), name=name)(*ins)


def mm_pw1_glu(name, xb, w4, layer, bias4, tm=512):
    M, K = xb.shape
    n = w4.shape[-1]
    half = N_CHIPS // 2
    tm = _tile(M, tm)

    def body(*refs):
        x_ref, w_refs, b_refs = refs[0], refs[1:1 + N_CHIPS], refs[1 + N_CHIPS:1 + 2 * N_CHIPS]
        h_ref, u_ref = refs[-2:]
        xv = x_ref[...].astype(BF16)
        for j in range(N_CHIPS):
            h_ref[j] = jnp.dot(xv, w_refs[j][...].astype(BF16), preferred_element_type=F32) + b_refs[j][...]
        for j in range(half):
            u_ref[:, j * n:(j + 1) * n] = h_ref[j] * _sigmoid(h_ref[j + half])

    specs = [pl.BlockSpec((tm, K), lambda i: (i, 0))]
    specs += [pl.BlockSpec((None, None, K, n), lambda i, j=j: (j, layer, 0, 0)) for j in range(N_CHIPS)]
    specs += [pl.BlockSpec((None, None, 1, n), lambda i, j=j: (j, layer, 0, 0)) for j in range(N_CHIPS)]
    return pl.pallas_call(body, grid=(M // tm,), in_specs=specs,
                          out_specs=(pl.BlockSpec((N_CHIPS, tm, n), lambda i: (0, i, 0)), pl.BlockSpec((tm, half * n), lambda i: (i, 0))),
                          out_shape=(jax.ShapeDtypeStruct((N_CHIPS, M, n), F32), jax.ShapeDtypeStruct((M, half * n), F32)),
                          compiler_params=_params(("parallel",)), name=name)(xb, *([w4] * N_CHIPS), *([bias4] * N_CHIPS))


def mm_nt_rowsm(name, dy, w4, layer, out_sm=False, res=None, res_alpha=1.0, out_dtype=F32, tm=1024):
    kc, N = w4.shape[-2:]
    M = dy.shape[0]
    tm = _tile(M, tm)

    def store_cols(o_ref, j, d):
        o_ref[:, j * kc:(j + 1) * kc] = d

    if out_sm:
        out_spec, sds, store = pl.BlockSpec((N_CHIPS, tm, kc), lambda i: (0, i, 0)), jax.ShapeDtypeStruct((N_CHIPS, M, kc), out_dtype), _store_slot
    else:
        out_spec, sds, store = pl.BlockSpec((tm, N_CHIPS * kc), lambda i: (i, 0)), jax.ShapeDtypeStruct((M, N_CHIPS * kc), F32), store_cols
    r = None if res is None else (res, pl.BlockSpec((tm, N_CHIPS * kc), lambda i: (i, 0)), lambda ref, j: ref[:, j * kc:(j + 1) * kc])
    return _mm_fanout(name, dy, pl.BlockSpec((tm, N), lambda i: (i, 0)), w4, (kc, N), layer, NT, out_spec, sds, store, (M // tm,),
                      res=r, res_alpha=res_alpha)


def mm_nt_colsm(name, dy_parts, w4, layer, res=None, res_alpha=1.0, tm=512):
    K, n = w4.shape[-2:]
    M = dy_parts[0][0].shape[1]
    tm = _tile(M, tm)
    terms = [(arr, pl.BlockSpec((None, tm, n), lambda i, idx=idx: (idx, i, 0)), w4,
              pl.BlockSpec((None, None, K, n), lambda i, j=j: (j, layer, 0, 0))) for j, (arr, idx) in enumerate(dy_parts)]
    r = None if res is None else (res, pl.BlockSpec((tm, K), lambda i: (i, 0)))
    return _mm_call(name, (M // tm,), terms, pl.BlockSpec((tm, K), lambda i: (i, 0)), jax.ShapeDtypeStruct((M, K), F32), NT,
                    res=r, res_alpha=res_alpha)


def _sm_parts(a):
    return [(a, j) for j in range(a.shape[0])]


def mm_tn_colsm(name, x, dy, into, off, j0=0, tk=512):
    M, K = x.shape
    nj, _, n = dy.shape
    tk = _tile(K, tk, LANES)
    assert off % tk == 0
    terms = [(x, pl.BlockSpec((M, tk), lambda j, k: (0, k)), dy, pl.BlockSpec((None, M, n), lambda j, k: (j, 0, 0)))]
    return _mm_call(name, (nj, K // tk), terms, pl.BlockSpec((None, tk, n), lambda j, k: (j + j0, off // tk + k, 0)),
                    None, TN, into=into)


def mm_tn_rowsm(name, a, dy, into, off, a_sm=False, tn=512):
    M, N = dy.shape
    kc = a.shape[-1] if a_sm else a.shape[-1] // N_CHIPS
    tn = _tile(N, tn, LANES)
    assert off % kc == 0
    a_spec = pl.BlockSpec((None, M, kc), lambda j, n: (j, 0, 0)) if a_sm else pl.BlockSpec((M, kc), lambda j, n: (0, j))
    terms = [(a, a_spec, dy, pl.BlockSpec((M, tn), lambda j, n: (0, n)))]
    return _mm_call(name, (N_CHIPS, N // tn), terms, pl.BlockSpec((None, kc, tn), lambda j, n: (j, off // kc, n)),
                    None, TN, into=into)


def mm_tn_rowsm_fan(name, a, dy, into, off, tn=256):
    M, N = dy.shape
    kc = a.shape[-1] // N_CHIPS
    tn = _tile(N, tn, LANES)
    assert off % kc == 0

    def body(a_ref, dy_ref, into_ref, o_ref):
        dyb = dy_ref[...].astype(BF16)
        for j in range(N_CHIPS):
            aj = a_ref[:, j * kc:(j + 1) * kc].astype(BF16)
            o_ref[j] = lax.dot_general(aj, dyb, TN, preferred_element_type=F32).astype(o_ref.dtype)

    return pl.pallas_call(body, out_shape=jax.ShapeDtypeStruct(into.shape, into.dtype), grid=(N // tn,),
                          in_specs=[pl.BlockSpec((M, N_CHIPS * kc), lambda n: (0, 0)), pl.BlockSpec((M, tn), lambda n: (0, n)),
                                    pl.BlockSpec(memory_space=pl.ANY)],
                          out_specs=pl.BlockSpec((N_CHIPS, kc, tn), lambda n: (0, off // kc, n)), input_output_aliases={2: 0},
                          compiler_params=_params(("parallel",)), name=name)(a, dy, into)


def mm_proj(name, p4, layer, w4, tm=512):
    S, P = p4.shape[-2:]
    n = w4.shape[-1]
    tm = _tile(S, tm)

    def store_cols(o_ref, j, d):
        o_ref[:, j * n:(j + 1) * n] = d

    return _mm_fanout(name, p4, pl.BlockSpec((None, None, tm, P), lambda i: (layer, 0, i, 0)), w4, (P, n), layer, NN,
                      pl.BlockSpec((tm, N_CHIPS * n), lambda i: (i, 0)), jax.ShapeDtypeStruct((S, N_CHIPS * n), F32),
                      store_cols, (S // tm,))


def mm_tn_proj(name, p4, layer, dpp, into, off):
    S, P = p4.shape[-2:]
    n = dpp.shape[-1] // N_CHIPS
    assert off % P == 0
    terms = [(p4, pl.BlockSpec((None, None, S, P), lambda j: (layer, 0, 0, 0)), dpp, pl.BlockSpec((S, n), lambda j: (0, j)))]
    return _mm_call(name, (N_CHIPS,), terms, pl.BlockSpec((None, P, n), lambda j: (j, off // P, 0)), None, TN, into=into)


def ln_bwd(name, dy, xh, rstd, g, layer, tg=None, pp=None, tm=512):
    S, D = dy.shape
    tm = _tile(S, tm)
    ple = tg is not None

    def body(*refs):
        if ple:
            dy_ref, xh_ref, rs_ref, g_ref, tg_ref, pp_ref, dr_ref, acc_ref, dtg_ref, dpp_ref = refs
        else:
            dy_ref, xh_ref, rs_ref, g_ref, dr_ref, acc_ref = refs
        dy_, xh_ = dy_ref[...], xh_ref[...]
        dxh = dy_ * g_ref[...]
        m1 = jnp.mean(dxh, axis=-1, keepdims=True)
        m2 = jnp.mean(dxh * xh_, axis=-1, keepdims=True)
        dr = rs_ref[...] * (dxh - m1 - xh_ * m2)
        dr_ref[...] = dr

        @pl.when(pl.program_id(0) == 0)
        def _():
            acc_ref[...] = jnp.zeros_like(acc_ref)
        acc_ref[0:1, :] += jnp.sum(dy_ * xh_, axis=0, keepdims=True)
        acc_ref[1:2, :] += jnp.sum(dy_, axis=0, keepdims=True)
        acc_ref[2:3, :] += jnp.sum(dr, axis=0, keepdims=True)
        if ple:
            pg = _sigmoid(tg_ref[...])
            dtg_ref[...] = (dr * pp_ref[...] * pg * (1.0 - pg)).astype(BF16)
            dpp_ref[...] = (dr * pg).astype(BF16)

    row = pl.BlockSpec((tm, D), lambda i: (i, 0))
    ins = [dy, xh, rstd, g] + ([tg, pp] if ple else [])
    specs = [row, row, pl.BlockSpec((tm, 1), lambda i: (i, 0)), pl.BlockSpec((None, 1, D), lambda i: (layer, 0, 0))] + ([row, row] if ple else [])
    outs = [jax.ShapeDtypeStruct((S, D), F32), jax.ShapeDtypeStruct((8, D), F32)]
    out_specs = [row, pl.BlockSpec((8, D), lambda i: (0, 0))]
    if ple:
        outs += [jax.ShapeDtypeStruct((S, D), BF16)] * 2
        out_specs += [row, row]
    return pl.pallas_call(body, grid=(S // tm,), in_specs=specs, out_specs=tuple(out_specs), out_shape=tuple(outs),
                          compiler_params=_params(("arbitrary",)), name=name)(*ins)


def conv_ln_silu_fwd(name, u, w, b, g, beta, layer, ts=128):
    S, D = u.shape
    kw = w.shape[1]
    ts = _tile(S, ts, CONV_HALO)
    lc = LANES if D % LANES == 0 else D

    def body(h_ref, u_ref, w_ref, b_ref, g_ref, be_ref, c_ref, s_ref, win_ref, rot_ref):
        i = pl.program_id(0)
        win_ref[0:CONV_HALO, :] = jnp.where(i == 0, 0.0, h_ref[...])
        win_ref[CONV_HALO:, :] = u_ref[...]
        for cc in range(D // lc):
            cs = slice(cc * lc, (cc + 1) * lc)
            for sub in range(1, SUBLANES):
                rot_ref[sub] = win_ref[sub:sub + ts + CONV_HALO - SUBLANES, cs]
            acc = jnp.zeros((ts, lc), F32) + b_ref[:, cs]
            for k in range(kw):
                whole, sub = divmod(CONV_HALO - (kw - 1) + k, SUBLANES)
                r0 = whole * SUBLANES
                acc = acc + w_ref[k:k + 1, cs] * (win_ref[r0:r0 + ts, cs] if sub == 0 else rot_ref[sub, r0:r0 + ts, :])
            c_ref[:, cs] = acc
        c = c_ref[...]
        mu = jnp.mean(c, axis=-1, keepdims=True)
        d = c - mu
        var = jnp.mean(d * d, axis=-1, keepdims=True)
        nrm = d * lax.rsqrt(var + LN_EPS) * g_ref[...] + be_ref[...]
        s_ref[...] = (nrm * _sigmoid(nrm)).astype(BF16)

    row = pl.BlockSpec((ts, D), lambda i: (i, 0))
    vec = pl.BlockSpec((None, 1, D), lambda i: (layer, 0, 0))
    halo = pl.BlockSpec((CONV_HALO, D), lambda i: (jnp.maximum(i * (ts // CONV_HALO) - 1, 0), 0))
    return pl.pallas_call(body, grid=(S // ts,),
                          in_specs=[halo, row, pl.BlockSpec((None, kw, D), lambda i: (layer, 0, 0)), vec, vec, vec],
                          out_specs=(row, row),
                          out_shape=(jax.ShapeDtypeStruct((S, D), F32), jax.ShapeDtypeStruct((S, D), BF16)),
                          scratch_shapes=[pltpu.VMEM((ts + CONV_HALO, D), F32),
                                          pltpu.VMEM((SUBLANES, ts + CONV_HALO - SUBLANES, lc), F32)],
                          compiler_params=_params(("parallel",)), name=name)(u, u, w, b, g, beta)


def ln_silu_bwd(name, ds, c, g, beta, layer, tm=256):
    S, D = c.shape
    tm = _tile(S, tm)

    def body(ds_ref, c_ref, g_ref, be_ref, dc_ref, acc_ref):
        c_ = c_ref[...]
        mu = jnp.mean(c_, axis=-1, keepdims=True)
        d = c_ - mu
        var = jnp.mean(d * d, axis=-1, keepdims=True)
        rstd = lax.rsqrt(var + LN_EPS)
        xh = d * rstd
        nrm = xh * g_ref[...] + be_ref[...]
        sg = _sigmoid(nrm)
        dn = ds_ref[...] * (sg * (1.0 + nrm * (1.0 - sg)))
        dxh = dn * g_ref[...]
        m1 = jnp.mean(dxh, axis=-1, keepdims=True)
        m2 = jnp.mean(dxh * xh, axis=-1, keepdims=True)
        dc = rstd * (dxh - m1 - xh * m2)
        dc_ref[...] = dc

        @pl.when(pl.program_id(0) == 0)
        def _():
            acc_ref[...] = jnp.zeros_like(acc_ref)
        acc_ref[0:1, :] += jnp.sum(dn * xh, axis=0, keepdims=True)
        acc_ref[1:2, :] += jnp.sum(dn, axis=0, keepdims=True)
        acc_ref[2:3, :] += jnp.sum(dc, axis=0, keepdims=True)

    row = pl.BlockSpec((tm, D), lambda i: (i, 0))
    vec = pl.BlockSpec((None, 1, D), lambda i: (layer, 0, 0))
    return pl.pallas_call(body, grid=(S // tm,), in_specs=[row, row, vec, vec],
                          out_specs=(row, pl.BlockSpec((8, D), lambda i: (0, 0))),
                          out_shape=(jax.ShapeDtypeStruct((S, D), F32), jax.ShapeDtypeStruct((8, D), F32)),
                          compiler_params=_params(("arbitrary",)), name=name)(ds, c, g, beta)


def conv_glu_bwd(name, dc, u, h_sm, w, layer, ts=128):
    S, D = dc.shape
    kw = w.shape[1]
    half = N_CHIPS // 2
    n = D // half
    ts = _tile(S, ts, CONV_HALO)
    nblk = S // ts
    lc = LANES if n % LANES == 0 else n

    def body(dc_ref, dcn_ref, u_ref, a_ref, g_ref, w_ref, da_ref, dg_ref, dw_ref, dba_ref, dbg_ref, dwin_ref, rot_ref):
        i = pl.program_id(1)
        dwin_ref[0:ts, :] = dc_ref[...]
        dwin_ref[ts:, :] = jnp.where(i == nblk - 1, 0.0, dcn_ref[...])

        @pl.when(i == 0)
        def _():
            dw_ref[...] = jnp.zeros_like(dw_ref)
            dba_ref[...] = jnp.zeros_like(dba_ref)
            dbg_ref[...] = jnp.zeros_like(dbg_ref)

        for cc in range(n // lc):
            cs = slice(cc * lc, (cc + 1) * lc)
            for sub in range(1, SUBLANES):
                rot_ref[sub] = dwin_ref[sub:sub + ts + CONV_HALO - SUBLANES, cs]
            ub = u_ref[:, cs]
            du = jnp.zeros((ts, lc), F32)
            for k in range(kw):
                whole, sub = divmod(kw - 1 - k, SUBLANES)
                r0 = whole * SUBLANES
                shifted = dwin_ref[r0:r0 + ts, cs] if sub == 0 else rot_ref[sub, r0:r0 + ts, :]
                du = du + w_ref[k:k + 1, cs] * shifted
                dw_ref[k:k + 1, cs] += jnp.sum(shifted * ub, axis=0, keepdims=True)
            a = a_ref[:, cs]
            sg = _sigmoid(g_ref[:, cs])
            da = du * sg
            dg = du * a * sg * (1.0 - sg)
            da_ref[:, cs] = da.astype(BF16)
            dg_ref[:, cs] = dg.astype(BF16)
            dba_ref[0:1, cs] += jnp.sum(da, axis=0, keepdims=True)
            dbg_ref[0:1, cs] += jnp.sum(dg, axis=0, keepdims=True)

    r = ts // CONV_HALO
    main = pl.BlockSpec((ts, n), lambda j, i: (i, j))
    nxt = pl.BlockSpec((CONV_HALO, n), lambda j, i: (jnp.minimum((i + 1) * r, S // CONV_HALO - 1), j))
    sm_a = pl.BlockSpec((None, ts, n), lambda j, i: (j, i, 0))
    sm_g = pl.BlockSpec((None, ts, n), lambda j, i: (j + half, i, 0))
    da, dg, dw, dba, dbg = pl.pallas_call(
        body, grid=(half, nblk),
        in_specs=[main, nxt, main, sm_a, sm_g, pl.BlockSpec((None, kw, n), lambda j, i: (layer, 0, j))],
        out_specs=(pl.BlockSpec((None, ts, n), lambda j, i: (j, i, 0)), pl.BlockSpec((None, ts, n), lambda j, i: (j, i, 0)),
                   pl.BlockSpec((None, 32, n), lambda j, i: (j, 0, 0)),
                   pl.BlockSpec((None, 8, n), lambda j, i: (j, 0, 0)), pl.BlockSpec((None, 8, n), lambda j, i: (j, 0, 0))),
        out_shape=(jax.ShapeDtypeStruct((half, S, n), BF16), jax.ShapeDtypeStruct((half, S, n), BF16),
                   jax.ShapeDtypeStruct((half, 32, n), F32),
                   jax.ShapeDtypeStruct((half, 8, n), F32), jax.ShapeDtypeStruct((half, 8, n), F32)),
        scratch_shapes=[pltpu.VMEM((ts + CONV_HALO, n), F32), pltpu.VMEM((SUBLANES, ts + CONV_HALO - SUBLANES, lc), F32)],
        compiler_params=_params(("parallel", "arbitrary")), name=name)(dc, dc, u, h_sm, h_sm, w)
    return da, dg, dw, dba, dbg


ROW_CHUNK = 16


def _ffn_gc(win_ref, w_ref, b_ref, r0, rows, kw, base):
    gc = b_ref[...] + jnp.zeros((rows, win_ref.shape[1]), F32)
    for k in range(kw):
        off = r0 + base - (kw - 1) + k
        gc = gc + w_ref[k:k + 1, :] * win_ref[off:off + rows, :]
    return gc


def ffn_up_gate_fwd(name, xb, w_up, w_gate, cw4, cb4, layer, tm=256):
    S, K = xb.shape
    n = w_up.shape[-1]
    kw = cw4.shape[2]
    tm = _tile(S, tm, ROW_CHUNK)
    rc = ROW_CHUNK

    def body(*refs):
        x_ref, xh_ref = refs[0:2]
        wu, wg, cw, cb = refs[2:6], refs[6:10], refs[10:14], refs[14:18]
        up_ref, gp_ref, gc_ref, hf_ref, win_ref = refs[18:]
        i = pl.program_id(0)
        xv = x_ref[...]
        xhalo = xh_ref[...]
        for j in range(N_CHIPS):
            wgj = wg[j][...]
            gp = jnp.dot(xv, wgj, preferred_element_type=F32)
            up = jnp.dot(xv, wu[j][...], preferred_element_type=F32)
            gph = jnp.dot(xhalo, wgj, preferred_element_type=F32)
            gpb = gp.astype(BF16)
            gp_ref[j] = gpb
            up_ref[j] = up.astype(BF16)
            win_ref[0:FFN_HALO, :] = jnp.where(i == 0, 0.0, gph.astype(BF16).astype(F32))
            win_ref[FFN_HALO:, :] = gpb.astype(F32)
            for r0 in range(0, tm, rc):
                gc = _ffn_gc(win_ref, cw[j], cb[j], r0, rc, kw, FFN_HALO)
                gc_ref[j, r0:r0 + rc, :] = gc.astype(BF16)
                hf_ref[j, r0:r0 + rc, :] = (gc * _sigmoid(gc) * up_ref[j, r0:r0 + rc, :].astype(F32)).astype(BF16)

    r = tm // FFN_HALO
    out = pl.BlockSpec((N_CHIPS, tm, n), lambda i: (0, i, 0))
    sds = jax.ShapeDtypeStruct((N_CHIPS, S, n), BF16)
    specs = [pl.BlockSpec((tm, K), lambda i: (i, 0)), pl.BlockSpec((FFN_HALO, K), lambda i: (jnp.maximum(i * r - 1, 0), 0))]
    specs += [pl.BlockSpec((None, None, K, n), lambda i, j=j: (j, layer, 0, 0)) for j in range(N_CHIPS)] * 2
    specs += [pl.BlockSpec((None, None, kw, n), lambda i, j=j: (j, layer, 0, 0)) for j in range(N_CHIPS)]
    specs += [pl.BlockSpec((None, None, 1, n), lambda i, j=j: (layer, j, 0, 0)) for j in range(N_CHIPS)]
    return pl.pallas_call(body, grid=(S // tm,), in_specs=specs, out_specs=(out, out, out, out), out_shape=(sds, sds, sds, sds),
                          scratch_shapes=[pltpu.VMEM((tm + FFN_HALO, n), F32)],
                          compiler_params=_params(("parallel",)), name=name)(
                              xb, xb, *([w_up] * N_CHIPS), *([w_gate] * N_CHIPS), *([cw4] * N_CHIPS), *([cb4] * N_CHIPS))


def ffn_gate_conv_bwd(name, dr, w_down, cw4, layer, up_sm, gc_sm, gp_sm, tm=256):
    n, N = w_down.shape[-2:]
    S = dr.shape[0]
    kw = cw4.shape[2]
    tm = _tile(S, tm, ROW_CHUNK)
    rc = ROW_CHUNK
    nblk = S // tm

    def dgc_of(dhf, up, gc):
        sg = _sigmoid(gc)
        return dhf * up * (sg * (1.0 + gc * (1.0 - sg))), sg

    def body(*refs):
        dr_ref, drn_ref = refs[0:2]
        wd, cw = refs[2:6], refs[6:10]
        up_ref, gc_ref, gp_ref, upn_ref, gcn_ref, dup_ref, dgp_ref, acc_ref, win_ref = refs[10:]
        i = pl.program_id(0)

        @pl.when(i == 0)
        def _():
            acc_ref[...] = jnp.zeros_like(acc_ref)
        a = dr_ref[...].astype(BF16)
        an = drn_ref[...].astype(BF16)
        for j in range(N_CHIPS):
            wj = wd[j][...]
            dhf = lax.dot_general(a, wj, NT, preferred_element_type=F32)
            gc = gc_ref[j].astype(F32)
            dgc, sg = dgc_of(dhf, up_ref[j].astype(F32), gc)
            dup_ref[j] = (dhf * gc * sg).astype(BF16)
            win_ref[0:tm, :] = dgc
            dhfn = lax.dot_general(an, wj, NT, preferred_element_type=F32)
            dgcn, _ = dgc_of(dhfn, upn_ref[j].astype(F32), gcn_ref[j].astype(F32))
            win_ref[tm:, :] = jnp.where(i == nblk - 1, 0.0, dgcn)
            sums = [jnp.zeros((8, n), F32) for _ in range(kw + 1)]
            for r0 in range(0, tm, rc):
                gp = gp_ref[j, r0:r0 + rc, :].astype(F32)
                dgp = jnp.zeros((rc, n), F32)
                for k in range(kw):
                    d = kw - 1 - k
                    shifted = win_ref[r0 + d:r0 + d + rc, :]
                    dgp = dgp + cw[j][k:k + 1, :] * shifted
                    prod = shifted * gp
                    sums[k] = sums[k] + prod[0:8, :] + prod[8:16, :]
                    if d == 0:
                        sums[kw] = sums[kw] + shifted[0:8, :] + shifted[8:16, :]
                dgp_ref[j, r0:r0 + rc, :] = dgp.astype(BF16)
            for k in range(kw):
                acc_ref[j, k:k + 1, :] += jnp.sum(sums[k], axis=0, keepdims=True)
            acc_ref[j, 7:8, :] += jnp.sum(sums[kw], axis=0, keepdims=True)

    r = tm // FFN_HALO
    nxt_row = lambda i: jnp.minimum((i + 1) * r, S // FFN_HALO - 1)
    blk = pl.BlockSpec((N_CHIPS, tm, n), lambda i: (0, i, 0))
    halo = pl.BlockSpec((N_CHIPS, FFN_HALO, n), lambda i: (0, nxt_row(i), 0))
    sds = jax.ShapeDtypeStruct((N_CHIPS, S, n), BF16)
    specs = [pl.BlockSpec((tm, N), lambda i: (i, 0)), pl.BlockSpec((FFN_HALO, N), lambda i: (nxt_row(i), 0))]
    specs += [pl.BlockSpec((None, None, n, N), lambda i, j=j: (j, layer, 0, 0)) for j in range(N_CHIPS)]
    specs += [pl.BlockSpec((None, None, kw, n), lambda i, j=j: (j, layer, 0, 0)) for j in range(N_CHIPS)]
    specs += [blk, blk, blk, halo, halo]
    return pl.pallas_call(body, grid=(nblk,), in_specs=specs,
                          out_specs=(blk, blk, pl.BlockSpec((N_CHIPS, 8, n), lambda i: (0, 0, 0))),
                          out_shape=(sds, sds, jax.ShapeDtypeStruct((N_CHIPS, 8, n), F32)),
                          scratch_shapes=[pltpu.VMEM((tm + FFN_HALO, n), F32)],
                          compiler_params=_params(("arbitrary",)), name=name)(
                              dr, dr, *([w_down] * N_CHIPS), *([cw4] * N_CHIPS), up_sm, gc_sm, gp_sm, up_sm, gc_sm)


def _neg_softplus(z):
    e = jnp.exp(-jnp.abs(z))
    return -(jnp.maximum(z, 0.0) + jnp.log(1.0 + e)), e


def _split_dot(x, t):
    hi = x.astype(BF16)
    lo = (x - hi.astype(F32)).astype(BF16)
    return jnp.dot(hi, t, preferred_element_type=F32) + jnp.dot(lo, t, preferred_element_type=F32)


STICK_GONE = -100.0
NOT_SWEPT = -1e30


def attn_fwd(name, q, k, v, bq=512, w=256):
    S, D = q.shape
    dh = HEAD_DIM
    hpb = LANES // dh
    bq = _tile(S, bq)
    w = _tile(bq, w)
    nsub = bq // w
    nkb = S // w

    def body(q_ref, k_ref, v_ref, o_ref, runs_ref, rs_ref):
        qi = pl.program_id(1)
        rr = lax.broadcasted_iota(jnp.int32, (w, w), 0)
        cc = lax.broadcasted_iota(jnp.int32, (w, w), 1)
        t_suf = (rr >= cc).astype(BF16)
        tq = qi * bq + lax.broadcasted_iota(jnp.int32, (bq, w), 0)
        tk = lax.broadcasted_iota(jnp.int32, (bq, w), 1)
        lane = lax.broadcasted_iota(jnp.int32, (bq, LANES), 1)
        ntot = (qi + 1) * nsub
        heads = [slice(hh * dh, (hh + 1) * dh) for hh in range(hpb)]
        qbs = [q_ref[:, hs] for hs in heads]
        for hh in range(hpb):
            rs_ref[hh] = jnp.where(lane < ntot, NOT_SWEPT, 0.0)

        def block(kb, carry, masked):
            kstart = pl.multiple_of(kb * w, w)
            if masked:
                m = (tk + kstart) < tq
            out = []
            for hh, hs in enumerate(heads):
                run, acc = carry[2 * hh], carry[2 * hh + 1]
                kblk = k_ref[pl.ds(kstart, w), hs]
                vblk = v_ref[pl.ds(kstart, w), hs]
                z = lax.dot_general(qbs[hh], kblk, NT, preferred_element_type=F32)
                lg, _ = _neg_softplus(z)
                if masked:
                    lg = jnp.where(m, lg, 0.0)
                cum = _split_dot(lg, t_suf) + run
                a = jnp.exp(z + cum)
                if masked:
                    a = jnp.where(m, a, 0.0)
                acc = acc + jnp.dot(a.astype(BF16), vblk, preferred_element_type=F32)
                run = cum[:, 0:1]
                rs_ref[hh] = jnp.where(lane == kb, run, rs_ref[hh])
                out += [run, acc]
            return tuple(out)

        carry = (jnp.zeros((bq, 1), F32), jnp.zeros((bq, dh), F32)) * hpb
        for sb in reversed(range(nsub)):
            carry = block(qi * nsub + sb, carry, True)

        def cond(c):
            alive = functools.reduce(jnp.maximum, [jnp.max(c[1 + 2 * hh]) for hh in range(hpb)])
            return jnp.logical_and(c[0] >= 0, alive > STICK_GONE)

        def step(c):
            return (c[0] - 1,) + block(c[0], c[1:], False)
        carry = lax.while_loop(cond, step, (qi * nsub - 1,) + carry)[1:]
        for hh, hs in enumerate(heads):
            o_ref[:, hs] = carry[2 * hh + 1].astype(o_ref.dtype)
            runs_ref[hh] = rs_ref[hh, :, 0:nkb]

    qs = pl.BlockSpec((bq, LANES), lambda h, i: (i, h))
    kv = pl.BlockSpec((S, LANES), lambda h, i: (0, h))
    return pl.pallas_call(body, grid=(D // LANES, S // bq), in_specs=[qs, kv, kv],
                          out_specs=(qs, pl.BlockSpec((hpb, bq, nkb), lambda h, i: (h, i, 0))),
                          out_shape=(jax.ShapeDtypeStruct((S, D), BF16), jax.ShapeDtypeStruct((D // dh, S, nkb), F32)),
                          scratch_shapes=[pltpu.VMEM((hpb, bq, LANES), F32)],
                          compiler_params=_params(("parallel", "parallel")), name=name)(q, k, v)


def attn_bwd(name, q, k, v, do, runs, dk0=None, dv0=None, bq=512, w=256):
    S, D = q.shape
    dh = HEAD_DIM
    hpb = LANES // dh
    bq = _tile(S, bq)
    w = _tile(bq, w)
    nsub = bq // w
    nkb = S // w
    scale = 1.0 / math.sqrt(dh)
    init = dk0 is not None

    def body(*refs):
        if init:
            q_ref, k_ref, v_ref, do_ref, runs_ref, dk0_ref, dv0_ref, dq_ref, dk_ref, dv_ref, rs_ref = refs
        else:
            q_ref, k_ref, v_ref, do_ref, runs_ref, dq_ref, dk_ref, dv_ref, rs_ref = refs
        qi = pl.program_id(1)

        @pl.when(qi == 0)
        def _():
            dk_ref[...] = dk0_ref[...] if init else jnp.zeros_like(dk_ref)
            dv_ref[...] = dv0_ref[...] if init else jnp.zeros_like(dv_ref)

        rr = lax.broadcasted_iota(jnp.int32, (w, w), 0)
        cc = lax.broadcasted_iota(jnp.int32, (w, w), 1)
        t_suf = (rr >= cc).astype(BF16)
        t_pre = (rr <= cc).astype(BF16)
        tq = qi * bq + lax.broadcasted_iota(jnp.int32, (bq, w), 0)
        tk = lax.broadcasted_iota(jnp.int32, (bq, w), 1)
        lane = lax.broadcasted_iota(jnp.int32, (bq, LANES), 1)
        lane1 = lax.broadcasted_iota(jnp.int32, (1, LANES), 1)
        ntot = (qi + 1) * nsub
        heads = [slice(hh * dh, (hh + 1) * dh) for hh in range(hpb)]
        qbs = [q_ref[:, hs] for hs in heads]
        dobs = [do_ref[:, hs].astype(BF16) for hs in heads]
        kb0 = ntot - nsub
        for hh in range(hpb):
            rs_ref[hh] = jnp.zeros((bq, LANES), F32)
            rs_ref[hh, :, 0:nkb] = runs_ref[hh]
            colmax = jnp.max(rs_ref[hh], axis=0, keepdims=True)
            dead = jnp.logical_and(jnp.logical_and(lane1 >= 1, lane1 <= ntot), colmax <= STICK_GONE)
            kb0 = jnp.minimum(kb0, jnp.sum(dead.astype(jnp.int32)))

        def block(kb, carry, masked):
            kstart = pl.multiple_of(kb * w, w)
            if masked:
                m = (tk + kstart) < tq
            out = []
            for hh, hs in enumerate(heads):
                pg_run, dq = carry[2 * hh], carry[2 * hh + 1]
                qb, dob = qbs[hh], dobs[hh]
                kblk = k_ref[pl.ds(kstart, w), hs]
                vblk = v_ref[pl.ds(kstart, w), hs]
                right = jnp.sum(jnp.where(lane == kb + 1, rs_ref[hh], 0.0), axis=1, keepdims=True)
                z = lax.dot_general(qb, kblk, NT, preferred_element_type=F32)
                lg, _ = _neg_softplus(z)
                sig = 1.0 - jnp.exp(lg)
                if masked:
                    lg = jnp.where(m, lg, 0.0)
                a = jnp.exp(z + _split_dot(lg, t_suf) + right)
                if masked:
                    a = jnp.where(m, a, 0.0)
                da = lax.dot_general(dob, vblk, NT, preferred_element_type=F32)
                g = da * a
                pin = _split_dot(g, t_pre) + pg_run
                dz = g - sig * pin
                if masked:
                    dz = jnp.where(m, dz, 0.0)
                dzb = dz.astype(BF16)
                dq = dq + jnp.dot(dzb, kblk, preferred_element_type=F32)
                dk_ref[pl.ds(kstart, w), hs] += lax.dot_general(dzb, qb, TN, preferred_element_type=F32)
                dv_ref[pl.ds(kstart, w), hs] += lax.dot_general(a.astype(BF16), dob, TN, preferred_element_type=F32)
                out += [pin[:, w - 1:w], dq]
            return tuple(out)

        carry = (jnp.zeros((bq, 1), F32), jnp.zeros((bq, dh), F32)) * hpb
        carry = lax.fori_loop(kb0, qi * nsub, lambda kb, c: block(kb, c, False), carry)
        for sb in range(nsub):
            carry = block(qi * nsub + sb, carry, True)
        for hh, hs in enumerate(heads):
            dq_ref[:, hs] = carry[2 * hh + 1] * scale

    qs = pl.BlockSpec((bq, LANES), lambda h, i: (i, h))
    kv = pl.BlockSpec((S, LANES), lambda h, i: (0, h))
    ins = [q, k, v, do, runs] + ([dk0, dv0] if init else [])
    specs = [qs, kv, kv, qs, pl.BlockSpec((hpb, bq, nkb), lambda h, i: (h, i, 0))] + ([kv, kv] if init else [])
    sds = jax.ShapeDtypeStruct((S, D), F32)
    return pl.pallas_call(body, grid=(D // LANES, S // bq), in_specs=specs, out_specs=(qs, kv, kv), out_shape=(sds, sds, sds),
                          scratch_shapes=[pltpu.VMEM((hpb, bq, LANES), F32)],
                          compiler_params=_params(("parallel", "arbitrary")), name=name)(*ins)


def loss_head(name, y, tgt, tm=512):
    S, D = y.shape
    tm = _tile(S, tm)

    def body(y_ref, t_ref, dy_ref, acc_ref):
        @pl.when(pl.program_id(0) == 0)
        def _():
            acc_ref[...] = jnp.zeros_like(acc_ref)
        e = y_ref[...] - t_ref[...]
        dy_ref[...] = e * (1.0 / D)
        acc_ref[...] += jnp.sum(e * e)

    row = pl.BlockSpec((tm, D), lambda i: (i, 0))
    return pl.pallas_call(body, grid=(S // tm,), in_specs=[row, row],
                          out_specs=(row, pl.BlockSpec((8, LANES), lambda i: (0, 0))),
                          out_shape=(jax.ShapeDtypeStruct((S, D), F32), jax.ShapeDtypeStruct((8, LANES), F32)),
                          compiler_params=_params(("arbitrary",)), name=name)(y, tgt)


def _to_sm(a, axis=-1):
    axis = axis % a.ndim
    shp = a.shape[:axis] + (N_CHIPS, a.shape[axis] // N_CHIPS) + a.shape[axis + 1:]
    return jnp.moveaxis(a.reshape(shp), axis, 0)


def _from_sm(a, axis=-1):
    nd = a.ndim - 1
    axis = axis % nd
    b = jnp.moveaxis(a, 0, axis)
    return b.reshape(b.shape[:axis] + (b.shape[axis] * b.shape[axis + 1],) + b.shape[axis + 2:])


class GradBuffers:
    def __init__(self, W):
        groups = {}
        for n in BIG:
            _, layers, rows, cols = W[n].shape
            groups.setdefault(cols, []).append((rows, n, layers))
        self.where, self.cols_of, self.buf, self.members = {}, {}, {}, {}
        for cols, items in groups.items():
            off, members = 0, []
            for rows, n, layers in sorted(items, key=lambda t: -t[0]):
                assert off % rows == 0
                self.where[n], self.cols_of[n] = (off, rows), cols
                members.append((n, off, rows * layers))
                off += rows * layers
            assert off % 32 == 0
            self.buf[cols] = lax.empty((N_CHIPS, off, cols), BF16)
            self.members[cols] = members

    def put(self, n, layer, fn, **kw):
        cols = self.cols_of[n]
        off, rows = self.where[n]
        self.buf[cols] = fn(into=self.buf[cols], off=off + layer * rows, **kw)


def forward_backward(x, p4, tgt, W):
    S, D = x.shape
    scale = 1.0 / math.sqrt(HEAD_DIM)
    saved = []
    kh = vh = xb_kv = None
    xb = x.astype(BF16)
    for i in range(DEPTH):
        sv = {'xb': xb}
        if i < N_A:
            h_sm, u = mm_pw1_glu(f"pw1glu_{i}", xb, W['a_pw1_w'], i, W['a_pw1_b'])
            c, s = conv_ln_silu_fwd(f"convln_{i}", u, W['a_dw_w'], W['a_dw_b'], W['a_ln_g'], W['a_ln_b'], i)
            x1, x1b, xh1, rs1 = mm_rowsm_ln(f"pw2ln_{i}", s, W['a_pw2_w'], i, x, W['ln_mix_g'], W['ln_mix_b'], i, bias=W['a_pw2_b'])
            sv.update(h_sm=h_sm, u=u, c=c, s=s)
        else:
            j = i - N_A
            if kh is None:
                xb_kv = xb
                kh = mm_rowsm("wk", xb, W['kv_wk'], 0, out_dtype=BF16)
                vh = mm_rowsm("wv", xb, W['kv_wv'], 0, out_dtype=BF16)
            qh = mm_rowsm(f"wq_{j}", xb, W['b_wq'], j, out_dtype=BF16, out_scale=scale)
            o, runs = attn_fwd(f"attn_{j}", qh, kh, vh)
            x1, x1b, xh1, rs1 = mm_rowsm_ln(f"woln_{j}", o, W['b_wo'], j, x, W['ln_mix_g'], W['ln_mix_b'], i)
            sv.update(qh=qh, o=o, runs=runs)
        up_sm, gp_sm, gc_sm, hf_sm = ffn_up_gate_fwd(f"upgate_{i}", x1b, W['ffn_w_up'], W['ffn_w_gate'], W['ffn_conv_w'],
                                                     W['ffn_conv_b'], i)
        tg = mm_rowsm(f"plegate_{i}", x1b, W['ple_w_gate'], i)
        pp = mm_proj(f"pleproj_{i}", p4, i, W['ple_w_proj'])
        x2, x2b, xh2, rs2 = mm_rowsm_ln(f"downln_{i}", hf_sm, W['ffn_w_down'], i, x1, W['ln_ffn_g'], W['ln_ffn_b'], i, a_sm=True,
                                        tg=tg, pp=pp)
        sv.update(x1b=x1b, xh1=xh1, rs1=rs1, up_sm=up_sm, gp_sm=gp_sm, gc_sm=gc_sm, hf_sm=hf_sm, tg=tg, pp=pp, xh2=xh2, rs2=rs2)
        saved.append(sv)
        x, xb = x2, x2b

    dx, lacc = loss_head("loss", x, tgt)
    loss_sum = lacc[0, 0]

    G = {n: [None] * DEPTH for n in WEIGHTS if n not in BIG}
    gb = GradBuffers(W)
    dk = dv = None
    for i in reversed(range(DEPTH)):
        sv = saved[i]
        dr, acc, dtg, dpp = ln_bwd(f"lnffn_b_{i}", dx, sv['xh2'], sv['rs2'], W['ln_ffn_g'], i, tg=sv['tg'], pp=sv['pp'])
        G['ln_ffn_g'][i], G['ln_ffn_b'][i] = acc[0], acc[1]
        gb.put('ple_w_proj', i, functools.partial(mm_tn_proj, f"dproj_{i}", p4, i, dpp))
        gb.put('ple_w_gate', i, functools.partial(mm_tn_rowsm_fan, f"dplegate_{i}", sv['x1b'], dtg))
        gb.put('ffn_w_down', i, functools.partial(mm_tn_rowsm, f"ddown_{i}", sv['hf_sm'], dr, a_sm=True))
        dup_sm, dgp_sm, cacc = ffn_gate_conv_bwd(f"gateconv_b_{i}", dr, W['ffn_w_down'], W['ffn_conv_w'], i, sv['up_sm'],
                                                 sv['gc_sm'], sv['gp_sm'])
        kw = W['ffn_conv_w'].shape[2]
        G['ffn_conv_w'][i] = cacc[:, 0:kw, :]
        G['ffn_conv_b'][i] = cacc[:, 7, :].reshape(-1)
        gb.put('ffn_w_up', i, functools.partial(mm_tn_colsm, f"dup_{i}", sv['x1b'], dup_sm))
        gb.put('ffn_w_gate', i, functools.partial(mm_tn_colsm, f"dgate_{i}", sv['x1b'], dgp_sm))
        dx1 = mm_nt_rowsm(f"dx1a_{i}", dtg, W['ple_w_gate'], i, res=dr, res_alpha=DN_ALPHA)
        dx1 = mm_nt_colsm(f"dx1b_{i}", _sm_parts(dup_sm), W['ffn_w_up'], i, res=dx1)
        dx1 = mm_nt_colsm(f"dx1c_{i}", _sm_parts(dgp_sm), W['ffn_w_gate'], i, res=dx1)

        dr1, acc1 = ln_bwd(f"lnmix_b_{i}", dx1, sv['xh1'], sv['rs1'], W['ln_mix_g'], i)
        G['ln_mix_g'][i], G['ln_mix_b'][i] = acc1[0], acc1[1]
        xin = sv['xb']
        if i < N_A:
            G['a_pw2_b'][i] = acc1[2]
            gb.put('a_pw2_w', i, functools.partial(mm_tn_rowsm_fan, f"dpw2_{i}", sv['s'], dr1))
            ds = mm_nt_rowsm(f"ds_{i}", dr1, W['a_pw2_w'], i)
            dc, cacc = ln_silu_bwd(f"lnsilu_b_{i}", ds, sv['c'], W['a_ln_g'], W['a_ln_b'], i)
            G['a_ln_g'][i], G['a_ln_b'][i], G['a_dw_b'][i] = cacc[0], cacc[1], cacc[2]
            da, dg, dw, dba, dbg = conv_glu_bwd(f"convglu_b_{i}", dc, sv['u'], sv['h_sm'], W['a_dw_w'], i)
            kw = W['a_dw_w'].shape[1]
            G['a_dw_w'][i] = _from_sm(dw[:, 0:kw, :], axis=-1)
            G['a_pw1_b'][i] = jnp.concatenate([dba[:, 0, :], dbg[:, 0, :]], axis=0)
            half = da.shape[0]
            gb.put('a_pw1_w', i, functools.partial(mm_tn_colsm, f"dpw1a_{i}", xin, da))
            gb.put('a_pw1_w', i, functools.partial(mm_tn_colsm, f"dpw1g_{i}", xin, dg), j0=half)
            dx = mm_nt_colsm(f"dxa_{i}", _sm_parts(da) + _sm_parts(dg), W['a_pw1_w'], i, res=dr1, res_alpha=DN_ALPHA)
        else:
            j = i - N_A
            gb.put('b_wo', j, functools.partial(mm_tn_rowsm_fan, f"dwo_{j}", sv['o'], dr1))
            do = mm_nt_rowsm(f"do_{j}", dr1, W['b_wo'], j)
            dq, dk, dv = attn_bwd(f"attn_b_{j}", sv['qh'], kh, vh, do, sv['runs'], dk, dv)
            gb.put('b_wq', j, functools.partial(mm_tn_rowsm_fan, f"dwq_{j}", xin, dq))
            dx = mm_nt_rowsm(f"dxq_{j}", dq, W['b_wq'], j, res=dr1, res_alpha=DN_ALPHA)
            if j == 0:
                gb.put('kv_wk', 0, functools.partial(mm_tn_rowsm_fan, "dwk", xb_kv, dk))
                gb.put('kv_wv', 0, functools.partial(mm_tn_rowsm_fan, "dwv", xb_kv, dv))
                dx = mm_nt_rowsm("dxk", dk, W['kv_wk'], 0, res=dx)
                dx = mm_nt_rowsm("dxv", dv, W['kv_wv'], 0, res=dx)
    return loss_sum, dx, G, gb


MESH = pl.DeviceIdType.MESH
HBM = pl.BlockSpec(memory_space=pltpu.HBM)


def _place():
    x, y, c = lax.axis_index("x"), lax.axis_index("y"), lax.axis_index("c")
    others = [(1 - x, y), (x, 1 - y), (1 - x, 1 - y)]
    return x, y, c, others


def allgather_chips(name, arrs):
    n = len(arrs)

    def body(*refs):
        ins, outs = refs[:n], refs[n:2 * n]
        send_sems, recv_sems = refs[2 * n:]
        x, y, c, others = _place()
        me = 2 * x + y
        sibling = (x, y, 1 - c)
        ids = [2 * ch[0] + ch[1] for ch in others]
        from_id = jnp.where(c == 0, ids[0], ids[1])
        to_chip = (jnp.where(c == 0, x, 1 - x), jnp.where(c == 0, 1 - y, y))

        def remote(a, k, src, chip_id, half, to):
            return pltpu.make_async_remote_copy(src_ref=src, dst_ref=outs[a].at[chip_id, half], send_sem=send_sems.at[a, k],
                                                recv_sem=recv_sems.at[a, k], device_id=to, device_id_type=MESH)

        sent = [remote(a, k, ins[a].at[c], me, c, (others[k][0], others[k][1], c)) for a in range(n) for k in range(2)]
        for cp in sent:
            cp.start()
        for a in range(n):
            for k in range(2):
                remote(a, k, ins[a].at[c], ids[k], c, sibling).wait_recv()
            sent.append(remote(a, 2, outs[a].at[from_id, c], from_id, c, (to_chip[0], to_chip[1], c)))
            sent[-1].start()
            for k in range(2):
                sent.append(remote(a, 3 + k, outs[a].at[ids[k], c], ids[k], c, sibling))
                sent[-1].start()
        for a in range(n):
            remote(a, 2, ins[a].at[c], ids[2], c, sibling).wait_recv()
            sent.append(remote(a, 5, outs[a].at[ids[2], c], ids[2], c, sibling))
            sent[-1].start()
        for a in range(n):
            for k in range(3):
                remote(a, 3 + k, ins[a].at[c], ids[k], 1 - c, sibling).wait_recv()
        for cp in sent:
            cp.wait_send()

    outs = pl.pallas_call(body, out_shape=tuple(jax.ShapeDtypeStruct((N_CHIPS,) + a.shape, a.dtype) for a in arrs),
                          in_specs=[HBM] * n, out_specs=tuple([HBM] * n),
                          scratch_shapes=[pltpu.SemaphoreType.DMA((n, 6)), pltpu.SemaphoreType.DMA((n, 6))],
                          name=name)(*arrs)
    me = 2 * lax.axis_index("x") + lax.axis_index("y")
    return [lax.dynamic_update_index_in_dim(o, a, me, 0) for o, a in zip(outs, arrs)]


def exchange_sibling(name, gs):
    n = len(gs)

    def body(*refs):
        g_refs, o_refs = refs[:n], refs[n:2 * n]
        send_sems, recv_sems = refs[2 * n:]
        x, y, c, _ = _place()
        cps = [pltpu.make_async_remote_copy(src_ref=g_refs[a].at[j, 1 - c], dst_ref=o_refs[a].at[j], send_sem=send_sems.at[a, j],
                                            recv_sem=recv_sems.at[a, j], device_id=(x, y, 1 - c), device_id_type=MESH)
               for a in range(n) for j in range(N_CHIPS)]
        for cp in cps:
            cp.start()
        for cp in cps:
            cp.wait()

    return pl.pallas_call(body, out_shape=tuple(jax.ShapeDtypeStruct((N_CHIPS,) + g.shape[2:], g.dtype) for g in gs),
                          in_specs=[HBM] * n, out_specs=tuple([HBM] * n),
                          scratch_shapes=[pltpu.SemaphoreType.DMA((n, N_CHIPS)), pltpu.SemaphoreType.DMA((n, N_CHIPS))],
                          name=name)(*gs)


def _ring_peers():
    x, y, c, _ = _place()
    first = (jnp.where(c == 0, 1 - x, x), jnp.where(c == 0, y, 1 - y))
    second = (jnp.where(c == 0, x, 1 - x), jnp.where(c == 0, 1 - y, y))
    return c, first, second, 2 * (1 - x) + (1 - y)


def exchange_first(name, ss):
    n = len(ss)

    def body(*refs):
        s_refs, o_refs = refs[:n], refs[n:2 * n]
        send_sems, recv_sems = refs[2 * n:]
        c, first, _, diag = _ring_peers()
        cps = [pltpu.make_async_remote_copy(src_ref=s_refs[a].at[slot], dst_ref=o_refs[a].at[k], send_sem=send_sems.at[a, k],
                                            recv_sem=recv_sems.at[a, k], device_id=(first[0], first[1], c), device_id_type=MESH)
               for a in range(n) for k, slot in enumerate((2 * first[0] + first[1], diag))]
        for cp in cps:
            cp.start()
        for cp in cps:
            cp.wait()

    return pl.pallas_call(body, out_shape=tuple(jax.ShapeDtypeStruct((2,) + s.shape[1:], s.dtype) for s in ss),
                          in_specs=[HBM] * n, out_specs=tuple([HBM] * n),
                          scratch_shapes=[pltpu.SemaphoreType.DMA((n, 2)), pltpu.SemaphoreType.DMA((n, 2))], name=name)(*ss)


def exchange_second(name, ts):
    n = len(ts)

    def body(*refs):
        t_refs, o_refs = refs[:n], refs[n:2 * n]
        send_sems, recv_sems = refs[2 * n:]
        c, _, second, _ = _ring_peers()
        cps = [pltpu.make_async_remote_copy(src_ref=t_refs[a], dst_ref=o_refs[a], send_sem=send_sems.at[a], recv_sem=recv_sems.at[a],
                                            device_id=(second[0], second[1], c), device_id_type=MESH) for a in range(n)]
        for cp in cps:
            cp.start()
        for cp in cps:
            cp.wait()

    return pl.pallas_call(body, out_shape=tuple(jax.ShapeDtypeStruct(t.shape, t.dtype) for t in ts),
                          in_specs=[HBM] * n, out_specs=tuple([HBM] * n),
                          scratch_shapes=[pltpu.SemaphoreType.DMA((n,)), pltpu.SemaphoreType.DMA((n,))], name=name)(*ts)


def share_sibling(name, ts):
    n = len(ts)

    def body(*refs):
        o_refs = refs[n:2 * n]
        send_sems, recv_sems = refs[2 * n:]
        x, y, c, _ = _place()
        cps = [pltpu.make_async_remote_copy(src_ref=o_refs[a].at[c], dst_ref=o_refs[a].at[c], send_sem=send_sems.at[a],
                                            recv_sem=recv_sems.at[a], device_id=(x, y, 1 - c), device_id_type=MESH)
               for a in range(n)]
        for cp in cps:
            cp.start()
        for a in range(n):
            pltpu.make_async_remote_copy(src_ref=o_refs[a].at[c], dst_ref=o_refs[a].at[1 - c], send_sem=send_sems.at[a],
                                         recv_sem=recv_sems.at[a], device_id=(x, y, 1 - c), device_id_type=MESH).wait_recv()
        for cp in cps:
            cp.wait_send()

    return pl.pallas_call(body, out_shape=tuple(jax.ShapeDtypeStruct(t.shape, t.dtype) for t in ts),
                          in_specs=[HBM] * n, out_specs=tuple([HBM] * n), input_output_aliases={a: a for a in range(n)},
                          scratch_shapes=[pltpu.SemaphoreType.DMA((n,)), pltpu.SemaphoreType.DMA((n,))],
                          name=name)(*ts)


def add_halves(name, g, recv, place, out_dtype, tr=512):
    _, _, R, C = g.shape
    tr = _tile(R, tr, 16)

    def body(p_ref, a_ref, b_ref, o_ref):
        o_ref[...] = (a_ref[...].astype(F32) + b_ref[...].astype(F32)).astype(o_ref.dtype)

    blk = pl.BlockSpec((None, tr, C), lambda j, i, p: (j, i, 0))
    gs = pltpu.PrefetchScalarGridSpec(num_scalar_prefetch=1, grid=(N_CHIPS, R // tr),
                                      in_specs=[pl.BlockSpec((None, None, tr, C), lambda j, i, p: (j, p[0], i, 0)), blk],
                                      out_specs=blk)
    return pl.pallas_call(body, grid_spec=gs, out_shape=jax.ShapeDtypeStruct((N_CHIPS, R, C), out_dtype),
                          compiler_params=_params(("parallel", "parallel")), name=name)(place, g, recv)


def add_pass_on(name, s, got, place, tr=512):
    _, R, C = s.shape
    tr = _tile(R, tr, 16)

    def body(p_ref, a_ref, b_ref, o_ref):
        o_ref[...] = (a_ref[...].astype(F32) + b_ref[...].astype(F32)).astype(o_ref.dtype)

    gs = pltpu.PrefetchScalarGridSpec(num_scalar_prefetch=1, grid=(R // tr,),
                                      in_specs=[pl.BlockSpec((None, tr, C), lambda i, p: (p[2], i, 0)),
                                                pl.BlockSpec((None, tr, C), lambda i, p: (1, i, 0))],
                                      out_specs=pl.BlockSpec((tr, C), lambda i, p: (i, 0)))
    return pl.pallas_call(body, grid_spec=gs, out_shape=jax.ShapeDtypeStruct((R, C), s.dtype),
                          compiler_params=_params(("parallel",)), name=name)(place, s, got)


def add_chips(name, g, r1, got1, got2, place, tr=512):
    _, _, R, C = g.shape
    tr = _tile(R, tr, 16)

    def body(p_ref, a_ref, b_ref, c_ref, d_ref, o_ref):
        o_ref[...] = ((a_ref[...].astype(F32) + b_ref[...].astype(F32)) + c_ref[...].astype(F32)) + d_ref[...].astype(F32)

    gs = pltpu.PrefetchScalarGridSpec(num_scalar_prefetch=1, grid=(R // tr,),
                                      in_specs=[pl.BlockSpec((None, None, tr, C), lambda i, p: (p[1], p[0], i, 0)),
                                                pl.BlockSpec((None, tr, C), lambda i, p: (p[1], i, 0)),
                                                pl.BlockSpec((None, tr, C), lambda i, p: (0, i, 0)),
                                                pl.BlockSpec((tr, C), lambda i, p: (i, 0))],
                                      out_specs=pl.BlockSpec((None, tr, C), lambda i, p: (p[0], i, 0)))
    return pl.pallas_call(body, grid_spec=gs, out_shape=jax.ShapeDtypeStruct((2, R, C), F32),
                          compiler_params=_params(("parallel",)), name=name)(place, g, r1, got1, got2)


def reduce_scatter(gs, wire_dtypes, place):
    r1 = exchange_sibling("rs_sibling", gs)
    s1 = [add_halves(f"rs_add_cores_{a}", g, r, place, dt) for a, (g, r, dt) in enumerate(zip(gs, r1, wire_dtypes))]
    got1 = exchange_first("rs_first", s1)
    t = [add_pass_on(f"rs_add_pass_{a}", s, g1, place) for a, (s, g1) in enumerate(zip(s1, got1))]
    got2 = exchange_second("rs_second", t)
    tot = [add_chips(f"rs_add_chips_{a}", g, r, g1, g2, place) for a, (g, r, g1, g2) in enumerate(zip(gs, r1, got1, got2))]
    return share_sibling("rs_share", tot)


def adamw(name, w, g, m, v, tr=512):
    shp = w.shape
    cols = shp[-1]
    w2, g2, m2, v2 = (a.reshape(-1, cols) for a in (w, g, m, v))
    rows = w2.shape[0]
    tr = _tile(rows, tr)

    def body(w_ref, g_ref, m_ref, v_ref, d_ref, mo_ref, vo_ref):
        g_ = g_ref[...]
        m_ = ADAM_B1 * m_ref[...] + (1.0 - ADAM_B1) * g_
        v_ = ADAM_B2 * v_ref[...] + (1.0 - ADAM_B2) * (g_ * g_)
        m_hat = m_ / (1.0 - ADAM_B1 ** ADAM_STEP)
        v_hat = v_ / (1.0 - ADAM_B2 ** ADAM_STEP)
        d_ref[...] = -ADAM_LR * (m_hat / (jnp.sqrt(v_hat) + ADAM_EPS) + ADAM_WD * w_ref[...])
        mo_ref[...] = m_
        vo_ref[...] = v_

    blk = pl.BlockSpec((tr, cols), lambda i: (i, 0))
    sds = jax.ShapeDtypeStruct((rows, cols), F32)
    d, mo, vo = pl.pallas_call(body, grid=(rows // tr,), in_specs=[blk] * 4, out_specs=(blk, blk, blk), out_shape=(sds, sds, sds),
                               compiler_params=_params(("parallel",)), name=name)(w2, g2, m2, v2)
    return d.reshape(shp), mo.reshape(shp), vo.reshape(shp)


PACK_ALIGN = 1024


def _pad_to(a, mult, axis=-1):
    axis = axis % a.ndim
    extra = (-a.shape[axis]) % mult
    if extra == 0:
        return a
    pads = [(0, 0)] * a.ndim
    pads[axis] = (0, extra)
    return jnp.pad(a, pads)


def _pack(pieces, lead, row_mult):
    nl = len(lead)
    flat, offs, sizes, off = [], [], [], 0
    for a in pieces:
        f = a.reshape(lead + (-1,))
        sizes.append(f.shape[-1])
        f = _pad_to(f, PACK_ALIGN)
        offs.append(off)
        off += f.shape[-1]
        flat.append(f)
    cat = _pad_to(jnp.concatenate(flat, axis=nl), 2 * row_mult * LANES)
    return cat.reshape(lead + (2, -1, LANES)), offs, sizes


def _unpack(packed, lead, offs, sizes, shapes):
    flat = packed.reshape(lead + (-1,))
    return [lax.slice_in_dim(flat, o, o + s, axis=len(lead)).reshape(lead + tuple(shp)) for o, s, shp in zip(offs, sizes, shapes)]


def _stack_grads(G, names):
    out = {}
    for n in names:
        parts = [g for g in G[n] if g is not None]
        if n in ('kv_wk', 'kv_wv'):
            out[n] = parts[0]
        elif n in REPLICATED:
            out[n] = jnp.stack(parts, axis=0).reshape(N_CHIPS, -1)
        elif n in ('a_dw_w', 'a_dw_b', 'a_ln_g', 'a_ln_b', 'a_pw2_b'):
            out[n] = _to_sm(jnp.stack(parts, axis=0), axis=-1)
        else:
            out[n] = jnp.stack(parts, axis=1)
    return out


def _whole_weights(big, small, rep, D):
    W = {}
    for n in BIG:
        a = big[n]
        W[n] = a[:, None] if n in ('kv_wk', 'kv_wv') else a
    W['a_pw1_b'] = small['a_pw1_b'][:, :, None, :]
    W['a_dw_w'] = _from_sm(small['a_dw_w'], axis=-1)
    for n in ('a_dw_b', 'a_ln_g', 'a_ln_b', 'a_pw2_b'):
        W[n] = _from_sm(small[n], axis=-1)[:, None, :]
    W['ffn_conv_w'] = small['ffn_conv_w']
    L, F = rep['ffn_conv_b'].shape
    W['ffn_conv_b'] = rep['ffn_conv_b'].reshape(L, N_CHIPS, 1, F // N_CHIPS)
    for n in ('ln_mix_g', 'ln_mix_b', 'ln_ffn_g', 'ln_ffn_b'):
        W[n] = rep[n][:, None, :]
    return W


SMALL = ('a_pw1_b', 'a_dw_w', 'a_dw_b', 'a_ln_g', 'a_ln_b', 'a_pw2_b', 'ffn_conv_w')


def _step(x, p, loss_target, w, m, v):
    S, D = x.shape[-2:]
    x2, tgt = x.reshape(S, D), loss_target.reshape(S, D)
    ax, ay, ac = lax.axis_index("x"), lax.axis_index("y"), lax.axis_index("c")
    second = jnp.where(ac == 0, 2 * ax + (1 - ay), 2 * (1 - ax) + ay)
    place = jnp.stack([ac, 2 * ax + ay, second]).astype(jnp.int32)

    big_in = [w[n].astype(BF16).reshape((2, -1) + w[n].shape[1:] if w[n].ndim == 3 else (2, -1, w[n].shape[-1])) for n in BIG]
    small_in, s_offs, s_sizes = _pack([w[n] for n in SMALL], (), 8)
    gathered = allgather_chips("gather_weights", big_in + [small_in])
    big = {n: g.reshape((N_CHIPS,) + w[n].shape) for n, g in zip(BIG, gathered[:-1])}
    small = dict(zip(SMALL, _unpack(gathered[-1], (N_CHIPS,), s_offs, s_sizes, [w[n].shape for n in SMALL])))
    W = _whole_weights(big, small, {n: w[n] for n in REPLICATED}, D)

    loss_sum, dx, G, gb = forward_backward(x2, p, tgt, W)
    loss = lax.psum(0.5 * loss_sum / D, ("x", "y", "c"))

    vectors = [n for n in WEIGHTS if n not in BIG]
    mats = [b.reshape(N_CHIPS, 2, b.shape[1] // 2, b.shape[2]) for b in gb.buf.values()]
    members = [gb.members[cols] for cols in gb.buf]
    g_sm = _stack_grads(G, vectors)
    packed, offs, sizes = _pack([g_sm[n] for n in vectors], (N_CHIPS,), 512)
    reduced = reduce_scatter(mats + [packed], [BF16] * len(mats) + [F32], place)
    shapes = [w[n].shape if n not in REPLICATED else (w[n].size // N_CHIPS,) for n in vectors]
    g_mine = dict(zip(vectors, _unpack(reduced[-1], (), offs, sizes, shapes)))
    for red, where in zip(reduced[:-1], members):
        rows = red.reshape(-1, red.shape[-1])
        for n, off, cnt in where:
            g_mine[n] = lax.slice_in_dim(rows, off, off + cnt, axis=0).reshape(w[n].shape)
    rep_in, r_offs, r_sizes = _pack([g_mine[n] for n in REPLICATED], (), 8)
    rep_all = allgather_chips("gather_replicated_grads", [rep_in])[0]
    for n, g in zip(REPLICATED, _unpack(rep_all, (N_CHIPS,), r_offs, r_sizes, [(w[n].size // N_CHIPS,) for n in REPLICATED])):
        g_mine[n] = g.reshape(w[n].shape)

    grads, deltas, new_m, new_v = [], [], [], []
    for n in WEIGHTS:
        d, mo, vo = adamw(f"adamw_{n}", w[n], g_mine[n], m[n], v[n])
        grads.append(g_mine[n])
        deltas.append(d)
        new_m.append(mo)
        new_v.append(vo)
    return (loss, dx.reshape(x.shape), *grads, *deltas, *new_m, *new_v)


def kernel(x, p, a_pw1_w, a_pw1_b, a_dw_w, a_dw_b, a_ln_g, a_ln_b, a_pw2_w, a_pw2_b, b_wq, kv_wk, kv_wv, b_wo, ln_mix_g, ln_mix_b, ffn_w_up, ffn_w_gate, ffn_conv_w, ffn_conv_b, ffn_w_down, ple_w_gate, ple_w_proj, ln_ffn_g, ln_ffn_b, loss_target, m_a_pw1_w, m_a_pw1_b, m_a_dw_w, m_a_dw_b, m_a_ln_g, m_a_ln_b, m_a_pw2_w, m_a_pw2_b, m_b_wq, m_kv_wk, m_kv_wv, m_b_wo, m_ln_mix_g, m_ln_mix_b, m_ffn_w_up, m_ffn_w_gate, m_ffn_conv_w, m_ffn_conv_b, m_ffn_w_down, m_ple_w_gate, m_ple_w_proj, m_ln_ffn_g, m_ln_ffn_b, v_a_pw1_w, v_a_pw1_b, v_a_dw_w, v_a_dw_b, v_a_ln_g, v_a_ln_b, v_a_pw2_w, v_a_pw2_b, v_b_wq, v_kv_wk, v_kv_wv, v_b_wo, v_ln_mix_g, v_ln_mix_b, v_ffn_w_up, v_ffn_w_gate, v_ffn_conv_w, v_ffn_conv_b, v_ffn_w_down, v_ple_w_gate, v_ple_w_proj, v_ln_ffn_g, v_ln_ffn_b):
    vals = dict(locals())
    w = {n: vals[n] for n in WEIGHTS}
    m = {n: vals["m_" + n] for n in WEIGHTS}
    v = {n: vals["v_" + n] for n in WEIGHTS}
    return _step(x, p, loss_target, w, m, v)
```

```python
import functools
import math

import jax
import jax.numpy as jnp
import numpy as np
from jax import lax
from jax.experimental import pallas as pl
from jax.experimental.pallas import tpu as pltpu

F32, BF16 = jnp.float32, jnp.bfloat16

HEAD_DIM = 64
LN_EPS = 1e-5
DEPTH = 4
N_A = DEPTH // 2
DN_ALPHA = (2.0 * DEPTH) ** 0.25
N_CHIPS = 4

ADAM_LR, ADAM_B1, ADAM_B2, ADAM_EPS, ADAM_WD, ADAM_STEP = 0.001, 0.9, 0.999, 1e-08, 0.01, 10

VMEM_LIMIT_BYTES = 56 * 2**20
LANES = 128
SUBLANES = 8
CONV_HALO = 32
FFN_HALO = 16

NN = (((1,), (0,)), ((), ()))
NT = (((1,), (1,)), ((), ()))
TN = (((0,), (0,)), ((), ()))

WEIGHTS = ['a_pw1_w', 'a_pw1_b', 'a_dw_w', 'a_dw_b', 'a_ln_g', 'a_ln_b', 'a_pw2_w', 'a_pw2_b', 'b_wq', 'kv_wk', 'kv_wv',
           'b_wo', 'ln_mix_g', 'ln_mix_b', 'ffn_w_up', 'ffn_w_gate', 'ffn_conv_w', 'ffn_conv_b', 'ffn_w_down', 'ple_w_gate',
           'ple_w_proj', 'ln_ffn_g', 'ln_ffn_b']
REPLICATED = ('ln_mix_g', 'ln_mix_b', 'ffn_conv_b', 'ln_ffn_g', 'ln_ffn_b')
BIG = ('a_pw1_w', 'a_pw2_w', 'b_wq', 'kv_wk', 'kv_wv', 'b_wo', 'ffn_w_up', 'ffn_w_gate', 'ffn_w_down', 'ple_w_gate',
       'ple_w_proj')


def _tile(n, pref, mult=8):
    t = min(n, pref)
    while t > 0:
        if n % t == 0 and t % mult == 0:
            return t
        t -= 1
    return n


def _params(sem):
    return pltpu.CompilerParams(dimension_semantics=sem, vmem_limit_bytes=VMEM_LIMIT_BYTES)


def _sigmoid(x):
    return 0.5 * jnp.tanh(0.5 * x) + 0.5


def _mm_call(name, grid, terms, out_spec, out_sds, dims, bias=None, res=None, res_alpha=1.0, out_scale=None, into=None):
    n_terms = len(terms)

    def body(*refs):
        o_ref = refs[-1]
        acc = None
        for t in range(n_terms):
            a = refs[2 * t][...].astype(BF16)
            b = refs[2 * t + 1][...].astype(BF16)
            d = lax.dot_general(a, b, dims, preferred_element_type=F32)
            acc = d if acc is None else acc + d
        k = 2 * n_terms
        if bias is not None:
            acc = acc + refs[k][...]
            k += 1
        if res is not None:
            acc = acc + res_alpha * refs[k][...]
        if out_scale is not None:
            acc = acc * out_scale
        o_ref[...] = acc.astype(o_ref.dtype)

    operands, specs = [], []
    for a, a_spec, b, b_spec in terms:
        operands += [a, b]
        specs += [a_spec, b_spec]
    for extra in (bias, res):
        if extra is not None:
            operands.append(extra[0])
            specs.append(extra[1])
    aliases = {}
    if into is not None:
        aliases = {len(operands): 0}
        operands.append(into)
        specs.append(pl.BlockSpec(memory_space=pl.ANY))
        out_sds = jax.ShapeDtypeStruct(into.shape, into.dtype)
    return pl.pallas_call(body, out_shape=out_sds, grid=grid, in_specs=specs, out_specs=out_spec, input_output_aliases=aliases,
                          compiler_params=_params(("parallel",) * len(grid)), name=name)(*operands)


def _mm_fanout(name, a, a_spec, w4, w_block, layer, dims, out_spec, out_sds, store, grid, bias4=None, res=None, res_alpha=1.0):
    def body(*refs):
        a_ref, w_refs, o_ref = refs[0], refs[1:1 + N_CHIPS], refs[-1]
        k = 1 + N_CHIPS
        b_refs = refs[k:k + N_CHIPS] if bias4 is not None else None
        k += N_CHIPS if bias4 is not None else 0
        av = a_ref[...].astype(BF16)
        for j in range(N_CHIPS):
            d = lax.dot_general(av, w_refs[j][...].astype(BF16), dims, preferred_element_type=F32)
            if b_refs is not None:
                d = d + b_refs[j][...]
            if res is not None:
                d = d + res_alpha * res[2](refs[k], j)
            store(o_ref, j, d)

    nd = len(grid)
    operands = [a] + [w4] * N_CHIPS
    specs = [a_spec] + [pl.BlockSpec((None, None) + w_block, lambda *g, j=j: (j, layer, 0, 0)) for j in range(N_CHIPS)]
    if bias4 is not None:
        operands += [bias4] * N_CHIPS
        specs += [pl.BlockSpec((None, None, 1, bias4.shape[-1]), lambda *g, j=j: (j, layer, 0, 0)) for j in range(N_CHIPS)]
    if res is not None:
        operands.append(res[0])
        specs.append(res[1])
    return pl.pallas_call(body, out_shape=out_sds, grid=grid, in_specs=specs, out_specs=out_spec,
                          compiler_params=_params(("parallel",) * nd), name=name)(*operands)


def _store_slot(o_ref, j, d):
    o_ref[j] = d.astype(o_ref.dtype)


def mm_rowsm(name, a, w4, layer, a_sm=False, bias=None, out_dtype=F32, out_scale=None, tm=512):
    kc, N = w4.shape[-2:]
    M = a.shape[-2]
    tm = _tile(M, tm)
    terms = []
    for j in range(N_CHIPS):
        if a_sm:
            a_spec = pl.BlockSpec((None, tm, kc), lambda i, j=j: (j, i, 0))
        else:
            a_spec = pl.BlockSpec((tm, kc), lambda i, j=j: (i, j))
        terms.append((a, a_spec, w4, pl.BlockSpec((None, None, kc, N), lambda i, j=j: (j, layer, 0, 0))))
    b = None if bias is None else (bias, pl.BlockSpec((None, 1, N), lambda i: (layer, 0, 0)))
    return _mm_call(name, (M // tm,), terms, pl.BlockSpec((tm, N), lambda i: (i, 0)), jax.ShapeDtypeStruct((M, N), out_dtype),
                    NN, bias=b, out_scale=out_scale)


def mm_rowsm_ln(name, a, w4, layer, x, g, b, ln_layer, a_sm=False, bias=None, tg=None, pp=None, tm=512):
    kc, N = w4.shape[-2:]
    M = a.shape[-2]
    tm = _tile(M, tm, 16)
    ple = tg is not None
    nb = bias is not None

    def body(*refs):
        a_refs, w_refs = refs[0:N_CHIPS], refs[N_CHIPS:2 * N_CHIPS]
        k = 2 * N_CHIPS
        acc = None
        for j in range(N_CHIPS):
            d = jnp.dot(a_refs[j][...].astype(BF16), w_refs[j][...].astype(BF16), preferred_element_type=F32)
            acc = d if acc is None else acc + d
        if nb:
            acc = acc + refs[k][...]
            k += 1
        r = DN_ALPHA * refs[k][...] + acc
        k += 1
        if ple:
            r = r + _sigmoid(refs[k][...]) * refs[k + 1][...]
            k += 2
        g_ref, b_ref, y_ref, yb_ref, xh_ref, rs_ref = refs[k:]
        mu = jnp.mean(r, axis=-1, keepdims=True)
        dd = r - mu
        var = jnp.mean(dd * dd, axis=-1, keepdims=True)
        rstd = lax.rsqrt(var + LN_EPS)
        xh = dd * rstd
        y = xh * g_ref[...] + b_ref[...]
        y_ref[...] = y
        yb_ref[...] = y.astype(BF16)
        xh_ref[...] = xh
        rs_ref[...] = rstd

    row = pl.BlockSpec((tm, N), lambda i: (i, 0))
    vec = pl.BlockSpec((None, 1, N), lambda i: (ln_layer, 0, 0))
    if a_sm:
        a_specs = [pl.BlockSpec((None, tm, kc), lambda i, j=j: (j, i, 0)) for j in range(N_CHIPS)]
    else:
        a_specs = [pl.BlockSpec((tm, kc), lambda i, j=j: (i, j)) for j in range(N_CHIPS)]
    w_specs = [pl.BlockSpec((None, None, kc, N), lambda i, j=j: (j, layer, 0, 0)) for j in range(N_CHIPS)]
    ins = [a] * N_CHIPS + [w4] * N_CHIPS + ([bias] if nb else []) + [x] + ([tg, pp] if ple else []) + [g, b]
    specs = a_specs + w_specs + ([pl.BlockSpec((None, 1, N), lambda i: (layer, 0, 0))] if nb else []) + [row] + ([row, row] if ple else []) + [vec, vec]
    return pl.pallas_call(body, grid=(M // tm,), in_specs=specs,
                          out_specs=(row, row, row, pl.BlockSpec((tm, 1), lambda i: (i, 0))),
                          out_shape=(jax.ShapeDtypeStruct((M, N), F32), jax.ShapeDtypeStruct((M, N), BF16),
                                     jax.ShapeDtypeStruct((M, N), F32), jax.ShapeDtypeStruct((M, 1), F32)),
                          compiler_params=_params(("parallel",)), name=name)(*ins)


def mm_pw1_glu(name, xb, w4, layer, bias4, tm=512):
    M, K = xb.shape
    n = w4.shape[-1]
    half = N_CHIPS // 2
    tm = _tile(M, tm)

    def body(*refs):
        x_ref, w_refs, b_refs = refs[0], refs[1:1 + N_CHIPS], refs[1 + N_CHIPS:1 + 2 * N_CHIPS]
        h_ref, u_ref = refs[-2:]
        xv = x_ref[...].astype(BF16)
        for j in range(N_CHIPS):
            h_ref[j] = jnp.dot(xv, w_refs[j][...].astype(BF16), preferred_element_type=F32) + b_refs[j][...]
        for j in range(half):
            u_ref[:, j * n:(j + 1) * n] = h_ref[j] * _sigmoid(h_ref[j + half])

    specs = [pl.BlockSpec((tm, K), lambda i: (i, 0))]
    specs += [pl.BlockSpec((None, None, K, n), lambda i, j=j: (j, layer, 0, 0)) for j in range(N_CHIPS)]
    specs += [pl.BlockSpec((None, None, 1, n), lambda i, j=j: (j, layer, 0, 0)) for j in range(N_CHIPS)]
    return pl.pallas_call(body, grid=(M // tm,), in_specs=specs,
                          out_specs=(pl.BlockSpec((N_CHIPS, tm, n), lambda i: (0, i, 0)), pl.BlockSpec((tm, half * n), lambda i: (i, 0))),
                          out_shape=(jax.ShapeDtypeStruct((N_CHIPS, M, n), F32), jax.ShapeDtypeStruct((M, half * n), F32)),
                          compiler_params=_params(("parallel",)), name=name)(xb, *([w4] * N_CHIPS), *([bias4] * N_CHIPS))


def mm_nt_rowsm(name, dy, w4, layer, out_sm=False, res=None, res_alpha=1.0, out_dtype=F32, tm=1024):
    kc, N = w4.shape[-2:]
    M = dy.shape[0]
    tm = _tile(M, tm)

    def store_cols(o_ref, j, d):
        o_ref[:, j * kc:(j + 1) * kc] = d

    if out_sm:
        out_spec, sds, store = pl.BlockSpec((N_CHIPS, tm, kc), lambda i: (0, i, 0)), jax.ShapeDtypeStruct((N_CHIPS, M, kc), out_dtype), _store_slot
    else:
        out_spec, sds, store = pl.BlockSpec((tm, N_CHIPS * kc), lambda i: (i, 0)), jax.ShapeDtypeStruct((M, N_CHIPS * kc), F32), store_cols
    r = None if res is None else (res, pl.BlockSpec((tm, N_CHIPS * kc), lambda i: (i, 0)), lambda ref, j: ref[:, j * kc:(j + 1) * kc])
    return _mm_fanout(name, dy, pl.BlockSpec((tm, N), lambda i: (i, 0)), w4, (kc, N), layer, NT, out_spec, sds, store, (M // tm,),
                      res=r, res_alpha=res_alpha)


def mm_nt_colsm(name, dy_parts, w4, layer, res=None, res_alpha=1.0, tm=512):
    K, n = w4.shape[-2:]
    M = dy_parts[0][0].shape[1]
    tm = _tile(M, tm)
    terms = [(arr, pl.BlockSpec((None, tm, n), lambda i, idx=idx: (idx, i, 0)), w4,
              pl.BlockSpec((None, None, K, n), lambda i, j=j: (j, layer, 0, 0))) for j, (arr, idx) in enumerate(dy_parts)]
    r = None if res is None else (res, pl.BlockSpec((tm, K), lambda i: (i, 0)))
    return _mm_call(name, (M // tm,), terms, pl.BlockSpec((tm, K), lambda i: (i, 0)), jax.ShapeDtypeStruct((M, K), F32), NT,
                    res=r, res_alpha=res_alpha)


def _sm_parts(a):
    return [(a, j) for j in range(a.shape[0])]


def mm_tn_colsm(name, x, dy, into, off, j0=0, tk=512):
    M, K = x.shape
    nj, _, n = dy.shape
    tk = _tile(K, tk, LANES)
    assert off % tk == 0
    terms = [(x, pl.BlockSpec((M, tk), lambda j, k: (0, k)), dy, pl.BlockSpec((None, M, n), lambda j, k: (j, 0, 0)))]
    return _mm_call(name, (nj, K // tk), terms, pl.BlockSpec((None, tk, n), lambda j, k: (j + j0, off // tk + k, 0)),
                    None, TN, into=into)


def mm_tn_rowsm(name, a, dy, into, off, a_sm=False, tn=512):
    M, N = dy.shape
    kc = a.shape[-1] if a_sm else a.shape[-1] // N_CHIPS
    tn = _tile(N, tn, LANES)
    assert off % kc == 0
    a_spec = pl.BlockSpec((None, M, kc), lambda j, n: (j, 0, 0)) if a_sm else pl.BlockSpec((M, kc), lambda j, n: (0, j))
    terms = [(a, a_spec, dy, pl.BlockSpec((M, tn), lambda j, n: (0, n)))]
    return _mm_call(name, (N_CHIPS, N // tn), terms, pl.BlockSpec((None, kc, tn), lambda j, n: (j, off // kc, n)),
                    None, TN, into=into)


def mm_tn_rowsm_fan(name, a, dy, into, off, tn=256):
    M, N = dy.shape
    kc = a.shape[-1] // N_CHIPS
    tn = _tile(N, tn, LANES)
    assert off % kc == 0

    def body(a_ref, dy_ref, into_ref, o_ref):
        dyb = dy_ref[...].astype(BF16)
        for j in range(N_CHIPS):
            aj = a_ref[:, j * kc:(j + 1) * kc].astype(BF16)
            o_ref[j] = lax.dot_general(aj, dyb, TN, preferred_element_type=F32).astype(o_ref.dtype)

    return pl.pallas_call(body, out_shape=jax.ShapeDtypeStruct(into.shape, into.dtype), grid=(N // tn,),
                          in_specs=[pl.BlockSpec((M, N_CHIPS * kc), lambda n: (0, 0)), pl.BlockSpec((M, tn), lambda n: (0, n)),
                                    pl.BlockSpec(memory_space=pl.ANY)],
                          out_specs=pl.BlockSpec((N_CHIPS, kc, tn), lambda n: (0, off // kc, n)), input_output_aliases={2: 0},
                          compiler_params=_params(("parallel",)), name=name)(a, dy, into)


def mm_proj(name, p4, layer, w4, tm=512):
    S, P = p4.shape[-2:]
    n = w4.shape[-1]
    tm = _tile(S, tm)

    def store_cols(o_ref, j, d):
        o_ref[:, j * n:(j + 1) * n] = d

    return _mm_fanout(name, p4, pl.BlockSpec((None, None, tm, P), lambda i: (layer, 0, i, 0)), w4, (P, n), layer, NN,
                      pl.BlockSpec((tm, N_CHIPS * n), lambda i: (i, 0)), jax.ShapeDtypeStruct((S, N_CHIPS * n), F32),
                      store_cols, (S // tm,))


def mm_tn_proj(name, p4, layer, dpp, into, off):
    S, P = p4.shape[-2:]
    n = dpp.shape[-1] // N_CHIPS
    assert off % P == 0
    terms = [(p4, pl.BlockSpec((None, None, S, P), lambda j: (layer, 0, 0, 0)), dpp, pl.BlockSpec((S, n), lambda j: (0, j)))]
    return _mm_call(name, (N_CHIPS,), terms, pl.BlockSpec((None, P, n), lambda j: (j, off // P, 0)), None, TN, into=into)


def ln_bwd(name, dy, xh, rstd, g, layer, tg=None, pp=None, tm=512):
    S, D = dy.shape
    tm = _tile(S, tm)
    ple = tg is not None

    def body(*refs):
        if ple:
            dy_ref, xh_ref, rs_ref, g_ref, tg_ref, pp_ref, dr_ref, acc_ref, dtg_ref, dpp_ref = refs
        else:
            dy_ref, xh_ref, rs_ref, g_ref, dr_ref, acc_ref = refs
        dy_, xh_ = dy_ref[...], xh_ref[...]
        dxh = dy_ * g_ref[...]
        m1 = jnp.mean(dxh, axis=-1, keepdims=True)
        m2 = jnp.mean(dxh * xh_, axis=-1, keepdims=True)
        dr = rs_ref[...] * (dxh - m1 - xh_ * m2)
        dr_ref[...] = dr

        @pl.when(pl.program_id(0) == 0)
        def _():
            acc_ref[...] = jnp.zeros_like(acc_ref)
        acc_ref[0:1, :] += jnp.sum(dy_ * xh_, axis=0, keepdims=True)
        acc_ref[1:2, :] += jnp.sum(dy_, axis=0, keepdims=True)
        acc_ref[2:3, :] += jnp.sum(dr, axis=0, keepdims=True)
        if ple:
            pg = _sigmoid(tg_ref[...])
            dtg_ref[...] = (dr * pp_ref[...] * pg * (1.0 - pg)).astype(BF16)
            dpp_ref[...] = (dr * pg).astype(BF16)

    row = pl.BlockSpec((tm, D), lambda i: (i, 0))
    ins = [dy, xh, rstd, g] + ([tg, pp] if ple else [])
    specs = [row, row, pl.BlockSpec((tm, 1), lambda i: (i, 0)), pl.BlockSpec((None, 1, D), lambda i: (layer, 0, 0))] + ([row, row] if ple else [])
    outs = [jax.ShapeDtypeStruct((S, D), F32), jax.ShapeDtypeStruct((8, D), F32)]
    out_specs = [row, pl.BlockSpec((8, D), lambda i: (0, 0))]
    if ple:
        outs += [jax.ShapeDtypeStruct((S, D), BF16)] * 2
        out_specs += [row, row]
    return pl.pallas_call(body, grid=(S // tm,), in_specs=specs, out_specs=tuple(out_specs), out_shape=tuple(outs),
                          compiler_params=_params(("arbitrary",)), name=name)(*ins)


def conv_ln_silu_fwd(name, u, w, b, g, beta, layer, ts=128):
    S, D = u.shape
    kw = w.shape[1]
    ts = _tile(S, ts, CONV_HALO)
    lc = LANES if D % LANES == 0 else D

    def body(h_ref, u_ref, w_ref, b_ref, g_ref, be_ref, c_ref, s_ref, win_ref, rot_ref):
        i = pl.program_id(0)
        win_ref[0:CONV_HALO, :] = jnp.where(i == 0, 0.0, h_ref[...])
        win_ref[CONV_HALO:, :] = u_ref[...]
        for cc in range(D // lc):
            cs = slice(cc * lc, (cc + 1) * lc)
            for sub in range(1, SUBLANES):
                rot_ref[sub] = win_ref[sub:sub + ts + CONV_HALO - SUBLANES, cs]
            acc = jnp.zeros((ts, lc), F32) + b_ref[:, cs]
            for k in range(kw):
                whole, sub = divmod(CONV_HALO - (kw - 1) + k, SUBLANES)
                r0 = whole * SUBLANES
                acc = acc + w_ref[k:k + 1, cs] * (win_ref[r0:r0 + ts, cs] if sub == 0 else rot_ref[sub, r0:r0 + ts, :])
            c_ref[:, cs] = acc
        c = c_ref[...]
        mu = jnp.mean(c, axis=-1, keepdims=True)
        d = c - mu
        var = jnp.mean(d * d, axis=-1, keepdims=True)
        nrm = d * lax.rsqrt(var + LN_EPS) * g_ref[...] + be_ref[...]
        s_ref[...] = (nrm * _sigmoid(nrm)).astype(BF16)

    row = pl.BlockSpec((ts, D), lambda i: (i, 0))
    vec = pl.BlockSpec((None, 1, D), lambda i: (layer, 0, 0))
    halo = pl.BlockSpec((CONV_HALO, D), lambda i: (jnp.maximum(i * (ts // CONV_HALO) - 1, 0), 0))
    return pl.pallas_call(body, grid=(S // ts,),
                          in_specs=[halo, row, pl.BlockSpec((None, kw, D), lambda i: (layer, 0, 0)), vec, vec, vec],
                          out_specs=(row, row),
                          out_shape=(jax.ShapeDtypeStruct((S, D), F32), jax.ShapeDtypeStruct((S, D), BF16)),
                          scratch_shapes=[pltpu.VMEM((ts + CONV_HALO, D), F32),
                                          pltpu.VMEM((SUBLANES, ts + CONV_HALO - SUBLANES, lc), F32)],
                          compiler_params=_params(("parallel",)), name=name)(u, u, w, b, g, beta)


def ln_silu_bwd(name, ds, c, g, beta, layer, tm=256):
    S, D = c.shape
    tm = _tile(S, tm)

    def body(ds_ref, c_ref, g_ref, be_ref, dc_ref, acc_ref):
        c_ = c_ref[...]
        mu = jnp.mean(c_, axis=-1, keepdims=True)
        d = c_ - mu
        var = jnp.mean(d * d, axis=-1, keepdims=True)
        rstd = lax.rsqrt(var + LN_EPS)
        xh = d * rstd
        nrm = xh * g_ref[...] + be_ref[...]
        sg = _sigmoid(nrm)
        dn = ds_ref[...] * (sg * (1.0 + nrm * (1.0 - sg)))
        dxh = dn * g_ref[...]
        m1 = jnp.mean(dxh, axis=-1, keepdims=True)
        m2 = jnp.mean(dxh * xh, axis=-1, keepdims=True)
        dc = rstd * (dxh - m1 - xh * m2)
        dc_ref[...] = dc

        @pl.when(pl.program_id(0) == 0)
        def _():
            acc_ref[...] = jnp.zeros_like(acc_ref)
        acc_ref[0:1, :] += jnp.sum(dn * xh, axis=0, keepdims=True)
        acc_ref[1:2, :] += jnp.sum(dn, axis=0, keepdims=True)
        acc_ref[2:3, :] += jnp.sum(dc, axis=0, keepdims=True)

    row = pl.BlockSpec((tm, D), lambda i: (i, 0))
    vec = pl.BlockSpec((None, 1, D), lambda i: (layer, 0, 0))
    return pl.pallas_call(body, grid=(S // tm,), in_specs=[row, row, vec, vec],
                          out_specs=(row, pl.BlockSpec((8, D), lambda i: (0, 0))),
                          out_shape=(jax.ShapeDtypeStruct((S, D), F32), jax.ShapeDtypeStruct((8, D), F32)),
                          compiler_params=_params(("arbitrary",)), name=name)(ds, c, g, beta)


def conv_glu_bwd(name, dc, u, h_sm, w, layer, ts=128):
    S, D = dc.shape
    kw = w.shape[1]
    half = N_CHIPS // 2
    n = D // half
    ts = _tile(S, ts, CONV_HALO)
    nblk = S // ts
    lc = LANES if n % LANES == 0 else n

    def body(dc_ref, dcn_ref, u_ref, a_ref, g_ref, w_ref, da_ref, dg_ref, dw_ref, dba_ref, dbg_ref, dwin_ref, rot_ref):
        i = pl.program_id(1)
        dwin_ref[0:ts, :] = dc_ref[...]
        dwin_ref[ts:, :] = jnp.where(i == nblk - 1, 0.0, dcn_ref[...])

        @pl.when(i == 0)
        def _():
            dw_ref[...] = jnp.zeros_like(dw_ref)
            dba_ref[...] = jnp.zeros_like(dba_ref)
            dbg_ref[...] = jnp.zeros_like(dbg_ref)

        for cc in range(n // lc):
            cs = slice(cc * lc, (cc + 1) * lc)
            for sub in range(1, SUBLANES):
                rot_ref[sub] = dwin_ref[sub:sub + ts + CONV_HALO - SUBLANES, cs]
            ub = u_ref[:, cs]
            du = jnp.zeros((ts, lc), F32)
            for k in range(kw):
                whole, sub = divmod(kw - 1 - k, SUBLANES)
                r0 = whole * SUBLANES
                shifted = dwin_ref[r0:r0 + ts, cs] if sub == 0 else rot_ref[sub, r0:r0 + ts, :]
                du = du + w_ref[k:k + 1, cs] * shifted
                dw_ref[k:k + 1, cs] += jnp.sum(shifted * ub, axis=0, keepdims=True)
            a = a_ref[:, cs]
            sg = _sigmoid(g_ref[:, cs])
            da = du * sg
            dg = du * a * sg * (1.0 - sg)
            da_ref[:, cs] = da.astype(BF16)
            dg_ref[:, cs] = dg.astype(BF16)
            dba_ref[0:1, cs] += jnp.sum(da, axis=0, keepdims=True)
            dbg_ref[0:1, cs] += jnp.sum(dg, axis=0, keepdims=True)

    r = ts // CONV_HALO
    main = pl.BlockSpec((ts, n), lambda j, i: (i, j))
    nxt = pl.BlockSpec((CONV_HALO, n), lambda j, i: (jnp.minimum((i + 1) * r, S // CONV_HALO - 1), j))
    sm_a = pl.BlockSpec((None, ts, n), lambda j, i: (j, i, 0))
    sm_g = pl.BlockSpec((None, ts, n), lambda j, i: (j + half, i, 0))
    da, dg, dw, dba, dbg = pl.pallas_call(
        body, grid=(half, nblk),
        in_specs=[main, nxt, main, sm_a, sm_g, pl.BlockSpec((None, kw, n), lambda j, i: (layer, 0, j))],
        out_specs=(pl.BlockSpec((None, ts, n), lambda j, i: (j, i, 0)), pl.BlockSpec((None, ts, n), lambda j, i: (j, i, 0)),
                   pl.BlockSpec((None, 32, n), lambda j, i: (j, 0, 0)),
                   pl.BlockSpec((None, 8, n), lambda j, i: (j, 0, 0)), pl.BlockSpec((None, 8, n), lambda j, i: (j, 0, 0))),
        out_shape=(jax.ShapeDtypeStruct((half, S, n), BF16), jax.ShapeDtypeStruct((half, S, n), BF16),
                   jax.ShapeDtypeStruct((half, 32, n), F32),
                   jax.ShapeDtypeStruct((half, 8, n), F32), jax.ShapeDtypeStruct((half, 8, n), F32)),
        scratch_shapes=[pltpu.VMEM((ts + CONV_HALO, n), F32), pltpu.VMEM((SUBLANES, ts + CONV_HALO - SUBLANES, lc), F32)],
        compiler_params=_params(("parallel", "arbitrary")), name=name)(dc, dc, u, h_sm, h_sm, w)
    return da, dg, dw, dba, dbg


ROW_CHUNK = 16


def _ffn_gc(win_ref, w_ref, b_ref, r0, rows, kw, base):
    gc = b_ref[...] + jnp.zeros((rows, win_ref.shape[1]), F32)
    for k in range(kw):
        off = r0 + base - (kw - 1) + k
        gc = gc + w_ref[k:k + 1, :] * win_ref[off:off + rows, :]
    return gc


def ffn_up_gate_fwd(name, xb, w_up, w_gate, cw4, cb4, layer, tm=256):
    S, K = xb.shape
    n = w_up.shape[-1]
    kw = cw4.shape[2]
    tm = _tile(S, tm, ROW_CHUNK)
    rc = ROW_CHUNK

    def body(*refs):
        x_ref, xh_ref = refs[0:2]
        wu, wg, cw, cb = refs[2:6], refs[6:10], refs[10:14], refs[14:18]
        up_ref, gp_ref, gc_ref, hf_ref, win_ref = refs[18:]
        i = pl.program_id(0)
        xv = x_ref[...]
        xhalo = xh_ref[...]
        for j in range(N_CHIPS):
            wgj = wg[j][...]
            gp = jnp.dot(xv, wgj, preferred_element_type=F32)
            up = jnp.dot(xv, wu[j][...], preferred_element_type=F32)
            gph = jnp.dot(xhalo, wgj, preferred_element_type=F32)
            gpb = gp.astype(BF16)
            gp_ref[j] = gpb
            up_ref[j] = up.astype(BF16)
            win_ref[0:FFN_HALO, :] = jnp.where(i == 0, 0.0, gph.astype(BF16).astype(F32))
            win_ref[FFN_HALO:, :] = gpb.astype(F32)
            for r0 in range(0, tm, rc):
                gc = _ffn_gc(win_ref, cw[j], cb[j], r0, rc, kw, FFN_HALO)
                gc_ref[j, r0:r0 + rc, :] = gc.astype(BF16)
                hf_ref[j, r0:r0 + rc, :] = (gc * _sigmoid(gc) * up_ref[j, r0:r0 + rc, :].astype(F32)).astype(BF16)

    r = tm // FFN_HALO
    out = pl.BlockSpec((N_CHIPS, tm, n), lambda i: (0, i, 0))
    sds = jax.ShapeDtypeStruct((N_CHIPS, S, n), BF16)
    specs = [pl.BlockSpec((tm, K), lambda i: (i, 0)), pl.BlockSpec((FFN_HALO, K), lambda i: (jnp.maximum(i * r - 1, 0), 0))]
    specs += [pl.BlockSpec((None, None, K, n), lambda i, j=j: (j, layer, 0, 0)) for j in range(N_CHIPS)] * 2
    specs += [pl.BlockSpec((None, None, kw, n), lambda i, j=j: (j, layer, 0, 0)) for j in range(N_CHIPS)]
    specs += [pl.BlockSpec((None, None, 1, n), lambda i, j=j: (layer, j, 0, 0)) for j in range(N_CHIPS)]
    return pl.pallas_call(body, grid=(S // tm,), in_specs=specs, out_specs=(out, out, out, out), out_shape=(sds, sds, sds, sds),
                          scratch_shapes=[pltpu.VMEM((tm + FFN_HALO, n), F32)],
                          compiler_params=_params(("parallel",)), name=name)(
                              xb, xb, *([w_up] * N_CHIPS), *([w_gate] * N_CHIPS), *([cw4] * N_CHIPS), *([cb4] * N_CHIPS))


def ffn_gate_conv_bwd(name, dr, w_down, cw4, layer, up_sm, gc_sm, gp_sm, tm=512):
    n, N = w_down.shape[-2:]
    S = dr.shape[0]
    kw = cw4.shape[2]
    tm = _tile(S, tm, ROW_CHUNK)
    rc = ROW_CHUNK
    nblk = S // tm

    def dgc_of(dhf, up, gc):
        sg = _sigmoid(gc)
        return dhf * up * (sg * (1.0 + gc * (1.0 - sg))), sg

    def body(*refs):
        dr_ref, drn_ref = refs[0:2]
        wd, cw = refs[2:6], refs[6:10]
        up_ref, gc_ref, gp_ref, upn_ref, gcn_ref, dup_ref, dgp_ref, acc_ref, win_ref = refs[10:]
        i = pl.program_id(0)

        @pl.when(i == 0)
        def _():
            acc_ref[...] = jnp.zeros_like(acc_ref)
        a = dr_ref[...].astype(BF16)
        an = drn_ref[...].astype(BF16)
        for j in range(N_CHIPS):
            wj = wd[j][...]
            dhf = lax.dot_general(a, wj, NT, preferred_element_type=F32)
            gc = gc_ref[j].astype(F32)
            dgc, sg = dgc_of(dhf, up_ref[j].astype(F32), gc)
            dup_ref[j] = (dhf * gc * sg).astype(BF16)
            win_ref[0:tm, :] = dgc
            dhfn = lax.dot_general(an, wj, NT, preferred_element_type=F32)
            dgcn, _ = dgc_of(dhfn, upn_ref[j].astype(F32), gcn_ref[j].astype(F32))
            win_ref[tm:, :] = jnp.where(i == nblk - 1, 0.0, dgcn)
            sums = [jnp.zeros((8, n), F32) for _ in range(kw + 1)]
            for r0 in range(0, tm, rc):
                gp = gp_ref[j, r0:r0 + rc, :].astype(F32)
                dgp = jnp.zeros((rc, n), F32)
                for k in range(kw):
                    d = kw - 1 - k
                    shifted = win_ref[r0 + d:r0 + d + rc, :]
                    dgp = dgp + cw[j][k:k + 1, :] * shifted
                    prod = shifted * gp
                    sums[k] = sums[k] + prod[0:8, :] + prod[8:16, :]
                    if d == 0:
                        sums[kw] = sums[kw] + shifted[0:8, :] + shifted[8:16, :]
                dgp_ref[j, r0:r0 + rc, :] = dgp.astype(BF16)
            for k in range(kw):
                acc_ref[j, k:k + 1, :] += jnp.sum(sums[k], axis=0, keepdims=True)
            acc_ref[j, 7:8, :] += jnp.sum(sums[kw], axis=0, keepdims=True)

    r = tm // FFN_HALO
    nxt_row = lambda i: jnp.minimum((i + 1) * r, S // FFN_HALO - 1)
    blk = pl.BlockSpec((N_CHIPS, tm, n), lambda i: (0, i, 0))
    halo = pl.BlockSpec((N_CHIPS, FFN_HALO, n), lambda i: (0, nxt_row(i), 0))
    sds = jax.ShapeDtypeStruct((N_CHIPS, S, n), BF16)
    specs = [pl.BlockSpec((tm, N), lambda i: (i, 0)), pl.BlockSpec((FFN_HALO, N), lambda i: (nxt_row(i), 0))]
    specs += [pl.BlockSpec((None, None, n, N), lambda i, j=j: (j, layer, 0, 0)) for j in range(N_CHIPS)]
    specs += [pl.BlockSpec((None, None, kw, n), lambda i, j=j: (j, layer, 0, 0)) for j in range(N_CHIPS)]
    specs += [blk, blk, blk, halo, halo]
    return pl.pallas_call(body, grid=(nblk,), in_specs=specs,
                          out_specs=(blk, blk, pl.BlockSpec((N_CHIPS, 8, n), lambda i: (0, 0, 0))),
                          out_shape=(sds, sds, jax.ShapeDtypeStruct((N_CHIPS, 8, n), F32)),
                          scratch_shapes=[pltpu.VMEM((tm + FFN_HALO, n), F32)],
                          compiler_params=_params(("arbitrary",)), name=name)(
                              dr, dr, *([w_down] * N_CHIPS), *([cw4] * N_CHIPS), up_sm, gc_sm, gp_sm, up_sm, gc_sm)


def _neg_softplus(z):
    e = jnp.exp(-jnp.abs(z))
    return -(jnp.maximum(z, 0.0) + jnp.log(1.0 + e)), e


def _split_dot(x, t):
    hi = x.astype(BF16)
    lo = (x - hi.astype(F32)).astype(BF16)
    return jnp.dot(hi, t, preferred_element_type=F32) + jnp.dot(lo, t, preferred_element_type=F32)


STICK_GONE = -100.0
NOT_SWEPT = -1e30


def attn_fwd(name, q, k, v, bq=512, w=256):
    S, D = q.shape
    dh = HEAD_DIM
    hpb = LANES // dh
    bq = _tile(S, bq)
    w = _tile(bq, w)
    nsub = bq // w
    nkb = S // w

    def body(q_ref, k_ref, v_ref, o_ref, runs_ref, rs_ref):
        qi = pl.program_id(1)
        rr = lax.broadcasted_iota(jnp.int32, (w, w), 0)
        cc = lax.broadcasted_iota(jnp.int32, (w, w), 1)
        t_suf = (rr >= cc).astype(BF16)
        tq = qi * bq + lax.broadcasted_iota(jnp.int32, (bq, w), 0)
        tk = lax.broadcasted_iota(jnp.int32, (bq, w), 1)
        lane = lax.broadcasted_iota(jnp.int32, (bq, LANES), 1)
        ntot = (qi + 1) * nsub
        heads = [slice(hh * dh, (hh + 1) * dh) for hh in range(hpb)]
        qbs = [q_ref[:, hs] for hs in heads]
        for hh in range(hpb):
            rs_ref[hh] = jnp.where(lane < ntot, NOT_SWEPT, 0.0)

        def block(kb, carry, masked):
            kstart = pl.multiple_of(kb * w, w)
            if masked:
                m = (tk + kstart) < tq
            out = []
            for hh, hs in enumerate(heads):
                run, acc = carry[2 * hh], carry[2 * hh + 1]
                kblk = k_ref[pl.ds(kstart, w), hs]
                vblk = v_ref[pl.ds(kstart, w), hs]
                z = lax.dot_general(qbs[hh], kblk, NT, preferred_element_type=F32)
                lg, _ = _neg_softplus(z)
                if masked:
                    lg = jnp.where(m, lg, 0.0)
                cum = _split_dot(lg, t_suf) + run
                a = jnp.exp(z + cum)
                if masked:
                    a = jnp.where(m, a, 0.0)
                acc = acc + jnp.dot(a.astype(BF16), vblk, preferred_element_type=F32)
                run = cum[:, 0:1]
                rs_ref[hh] = jnp.where(lane == kb, run, rs_ref[hh])
                out += [run, acc]
            return tuple(out)

        carry = (jnp.zeros((bq, 1), F32), jnp.zeros((bq, dh), F32)) * hpb
        for sb in reversed(range(nsub)):
            carry = block(qi * nsub + sb, carry, True)

        def cond(c):
            alive = functools.reduce(jnp.maximum, [jnp.max(c[1 + 2 * hh]) for hh in range(hpb)])
            return jnp.logical_and(c[0] >= 0, alive > STICK_GONE)

        def step(c):
            return (c[0] - 1,) + block(c[0], c[1:], False)
        carry = lax.while_loop(cond, step, (qi * nsub - 1,) + carry)[1:]
        for hh, hs in enumerate(heads):
            o_ref[:, hs] = carry[2 * hh + 1].astype(o_ref.dtype)
            runs_ref[hh] = rs_ref[hh, :, 0:nkb]

    qs = pl.BlockSpec((bq, LANES), lambda h, i: (i, h))
    kv = pl.BlockSpec((S, LANES), lambda h, i: (0, h))
    return pl.pallas_call(body, grid=(D // LANES, S // bq), in_specs=[qs, kv, kv],
                          out_specs=(qs, pl.BlockSpec((hpb, bq, nkb), lambda h, i: (h, i, 0))),
                          out_shape=(jax.ShapeDtypeStruct((S, D), BF16), jax.ShapeDtypeStruct((D // dh, S, nkb), F32)),
                          scratch_shapes=[pltpu.VMEM((hpb, bq, LANES), F32)],
                          compiler_params=_params(("parallel", "parallel")), name=name)(q, k, v)


def attn_bwd(name, q, k, v, do, runs, dk0=None, dv0=None, bq=512, w=256):
    S, D = q.shape
    dh = HEAD_DIM
    hpb = LANES // dh
    bq = _tile(S, bq)
    w = _tile(bq, w)
    nsub = bq // w
    nkb = S // w
    scale = 1.0 / math.sqrt(dh)
    init = dk0 is not None

    def body(*refs):
        if init:
            q_ref, k_ref, v_ref, do_ref, runs_ref, dk0_ref, dv0_ref, dq_ref, dk_ref, dv_ref, rs_ref = refs
        else:
            q_ref, k_ref, v_ref, do_ref, runs_ref, dq_ref, dk_ref, dv_ref, rs_ref = refs
        qi = pl.program_id(1)

        @pl.when(qi == 0)
        def _():
            dk_ref[...] = dk0_ref[...] if init else jnp.zeros_like(dk_ref)
            dv_ref[...] = dv0_ref[...] if init else jnp.zeros_like(dv_ref)

        rr = lax.broadcasted_iota(jnp.int32, (w, w), 0)
        cc = lax.broadcasted_iota(jnp.int32, (w, w), 1)
        t_suf = (rr >= cc).astype(BF16)
        t_pre = (rr <= cc).astype(BF16)
        tq = qi * bq + lax.broadcasted_iota(jnp.int32, (bq, w), 0)
        tk = lax.broadcasted_iota(jnp.int32, (bq, w), 1)
        lane = lax.broadcasted_iota(jnp.int32, (bq, LANES), 1)
        lane1 = lax.broadcasted_iota(jnp.int32, (1, LANES), 1)
        ntot = (qi + 1) * nsub
        heads = [slice(hh * dh, (hh + 1) * dh) for hh in range(hpb)]
        qbs = [q_ref[:, hs] for hs in heads]
        dobs = [do_ref[:, hs].astype(BF16) for hs in heads]
        kb0 = ntot - nsub
        for hh in range(hpb):
            rs_ref[hh] = jnp.zeros((bq, LANES), F32)
            rs_ref[hh, :, 0:nkb] = runs_ref[hh]
            colmax = jnp.max(rs_ref[hh], axis=0, keepdims=True)
            dead = jnp.logical_and(jnp.logical_and(lane1 >= 1, lane1 <= ntot), colmax <= STICK_GONE)
            kb0 = jnp.minimum(kb0, jnp.sum(dead.astype(jnp.int32)))

        def block(kb, carry, masked):
            kstart = pl.multiple_of(kb * w, w)
            if masked:
                m = (tk + kstart) < tq
            out = []
            for hh, hs in enumerate(heads):
                pg_run, dq = carry[2 * hh], carry[2 * hh + 1]
                qb, dob = qbs[hh], dobs[hh]
                kblk = k_ref[pl.ds(kstart, w), hs]
                vblk = v_ref[pl.ds(kstart, w), hs]
                right = jnp.sum(jnp.where(lane == kb + 1, rs_ref[hh], 0.0), axis=1, keepdims=True)
                z = lax.dot_general(qb, kblk, NT, preferred_element_type=F32)
                lg, _ = _neg_softplus(z)
                sig = 1.0 - jnp.exp(lg)
                if masked:
                    lg = jnp.where(m, lg, 0.0)
                a = jnp.exp(z + _split_dot(lg, t_suf) + right)
                if masked:
                    a = jnp.where(m, a, 0.0)
                da = lax.dot_general(dob, vblk, NT, preferred_element_type=F32)
                g = da * a
                pin = _split_dot(g, t_pre) + pg_run
                dz = g - sig * pin
                if masked:
                    dz = jnp.where(m, dz, 0.0)
                dzb = dz.astype(BF16)
                dq = dq + jnp.dot(dzb, kblk, preferred_element_type=F32)
                dk_ref[pl.ds(kstart, w), hs] += lax.dot_general(dzb, qb, TN, preferred_element_type=F32)
                dv_ref[pl.ds(kstart, w), hs] += lax.dot_general(a.astype(BF16), dob, TN, preferred_element_type=F32)
                out += [pin[:, w - 1:w], dq]
            return tuple(out)

        carry = (jnp.zeros((bq, 1), F32), jnp.zeros((bq, dh), F32)) * hpb
        carry = lax.fori_loop(kb0, qi * nsub, lambda kb, c: block(kb, c, False), carry)
        for sb in range(nsub):
            carry = block(qi * nsub + sb, carry, True)
        for hh, hs in enumerate(heads):
            dq_ref[:, hs] = carry[2 * hh + 1] * scale

    qs = pl.BlockSpec((bq, LANES), lambda h, i: (i, h))
    kv = pl.BlockSpec((S, LANES), lambda h, i: (0, h))
    ins = [q, k, v, do, runs] + ([dk0, dv0] if init else [])
    specs = [qs, kv, kv, qs, pl.BlockSpec((hpb, bq, nkb), lambda h, i: (h, i, 0))] + ([kv, kv] if init else [])
    sds = jax.ShapeDtypeStruct((S, D), F32)
    return pl.pallas_call(body, grid=(D // LANES, S // bq), in_specs=specs, out_specs=(qs, kv, kv), out_shape=(sds, sds, sds),
                          scratch_shapes=[pltpu.VMEM((hpb, bq, LANES), F32)],
                          compiler_params=_params(("parallel", "arbitrary")), name=name)(*ins)


def loss_head(name, y, tgt, tm=512):
    S, D = y.shape
    tm = _tile(S, tm)

    def body(y_ref, t_ref, dy_ref, acc_ref):
        @pl.when(pl.program_id(0) == 0)
        def _():
            acc_ref[...] = jnp.zeros_like(acc_ref)
        e = y_ref[...] - t_ref[...]
        dy_ref[...] = e * (1.0 / D)
        acc_ref[...] += jnp.sum(e * e)

    row = pl.BlockSpec((tm, D), lambda i: (i, 0))
    return pl.pallas_call(body, grid=(S // tm,), in_specs=[row, row],
                          out_specs=(row, pl.BlockSpec((8, LANES), lambda i: (0, 0))),
                          out_shape=(jax.ShapeDtypeStruct((S, D), F32), jax.ShapeDtypeStruct((8, LANES), F32)),
                          compiler_params=_params(("arbitrary",)), name=name)(y, tgt)


def _to_sm(a, axis=-1):
    axis = axis % a.ndim
    shp = a.shape[:axis] + (N_CHIPS, a.shape[axis] // N_CHIPS) + a.shape[axis + 1:]
    return jnp.moveaxis(a.reshape(shp), axis, 0)


def _from_sm(a, axis=-1):
    nd = a.ndim - 1
    axis = axis % nd
    b = jnp.moveaxis(a, 0, axis)
    return b.reshape(b.shape[:axis] + (b.shape[axis] * b.shape[axis + 1],) + b.shape[axis + 2:])


class GradBuffers:
    def __init__(self, W):
        groups = {}
        for n in BIG:
            _, layers, rows, cols = W[n].shape
            groups.setdefault(cols, []).append((rows, n, layers))
        self.where, self.cols_of, self.buf, self.members = {}, {}, {}, {}
        for cols, items in groups.items():
            off, members = 0, []
            for rows, n, layers in sorted(items, key=lambda t: -t[0]):
                assert off % rows == 0
                self.where[n], self.cols_of[n] = (off, rows), cols
                members.append((n, off, rows * layers))
                off += rows * layers
            assert off % 32 == 0
            self.buf[cols] = lax.empty((N_CHIPS, off, cols), BF16)
            self.members[cols] = members

    def put(self, n, layer, fn, **kw):
        cols = self.cols_of[n]
        off, rows = self.where[n]
        self.buf[cols] = fn(into=self.buf[cols], off=off + layer * rows, **kw)


def forward_backward(x, p4, tgt, W):
    S, D = x.shape
    scale = 1.0 / math.sqrt(HEAD_DIM)
    saved = []
    kh = vh = xb_kv = None
    xb = x.astype(BF16)
    for i in range(DEPTH):
        sv = {'xb': xb}
        if i < N_A:
            h_sm, u = mm_pw1_glu(f"pw1glu_{i}", xb, W['a_pw1_w'], i, W['a_pw1_b'])
            c, s = conv_ln_silu_fwd(f"convln_{i}", u, W['a_dw_w'], W['a_dw_b'], W['a_ln_g'], W['a_ln_b'], i)
            x1, x1b, xh1, rs1 = mm_rowsm_ln(f"pw2ln_{i}", s, W['a_pw2_w'], i, x, W['ln_mix_g'], W['ln_mix_b'], i, bias=W['a_pw2_b'])
            sv.update(h_sm=h_sm, u=u, c=c, s=s)
        else:
            j = i - N_A
            if kh is None:
                xb_kv = xb
                kh = mm_rowsm("wk", xb, W['kv_wk'], 0, out_dtype=BF16)
                vh = mm_rowsm("wv", xb, W['kv_wv'], 0, out_dtype=BF16)
            qh = mm_rowsm(f"wq_{j}", xb, W['b_wq'], j, out_dtype=BF16, out_scale=scale)
            o, runs = attn_fwd(f"attn_{j}", qh, kh, vh)
            x1, x1b, xh1, rs1 = mm_rowsm_ln(f"woln_{j}", o, W['b_wo'], j, x, W['ln_mix_g'], W['ln_mix_b'], i)
            sv.update(qh=qh, o=o, runs=runs)
        up_sm, gp_sm, gc_sm, hf_sm = ffn_up_gate_fwd(f"upgate_{i}", x1b, W['ffn_w_up'], W['ffn_w_gate'], W['ffn_conv_w'],
                                                     W['ffn_conv_b'], i)
        tg = mm_rowsm(f"plegate_{i}", x1b, W['ple_w_gate'], i)
        pp = mm_proj(f"pleproj_{i}", p4, i, W['ple_w_proj'])
        x2, x2b, xh2, rs2 = mm_rowsm_ln(f"downln_{i}", hf_sm, W['ffn_w_down'], i, x1, W['ln_ffn_g'], W['ln_ffn_b'], i, a_sm=True,
                                        tg=tg, pp=pp)
        sv.update(x1b=x1b, xh1=xh1, rs1=rs1, up_sm=up_sm, gp_sm=gp_sm, gc_sm=gc_sm, hf_sm=hf_sm, tg=tg, pp=pp, xh2=xh2, rs2=rs2)
        saved.append(sv)
        x, xb = x2, x2b

    dx, lacc = loss_head("loss", x, tgt)
    loss_sum = lacc[0, 0]

    G = {n: [None] * DEPTH for n in WEIGHTS if n not in BIG}
    gb = GradBuffers(W)
    dk = dv = None
    for i in reversed(range(DEPTH)):
        sv = saved[i]
        dr, acc, dtg, dpp = ln_bwd(f"lnffn_b_{i}", dx, sv['xh2'], sv['rs2'], W['ln_ffn_g'], i, tg=sv['tg'], pp=sv['pp'])
        G['ln_ffn_g'][i], G['ln_ffn_b'][i] = acc[0], acc[1]
        gb.put('ple_w_proj', i, functools.partial(mm_tn_proj, f"dproj_{i}", p4, i, dpp))
        gb.put('ple_w_gate', i, functools.partial(mm_tn_rowsm_fan, f"dplegate_{i}", sv['x1b'], dtg))
        gb.put('ffn_w_down', i, functools.partial(mm_tn_rowsm, f"ddown_{i}", sv['hf_sm'], dr, a_sm=True))
        dup_sm, dgp_sm, cacc = ffn_gate_conv_bwd(f"gateconv_b_{i}", dr, W['ffn_w_down'], W['ffn_conv_w'], i, sv['up_sm'],
                                                 sv['gc_sm'], sv['gp_sm'])
        kw = W['ffn_conv_w'].shape[2]
        G['ffn_conv_w'][i] = cacc[:, 0:kw, :]
        G['ffn_conv_b'][i] = cacc[:, 7, :].reshape(-1)
        gb.put('ffn_w_up', i, functools.partial(mm_tn_colsm, f"dup_{i}", sv['x1b'], dup_sm))
        gb.put('ffn_w_gate', i, functools.partial(mm_tn_colsm, f"dgate_{i}", sv['x1b'], dgp_sm))
        dx1 = mm_nt_rowsm(f"dx1a_{i}", dtg, W['ple_w_gate'], i, res=dr, res_alpha=DN_ALPHA)
        dx1 = mm_nt_colsm(f"dx1b_{i}", _sm_parts(dup_sm), W['ffn_w_up'], i, res=dx1)
        dx1 = mm_nt_colsm(f"dx1c_{i}", _sm_parts(dgp_sm), W['ffn_w_gate'], i, res=dx1)

        dr1, acc1 = ln_bwd(f"lnmix_b_{i}", dx1, sv['xh1'], sv['rs1'], W['ln_mix_g'], i)
        G['ln_mix_g'][i], G['ln_mix_b'][i] = acc1[0], acc1[1]
        xin = sv['xb']
        if i < N_A:
            G['a_pw2_b'][i] = acc1[2]
            gb.put('a_pw2_w', i, functools.partial(mm_tn_rowsm_fan, f"dpw2_{i}", sv['s'], dr1))
            ds = mm_nt_rowsm(f"ds_{i}", dr1, W['a_pw2_w'], i)
            dc, cacc = ln_silu_bwd(f"lnsilu_b_{i}", ds, sv['c'], W['a_ln_g'], W['a_ln_b'], i)
            G['a_ln_g'][i], G['a_ln_b'][i], G['a_dw_b'][i] = cacc[0], cacc[1], cacc[2]
            da, dg, dw, dba, dbg = conv_glu_bwd(f"convglu_b_{i}", dc, sv['u'], sv['h_sm'], W['a_dw_w'], i)
            kw = W['a_dw_w'].shape[1]
            G['a_dw_w'][i] = _from_sm(dw[:, 0:kw, :], axis=-1)
            G['a_pw1_b'][i] = jnp.concatenate([dba[:, 0, :], dbg[:, 0, :]], axis=0)
            half = da.shape[0]
            gb.put('a_pw1_w', i, functools.partial(mm_tn_colsm, f"dpw1a_{i}", xin, da))
            gb.put('a_pw1_w', i, functools.partial(mm_tn_colsm, f"dpw1g_{i}", xin, dg), j0=half)
            dx = mm_nt_colsm(f"dxa_{i}", _sm_parts(da) + _sm_parts(dg), W['a_pw1_w'], i, res=dr1, res_alpha=DN_ALPHA)
        else:
            j = i - N_A
            gb.put('b_wo', j, functools.partial(mm_tn_rowsm_fan, f"dwo_{j}", sv['o'], dr1))
            do = mm_nt_rowsm(f"do_{j}", dr1, W['b_wo'], j)
            dq, dk, dv = attn_bwd(f"attn_b_{j}", sv['qh'], kh, vh, do, sv['runs'], dk, dv)
            gb.put('b_wq', j, functools.partial(mm_tn_rowsm_fan, f"dwq_{j}", xin, dq))
            dx = mm_nt_rowsm(f"dxq_{j}", dq, W['b_wq'], j, res=dr1, res_alpha=DN_ALPHA)
            if j == 0:
                gb.put('kv_wk', 0, functools.partial(mm_tn_rowsm_fan, "dwk", xb_kv, dk))
                gb.put('kv_wv', 0, functools.partial(mm_tn_rowsm_fan, "dwv", xb_kv, dv))
                dx = mm_nt_rowsm("dxk", dk, W['kv_wk'], 0, res=dx)
                dx = mm_nt_rowsm("dxv", dv, W['kv_wv'], 0, res=dx)
    return loss_sum, dx, G, gb


MESH = pl.DeviceIdType.MESH
HBM = pl.BlockSpec(memory_space=pltpu.HBM)


def _place():
    x, y, c = lax.axis_index("x"), lax.axis_index("y"), lax.axis_index("c")
    others = [(1 - x, y), (x, 1 - y), (1 - x, 1 - y)]
    return x, y, c, others


def allgather_chips(name, arrs):
    n = len(arrs)

    def body(*refs):
        ins, outs = refs[:n], refs[n:2 * n]
        send_sems, recv_sems = refs[2 * n:]
        x, y, c, others = _place()
        me = 2 * x + y
        sibling = (x, y, 1 - c)
        ids = [2 * ch[0] + ch[1] for ch in others]
        from_id = jnp.where(c == 0, ids[0], ids[1])
        to_chip = (jnp.where(c == 0, x, 1 - x), jnp.where(c == 0, 1 - y, y))

        def remote(a, k, src, chip_id, half, to):
            return pltpu.make_async_remote_copy(src_ref=src, dst_ref=outs[a].at[chip_id, half], send_sem=send_sems.at[a, k],
                                                recv_sem=recv_sems.at[a, k], device_id=to, device_id_type=MESH)

        sent = [remote(a, k, ins[a].at[c], me, c, (others[k][0], others[k][1], c)) for a in range(n) for k in range(2)]
        for cp in sent:
            cp.start()
        for a in range(n):
            for k in range(2):
                remote(a, k, ins[a].at[c], ids[k], c, sibling).wait_recv()
            sent.append(remote(a, 2, outs[a].at[from_id, c], from_id, c, (to_chip[0], to_chip[1], c)))
            sent[-1].start()
            for k in range(2):
                sent.append(remote(a, 3 + k, outs[a].at[ids[k], c], ids[k], c, sibling))
                sent[-1].start()
        for a in range(n):
            remote(a, 2, ins[a].at[c], ids[2], c, sibling).wait_recv()
            sent.append(remote(a, 5, outs[a].at[ids[2], c], ids[2], c, sibling))
            sent[-1].start()
        for a in range(n):
            for k in range(3):
                remote(a, 3 + k, ins[a].at[c], ids[k], 1 - c, sibling).wait_recv()
        for cp in sent:
            cp.wait_send()

    outs = pl.pallas_call(body, out_shape=tuple(jax.ShapeDtypeStruct((N_CHIPS,) + a.shape, a.dtype) for a in arrs),
                          in_specs=[HBM] * n, out_specs=tuple([HBM] * n),
                          scratch_shapes=[pltpu.SemaphoreType.DMA((n, 6)), pltpu.SemaphoreType.DMA((n, 6))],
                          name=name)(*arrs)
    me = 2 * lax.axis_index("x") + lax.axis_index("y")
    return [lax.dynamic_update_index_in_dim(o, a, me, 0) for o, a in zip(outs, arrs)]


def exchange_sibling(name, gs):
    n = len(gs)

    def body(*refs):
        g_refs, o_refs = refs[:n], refs[n:2 * n]
        send_sems, recv_sems = refs[2 * n:]
        x, y, c, _ = _place()
        cps = [pltpu.make_async_remote_copy(src_ref=g_refs[a].at[j, 1 - c], dst_ref=o_refs[a].at[j], send_sem=send_sems.at[a, j],
                                            recv_sem=recv_sems.at[a, j], device_id=(x, y, 1 - c), device_id_type=MESH)
               for a in range(n) for j in range(N_CHIPS)]
        for cp in cps:
            cp.start()
        for cp in cps:
            cp.wait()

    return pl.pallas_call(body, out_shape=tuple(jax.ShapeDtypeStruct((N_CHIPS,) + g.shape[2:], g.dtype) for g in gs),
                          in_specs=[HBM] * n, out_specs=tuple([HBM] * n),
                          scratch_shapes=[pltpu.SemaphoreType.DMA((n, N_CHIPS)), pltpu.SemaphoreType.DMA((n, N_CHIPS))],
                          name=name)(*gs)


def _ring_peers():
    x, y, c, _ = _place()
    first = (jnp.where(c == 0, 1 - x, x), jnp.where(c == 0, y, 1 - y))
    second = (jnp.where(c == 0, x, 1 - x), jnp.where(c == 0, 1 - y, y))
    return c, first, second, 2 * (1 - x) + (1 - y)


def exchange_first(name, ss):
    n = len(ss)

    def body(*refs):
        s_refs, o_refs = refs[:n], refs[n:2 * n]
        send_sems, recv_sems = refs[2 * n:]
        c, first, _, diag = _ring_peers()
        cps = [pltpu.make_async_remote_copy(src_ref=s_refs[a].at[slot], dst_ref=o_refs[a].at[k], send_sem=send_sems.at[a, k],
                                            recv_sem=recv_sems.at[a, k], device_id=(first[0], first[1], c), device_id_type=MESH)
               for a in range(n) for k, slot in enumerate((2 * first[0] + first[1], diag))]
        for cp in cps:
            cp.start()
        for cp in cps:
            cp.wait()

    return pl.pallas_call(body, out_shape=tuple(jax.ShapeDtypeStruct((2,) + s.shape[1:], s.dtype) for s in ss),
                          in_specs=[HBM] * n, out_specs=tuple([HBM] * n),
                          scratch_shapes=[pltpu.SemaphoreType.DMA((n, 2)), pltpu.SemaphoreType.DMA((n, 2))], name=name)(*ss)


def exchange_second(name, ts):
    n = len(ts)

    def body(*refs):
        t_refs, o_refs = refs[:n], refs[n:2 * n]
        send_sems, recv_sems = refs[2 * n:]
        c, _, second, _ = _ring_peers()
        cps = [pltpu.make_async_remote_copy(src_ref=t_refs[a], dst_ref=o_refs[a], send_sem=send_sems.at[a], recv_sem=recv_sems.at[a],
                                            device_id=(second[0], second[1], c), device_id_type=MESH) for a in range(n)]
        for cp in cps:
            cp.start()
        for cp in cps:
            cp.wait()

    return pl.pallas_call(body, out_shape=tuple(jax.ShapeDtypeStruct(t.shape, t.dtype) for t in ts),
                          in_specs=[HBM] * n, out_specs=tuple([HBM] * n),
                          scratch_shapes=[pltpu.SemaphoreType.DMA((n,)), pltpu.SemaphoreType.DMA((n,))], name=name)(*ts)


def share_sibling(name, ts):
    n = len(ts)

    def body(*refs):
        o_refs = refs[n:2 * n]
        send_sems, recv_sems = refs[2 * n:]
        x, y, c, _ = _place()
        cps = [pltpu.make_async_remote_copy(src_ref=o_refs[a].at[c], dst_ref=o_refs[a].at[c], send_sem=send_sems.at[a],
                                            recv_sem=recv_sems.at[a], device_id=(x, y, 1 - c), device_id_type=MESH)
               for a in range(n)]
        for cp in cps:
            cp.start()
        for a in range(n):
            pltpu.make_async_remote_copy(src_ref=o_refs[a].at[c], dst_ref=o_refs[a].at[1 - c], send_sem=send_sems.at[a],
                                         recv_sem=recv_sems.at[a], device_id=(x, y, 1 - c), device_id_type=MESH).wait_recv()
        for cp in cps:
            cp.wait_send()

    return pl.pallas_call(body, out_shape=tuple(jax.ShapeDtypeStruct(t.shape, t.dtype) for t in ts),
                          in_specs=[HBM] * n, out_specs=tuple([HBM] * n), input_output_aliases={a: a for a in range(n)},
                          scratch_shapes=[pltpu.SemaphoreType.DMA((n,)), pltpu.SemaphoreType.DMA((n,))],
                          name=name)(*ts)


def add_halves(name, g, recv, place, out_dtype, tr=512):
    _, _, R, C = g.shape
    tr = _tile(R, tr, 16)

    def body(p_ref, a_ref, b_ref, o_ref):
        o_ref[...] = (a_ref[...].astype(F32) + b_ref[...].astype(F32)).astype(o_ref.dtype)

    blk = pl.BlockSpec((None, tr, C), lambda j, i, p: (j, i, 0))
    gs = pltpu.PrefetchScalarGridSpec(num_scalar_prefetch=1, grid=(N_CHIPS, R // tr),
                                      in_specs=[pl.BlockSpec((None, None, tr, C), lambda j, i, p: (j, p[0], i, 0)), blk],
                                      out_specs=blk)
    return pl.pallas_call(body, grid_spec=gs, out_shape=jax.ShapeDtypeStruct((N_CHIPS, R, C), out_dtype),
                          compiler_params=_params(("parallel", "parallel")), name=name)(place, g, recv)


def add_pass_on(name, s, got, place, tr=512):
    _, R, C = s.shape
    tr = _tile(R, tr, 16)

    def body(p_ref, a_ref, b_ref, o_ref):
        o_ref[...] = (a_ref[...].astype(F32) + b_ref[...].astype(F32)).astype(o_ref.dtype)

    gs = pltpu.PrefetchScalarGridSpec(num_scalar_prefetch=1, grid=(R // tr,),
                                      in_specs=[pl.BlockSpec((None, tr, C), lambda i, p: (p[2], i, 0)),
                                                pl.BlockSpec((None, tr, C), lambda i, p: (1, i, 0))],
                                      out_specs=pl.BlockSpec((tr, C), lambda i, p: (i, 0)))
    return pl.pallas_call(body, grid_spec=gs, out_shape=jax.ShapeDtypeStruct((R, C), s.dtype),
                          compiler_params=_params(("parallel",)), name=name)(place, s, got)


def add_chips(name, g, r1, got1, got2, place, tr=512):
    _, _, R, C = g.shape
    tr = _tile(R, tr, 16)

    def body(p_ref, a_ref, b_ref, c_ref, d_ref, o_ref):
        o_ref[...] = ((a_ref[...].astype(F32) + b_ref[...].astype(F32)) + c_ref[...].astype(F32)) + d_ref[...].astype(F32)

    gs = pltpu.PrefetchScalarGridSpec(num_scalar_prefetch=1, grid=(R // tr,),
                                      in_specs=[pl.BlockSpec((None, None, tr, C), lambda i, p: (p[1], p[0], i, 0)),
                                                pl.BlockSpec((None, tr, C), lambda i, p: (p[1], i, 0)),
                                                pl.BlockSpec((None, tr, C), lambda i, p: (0, i, 0)),
                                                pl.BlockSpec((tr, C), lambda i, p: (i, 0))],
                                      out_specs=pl.BlockSpec((None, tr, C), lambda i, p: (p[0], i, 0)))
    return pl.pallas_call(body, grid_spec=gs, out_shape=jax.ShapeDtypeStruct((2, R, C), F32),
                          compiler_params=_params(("parallel",)), name=name)(place, g, r1, got1, got2)


def reduce_scatter(gs, wire_dtypes, place):
    r1 = exchange_sibling("rs_sibling", gs)
    s1 = [add_halves(f"rs_add_cores_{a}", g, r, place, dt) for a, (g, r, dt) in enumerate(zip(gs, r1, wire_dtypes))]
    got1 = exchange_first("rs_first", s1)
    t = [add_pass_on(f"rs_add_pass_{a}", s, g1, place) for a, (s, g1) in enumerate(zip(s1, got1))]
    got2 = exchange_second("rs_second", t)
    tot = [add_chips(f"rs_add_chips_{a}", g, r, g1, g2, place) for a, (g, r, g1, g2) in enumerate(zip(gs, r1, got1, got2))]
    return share_sibling("rs_share", tot)


def adamw(name, w, g, m, v, tr=512):
    shp = w.shape
    cols = shp[-1]
    w2, g2, m2, v2 = (a.reshape(-1, cols) for a in (w, g, m, v))
    rows = w2.shape[0]
    tr = _tile(rows, tr)

    def body(w_ref, g_ref, m_ref, v_ref, d_ref, mo_ref, vo_ref):
        g_ = g_ref[...]
        m_ = ADAM_B1 * m_ref[...] + (1.0 - ADAM_B1) * g_
        v_ = ADAM_B2 * v_ref[...] + (1.0 - ADAM_B2) * (g_ * g_)
        m_hat = m_ / (1.0 - ADAM_B1 ** ADAM_STEP)
        v_hat = v_ / (1.0 - ADAM_B2 ** ADAM_STEP)
        d_ref[...] = -ADAM_LR * (m_hat / (jnp.sqrt(v_hat) + ADAM_EPS) + ADAM_WD * w_ref[...])
        mo_ref[...] = m_
        vo_ref[...] = v_

    blk = pl.BlockSpec((tr, cols), lambda i: (i, 0))
    sds = jax.ShapeDtypeStruct((rows, cols), F32)
    d, mo, vo = pl.pallas_call(body, grid=(rows // tr,), in_specs=[blk] * 4, out_specs=(blk, blk, blk), out_shape=(sds, sds, sds),
                               compiler_params=_params(("parallel",)), name=name)(w2, g2, m2, v2)
    return d.reshape(shp), mo.reshape(shp), vo.reshape(shp)


PACK_ALIGN = 1024


def _pad_to(a, mult, axis=-1):
    axis = axis % a.ndim
    extra = (-a.shape[axis]) % mult
    if extra == 0:
        return a
    pads = [(0, 0)] * a.ndim
    pads[axis] = (0, extra)
    return jnp.pad(a, pads)


def _pack(pieces, lead, row_mult):
    nl = len(lead)
    flat, offs, sizes, off = [], [], [], 0
    for a in pieces:
        f = a.reshape(lead + (-1,))
        sizes.append(f.shape[-1])
        f = _pad_to(f, PACK_ALIGN)
        offs.append(off)
        off += f.shape[-1]
        flat.append(f)
    cat = _pad_to(jnp.concatenate(flat, axis=nl), 2 * row_mult * LANES)
    return cat.reshape(lead + (2, -1, LANES)), offs, sizes


def _unpack(packed, lead, offs, sizes, shapes):
    flat = packed.reshape(lead + (-1,))
    return [lax.slice_in_dim(flat, o, o + s, axis=len(lead)).reshape(lead + tuple(shp)) for o, s, shp in zip(offs, sizes, shapes)]


def _stack_grads(G, names):
    out = {}
    for n in names:
        parts = [g for g in G[n] if g is not None]
        if n in ('kv_wk', 'kv_wv'):
            out[n] = parts[0]
        elif n in REPLICATED:
            out[n] = jnp.stack(parts, axis=0).reshape(N_CHIPS, -1)
        elif n in ('a_dw_w', 'a_dw_b', 'a_ln_g', 'a_ln_b', 'a_pw2_b'):
            out[n] = _to_sm(jnp.stack(parts, axis=0), axis=-1)
        else:
            out[n] = jnp.stack(parts, axis=1)
    return out


def _whole_weights(big, small, rep, D):
    W = {}
    for n in BIG:
        a = big[n]
        W[n] = a[:, None] if n in ('kv_wk', 'kv_wv') else a
    W['a_pw1_b'] = small['a_pw1_b'][:, :, None, :]
    W['a_dw_w'] = _from_sm(small['a_dw_w'], axis=-1)
    for n in ('a_dw_b', 'a_ln_g', 'a_ln_b', 'a_pw2_b'):
        W[n] = _from_sm(small[n], axis=-1)[:, None, :]
    W['ffn_conv_w'] = small['ffn_conv_w']
    L, F = rep['ffn_conv_b'].shape
    W['ffn_conv_b'] = rep['ffn_conv_b'].reshape(L, N_CHIPS, 1, F // N_CHIPS)
    for n in ('ln_mix_g', 'ln_mix_b', 'ln_ffn_g', 'ln_ffn_b'):
        W[n] = rep[n][:, None, :]
    return W


SMALL = ('a_pw1_b', 'a_dw_w', 'a_dw_b', 'a_ln_g', 'a_ln_b', 'a_pw2_b', 'ffn_conv_w')


def _step(x, p, loss_target, w, m, v):
    S, D = x.shape[-2:]
    x2, tgt = x.reshape(S, D), loss_target.reshape(S, D)
    ax, ay, ac = lax.axis_index("x"), lax.axis_index("y"), lax.axis_index("c")
    second = jnp.where(ac == 0, 2 * ax + (1 - ay), 2 * (1 - ax) + ay)
    place = jnp.stack([ac, 2 * ax + ay, second]).astype(jnp.int32)

    big_in = [w[n].astype(BF16).reshape((2, -1) + w[n].shape[1:] if w[n].ndim == 3 else (2, -1, w[n].shape[-1])) for n in BIG]
    small_in, s_offs, s_sizes = _pack([w[n] for n in SMALL], (), 8)
    gathered = allgather_chips("gather_weights", big_in + [small_in])
    big = {n: g.reshape((N_CHIPS,) + w[n].shape) for n, g in zip(BIG, gathered[:-1])}
    small = dict(zip(SMALL, _unpack(gathered[-1], (N_CHIPS,), s_offs, s_sizes, [w[n].shape for n in SMALL])))
    W = _whole_weights(big, small, {n: w[n] for n in REPLICATED}, D)

    loss_sum, dx, G, gb = forward_backward(x2, p, tgt, W)
    loss = lax.psum(0.5 * loss_sum / D, ("x", "y", "c"))

    vectors = [n for n in WEIGHTS if n not in BIG]
    mats = [b.reshape(N_CHIPS, 2, b.shape[1] // 2, b.shape[2]) for b in gb.buf.values()]
    members = [gb.members[cols] for cols in gb.buf]
    g_sm = _stack_grads(G, vectors)
    packed, offs, sizes = _pack([g_sm[n] for n in vectors], (N_CHIPS,), 512)
    reduced = reduce_scatter(mats + [packed], [BF16] * len(mats) + [F32], place)
    shapes = [w[n].shape if n not in REPLICATED else (w[n].size // N_CHIPS,) for n in vectors]
    g_mine = dict(zip(vectors, _unpack(reduced[-1], (), offs, sizes, shapes)))
    for red, where in zip(reduced[:-1], members):
        rows = red.reshape(-1, red.shape[-1])
        for n, off, cnt in where:
            g_mine[n] = lax.slice_in_dim(rows, off, off + cnt, axis=0).reshape(w[n].shape)
    rep_in, r_offs, r_sizes = _pack([g_mine[n] for n in REPLICATED], (), 8)
    rep_all = allgather_chips("gather_replicated_grads", [rep_in])[0]
    for n, g in zip(REPLICATED, _unpack(rep_all, (N_CHIPS,), r_offs, r_sizes, [(w[n].size // N_CHIPS,) for n in REPLICATED])):
        g_mine[n] = g.reshape(w[n].shape)

    grads, deltas, new_m, new_v = [], [], [], []
    for n in WEIGHTS:
        d, mo, vo = adamw(f"adamw_{n}", w[n], g_mine[n], m[n], v[n])
        grads.append(g_mine[n])
        deltas.append(d)
        new_m.append(mo)
        new_v.append(vo)
    return (loss, dx.reshape(x.shape), *grads, *deltas, *new_m, *new_v)


def kernel(x, p, a_pw1_w, a_pw1_b, a_dw_w, a_dw_b, a_ln_g, a_ln_b, a_pw2_w, a_pw2_b, b_wq, kv_wk, kv_wv, b_wo, ln_mix_g, ln_mix_b, ffn_w_up, ffn_w_gate, ffn_conv_w, ffn_conv_b, ffn_w_down, ple_w_gate, ple_w_proj, ln_ffn_g, ln_ffn_b, loss_target, m_a_pw1_w, m_a_pw1_b, m_a_dw_w, m_a_dw_b, m_a_ln_g, m_a_ln_b, m_a_pw2_w, m_a_pw2_b, m_b_wq, m_kv_wk, m_kv_wv, m_b_wo, m_ln_mix_g, m_ln_mix_b, m_ffn_w_up, m_ffn_w_gate, m_ffn_conv_w, m_ffn_conv_b, m_ffn_w_down, m_ple_w_gate, m_ple_w_proj, m_ln_ffn_g, m_ln_ffn_b, v_a_pw1_w, v_a_pw1_b, v_a_dw_w, v_a_dw_b, v_a_ln_g, v_a_ln_b, v_a_pw2_w, v_a_pw2_b, v_b_wq, v_kv_wk, v_kv_wv, v_b_wo, v_ln_mix_g, v_ln_mix_b, v_ffn_w_up, v_ffn_w_gate, v_ffn_conv_w, v_ffn_conv_b, v_ffn_w_down, v_ple_w_gate, v_ple_w_proj, v_ln_ffn_g, v_ln_ffn_b):
    vals = dict(locals())
    w = {n: vals[n] for n in WEIGHTS}
    m = {n: vals["m_" + n] for n in WEIGHTS}
    v = {n: vals["v_" + n] for n in WEIGHTS}
    return _step(x, p, loss_target, w, m, v)
```

```python
import functools
import math

import jax
import jax.numpy as jnp
import numpy as np
from jax import lax
from jax.experimental import pallas as pl
from jax.experimental.pallas import tpu as pltpu

F32, BF16 = jnp.float32, jnp.bfloat16

HEAD_DIM = 64
LN_EPS = 1e-5
DEPTH = 4
N_A = DEPTH // 2
DN_ALPHA = (2.0 * DEPTH) ** 0.25
N_CHIPS = 4

ADAM_LR, ADAM_B1, ADAM_B2, ADAM_EPS, ADAM_WD, ADAM_STEP = 0.001, 0.9, 0.999, 1e-08, 0.01, 10

VMEM_LIMIT_BYTES = 56 * 2**20
LANES = 128
SUBLANES = 8
CONV_HALO = 32
FFN_HALO = 16

NN = (((1,), (0,)), ((), ()))
NT = (((1,), (1,)), ((), ()))
TN = (((0,), (0,)), ((), ()))

WEIGHTS = ['a_pw1_w', 'a_pw1_b', 'a_dw_w', 'a_dw_b', 'a_ln_g', 'a_ln_b', 'a_pw2_w', 'a_pw2_b', 'b_wq', 'kv_wk', 'kv_wv',
           'b_wo', 'ln_mix_g', 'ln_mix_b', 'ffn_w_up', 'ffn_w_gate', 'ffn_conv_w', 'ffn_conv_b', 'ffn_w_down', 'ple_w_gate',
           'ple_w_proj', 'ln_ffn_g', 'ln_ffn_b']
REPLICATED = ('ln_mix_g', 'ln_mix_b', 'ffn_conv_b', 'ln_ffn_g', 'ln_ffn_b')
BIG = ('a_pw1_w', 'a_pw2_w', 'b_wq', 'kv_wk', 'kv_wv', 'b_wo', 'ffn_w_up', 'ffn_w_gate', 'ffn_w_down', 'ple_w_gate',
       'ple_w_proj')


def _tile(n, pref, mult=8):
    t = min(n, pref)
    while t > 0:
        if n % t == 0 and t % mult == 0:
            return t
        t -= 1
    return n


def _params(sem):
    return pltpu.CompilerParams(dimension_semantics=sem, vmem_limit_bytes=VMEM_LIMIT_BYTES)


def _sigmoid(x):
    return 0.5 * jnp.tanh(0.5 * x) + 0.5


def _mm_call(name, grid, terms, out_spec, out_sds, dims, bias=None, res=None, res_alpha=1.0, out_scale=None, into=None):
    n_terms = len(terms)

    def body(*refs):
        o_ref = refs[-1]
        acc = None
        for t in range(n_terms):
            a = refs[2 * t][...].astype(BF16)
            b = refs[2 * t + 1][...].astype(BF16)
            d = lax.dot_general(a, b, dims, preferred_element_type=F32)
            acc = d if acc is None else acc + d
        k = 2 * n_terms
        if bias is not None:
            acc = acc + refs[k][...]
            k += 1
        if res is not None:
            acc = acc + res_alpha * refs[k][...]
        if out_scale is not None:
            acc = acc * out_scale
        o_ref[...] = acc.astype(o_ref.dtype)

    operands, specs = [], []
    for a, a_spec, b, b_spec in terms:
        operands += [a, b]
        specs += [a_spec, b_spec]
    for extra in (bias, res):
        if extra is not None:
            operands.append(extra[0])
            specs.append(extra[1])
    aliases = {}
    if into is not None:
        aliases = {len(operands): 0}
        operands.append(into)
        specs.append(pl.BlockSpec(memory_space=pl.ANY))
        out_sds = jax.ShapeDtypeStruct(into.shape, into.dtype)
    return pl.pallas_call(body, out_shape=out_sds, grid=grid, in_specs=specs, out_specs=out_spec, input_output_aliases=aliases,
                          compiler_params=_params(("parallel",) * len(grid)), name=name)(*operands)


def _mm_fanout(name, a, a_spec, w4, w_block, layer, dims, out_spec, out_sds, store, grid, bias4=None, res=None, res_alpha=1.0):
    def body(*refs):
        a_ref, w_refs, o_ref = refs[0], refs[1:1 + N_CHIPS], refs[-1]
        k = 1 + N_CHIPS
        b_refs = refs[k:k + N_CHIPS] if bias4 is not None else None
        k += N_CHIPS if bias4 is not None else 0
        av = a_ref[...].astype(BF16)
        for j in range(N_CHIPS):
            d = lax.dot_general(av, w_refs[j][...].astype(BF16), dims, preferred_element_type=F32)
            if b_refs is not None:
                d = d + b_refs[j][...]
            if res is not None:
                d = d + res_alpha * res[2](refs[k], j)
            store(o_ref, j, d)

    nd = len(grid)
    operands = [a] + [w4] * N_CHIPS
    specs = [a_spec] + [pl.BlockSpec((None, None) + w_block, lambda *g, j=j: (j, layer, 0, 0)) for j in range(N_CHIPS)]
    if bias4 is not None:
        operands += [bias4] * N_CHIPS
        specs += [pl.BlockSpec((None, None, 1, bias4.shape[-1]), lambda *g, j=j: (j, layer, 0, 0)) for j in range(N_CHIPS)]
    if res is not None:
        operands.append(res[0])
        specs.append(res[1])
    return pl.pallas_call(body, out_shape=out_sds, grid=grid, in_specs=specs, out_specs=out_spec,
                          compiler_params=_params(("parallel",) * nd), name=name)(*operands)


def _store_slot(o_ref, j, d):
    o_ref[j] = d.astype(o_ref.dtype)


def mm_rowsm(name, a, w4, layer, a_sm=False, bias=None, out_dtype=F32, out_scale=None, tm=512):
    kc, N = w4.shape[-2:]
    M = a.shape[-2]
    tm = _tile(M, tm)
    terms = []
    for j in range(N_CHIPS):
        if a_sm:
            a_spec = pl.BlockSpec((None, tm, kc), lambda i, j=j: (j, i, 0))
        else:
            a_spec = pl.BlockSpec((tm, kc), lambda i, j=j: (i, j))
        terms.append((a, a_spec, w4, pl.BlockSpec((None, None, kc, N), lambda i, j=j: (j, layer, 0, 0))))
    b = None if bias is None else (bias, pl.BlockSpec((None, 1, N), lambda i: (layer, 0, 0)))
    return _mm_call(name, (M // tm,), terms, pl.BlockSpec((tm, N), lambda i: (i, 0)), jax.ShapeDtypeStruct((M, N), out_dtype),
                    NN, bias=b, out_scale=out_scale)


def mm_rowsm_ln(name, a, w4, layer, x, g, b, ln_layer, a_sm=False, bias=None, tg=None, pp=None, tm=512):
    kc, N = w4.shape[-2:]
    M = a.shape[-2]
    tm = _tile(M, tm, 16)
    ple = tg is not None
    nb = bias is not None

    def body(*refs):
        a_refs, w_refs = refs[0:N_CHIPS], refs[N_CHIPS:2 * N_CHIPS]
        k = 2 * N_CHIPS
        acc = None
        for j in range(N_CHIPS):
            d = jnp.dot(a_refs[j][...].astype(BF16), w_refs[j][...].astype(BF16), preferred_element_type=F32)
            acc = d if acc is None else acc + d
        if nb:
            acc = acc + refs[k][...]
            k += 1
        r = DN_ALPHA * refs[k][...] + acc
        k += 1
        if ple:
            r = r + _sigmoid(refs[k][...]) * refs[k + 1][...]
            k += 2
        g_ref, b_ref, y_ref, yb_ref, xh_ref, rs_ref = refs[k:]
        mu = jnp.mean(r, axis=-1, keepdims=True)
        dd = r - mu
        var = jnp.mean(dd * dd, axis=-1, keepdims=True)
        rstd = lax.rsqrt(var + LN_EPS)
        xh = dd * rstd
        y = xh * g_ref[...] + b_ref[...]
        y_ref[...] = y
        yb_ref[...] = y.astype(BF16)
        xh_ref[...] = xh
        rs_ref[...] = rstd

    row = pl.BlockSpec((tm, N), lambda i: (i, 0))
    vec = pl.BlockSpec((None, 1, N), lambda i: (ln_layer, 0, 0))
    if a_sm:
        a_specs = [pl.BlockSpec((None, tm, kc), lambda i, j=j: (j, i, 0)) for j in range(N_CHIPS)]
    else:
        a_specs = [pl.BlockSpec((tm, kc), lambda i, j=j: (i, j)) for j in range(N_CHIPS)]
    w_specs = [pl.BlockSpec((None, None, kc, N), lambda i, j=j: (j, layer, 0, 0)) for j in range(N_CHIPS)]
    ins = [a] * N_CHIPS + [w4] * N_CHIPS + ([bias] if nb else []) + [x] + ([tg, pp] if ple else []) + [g, b]
    specs = a_specs + w_specs + ([pl.BlockSpec((None, 1, N), lambda i: (layer, 0, 0))] if nb else []) + [row] + ([row, row] if ple else []) + [vec, vec]
    return pl.pallas_call(body, grid=(M // tm,), in_specs=specs,
                          out_specs=(row, row, row, pl.BlockSpec((tm, 1), lambda i: (i, 0))),
                          out_shape=(jax.ShapeDtypeStruct((M, N), F32), jax.ShapeDtypeStruct((M, N), BF16),
                                     jax.ShapeDtypeStruct((M, N), F32), jax.ShapeDtypeStruct((M, 1), F32)),
                          compiler_params=_params(("parallel",)), name=name)(*ins)


def mm_pw1_glu(name, xb, w4, layer, bias4, tm=512):
    M, K = xb.shape
    n = w4.shape[-1]
    half = N_CHIPS // 2
    tm = _tile(M, tm)

    def body(*refs):
        x_ref, w_refs, b_refs = refs[0], refs[1:1 + N_CHIPS], refs[1 + N_CHIPS:1 + 2 * N_CHIPS]
        h_ref, u_ref = refs[-2:]
        xv = x_ref[...].astype(BF16)
        for j in range(N_CHIPS):
            h_ref[j] = jnp.dot(xv, w_refs[j][...].astype(BF16), preferred_element_type=F32) + b_refs[j][...]
        for j in range(half):
            u_ref[:, j * n:(j + 1) * n] = h_ref[j] * _sigmoid(h_ref[j + half])

    specs = [pl.BlockSpec((tm, K), lambda i: (i, 0))]
    specs += [pl.BlockSpec((None, None, K, n), lambda i, j=j: (j, layer, 0, 0)) for j in range(N_CHIPS)]
    specs += [pl.BlockSpec((None, None, 1, n), lambda i, j=j: (j, layer, 0, 0)) for j in range(N_CHIPS)]
    return pl.pallas_call(body, grid=(M // tm,), in_specs=specs,
                          out_specs=(pl.BlockSpec((N_CHIPS, tm, n), lambda i: (0, i, 0)), pl.BlockSpec((tm, half * n), lambda i: (i, 0))),
                          out_shape=(jax.ShapeDtypeStruct((N_CHIPS, M, n), F32), jax.ShapeDtypeStruct((M, half * n), F32)),
                          compiler_params=_params(("parallel",)), name=name)(xb, *([w4] * N_CHIPS), *([bias4] * N_CHIPS))


def mm_nt_rowsm(name, dy, w4, layer, out_sm=False, res=None, res_alpha=1.0, out_dtype=F32, tm=1024):
    kc, N = w4.shape[-2:]
    M = dy.shape[0]
    tm = _tile(M, tm)

    def store_cols(o_ref, j, d):
        o_ref[:, j * kc:(j + 1) * kc] = d

    if out_sm:
        out_spec, sds, store = pl.BlockSpec((N_CHIPS, tm, kc), lambda i: (0, i, 0)), jax.ShapeDtypeStruct((N_CHIPS, M, kc), out_dtype), _store_slot
    else:
        out_spec, sds, store = pl.BlockSpec((tm, N_CHIPS * kc), lambda i: (i, 0)), jax.ShapeDtypeStruct((M, N_CHIPS * kc), F32), store_cols
    r = None if res is None else (res, pl.BlockSpec((tm, N_CHIPS * kc), lambda i: (i, 0)), lambda ref, j: ref[:, j * kc:(j + 1) * kc])
    return _mm_fanout(name, dy, pl.BlockSpec((tm, N), lambda i: (i, 0)), w4, (kc, N), layer, NT, out_spec, sds, store, (M // tm,),
                      res=r, res_alpha=res_alpha)


def mm_nt_colsm(name, dy_parts, w4, layer, res=None, res_alpha=1.0, tm=512):
    K, n = w4.shape[-2:]
    M = dy_parts[0][0].shape[1]
    tm = _tile(M, tm)
    terms = [(arr, pl.BlockSpec((None, tm, n), lambda i, idx=idx: (idx, i, 0)), w4,
              pl.BlockSpec((None, None, K, n), lambda i, j=j: (j, layer, 0, 0))) for j, (arr, idx) in enumerate(dy_parts)]
    r = None if res is None else (res, pl.BlockSpec((tm, K), lambda i: (i, 0)))
    return _mm_call(name, (M // tm,), terms, pl.BlockSpec((tm, K), lambda i: (i, 0)), jax.ShapeDtypeStruct((M, K), F32), NT,
                    res=r, res_alpha=res_alpha)


def _sm_parts(a):
    return [(a, j) for j in range(a.shape[0])]


def mm_tn_colsm(name, x, dy, into, off, j0=0, tk=512):
    M, K = x.shape
    nj, _, n = dy.shape
    tk = _tile(K, tk, LANES)
    assert off % tk == 0
    terms = [(x, pl.BlockSpec((M, tk), lambda j, k: (0, k)), dy, pl.BlockSpec((None, M, n), lambda j, k: (j, 0, 0)))]
    return _mm_call(name, (nj, K // tk), terms, pl.BlockSpec((None, tk, n), lambda j, k: (j + j0, off // tk + k, 0)),
                    None, TN, into=into)


def mm_tn_rowsm(name, a, dy, into, off, a_sm=False, tn=512):
    M, N = dy.shape
    kc = a.shape[-1] if a_sm else a.shape[-1] // N_CHIPS
    tn = _tile(N, tn, LANES)
    assert off % kc == 0
    a_spec = pl.BlockSpec((None, M, kc), lambda j, n: (j, 0, 0)) if a_sm else pl.BlockSpec((M, kc), lambda j, n: (0, j))
    terms = [(a, a_spec, dy, pl.BlockSpec((M, tn), lambda j, n: (0, n)))]
    return _mm_call(name, (N_CHIPS, N // tn), terms, pl.BlockSpec((None, kc, tn), lambda j, n: (j, off // kc, n)),
                    None, TN, into=into)


def mm_tn_rowsm_fan(name, a, dy, into, off, tn=256):
    M, N = dy.shape
    kc = a.shape[-1] // N_CHIPS
    tn = _tile(N, tn, LANES)
    assert off % kc == 0

    def body(a_ref, dy_ref, into_ref, o_ref):
        dyb = dy_ref[...].astype(BF16)
        for j in range(N_CHIPS):
            aj = a_ref[:, j * kc:(j + 1) * kc].astype(BF16)
            o_ref[j] = lax.dot_general(aj, dyb, TN, preferred_element_type=F32).astype(o_ref.dtype)

    return pl.pallas_call(body, out_shape=jax.ShapeDtypeStruct(into.shape, into.dtype), grid=(N // tn,),
                          in_specs=[pl.BlockSpec((M, N_CHIPS * kc), lambda n: (0, 0)), pl.BlockSpec((M, tn), lambda n: (0, n)),
                                    pl.BlockSpec(memory_space=pl.ANY)],
                          out_specs=pl.BlockSpec((N_CHIPS, kc, tn), lambda n: (0, off // kc, n)), input_output_aliases={2: 0},
                          compiler_params=_params(("parallel",)), name=name)(a, dy, into)


def mm_proj(name, p4, layer, w4, tm=512):
    S, P = p4.shape[-2:]
    n = w4.shape[-1]
    tm = _tile(S, tm)

    def store_cols(o_ref, j, d):
        o_ref[:, j * n:(j + 1) * n] = d

    return _mm_fanout(name, p4, pl.BlockSpec((None, None, tm, P), lambda i: (layer, 0, i, 0)), w4, (P, n), layer, NN,
                      pl.BlockSpec((tm, N_CHIPS * n), lambda i: (i, 0)), jax.ShapeDtypeStruct((S, N_CHIPS * n), F32),
                      store_cols, (S // tm,))


def mm_tn_proj(name, p4, layer, dpp, into, off):
    S, P = p4.shape[-2:]
    n = dpp.shape[-1] // N_CHIPS
    assert off % P == 0
    terms = [(p4, pl.BlockSpec((None, None, S, P), lambda j: (layer, 0, 0, 0)), dpp, pl.BlockSpec((S, n), lambda j: (0, j)))]
    return _mm_call(name, (N_CHIPS,), terms, pl.BlockSpec((None, P, n), lambda j: (j, off // P, 0)), None, TN, into=into)


def ln_bwd(name, dy, xh, rstd, g, layer, tg=None, pp=None, tm=512):
    S, D = dy.shape
    tm = _tile(S, tm)
    ple = tg is not None

    def body(*refs):
        if ple:
            dy_ref, xh_ref, rs_ref, g_ref, tg_ref, pp_ref, dr_ref, acc_ref, dtg_ref, dpp_ref = refs
        else:
            dy_ref, xh_ref, rs_ref, g_ref, dr_ref, acc_ref = refs
        dy_, xh_ = dy_ref[...], xh_ref[...]
        dxh = dy_ * g_ref[...]
        m1 = jnp.mean(dxh, axis=-1, keepdims=True)
        m2 = jnp.mean(dxh * xh_, axis=-1, keepdims=True)
        dr = rs_ref[...] * (dxh - m1 - xh_ * m2)
        dr_ref[...] = dr

        @pl.when(pl.program_id(0) == 0)
        def _():
            acc_ref[...] = jnp.zeros_like(acc_ref)
        acc_ref[0:1, :] += jnp.sum(dy_ * xh_, axis=0, keepdims=True)
        acc_ref[1:2, :] += jnp.sum(dy_, axis=0, keepdims=True)
        acc_ref[2:3, :] += jnp.sum(dr, axis=0, keepdims=True)
        if ple:
            pg = _sigmoid(tg_ref[...])
            dtg_ref[...] = (dr * pp_ref[...] * pg * (1.0 - pg)).astype(BF16)
            dpp_ref[...] = (dr * pg).astype(BF16)

    row = pl.BlockSpec((tm, D), lambda i: (i, 0))
    ins = [dy, xh, rstd, g] + ([tg, pp] if ple else [])
    specs = [row, row, pl.BlockSpec((tm, 1), lambda i: (i, 0)), pl.BlockSpec((None, 1, D), lambda i: (layer, 0, 0))] + ([row, row] if ple else [])
    outs = [jax.ShapeDtypeStruct((S, D), F32), jax.ShapeDtypeStruct((8, D), F32)]
    out_specs = [row, pl.BlockSpec((8, D), lambda i: (0, 0))]
    if ple:
        outs += [jax.ShapeDtypeStruct((S, D), BF16)] * 2
        out_specs += [row, row]
    return pl.pallas_call(body, grid=(S // tm,), in_specs=specs, out_specs=tuple(out_specs), out_shape=tuple(outs),
                          compiler_params=_params(("arbitrary",)), name=name)(*ins)


def conv_ln_silu_fwd(name, u, w, b, g, beta, layer, ts=128):
    S, D = u.shape
    kw = w.shape[1]
    ts = _tile(S, ts, CONV_HALO)
    lc = LANES if D % LANES == 0 else D

    def body(h_ref, u_ref, w_ref, b_ref, g_ref, be_ref, c_ref, s_ref, win_ref, rot_ref):
        i = pl.program_id(0)
        win_ref[0:CONV_HALO, :] = jnp.where(i == 0, 0.0, h_ref[...])
        win_ref[CONV_HALO:, :] = u_ref[...]
        for cc in range(D // lc):
            cs = slice(cc * lc, (cc + 1) * lc)
            for sub in range(1, SUBLANES):
                rot_ref[sub] = win_ref[sub:sub + ts + CONV_HALO - SUBLANES, cs]
            acc = jnp.zeros((ts, lc), F32) + b_ref[:, cs]
            for k in range(kw):
                whole, sub = divmod(CONV_HALO - (kw - 1) + k, SUBLANES)
                r0 = whole * SUBLANES
                acc = acc + w_ref[k:k + 1, cs] * (win_ref[r0:r0 + ts, cs] if sub == 0 else rot_ref[sub, r0:r0 + ts, :])
            c_ref[:, cs] = acc
        c = c_ref[...]
        mu = jnp.mean(c, axis=-1, keepdims=True)
        d = c - mu
        var = jnp.mean(d * d, axis=-1, keepdims=True)
        nrm = d * lax.rsqrt(var + LN_EPS) * g_ref[...] + be_ref[...]
        s_ref[...] = (nrm * _sigmoid(nrm)).astype(BF16)

    row = pl.BlockSpec((ts, D), lambda i: (i, 0))
    vec = pl.BlockSpec((None, 1, D), lambda i: (layer, 0, 0))
    halo = pl.BlockSpec((CONV_HALO, D), lambda i: (jnp.maximum(i * (ts // CONV_HALO) - 1, 0), 0))
    return pl.pallas_call(body, grid=(S // ts,),
                          in_specs=[halo, row, pl.BlockSpec((None, kw, D), lambda i: (layer, 0, 0)), vec, vec, vec],
                          out_specs=(row, row),
                          out_shape=(jax.ShapeDtypeStruct((S, D), F32), jax.ShapeDtypeStruct((S, D), BF16)),
                          scratch_shapes=[pltpu.VMEM((ts + CONV_HALO, D), F32),
                                          pltpu.VMEM((SUBLANES, ts + CONV_HALO - SUBLANES, lc), F32)],
                          compiler_params=_params(("parallel",)), name=name)(u, u, w, b, g, beta)


def ln_silu_bwd(name, ds, c, g, beta, layer, tm=256):
    S, D = c.shape
    tm = _tile(S, tm)

    def body(ds_ref, c_ref, g_ref, be_ref, dc_ref, acc_ref):
        c_ = c_ref[...]
        mu = jnp.mean(c_, axis=-1, keepdims=True)
        d = c_ - mu
        var = jnp.mean(d * d, axis=-1, keepdims=True)
        rstd = lax.rsqrt(var + LN_EPS)
        xh = d * rstd
        nrm = xh * g_ref[...] + be_ref[...]
        sg = _sigmoid(nrm)
        dn = ds_ref[...] * (sg * (1.0 + nrm * (1.0 - sg)))
        dxh = dn * g_ref[...]
        m1 = jnp.mean(dxh, axis=-1, keepdims=True)
        m2 = jnp.mean(dxh * xh, axis=-1, keepdims=True)
        dc = rstd * (dxh - m1 - xh * m2)
        dc_ref[...] = dc

        @pl.when(pl.program_id(0) == 0)
        def _():
            acc_ref[...] = jnp.zeros_like(acc_ref)
        acc_ref[0:1, :] += jnp.sum(dn * xh, axis=0, keepdims=True)
        acc_ref[1:2, :] += jnp.sum(dn, axis=0, keepdims=True)
        acc_ref[2:3, :] += jnp.sum(dc, axis=0, keepdims=True)

    row = pl.BlockSpec((tm, D), lambda i: (i, 0))
    vec = pl.BlockSpec((None, 1, D), lambda i: (layer, 0, 0))
    return pl.pallas_call(body, grid=(S // tm,), in_specs=[row, row, vec, vec],
                          out_specs=(row, pl.BlockSpec((8, D), lambda i: (0, 0))),
                          out_shape=(jax.ShapeDtypeStruct((S, D), F32), jax.ShapeDtypeStruct((8, D), F32)),
                          compiler_params=_params(("arbitrary",)), name=name)(ds, c, g, beta)


def conv_glu_bwd(name, dc, u, h_sm, w, layer, ts=256):
    S, D = dc.shape
    kw = w.shape[1]
    half = N_CHIPS // 2
    n = D // half
    ts = _tile(S, ts, CONV_HALO)
    nblk = S // ts
    lc = LANES if n % LANES == 0 else n

    def body(dc_ref, dcn_ref, u_ref, a_ref, g_ref, w_ref, da_ref, dg_ref, dw_ref, dba_ref, dbg_ref, dwin_ref, rot_ref):
        i = pl.program_id(1)
        dwin_ref[0:ts, :] = dc_ref[...]
        dwin_ref[ts:, :] = jnp.where(i == nblk - 1, 0.0, dcn_ref[...])

        @pl.when(i == 0)
        def _():
            dw_ref[...] = jnp.zeros_like(dw_ref)
            dba_ref[...] = jnp.zeros_like(dba_ref)
            dbg_ref[...] = jnp.zeros_like(dbg_ref)

        for cc in range(n // lc):
            cs = slice(cc * lc, (cc + 1) * lc)
            for sub in range(1, SUBLANES):
                rot_ref[sub] = dwin_ref[sub:sub + ts + CONV_HALO - SUBLANES, cs]
            ub = u_ref[:, cs]
            du = jnp.zeros((ts, lc), F32)
            for k in range(kw):
                whole, sub = divmod(kw - 1 - k, SUBLANES)
                r0 = whole * SUBLANES
                shifted = dwin_ref[r0:r0 + ts, cs] if sub == 0 else rot_ref[sub, r0:r0 + ts, :]
                du = du + w_ref[k:k + 1, cs] * shifted
                dw_ref[k:k + 1, cs] += jnp.sum(shifted * ub, axis=0, keepdims=True)
            a = a_ref[:, cs]
            sg = _sigmoid(g_ref[:, cs])
            da = du * sg
            dg = du * a * sg * (1.0 - sg)
            da_ref[:, cs] = da.astype(BF16)
            dg_ref[:, cs] = dg.astype(BF16)
            dba_ref[0:1, cs] += jnp.sum(da, axis=0, keepdims=True)
            dbg_ref[0:1, cs] += jnp.sum(dg, axis=0, keepdims=True)

    r = ts // CONV_HALO
    main = pl.BlockSpec((ts, n), lambda j, i: (i, j))
    nxt = pl.BlockSpec((CONV_HALO, n), lambda j, i: (jnp.minimum((i + 1) * r, S // CONV_HALO - 1), j))
    sm_a = pl.BlockSpec((None, ts, n), lambda j, i: (j, i, 0))
    sm_g = pl.BlockSpec((None, ts, n), lambda j, i: (j + half, i, 0))
    da, dg, dw, dba, dbg = pl.pallas_call(
        body, grid=(half, nblk),
        in_specs=[main, nxt, main, sm_a, sm_g, pl.BlockSpec((None, kw, n), lambda j, i: (layer, 0, j))],
        out_specs=(pl.BlockSpec((None, ts, n), lambda j, i: (j, i, 0)), pl.BlockSpec((None, ts, n), lambda j, i: (j, i, 0)),
                   pl.BlockSpec((None, 32, n), lambda j, i: (j, 0, 0)),
                   pl.BlockSpec((None, 8, n), lambda j, i: (j, 0, 0)), pl.BlockSpec((None, 8, n), lambda j, i: (j, 0, 0))),
        out_shape=(jax.ShapeDtypeStruct((half, S, n), BF16), jax.ShapeDtypeStruct((half, S, n), BF16),
                   jax.ShapeDtypeStruct((half, 32, n), F32),
                   jax.ShapeDtypeStruct((half, 8, n), F32), jax.ShapeDtypeStruct((half, 8, n), F32)),
        scratch_shapes=[pltpu.VMEM((ts + CONV_HALO, n), F32), pltpu.VMEM((SUBLANES, ts + CONV_HALO - SUBLANES, lc), F32)],
        compiler_params=_params(("parallel", "arbitrary")), name=name)(dc, dc, u, h_sm, h_sm, w)
    return da, dg, dw, dba, dbg


ROW_CHUNK = 16


def _ffn_gc(win_ref, w_ref, b_ref, r0, rows, kw, base):
    gc = b_ref[...] + jnp.zeros((rows, win_ref.shape[1]), F32)
    for k in range(kw):
        off = r0 + base - (kw - 1) + k
        gc = gc + w_ref[k:k + 1, :] * win_ref[off:off + rows, :]
    return gc


def ffn_up_gate_fwd(name, xb, w_up, w_gate, cw4, cb4, layer, tm=256):
    S, K = xb.shape
    n = w_up.shape[-1]
    kw = cw4.shape[2]
    tm = _tile(S, tm, ROW_CHUNK)
    rc = ROW_CHUNK

    def body(*refs):
        x_ref, xh_ref = refs[0:2]
        wu, wg, cw, cb = refs[2:6], refs[6:10], refs[10:14], refs[14:18]
        up_ref, gp_ref, gc_ref, hf_ref, win_ref = refs[18:]
        i = pl.program_id(0)
        xv = x_ref[...]
        xhalo = xh_ref[...]
        for j in range(N_CHIPS):
            wgj = wg[j][...]
            gp = jnp.dot(xv, wgj, preferred_element_type=F32)
            up = jnp.dot(xv, wu[j][...], preferred_element_type=F32)
            gph = jnp.dot(xhalo, wgj, preferred_element_type=F32)
            gpb = gp.astype(BF16)
            gp_ref[j] = gpb
            up_ref[j] = up.astype(BF16)
            win_ref[0:FFN_HALO, :] = jnp.where(i == 0, 0.0, gph.astype(BF16).astype(F32))
            win_ref[FFN_HALO:, :] = gpb.astype(F32)
            for r0 in range(0, tm, rc):
                gc = _ffn_gc(win_ref, cw[j], cb[j], r0, rc, kw, FFN_HALO)
                gc_ref[j, r0:r0 + rc, :] = gc.astype(BF16)
                hf_ref[j, r0:r0 + rc, :] = (gc * _sigmoid(gc) * up_ref[j, r0:r0 + rc, :].astype(F32)).astype(BF16)

    r = tm // FFN_HALO
    out = pl.BlockSpec((N_CHIPS, tm, n), lambda i: (0, i, 0))
    sds = jax.ShapeDtypeStruct((N_CHIPS, S, n), BF16)
    specs = [pl.BlockSpec((tm, K), lambda i: (i, 0)), pl.BlockSpec((FFN_HALO, K), lambda i: (jnp.maximum(i * r - 1, 0), 0))]
    specs += [pl.BlockSpec((None, None, K, n), lambda i, j=j: (j, layer, 0, 0)) for j in range(N_CHIPS)] * 2
    specs += [pl.BlockSpec((None, None, kw, n), lambda i, j=j: (j, layer, 0, 0)) for j in range(N_CHIPS)]
    specs += [pl.BlockSpec((None, None, 1, n), lambda i, j=j: (layer, j, 0, 0)) for j in range(N_CHIPS)]
    return pl.pallas_call(body, grid=(S // tm,), in_specs=specs, out_specs=(out, out, out, out), out_shape=(sds, sds, sds, sds),
                          scratch_shapes=[pltpu.VMEM((tm + FFN_HALO, n), F32)],
                          compiler_params=_params(("parallel",)), name=name)(
                              xb, xb, *([w_up] * N_CHIPS), *([w_gate] * N_CHIPS), *([cw4] * N_CHIPS), *([cb4] * N_CHIPS))


def ffn_gate_conv_bwd(name, dr, w_down, cw4, layer, up_sm, gc_sm, gp_sm, tm=512):
    n, N = w_down.shape[-2:]
    S = dr.shape[0]
    kw = cw4.shape[2]
    tm = _tile(S, tm, ROW_CHUNK)
    rc = ROW_CHUNK
    nblk = S // tm

    def dgc_of(dhf, up, gc):
        sg = _sigmoid(gc)
        return dhf * up * (sg * (1.0 + gc * (1.0 - sg))), sg

    def body(*refs):
        dr_ref, drn_ref = refs[0:2]
        wd, cw = refs[2:6], refs[6:10]
        up_ref, gc_ref, gp_ref, upn_ref, gcn_ref, dup_ref, dgp_ref, acc_ref, win_ref = refs[10:]
        i = pl.program_id(0)

        @pl.when(i == 0)
        def _():
            acc_ref[...] = jnp.zeros_like(acc_ref)
        a = dr_ref[...].astype(BF16)
        an = drn_ref[...].astype(BF16)
        for j in range(N_CHIPS):
            wj = wd[j][...]
            dhf = lax.dot_general(a, wj, NT, preferred_element_type=F32)
            gc = gc_ref[j].astype(F32)
            dgc, sg = dgc_of(dhf, up_ref[j].astype(F32), gc)
            dup_ref[j] = (dhf * gc * sg).astype(BF16)
            win_ref[0:tm, :] = dgc
            dhfn = lax.dot_general(an, wj, NT, preferred_element_type=F32)
            dgcn, _ = dgc_of(dhfn, upn_ref[j].astype(F32), gcn_ref[j].astype(F32))
            win_ref[tm:, :] = jnp.where(i == nblk - 1, 0.0, dgcn)
            sums = [jnp.zeros((8, n), F32) for _ in range(kw + 1)]
            for r0 in range(0, tm, rc):
                gp = gp_ref[j, r0:r0 + rc, :].astype(F32)
                dgp = jnp.zeros((rc, n), F32)
                for k in range(kw):
                    d = kw - 1 - k
                    shifted = win_ref[r0 + d:r0 + d + rc, :]
                    dgp = dgp + cw[j][k:k + 1, :] * shifted
                    prod = shifted * gp
                    sums[k] = sums[k] + prod[0:8, :] + prod[8:16, :]
                    if d == 0:
                        sums[kw] = sums[kw] + shifted[0:8, :] + shifted[8:16, :]
                dgp_ref[j, r0:r0 + rc, :] = dgp.astype(BF16)
            for k in range(kw):
                acc_ref[j, k:k + 1, :] += jnp.sum(sums[k], axis=0, keepdims=True)
            acc_ref[j, 7:8, :] += jnp.sum(sums[kw], axis=0, keepdims=True)

    r = tm // FFN_HALO
    nxt_row = lambda i: jnp.minimum((i + 1) * r, S // FFN_HALO - 1)
    blk = pl.BlockSpec((N_CHIPS, tm, n), lambda i: (0, i, 0))
    halo = pl.BlockSpec((N_CHIPS, FFN_HALO, n), lambda i: (0, nxt_row(i), 0))
    sds = jax.ShapeDtypeStruct((N_CHIPS, S, n), BF16)
    specs = [pl.BlockSpec((tm, N), lambda i: (i, 0)), pl.BlockSpec((FFN_HALO, N), lambda i: (nxt_row(i), 0))]
    specs += [pl.BlockSpec((None, None, n, N), lambda i, j=j: (j, layer, 0, 0)) for j in range(N_CHIPS)]
    specs += [pl.BlockSpec((None, None, kw, n), lambda i, j=j: (j, layer, 0, 0)) for j in range(N_CHIPS)]
    specs += [blk, blk, blk, halo, halo]
    return pl.pallas_call(body, grid=(nblk,), in_specs=specs,
                          out_specs=(blk, blk, pl.BlockSpec((N_CHIPS, 8, n), lambda i: (0, 0, 0))),
                          out_shape=(sds, sds, jax.ShapeDtypeStruct((N_CHIPS, 8, n), F32)),
                          scratch_shapes=[pltpu.VMEM((tm + FFN_HALO, n), F32)],
                          compiler_params=_params(("arbitrary",)), name=name)(
                              dr, dr, *([w_down] * N_CHIPS), *([cw4] * N_CHIPS), up_sm, gc_sm, gp_sm, up_sm, gc_sm)


def _neg_softplus(z):
    e = jnp.exp(-jnp.abs(z))
    return -(jnp.maximum(z, 0.0) + jnp.log(1.0 + e)), e


def _split_dot(x, t):
    hi = x.astype(BF16)
    lo = (x - hi.astype(F32)).astype(BF16)
    return jnp.dot(hi, t, preferred_element_type=F32) + jnp.dot(lo, t, preferred_element_type=F32)


STICK_GONE = -100.0
NOT_SWEPT = -1e30


def attn_fwd(name, q, k, v, bq=512, w=256):
    S, D = q.shape
    dh = HEAD_DIM
    hpb = LANES // dh
    bq = _tile(S, bq)
    w = _tile(bq, w)
    nsub = bq // w
    nkb = S // w

    def body(q_ref, k_ref, v_ref, o_ref, runs_ref, rs_ref):
        qi = pl.program_id(1)
        rr = lax.broadcasted_iota(jnp.int32, (w, w), 0)
        cc = lax.broadcasted_iota(jnp.int32, (w, w), 1)
        t_suf = (rr >= cc).astype(BF16)
        tq = qi * bq + lax.broadcasted_iota(jnp.int32, (bq, w), 0)
        tk = lax.broadcasted_iota(jnp.int32, (bq, w), 1)
        lane = lax.broadcasted_iota(jnp.int32, (bq, LANES), 1)
        ntot = (qi + 1) * nsub
        heads = [slice(hh * dh, (hh + 1) * dh) for hh in range(hpb)]
        qbs = [q_ref[:, hs] for hs in heads]
        for hh in range(hpb):
            rs_ref[hh] = jnp.where(lane < ntot, NOT_SWEPT, 0.0)

        def block(kb, carry, masked):
            kstart = pl.multiple_of(kb * w, w)
            if masked:
                m = (tk + kstart) < tq
            out = []
            for hh, hs in enumerate(heads):
                run, acc = carry[2 * hh], carry[2 * hh + 1]
                kblk = k_ref[pl.ds(kstart, w), hs]
                vblk = v_ref[pl.ds(kstart, w), hs]
                z = lax.dot_general(qbs[hh], kblk, NT, preferred_element_type=F32)
                lg, _ = _neg_softplus(z)
                if masked:
                    lg = jnp.where(m, lg, 0.0)
                cum = _split_dot(lg, t_suf) + run
                a = jnp.exp(z + cum)
                if masked:
                    a = jnp.where(m, a, 0.0)
                acc = acc + jnp.dot(a.astype(BF16), vblk, preferred_element_type=F32)
                run = cum[:, 0:1]
                rs_ref[hh] = jnp.where(lane == kb, run, rs_ref[hh])
                out += [run, acc]
            return tuple(out)

        carry = (jnp.zeros((bq, 1), F32), jnp.zeros((bq, dh), F32)) * hpb
        for sb in reversed(range(nsub)):
            carry = block(qi * nsub + sb, carry, True)

        def cond(c):
            alive = functools.reduce(jnp.maximum, [jnp.max(c[1 + 2 * hh]) for hh in range(hpb)])
            return jnp.logical_and(c[0] >= 0, alive > STICK_GONE)

        def step(c):
            return (c[0] - 1,) + block(c[0], c[1:], False)
        carry = lax.while_loop(cond, step, (qi * nsub - 1,) + carry)[1:]
        for hh, hs in enumerate(heads):
            o_ref[:, hs] = carry[2 * hh + 1].astype(o_ref.dtype)
            runs_ref[hh] = rs_ref[hh, :, 0:nkb]

    qs = pl.BlockSpec((bq, LANES), lambda h, i: (i, h))
    kv = pl.BlockSpec((S, LANES), lambda h, i: (0, h))
    return pl.pallas_call(body, grid=(D // LANES, S // bq), in_specs=[qs, kv, kv],
                          out_specs=(qs, pl.BlockSpec((hpb, bq, nkb), lambda h, i: (h, i, 0))),
                          out_shape=(jax.ShapeDtypeStruct((S, D), BF16), jax.ShapeDtypeStruct((D // dh, S, nkb), F32)),
                          scratch_shapes=[pltpu.VMEM((hpb, bq, LANES), F32)],
                          compiler_params=_params(("parallel", "parallel")), name=name)(q, k, v)


def attn_bwd(name, q, k, v, do, runs, dk0=None, dv0=None, bq=512, w=256):
    S, D = q.shape
    dh = HEAD_DIM
    hpb = LANES // dh
    bq = _tile(S, bq)
    w = _tile(bq, w)
    nsub = bq // w
    nkb = S // w
    scale = 1.0 / math.sqrt(dh)
    init = dk0 is not None

    def body(*refs):
        if init:
            q_ref, k_ref, v_ref, do_ref, runs_ref, dk0_ref, dv0_ref, dq_ref, dk_ref, dv_ref, rs_ref = refs
        else:
            q_ref, k_ref, v_ref, do_ref, runs_ref, dq_ref, dk_ref, dv_ref, rs_ref = refs
        qi = pl.program_id(1)

        @pl.when(qi == 0)
        def _():
            dk_ref[...] = dk0_ref[...] if init else jnp.zeros_like(dk_ref)
            dv_ref[...] = dv0_ref[...] if init else jnp.zeros_like(dv_ref)

        rr = lax.broadcasted_iota(jnp.int32, (w, w), 0)
        cc = lax.broadcasted_iota(jnp.int32, (w, w), 1)
        t_suf = (rr >= cc).astype(BF16)
        t_pre = (rr <= cc).astype(BF16)
        tq = qi * bq + lax.broadcasted_iota(jnp.int32, (bq, w), 0)
        tk = lax.broadcasted_iota(jnp.int32, (bq, w), 1)
        lane = lax.broadcasted_iota(jnp.int32, (bq, LANES), 1)
        lane1 = lax.broadcasted_iota(jnp.int32, (1, LANES), 1)
        ntot = (qi + 1) * nsub
        heads = [slice(hh * dh, (hh + 1) * dh) for hh in range(hpb)]
        qbs = [q_ref[:, hs] for hs in heads]
        dobs = [do_ref[:, hs].astype(BF16) for hs in heads]
        kb0 = ntot - nsub
        for hh in range(hpb):
            rs_ref[hh] = jnp.zeros((bq, LANES), F32)
            rs_ref[hh, :, 0:nkb] = runs_ref[hh]
            colmax = jnp.max(rs_ref[hh], axis=0, keepdims=True)
            dead = jnp.logical_and(jnp.logical_and(lane1 >= 1, lane1 <= ntot), colmax <= STICK_GONE)
            kb0 = jnp.minimum(kb0, jnp.sum(dead.astype(jnp.int32)))

        def block(kb, carry, masked):
            kstart = pl.multiple_of(kb * w, w)
            if masked:
                m = (tk + kstart) < tq
            out = []
            for hh, hs in enumerate(heads):
                pg_run, dq = carry[2 * hh], carry[2 * hh + 1]
                qb, dob = qbs[hh], dobs[hh]
                kblk = k_ref[pl.ds(kstart, w), hs]
                vblk = v_ref[pl.ds(kstart, w), hs]
                right = jnp.sum(jnp.where(lane == kb + 1, rs_ref[hh], 0.0), axis=1, keepdims=True)
                z = lax.dot_general(qb, kblk, NT, preferred_element_type=F32)
                lg, _ = _neg_softplus(z)
                sig = 1.0 - jnp.exp(lg)
                if masked:
                    lg = jnp.where(m, lg, 0.0)
                a = jnp.exp(z + _split_dot(lg, t_suf) + right)
                if masked:
                    a = jnp.where(m, a, 0.0)
                da = lax.dot_general(dob, vblk, NT, preferred_element_type=F32)
                g = da * a
                pin = _split_dot(g, t_pre) + pg_run
                dz = g - sig * pin
                if masked:
                    dz = jnp.where(m, dz, 0.0)
                dzb = dz.astype(BF16)
                dq = dq + jnp.dot(dzb, kblk, preferred_element_type=F32)
                dk_ref[pl.ds(kstart, w), hs] += lax.dot_general(dzb, qb, TN, preferred_element_type=F32)
                dv_ref[pl.ds(kstart, w), hs] += lax.dot_general(a.astype(BF16), dob, TN, preferred_element_type=F32)
                out += [pin[:, w - 1:w], dq]
            return tuple(out)

        carry = (jnp.zeros((bq, 1), F32), jnp.zeros((bq, dh), F32)) * hpb
        carry = lax.fori_loop(kb0, qi * nsub, lambda kb, c: block(kb, c, False), carry)
        for sb in range(nsub):
            carry = block(qi * nsub + sb, carry, True)
        for hh, hs in enumerate(heads):
            dq_ref[:, hs] = carry[2 * hh + 1] * scale

    qs = pl.BlockSpec((bq, LANES), lambda h, i: (i, h))
    kv = pl.BlockSpec((S, LANES), lambda h, i: (0, h))
    ins = [q, k, v, do, runs] + ([dk0, dv0] if init else [])
    specs = [qs, kv, kv, qs, pl.BlockSpec((hpb, bq, nkb), lambda h, i: (h, i, 0))] + ([kv, kv] if init else [])
    sds = jax.ShapeDtypeStruct((S, D), F32)
    return pl.pallas_call(body, grid=(D // LANES, S // bq), in_specs=specs, out_specs=(qs, kv, kv), out_shape=(sds, sds, sds),
                          scratch_shapes=[pltpu.VMEM((hpb, bq, LANES), F32)],
                          compiler_params=_params(("parallel", "arbitrary")), name=name)(*ins)


def loss_head(name, y, tgt, tm=512):
    S, D = y.shape
    tm = _tile(S, tm)

    def body(y_ref, t_ref, dy_ref, acc_ref):
        @pl.when(pl.program_id(0) == 0)
        def _():
            acc_ref[...] = jnp.zeros_like(acc_ref)
        e = y_ref[...] - t_ref[...]
        dy_ref[...] = e * (1.0 / D)
        acc_ref[...] += jnp.sum(e * e)

    row = pl.BlockSpec((tm, D), lambda i: (i, 0))
    return pl.pallas_call(body, grid=(S // tm,), in_specs=[row, row],
                          out_specs=(row, pl.BlockSpec((8, LANES), lambda i: (0, 0))),
                          out_shape=(jax.ShapeDtypeStruct((S, D), F32), jax.ShapeDtypeStruct((8, LANES), F32)),
                          compiler_params=_params(("arbitrary",)), name=name)(y, tgt)


def _to_sm(a, axis=-1):
    axis = axis % a.ndim
    shp = a.shape[:axis] + (N_CHIPS, a.shape[axis] // N_CHIPS) + a.shape[axis + 1:]
    return jnp.moveaxis(a.reshape(shp), axis, 0)


def _from_sm(a, axis=-1):
    nd = a.ndim - 1
    axis = axis % nd
    b = jnp.moveaxis(a, 0, axis)
    return b.reshape(b.shape[:axis] + (b.shape[axis] * b.shape[axis + 1],) + b.shape[axis + 2:])


class GradBuffers:
    def __init__(self, W):
        groups = {}
        for n in BIG:
            _, layers, rows, cols = W[n].shape
            groups.setdefault(cols, []).append((rows, n, layers))
        self.where, self.cols_of, self.buf, self.members = {}, {}, {}, {}
        for cols, items in groups.items():
            off, members = 0, []
            for rows, n, layers in sorted(items, key=lambda t: -t[0]):
                assert off % rows == 0
                self.where[n], self.cols_of[n] = (off, rows), cols
                members.append((n, off, rows * layers))
                off += rows * layers
            assert off % 32 == 0
            self.buf[cols] = lax.empty((N_CHIPS, off, cols), BF16)
            self.members[cols] = members

    def put(self, n, layer, fn, **kw):
        cols = self.cols_of[n]
        off, rows = self.where[n]
        self.buf[cols] = fn(into=self.buf[cols], off=off + layer * rows, **kw)


def forward_backward(x, p4, tgt, W):
    S, D = x.shape
    scale = 1.0 / math.sqrt(HEAD_DIM)
    saved = []
    kh = vh = xb_kv = None
    xb = x.astype(BF16)
    for i in range(DEPTH):
        sv = {'xb': xb}
        if i < N_A:
            h_sm, u = mm_pw1_glu(f"pw1glu_{i}", xb, W['a_pw1_w'], i, W['a_pw1_b'])
            c, s = conv_ln_silu_fwd(f"convln_{i}", u, W['a_dw_w'], W['a_dw_b'], W['a_ln_g'], W['a_ln_b'], i)
            x1, x1b, xh1, rs1 = mm_rowsm_ln(f"pw2ln_{i}", s, W['a_pw2_w'], i, x, W['ln_mix_g'], W['ln_mix_b'], i, bias=W['a_pw2_b'])
            sv.update(h_sm=h_sm, u=u, c=c, s=s)
        else:
            j = i - N_A
            if kh is None:
                xb_kv = xb
                kh = mm_rowsm("wk", xb, W['kv_wk'], 0, out_dtype=BF16)
                vh = mm_rowsm("wv", xb, W['kv_wv'], 0, out_dtype=BF16)
            qh = mm_rowsm(f"wq_{j}", xb, W['b_wq'], j, out_dtype=BF16, out_scale=scale)
            o, runs = attn_fwd(f"attn_{j}", qh, kh, vh)
            x1, x1b, xh1, rs1 = mm_rowsm_ln(f"woln_{j}", o, W['b_wo'], j, x, W['ln_mix_g'], W['ln_mix_b'], i)
            sv.update(qh=qh, o=o, runs=runs)
        up_sm, gp_sm, gc_sm, hf_sm = ffn_up_gate_fwd(f"upgate_{i}", x1b, W['ffn_w_up'], W['ffn_w_gate'], W['ffn_conv_w'],
                                                     W['ffn_conv_b'], i)
        tg = mm_rowsm(f"plegate_{i}", x1b, W['ple_w_gate'], i)
        pp = mm_proj(f"pleproj_{i}", p4, i, W['ple_w_proj'])
        x2, x2b, xh2, rs2 = mm_rowsm_ln(f"downln_{i}", hf_sm, W['ffn_w_down'], i, x1, W['ln_ffn_g'], W['ln_ffn_b'], i, a_sm=True,
                                        tg=tg, pp=pp)
        sv.update(x1b=x1b, xh1=xh1, rs1=rs1, up_sm=up_sm, gp_sm=gp_sm, gc_sm=gc_sm, hf_sm=hf_sm, tg=tg, pp=pp, xh2=xh2, rs2=rs2)
        saved.append(sv)
        x, xb = x2, x2b

    dx, lacc = loss_head("loss", x, tgt)
    loss_sum = lacc[0, 0]

    G = {n: [None] * DEPTH for n in WEIGHTS if n not in BIG}
    gb = GradBuffers(W)
    dk = dv = None
    for i in reversed(range(DEPTH)):
        sv = saved[i]
        dr, acc, dtg, dpp = ln_bwd(f"lnffn_b_{i}", dx, sv['xh2'], sv['rs2'], W['ln_ffn_g'], i, tg=sv['tg'], pp=sv['pp'])
        G['ln_ffn_g'][i], G['ln_ffn_b'][i] = acc[0], acc[1]
        gb.put('ple_w_proj', i, functools.partial(mm_tn_proj, f"dproj_{i}", p4, i, dpp))
        gb.put('ple_w_gate', i, functools.partial(mm_tn_rowsm_fan, f"dplegate_{i}", sv['x1b'], dtg))
        gb.put('ffn_w_down', i, functools.partial(mm_tn_rowsm, f"ddown_{i}", sv['hf_sm'], dr, a_sm=True))
        dup_sm, dgp_sm, cacc = ffn_gate_conv_bwd(f"gateconv_b_{i}", dr, W['ffn_w_down'], W['ffn_conv_w'], i, sv['up_sm'],
                                                 sv['gc_sm'], sv['gp_sm'])
        kw = W['ffn_conv_w'].shape[2]
        G['ffn_conv_w'][i] = cacc[:, 0:kw, :]
        G['ffn_conv_b'][i] = cacc[:, 7, :].reshape(-1)
        gb.put('ffn_w_up', i, functools.partial(mm_tn_colsm, f"dup_{i}", sv['x1b'], dup_sm))
        gb.put('ffn_w_gate', i, functools.partial(mm_tn_colsm, f"dgate_{i}", sv['x1b'], dgp_sm))
        dx1 = mm_nt_rowsm(f"dx1a_{i}", dtg, W['ple_w_gate'], i, res=dr, res_alpha=DN_ALPHA)
        dx1 = mm_nt_colsm(f"dx1b_{i}", _sm_parts(dup_sm), W['ffn_w_up'], i, res=dx1)
        dx1 = mm_nt_colsm(f"dx1c_{i}", _sm_parts(dgp_sm), W['ffn_w_gate'], i, res=dx1)

        dr1, acc1 = ln_bwd(f"lnmix_b_{i}", dx1, sv['xh1'], sv['rs1'], W['ln_mix_g'], i)
        G['ln_mix_g'][i], G['ln_mix_b'][i] = acc1[0], acc1[1]
        xin = sv['xb']
        if i < N_A:
            G['a_pw2_b'][i] = acc1[2]
            gb.put('a_pw2_w', i, functools.partial(mm_tn_rowsm_fan, f"dpw2_{i}", sv['s'], dr1))
            ds = mm_nt_rowsm(f"ds_{i}", dr1, W['a_pw2_w'], i)
            dc, cacc = ln_silu_bwd(f"lnsilu_b_{i}", ds, sv['c'], W['a_ln_g'], W['a_ln_b'], i)
            G['a_ln_g'][i], G['a_ln_b'][i], G['a_dw_b'][i] = cacc[0], cacc[1], cacc[2]
            da, dg, dw, dba, dbg = conv_glu_bwd(f"convglu_b_{i}", dc, sv['u'], sv['h_sm'], W['a_dw_w'], i)
            kw = W['a_dw_w'].shape[1]
            G['a_dw_w'][i] = _from_sm(dw[:, 0:kw, :], axis=-1)
            G['a_pw1_b'][i] = jnp.concatenate([dba[:, 0, :], dbg[:, 0, :]], axis=0)
            half = da.shape[0]
            gb.put('a_pw1_w', i, functools.partial(mm_tn_colsm, f"dpw1a_{i}", xin, da))
            gb.put('a_pw1_w', i, functools.partial(mm_tn_colsm, f"dpw1g_{i}", xin, dg), j0=half)
            dx = mm_nt_colsm(f"dxa_{i}", _sm_parts(da) + _sm_parts(dg), W['a_pw1_w'], i, res=dr1, res_alpha=DN_ALPHA)
        else:
            j = i - N_A
            gb.put('b_wo', j, functools.partial(mm_tn_rowsm_fan, f"dwo_{j}", sv['o'], dr1))
            do = mm_nt_rowsm(f"do_{j}", dr1, W['b_wo'], j)
            dq, dk, dv = attn_bwd(f"attn_b_{j}", sv['qh'], kh, vh, do, sv['runs'], dk, dv)
            gb.put('b_wq', j, functools.partial(mm_tn_rowsm_fan, f"dwq_{j}", xin, dq))
            dx = mm_nt_rowsm(f"dxq_{j}", dq, W['b_wq'], j, res=dr1, res_alpha=DN_ALPHA)
            if j == 0:
                gb.put('kv_wk', 0, functools.partial(mm_tn_rowsm_fan, "dwk", xb_kv, dk))
                gb.put('kv_wv', 0, functools.partial(mm_tn_rowsm_fan, "dwv", xb_kv, dv))
                dx = mm_nt_rowsm("dxk", dk, W['kv_wk'], 0, res=dx)
                dx = mm_nt_rowsm("dxv", dv, W['kv_wv'], 0, res=dx)
    return loss_sum, dx, G, gb


MESH = pl.DeviceIdType.MESH
HBM = pl.BlockSpec(memory_space=pltpu.HBM)


def _place():
    x, y, c = lax.axis_index("x"), lax.axis_index("y"), lax.axis_index("c")
    others = [(1 - x, y), (x, 1 - y), (1 - x, 1 - y)]
    return x, y, c, others


def allgather_chips(name, arrs):
    n = len(arrs)

    def body(*refs):
        ins, outs = refs[:n], refs[n:2 * n]
        send_sems, recv_sems = refs[2 * n:]
        x, y, c, others = _place()
        me = 2 * x + y
        sibling = (x, y, 1 - c)
        ids = [2 * ch[0] + ch[1] for ch in others]
        from_id = jnp.where(c == 0, ids[0], ids[1])
        to_chip = (jnp.where(c == 0, x, 1 - x), jnp.where(c == 0, 1 - y, y))

        def remote(a, k, src, chip_id, half, to):
            return pltpu.make_async_remote_copy(src_ref=src, dst_ref=outs[a].at[chip_id, half], send_sem=send_sems.at[a, k],
                                                recv_sem=recv_sems.at[a, k], device_id=to, device_id_type=MESH)

        sent = [remote(a, k, ins[a].at[c], me, c, (others[k][0], others[k][1], c)) for a in range(n) for k in range(2)]
        for cp in sent:
            cp.start()
        for a in range(n):
            for k in range(2):
                remote(a, k, ins[a].at[c], ids[k], c, sibling).wait_recv()
            sent.append(remote(a, 2, outs[a].at[from_id, c], from_id, c, (to_chip[0], to_chip[1], c)))
            sent[-1].start()
            for k in range(2):
                sent.append(remote(a, 3 + k, outs[a].at[ids[k], c], ids[k], c, sibling))
                sent[-1].start()
        for a in range(n):
            remote(a, 2, ins[a].at[c], ids[2], c, sibling).wait_recv()
            sent.append(remote(a, 5, outs[a].at[ids[2], c], ids[2], c, sibling))
            sent[-1].start()
        for a in range(n):
            for k in range(3):
                remote(a, 3 + k, ins[a].at[c], ids[k], 1 - c, sibling).wait_recv()
        for cp in sent:
            cp.wait_send()

    outs = pl.pallas_call(body, out_shape=tuple(jax.ShapeDtypeStruct((N_CHIPS,) + a.shape, a.dtype) for a in arrs),
                          in_specs=[HBM] * n, out_specs=tuple([HBM] * n),
                          scratch_shapes=[pltpu.SemaphoreType.DMA((n, 6)), pltpu.SemaphoreType.DMA((n, 6))],
                          name=name)(*arrs)
    me = 2 * lax.axis_index("x") + lax.axis_index("y")
    return [lax.dynamic_update_index_in_dim(o, a, me, 0) for o, a in zip(outs, arrs)]


def exchange_sibling(name, gs):
    n = len(gs)

    def body(*refs):
        g_refs, o_refs = refs[:n], refs[n:2 * n]
        send_sems, recv_sems = refs[2 * n:]
        x, y, c, _ = _place()
        cps = [pltpu.make_async_remote_copy(src_ref=g_refs[a].at[j, 1 - c], dst_ref=o_refs[a].at[j], send_sem=send_sems.at[a, j],
                                            recv_sem=recv_sems.at[a, j], device_id=(x, y, 1 - c), device_id_type=MESH)
               for a in range(n) for j in range(N_CHIPS)]
        for cp in cps:
            cp.start()
        for cp in cps:
            cp.wait()

    return pl.pallas_call(body, out_shape=tuple(jax.ShapeDtypeStruct((N_CHIPS,) + g.shape[2:], g.dtype) for g in gs),
                          in_specs=[HBM] * n, out_specs=tuple([HBM] * n),
                          scratch_shapes=[pltpu.SemaphoreType.DMA((n, N_CHIPS)), pltpu.SemaphoreType.DMA((n, N_CHIPS))],
                          name=name)(*gs)


def _ring_peers():
    x, y, c, _ = _place()
    first = (jnp.where(c == 0, 1 - x, x), jnp.where(c == 0, y, 1 - y))
    second = (jnp.where(c == 0, x, 1 - x), jnp.where(c == 0, 1 - y, y))
    return c, first, second, 2 * (1 - x) + (1 - y)


def exchange_first(name, ss):
    n = len(ss)

    def body(*refs):
        s_refs, o_refs = refs[:n], refs[n:2 * n]
        send_sems, recv_sems = refs[2 * n:]
        c, first, _, diag = _ring_peers()
        cps = [pltpu.make_async_remote_copy(src_ref=s_refs[a].at[slot], dst_ref=o_refs[a].at[k], send_sem=send_sems.at[a, k],
                                            recv_sem=recv_sems.at[a, k], device_id=(first[0], first[1], c), device_id_type=MESH)
               for a in range(n) for k, slot in enumerate((2 * first[0] + first[1], diag))]
        for cp in cps:
            cp.start()
        for cp in cps:
            cp.wait()

    return pl.pallas_call(body, out_shape=tuple(jax.ShapeDtypeStruct((2,) + s.shape[1:], s.dtype) for s in ss),
                          in_specs=[HBM] * n, out_specs=tuple([HBM] * n),
                          scratch_shapes=[pltpu.SemaphoreType.DMA((n, 2)), pltpu.SemaphoreType.DMA((n, 2))], name=name)(*ss)


def exchange_second(name, ts):
    n = len(ts)

    def body(*refs):
        t_refs, o_refs = refs[:n], refs[n:2 * n]
        send_sems, recv_sems = refs[2 * n:]
        c, _, second, _ = _ring_peers()
        cps = [pltpu.make_async_remote_copy(src_ref=t_refs[a], dst_ref=o_refs[a], send_sem=send_sems.at[a], recv_sem=recv_sems.at[a],
                                            device_id=(second[0], second[1], c), device_id_type=MESH) for a in range(n)]
        for cp in cps:
            cp.start()
        for cp in cps:
            cp.wait()

    return pl.pallas_call(body, out_shape=tuple(jax.ShapeDtypeStruct(t.shape, t.dtype) for t in ts),
                          in_specs=[HBM] * n, out_specs=tuple([HBM] * n),
                          scratch_shapes=[pltpu.SemaphoreType.DMA((n,)), pltpu.SemaphoreType.DMA((n,))], name=name)(*ts)


def share_sibling(name, ts):
    n = len(ts)

    def body(*refs):
        o_refs = refs[n:2 * n]
        send_sems, recv_sems = refs[2 * n:]
        x, y, c, _ = _place()
        cps = [pltpu.make_async_remote_copy(src_ref=o_refs[a].at[c], dst_ref=o_refs[a].at[c], send_sem=send_sems.at[a],
                                            recv_sem=recv_sems.at[a], device_id=(x, y, 1 - c), device_id_type=MESH)
               for a in range(n)]
        for cp in cps:
            cp.start()
        for a in range(n):
            pltpu.make_async_remote_copy(src_ref=o_refs[a].at[c], dst_ref=o_refs[a].at[1 - c], send_sem=send_sems.at[a],
                                         recv_sem=recv_sems.at[a], device_id=(x, y, 1 - c), device_id_type=MESH).wait_recv()
        for cp in cps:
            cp.wait_send()

    return pl.pallas_call(body, out_shape=tuple(jax.ShapeDtypeStruct(t.shape, t.dtype) for t in ts),
                          in_specs=[HBM] * n, out_specs=tuple([HBM] * n), input_output_aliases={a: a for a in range(n)},
                          scratch_shapes=[pltpu.SemaphoreType.DMA((n,)), pltpu.SemaphoreType.DMA((n,))],
                          name=name)(*ts)


def add_halves(name, g, recv, place, out_dtype, tr=512):
    _, _, R, C = g.shape
    tr = _tile(R, tr, 16)

    def body(p_ref, a_ref, b_ref, o_ref):
        o_ref[...] = (a_ref[...].astype(F32) + b_ref[...].astype(F32)).astype(o_ref.dtype)

    blk = pl.BlockSpec((None, tr, C), lambda j, i, p: (j, i, 0))
    gs = pltpu.PrefetchScalarGridSpec(num_scalar_prefetch=1, grid=(N_CHIPS, R // tr),
                                      in_specs=[pl.BlockSpec((None, None, tr, C), lambda j, i, p: (j, p[0], i, 0)), blk],
                                      out_specs=blk)
    return pl.pallas_call(body, grid_spec=gs, out_shape=jax.ShapeDtypeStruct((N_CHIPS, R, C), out_dtype),
                          compiler_params=_params(("parallel", "parallel")), name=name)(place, g, recv)


def add_pass_on(name, s, got, place, tr=512):
    _, R, C = s.shape
    tr = _tile(R, tr, 16)

    def body(p_ref, a_ref, b_ref, o_ref):
        o_ref[...] = (a_ref[...].astype(F32) + b_ref[...].astype(F32)).astype(o_ref.dtype)

    gs = pltpu.PrefetchScalarGridSpec(num_scalar_prefetch=1, grid=(R // tr,),
                                      in_specs=[pl.BlockSpec((None, tr, C), lambda i, p: (p[2], i, 0)),
                                                pl.BlockSpec((None, tr, C), lambda i, p: (1, i, 0))],
                                      out_specs=pl.BlockSpec((tr, C), lambda i, p: (i, 0)))
    return pl.pallas_call(body, grid_spec=gs, out_shape=jax.ShapeDtypeStruct((R, C), s.dtype),
                          compiler_params=_params(("parallel",)), name=name)(place, s, got)


def add_chips(name, g, r1, got1, got2, place, tr=512):
    _, _, R, C = g.shape
    tr = _tile(R, tr, 16)

    def body(p_ref, a_ref, b_ref, c_ref, d_ref, o_ref):
        o_ref[...] = ((a_ref[...].astype(F32) + b_ref[...].astype(F32)) + c_ref[...].astype(F32)) + d_ref[...].astype(F32)

    gs = pltpu.PrefetchScalarGridSpec(num_scalar_prefetch=1, grid=(R // tr,),
                                      in_specs=[pl.BlockSpec((None, None, tr, C), lambda i, p: (p[1], p[0], i, 0)),
                                                pl.BlockSpec((None, tr, C), lambda i, p: (p[1], i, 0)),
                                                pl.BlockSpec((None, tr, C), lambda i, p: (0, i, 0)),
                                                pl.BlockSpec((tr, C), lambda i, p: (i, 0))],
                                      out_specs=pl.BlockSpec((None, tr, C), lambda i, p: (p[0], i, 0)))
    return pl.pallas_call(body, grid_spec=gs, out_shape=jax.ShapeDtypeStruct((2, R, C), F32),
                          compiler_params=_params(("parallel",)), name=name)(place, g, r1, got1, got2)


def reduce_scatter(gs, wire_dtypes, place):
    r1 = exchange_sibling("rs_sibling", gs)
    s1 = [add_halves(f"rs_add_cores_{a}", g, r, place, dt) for a, (g, r, dt) in enumerate(zip(gs, r1, wire_dtypes))]
    got1 = exchange_first("rs_first", s1)
    t = [add_pass_on(f"rs_add_pass_{a}", s, g1, place) for a, (s, g1) in enumerate(zip(s1, got1))]
    got2 = exchange_second("rs_second", t)
    tot = [add_chips(f"rs_add_chips_{a}", g, r, g1, g2, place) for a, (g, r, g1, g2) in enumerate(zip(gs, r1, got1, got2))]
    return share_sibling("rs_share", tot)


def adamw(name, w, g, m, v, tr=512):
    shp = w.shape
    cols = shp[-1]
    w2, g2, m2, v2 = (a.reshape(-1, cols) for a in (w, g, m, v))
    rows = w2.shape[0]
    tr = _tile(rows, tr)

    def body(w_ref, g_ref, m_ref, v_ref, d_ref, mo_ref, vo_ref):
        g_ = g_ref[...]
        m_ = ADAM_B1 * m_ref[...] + (1.0 - ADAM_B1) * g_
        v_ = ADAM_B2 * v_ref[...] + (1.0 - ADAM_B2) * (g_ * g_)
        m_hat = m_ / (1.0 - ADAM_B1 ** ADAM_STEP)
        v_hat = v_ / (1.0 - ADAM_B2 ** ADAM_STEP)
        d_ref[...] = -ADAM_LR * (m_hat / (jnp.sqrt(v_hat) + ADAM_EPS) + ADAM_WD * w_ref[...])
        mo_ref[...] = m_
        vo_ref[...] = v_

    blk = pl.BlockSpec((tr, cols), lambda i: (i, 0))
    sds = jax.ShapeDtypeStruct((rows, cols), F32)
    d, mo, vo = pl.pallas_call(body, grid=(rows // tr,), in_specs=[blk] * 4, out_specs=(blk, blk, blk), out_shape=(sds, sds, sds),
                               compiler_params=_params(("parallel",)), name=name)(w2, g2, m2, v2)
    return d.reshape(shp), mo.reshape(shp), vo.reshape(shp)


PACK_ALIGN = 1024


def _pad_to(a, mult, axis=-1):
    axis = axis % a.ndim
    extra = (-a.shape[axis]) % mult
    if extra == 0:
        return a
    pads = [(0, 0)] * a.ndim
    pads[axis] = (0, extra)
    return jnp.pad(a, pads)


def _pack(pieces, lead, row_mult):
    nl = len(lead)
    flat, offs, sizes, off = [], [], [], 0
    for a in pieces:
        f = a.reshape(lead + (-1,))
        sizes.append(f.shape[-1])
        f = _pad_to(f, PACK_ALIGN)
        offs.append(off)
        off += f.shape[-1]
        flat.append(f)
    cat = _pad_to(jnp.concatenate(flat, axis=nl), 2 * row_mult * LANES)
    return cat.reshape(lead + (2, -1, LANES)), offs, sizes


def _unpack(packed, lead, offs, sizes, shapes):
    flat = packed.reshape(lead + (-1,))
    return [lax.slice_in_dim(flat, o, o + s, axis=len(lead)).reshape(lead + tuple(shp)) for o, s, shp in zip(offs, sizes, shapes)]


def _stack_grads(G, names):
    out = {}
    for n in names:
        parts = [g for g in G[n] if g is not None]
        if n in ('kv_wk', 'kv_wv'):
            out[n] = parts[0]
        elif n in REPLICATED:
            out[n] = jnp.stack(parts, axis=0).reshape(N_CHIPS, -1)
        elif n in ('a_dw_w', 'a_dw_b', 'a_ln_g', 'a_ln_b', 'a_pw2_b'):
            out[n] = _to_sm(jnp.stack(parts, axis=0), axis=-1)
        else:
            out[n] = jnp.stack(parts, axis=1)
    return out


def _whole_weights(big, small, rep, D):
    W = {}
    for n in BIG:
        a = big[n]
        W[n] = a[:, None] if n in ('kv_wk', 'kv_wv') else a
    W['a_pw1_b'] = small['a_pw1_b'][:, :, None, :]
    W['a_dw_w'] = _from_sm(small['a_dw_w'], axis=-1)
    for n in ('a_dw_b', 'a_ln_g', 'a_ln_b', 'a_pw2_b'):
        W[n] = _from_sm(small[n], axis=-1)[:, None, :]
    W['ffn_conv_w'] = small['ffn_conv_w']
    L, F = rep['ffn_conv_b'].shape
    W['ffn_conv_b'] = rep['ffn_conv_b'].reshape(L, N_CHIPS, 1, F // N_CHIPS)
    for n in ('ln_mix_g', 'ln_mix_b', 'ln_ffn_g', 'ln_ffn_b'):
        W[n] = rep[n][:, None, :]
    return W


SMALL = ('a_pw1_b', 'a_dw_w', 'a_dw_b', 'a_ln_g', 'a_ln_b', 'a_pw2_b', 'ffn_conv_w')


def _step(x, p, loss_target, w, m, v):
    S, D = x.shape[-2:]
    x2, tgt = x.reshape(S, D), loss_target.reshape(S, D)
    ax, ay, ac = lax.axis_index("x"), lax.axis_index("y"), lax.axis_index("c")
    second = jnp.where(ac == 0, 2 * ax + (1 - ay), 2 * (1 - ax) + ay)
    place = jnp.stack([ac, 2 * ax + ay, second]).astype(jnp.int32)

    big_in = [w[n].astype(BF16).reshape((2, -1) + w[n].shape[1:] if w[n].ndim == 3 else (2, -1, w[n].shape[-1])) for n in BIG]
    small_in, s_offs, s_sizes = _pack([w[n] for n in SMALL], (), 8)
    gathered = allgather_chips("gather_weights", big_in + [small_in])
    big = {n: g.reshape((N_CHIPS,) + w[n].shape) for n, g in zip(BIG, gathered[:-1])}
    small = dict(zip(SMALL, _unpack(gathered[-1], (N_CHIPS,), s_offs, s_sizes, [w[n].shape for n in SMALL])))
    W = _whole_weights(big, small, {n: w[n] for n in REPLICATED}, D)

    loss_sum, dx, G, gb = forward_backward(x2, p, tgt, W)
    loss = lax.psum(0.5 * loss_sum / D, ("x", "y", "c"))

    vectors = [n for n in WEIGHTS if n not in BIG]
    mats = [b.reshape(N_CHIPS, 2, b.shape[1] // 2, b.shape[2]) for b in gb.buf.values()]
    members = [gb.members[cols] for cols in gb.buf]
    g_sm = _stack_grads(G, vectors)
    packed, offs, sizes = _pack([g_sm[n] for n in vectors], (N_CHIPS,), 512)
    reduced = reduce_scatter(mats + [packed], [BF16] * len(mats) + [F32], place)
    shapes = [w[n].shape if n not in REPLICATED else (w[n].size // N_CHIPS,) for n in vectors]
    g_mine = dict(zip(vectors, _unpack(reduced[-1], (), offs, sizes, shapes)))
    for red, where in zip(reduced[:-1], members):
        rows = red.reshape(-1, red.shape[-1])
        for n, off, cnt in where:
            g_mine[n] = lax.slice_in_dim(rows, off, off + cnt, axis=0).reshape(w[n].shape)
    rep_in, r_offs, r_sizes = _pack([g_mine[n] for n in REPLICATED], (), 8)
    rep_all = allgather_chips("gather_replicated_grads", [rep_in])[0]
    for n, g in zip(REPLICATED, _unpack(rep_all, (N_CHIPS,), r_offs, r_sizes, [(w[n].size // N_CHIPS,) for n in REPLICATED])):
        g_mine[n] = g.reshape(w[n].shape)

    grads, deltas, new_m, new_v = [], [], [], []
    for n in WEIGHTS:
        d, mo, vo = adamw(f"adamw_{n}", w[n], g_mine[n], m[n], v[n])
        grads.append(g_mine[n])
        deltas.append(d)
        new_m.append(mo)
        new_v.append(vo)
    return (loss, dx.reshape(x.shape), *grads, *deltas, *new_m, *new_v)


def kernel(x, p, a_pw1_w, a_pw1_b, a_dw_w, a_dw_b, a_ln_g, a_ln_b, a_pw2_w, a_pw2_b, b_wq, kv_wk, kv_wv, b_wo, ln_mix_g, ln_mix_b, ffn_w_up, ffn_w_gate, ffn_conv_w, ffn_conv_b, ffn_w_down, ple_w_gate, ple_w_proj, ln_ffn_g, ln_ffn_b, loss_target, m_a_pw1_w, m_a_pw1_b, m_a_dw_w, m_a_dw_b, m_a_ln_g, m_a_ln_b, m_a_pw2_w, m_a_pw2_b, m_b_wq, m_kv_wk, m_kv_wv, m_b_wo, m_ln_mix_g, m_ln_mix_b, m_ffn_w_up, m_ffn_w_gate, m_ffn_conv_w, m_ffn_conv_b, m_ffn_w_down, m_ple_w_gate, m_ple_w_proj, m_ln_ffn_g, m_ln_ffn_b, v_a_pw1_w, v_a_pw1_b, v_a_dw_w, v_a_dw_b, v_a_ln_g, v_a_ln_b, v_a_pw2_w, v_a_pw2_b, v_b_wq, v_kv_wk, v_kv_wv, v_b_wo, v_ln_mix_g, v_ln_mix_b, v_ffn_w_up, v_ffn_w_gate, v_ffn_conv_w, v_ffn_conv_b, v_ffn_w_down, v_ple_w_gate, v_ple_w_proj, v_ln_ffn_g, v_ln_ffn_b):
    vals = dict(locals())
    w = {n: vals[n] for n in WEIGHTS}
    m = {n: vals["m_" + n] for n in WEIGHTS}
    v = {n: vals["v_" + n] for n in WEIGHTS}
    return _step(x, p, loss_target, w, m, v)
```
